```python
import jax, jax.numpy as jnp
from jax import lax
import numpy as np

D_MODEL = 1024
BATCH = 16
SEQ = 256
DEPTH = 1
DEC_BATCH = 8
DEC_SEQ = 4096
PAST_LEN = 256

GRID_W = 64
ATT_HEADS = 8
ATT_KV_HEADS = 2
ATT_GROUP = ATT_HEADS // ATT_KV_HEADS
HEAD_DIM = 64
WINDOW = 128
BLOCK = 128
ROPE_THETA = 10000.0
AXIS_FREQS = HEAD_DIM // 4
ML_HEADS = 4
ML_DK = 128
ML_DV = 256
ML_CHUNK = 128
N_EXPERTS = 32
TOP_K = 4
D_FF = 1024
SWIGLU_ALPHA = 1.702
SWIGLU_LIMIT = 7.0
MOE_BLOCK = 128
EPS = 1e-6
NEG = -1e30

Q_W = ATT_HEADS * HEAD_DIM
KV_W = ATT_KV_HEADS * HEAD_DIM
MLQK_W = ML_HEADS * ML_DK
MLV_W = ML_HEADS * ML_DV
GATE_W = 4 * ML_HEADS
IN_SPLITS = (Q_W, KV_W, KV_W, MLQK_W, MLQK_W, MLV_W, GATE_W, MLV_W, D_MODEL, D_MODEL)
IN_W = Q_W + 2 * KV_W + 2 * MLQK_W + 2 * MLV_W + GATE_W + 2 * D_MODEL

kernel_name = "hybrid_diffusion_swa_mlstm_moe_step"


def rmsnorm(x, g):
    xf = x.astype(jnp.float32)
    y = xf * lax.rsqrt(jnp.mean(xf * xf, axis=-1, keepdims=True) + EPS)
    return (y * g.astype(jnp.float32)).astype(x.dtype)


def modulate(x, shift, scale):
    return x * (1 + scale) + shift


def adaln(cond, w_ada, b_ada):
    m = jax.nn.silu(cond) @ w_ada + b_ada
    return [t[:, None, :] for t in jnp.split(m, 6, axis=-1)]


def split_projection(u):
    out, start = [], 0
    for w in IN_SPLITS:
        out.append(u[..., start:start + w])
        start += w
    return out


def axial_rope(n_tok):
    rows = n_tok // GRID_W
    row = jnp.repeat(jnp.arange(rows), GRID_W).astype(jnp.float32)
    col = jnp.tile(jnp.arange(GRID_W), rows).astype(jnp.float32)
    inv = ROPE_THETA ** (-jnp.arange(AXIS_FREQS, dtype=jnp.float32) / AXIS_FREQS)
    ang = jnp.concatenate([row[:, None] * inv, col[:, None] * inv], axis=-1)
    return jnp.cos(ang)[:, None, :], jnp.sin(ang)[:, None, :]


def apply_rope(x, cos, sin):
    xf = x.astype(jnp.float32)
    a, b = xf[..., :HEAD_DIM // 2], xf[..., HEAD_DIM // 2:]
    return jnp.concatenate([a * cos - b * sin, b * cos + a * sin], axis=-1).astype(x.dtype)


def sink_attention(q, k, v, sink, mask):
    s = jnp.einsum('bqkgd,bskd->bkgqs', q.astype(jnp.float32), k.astype(jnp.float32)) * HEAD_DIM ** -0.5
    if mask is not None:
        s = jnp.where(mask, s, NEG)
    snk = sink.astype(jnp.float32)[None, :, :, None, None]
    m = jnp.maximum(s.max(axis=-1, keepdims=True), snk)
    p = jnp.exp(s - m)
    den = p.sum(axis=-1, keepdims=True) + jnp.exp(snk - m)
    return jnp.einsum('bkgqs,bskd->bqkgd', p / den, v.astype(jnp.float32))


def context_attention(q, k, v, sink):
    B, S = q.shape[:2]
    nb = S // BLOCK
    qb = jnp.moveaxis(q.reshape(B, nb, BLOCK, ATT_KV_HEADS, ATT_GROUP, HEAD_DIM), 1, 0)
    o = lax.map(lambda qj: sink_attention(qj, k, v, sink, None), qb)
    return jnp.moveaxis(o, 0, 1).reshape(B, S, Q_W)


def latent_attention(q, k, v, ck, cv, sink):
    B, S = q.shape[:2]
    nb = S // BLOCK
    Lc = ck.shape[1]
    pad = ((0, 0), (BLOCK, BLOCK), (0, 0), (0, 0))
    kp, vp = jnp.pad(k, pad), jnp.pad(v, pad)
    qb = jnp.moveaxis(q.reshape(B, nb, BLOCK, ATT_KV_HEADS, ATT_GROUP, HEAD_DIM), 1, 0)
    ctx_mask = jnp.ones((BLOCK, Lc), dtype=bool)

    def one_block(args):
        j, qj = args
        kj = lax.dynamic_slice_in_dim(kp, j * BLOCK, 3 * BLOCK, axis=1)
        vj = lax.dynamic_slice_in_dim(vp, j * BLOCK, 3 * BLOCK, axis=1)
        qpos = j * BLOCK + jnp.arange(BLOCK)
        kpos = j * BLOCK - BLOCK + jnp.arange(3 * BLOCK)
        band = (jnp.abs(qpos[:, None] - kpos[None, :]) <= WINDOW) & (kpos >= 0)[None, :] & (kpos < S)[None, :]
        mask = jnp.concatenate([band, ctx_mask], axis=1)
        keys = jnp.concatenate([kj, ck.astype(kj.dtype)], axis=1)
        vals = jnp.concatenate([vj, cv.astype(vj.dtype)], axis=1)
        return sink_attention(qj, keys, vals, sink, mask)

    o = lax.map(one_block, (jnp.arange(nb), qb))
    return jnp.moveaxis(o, 0, 1).reshape(B, S, Q_W)


def mlstm_scan(q, k, v, ig, lf, C0, n0, m0):
    B, S = q.shape[:2]
    nc = S // ML_CHUNK
    causal = jnp.tril(jnp.ones((ML_CHUNK, ML_CHUNK), dtype=bool))

    def chunks(t):
        return jnp.moveaxis(t.reshape((B, nc, ML_CHUNK) + t.shape[2:]), 1, 0)

    def step(carry, inp):
        C, n, m = carry
        qc, kc, vc, ic, fc = inp
        b = jnp.swapaxes(jnp.cumsum(fc, axis=1), 1, 2)
        ic = jnp.swapaxes(ic, 1, 2)
        dmat = jnp.where(causal, b[..., :, None] - b[..., None, :] + ic[..., None, :], NEG)
        inter = b + m[..., None]
        m_t = jnp.maximum(inter, dmat.max(axis=-1))
        w = jnp.exp(dmat - m_t[..., None])
        w_inter = jnp.exp(inter - m_t)
        s = jnp.einsum('blhd,bshd->bhls', qc, kc) * w
        cq = jnp.einsum('bhvd,blhd->blhv', C, qc) * jnp.swapaxes(w_inter, 1, 2)[..., None]
        num = jnp.einsum('bhls,bshv->blhv', s, vc) + cq
        den = s.sum(axis=-1) + w_inter * jnp.einsum('bhd,blhd->bhl', n, qc)
        h = num / jnp.swapaxes(jnp.maximum(jnp.abs(den), jnp.exp(-m_t)), 1, 2)[..., None]
        bL = b[..., -1]
        a = bL[..., None] - b + ic
        m_new = jnp.maximum(bL + m, a.max(axis=-1))
        wk = jnp.exp(a - m_new[..., None])
        decay = jnp.exp(bL + m - m_new)
        C_new = decay[..., None, None] * C + jnp.einsum('bhs,bshv,bshd->bhvd', wk, vc, kc)
        n_new = decay[..., None] * n + jnp.einsum('bhs,bshd->bhd', wk, kc)
        return (C_new, n_new, m_new), h

    (C, n, m), h = lax.scan(step, (C0, n0, m0), (chunks(q), chunks(k), chunks(v), chunks(ig), chunks(lf)))
    return jnp.moveaxis(h, 0, 1).reshape(B, S, ML_HEADS, ML_DV), C, n, m


def mlstm_bidir(q, k, v, gates, C0, n0, m0):
    ig_f, fg_f, ig_b, fg_b = jnp.split(gates, 4, axis=-1)
    hf, Cf, nf, mf = mlstm_scan(q, k, v, ig_f, jax.nn.log_sigmoid(fg_f), C0[:, 0], n0[:, 0], m0[:, 0])
    rev = lambda t: jnp.flip(t, axis=1)
    hb, Cb, nb_, mb = mlstm_scan(rev(q), rev(k), rev(v), rev(ig_b), rev(jax.nn.log_sigmoid(fg_b)),
                                 C0[:, 1], n0[:, 1], m0[:, 1])
    h = hf + rev(hb)
    return h, jnp.stack([Cf, Cb], axis=1), jnp.stack([nf, nb_], axis=1), jnp.stack([mf, mb], axis=1)


def mlstm_branch(mq, mk, mv, gts, mo, C0, n0, m0, b_gates, norm_ml):
    B, S = mq.shape[:2]
    f32 = jnp.float32
    qh = mq.reshape(B, S, ML_HEADS, ML_DK).astype(f32) * ML_DK ** -0.5
    kh = mk.reshape(B, S, ML_HEADS, ML_DK).astype(f32)
    vh = mv.reshape(B, S, ML_HEADS, ML_DV).astype(f32)
    gates = gts.astype(f32) + b_gates.astype(f32)
    h, C, n, m = mlstm_bidir(qh, kh, vh, gates, C0.astype(f32), n0.astype(f32), m0.astype(f32))
    hn = h * lax.rsqrt(jnp.mean(h * h, axis=-1, keepdims=True) + EPS)
    hn = hn.reshape(B, S, MLV_W) * norm_ml.astype(f32)
    out = (hn * jax.nn.sigmoid(mo.astype(f32))).astype(mo.dtype)
    return out, C, n, m


def merge_branches(att, ml, ga, gm, p):
    a = att @ p['w_up_att']
    m = ml @ p['w_up_ml']
    return (jax.nn.sigmoid(ga) * a + jax.nn.sigmoid(gm) * m) @ p['w_out']


def expert_ffn(xb, w1e, b1e, w2e, b2e):
    h = xb @ w1e + b1e
    glu, lin = h[..., :D_FF], h[..., D_FF:]
    glu = jnp.minimum(glu, SWIGLU_LIMIT)
    lin = jnp.clip(lin, -SWIGLU_LIMIT, SWIGLU_LIMIT)
    return ((lin + 1) * glu * jax.nn.sigmoid(SWIGLU_ALPHA * glu)) @ w2e + b2e


def moe(x, w_router, b_router, w1, b1, w2, b2):
    T, D = x.shape
    logits = x.astype(jnp.float32) @ w_router.astype(jnp.float32) + b_router.astype(jnp.float32)
    top_val, top_idx = lax.top_k(logits, TOP_K)
    gate = jax.nn.softmax(top_val, axis=-1)
    n_assign = T * TOP_K
    flat_e = top_idx.reshape(-1)
    flat_g = gate.reshape(-1)
    flat_tok = jnp.arange(n_assign) // TOP_K
    order = jnp.argsort(flat_e)
    e_sorted = flat_e[order]
    counts = jnp.zeros((N_EXPERTS,), jnp.int32).at[flat_e].add(1)
    padded = (counts + MOE_BLOCK - 1) // MOE_BLOCK * MOE_BLOCK
    pad_end = jnp.cumsum(padded)
    pad_start = pad_end - padded
    start = jnp.cumsum(counts) - counts
    dest = pad_start[e_sorted] + jnp.arange(n_assign) - start[e_sorted]
    n_blocks = -(-n_assign // MOE_BLOCK) + N_EXPERTS
    n_rows = n_blocks * MOE_BLOCK
    row_tok = jnp.zeros((n_rows,), jnp.int32).at[dest].set(flat_tok[order])
    row_gate = jnp.zeros((n_rows,), jnp.float32).at[dest].set(flat_g[order])
    block_e = jnp.minimum(jnp.searchsorted(pad_end, jnp.arange(n_blocks) * MOE_BLOCK, side='right'), N_EXPERTS - 1)

    def expert_block(args):
        toks, e = args
        return expert_ffn(x[toks], w1[e], b1[e], w2[e], b2[e])

    y_rows = lax.map(expert_block, (row_tok.reshape(n_blocks, MOE_BLOCK), block_e)).reshape(n_rows, D)
    out = jnp.zeros((T, D), jnp.float32).at[row_tok].add(y_rows.astype(jnp.float32) * row_gate[:, None])
    return out.astype(x.dtype)


def ffn_sublayer(x, shift, scale, gate, p):
    B, S, D = x.shape
    h = modulate(rmsnorm(x, p['g_pre_ffn']), shift, scale)
    y = moe(h.reshape(B * S, D), p['w_router'], p['b_router'], p['w1'], p['b1'], p['w2'], p['b2']).reshape(B, S, D)
    return x + gate * rmsnorm(y, p['g_post_ffn'])


def context_layer(x, c_ctx, p):
    B, S = x.shape[:2]
    sh1, sc1, gt1, sh2, sc2, gt2 = adaln(c_ctx[None, :], p['w_ada'], p['b_ada'])
    xm = modulate(rmsnorm(x, p['g_pre_mix']), sh1, sc1)
    q, k, v, mq, mk, mv, gts, mo, ga, gm = split_projection(xm @ p['w_in'])
    q = q.reshape(B, S, ATT_HEADS, HEAD_DIM)
    k = k.reshape(B, S, ATT_KV_HEADS, HEAD_DIM)
    v = v.reshape(B, S, ATT_KV_HEADS, HEAD_DIM)
    att = context_attention(q, k, v, p['sink']).astype(xm.dtype)
    C0 = jnp.zeros((B, 2, ML_HEADS, ML_DV, ML_DK), jnp.float32)
    n0 = jnp.zeros((B, 2, ML_HEADS, ML_DK), jnp.float32)
    m0 = jnp.zeros((B, 2, ML_HEADS), jnp.float32)
    ml, C, n, m = mlstm_branch(mq, mk, mv, gts, mo, C0, n0, m0, p['b_gates'], p['norm_ml'])
    mix = merge_branches(att, ml, ga, gm, p)
    x = x + gt1 * rmsnorm(mix, p['g_post_mix'])
    x = ffn_sublayer(x, sh2, sc2, gt2, p)
    return x, k, v, C, n, m


def latent_layer(x, c, ck, cv, C0, n0, m0, p):
    B, S = x.shape[:2]
    sh1, sc1, gt1, sh2, sc2, gt2 = adaln(c, p['w_ada'], p['b_ada'])
    xm = modulate(rmsnorm(x, p['g_pre_mix']), sh1, sc1)
    q, k, v, mq, mk, mv, gts, mo, ga, gm = split_projection(xm @ p['w_in'])
    cos, sin = axial_rope(S)
    q = apply_rope(q.reshape(B, S, ATT_HEADS, HEAD_DIM), cos, sin)
    k = apply_rope(k.reshape(B, S, ATT_KV_HEADS, HEAD_DIM), cos, sin)
    v = v.reshape(B, S, ATT_KV_HEADS, HEAD_DIM)
    att = latent_attention(q, k, v, ck, cv, p['sink']).astype(xm.dtype)
    ml, _, _, _ = mlstm_branch(mq, mk, mv, gts, mo, C0, n0, m0, p['b_gates'], p['norm_ml'])
    mix = merge_branches(att, ml, ga, gm, p)
    x = x + gt1 * rmsnorm(mix, p['g_post_mix'])
    return ffn_sublayer(x, sh2, sc2, gt2, p)


def setup_inputs(seed: int = 0) -> dict:
    key = jax.random.key(seed)
    ks = jax.random.split(key, 32)
    f32 = jnp.float32

    def nrm(k, shape, scale=1.0):
        return scale * jax.random.normal(k, shape, f32)

    def gain(k, shape):
        return 1.0 + nrm(k, shape, 0.05)

    u_f = jax.random.uniform(ks[29], (DEPTH, 2, ML_HEADS), f32)
    i_b = nrm(ks[30], (DEPTH, 2, ML_HEADS), 0.1)
    b_gates = jnp.concatenate([i_b[:, 0], 3.0 + 3.0 * u_f[:, 0], i_b[:, 1], 3.0 + 3.0 * u_f[:, 1]], axis=-1)
    return {
        'x_prompt': nrm(ks[0], (BATCH, SEQ, D_MODEL)),
        'x_sample': nrm(ks[1], (DEC_BATCH, DEC_SEQ, D_MODEL)),
        'c': nrm(ks[2], (DEC_BATCH, D_MODEL)),
        'cache_k': nrm(ks[3], (DEC_BATCH, DEPTH, PAST_LEN, ATT_KV_HEADS, HEAD_DIM)),
        'cache_v': nrm(ks[4], (DEC_BATCH, DEPTH, PAST_LEN, ATT_KV_HEADS, HEAD_DIM)),
        'state_C': nrm(ks[5], (DEC_BATCH, DEPTH, 2, ML_HEADS, ML_DV, ML_DK), 0.05),
        'state_n': nrm(ks[6], (DEC_BATCH, DEPTH, 2, ML_HEADS, ML_DK), 0.5),
        'state_m': nrm(ks[7], (DEC_BATCH, DEPTH, 2, ML_HEADS)),
        'c_ctx': nrm(ks[8], (D_MODEL,)),
        'w_ada': nrm(ks[9], (DEPTH, D_MODEL, 6 * D_MODEL), D_MODEL ** -0.5),
        'b_ada': nrm(ks[10], (DEPTH, 6 * D_MODEL), 0.01),
        'g_pre_mix': gain(ks[11], (DEPTH, D_MODEL)),
        'w_in': nrm(ks[12], (DEPTH, D_MODEL, IN_W), D_MODEL ** -0.5),
        'b_gates': b_gates,
        'attn_sink': nrm(ks[13], (DEPTH, ATT_HEADS), 0.5),
        'norm_mlstm': gain(ks[14], (DEPTH, MLV_W)),
        'w_up_att': nrm(ks[15], (DEPTH, Q_W, D_MODEL), Q_W ** -0.5),
        'w_up_ml': nrm(ks[16], (DEPTH, MLV_W, D_MODEL), MLV_W ** -0.5),
        'w_out': nrm(ks[17], (DEPTH, D_MODEL, D_MODEL), D_MODEL ** -0.5),
        'g_post_mix': gain(ks[18], (DEPTH, D_MODEL)),
        'g_pre_ffn': gain(ks[19], (DEPTH, D_MODEL)),
        'w_router': nrm(ks[20], (DEPTH, D_MODEL, N_EXPERTS), D_MODEL ** -0.5),
        'b_router': nrm(ks[21], (DEPTH, N_EXPERTS), 0.01),
        'w1': nrm(ks[22], (DEPTH, N_EXPERTS, D_MODEL, 2 * D_FF), D_MODEL ** -0.5),
        'b1': nrm(ks[23], (DEPTH, N_EXPERTS, 2 * D_FF), 0.01),
        'w2': nrm(ks[24], (DEPTH, N_EXPERTS, D_FF, D_MODEL), D_FF ** -0.5),
        'b2': nrm(ks[25], (DEPTH, N_EXPERTS, D_MODEL), 0.01),
        'g_post_ffn': gain(ks[26], (DEPTH, D_MODEL)),
    }


def reference(x_prompt, x_sample, c, cache_k, cache_v, state_C, state_n, state_m, c_ctx,
              w_ada, b_ada, g_pre_mix, w_in, b_gates, attn_sink, norm_mlstm, w_up_att, w_up_ml,
              w_out, g_post_mix, g_pre_ffn, w_router, b_router, w1, b1, w2, b2, g_post_ffn):
    y_prompt, y_sample = x_prompt, x_sample
    ks_, vs_, Cs_, ns_, ms_ = [], [], [], [], []
    for l in range(DEPTH):
        p = dict(w_ada=w_ada[l], b_ada=b_ada[l], g_pre_mix=g_pre_mix[l], w_in=w_in[l],
                 b_gates=b_gates[l], sink=attn_sink[l].reshape(ATT_KV_HEADS, ATT_GROUP),
                 norm_ml=norm_mlstm[l], w_up_att=w_up_att[l], w_up_ml=w_up_ml[l], w_out=w_out[l],
                 g_post_mix=g_post_mix[l], g_pre_ffn=g_pre_ffn[l], w_router=w_router[l],
                 b_router=b_router[l], w1=w1[l], b1=b1[l], w2=w2[l], b2=b2[l], g_post_ffn=g_post_ffn[l])
        y_prompt, k_l, v_l, C_l, n_l, m_l = context_layer(y_prompt, c_ctx, p)
        ks_.append(k_l)
        vs_.append(v_l)
        Cs_.append(C_l)
        ns_.append(n_l)
        ms_.append(m_l)
        y_sample = latent_layer(y_sample, c, cache_k[:, l], cache_v[:, l], state_C[:, l], state_n[:, l],
                                state_m[:, l], p)
    new_cache_k = jnp.stack(ks_, axis=1)
    new_cache_v = jnp.stack(vs_, axis=1)
    new_state_C = jnp.stack(Cs_, axis=1)
    new_state_n = jnp.stack(ns_, axis=1)
    new_state_m = jnp.stack(ms_, axis=1)
    return (y_prompt, y_sample, new_cache_k, new_cache_v, new_state_C, new_state_n, new_state_m)
```

```python
import functools

import numpy as np
import jax
import jax.numpy as jnp
from jax import lax
from jax.experimental import pallas as pl
from jax.experimental.pallas import tpu as pltpu

F32 = jnp.float32
BF16 = jnp.bfloat16

D_MODEL = 1024
GRID_W = 64
ATT_HEADS = 8
ATT_KV_HEADS = 2
ATT_GROUP = ATT_HEADS // ATT_KV_HEADS
HEAD_DIM = 64
BLOCK = 128
ROPE_THETA = 10000.0
AXIS_FREQS = HEAD_DIM // 4
ML_HEADS = 4
ML_DK = 128
ML_DV = 256
ML_CHUNK = 128
N_EXPERTS = 32
TOP_K = 4
D_FF = 1024
SWIGLU_ALPHA = 1.702
SWIGLU_LIMIT = 7.0
EPS = 1e-6
NEG = -1e30

Q_W = ATT_HEADS * HEAD_DIM
KV_W = ATT_KV_HEADS * HEAD_DIM
MLQK_W = ML_HEADS * ML_DK
MLV_W = ML_HEADS * ML_DV
GATE_W = 4 * ML_HEADS

LANES = 128
VMEM_LIMIT = 56 * 1024 * 1024
MOE_TILE = 2048
MOE_ROWS = 128
MOE_PAD = 128


def _cparams(sem):
    return pltpu.CompilerParams(dimension_semantics=sem, vmem_limit_bytes=VMEM_LIMIT)


def _const_spec(shape):
    nd = len(shape)
    return pl.BlockSpec(shape, lambda *_: (0,) * nd)


def _adaln_kernel(cond_ref, w_ref, b_ref, o_ref):
    cnd = cond_ref[...]
    act = cnd * jax.nn.sigmoid(cnd)
    o_ref[...] = jnp.dot(act.astype(BF16), w_ref[...].astype(BF16),
                         preferred_element_type=F32) + b_ref[...]


def _adaln(cond, w_ada, b_ada):
    rows, d = cond.shape
    n = w_ada.shape[1]
    tn = 1536
    return pl.pallas_call(
        _adaln_kernel,
        grid=(n // tn,),
        in_specs=[_const_spec((rows, d)),
                  pl.BlockSpec((d, tn), lambda j: (0, j)),
                  pl.BlockSpec((1, tn), lambda j: (0, j))],
        out_specs=pl.BlockSpec((rows, tn), lambda j: (0, j)),
        out_shape=jax.ShapeDtypeStruct((rows, n), F32),
        compiler_params=_cparams(("parallel",)),
        name="adaln",
    )(cond, w_ada, b_ada.reshape(1, n))


def _pack_w_in(w_in, b_gates, rope, with_nat):
    o_q, o_k, o_v = 0, Q_W, Q_W + KV_W
    o_mq = Q_W + 2 * KV_W
    o_mk = o_mq + MLQK_W
    o_mv = o_mk + MLQK_W
    o_g = o_mv + MLV_W
    o_mo = o_g + GATE_W
    o_ga = o_mo + MLV_W
    o_gm = o_ga + D_MODEL
    half = HEAD_DIM // 2
    zeros64 = jnp.zeros((D_MODEL, HEAD_DIM), w_in.dtype)

    def head_cols(base, h, rot):
        lo = base + h * HEAD_DIM
        if rot:
            return [w_in[:, lo + half:lo + HEAD_DIM], w_in[:, lo:lo + half]]
        return [w_in[:, lo:lo + HEAD_DIM]]

    def q_cols(rot):
        cols = []
        for h in range(ATT_HEADS):
            hc = head_cols(o_q, h, rot)
            cols += (hc + [zeros64]) if h % 2 == 0 else ([zeros64] + hc)
        return cols

    def dup_cols(base, rot):
        cols = []
        for h in range(ATT_KV_HEADS):
            hc = head_cols(base, h, rot)
            cols += hc + hc
        return cols

    segs, cols, off = {}, [], 0

    def add(name, cl):
        nonlocal off
        width = sum(c.shape[1] for c in cl)
        segs[name] = (off, off + width)
        cols.extend(cl)
        off += width

    add("q", q_cols(False))
    if rope:
        add("qrot", q_cols(True))
    add("kd", dup_cols(o_k, False))
    if rope:
        add("kdrot", dup_cols(o_k, True))
    add("vd", dup_cols(o_v, False))
    if with_nat:
        add("nat", [w_in[:, o_k:o_k + 2 * KV_W]])
    add("mq", [w_in[:, o_mq:o_mq + MLQK_W]])
    add("mk", [w_in[:, o_mk:o_mk + MLQK_W]])
    add("mv", [w_in[:, o_mv:o_mv + MLV_W]])
    add("mo", [w_in[:, o_mo:o_mo + MLV_W]])
    add("ga", [w_in[:, o_ga:o_ga + D_MODEL]])
    add("gm", [w_in[:, o_gm:o_gm + D_MODEL]])
    add("gates", [w_in[:, o_g:o_g + GATE_W], jnp.zeros((D_MODEL, LANES - GATE_W), w_in.dtype)])
    wp = jnp.concatenate(cols, axis=1).astype(BF16)
    bg = jnp.pad(b_gates.astype(F32), (0, LANES - GATE_W)).reshape(1, LANES)
    return wp, bg, segs


def _rope_tables(n_tok):
    rows = n_tok // GRID_W
    row = np.repeat(np.arange(rows), GRID_W).astype(np.float32)
    col = np.tile(np.arange(GRID_W), rows).astype(np.float32)
    inv = (np.float32(ROPE_THETA) ** (-np.arange(AXIS_FREQS, dtype=np.float32) / AXIS_FREQS)).astype(np.float32)
    ang = np.concatenate([row[:, None] * inv, col[:, None] * inv], axis=-1).astype(np.float32)
    cos, sin = np.cos(ang), np.sin(ang)
    c64 = np.concatenate([cos, cos], axis=-1)
    s64 = np.concatenate([-sin, sin], axis=-1)
    return (jnp.asarray(np.tile(c64, (1, 2)), F32), jnp.asarray(np.tile(s64, (1, 2)), F32))


def _proj_kernel(*refs, segs, rope, with_nat):
    it = iter(refs)
    x_ref, mod_ref, g_ref, w_ref, bg_ref = (next(it) for _ in range(5))
    cos_ref = sin_ref = None
    if rope:
        cos_ref, sin_ref = next(it), next(it)
    q_ref, kd_ref, vd_ref = next(it), next(it), next(it)
    nat_ref = next(it) if with_nat else None
    mq_ref, mk_ref, mv_ref, gates_ref, mo_ref, ga_ref, gm_ref = (next(it) for _ in range(7))

    x = x_ref[0]
    shift = mod_ref[0, 0:1, :]
    scale = mod_ref[0, 1:2, :]
    ms = jnp.mean(x * x, axis=-1, keepdims=True)
    xn = x * lax.rsqrt(ms + EPS) * g_ref[...]
    xb = (xn * (1.0 + scale) + shift).astype(BF16)

    def seg(name):
        lo, hi = segs[name]
        return jnp.dot(xb, w_ref[:, lo:hi], preferred_element_type=F32)

    uq = seg("q")
    ukd = seg("kd")
    if rope:
        uqr = seg("qrot")
        ukr = seg("kdrot")
        cs = cos_ref[...]
        sn = sin_ref[...]
        for h in range(ATT_HEADS):
            sl = slice(h * LANES, (h + 1) * LANES)
            q_ref[0, h] = ((uq[:, sl] * cs + uqr[:, sl] * sn) * HEAD_DIM ** -0.5).astype(BF16)
        for h in range(ATT_KV_HEADS):
            sl = slice(h * LANES, (h + 1) * LANES)
            kd_ref[0, :, sl] = (ukd[:, sl] * cs + ukr[:, sl] * sn).astype(BF16)
    else:
        for h in range(ATT_HEADS):
            sl = slice(h * LANES, (h + 1) * LANES)
            q_ref[0, h] = (uq[:, sl] * HEAD_DIM ** -0.5).astype(BF16)
        kd_ref[0] = ukd.astype(BF16)
    vd_ref[0] = seg("vd").astype(BF16)
    if with_nat:
        nat_ref[0] = seg("nat")
    mq_ref[0] = (seg("mq") * ML_DK ** -0.5).astype(BF16)
    mk_ref[0] = seg("mk").astype(BF16)
    mv_ref[0] = seg("mv").astype(BF16)
    gates_ref[0] = seg("gates") + bg_ref[...]
    mo_ref[0] = seg("mo")
    ga_ref[0] = seg("ga")
    gm_ref[0] = seg("gm")


def _project(x, mod, g_pre, w_in, b_gates, mod_row, rope, with_nat, tm):
    bsz, seq, d = x.shape
    wp, bg, segs = _pack_w_in(w_in, b_gates, rope, with_nat)
    nw = wp.shape[1]
    tok3 = lambda b, i: (b, i, 0)
    in_specs = [pl.BlockSpec((1, tm, d), tok3),
                pl.BlockSpec((1, 6, d), lambda b, i: (mod_row(b), 0, 0)),
                _const_spec((1, d)),
                pl.BlockSpec((d, nw), lambda b, i: (0, 0), pipeline_mode=pl.Buffered(1)),
                _const_spec((1, LANES))]
    args = [x, mod, g_pre.reshape(1, d), wp, bg]
    if rope:
        cos_t, sin_t = _rope_tables(seq)
        in_specs += [pl.BlockSpec((tm, LANES), lambda b, i: (i, 0))] * 2
        args += [cos_t, sin_t]

    def tok_out(width, dtype):
        return (jax.ShapeDtypeStruct((bsz, seq, width), dtype), pl.BlockSpec((1, tm, width), tok3))

    outs = [(jax.ShapeDtypeStruct((bsz, ATT_HEADS, seq, LANES), BF16),
             pl.BlockSpec((1, ATT_HEADS, tm, LANES), lambda b, i: (b, 0, i, 0))),
            tok_out(2 * LANES, BF16), tok_out(2 * LANES, BF16)]
    if with_nat:
        outs.append(tok_out(2 * KV_W, F32))
    outs += [tok_out(MLQK_W, BF16), tok_out(MLQK_W, BF16), tok_out(MLV_W, BF16), tok_out(LANES, F32),
             tok_out(MLV_W, F32), tok_out(D_MODEL, F32), tok_out(D_MODEL, F32)]
    res = pl.pallas_call(
        functools.partial(_proj_kernel, segs=segs, rope=rope, with_nat=with_nat),
        grid=(bsz, seq // tm),
        in_specs=in_specs,
        out_specs=[o[1] for o in outs],
        out_shape=[o[0] for o in outs],
        compiler_params=_cparams(("parallel", "parallel")),
        name="proj_rope" if rope else "proj_ctx",
    )(*args)
    names = ["q", "kd", "vd"] + (["nat"] if with_nat else []) + ["mq", "mk", "mv", "gates", "mo", "ga", "gm"]
    return dict(zip(names, res))


def _attn_kernel(*refs, window, n_blocks, n_ctx):
    it = iter(refs)
    sink_ref, q_ref = next(it), next(it)
    if window:
        kp_ref, kc_ref, kn_ref, vp_ref, vc_ref, vn_ref = (next(it) for _ in range(6))
    ck_ref, cv_ref, o_ref = next(it), next(it), next(it)
    j = pl.program_id(1)
    n_win = 3 * BLOCK if window else 0
    n_keys = n_win + n_ctx
    rows = ATT_GROUP * BLOCK

    row_id = lax.broadcasted_iota(jnp.int32, (rows, 1), 0)
    if window:
        rr = lax.broadcasted_iota(jnp.int32, (rows, n_keys), 0) & (BLOCK - 1)
        cc = lax.broadcasted_iota(jnp.int32, (rows, n_keys), 1)
        dd = cc - rr
        lo_b = jnp.where(j == 0, BLOCK, 0)
        hi_b = jnp.where(j == n_blocks - 1, 2 * BLOCK - 1, 3 * BLOCK - 1)
        in_band = (dd >= 0) & (dd <= 2 * BLOCK) & (cc >= lo_b) & (cc <= hi_b)
        mask = in_band | (cc >= n_win)
    lane_lo = lax.broadcasted_iota(jnp.int32, (BLOCK, LANES), 1) < HEAD_DIM

    for kvh in range(ATT_KV_HEADS):
        sl = slice(kvh * LANES, (kvh + 1) * LANES)
        if window:
            keys = jnp.concatenate([kp_ref[0, :, sl], kc_ref[0, :, sl], kn_ref[0, :, sl], ck_ref[0, :, sl]], axis=0)
            vals = jnp.concatenate([vp_ref[0, :, sl], vc_ref[0, :, sl], vn_ref[0, :, sl], cv_ref[0, :, sl]], axis=0)
        else:
            keys = ck_ref[0, :, sl]
            vals = cv_ref[0, :, sl]
        qs = q_ref[0, kvh * ATT_GROUP:(kvh + 1) * ATT_GROUP].reshape(rows, LANES)
        s = lax.dot_general(qs, keys, (((1,), (1,)), ((), ())), preferred_element_type=F32)
        if window:
            s = jnp.where(mask, s, NEG)
        snk = jnp.full((rows, 1), sink_ref[kvh * ATT_GROUP + ATT_GROUP - 1], F32)
        for g in range(ATT_GROUP - 2, -1, -1):
            snk = jnp.where(row_id < (g + 1) * BLOCK, sink_ref[kvh * ATT_GROUP + g], snk)
        m = jnp.maximum(jnp.max(s, axis=-1, keepdims=True), snk)
        p = jnp.exp(s - m)
        den = jnp.sum(p, axis=-1, keepdims=True) + jnp.exp(snk - m)
        o = jnp.dot(p.astype(BF16), vals, preferred_element_type=F32) / den
        for pair in range(ATT_GROUP // 2):
            even = o[(2 * pair) * BLOCK:(2 * pair + 1) * BLOCK]
            odd = o[(2 * pair + 1) * BLOCK:(2 * pair + 2) * BLOCK]
            col = (kvh * (ATT_GROUP // 2) + pair) * LANES
            o_ref[0, :, col:col + LANES] = jnp.where(lane_lo, even, odd).astype(BF16)


def _attention(q, kd, vd, ckd, cvd, sink, window):
    bsz, _, seq, _ = q.shape
    nb = seq // BLOCK
    n_ctx = ckd.shape[1]
    in_specs = [pl.BlockSpec(memory_space=pltpu.SMEM),
                pl.BlockSpec((1, ATT_HEADS, BLOCK, LANES), lambda b, j: (b, 0, j, 0))]
    args = [sink.astype(F32), q]
    if window:
        prev = lambda b, j: (b, jnp.maximum(j - 1, 0), 0)
        cur = lambda b, j: (b, j, 0)
        nxt = lambda b, j: (b, jnp.minimum(j + 1, nb - 1), 0)
        for arr in (kd, vd):
            in_specs += [pl.BlockSpec((1, BLOCK, 2 * LANES), f) for f in (prev, cur, nxt)]
            args += [arr, arr, arr]
    in_specs += [pl.BlockSpec((1, n_ctx, 2 * LANES), lambda b, j: (b, 0, 0))] * 2
    args += [ckd, cvd]
    return pl.pallas_call(
        functools.partial(_attn_kernel, window=window, n_blocks=nb, n_ctx=n_ctx),
        grid=(bsz, nb),
        in_specs=in_specs,
        out_specs=pl.BlockSpec((1, BLOCK, Q_W), lambda b, j: (b, j, 0)),
        out_shape=jax.ShapeDtypeStruct((bsz, seq, Q_W), BF16),
        compiler_params=_cparams(("parallel", "parallel")),
        name="attn_window" if window else "attn_ctx",
    )(*args)


def _dup_heads(t):
    b, l = t.shape[:2]
    return jnp.concatenate([t, t], axis=-1).reshape(b, l, ATT_KV_HEADS * LANES).astype(BF16)


def _split3(x):
    hi = x.astype(BF16)
    r1 = x - hi.astype(F32)
    mid = r1.astype(BF16)
    lo = (r1 - mid.astype(F32)).astype(BF16)
    return hi, mid, lo


def _log_sigmoid(x):
    return jnp.minimum(x, 0.0) - jnp.log1p(jnp.exp(-jnp.abs(x)))


def _mlstm_kernel(*refs, n_chunks, has_init, emit_state):
    it = iter(refs)
    q_ref, k_ref, v_ref, g_ref = (next(it) for _ in range(4))
    if has_init:
        c0_ref, n0_ref, m0_ref = next(it), next(it), next(it)
    h_ref = next(it)
    if emit_state:
        co_ref, no_ref, mo_ref = next(it), next(it), next(it)
    ct_s, n_s, m_s = next(it), next(it), next(it)
    d = pl.program_id(1)
    t = pl.program_id(2)
    L = ML_CHUNK

    @pl.when(t == 0)
    def _():
        for h in range(ML_HEADS):
            if has_init:
                ct_s[h] = c0_ref[0, 0, h].T
                n_s[h] = n0_ref[0, 0, h:h + 1, :]
                m_s[h] = m0_ref[0, 0, h:h + 1, :]
            else:
                ct_s[h] = jnp.zeros((ML_DK, ML_DV), F32)
                n_s[h] = jnp.zeros((1, ML_DK), F32)
                m_s[h] = jnp.zeros((1, LANES), F32)

    ri = lax.broadcasted_iota(jnp.int32, (L, L), 0)
    ci = lax.broadcasted_iota(jnp.int32, (L, L), 1)
    tri = (ri - ci) * jnp.where(d == 0, 1, -1) >= 0
    tri_b = jnp.where(tri, 1.0, 0.0).astype(BF16)

    g = g_ref[0]
    g = jnp.where(d == 0, g, pltpu.roll(g, LANES - 2 * ML_HEADS, 1))
    lf = _log_sigmoid(g)
    g_t = g.T
    lf_t = lf.T
    bc = sum(jnp.dot(tri_b, part, preferred_element_type=F32) for part in _split3(lf))
    br = sum(lax.dot_general(part, tri_b, (((1,), (1,)), ((), ())), preferred_element_type=F32)
             for part in _split3(lf_t))
    tot = jnp.sum(lf, axis=0, keepdims=True)

    for h in range(ML_HEADS):
        fcol = ML_HEADS + h
        bcol = bc[:, fcol:fcol + 1]
        brow = br[fcol:fcol + 1, :]
        irow = g_t[h:h + 1, :]
        icol = g[:, h:h + 1]
        b_last = tot[:, fcol:fcol + 1]
        m_prev = m_s[h][:, 0:1]
        n_prev = n_s[h]
        ct_prev = ct_s[h]
        qh = q_ref[0, :, h * ML_DK:(h + 1) * ML_DK]
        kh = k_ref[0, :, h * ML_DK:(h + 1) * ML_DK]
        vh = v_ref[0, :, h * ML_DV:(h + 1) * ML_DV]

        dmat = jnp.where(tri, bcol - brow + irow, NEG)
        inter = bcol + m_prev
        m_t = jnp.maximum(inter, jnp.max(dmat, axis=-1, keepdims=True))
        w = jnp.exp(dmat - m_t)
        w_inter = jnp.exp(inter - m_t)
        s = lax.dot_general(qh, kh, (((1,), (1,)), ((), ())), preferred_element_type=F32) * w
        cq = jnp.dot(qh, ct_prev.astype(BF16), preferred_element_type=F32) * w_inter
        num = jnp.dot(s.astype(BF16), vh, preferred_element_type=F32) + cq
        qn = jnp.sum(qh.astype(F32) * n_prev, axis=-1, keepdims=True)
        den = jnp.sum(s, axis=-1, keepdims=True) + w_inter * qn
        h_ref[0, 0, :, h * ML_DV:(h + 1) * ML_DV] = num / jnp.maximum(jnp.abs(den), jnp.exp(-m_t))

        a = b_last - bcol + icol
        m_new = jnp.maximum(b_last + m_prev, jnp.max(a, axis=0, keepdims=True))
        wk = jnp.exp(a - m_new)
        decay = jnp.exp(b_last + m_prev - m_new)
        kw = kh.astype(F32) * wk
        ct_new = decay * ct_prev + jnp.dot(kw.T.astype(BF16), vh, preferred_element_type=F32)
        n_new = decay * n_prev + jnp.sum(kw, axis=0, keepdims=True)
        ct_s[h] = ct_new
        n_s[h] = n_new
        m_s[h] = jnp.broadcast_to(m_new, (1, LANES))
        if emit_state:
            @pl.when(t == n_chunks - 1)
            def _():
                co_ref[0, 0, h] = ct_new.T
                no_ref[0, 0, h:h + 1, :] = n_new
                mo_ref[0, 0, h:h + 1, :] = jnp.broadcast_to(m_new, (1, LANES))


def _mlstm(mq, mk, mv, gates, init, emit_state):
    bsz, seq, _ = mq.shape
    nc = seq // ML_CHUNK
    chunk = lambda b, d, t: (b, jnp.where(d == 0, t, nc - 1 - t), 0)
    state5 = lambda b, d, t: (b, d, 0, 0, 0)
    state4 = lambda b, d, t: (b, d, 0, 0)
    in_specs = [pl.BlockSpec((1, ML_CHUNK, MLQK_W), chunk), pl.BlockSpec((1, ML_CHUNK, MLQK_W), chunk),
                pl.BlockSpec((1, ML_CHUNK, MLV_W), chunk), pl.BlockSpec((1, ML_CHUNK, LANES), chunk)]
    args = [mq, mk, mv, gates]
    if init is not None:
        c0, n0, m0 = init
        in_specs += [pl.BlockSpec((1, 1, ML_HEADS, ML_DV, ML_DK), state5),
                     pl.BlockSpec((1, 1, ML_HEADS, ML_DK), state4),
                     pl.BlockSpec((1, 1, ML_HEADS, LANES), state4)]
        args += [c0.astype(F32), n0.astype(F32),
                 jnp.broadcast_to(m0.astype(F32)[..., None], m0.shape + (LANES,))]
    out_shape = [jax.ShapeDtypeStruct((2, bsz, seq, MLV_W), F32)]
    out_specs = [pl.BlockSpec((1, 1, ML_CHUNK, MLV_W),
                              lambda b, d, t: (d, b, jnp.where(d == 0, t, nc - 1 - t), 0))]
    if emit_state:
        out_shape += [jax.ShapeDtypeStruct((bsz, 2, ML_HEADS, ML_DV, ML_DK), F32),
                      jax.ShapeDtypeStruct((bsz, 2, ML_HEADS, ML_DK), F32),
                      jax.ShapeDtypeStruct((bsz, 2, ML_HEADS, LANES), F32)]
        out_specs += [pl.BlockSpec((1, 1, ML_HEADS, ML_DV, ML_DK), state5),
                      pl.BlockSpec((1, 1, ML_HEADS, ML_DK), state4),
                      pl.BlockSpec((1, 1, ML_HEADS, LANES), state4)]
    return pl.pallas_call(
        functools.partial(_mlstm_kernel, n_chunks=nc, has_init=init is not None, emit_state=emit_state),
        grid=(bsz, 2, nc),
        in_specs=in_specs,
        out_specs=out_specs,
        out_shape=out_shape,
        scratch_shapes=[pltpu.VMEM((ML_HEADS, ML_DK, ML_DV), F32),
                        pltpu.VMEM((ML_HEADS, 1, ML_DK), F32),
                        pltpu.VMEM((ML_HEADS, 1, LANES), F32)],
        compiler_params=_cparams(("parallel", "parallel", "arbitrary")),
        name="mlstm_state" if emit_state else "mlstm",
    )(*args)


def _rms(x, g):
    return x * lax.rsqrt(jnp.mean(x * x, axis=-1, keepdims=True) + EPS) * g


def _merge_kernel(att_ref, hf_ref, hb_ref, mo_ref, ga_ref, gm_ref, x_ref, mod_ref, nml_ref, gpm_ref, gpf_ref,
                  wua_ref, wum_ref, wo_ref, wr_ref, br_ref, x1_ref, h2_ref, idx_ref, gate_ref):
    hsum = hf_ref[0, 0] + hb_ref[0, 0]
    parts = []
    for h in range(ML_HEADS):
        hh = hsum[:, h * ML_DV:(h + 1) * ML_DV]
        parts.append(hh * lax.rsqrt(jnp.mean(hh * hh, axis=-1, keepdims=True) + EPS))
    hn = jnp.concatenate(parts, axis=-1) * nml_ref[...]
    ml = (hn * jax.nn.sigmoid(mo_ref[0])).astype(BF16)
    a = jnp.dot(att_ref[0], wua_ref[...], preferred_element_type=F32)
    m = jnp.dot(ml, wum_ref[...], preferred_element_type=F32)
    z = (jax.nn.sigmoid(ga_ref[0]) * a + jax.nn.sigmoid(gm_ref[0]) * m).astype(BF16)
    mix = jnp.dot(z, wo_ref[...], preferred_element_type=F32)
    gate1 = mod_ref[0, 2:3, :]
    shift2 = mod_ref[0, 3:4, :]
    scale2 = mod_ref[0, 4:5, :]
    x1 = x_ref[0] + gate1 * _rms(mix, gpm_ref[...])
    x1_ref[0] = x1
    h2 = _rms(x1, gpf_ref[...]) * (1.0 + scale2) + shift2
    h2_ref[0] = h2

    h2_hi = h2.astype(BF16)
    h2_lo = (h2 - h2_hi.astype(F32)).astype(BF16)
    wr = wr_ref[...]
    wr_hi = wr.astype(BF16)
    wr_lo = (wr - wr_hi.astype(F32)).astype(BF16)
    logits = (jnp.dot(h2_hi, wr_hi, preferred_element_type=F32)
              + jnp.dot(h2_lo, wr_hi, preferred_element_type=F32)
              + jnp.dot(h2_hi, wr_lo, preferred_element_type=F32)) + br_ref[...]
    lane = lax.broadcasted_iota(jnp.int32, logits.shape, 1).astype(F32)
    work = jnp.where(lane < N_EXPERTS, logits, -jnp.inf)
    idx_out = jnp.zeros(logits.shape, F32)
    val_out = jnp.zeros(logits.shape, F32)
    top0 = None
    esum = None
    for k in range(TOP_K):
        mx = jnp.max(work, axis=-1, keepdims=True)
        sel = jnp.min(jnp.where(work == mx, lane, float(LANES)), axis=-1, keepdims=True)
        if k == 0:
            top0 = mx
        e = jnp.exp(mx - top0)
        esum = e if k == 0 else esum + e
        idx_out = jnp.where(lane == k, sel, idx_out)
        val_out = jnp.where(lane == k, e, val_out)
        work = jnp.where(lane == sel, -jnp.inf, work)
    idx_ref[0] = idx_out.astype(jnp.int32)
    gate_ref[0] = val_out / esum


def _merge(att, h, pj, x, mod, mod_row, p, tm):
    bsz, seq, d = x.shape
    tok3 = lambda b, i: (b, i, 0)
    wr = jnp.pad(p["w_router"].astype(F32), ((0, 0), (0, LANES - N_EXPERTS)))
    br = jnp.pad(p["b_router"].astype(F32), (0, LANES - N_EXPERTS)).reshape(1, LANES)
    row = lambda v: v.astype(F32).reshape(1, -1)
    in_specs = [pl.BlockSpec((1, tm, Q_W), tok3),
                pl.BlockSpec((1, 1, tm, MLV_W), lambda b, i: (0, b, i, 0)),
                pl.BlockSpec((1, 1, tm, MLV_W), lambda b, i: (1, b, i, 0)),
                pl.BlockSpec((1, tm, MLV_W), tok3), pl.BlockSpec((1, tm, d), tok3), pl.BlockSpec((1, tm, d), tok3),
                pl.BlockSpec((1, tm, d), tok3),
                pl.BlockSpec((1, 6, d), lambda b, i: (mod_row(b), 0, 0)),
                _const_spec((1, MLV_W)), _const_spec((1, d)), _const_spec((1, d)),
                _const_spec((Q_W, d)), _const_spec((MLV_W, d)), _const_spec((d, d)),
                _const_spec((d, LANES)), _const_spec((1, LANES))]
    out_shape = [jax.ShapeDtypeStruct((bsz, seq, d), F32), jax.ShapeDtypeStruct((bsz, seq, d), F32),
                 jax.ShapeDtypeStruct((bsz, seq, LANES), jnp.int32), jax.ShapeDtypeStruct((bsz, seq, LANES), F32)]
    out_specs = [pl.BlockSpec((1, tm, d), tok3), pl.BlockSpec((1, tm, d), tok3),
                 pl.BlockSpec((1, tm, LANES), tok3), pl.BlockSpec((1, tm, LANES), tok3)]
    return pl.pallas_call(
        _merge_kernel,
        grid=(bsz, seq // tm),
        in_specs=in_specs,
        out_specs=out_specs,
        out_shape=out_shape,
        compiler_params=_cparams(("parallel", "parallel")),
        name="merge_router",
    )(att, h, h, pj["mo"], pj["ga"], pj["gm"], x, mod, row(p["norm_ml"]), row(p["g_post_mix"]),
      row(p["g_pre_ffn"]), p["w_up_att"].astype(BF16), p["w_up_ml"].astype(BF16), p["w_out"].astype(BF16), wr, br)


def _moe_kernel(off_ref, tok_ref, gate_ref, x_ref, w1_ref, b1_ref, w2_ref, b2_ref, y_ref, xg_s, ys_s):
    t = pl.program_id(0)
    e = pl.program_id(1)

    @pl.when((t == 0) & (e == 0))
    def _():
        xg_s[...] = jnp.zeros(xg_s.shape, F32)

    @pl.when(e == 0)
    def _():
        y_ref[...] = jnp.zeros(y_ref.shape, F32)

    start = off_ref[t * (N_EXPERTS + 1) + e]
    count = off_ref[t * (N_EXPERTS + 1) + e + 1] - start
    n_sub = (count + MOE_ROWS - 1) // MOE_ROWS

    def sub_block(jb, carry):
        base = start + jb * MOE_ROWS
        n_rows = jnp.minimum(MOE_ROWS, count - jb * MOE_ROWS)
        n_grp = (n_rows + 7) // 8

        def gather(gi, c):
            r0 = pl.multiple_of(gi * 8, 8)
            for u in range(8):
                tok = tok_ref[0, 0, base + r0 + u]
                xg_s[pl.ds(r0 + u, 1), :] = x_ref[pl.ds(tok, 1), :]
            return c

        lax.fori_loop(0, n_grp, gather, 0)
        xb = xg_s[...].astype(BF16)
        hmid = jnp.dot(xb, w1_ref[0], preferred_element_type=F32) + b1_ref[0]
        glu = jnp.minimum(hmid[:, :D_FF], SWIGLU_LIMIT)
        lin = jnp.clip(hmid[:, D_FF:], -SWIGLU_LIMIT, SWIGLU_LIMIT)
        act = ((lin + 1.0) * glu * jax.nn.sigmoid(SWIGLU_ALPHA * glu)).astype(BF16)
        ys_s[...] = jnp.dot(act, w2_ref[0], preferred_element_type=F32) + b2_ref[0]

        def scatter(gi, c):
            r0 = pl.multiple_of(gi * 8, 8)
            for u in range(8):
                r = r0 + u
                tok = tok_ref[0, 0, base + r]
                gt = jnp.where(r < n_rows, gate_ref[0, 0, base + r], 0.0)
                y_ref[pl.ds(tok, 1), :] = y_ref[pl.ds(tok, 1), :] + gt * ys_s[pl.ds(r, 1), :]
            return c

        lax.fori_loop(0, n_grp, scatter, 0)
        return carry

    lax.fori_loop(0, n_sub, sub_block, 0)


def _route(idx, gate, n_tok):
    n_tiles = n_tok // MOE_TILE
    n_assign = MOE_TILE * TOP_K
    flat_e = idx.reshape(n_tiles, n_assign)
    flat_g = gate.reshape(n_tiles, n_assign)
    order = jnp.argsort(flat_e, axis=1, stable=True)
    tok_sorted = (order // TOP_K).astype(jnp.int32)
    gate_sorted = jnp.take_along_axis(flat_g, order, axis=1)
    counts = jnp.sum(flat_e[:, :, None] == jnp.arange(N_EXPERTS, dtype=jnp.int32)[None, None, :], axis=1)
    offs = jnp.concatenate([jnp.zeros((n_tiles, 1), jnp.int32), jnp.cumsum(counts, axis=1).astype(jnp.int32)], axis=1)
    pad = ((0, 0), (0, MOE_PAD))
    return (offs.reshape(-1), jnp.pad(tok_sorted, pad).reshape(n_tiles, 1, -1),
            jnp.pad(gate_sorted, pad).reshape(n_tiles, 1, -1))


def _moe(h2, idx, gate, w1, b1, w2, b2):
    n_tok, d = h2.shape
    n_tiles = n_tok // MOE_TILE
    offs, tok_sorted, gate_sorted = _route(idx, gate, n_tok)
    n_list = tok_sorted.shape[-1]
    grid_spec = pltpu.PrefetchScalarGridSpec(
        num_scalar_prefetch=1,
        grid=(n_tiles, N_EXPERTS),
        in_specs=[pl.BlockSpec((1, 1, n_list), lambda t, e, o: (t, 0, 0), memory_space=pltpu.SMEM),
                  pl.BlockSpec((1, 1, n_list), lambda t, e, o: (t, 0, 0), memory_space=pltpu.SMEM),
                  pl.BlockSpec((MOE_TILE, d), lambda t, e, o: (t, 0), pipeline_mode=pl.Buffered(1)),
                  pl.BlockSpec((1, d, 2 * D_FF), lambda t, e, o: (e, 0, 0)),
                  pl.BlockSpec((1, 1, 2 * D_FF), lambda t, e, o: (e, 0, 0)),
                  pl.BlockSpec((1, D_FF, d), lambda t, e, o: (e, 0, 0)),
                  pl.BlockSpec((1, 1, d), lambda t, e, o: (e, 0, 0))],
        out_specs=pl.BlockSpec((MOE_TILE, d), lambda t, e, o: (t, 0)),
        scratch_shapes=[pltpu.VMEM((MOE_ROWS, d), F32), pltpu.VMEM((MOE_ROWS, d), F32)],
    )
    return pl.pallas_call(
        _moe_kernel,
        grid_spec=grid_spec,
        out_shape=jax.ShapeDtypeStruct((n_tok, d), F32),
        compiler_params=_cparams(("parallel", "arbitrary")),
        name="moe_ffn",
    )(offs, tok_sorted, gate_sorted, h2, w1, b1.reshape(N_EXPERTS, 1, -1), w2, b2.reshape(N_EXPERTS, 1, -1))


def _final_kernel(x1_ref, y_ref, mod_ref, g_ref, o_ref):
    gate2 = mod_ref[0, 5:6, :]
    o_ref[0] = x1_ref[0] + gate2 * _rms(y_ref[0], g_ref[...])


def _final(x1, y, mod, mod_row, g_post_ffn, tm):
    bsz, seq, d = x1.shape
    tok3 = lambda b, i: (b, i, 0)
    return pl.pallas_call(
        _final_kernel,
        grid=(bsz, seq // tm),
        in_specs=[pl.BlockSpec((1, tm, d), tok3), pl.BlockSpec((1, tm, d), tok3),
                  pl.BlockSpec((1, 6, d), lambda b, i: (mod_row(b), 0, 0)), _const_spec((1, d))],
        out_specs=pl.BlockSpec((1, tm, d), tok3),
        out_shape=jax.ShapeDtypeStruct((bsz, seq, d), F32),
        compiler_params=_cparams(("parallel", "parallel")),
        name="final_residual",
    )(x1, y, mod, g_post_ffn.astype(F32).reshape(1, d))


def _stream(x, mod, mod_row, p, ctx_kv, init_state, rope, tm):
    bsz, seq, d = x.shape
    is_ctx = ctx_kv is None
    pj = _project(x, mod, p["g_pre_mix"], p["w_in"], p["b_gates"], mod_row, rope, is_ctx, tm)
    if is_ctx:
        att = _attention(pj["q"], None, None, pj["kd"], pj["vd"], p["sink"], window=False)
    else:
        att = _attention(pj["q"], pj["kd"], pj["vd"], ctx_kv[0], ctx_kv[1], p["sink"], window=True)
    ml = _mlstm(pj["mq"], pj["mk"], pj["mv"], pj["gates"], init_state, emit_state=is_ctx)
    x1, h2, idx, gate = _merge(att, ml[0], pj, x, mod, mod_row, p, tm)
    w1b, w2b = p["w1"].astype(BF16), p["w2"].astype(BF16)
    y = _moe(h2.reshape(bsz * seq, d), idx[..., :TOP_K], gate[..., :TOP_K], w1b, p["b1"].astype(F32),
             w2b, p["b2"].astype(F32))
    out = _final(x1, y.reshape(bsz, seq, d), mod, mod_row, p["g_post_ffn"], tm)
    return out, pj, ml


def kernel(x_prompt, x_sample, c, cache_k, cache_v, state_C, state_n, state_m, c_ctx, w_ada, b_ada, g_pre_mix,
           w_in, b_gates, attn_sink, norm_mlstm, w_up_att, w_up_ml, w_out, g_post_mix, g_pre_ffn, w_router,
           b_router, w1, b1, w2, b2, g_post_ffn):
    depth = w_ada.shape[0]
    n_dec = c.shape[0]
    cond = jnp.concatenate([c_ctx[None, :], c], axis=0).astype(F32)
    cond = jnp.pad(cond, ((0, 16 - cond.shape[0]), (0, 0)))
    y_prompt, y_sample = x_prompt, x_sample
    ks_, vs_, cs_, ns_, ms_ = [], [], [], [], []
    for l in range(depth):
        p = dict(g_pre_mix=g_pre_mix[l], w_in=w_in[l], b_gates=b_gates[l], sink=attn_sink[l], norm_ml=norm_mlstm[l],
                 w_up_att=w_up_att[l], w_up_ml=w_up_ml[l], w_out=w_out[l], g_post_mix=g_post_mix[l],
                 g_pre_ffn=g_pre_ffn[l], w_router=w_router[l], b_router=b_router[l], w1=w1[l], b1=b1[l],
                 w2=w2[l], b2=b2[l], g_post_ffn=g_post_ffn[l])
        mod = _adaln(cond, w_ada[l], b_ada[l]).reshape(16, 6, D_MODEL)
        y_prompt, pj, ml = _stream(y_prompt, mod, lambda b: 0, p, None, None, rope=False, tm=256)
        bsz, seq = x_prompt.shape[:2]
        nat = pj["nat"]
        ks_.append(nat[..., :KV_W].reshape(bsz, seq, ATT_KV_HEADS, HEAD_DIM))
        vs_.append(nat[..., KV_W:].reshape(bsz, seq, ATT_KV_HEADS, HEAD_DIM))
        cs_.append(ml[1])
        ns_.append(ml[2])
        ms_.append(ml[3][..., 0])
        ctx_kv = (_dup_heads(cache_k[:, l]), _dup_heads(cache_v[:, l]))
        init = (state_C[:, l], state_n[:, l], state_m[:, l])
        y_sample, _, _ = _stream(y_sample, mod, lambda b: b + 1, p, ctx_kv, init, rope=True, tm=512)
    return (y_prompt, y_sample, jnp.stack(ks_, axis=1), jnp.stack(vs_, axis=1), jnp.stack(cs_, axis=1),
            jnp.stack(ns_, axis=1), jnp.stack(ms_, axis=1))
```

```python
import functools

import numpy as np
import jax
import jax.numpy as jnp
from jax import lax
from jax.experimental import pallas as pl
from jax.experimental.pallas import tpu as pltpu
from jax.experimental.pallas import tpu_sc as plsc

F32 = jnp.float32
BF16 = jnp.bfloat16

D_MODEL = 1024
GRID_W = 64
ATT_HEADS = 8
ATT_KV_HEADS = 2
ATT_GROUP = ATT_HEADS // ATT_KV_HEADS
HEAD_DIM = 64
BLOCK = 128
ROPE_THETA = 10000.0
AXIS_FREQS = HEAD_DIM // 4
ML_HEADS = 4
ML_DK = 128
ML_DV = 256
ML_CHUNK = 128
N_EXPERTS = 32
TOP_K = 4
D_FF = 1024
SWIGLU_ALPHA = 1.702
SWIGLU_LIMIT = 7.0
EPS = 1e-6
NEG = -1e30

Q_W = ATT_HEADS * HEAD_DIM
KV_W = ATT_KV_HEADS * HEAD_DIM
MLQK_W = ML_HEADS * ML_DK
MLV_W = ML_HEADS * ML_DV
GATE_W = 4 * ML_HEADS

LANES = 128
VMEM_LIMIT = 56 * 1024 * 1024
MOE_ROWS = 256
SC_CORES = 2
SC_SUBCORES = 16
SC_GATHER_CHUNK = 32


def _cparams(sem):
    return pltpu.CompilerParams(dimension_semantics=sem, vmem_limit_bytes=VMEM_LIMIT)


def _const_spec(shape):
    nd = len(shape)
    return pl.BlockSpec(shape, lambda *_: (0,) * nd)


def _adaln_kernel(cond_ref, w_ref, b_ref, o_ref):
    cnd = cond_ref[...]
    act = cnd * jax.nn.sigmoid(cnd)
    o_ref[...] = jnp.dot(act.astype(BF16), w_ref[...].astype(BF16),
                         preferred_element_type=F32) + b_ref[...]


def _adaln(cond, w_ada, b_ada):
    rows, d = cond.shape
    n = w_ada.shape[1]
    tn = 1536
    return pl.pallas_call(
        _adaln_kernel,
        grid=(n // tn,),
        in_specs=[_const_spec((rows, d)),
                  pl.BlockSpec((d, tn), lambda j: (0, j)),
                  pl.BlockSpec((1, tn), lambda j: (0, j))],
        out_specs=pl.BlockSpec((rows, tn), lambda j: (0, j)),
        out_shape=jax.ShapeDtypeStruct((rows, n), F32),
        compiler_params=_cparams(("parallel",)),
        name="adaln",
    )(cond, w_ada, b_ada.reshape(1, n))


def _pack_w_in(w_in, b_gates, rope, with_nat):
    o_q, o_k, o_v = 0, Q_W, Q_W + KV_W
    o_mq = Q_W + 2 * KV_W
    o_mk = o_mq + MLQK_W
    o_mv = o_mk + MLQK_W
    o_g = o_mv + MLV_W
    o_mo = o_g + GATE_W
    o_ga = o_mo + MLV_W
    o_gm = o_ga + D_MODEL
    half = HEAD_DIM // 2
    zeros64 = jnp.zeros((D_MODEL, HEAD_DIM), w_in.dtype)

    def head_cols(base, h, rot):
        lo = base + h * HEAD_DIM
        if rot:
            return [w_in[:, lo + half:lo + HEAD_DIM], w_in[:, lo:lo + half]]
        return [w_in[:, lo:lo + HEAD_DIM]]

    def q_cols(rot):
        cols = []
        for h in range(ATT_HEADS):
            hc = head_cols(o_q, h, rot)
            cols += (hc + [zeros64]) if h % 2 == 0 else ([zeros64] + hc)
        return cols

    def dup_cols(base, rot):
        cols = []
        for h in range(ATT_KV_HEADS):
            hc = head_cols(base, h, rot)
            cols += hc + hc
        return cols

    segs, cols, off = {}, [], 0

    def add(name, cl):
        nonlocal off
        width = sum(c.shape[1] for c in cl)
        segs[name] = (off, off + width)
        cols.extend(cl)
        off += width

    add("q", q_cols(False))
    if rope:
        add("qrot", q_cols(True))
    add("kd", dup_cols(o_k, False))
    if rope:
        add("kdrot", dup_cols(o_k, True))
    add("vd", dup_cols(o_v, False))
    if with_nat:
        add("nat", [w_in[:, o_k:o_k + 2 * KV_W]])
    add("mq", [w_in[:, o_mq:o_mq + MLQK_W]])
    add("mk", [w_in[:, o_mk:o_mk + MLQK_W]])
    add("mv", [w_in[:, o_mv:o_mv + MLV_W]])
    add("mo", [w_in[:, o_mo:o_mo + MLV_W]])
    add("ga", [w_in[:, o_ga:o_ga + D_MODEL]])
    add("gm", [w_in[:, o_gm:o_gm + D_MODEL]])
    add("gates", [w_in[:, o_g:o_g + GATE_W], jnp.zeros((D_MODEL, LANES - GATE_W), w_in.dtype)])
    wp = jnp.concatenate(cols, axis=1).astype(BF16)
    bg = jnp.pad(b_gates.astype(F32), (0, LANES - GATE_W)).reshape(1, LANES)
    return wp, bg, segs


def _rope_tables(n_tok):
    rows = n_tok // GRID_W
    row = np.repeat(np.arange(rows), GRID_W).astype(np.float32)
    col = np.tile(np.arange(GRID_W), rows).astype(np.float32)
    inv = (np.float32(ROPE_THETA) ** (-np.arange(AXIS_FREQS, dtype=np.float32) / AXIS_FREQS)).astype(np.float32)
    ang = np.concatenate([row[:, None] * inv, col[:, None] * inv], axis=-1).astype(np.float32)
    cos, sin = np.cos(ang), np.sin(ang)
    c64 = np.concatenate([cos, cos], axis=-1)
    s64 = np.concatenate([-sin, sin], axis=-1)
    return (jnp.asarray(np.tile(c64, (1, 2)), F32), jnp.asarray(np.tile(s64, (1, 2)), F32))


def _proj_kernel(*refs, segs, rope, with_nat):
    it = iter(refs)
    x_ref, mod_ref, g_ref, w_ref, bg_ref = (next(it) for _ in range(5))
    cos_ref = sin_ref = None
    if rope:
        cos_ref, sin_ref = next(it), next(it)
    q_ref, kd_ref, vd_ref = next(it), next(it), next(it)
    nat_ref = next(it) if with_nat else None
    mq_ref, mk_ref, mv_ref, gates_ref, mo_ref, ga_ref, gm_ref = (next(it) for _ in range(7))

    x = x_ref[0]
    shift = mod_ref[0, 0:1, :]
    scale = mod_ref[0, 1:2, :]
    ms = jnp.mean(x * x, axis=-1, keepdims=True)
    xn = x * lax.rsqrt(ms + EPS) * g_ref[...]
    xb = (xn * (1.0 + scale) + shift).astype(BF16)

    def seg(name):
        lo, hi = segs[name]
        return jnp.dot(xb, w_ref[:, lo:hi], preferred_element_type=F32)

    uq = seg("q")
    ukd = seg("kd")
    if rope:
        uqr = seg("qrot")
        ukr = seg("kdrot")
        cs = cos_ref[...]
        sn = sin_ref[...]
        for h in range(ATT_HEADS):
            sl = slice(h * LANES, (h + 1) * LANES)
            q_ref[0, h] = ((uq[:, sl] * cs + uqr[:, sl] * sn) * HEAD_DIM ** -0.5).astype(BF16)
        for h in range(ATT_KV_HEADS):
            sl = slice(h * LANES, (h + 1) * LANES)
            kd_ref[0, :, sl] = (ukd[:, sl] * cs + ukr[:, sl] * sn).astype(BF16)
    else:
        for h in range(ATT_HEADS):
            sl = slice(h * LANES, (h + 1) * LANES)
            q_ref[0, h] = (uq[:, sl] * HEAD_DIM ** -0.5).astype(BF16)
        kd_ref[0] = ukd.astype(BF16)
    vd_ref[0] = seg("vd").astype(BF16)
    if with_nat:
        nat_ref[0] = seg("nat")
    mq_ref[0] = (seg("mq") * ML_DK ** -0.5).astype(BF16)
    mk_ref[0] = seg("mk").astype(BF16)
    mv_ref[0] = seg("mv").astype(BF16)
    gates_ref[0] = seg("gates") + bg_ref[...]
    mo_ref[0] = seg("mo")
    ga_ref[0] = seg("ga")
    gm_ref[0] = seg("gm")


def _project(x, mod, g_pre, w_in, b_gates, mod_row, rope, with_nat, tm):
    bsz, seq, d = x.shape
    wp, bg, segs = _pack_w_in(w_in, b_gates, rope, with_nat)
    nw = wp.shape[1]
    tok3 = lambda b, i: (b, i, 0)
    in_specs = [pl.BlockSpec((1, tm, d), tok3),
                pl.BlockSpec((1, 6, d), lambda b, i: (mod_row(b), 0, 0)),
                _const_spec((1, d)),
                pl.BlockSpec((d, nw), lambda b, i: (0, 0), pipeline_mode=pl.Buffered(1)),
                _const_spec((1, LANES))]
    args = [x, mod, g_pre.reshape(1, d), wp, bg]
    if rope:
        cos_t, sin_t = _rope_tables(seq)
        in_specs += [pl.BlockSpec((tm, LANES), lambda b, i: (i, 0))] * 2
        args += [cos_t, sin_t]

    def tok_out(width, dtype):
        return (jax.ShapeDtypeStruct((bsz, seq, width), dtype), pl.BlockSpec((1, tm, width), tok3))

    outs = [(jax.ShapeDtypeStruct((bsz, ATT_HEADS, seq, LANES), BF16),
             pl.BlockSpec((1, ATT_HEADS, tm, LANES), lambda b, i: (b, 0, i, 0))),
            tok_out(2 * LANES, BF16), tok_out(2 * LANES, BF16)]
    if with_nat:
        outs.append(tok_out(2 * KV_W, F32))
    outs += [tok_out(MLQK_W, BF16), tok_out(MLQK_W, BF16), tok_out(MLV_W, BF16), tok_out(LANES, F32),
             tok_out(MLV_W, F32), tok_out(D_MODEL, F32), tok_out(D_MODEL, F32)]
    res = pl.pallas_call(
        functools.partial(_proj_kernel, segs=segs, rope=rope, with_nat=with_nat),
        grid=(bsz, seq // tm),
        in_specs=in_specs,
        out_specs=[o[1] for o in outs],
        out_shape=[o[0] for o in outs],
        compiler_params=_cparams(("parallel", "parallel")),
        name="proj_rope" if rope else "proj_ctx",
    )(*args)
    names = ["q", "kd", "vd"] + (["nat"] if with_nat else []) + ["mq", "mk", "mv", "gates", "mo", "ga", "gm"]
    return dict(zip(names, res))


def _attn_kernel(*refs, window, n_blocks, n_ctx):
    it = iter(refs)
    sink_ref, q_ref = next(it), next(it)
    if window:
        kp_ref, kc_ref, kn_ref, vp_ref, vc_ref, vn_ref = (next(it) for _ in range(6))
    ck_ref, cv_ref, o_ref = next(it), next(it), next(it)
    j = pl.program_id(1)
    n_win = 3 * BLOCK if window else 0
    n_keys = n_win + n_ctx
    rows = ATT_GROUP * BLOCK

    row_id = lax.broadcasted_iota(jnp.int32, (rows, 1), 0)
    if window:
        rr = lax.broadcasted_iota(jnp.int32, (rows, n_keys), 0) & (BLOCK - 1)
        cc = lax.broadcasted_iota(jnp.int32, (rows, n_keys), 1)
        dd = cc - rr
        lo_b = jnp.where(j == 0, BLOCK, 0)
        hi_b = jnp.where(j == n_blocks - 1, 2 * BLOCK - 1, 3 * BLOCK - 1)
        in_band = (dd >= 0) & (dd <= 2 * BLOCK) & (cc >= lo_b) & (cc <= hi_b)
        mask = in_band | (cc >= n_win)
    lane_lo = lax.broadcasted_iota(jnp.int32, (BLOCK, LANES), 1) < HEAD_DIM

    for kvh in range(ATT_KV_HEADS):
        sl = slice(kvh * LANES, (kvh + 1) * LANES)
        if window:
            keys = jnp.concatenate([kp_ref[0, :, sl], kc_ref[0, :, sl], kn_ref[0, :, sl], ck_ref[0, :, sl]], axis=0)
            vals = jnp.concatenate([vp_ref[0, :, sl], vc_ref[0, :, sl], vn_ref[0, :, sl], cv_ref[0, :, sl]], axis=0)
        else:
            keys = ck_ref[0, :, sl]
            vals = cv_ref[0, :, sl]
        qs = q_ref[0, kvh * ATT_GROUP:(kvh + 1) * ATT_GROUP].reshape(rows, LANES)
        s = lax.dot_general(qs, keys, (((1,), (1,)), ((), ())), preferred_element_type=F32)
        if window:
            s = jnp.where(mask, s, NEG)
        snk = jnp.full((rows, 1), sink_ref[kvh * ATT_GROUP + ATT_GROUP - 1], F32)
        for g in range(ATT_GROUP - 2, -1, -1):
            snk = jnp.where(row_id < (g + 1) * BLOCK, sink_ref[kvh * ATT_GROUP + g], snk)
        m = jnp.maximum(jnp.max(s, axis=-1, keepdims=True), snk)
        p = jnp.exp(s - m)
        den = jnp.sum(p, axis=-1, keepdims=True) + jnp.exp(snk - m)
        o = jnp.dot(p.astype(BF16), vals, preferred_element_type=F32) / den
        for pair in range(ATT_GROUP // 2):
            even = o[(2 * pair) * BLOCK:(2 * pair + 1) * BLOCK]
            odd = o[(2 * pair + 1) * BLOCK:(2 * pair + 2) * BLOCK]
            col = (kvh * (ATT_GROUP // 2) + pair) * LANES
            o_ref[0, :, col:col + LANES] = jnp.where(lane_lo, even, odd).astype(BF16)


def _attention(q, kd, vd, ckd, cvd, sink, window):
    bsz, _, seq, _ = q.shape
    nb = seq // BLOCK
    n_ctx = ckd.shape[1]
    in_specs = [pl.BlockSpec(memory_space=pltpu.SMEM),
                pl.BlockSpec((1, ATT_HEADS, BLOCK, LANES), lambda b, j: (b, 0, j, 0))]
    args = [sink.astype(F32), q]
    if window:
        prev = lambda b, j: (b, jnp.maximum(j - 1, 0), 0)
        cur = lambda b, j: (b, j, 0)
        nxt = lambda b, j: (b, jnp.minimum(j + 1, nb - 1), 0)
        for arr in (kd, vd):
            in_specs += [pl.BlockSpec((1, BLOCK, 2 * LANES), f) for f in (prev, cur, nxt)]
            args += [arr, arr, arr]
    in_specs += [pl.BlockSpec((1, n_ctx, 2 * LANES), lambda b, j: (b, 0, 0))] * 2
    args += [ckd, cvd]
    return pl.pallas_call(
        functools.partial(_attn_kernel, window=window, n_blocks=nb, n_ctx=n_ctx),
        grid=(bsz, nb),
        in_specs=in_specs,
        out_specs=pl.BlockSpec((1, BLOCK, Q_W), lambda b, j: (b, j, 0)),
        out_shape=jax.ShapeDtypeStruct((bsz, seq, Q_W), BF16),
        compiler_params=_cparams(("parallel", "parallel")),
        name="attn_window" if window else "attn_ctx",
    )(*args)


def _dup_heads(t):
    b, l = t.shape[:2]
    return jnp.concatenate([t, t], axis=-1).reshape(b, l, ATT_KV_HEADS * LANES).astype(BF16)


def _split3(x):
    hi = x.astype(BF16)
    r1 = x - hi.astype(F32)
    mid = r1.astype(BF16)
    lo = (r1 - mid.astype(F32)).astype(BF16)
    return hi, mid, lo


def _log_sigmoid(x):
    return jnp.minimum(x, 0.0) - jnp.log1p(jnp.exp(-jnp.abs(x)))


def _mlstm_kernel(*refs, n_chunks, has_init, emit_state):
    it = iter(refs)
    q_ref, k_ref, v_ref, g_ref = (next(it) for _ in range(4))
    if has_init:
        c0_ref, n0_ref, m0_ref = next(it), next(it), next(it)
    h_ref = next(it)
    if emit_state:
        co_ref, no_ref, mo_ref = next(it), next(it), next(it)
    ct_s, n_s, m_s = next(it), next(it), next(it)
    d = pl.program_id(1)
    t = pl.program_id(2)
    L = ML_CHUNK

    @pl.when(t == 0)
    def _():
        for h in range(ML_HEADS):
            if has_init:
                ct_s[h] = c0_ref[0, 0, h].T
                n_s[h] = n0_ref[0, 0, h:h + 1, :]
                m_s[h] = m0_ref[0, 0, h:h + 1, :]
            else:
                ct_s[h] = jnp.zeros((ML_DK, ML_DV), F32)
                n_s[h] = jnp.zeros((1, ML_DK), F32)
                m_s[h] = jnp.zeros((1, LANES), F32)

    ri = lax.broadcasted_iota(jnp.int32, (L, L), 0)
    ci = lax.broadcasted_iota(jnp.int32, (L, L), 1)
    tri = (ri - ci) * jnp.where(d == 0, 1, -1) >= 0
    tri_b = jnp.where(tri, 1.0, 0.0).astype(BF16)

    g = g_ref[0]
    g = jnp.where(d == 0, g, pltpu.roll(g, LANES - 2 * ML_HEADS, 1))
    lf = _log_sigmoid(g)
    g_t = g.T
    lf_t = lf.T
    bc = sum(jnp.dot(tri_b, part, preferred_element_type=F32) for part in _split3(lf))
    br = sum(lax.dot_general(part, tri_b, (((1,), (1,)), ((), ())), preferred_element_type=F32)
             for part in _split3(lf_t))
    tot = jnp.sum(lf, axis=0, keepdims=True)

    for h in range(ML_HEADS):
        fcol = ML_HEADS + h
        bcol = bc[:, fcol:fcol + 1]
        brow = br[fcol:fcol + 1, :]
        irow = g_t[h:h + 1, :]
        icol = g[:, h:h + 1]
        b_last = tot[:, fcol:fcol + 1]
        m_prev = m_s[h][:, 0:1]
        n_prev = n_s[h]
        ct_prev = ct_s[h]
        qh = q_ref[0, :, h * ML_DK:(h + 1) * ML_DK]
        kh = k_ref[0, :, h * ML_DK:(h + 1) * ML_DK]
        vh = v_ref[0, :, h * ML_DV:(h + 1) * ML_DV]

        dmat = jnp.where(tri, bcol - brow + irow, NEG)
        inter = bcol + m_prev
        m_t = jnp.maximum(inter, jnp.max(dmat, axis=-1, keepdims=True))
        w = jnp.exp(dmat - m_t)
        w_inter = jnp.exp(inter - m_t)
        s = lax.dot_general(qh, kh, (((1,), (1,)), ((), ())), preferred_element_type=F32) * w
        cq = jnp.dot(qh, ct_prev.astype(BF16), preferred_element_type=F32) * w_inter
        num = jnp.dot(s.astype(BF16), vh, preferred_element_type=F32) + cq
        qn = jnp.sum(qh.astype(F32) * n_prev, axis=-1, keepdims=True)
        den = jnp.sum(s, axis=-1, keepdims=True) + w_inter * qn
        h_ref[0, 0, :, h * ML_DV:(h + 1) * ML_DV] = num / jnp.maximum(jnp.abs(den), jnp.exp(-m_t))

        a = b_last - bcol + icol
        m_new = jnp.maximum(b_last + m_prev, jnp.max(a, axis=0, keepdims=True))
        wk = jnp.exp(a - m_new)
        decay = jnp.exp(b_last + m_prev - m_new)
        kw = kh.astype(F32) * wk
        ct_new = decay * ct_prev + jnp.dot(kw.T.astype(BF16), vh, preferred_element_type=F32)
        n_new = decay * n_prev + jnp.sum(kw, axis=0, keepdims=True)
        ct_s[h] = ct_new
        n_s[h] = n_new
        m_s[h] = jnp.broadcast_to(m_new, (1, LANES))
        if emit_state:
            @pl.when(t == n_chunks - 1)
            def _():
                co_ref[0, 0, h] = ct_new.T
                no_ref[0, 0, h:h + 1, :] = n_new
                mo_ref[0, 0, h:h + 1, :] = jnp.broadcast_to(m_new, (1, LANES))


def _mlstm(mq, mk, mv, gates, init, emit_state):
    bsz, seq, _ = mq.shape
    nc = seq // ML_CHUNK
    chunk = lambda b, d, t: (b, jnp.where(d == 0, t, nc - 1 - t), 0)
    state5 = lambda b, d, t: (b, d, 0, 0, 0)
    state4 = lambda b, d, t: (b, d, 0, 0)
    in_specs = [pl.BlockSpec((1, ML_CHUNK, MLQK_W), chunk), pl.BlockSpec((1, ML_CHUNK, MLQK_W), chunk),
                pl.BlockSpec((1, ML_CHUNK, MLV_W), chunk), pl.BlockSpec((1, ML_CHUNK, LANES), chunk)]
    args = [mq, mk, mv, gates]
    if init is not None:
        c0, n0, m0 = init
        in_specs += [pl.BlockSpec((1, 1, ML_HEADS, ML_DV, ML_DK), state5),
                     pl.BlockSpec((1, 1, ML_HEADS, ML_DK), state4),
                     pl.BlockSpec((1, 1, ML_HEADS, LANES), state4)]
        args += [c0.astype(F32), n0.astype(F32),
                 jnp.broadcast_to(m0.astype(F32)[..., None], m0.shape + (LANES,))]
    out_shape = [jax.ShapeDtypeStruct((2, bsz, seq, MLV_W), F32)]
    out_specs = [pl.BlockSpec((1, 1, ML_CHUNK, MLV_W),
                              lambda b, d, t: (d, b, jnp.where(d == 0, t, nc - 1 - t), 0))]
    if emit_state:
        out_shape += [jax.ShapeDtypeStruct((bsz, 2, ML_HEADS, ML_DV, ML_DK), F32),
                      jax.ShapeDtypeStruct((bsz, 2, ML_HEADS, ML_DK), F32),
                      jax.ShapeDtypeStruct((bsz, 2, ML_HEADS, LANES), F32)]
        out_specs += [pl.BlockSpec((1, 1, ML_HEADS, ML_DV, ML_DK), state5),
                      pl.BlockSpec((1, 1, ML_HEADS, ML_DK), state4),
                      pl.BlockSpec((1, 1, ML_HEADS, LANES), state4)]
    return pl.pallas_call(
        functools.partial(_mlstm_kernel, n_chunks=nc, has_init=init is not None, emit_state=emit_state),
        grid=(bsz, 2, nc),
        in_specs=in_specs,
        out_specs=out_specs,
        out_shape=out_shape,
        scratch_shapes=[pltpu.VMEM((ML_HEADS, ML_DK, ML_DV), F32),
                        pltpu.VMEM((ML_HEADS, 1, ML_DK), F32),
                        pltpu.VMEM((ML_HEADS, 1, LANES), F32)],
        compiler_params=_cparams(("parallel", "parallel", "arbitrary")),
        name="mlstm_state" if emit_state else "mlstm",
    )(*args)


def _rms(x, g):
    return x * lax.rsqrt(jnp.mean(x * x, axis=-1, keepdims=True) + EPS) * g


def _pack_bf16_pairs(x):
    n = x.shape[1] // 2
    lo = pltpu.bitcast(x[:, :n].astype(BF16).astype(F32), jnp.int32)
    hi = pltpu.bitcast(x[:, n:].astype(BF16).astype(F32), jnp.int32)
    return lax.shift_right_logical(lo, 16) | hi


def _unpack_bf16_pairs(p):
    lo = pltpu.bitcast(lax.shift_left(p, 16), F32)
    hi = pltpu.bitcast(p & jnp.int32(-65536), F32)
    return lo, hi


def _merge_kernel(att_ref, hf_ref, hb_ref, mo_ref, ga_ref, gm_ref, x_ref, mod_ref, nml_ref, gpm_ref, gpf_ref,
                  wua_ref, wum_ref, wo_ref, wr_ref, br_ref, x1_ref, h2_ref, idx_ref, gate_ref):
    hsum = hf_ref[0, 0] + hb_ref[0, 0]
    parts = []
    for h in range(ML_HEADS):
        hh = hsum[:, h * ML_DV:(h + 1) * ML_DV]
        parts.append(hh * lax.rsqrt(jnp.mean(hh * hh, axis=-1, keepdims=True) + EPS))
    hn = jnp.concatenate(parts, axis=-1) * nml_ref[...]
    ml = (hn * jax.nn.sigmoid(mo_ref[0])).astype(BF16)
    a = jnp.dot(att_ref[0], wua_ref[...], preferred_element_type=F32)
    m = jnp.dot(ml, wum_ref[...], preferred_element_type=F32)
    z = (jax.nn.sigmoid(ga_ref[0]) * a + jax.nn.sigmoid(gm_ref[0]) * m).astype(BF16)
    mix = jnp.dot(z, wo_ref[...], preferred_element_type=F32)
    gate1 = mod_ref[0, 2:3, :]
    shift2 = mod_ref[0, 3:4, :]
    scale2 = mod_ref[0, 4:5, :]
    x1 = x_ref[0] + gate1 * _rms(mix, gpm_ref[...])
    x1_ref[0] = x1
    h2 = _rms(x1, gpf_ref[...]) * (1.0 + scale2) + shift2
    h2_ref[0] = _pack_bf16_pairs(h2)

    h2_hi = h2.astype(BF16)
    h2_lo = (h2 - h2_hi.astype(F32)).astype(BF16)
    wr = wr_ref[...]
    wr_hi = wr.astype(BF16)
    wr_lo = (wr - wr_hi.astype(F32)).astype(BF16)
    logits = (jnp.dot(h2_hi, wr_hi, preferred_element_type=F32)
              + jnp.dot(h2_lo, wr_hi, preferred_element_type=F32)
              + jnp.dot(h2_hi, wr_lo, preferred_element_type=F32)) + br_ref[...]
    lane = lax.broadcasted_iota(jnp.int32, logits.shape, 1).astype(F32)
    work = jnp.where(lane < N_EXPERTS, logits, -jnp.inf)
    idx_out = jnp.zeros(logits.shape, F32)
    val_out = jnp.zeros(logits.shape, F32)
    top0 = None
    esum = None
    for k in range(TOP_K):
        mx = jnp.max(work, axis=-1, keepdims=True)
        sel = jnp.min(jnp.where(work == mx, lane, float(LANES)), axis=-1, keepdims=True)
        if k == 0:
            top0 = mx
        e = jnp.exp(mx - top0)
        esum = e if k == 0 else esum + e
        idx_out = jnp.where(lane == k, sel, idx_out)
        val_out = jnp.where(lane == k, e, val_out)
        work = jnp.where(lane == sel, -jnp.inf, work)
    idx_ref[0] = idx_out.astype(jnp.int32)
    gate_ref[0] = val_out / esum


def _merge(att, h, pj, x, mod, mod_row, p, tm):
    bsz, seq, d = x.shape
    tok3 = lambda b, i: (b, i, 0)
    wr = jnp.pad(p["w_router"].astype(F32), ((0, 0), (0, LANES - N_EXPERTS)))
    br = jnp.pad(p["b_router"].astype(F32), (0, LANES - N_EXPERTS)).reshape(1, LANES)
    row = lambda v: v.astype(F32).reshape(1, -1)
    in_specs = [pl.BlockSpec((1, tm, Q_W), tok3),
                pl.BlockSpec((1, 1, tm, MLV_W), lambda b, i: (0, b, i, 0)),
                pl.BlockSpec((1, 1, tm, MLV_W), lambda b, i: (1, b, i, 0)),
                pl.BlockSpec((1, tm, MLV_W), tok3), pl.BlockSpec((1, tm, d), tok3), pl.BlockSpec((1, tm, d), tok3),
                pl.BlockSpec((1, tm, d), tok3),
                pl.BlockSpec((1, 6, d), lambda b, i: (mod_row(b), 0, 0)),
                _const_spec((1, MLV_W)), _const_spec((1, d)), _const_spec((1, d)),
                _const_spec((Q_W, d)), _const_spec((MLV_W, d)), _const_spec((d, d)),
                _const_spec((d, LANES)), _const_spec((1, LANES))]
    out_shape = [jax.ShapeDtypeStruct((bsz, seq, d), F32), jax.ShapeDtypeStruct((bsz, seq, d // 2), jnp.int32),
                 jax.ShapeDtypeStruct((bsz, seq, LANES), jnp.int32), jax.ShapeDtypeStruct((bsz, seq, LANES), F32)]
    out_specs = [pl.BlockSpec((1, tm, d), tok3), pl.BlockSpec((1, tm, d // 2), tok3),
                 pl.BlockSpec((1, tm, LANES), tok3), pl.BlockSpec((1, tm, LANES), tok3)]
    return pl.pallas_call(
        _merge_kernel,
        grid=(bsz, seq // tm),
        in_specs=in_specs,
        out_specs=out_specs,
        out_shape=out_shape,
        compiler_params=_cparams(("parallel", "parallel")),
        name="merge_router",
    )(att, h, h, pj["mo"], pj["ga"], pj["gm"], x, mod, row(p["norm_ml"]), row(p["g_post_mix"]),
      row(p["g_pre_ffn"]), p["w_up_att"].astype(BF16), p["w_up_ml"].astype(BF16), p["w_out"].astype(BF16), wr, br)


def _sc_gather_rows(table, idx):
    n = idx.shape[0]
    width = table.shape[1]
    chunk = SC_GATHER_CHUNK
    n_workers = SC_CORES * SC_SUBCORES
    rows_per_worker = n // n_workers
    n_pairs = rows_per_worker // (2 * chunk)
    assert n_pairs * 2 * chunk * n_workers == n
    mesh = plsc.VectorSubcoreMesh(core_axis_name="c", subcore_axis_name="s")

    def body(table_hbm, idx_hbm, out_hbm, idx_v, rows_v, gsem, osem):
        wid = lax.axis_index("s") * SC_CORES + lax.axis_index("c")
        base0 = wid * rows_per_worker

        def gather_copy(b):
            return pltpu.make_async_copy(table_hbm.at[idx_v.at[b]], rows_v.at[b], gsem.at[b])

        def write_copy(ci, b):
            base = pl.multiple_of(base0 + ci * chunk, 8)
            return pltpu.make_async_copy(rows_v.at[b], out_hbm.at[pl.ds(base, chunk)], osem.at[b])

        def issue(ci, b):
            base = pl.multiple_of(base0 + ci * chunk, 8)
            pltpu.sync_copy(idx_hbm.at[pl.ds(base, chunk)], idx_v.at[b])
            gather_copy(b).start()

        def finish(ci, b):
            gather_copy(b).wait()
            write_copy(ci, b).start()

        issue(0, 0)

        @pl.loop(0, n_pairs)
        def _(j):
            @pl.when(j > 0)
            def _():
                write_copy(2 * j - 1, 1).wait()
            issue(2 * j + 1, 1)
            finish(2 * j, 0)

            @pl.when(j < n_pairs - 1)
            def _():
                write_copy(2 * j, 0).wait()
                issue(2 * j + 2, 0)
            finish(2 * j + 1, 1)

        write_copy(2 * n_pairs - 2, 0).wait()
        write_copy(2 * n_pairs - 1, 1).wait()

    return pl.kernel(
        body, mesh=mesh,
        out_type=jax.ShapeDtypeStruct((n, width), table.dtype),
        scratch_types=[pltpu.VMEM((2, chunk), jnp.int32), pltpu.VMEM((2, chunk, width), table.dtype),
                       pltpu.SemaphoreType.DMA((2,)), pltpu.SemaphoreType.DMA((2,))],
    )(table, idx)


def _ffn_kernel(be_ref, nu_ref, xs_ref, w1_ref, b1_ref, w2_ref, b2_ref, ys_ref, w1b_s, w2b_s):
    i = pl.program_id(0)
    e = be_ref[i]
    e_prev = be_ref[jnp.maximum(i - 1, 0)]

    @pl.when((i == 0) | (e != e_prev))
    def _():
        w1b_s[...] = w1_ref[0].astype(BF16)
        w2b_s[...] = w2_ref[0].astype(BF16)

    @pl.when(i < nu_ref[0])
    def _():
        half = D_MODEL // 2
        x_lo, x_hi = _unpack_bf16_pairs(xs_ref[...])
        hmid = (jnp.dot(x_lo.astype(BF16), w1b_s[:half, :], preferred_element_type=F32)
                + jnp.dot(x_hi.astype(BF16), w1b_s[half:, :], preferred_element_type=F32)) + b1_ref[0]
        glu = jnp.minimum(hmid[:, :D_FF], SWIGLU_LIMIT)
        lin = jnp.clip(hmid[:, D_FF:], -SWIGLU_LIMIT, SWIGLU_LIMIT)
        act = ((lin + 1.0) * glu * jax.nn.sigmoid(SWIGLU_ALPHA * glu)).astype(BF16)
        y = jnp.dot(act, w2b_s[...], preferred_element_type=F32) + b2_ref[0]
        ys_ref[...] = _pack_bf16_pairs(y)

    @pl.when(i >= nu_ref[0])
    def _():
        ys_ref[...] = jnp.zeros(ys_ref.shape, jnp.int32)


def _route(idx, n_tok):
    n_assign = n_tok * TOP_K
    n_blocks = n_assign // MOE_ROWS + N_EXPERTS
    n_rows = n_blocks * MOE_ROWS
    flat_e = idx.reshape(-1).astype(jnp.int32)
    order = jnp.argsort(flat_e, stable=True).astype(jnp.int32)
    rank = jnp.argsort(order).astype(jnp.int32)
    counts = jnp.sum(flat_e[:, None] == jnp.arange(N_EXPERTS, dtype=jnp.int32)[None, :], axis=0).astype(jnp.int32)
    start = jnp.cumsum(counts) - counts
    padded = (counts + MOE_ROWS - 1) // MOE_ROWS * MOE_ROWS
    pend = jnp.cumsum(padded)
    pstart = pend - padded
    block_e = jnp.minimum(jnp.searchsorted(pend, jnp.arange(n_blocks, dtype=jnp.int32) * MOE_ROWS, side="right"),
                          N_EXPERTS - 1).astype(jnp.int32)
    n_used = (pend[-1] // MOE_ROWS).astype(jnp.int32).reshape(1)
    pos = pstart[flat_e] + rank - start[flat_e]
    pos_slot_major = pos.reshape(n_tok, TOP_K).T.reshape(-1).astype(jnp.int32)
    r = jnp.arange(n_rows, dtype=jnp.int32)
    e_r = block_e[r // MOE_ROWS]
    off = r - pstart[e_r]
    src = order[jnp.clip(start[e_r] + off, 0, n_assign - 1)] // TOP_K
    row_tok = jnp.where(off < counts[e_r], src, 0).astype(jnp.int32)
    return block_e, n_used, row_tok, pos_slot_major


def _moe(h2p, idx, w1, b1, w2, b2):
    n_tok, half = h2p.shape
    d = 2 * half
    block_e, n_used, row_tok, pos = _route(idx, n_tok)
    n_blocks = block_e.shape[0]
    xs = _sc_gather_rows(h2p, row_tok)
    grid_spec = pltpu.PrefetchScalarGridSpec(
        num_scalar_prefetch=2,
        grid=(n_blocks,),
        in_specs=[pl.BlockSpec((MOE_ROWS, half), lambda i, be, nu: (i, 0)),
                  pl.BlockSpec((1, d, 2 * D_FF), lambda i, be, nu: (be[i], 0, 0)),
                  pl.BlockSpec((1, 1, 2 * D_FF), lambda i, be, nu: (be[i], 0, 0)),
                  pl.BlockSpec((1, D_FF, d), lambda i, be, nu: (be[i], 0, 0)),
                  pl.BlockSpec((1, 1, d), lambda i, be, nu: (be[i], 0, 0))],
        out_specs=pl.BlockSpec((MOE_ROWS, half), lambda i, be, nu: (i, 0)),
        scratch_shapes=[pltpu.VMEM((d, 2 * D_FF), BF16), pltpu.VMEM((D_FF, d), BF16)],
    )
    ys = pl.pallas_call(
        _ffn_kernel,
        grid_spec=grid_spec,
        out_shape=jax.ShapeDtypeStruct((n_blocks * MOE_ROWS, half), jnp.int32),
        compiler_params=_cparams(("arbitrary",)),
        name="moe_ffn",
    )(block_e, n_used, xs, w1, b1.reshape(N_EXPERTS, 1, -1), w2, b2.reshape(N_EXPERTS, 1, -1))
    return _sc_gather_rows(ys, pos)


def _final_kernel(x1_ref, y0_ref, y1_ref, y2_ref, y3_ref, gate_ref, mod_ref, g_ref, o_ref):
    gates = gate_ref[0]
    lo = hi = None
    for k, y_ref in enumerate((y0_ref, y1_ref, y2_ref, y3_ref)):
        y_lo, y_hi = _unpack_bf16_pairs(y_ref[0, 0])
        gk = gates[:, k:k + 1]
        lo = gk * y_lo if k == 0 else lo + gk * y_lo
        hi = gk * y_hi if k == 0 else hi + gk * y_hi
    y = jnp.concatenate([lo, hi], axis=-1)
    gate2 = mod_ref[0, 5:6, :]
    o_ref[0] = x1_ref[0] + gate2 * _rms(y, g_ref[...])


def _final(x1, yg, gate, mod, mod_row, g_post_ffn, tm):
    bsz, seq, d = x1.shape
    tok3 = lambda b, i: (b, i, 0)
    yg = yg.reshape(TOP_K, bsz, seq, d // 2)
    slot_specs = [pl.BlockSpec((1, 1, tm, d // 2), functools.partial(lambda b, i, k: (k, b, i, 0), k=k))
                  for k in range(TOP_K)]
    return pl.pallas_call(
        _final_kernel,
        grid=(bsz, seq // tm),
        in_specs=[pl.BlockSpec((1, tm, d), tok3)] + slot_specs +
                 [pl.BlockSpec((1, tm, LANES), tok3),
                  pl.BlockSpec((1, 6, d), lambda b, i: (mod_row(b), 0, 0)), _const_spec((1, d))],
        out_specs=pl.BlockSpec((1, tm, d), tok3),
        out_shape=jax.ShapeDtypeStruct((bsz, seq, d), F32),
        compiler_params=_cparams(("parallel", "parallel")),
        name="final_residual",
    )(x1, yg, yg, yg, yg, gate, mod, g_post_ffn.astype(F32).reshape(1, d))


def _stream(x, mod, mod_row, p, ctx_kv, init_state, rope, tm):
    bsz, seq, d = x.shape
    is_ctx = ctx_kv is None
    pj = _project(x, mod, p["g_pre_mix"], p["w_in"], p["b_gates"], mod_row, rope, is_ctx, tm)
    if is_ctx:
        att = _attention(pj["q"], None, None, pj["kd"], pj["vd"], p["sink"], window=False)
    else:
        att = _attention(pj["q"], pj["kd"], pj["vd"], ctx_kv[0], ctx_kv[1], p["sink"], window=True)
    ml = _mlstm(pj["mq"], pj["mk"], pj["mv"], pj["gates"], init_state, emit_state=is_ctx)
    x1, h2, idx, gate = _merge(att, ml[0], pj, x, mod, mod_row, p, tm)
    yg = _moe(h2.reshape(bsz * seq, d // 2), idx[..., :TOP_K], p["w1"], p["b1"].astype(F32), p["w2"],
              p["b2"].astype(F32))
    out = _final(x1, yg, gate, mod, mod_row, p["g_post_ffn"], tm)
    return out, pj, ml


def kernel(x_prompt, x_sample, c, cache_k, cache_v, state_C, state_n, state_m, c_ctx, w_ada, b_ada, g_pre_mix,
           w_in, b_gates, attn_sink, norm_mlstm, w_up_att, w_up_ml, w_out, g_post_mix, g_pre_ffn, w_router,
           b_router, w1, b1, w2, b2, g_post_ffn):
    depth = w_ada.shape[0]
    n_dec = c.shape[0]
    cond = jnp.concatenate([c_ctx[None, :], c], axis=0).astype(F32)
    cond = jnp.pad(cond, ((0, 16 - cond.shape[0]), (0, 0)))
    y_prompt, y_sample = x_prompt, x_sample
    ks_, vs_, cs_, ns_, ms_ = [], [], [], [], []
    for l in range(depth):
        p = dict(g_pre_mix=g_pre_mix[l], w_in=w_in[l], b_gates=b_gates[l], sink=attn_sink[l], norm_ml=norm_mlstm[l],
                 w_up_att=w_up_att[l], w_up_ml=w_up_ml[l], w_out=w_out[l], g_post_mix=g_post_mix[l],
                 g_pre_ffn=g_pre_ffn[l], w_router=w_router[l], b_router=b_router[l], w1=w1[l], b1=b1[l],
                 w2=w2[l], b2=b2[l], g_post_ffn=g_post_ffn[l])
        mod = _adaln(cond, w_ada[l], b_ada[l]).reshape(16, 6, D_MODEL)
        y_prompt, pj, ml = _stream(y_prompt, mod, lambda b: 0, p, None, None, rope=False, tm=256)
        bsz, seq = x_prompt.shape[:2]
        nat = pj["nat"]
        ks_.append(nat[..., :KV_W].reshape(bsz, seq, ATT_KV_HEADS, HEAD_DIM))
        vs_.append(nat[..., KV_W:].reshape(bsz, seq, ATT_KV_HEADS, HEAD_DIM))
        cs_.append(ml[1])
        ns_.append(ml[2])
        ms_.append(ml[3][..., 0])
        ctx_kv = (_dup_heads(cache_k[:, l]), _dup_heads(cache_v[:, l]))
        init = (state_C[:, l], state_n[:, l], state_m[:, l])
        y_sample, _, _ = _stream(y_sample, mod, lambda b: b + 1, p, ctx_kv, init, rope=True, tm=512)
    return (y_prompt, y_sample, jnp.stack(ks_, axis=1), jnp.stack(vs_, axis=1), jnp.stack(cs_, axis=1),
            jnp.stack(ns_, axis=1), jnp.stack(ms_, axis=1))
```

```python
import functools

import numpy as np
import jax
import jax.numpy as jnp
from jax import lax
from jax.experimental import pallas as pl
from jax.experimental.pallas import tpu as pltpu
from jax.experimental.pallas import tpu_sc as plsc

F32 = jnp.float32
BF16 = jnp.bfloat16

D_MODEL = 1024
GRID_W = 64
ATT_HEADS = 8
ATT_KV_HEADS = 2
ATT_GROUP = ATT_HEADS // ATT_KV_HEADS
HEAD_DIM = 64
BLOCK = 128
ROPE_THETA = 10000.0
AXIS_FREQS = HEAD_DIM // 4
ML_HEADS = 4
ML_DK = 128
ML_DV = 256
ML_CHUNK = 128
N_EXPERTS = 32
TOP_K = 4
D_FF = 1024
SWIGLU_ALPHA = 1.702
SWIGLU_LIMIT = 7.0
EPS = 1e-6
NEG = -1e30

Q_W = ATT_HEADS * HEAD_DIM
KV_W = ATT_KV_HEADS * HEAD_DIM
MLQK_W = ML_HEADS * ML_DK
MLV_W = ML_HEADS * ML_DV
GATE_W = 4 * ML_HEADS

LANES = 128
VMEM_LIMIT = 56 * 1024 * 1024
MOE_ROWS = 256
SC_CORES = 2
SC_SUBCORES = 16
SC_GATHER_CHUNK = 32
ROUTE_TILE = 256


def _cparams(sem):
    return pltpu.CompilerParams(dimension_semantics=sem, vmem_limit_bytes=VMEM_LIMIT)


def _const_spec(shape):
    nd = len(shape)
    return pl.BlockSpec(shape, lambda *_: (0,) * nd)


def _adaln_kernel(cond_ref, w_ref, b_ref, o_ref):
    cnd = cond_ref[...]
    act = cnd * jax.nn.sigmoid(cnd)
    o_ref[...] = jnp.dot(act.astype(BF16), w_ref[...].astype(BF16),
                         preferred_element_type=F32) + b_ref[...]


def _adaln(cond, w_ada, b_ada):
    rows, d = cond.shape
    n = w_ada.shape[1]
    tn = 1536
    return pl.pallas_call(
        _adaln_kernel,
        grid=(n // tn,),
        in_specs=[_const_spec((rows, d)),
                  pl.BlockSpec((d, tn), lambda j: (0, j)),
                  pl.BlockSpec((1, tn), lambda j: (0, j))],
        out_specs=pl.BlockSpec((rows, tn), lambda j: (0, j)),
        out_shape=jax.ShapeDtypeStruct((rows, n), F32),
        compiler_params=_cparams(("parallel",)),
        name="adaln",
    )(cond, w_ada, b_ada.reshape(1, n))


def _pack_w_in(w_in, b_gates, rope, with_nat):
    o_q, o_k, o_v = 0, Q_W, Q_W + KV_W
    o_mq = Q_W + 2 * KV_W
    o_mk = o_mq + MLQK_W
    o_mv = o_mk + MLQK_W
    o_g = o_mv + MLV_W
    o_mo = o_g + GATE_W
    o_ga = o_mo + MLV_W
    o_gm = o_ga + D_MODEL
    half = HEAD_DIM // 2
    zeros64 = jnp.zeros((D_MODEL, HEAD_DIM), w_in.dtype)

    def head_cols(base, h, rot):
        lo = base + h * HEAD_DIM
        if rot:
            return [w_in[:, lo + half:lo + HEAD_DIM], w_in[:, lo:lo + half]]
        return [w_in[:, lo:lo + HEAD_DIM]]

    def q_cols(rot):
        cols = []
        for h in range(ATT_HEADS):
            hc = head_cols(o_q, h, rot)
            cols += (hc + [zeros64]) if h % 2 == 0 else ([zeros64] + hc)
        return cols

    def dup_cols(base, rot):
        cols = []
        for h in range(ATT_KV_HEADS):
            hc = head_cols(base, h, rot)
            cols += hc + hc
        return cols

    segs, cols, off = {}, [], 0

    def add(name, cl):
        nonlocal off
        width = sum(c.shape[1] for c in cl)
        segs[name] = (off, off + width)
        cols.extend(cl)
        off += width

    add("q", q_cols(False))
    if rope:
        add("qrot", q_cols(True))
    add("kd", dup_cols(o_k, False))
    if rope:
        add("kdrot", dup_cols(o_k, True))
    add("vd", dup_cols(o_v, False))
    if with_nat:
        add("nat", [w_in[:, o_k:o_k + 2 * KV_W]])
    add("mq", [w_in[:, o_mq:o_mq + MLQK_W]])
    add("mk", [w_in[:, o_mk:o_mk + MLQK_W]])
    add("mv", [w_in[:, o_mv:o_mv + MLV_W]])
    add("mo", [w_in[:, o_mo:o_mo + MLV_W]])
    add("ga", [w_in[:, o_ga:o_ga + D_MODEL]])
    add("gm", [w_in[:, o_gm:o_gm + D_MODEL]])
    add("gates", [w_in[:, o_g:o_g + GATE_W], jnp.zeros((D_MODEL, LANES - GATE_W), w_in.dtype)])
    wp = jnp.concatenate(cols, axis=1).astype(BF16)
    bg = jnp.pad(b_gates.astype(F32), (0, LANES - GATE_W)).reshape(1, LANES)
    return wp, bg, segs


def _rope_tables(n_tok):
    rows = n_tok // GRID_W
    row = np.repeat(np.arange(rows), GRID_W).astype(np.float32)
    col = np.tile(np.arange(GRID_W), rows).astype(np.float32)
    inv = (np.float32(ROPE_THETA) ** (-np.arange(AXIS_FREQS, dtype=np.float32) / AXIS_FREQS)).astype(np.float32)
    ang = np.concatenate([row[:, None] * inv, col[:, None] * inv], axis=-1).astype(np.float32)
    cos, sin = np.cos(ang), np.sin(ang)
    c64 = np.concatenate([cos, cos], axis=-1)
    s64 = np.concatenate([-sin, sin], axis=-1)
    return (jnp.asarray(np.tile(c64, (1, 2)), F32), jnp.asarray(np.tile(s64, (1, 2)), F32))


def _proj_kernel(*refs, segs, rope, with_nat):
    it = iter(refs)
    x_ref, mod_ref, g_ref, w_ref, bg_ref = (next(it) for _ in range(5))
    cos_ref = sin_ref = None
    if rope:
        cos_ref, sin_ref = next(it), next(it)
    q_ref, kd_ref, vd_ref = next(it), next(it), next(it)
    nat_ref = next(it) if with_nat else None
    mq_ref, mk_ref, mv_ref, gates_ref, mo_ref, ga_ref, gm_ref = (next(it) for _ in range(7))

    x = x_ref[0]
    shift = mod_ref[0, 0:1, :]
    scale = mod_ref[0, 1:2, :]
    ms = jnp.mean(x * x, axis=-1, keepdims=True)
    xn = x * lax.rsqrt(ms + EPS) * g_ref[...]
    xb = (xn * (1.0 + scale) + shift).astype(BF16)

    def seg(name):
        lo, hi = segs[name]
        return jnp.dot(xb, w_ref[:, lo:hi], preferred_element_type=F32)

    uq = seg("q")
    ukd = seg("kd")
    if rope:
        uqr = seg("qrot")
        ukr = seg("kdrot")
        cs = cos_ref[...]
        sn = sin_ref[...]
        for h in range(ATT_HEADS):
            sl = slice(h * LANES, (h + 1) * LANES)
            q_ref[0, h] = ((uq[:, sl] * cs + uqr[:, sl] * sn) * HEAD_DIM ** -0.5).astype(BF16)
        for h in range(ATT_KV_HEADS):
            sl = slice(h * LANES, (h + 1) * LANES)
            kd_ref[0, :, sl] = (ukd[:, sl] * cs + ukr[:, sl] * sn).astype(BF16)
    else:
        for h in range(ATT_HEADS):
            sl = slice(h * LANES, (h + 1) * LANES)
            q_ref[0, h] = (uq[:, sl] * HEAD_DIM ** -0.5).astype(BF16)
        kd_ref[0] = ukd.astype(BF16)
    vd_ref[0] = seg("vd").astype(BF16)
    if with_nat:
        nat_ref[0] = seg("nat")
    mq_ref[0] = (seg("mq") * ML_DK ** -0.5).astype(BF16)
    mk_ref[0] = seg("mk").astype(BF16)
    mv_ref[0] = seg("mv").astype(BF16)
    gates_ref[0] = seg("gates") + bg_ref[...]
    mo_ref[0] = seg("mo")
    ga_ref[0] = seg("ga")
    gm_ref[0] = seg("gm")


def _project(x, mod, g_pre, w_in, b_gates, mod_row, rope, with_nat, tm):
    bsz, seq, d = x.shape
    wp, bg, segs = _pack_w_in(w_in, b_gates, rope, with_nat)
    nw = wp.shape[1]
    tok3 = lambda b, i: (b, i, 0)
    in_specs = [pl.BlockSpec((1, tm, d), tok3),
                pl.BlockSpec((1, 6, d), lambda b, i: (mod_row(b), 0, 0)),
                _const_spec((1, d)),
                pl.BlockSpec((d, nw), lambda b, i: (0, 0), pipeline_mode=pl.Buffered(1)),
                _const_spec((1, LANES))]
    args = [x, mod, g_pre.reshape(1, d), wp, bg]
    if rope:
        cos_t, sin_t = _rope_tables(seq)
        in_specs += [pl.BlockSpec((tm, LANES), lambda b, i: (i, 0))] * 2
        args += [cos_t, sin_t]

    def tok_out(width, dtype):
        return (jax.ShapeDtypeStruct((bsz, seq, width), dtype), pl.BlockSpec((1, tm, width), tok3))

    outs = [(jax.ShapeDtypeStruct((bsz, ATT_HEADS, seq, LANES), BF16),
             pl.BlockSpec((1, ATT_HEADS, tm, LANES), lambda b, i: (b, 0, i, 0))),
            tok_out(2 * LANES, BF16), tok_out(2 * LANES, BF16)]
    if with_nat:
        outs.append(tok_out(2 * KV_W, F32))
    outs += [tok_out(MLQK_W, BF16), tok_out(MLQK_W, BF16), tok_out(MLV_W, BF16), tok_out(LANES, F32),
             tok_out(MLV_W, F32), tok_out(D_MODEL, F32), tok_out(D_MODEL, F32)]
    res = pl.pallas_call(
        functools.partial(_proj_kernel, segs=segs, rope=rope, with_nat=with_nat),
        grid=(bsz, seq // tm),
        in_specs=in_specs,
        out_specs=[o[1] for o in outs],
        out_shape=[o[0] for o in outs],
        compiler_params=_cparams(("parallel", "parallel")),
        name="proj_rope" if rope else "proj_ctx",
    )(*args)
    names = ["q", "kd", "vd"] + (["nat"] if with_nat else []) + ["mq", "mk", "mv", "gates", "mo", "ga", "gm"]
    return dict(zip(names, res))


def _attn_kernel(*refs, window, n_blocks, n_ctx):
    it = iter(refs)
    sink_ref, q_ref = next(it), next(it)
    if window:
        kp_ref, kc_ref, kn_ref, vp_ref, vc_ref, vn_ref = (next(it) for _ in range(6))
    ck_ref, cv_ref, o_ref = next(it), next(it), next(it)
    j = pl.program_id(1)
    n_win = 3 * BLOCK if window else 0
    n_keys = n_win + n_ctx
    rows = ATT_GROUP * BLOCK

    row_id = lax.broadcasted_iota(jnp.int32, (rows, 1), 0)
    if window:
        rr = lax.broadcasted_iota(jnp.int32, (rows, n_keys), 0) & (BLOCK - 1)
        cc = lax.broadcasted_iota(jnp.int32, (rows, n_keys), 1)
        dd = cc - rr
        lo_b = jnp.where(j == 0, BLOCK, 0)
        hi_b = jnp.where(j == n_blocks - 1, 2 * BLOCK - 1, 3 * BLOCK - 1)
        in_band = (dd >= 0) & (dd <= 2 * BLOCK) & (cc >= lo_b) & (cc <= hi_b)
        mask = in_band | (cc >= n_win)
    lane_lo = lax.broadcasted_iota(jnp.int32, (BLOCK, LANES), 1) < HEAD_DIM

    for kvh in range(ATT_KV_HEADS):
        sl = slice(kvh * LANES, (kvh + 1) * LANES)
        if window:
            keys = jnp.concatenate([kp_ref[0, :, sl], kc_ref[0, :, sl], kn_ref[0, :, sl], ck_ref[0, :, sl]], axis=0)
            vals = jnp.concatenate([vp_ref[0, :, sl], vc_ref[0, :, sl], vn_ref[0, :, sl], cv_ref[0, :, sl]], axis=0)
        else:
            keys = ck_ref[0, :, sl]
            vals = cv_ref[0, :, sl]
        qs = q_ref[0, kvh * ATT_GROUP:(kvh + 1) * ATT_GROUP].reshape(rows, LANES)
        s = lax.dot_general(qs, keys, (((1,), (1,)), ((), ())), preferred_element_type=F32)
        if window:
            s = jnp.where(mask, s, NEG)
        snk = jnp.full((rows, 1), sink_ref[kvh * ATT_GROUP + ATT_GROUP - 1], F32)
        for g in range(ATT_GROUP - 2, -1, -1):
            snk = jnp.where(row_id < (g + 1) * BLOCK, sink_ref[kvh * ATT_GROUP + g], snk)
        m = jnp.maximum(jnp.max(s, axis=-1, keepdims=True), snk)
        p = jnp.exp(s - m)
        den = jnp.sum(p, axis=-1, keepdims=True) + jnp.exp(snk - m)
        o = jnp.dot(p.astype(BF16), vals, preferred_element_type=F32) / den
        for pair in range(ATT_GROUP // 2):
            even = o[(2 * pair) * BLOCK:(2 * pair + 1) * BLOCK]
            odd = o[(2 * pair + 1) * BLOCK:(2 * pair + 2) * BLOCK]
            col = (kvh * (ATT_GROUP // 2) + pair) * LANES
            o_ref[0, :, col:col + LANES] = jnp.where(lane_lo, even, odd).astype(BF16)


def _attention(q, kd, vd, ckd, cvd, sink, window):
    bsz, _, seq, _ = q.shape
    nb = seq // BLOCK
    n_ctx = ckd.shape[1]
    in_specs = [pl.BlockSpec(memory_space=pltpu.SMEM),
                pl.BlockSpec((1, ATT_HEADS, BLOCK, LANES), lambda b, j: (b, 0, j, 0))]
    args = [sink.astype(F32), q]
    if window:
        prev = lambda b, j: (b, jnp.maximum(j - 1, 0), 0)
        cur = lambda b, j: (b, j, 0)
        nxt = lambda b, j: (b, jnp.minimum(j + 1, nb - 1), 0)
        for arr in (kd, vd):
            in_specs += [pl.BlockSpec((1, BLOCK, 2 * LANES), f) for f in (prev, cur, nxt)]
            args += [arr, arr, arr]
    in_specs += [pl.BlockSpec((1, n_ctx, 2 * LANES), lambda b, j: (b, 0, 0))] * 2
    args += [ckd, cvd]
    return pl.pallas_call(
        functools.partial(_attn_kernel, window=window, n_blocks=nb, n_ctx=n_ctx),
        grid=(bsz, nb),
        in_specs=in_specs,
        out_specs=pl.BlockSpec((1, BLOCK, Q_W), lambda b, j: (b, j, 0)),
        out_shape=jax.ShapeDtypeStruct((bsz, seq, Q_W), BF16),
        compiler_params=_cparams(("parallel", "parallel")),
        name="attn_window" if window else "attn_ctx",
    )(*args)


def _dup_heads(t):
    b, l = t.shape[:2]
    return jnp.concatenate([t, t], axis=-1).reshape(b, l, ATT_KV_HEADS * LANES).astype(BF16)


def _split3(x):
    hi = x.astype(BF16)
    r1 = x - hi.astype(F32)
    mid = r1.astype(BF16)
    lo = (r1 - mid.astype(F32)).astype(BF16)
    return hi, mid, lo


def _log_sigmoid(x):
    return jnp.minimum(x, 0.0) - jnp.log1p(jnp.exp(-jnp.abs(x)))


def _mlstm_kernel(*refs, n_chunks, has_init, emit_state):
    it = iter(refs)
    q_ref, k_ref, v_ref, g_ref = (next(it) for _ in range(4))
    if has_init:
        c0_ref, n0_ref, m0_ref = next(it), next(it), next(it)
    h_ref = next(it)
    if emit_state:
        co_ref, no_ref, mo_ref = next(it), next(it), next(it)
    ct_s, n_s, m_s = next(it), next(it), next(it)
    d = pl.program_id(1)
    t = pl.program_id(2)
    L = ML_CHUNK

    @pl.when(t == 0)
    def _():
        for h in range(ML_HEADS):
            if has_init:
                ct_s[h] = c0_ref[0, 0, h].T
                n_s[h] = n0_ref[0, 0, h:h + 1, :]
                m_s[h] = m0_ref[0, 0, h:h + 1, :]
            else:
                ct_s[h] = jnp.zeros((ML_DK, ML_DV), F32)
                n_s[h] = jnp.zeros((1, ML_DK), F32)
                m_s[h] = jnp.zeros((1, LANES), F32)

    ri = lax.broadcasted_iota(jnp.int32, (L, L), 0)
    ci = lax.broadcasted_iota(jnp.int32, (L, L), 1)
    tri = (ri - ci) * jnp.where(d == 0, 1, -1) >= 0
    tri_b = jnp.where(tri, 1.0, 0.0).astype(BF16)

    g = g_ref[0]
    g = jnp.where(d == 0, g, pltpu.roll(g, LANES - 2 * ML_HEADS, 1))
    lf = _log_sigmoid(g)
    g_t = g.T
    lf_t = lf.T
    bc = sum(jnp.dot(tri_b, part, preferred_element_type=F32) for part in _split3(lf))
    br = sum(lax.dot_general(part, tri_b, (((1,), (1,)), ((), ())), preferred_element_type=F32)
             for part in _split3(lf_t))
    tot = jnp.sum(lf, axis=0, keepdims=True)

    for h in range(ML_HEADS):
        fcol = ML_HEADS + h
        bcol = bc[:, fcol:fcol + 1]
        brow = br[fcol:fcol + 1, :]
        irow = g_t[h:h + 1, :]
        icol = g[:, h:h + 1]
        b_last = tot[:, fcol:fcol + 1]
        m_prev = m_s[h][:, 0:1]
        n_prev = n_s[h]
        ct_prev = ct_s[h]
        qh = q_ref[0, :, h * ML_DK:(h + 1) * ML_DK]
        kh = k_ref[0, :, h * ML_DK:(h + 1) * ML_DK]
        vh = v_ref[0, :, h * ML_DV:(h + 1) * ML_DV]

        dmat = jnp.where(tri, bcol - brow + irow, NEG)
        inter = bcol + m_prev
        m_t = jnp.maximum(inter, jnp.max(dmat, axis=-1, keepdims=True))
        w = jnp.exp(dmat - m_t)
        w_inter = jnp.exp(inter - m_t)
        s = lax.dot_general(qh, kh, (((1,), (1,)), ((), ())), preferred_element_type=F32) * w
        cq = jnp.dot(qh, ct_prev.astype(BF16), preferred_element_type=F32) * w_inter
        num = jnp.dot(s.astype(BF16), vh, preferred_element_type=F32) + cq
        qn = jnp.sum(qh.astype(F32) * n_prev, axis=-1, keepdims=True)
        den = jnp.sum(s, axis=-1, keepdims=True) + w_inter * qn
        h_ref[0, 0, :, h * ML_DV:(h + 1) * ML_DV] = num / jnp.maximum(jnp.abs(den), jnp.exp(-m_t))

        a = b_last - bcol + icol
        m_new = jnp.maximum(b_last + m_prev, jnp.max(a, axis=0, keepdims=True))
        wk = jnp.exp(a - m_new)
        decay = jnp.exp(b_last + m_prev - m_new)
        kw = kh.astype(F32) * wk
        ct_new = decay * ct_prev + jnp.dot(kw.T.astype(BF16), vh, preferred_element_type=F32)
        n_new = decay * n_prev + jnp.sum(kw, axis=0, keepdims=True)
        ct_s[h] = ct_new
        n_s[h] = n_new
        m_s[h] = jnp.broadcast_to(m_new, (1, LANES))
        if emit_state:
            @pl.when(t == n_chunks - 1)
            def _():
                co_ref[0, 0, h] = ct_new.T
                no_ref[0, 0, h:h + 1, :] = n_new
                mo_ref[0, 0, h:h + 1, :] = jnp.broadcast_to(m_new, (1, LANES))


def _mlstm(mq, mk, mv, gates, init, emit_state):
    bsz, seq, _ = mq.shape
    nc = seq // ML_CHUNK
    chunk = lambda b, d, t: (b, jnp.where(d == 0, t, nc - 1 - t), 0)
    state5 = lambda b, d, t: (b, d, 0, 0, 0)
    state4 = lambda b, d, t: (b, d, 0, 0)
    in_specs = [pl.BlockSpec((1, ML_CHUNK, MLQK_W), chunk), pl.BlockSpec((1, ML_CHUNK, MLQK_W), chunk),
                pl.BlockSpec((1, ML_CHUNK, MLV_W), chunk), pl.BlockSpec((1, ML_CHUNK, LANES), chunk)]
    args = [mq, mk, mv, gates]
    if init is not None:
        c0, n0, m0 = init
        in_specs += [pl.BlockSpec((1, 1, ML_HEADS, ML_DV, ML_DK), state5),
                     pl.BlockSpec((1, 1, ML_HEADS, ML_DK), state4),
                     pl.BlockSpec((1, 1, ML_HEADS, LANES), state4)]
        args += [c0.astype(F32), n0.astype(F32),
                 jnp.broadcast_to(m0.astype(F32)[..., None], m0.shape + (LANES,))]
    out_shape = [jax.ShapeDtypeStruct((2, bsz, seq, MLV_W), F32)]
    out_specs = [pl.BlockSpec((1, 1, ML_CHUNK, MLV_W),
                              lambda b, d, t: (d, b, jnp.where(d == 0, t, nc - 1 - t), 0))]
    if emit_state:
        out_shape += [jax.ShapeDtypeStruct((bsz, 2, ML_HEADS, ML_DV, ML_DK), F32),
                      jax.ShapeDtypeStruct((bsz, 2, ML_HEADS, ML_DK), F32),
                      jax.ShapeDtypeStruct((bsz, 2, ML_HEADS, LANES), F32)]
        out_specs += [pl.BlockSpec((1, 1, ML_HEADS, ML_DV, ML_DK), state5),
                      pl.BlockSpec((1, 1, ML_HEADS, ML_DK), state4),
                      pl.BlockSpec((1, 1, ML_HEADS, LANES), state4)]
    return pl.pallas_call(
        functools.partial(_mlstm_kernel, n_chunks=nc, has_init=init is not None, emit_state=emit_state),
        grid=(bsz, 2, nc),
        in_specs=in_specs,
        out_specs=out_specs,
        out_shape=out_shape,
        scratch_shapes=[pltpu.VMEM((ML_HEADS, ML_DK, ML_DV), F32),
                        pltpu.VMEM((ML_HEADS, 1, ML_DK), F32),
                        pltpu.VMEM((ML_HEADS, 1, LANES), F32)],
        compiler_params=_cparams(("parallel", "parallel", "arbitrary")),
        name="mlstm_state" if emit_state else "mlstm",
    )(*args)


def _rms(x, g):
    return x * lax.rsqrt(jnp.mean(x * x, axis=-1, keepdims=True) + EPS) * g


def _pack_bf16_pairs(x):
    n = x.shape[1] // 2
    lo = pltpu.bitcast(x[:, :n].astype(BF16).astype(F32), jnp.int32)
    hi = pltpu.bitcast(x[:, n:].astype(BF16).astype(F32), jnp.int32)
    return lax.shift_right_logical(lo, 16) | hi


def _unpack_bf16_pairs(p):
    lo = pltpu.bitcast(lax.shift_left(p, 16), F32)
    hi = pltpu.bitcast(p & jnp.int32(-65536), F32)
    return lo, hi


def _merge_kernel(att_ref, hf_ref, hb_ref, mo_ref, ga_ref, gm_ref, x_ref, mod_ref, nml_ref, gpm_ref, gpf_ref,
                  wua_ref, wum_ref, wo_ref, wr_ref, br_ref, x1_ref, h2_ref, idx_ref, gate_ref, cnt_ref):
    hsum = hf_ref[0, 0] + hb_ref[0, 0]
    parts = []
    for h in range(ML_HEADS):
        hh = hsum[:, h * ML_DV:(h + 1) * ML_DV]
        parts.append(hh * lax.rsqrt(jnp.mean(hh * hh, axis=-1, keepdims=True) + EPS))
    hn = jnp.concatenate(parts, axis=-1) * nml_ref[...]
    ml = (hn * jax.nn.sigmoid(mo_ref[0])).astype(BF16)
    a = jnp.dot(att_ref[0], wua_ref[...], preferred_element_type=F32)
    m = jnp.dot(ml, wum_ref[...], preferred_element_type=F32)
    z = (jax.nn.sigmoid(ga_ref[0]) * a + jax.nn.sigmoid(gm_ref[0]) * m).astype(BF16)
    mix = jnp.dot(z, wo_ref[...], preferred_element_type=F32)
    gate1 = mod_ref[0, 2:3, :]
    shift2 = mod_ref[0, 3:4, :]
    scale2 = mod_ref[0, 4:5, :]
    x1 = x_ref[0] + gate1 * _rms(mix, gpm_ref[...])
    x1_ref[0] = x1
    h2 = _rms(x1, gpf_ref[...]) * (1.0 + scale2) + shift2
    h2_ref[0] = _pack_bf16_pairs(h2)

    h2_hi = h2.astype(BF16)
    h2_lo = (h2 - h2_hi.astype(F32)).astype(BF16)
    wr = wr_ref[...]
    wr_hi = wr.astype(BF16)
    wr_lo = (wr - wr_hi.astype(F32)).astype(BF16)
    logits = (jnp.dot(h2_hi, wr_hi, preferred_element_type=F32)
              + jnp.dot(h2_lo, wr_hi, preferred_element_type=F32)
              + jnp.dot(h2_hi, wr_lo, preferred_element_type=F32)) + br_ref[...]
    lane = lax.broadcasted_iota(jnp.int32, logits.shape, 1).astype(F32)
    work = jnp.where(lane < N_EXPERTS, logits, -jnp.inf)
    idx_out = jnp.zeros(logits.shape, F32)
    val_out = jnp.zeros(logits.shape, F32)
    picked = jnp.zeros(logits.shape, F32)
    top0 = None
    esum = None
    for k in range(TOP_K):
        mx = jnp.max(work, axis=-1, keepdims=True)
        sel = jnp.min(jnp.where(work == mx, lane, float(LANES)), axis=-1, keepdims=True)
        if k == 0:
            top0 = mx
        e = jnp.exp(mx - top0)
        esum = e if k == 0 else esum + e
        idx_out = jnp.where(lane == k, sel, idx_out)
        val_out = jnp.where(lane == k, e, val_out)
        picked = jnp.where(lane == sel, 1.0, picked)
        work = jnp.where(lane == sel, -jnp.inf, work)
    idx_ref[0] = idx_out.astype(jnp.int32)
    gate_ref[0] = val_out / esum
    for r in range(logits.shape[0] // ROUTE_TILE):
        cnt_ref[0, r:r + 1, :] = jnp.sum(picked[r * ROUTE_TILE:(r + 1) * ROUTE_TILE], axis=0, keepdims=True)


def _merge(att, h, pj, x, mod, mod_row, p, tm):
    bsz, seq, d = x.shape
    tok3 = lambda b, i: (b, i, 0)
    wr = jnp.pad(p["w_router"].astype(F32), ((0, 0), (0, LANES - N_EXPERTS)))
    br = jnp.pad(p["b_router"].astype(F32), (0, LANES - N_EXPERTS)).reshape(1, LANES)
    row = lambda v: v.astype(F32).reshape(1, -1)
    in_specs = [pl.BlockSpec((1, tm, Q_W), tok3),
                pl.BlockSpec((1, 1, tm, MLV_W), lambda b, i: (0, b, i, 0)),
                pl.BlockSpec((1, 1, tm, MLV_W), lambda b, i: (1, b, i, 0)),
                pl.BlockSpec((1, tm, MLV_W), tok3), pl.BlockSpec((1, tm, d), tok3), pl.BlockSpec((1, tm, d), tok3),
                pl.BlockSpec((1, tm, d), tok3),
                pl.BlockSpec((1, 6, d), lambda b, i: (mod_row(b), 0, 0)),
                _const_spec((1, MLV_W)), _const_spec((1, d)), _const_spec((1, d)),
                _const_spec((Q_W, d)), _const_spec((MLV_W, d)), _const_spec((d, d)),
                _const_spec((d, LANES)), _const_spec((1, LANES))]
    out_shape = [jax.ShapeDtypeStruct((bsz, seq, d), F32), jax.ShapeDtypeStruct((bsz, seq, d // 2), jnp.int32),
                 jax.ShapeDtypeStruct((bsz, seq, LANES), jnp.int32), jax.ShapeDtypeStruct((bsz, seq, LANES), F32),
                 jax.ShapeDtypeStruct((bsz * (seq // tm), tm // ROUTE_TILE, LANES), F32)]
    out_specs = [pl.BlockSpec((1, tm, d), tok3), pl.BlockSpec((1, tm, d // 2), tok3),
                 pl.BlockSpec((1, tm, LANES), tok3), pl.BlockSpec((1, tm, LANES), tok3),
                 pl.BlockSpec((1, tm // ROUTE_TILE, LANES), lambda b, i: (b * (seq // tm) + i, 0, 0))]
    return pl.pallas_call(
        _merge_kernel,
        grid=(bsz, seq // tm),
        in_specs=in_specs,
        out_specs=out_specs,
        out_shape=out_shape,
        compiler_params=_cparams(("parallel", "parallel")),
        name="merge_router",
    )(att, h, h, pj["mo"], pj["ga"], pj["gm"], x, mod, row(p["norm_ml"]), row(p["g_post_mix"]),
      row(p["g_pre_ffn"]), p["w_up_att"].astype(BF16), p["w_up_ml"].astype(BF16), p["w_out"].astype(BF16), wr, br)


def _sc_gather_rows(table, idx):
    n = idx.shape[0]
    width = table.shape[1]
    chunk = SC_GATHER_CHUNK
    n_workers = SC_CORES * SC_SUBCORES
    rows_per_worker = n // n_workers
    n_pairs = rows_per_worker // (2 * chunk)
    assert n_pairs * 2 * chunk * n_workers == n
    mesh = plsc.VectorSubcoreMesh(core_axis_name="c", subcore_axis_name="s")

    def body(table_hbm, idx_hbm, out_hbm, idx_v, rows_v, gsem, osem):
        wid = lax.axis_index("s") * SC_CORES + lax.axis_index("c")
        base0 = wid * rows_per_worker

        def gather_copy(b):
            return pltpu.make_async_copy(table_hbm.at[idx_v.at[b]], rows_v.at[b], gsem.at[b])

        def write_copy(ci, b):
            base = pl.multiple_of(base0 + ci * chunk, 8)
            return pltpu.make_async_copy(rows_v.at[b], out_hbm.at[pl.ds(base, chunk)], osem.at[b])

        def issue(ci, b):
            base = pl.multiple_of(base0 + ci * chunk, 8)
            pltpu.sync_copy(idx_hbm.at[pl.ds(base, chunk)], idx_v.at[b])
            gather_copy(b).start()

        def finish(ci, b):
            gather_copy(b).wait()
            write_copy(ci, b).start()

        issue(0, 0)

        @pl.loop(0, n_pairs)
        def _(j):
            @pl.when(j > 0)
            def _():
                write_copy(2 * j - 1, 1).wait()
            issue(2 * j + 1, 1)
            finish(2 * j, 0)

            @pl.when(j < n_pairs - 1)
            def _():
                write_copy(2 * j, 0).wait()
                issue(2 * j + 2, 0)
            finish(2 * j + 1, 1)

        write_copy(2 * n_pairs - 2, 0).wait()
        write_copy(2 * n_pairs - 1, 1).wait()

    return pl.kernel(
        body, mesh=mesh,
        out_type=jax.ShapeDtypeStruct((n, width), table.dtype),
        scratch_types=[pltpu.VMEM((2, chunk), jnp.int32), pltpu.VMEM((2, chunk, width), table.dtype),
                       pltpu.SemaphoreType.DMA((2,)), pltpu.SemaphoreType.DMA((2,))],
    )(table, idx)


def _ffn_kernel(be_ref, nu_ref, xs_ref, w1_ref, b1_ref, w2_ref, b2_ref, ys_ref, w1b_s, w2b_s):
    i = pl.program_id(0)
    e = be_ref[i]
    e_prev = be_ref[jnp.maximum(i - 1, 0)]

    @pl.when((i == 0) | (e != e_prev))
    def _():
        w1b_s[...] = w1_ref[0].astype(BF16)
        w2b_s[...] = w2_ref[0].astype(BF16)

    @pl.when(i < nu_ref[0])
    def _():
        half = D_MODEL // 2
        x_lo, x_hi = _unpack_bf16_pairs(xs_ref[...])
        hmid = (jnp.dot(x_lo.astype(BF16), w1b_s[:half, :], preferred_element_type=F32)
                + jnp.dot(x_hi.astype(BF16), w1b_s[half:, :], preferred_element_type=F32)) + b1_ref[0]
        glu = jnp.minimum(hmid[:, :D_FF], SWIGLU_LIMIT)
        lin = jnp.clip(hmid[:, D_FF:], -SWIGLU_LIMIT, SWIGLU_LIMIT)
        act = ((lin + 1.0) * glu * jax.nn.sigmoid(SWIGLU_ALPHA * glu)).astype(BF16)
        y = jnp.dot(act, w2b_s[...], preferred_element_type=F32) + b2_ref[0]
        ys_ref[...] = _pack_bf16_pairs(y)

    @pl.when(i >= nu_ref[0])
    def _():
        ys_ref[...] = jnp.zeros(ys_ref.shape, jnp.int32)


def _sc_scatter_rows(table, pos, n_rows):
    n_tok, width = table.shape
    chunk = SC_GATHER_CHUNK
    n_workers = SC_CORES * SC_SUBCORES
    tok_per_worker = n_tok // n_workers
    n_pairs = tok_per_worker // (2 * chunk)
    assert n_pairs * 2 * chunk * n_workers == n_tok
    mesh = plsc.VectorSubcoreMesh(core_axis_name="c", subcore_axis_name="s")

    def body(table_hbm, pos_hbm, out_hbm, idx_v, rows_v, lsem, ssem):
        wid = lax.axis_index("s") * SC_CORES + lax.axis_index("c")
        base0 = wid * tok_per_worker

        def load_copy(ci, b):
            base = pl.multiple_of(base0 + ci * chunk, 8)
            return pltpu.make_async_copy(table_hbm.at[pl.ds(base, chunk)], rows_v.at[b], lsem.at[b])

        def scatter_copy(b, k):
            return pltpu.make_async_copy(rows_v.at[b], out_hbm.at[idx_v.at[b, k]], ssem.at[b])

        def load(ci, b):
            base = pl.multiple_of(base0 + ci * chunk, 8)
            load_copy(ci, b).start()
            for k in range(TOP_K):
                pltpu.sync_copy(pos_hbm.at[k, pl.ds(base, chunk)], idx_v.at[b, k])

        def scatter(ci, b):
            load_copy(ci, b).wait()
            for k in range(TOP_K):
                scatter_copy(b, k).start()

        def drain(b):
            for k in range(TOP_K):
                scatter_copy(b, k).wait()

        load(0, 0)

        @pl.loop(0, n_pairs)
        def _(j):
            @pl.when(j > 0)
            def _():
                drain(1)
            load(2 * j + 1, 1)
            scatter(2 * j, 0)

            @pl.when(j < n_pairs - 1)
            def _():
                drain(0)
                load(2 * j + 2, 0)
            scatter(2 * j + 1, 1)

        drain(0)
        drain(1)

    return pl.kernel(
        body, mesh=mesh,
        out_type=jax.ShapeDtypeStruct((n_rows, width), table.dtype),
        scratch_types=[pltpu.VMEM((2, TOP_K, chunk), jnp.int32), pltpu.VMEM((2, chunk, width), table.dtype),
                       pltpu.SemaphoreType.DMA((2,)), pltpu.SemaphoreType.DMA((2,))],
    )(table, pos)


def _pos_kernel(idx_ref, base_ref, pos_ref):
    idx = idx_ref[...]
    rows = idx.shape[0]
    lane = lax.broadcasted_iota(jnp.int32, idx.shape, 1)
    hots = [jnp.where(lane == idx[:, k:k + 1], 1.0, 0.0) for k in range(TOP_K)]
    cnt = hots[0] + hots[1] + hots[2] + hots[3]
    ri = lax.broadcasted_iota(jnp.int32, (rows, rows), 0)
    ci = lax.broadcasted_iota(jnp.int32, (rows, rows), 1)
    earlier = jnp.where(ci < ri, 1.0, 0.0).astype(BF16)
    ahead = jnp.dot(earlier, cnt.astype(BF16), preferred_element_type=F32) + base_ref[0]
    posmat = jnp.zeros(idx.shape, F32)
    for k in range(TOP_K):
        posmat = jnp.where(lane == k, jnp.sum(hots[k] * ahead, axis=-1, keepdims=True), posmat)
    pos_ref[...] = posmat.T[:TOP_K, :].astype(jnp.int32)


def _route(idx, tile_counts, n_tok):
    n_tiles = n_tok // ROUTE_TILE
    n_blocks = n_tok * TOP_K // MOE_ROWS + N_EXPERTS
    cnt = tile_counts.reshape(n_tiles, LANES).astype(jnp.int32)
    tile_off = jnp.cumsum(cnt, axis=0) - cnt
    total = jnp.sum(cnt, axis=0)
    padded = (total + MOE_ROWS - 1) // MOE_ROWS * MOE_ROWS
    pend = jnp.cumsum(padded)
    base = (pend - padded)[None, :] + tile_off
    starts = jnp.arange(n_blocks, dtype=jnp.int32) * MOE_ROWS
    block_e = jnp.minimum(jnp.sum(pend[None, :N_EXPERTS] <= starts[:, None], axis=1), N_EXPERTS - 1).astype(jnp.int32)
    n_used = (pend[N_EXPERTS - 1] // MOE_ROWS).astype(jnp.int32).reshape(1)
    pos = pl.pallas_call(
        _pos_kernel,
        grid=(n_tiles,),
        in_specs=[pl.BlockSpec((ROUTE_TILE, LANES), lambda i: (i, 0)),
                  pl.BlockSpec((1, 1, LANES), lambda i: (i, 0, 0))],
        out_specs=pl.BlockSpec((TOP_K, ROUTE_TILE), lambda i: (0, i)),
        out_shape=jax.ShapeDtypeStruct((TOP_K, n_tok), jnp.int32),
        compiler_params=_cparams(("parallel",)),
        name="route_pos",
    )(idx.reshape(n_tok, LANES), base.astype(F32).reshape(n_tiles, 1, LANES))
    return block_e, n_used, pos


def _moe(h2p, idx, tile_counts, w1, b1, w2, b2):
    n_tok, half = h2p.shape
    d = 2 * half
    block_e, n_used, pos = _route(idx, tile_counts, n_tok)
    n_blocks = block_e.shape[0]
    xs = _sc_scatter_rows(h2p, pos, n_blocks * MOE_ROWS)
    grid_spec = pltpu.PrefetchScalarGridSpec(
        num_scalar_prefetch=2,
        grid=(n_blocks,),
        in_specs=[pl.BlockSpec((MOE_ROWS, half), lambda i, be, nu: (i, 0)),
                  pl.BlockSpec((1, d, 2 * D_FF), lambda i, be, nu: (be[i], 0, 0)),
                  pl.BlockSpec((1, 1, 2 * D_FF), lambda i, be, nu: (be[i], 0, 0)),
                  pl.BlockSpec((1, D_FF, d), lambda i, be, nu: (be[i], 0, 0)),
                  pl.BlockSpec((1, 1, d), lambda i, be, nu: (be[i], 0, 0))],
        out_specs=pl.BlockSpec((MOE_ROWS, half), lambda i, be, nu: (i, 0)),
        scratch_shapes=[pltpu.VMEM((d, 2 * D_FF), BF16), pltpu.VMEM((D_FF, d), BF16)],
    )
    ys = pl.pallas_call(
        _ffn_kernel,
        grid_spec=grid_spec,
        out_shape=jax.ShapeDtypeStruct((n_blocks * MOE_ROWS, half), jnp.int32),
        compiler_params=_cparams(("arbitrary",)),
        name="moe_ffn",
    )(block_e, n_used, xs, w1, b1.reshape(N_EXPERTS, 1, -1), w2, b2.reshape(N_EXPERTS, 1, -1))
    return _sc_gather_rows(ys, pos.reshape(-1))


def _final_kernel(x1_ref, y0_ref, y1_ref, y2_ref, y3_ref, gate_ref, mod_ref, g_ref, o_ref):
    gates = gate_ref[0]
    lo = hi = None
    for k, y_ref in enumerate((y0_ref, y1_ref, y2_ref, y3_ref)):
        y_lo, y_hi = _unpack_bf16_pairs(y_ref[0, 0])
        gk = gates[:, k:k + 1]
        lo = gk * y_lo if k == 0 else lo + gk * y_lo
        hi = gk * y_hi if k == 0 else hi + gk * y_hi
    y = jnp.concatenate([lo, hi], axis=-1)
    gate2 = mod_ref[0, 5:6, :]
    o_ref[0] = x1_ref[0] + gate2 * _rms(y, g_ref[...])


def _final(x1, yg, gate, mod, mod_row, g_post_ffn, tm):
    bsz, seq, d = x1.shape
    tok3 = lambda b, i: (b, i, 0)
    yg = yg.reshape(TOP_K, bsz, seq, d // 2)
    slot_specs = [pl.BlockSpec((1, 1, tm, d // 2), functools.partial(lambda b, i, k: (k, b, i, 0), k=k))
                  for k in range(TOP_K)]
    return pl.pallas_call(
        _final_kernel,
        grid=(bsz, seq // tm),
        in_specs=[pl.BlockSpec((1, tm, d), tok3)] + slot_specs +
                 [pl.BlockSpec((1, tm, LANES), tok3),
                  pl.BlockSpec((1, 6, d), lambda b, i: (mod_row(b), 0, 0)), _const_spec((1, d))],
        out_specs=pl.BlockSpec((1, tm, d), tok3),
        out_shape=jax.ShapeDtypeStruct((bsz, seq, d), F32),
        compiler_params=_cparams(("parallel", "parallel")),
        name="final_residual",
    )(x1, yg, yg, yg, yg, gate, mod, g_post_ffn.astype(F32).reshape(1, d))


def _stream(x, mod, mod_row, p, ctx_kv, init_state, rope, tm):
    bsz, seq, d = x.shape
    is_ctx = ctx_kv is None
    pj = _project(x, mod, p["g_pre_mix"], p["w_in"], p["b_gates"], mod_row, rope, is_ctx, tm)
    if is_ctx:
        att = _attention(pj["q"], None, None, pj["kd"], pj["vd"], p["sink"], window=False)
    else:
        att = _attention(pj["q"], pj["kd"], pj["vd"], ctx_kv[0], ctx_kv[1], p["sink"], window=True)
    ml = _mlstm(pj["mq"], pj["mk"], pj["mv"], pj["gates"], init_state, emit_state=is_ctx)
    x1, h2, idx, gate, tile_counts = _merge(att, ml[0], pj, x, mod, mod_row, p, tm)
    yg = _moe(h2.reshape(bsz * seq, d // 2), idx, tile_counts, p["w1"], p["b1"].astype(F32), p["w2"],
              p["b2"].astype(F32))
    out = _final(x1, yg, gate, mod, mod_row, p["g_post_ffn"], tm)
    return out, pj, ml


def kernel(x_prompt, x_sample, c, cache_k, cache_v, state_C, state_n, state_m, c_ctx, w_ada, b_ada, g_pre_mix,
           w_in, b_gates, attn_sink, norm_mlstm, w_up_att, w_up_ml, w_out, g_post_mix, g_pre_ffn, w_router,
           b_router, w1, b1, w2, b2, g_post_ffn):
    depth = w_ada.shape[0]
    n_dec = c.shape[0]
    cond = jnp.concatenate([c_ctx[None, :], c], axis=0).astype(F32)
    cond = jnp.pad(cond, ((0, 16 - cond.shape[0]), (0, 0)))
    y_prompt, y_sample = x_prompt, x_sample
    ks_, vs_, cs_, ns_, ms_ = [], [], [], [], []
    for l in range(depth):
        p = dict(g_pre_mix=g_pre_mix[l], w_in=w_in[l], b_gates=b_gates[l], sink=attn_sink[l], norm_ml=norm_mlstm[l],
                 w_up_att=w_up_att[l], w_up_ml=w_up_ml[l], w_out=w_out[l], g_post_mix=g_post_mix[l],
                 g_pre_ffn=g_pre_ffn[l], w_router=w_router[l], b_router=b_router[l], w1=w1[l], b1=b1[l],
                 w2=w2[l], b2=b2[l], g_post_ffn=g_post_ffn[l])
        mod = _adaln(cond, w_ada[l], b_ada[l]).reshape(16, 6, D_MODEL)
        y_prompt, pj, ml = _stream(y_prompt, mod, lambda b: 0, p, None, None, rope=False, tm=256)
        bsz, seq = x_prompt.shape[:2]
        nat = pj["nat"]
        ks_.append(nat[..., :KV_W].reshape(bsz, seq, ATT_KV_HEADS, HEAD_DIM))
        vs_.append(nat[..., KV_W:].reshape(bsz, seq, ATT_KV_HEADS, HEAD_DIM))
        cs_.append(ml[1])
        ns_.append(ml[2])
        ms_.append(ml[3][..., 0])
        ctx_kv = (_dup_heads(cache_k[:, l]), _dup_heads(cache_v[:, l]))
        init = (state_C[:, l], state_n[:, l], state_m[:, l])
        y_sample, _, _ = _stream(y_sample, mod, lambda b: b + 1, p, ctx_kv, init, rope=True, tm=512)
    return (y_prompt, y_sample, jnp.stack(ks_, axis=1), jnp.stack(vs_, axis=1), jnp.stack(cs_, axis=1),
            jnp.stack(ns_, axis=1), jnp.stack(ms_, axis=1))
```

```python
import functools

import numpy as np
import jax
import jax.numpy as jnp
from jax import lax
from jax.experimental import pallas as pl
from jax.experimental.pallas import tpu as pltpu
from jax.experimental.pallas import tpu_sc as plsc

F32 = jnp.float32
BF16 = jnp.bfloat16

D_MODEL = 1024
GRID_W = 64
ATT_HEADS = 8
ATT_KV_HEADS = 2
ATT_GROUP = ATT_HEADS // ATT_KV_HEADS
HEAD_DIM = 64
BLOCK = 128
ROPE_THETA = 10000.0
AXIS_FREQS = HEAD_DIM // 4
ML_HEADS = 4
ML_DK = 128
ML_DV = 256
ML_CHUNK = 128
N_EXPERTS = 32
TOP_K = 4
D_FF = 1024
SWIGLU_ALPHA = 1.702
SWIGLU_LIMIT = 7.0
EPS = 1e-6
NEG = -1e30

Q_W = ATT_HEADS * HEAD_DIM
KV_W = ATT_KV_HEADS * HEAD_DIM
MLQK_W = ML_HEADS * ML_DK
MLV_W = ML_HEADS * ML_DV
GATE_W = 4 * ML_HEADS

LANES = 128
VMEM_LIMIT = 56 * 1024 * 1024
MOE_ROWS = 256
SC_CORES = 2
SC_SUBCORES = 16
SC_GATHER_CHUNK = 32
ROUTE_TILE = 256


def _cparams(sem):
    return pltpu.CompilerParams(dimension_semantics=sem, vmem_limit_bytes=VMEM_LIMIT)


def _const_spec(shape):
    nd = len(shape)
    return pl.BlockSpec(shape, lambda *_: (0,) * nd)


def _adaln_kernel(cond_ref, w_ref, b_ref, o_ref):
    cnd = cond_ref[...]
    act = cnd * jax.nn.sigmoid(cnd)
    o_ref[...] = jnp.dot(act.astype(BF16), w_ref[...].astype(BF16),
                         preferred_element_type=F32) + b_ref[...]


def _adaln(cond, w_ada, b_ada):
    rows, d = cond.shape
    n = w_ada.shape[1]
    tn = 1536
    return pl.pallas_call(
        _adaln_kernel,
        grid=(n // tn,),
        in_specs=[_const_spec((rows, d)),
                  pl.BlockSpec((d, tn), lambda j: (0, j)),
                  pl.BlockSpec((1, tn), lambda j: (0, j))],
        out_specs=pl.BlockSpec((rows, tn), lambda j: (0, j)),
        out_shape=jax.ShapeDtypeStruct((rows, n), F32),
        compiler_params=_cparams(("parallel",)),
        name="adaln",
    )(cond, w_ada, b_ada.reshape(1, n))


def _pack_w_in(w_in, b_gates, rope, with_nat):
    o_q, o_k, o_v = 0, Q_W, Q_W + KV_W
    o_mq = Q_W + 2 * KV_W
    o_mk = o_mq + MLQK_W
    o_mv = o_mk + MLQK_W
    o_g = o_mv + MLV_W
    o_mo = o_g + GATE_W
    o_ga = o_mo + MLV_W
    o_gm = o_ga + D_MODEL
    half = HEAD_DIM // 2
    zeros64 = jnp.zeros((D_MODEL, HEAD_DIM), w_in.dtype)

    def head_cols(base, h, rot):
        lo = base + h * HEAD_DIM
        if rot:
            return [w_in[:, lo + half:lo + HEAD_DIM], w_in[:, lo:lo + half]]
        return [w_in[:, lo:lo + HEAD_DIM]]

    def q_cols(rot):
        cols = []
        for h in range(ATT_HEADS):
            hc = head_cols(o_q, h, rot)
            cols += (hc + [zeros64]) if h % 2 == 0 else ([zeros64] + hc)
        return cols

    def dup_cols(base, rot):
        cols = []
        for h in range(ATT_KV_HEADS):
            hc = head_cols(base, h, rot)
            cols += hc + hc
        return cols

    segs, cols, off = {}, [], 0

    def add(name, cl):
        nonlocal off
        width = sum(c.shape[1] for c in cl)
        segs[name] = (off, off + width)
        cols.extend(cl)
        off += width

    add("q", q_cols(False))
    if rope:
        add("qrot", q_cols(True))
    add("kd", dup_cols(o_k, False))
    if rope:
        add("kdrot", dup_cols(o_k, True))
    add("vd", dup_cols(o_v, False))
    if with_nat:
        add("nat", [w_in[:, o_k:o_k + 2 * KV_W]])
    add("mq", [w_in[:, o_mq:o_mq + MLQK_W]])
    add("mk", [w_in[:, o_mk:o_mk + MLQK_W]])
    add("mv", [w_in[:, o_mv:o_mv + MLV_W]])
    add("mo", [w_in[:, o_mo:o_mo + MLV_W]])
    add("ga", [w_in[:, o_ga:o_ga + D_MODEL]])
    add("gm", [w_in[:, o_gm:o_gm + D_MODEL]])
    add("gates", [w_in[:, o_g:o_g + GATE_W], jnp.zeros((D_MODEL, LANES - GATE_W), w_in.dtype)])
    wp = jnp.concatenate(cols, axis=1).astype(BF16)
    bg = jnp.pad(b_gates.astype(F32), (0, LANES - GATE_W)).reshape(1, LANES)
    return wp, bg, segs


def _rope_tables(n_tok):
    rows = n_tok // GRID_W
    row = np.repeat(np.arange(rows), GRID_W).astype(np.float32)
    col = np.tile(np.arange(GRID_W), rows).astype(np.float32)
    inv = (np.float32(ROPE_THETA) ** (-np.arange(AXIS_FREQS, dtype=np.float32) / AXIS_FREQS)).astype(np.float32)
    ang = np.concatenate([row[:, None] * inv, col[:, None] * inv], axis=-1).astype(np.float32)
    cos, sin = np.cos(ang), np.sin(ang)
    c64 = np.concatenate([cos, cos], axis=-1)
    s64 = np.concatenate([-sin, sin], axis=-1)
    return (jnp.asarray(np.tile(c64, (1, 2)), F32), jnp.asarray(np.tile(s64, (1, 2)), F32))


def _proj_kernel(*refs, segs, rope, with_nat):
    it = iter(refs)
    x_ref, mod_ref, g_ref, w_ref, bg_ref = (next(it) for _ in range(5))
    cos_ref = sin_ref = None
    if rope:
        cos_ref, sin_ref = next(it), next(it)
    q_ref, kd_ref, vd_ref = next(it), next(it), next(it)
    nat_ref = next(it) if with_nat else None
    mq_ref, mk_ref, mv_ref, gates_ref, mo_ref, ga_ref, gm_ref = (next(it) for _ in range(7))

    x = x_ref[0]
    shift = mod_ref[0, 0:1, :]
    scale = mod_ref[0, 1:2, :]
    ms = jnp.mean(x * x, axis=-1, keepdims=True)
    xn = x * lax.rsqrt(ms + EPS) * g_ref[...]
    xb = (xn * (1.0 + scale) + shift).astype(BF16)

    def seg(name):
        lo, hi = segs[name]
        return jnp.dot(xb, w_ref[:, lo:hi], preferred_element_type=F32)

    uq = seg("q")
    ukd = seg("kd")
    if rope:
        uqr = seg("qrot")
        ukr = seg("kdrot")
        cs = cos_ref[...]
        sn = sin_ref[...]
        for h in range(ATT_HEADS):
            sl = slice(h * LANES, (h + 1) * LANES)
            q_ref[0, h] = ((uq[:, sl] * cs + uqr[:, sl] * sn) * HEAD_DIM ** -0.5).astype(BF16)
        for h in range(ATT_KV_HEADS):
            sl = slice(h * LANES, (h + 1) * LANES)
            kd_ref[0, :, sl] = (ukd[:, sl] * cs + ukr[:, sl] * sn).astype(BF16)
    else:
        for h in range(ATT_HEADS):
            sl = slice(h * LANES, (h + 1) * LANES)
            q_ref[0, h] = (uq[:, sl] * HEAD_DIM ** -0.5).astype(BF16)
        kd_ref[0] = ukd.astype(BF16)
    vd_ref[0] = seg("vd").astype(BF16)
    if with_nat:
        nat_ref[0] = seg("nat")
    mq_ref[0] = (seg("mq") * ML_DK ** -0.5).astype(BF16)
    mk_ref[0] = seg("mk").astype(BF16)
    mv_ref[0] = seg("mv").astype(BF16)
    gates_ref[0] = seg("gates") + bg_ref[...]
    mo_ref[0] = seg("mo")
    ga_ref[0] = seg("ga")
    gm_ref[0] = seg("gm")


def _project(x, mod, g_pre, w_in, b_gates, mod_row, rope, with_nat, tm):
    bsz, seq, d = x.shape
    wp, bg, segs = _pack_w_in(w_in, b_gates, rope, with_nat)
    nw = wp.shape[1]
    tok3 = lambda b, i: (b, i, 0)
    in_specs = [pl.BlockSpec((1, tm, d), tok3),
                pl.BlockSpec((1, 6, d), lambda b, i: (mod_row(b), 0, 0)),
                _const_spec((1, d)),
                pl.BlockSpec((d, nw), lambda b, i: (0, 0), pipeline_mode=pl.Buffered(1)),
                _const_spec((1, LANES))]
    args = [x, mod, g_pre.reshape(1, d), wp, bg]
    if rope:
        cos_t, sin_t = _rope_tables(seq)
        in_specs += [pl.BlockSpec((tm, LANES), lambda b, i: (i, 0))] * 2
        args += [cos_t, sin_t]

    def tok_out(width, dtype):
        return (jax.ShapeDtypeStruct((bsz, seq, width), dtype), pl.BlockSpec((1, tm, width), tok3))

    outs = [(jax.ShapeDtypeStruct((bsz, ATT_HEADS, seq, LANES), BF16),
             pl.BlockSpec((1, ATT_HEADS, tm, LANES), lambda b, i: (b, 0, i, 0))),
            tok_out(2 * LANES, BF16), tok_out(2 * LANES, BF16)]
    if with_nat:
        outs.append(tok_out(2 * KV_W, F32))
    outs += [tok_out(MLQK_W, BF16), tok_out(MLQK_W, BF16), tok_out(MLV_W, BF16), tok_out(LANES, F32),
             tok_out(MLV_W, F32), tok_out(D_MODEL, F32), tok_out(D_MODEL, F32)]
    res = pl.pallas_call(
        functools.partial(_proj_kernel, segs=segs, rope=rope, with_nat=with_nat),
        grid=(bsz, seq // tm),
        in_specs=in_specs,
        out_specs=[o[1] for o in outs],
        out_shape=[o[0] for o in outs],
        compiler_params=_cparams(("parallel", "parallel")),
        name="proj_rope" if rope else "proj_ctx",
    )(*args)
    names = ["q", "kd", "vd"] + (["nat"] if with_nat else []) + ["mq", "mk", "mv", "gates", "mo", "ga", "gm"]
    return dict(zip(names, res))


def _attn_kernel(*refs, window, n_blocks, n_ctx):
    it = iter(refs)
    sink_ref, q_ref = next(it), next(it)
    if window:
        kp_ref, kc_ref, kn_ref, vp_ref, vc_ref, vn_ref = (next(it) for _ in range(6))
    ck_ref, cv_ref, o_ref = next(it), next(it), next(it)
    j = pl.program_id(1)
    n_win = 3 * BLOCK if window else 0
    n_keys = n_win + n_ctx
    rows = ATT_GROUP * BLOCK

    row_id = lax.broadcasted_iota(jnp.int32, (rows, 1), 0)
    if window:
        rr = lax.broadcasted_iota(jnp.int32, (rows, n_keys), 0) & (BLOCK - 1)
        cc = lax.broadcasted_iota(jnp.int32, (rows, n_keys), 1)
        dd = cc - rr
        lo_b = jnp.where(j == 0, BLOCK, 0)
        hi_b = jnp.where(j == n_blocks - 1, 2 * BLOCK - 1, 3 * BLOCK - 1)
        in_band = (dd >= 0) & (dd <= 2 * BLOCK) & (cc >= lo_b) & (cc <= hi_b)
        mask = in_band | (cc >= n_win)
    lane_lo = lax.broadcasted_iota(jnp.int32, (BLOCK, LANES), 1) < HEAD_DIM

    for kvh in range(ATT_KV_HEADS):
        sl = slice(kvh * LANES, (kvh + 1) * LANES)
        if window:
            keys = jnp.concatenate([kp_ref[0, :, sl], kc_ref[0, :, sl], kn_ref[0, :, sl], ck_ref[0, :, sl]], axis=0)
            vals = jnp.concatenate([vp_ref[0, :, sl], vc_ref[0, :, sl], vn_ref[0, :, sl], cv_ref[0, :, sl]], axis=0)
        else:
            keys = ck_ref[0, :, sl]
            vals = cv_ref[0, :, sl]
        qs = q_ref[0, kvh * ATT_GROUP:(kvh + 1) * ATT_GROUP].reshape(rows, LANES)
        s = lax.dot_general(qs, keys, (((1,), (1,)), ((), ())), preferred_element_type=F32)
        if window:
            s = jnp.where(mask, s, NEG)
        snk = jnp.full((rows, 1), sink_ref[kvh * ATT_GROUP + ATT_GROUP - 1], F32)
        for g in range(ATT_GROUP - 2, -1, -1):
            snk = jnp.where(row_id < (g + 1) * BLOCK, sink_ref[kvh * ATT_GROUP + g], snk)
        m = jnp.maximum(jnp.max(s, axis=-1, keepdims=True), snk)
        p = jnp.exp(s - m)
        den = jnp.sum(p, axis=-1, keepdims=True) + jnp.exp(snk - m)
        o = jnp.dot(p.astype(BF16), vals, preferred_element_type=F32) / den
        for pair in range(ATT_GROUP // 2):
            even = o[(2 * pair) * BLOCK:(2 * pair + 1) * BLOCK]
            odd = o[(2 * pair + 1) * BLOCK:(2 * pair + 2) * BLOCK]
            col = (kvh * (ATT_GROUP // 2) + pair) * LANES
            o_ref[0, :, col:col + LANES] = jnp.where(lane_lo, even, odd).astype(BF16)


def _attention(q, kd, vd, ckd, cvd, sink, window):
    bsz, _, seq, _ = q.shape
    nb = seq // BLOCK
    n_ctx = ckd.shape[1]
    in_specs = [pl.BlockSpec(memory_space=pltpu.SMEM),
                pl.BlockSpec((1, ATT_HEADS, BLOCK, LANES), lambda b, j: (b, 0, j, 0))]
    args = [sink.astype(F32), q]
    if window:
        prev = lambda b, j: (b, jnp.maximum(j - 1, 0), 0)
        cur = lambda b, j: (b, j, 0)
        nxt = lambda b, j: (b, jnp.minimum(j + 1, nb - 1), 0)
        for arr in (kd, vd):
            in_specs += [pl.BlockSpec((1, BLOCK, 2 * LANES), f) for f in (prev, cur, nxt)]
            args += [arr, arr, arr]
    in_specs += [pl.BlockSpec((1, n_ctx, 2 * LANES), lambda b, j: (b, 0, 0))] * 2
    args += [ckd, cvd]
    return pl.pallas_call(
        functools.partial(_attn_kernel, window=window, n_blocks=nb, n_ctx=n_ctx),
        grid=(bsz, nb),
        in_specs=in_specs,
        out_specs=pl.BlockSpec((1, BLOCK, Q_W), lambda b, j: (b, j, 0)),
        out_shape=jax.ShapeDtypeStruct((bsz, seq, Q_W), BF16),
        compiler_params=_cparams(("parallel", "parallel")),
        name="attn_window" if window else "attn_ctx",
    )(*args)


def _dup_heads(t):
    b, l = t.shape[:2]
    return jnp.concatenate([t, t], axis=-1).reshape(b, l, ATT_KV_HEADS * LANES).astype(BF16)


def _split3(x):
    hi = x.astype(BF16)
    r1 = x - hi.astype(F32)
    mid = r1.astype(BF16)
    lo = (r1 - mid.astype(F32)).astype(BF16)
    return hi, mid, lo


def _log_sigmoid(x):
    return jnp.minimum(x, 0.0) - jnp.log1p(jnp.exp(-jnp.abs(x)))


def _mlstm_kernel(*refs, n_chunks, has_init, emit_state):
    it = iter(refs)
    q_ref, k_ref, v_ref, g_ref = (next(it) for _ in range(4))
    if has_init:
        c0_ref, n0_ref, m0_ref = next(it), next(it), next(it)
    h_ref = next(it)
    if emit_state:
        co_ref, no_ref, mo_ref = next(it), next(it), next(it)
    ct_s, n_s, m_s = next(it), next(it), next(it)
    d = pl.program_id(1)
    t = pl.program_id(2)
    L = ML_CHUNK

    @pl.when(t == 0)
    def _():
        for h in range(ML_HEADS):
            if has_init:
                ct_s[h] = c0_ref[0, 0, h].T
                n_s[h] = n0_ref[0, 0, h:h + 1, :]
                m_s[h] = m0_ref[0, 0, h:h + 1, :]
            else:
                ct_s[h] = jnp.zeros((ML_DK, ML_DV), F32)
                n_s[h] = jnp.zeros((1, ML_DK), F32)
                m_s[h] = jnp.zeros((1, LANES), F32)

    ri = lax.broadcasted_iota(jnp.int32, (L, L), 0)
    ci = lax.broadcasted_iota(jnp.int32, (L, L), 1)
    tri = (ri - ci) * jnp.where(d == 0, 1, -1) >= 0
    tri_b = jnp.where(tri, 1.0, 0.0).astype(BF16)

    g = g_ref[0]
    g = jnp.where(d == 0, g, pltpu.roll(g, LANES - 2 * ML_HEADS, 1))
    lf = _log_sigmoid(g)
    g_t = g.T
    lf_t = lf.T
    bc = sum(jnp.dot(tri_b, part, preferred_element_type=F32) for part in _split3(lf))
    br = sum(lax.dot_general(part, tri_b, (((1,), (1,)), ((), ())), preferred_element_type=F32)
             for part in _split3(lf_t))
    tot = jnp.sum(lf, axis=0, keepdims=True)

    for h in range(ML_HEADS):
        fcol = ML_HEADS + h
        bcol = bc[:, fcol:fcol + 1]
        brow = br[fcol:fcol + 1, :]
        irow = g_t[h:h + 1, :]
        icol = g[:, h:h + 1]
        b_last = tot[:, fcol:fcol + 1]
        m_prev = m_s[h][:, 0:1]
        n_prev = n_s[h]
        ct_prev = ct_s[h]
        qh = q_ref[0, :, h * ML_DK:(h + 1) * ML_DK]
        kh = k_ref[0, :, h * ML_DK:(h + 1) * ML_DK]
        vh = v_ref[0, :, h * ML_DV:(h + 1) * ML_DV]

        dmat = jnp.where(tri, bcol - brow + irow, NEG)
        inter = bcol + m_prev
        m_t = jnp.maximum(inter, jnp.max(dmat, axis=-1, keepdims=True))
        w = jnp.exp(dmat - m_t)
        w_inter = jnp.exp(inter - m_t)
        s = lax.dot_general(qh, kh, (((1,), (1,)), ((), ())), preferred_element_type=F32) * w
        cq = jnp.dot(qh, ct_prev.astype(BF16), preferred_element_type=F32) * w_inter
        num = jnp.dot(s.astype(BF16), vh, preferred_element_type=F32) + cq
        qn = jnp.sum(qh.astype(F32) * n_prev, axis=-1, keepdims=True)
        den = jnp.sum(s, axis=-1, keepdims=True) + w_inter * qn
        h_ref[0, 0, :, h * ML_DV:(h + 1) * ML_DV] = num / jnp.maximum(jnp.abs(den), jnp.exp(-m_t))

        a = b_last - bcol + icol
        m_new = jnp.maximum(b_last + m_prev, jnp.max(a, axis=0, keepdims=True))
        wk = jnp.exp(a - m_new)
        decay = jnp.exp(b_last + m_prev - m_new)
        kw = kh.astype(F32) * wk
        ct_new = decay * ct_prev + jnp.dot(kw.T.astype(BF16), vh, preferred_element_type=F32)
        n_new = decay * n_prev + jnp.sum(kw, axis=0, keepdims=True)
        ct_s[h] = ct_new
        n_s[h] = n_new
        m_s[h] = jnp.broadcast_to(m_new, (1, LANES))
        if emit_state:
            @pl.when(t == n_chunks - 1)
            def _():
                co_ref[0, 0, h] = ct_new.T
                no_ref[0, 0, h:h + 1, :] = n_new
                mo_ref[0, 0, h:h + 1, :] = jnp.broadcast_to(m_new, (1, LANES))


def _mlstm(mq, mk, mv, gates, init, emit_state):
    bsz, seq, _ = mq.shape
    nc = seq // ML_CHUNK
    chunk = lambda b, d, t: (b, jnp.where(d == 0, t, nc - 1 - t), 0)
    state5 = lambda b, d, t: (b, d, 0, 0, 0)
    state4 = lambda b, d, t: (b, d, 0, 0)
    in_specs = [pl.BlockSpec((1, ML_CHUNK, MLQK_W), chunk), pl.BlockSpec((1, ML_CHUNK, MLQK_W), chunk),
                pl.BlockSpec((1, ML_CHUNK, MLV_W), chunk), pl.BlockSpec((1, ML_CHUNK, LANES), chunk)]
    args = [mq, mk, mv, gates]
    if init is not None:
        c0, n0, m0 = init
        in_specs += [pl.BlockSpec((1, 1, ML_HEADS, ML_DV, ML_DK), state5),
                     pl.BlockSpec((1, 1, ML_HEADS, ML_DK), state4),
                     pl.BlockSpec((1, 1, ML_HEADS, LANES), state4)]
        args += [c0.astype(F32), n0.astype(F32),
                 jnp.broadcast_to(m0.astype(F32)[..., None], m0.shape + (LANES,))]
    out_shape = [jax.ShapeDtypeStruct((2, bsz, seq, MLV_W), F32)]
    out_specs = [pl.BlockSpec((1, 1, ML_CHUNK, MLV_W),
                              lambda b, d, t: (d, b, jnp.where(d == 0, t, nc - 1 - t), 0))]
    if emit_state:
        out_shape += [jax.ShapeDtypeStruct((bsz, 2, ML_HEADS, ML_DV, ML_DK), F32),
                      jax.ShapeDtypeStruct((bsz, 2, ML_HEADS, ML_DK), F32),
                      jax.ShapeDtypeStruct((bsz, 2, ML_HEADS, LANES), F32)]
        out_specs += [pl.BlockSpec((1, 1, ML_HEADS, ML_DV, ML_DK), state5),
                      pl.BlockSpec((1, 1, ML_HEADS, ML_DK), state4),
                      pl.BlockSpec((1, 1, ML_HEADS, LANES), state4)]
    return pl.pallas_call(
        functools.partial(_mlstm_kernel, n_chunks=nc, has_init=init is not None, emit_state=emit_state),
        grid=(bsz, 2, nc),
        in_specs=in_specs,
        out_specs=out_specs,
        out_shape=out_shape,
        scratch_shapes=[pltpu.VMEM((ML_HEADS, ML_DK, ML_DV), F32),
                        pltpu.VMEM((ML_HEADS, 1, ML_DK), F32),
                        pltpu.VMEM((ML_HEADS, 1, LANES), F32)],
        compiler_params=_cparams(("parallel", "parallel", "arbitrary")),
        name="mlstm_state" if emit_state else "mlstm",
    )(*args)


def _rms(x, g):
    return x * lax.rsqrt(jnp.mean(x * x, axis=-1, keepdims=True) + EPS) * g


def _pack_bf16_pairs(x):
    n = x.shape[1] // 2
    lo = pltpu.bitcast(x[:, :n].astype(BF16).astype(F32), jnp.int32)
    hi = pltpu.bitcast(x[:, n:].astype(BF16).astype(F32), jnp.int32)
    return lax.shift_right_logical(lo, 16) | hi


def _unpack_bf16_pairs(p):
    lo = pltpu.bitcast(lax.shift_left(p, 16), F32)
    hi = pltpu.bitcast(p & jnp.int32(-65536), F32)
    return lo, hi


def _merge_kernel(att_ref, hf_ref, hb_ref, mo_ref, ga_ref, gm_ref, x_ref, mod_ref, nml_ref, gpm_ref, gpf_ref,
                  wua_ref, wum_ref, wo_ref, wr_ref, br_ref, x1_ref, h2_ref, idx_ref, gate_ref, cnt_ref):
    hsum = hf_ref[0, 0] + hb_ref[0, 0]
    parts = []
    for h in range(ML_HEADS):
        hh = hsum[:, h * ML_DV:(h + 1) * ML_DV]
        parts.append(hh * lax.rsqrt(jnp.mean(hh * hh, axis=-1, keepdims=True) + EPS))
    hn = jnp.concatenate(parts, axis=-1) * nml_ref[...]
    ml = (hn * jax.nn.sigmoid(mo_ref[0])).astype(BF16)
    a = jnp.dot(att_ref[0], wua_ref[...], preferred_element_type=F32)
    m = jnp.dot(ml, wum_ref[...], preferred_element_type=F32)
    z = (jax.nn.sigmoid(ga_ref[0]) * a + jax.nn.sigmoid(gm_ref[0]) * m).astype(BF16)
    mix = jnp.dot(z, wo_ref[...], preferred_element_type=F32)
    gate1 = mod_ref[0, 2:3, :]
    shift2 = mod_ref[0, 3:4, :]
    scale2 = mod_ref[0, 4:5, :]
    x1 = x_ref[0] + gate1 * _rms(mix, gpm_ref[...])
    x1_ref[0] = x1
    h2 = _rms(x1, gpf_ref[...]) * (1.0 + scale2) + shift2
    h2_ref[0] = _pack_bf16_pairs(h2)

    h2_hi = h2.astype(BF16)
    h2_lo = (h2 - h2_hi.astype(F32)).astype(BF16)
    wr = wr_ref[...]
    wr_hi = wr.astype(BF16)
    wr_lo = (wr - wr_hi.astype(F32)).astype(BF16)
    logits = (jnp.dot(h2_hi, wr_hi, preferred_element_type=F32)
              + jnp.dot(h2_lo, wr_hi, preferred_element_type=F32)
              + jnp.dot(h2_hi, wr_lo, preferred_element_type=F32)) + br_ref[...]
    lane = lax.broadcasted_iota(jnp.int32, logits.shape, 1).astype(F32)
    work = jnp.where(lane < N_EXPERTS, logits, -jnp.inf)
    idx_out = jnp.zeros(logits.shape, F32)
    val_out = jnp.zeros(logits.shape, F32)
    picked = jnp.zeros(logits.shape, F32)
    top0 = None
    esum = None
    for k in range(TOP_K):
        mx = jnp.max(work, axis=-1, keepdims=True)
        sel = jnp.min(jnp.where(work == mx, lane, float(LANES)), axis=-1, keepdims=True)
        if k == 0:
            top0 = mx
        e = jnp.exp(mx - top0)
        esum = e if k == 0 else esum + e
        idx_out = jnp.where(lane == k, sel, idx_out)
        val_out = jnp.where(lane == k, e, val_out)
        picked = jnp.where(lane == sel, 1.0, picked)
        work = jnp.where(lane == sel, -jnp.inf, work)
    idx_ref[0] = idx_out.astype(jnp.int32)
    gate_ref[0] = val_out / esum
    for r in range(logits.shape[0] // ROUTE_TILE):
        cnt_ref[0, r:r + 1, :] = jnp.sum(picked[r * ROUTE_TILE:(r + 1) * ROUTE_TILE], axis=0, keepdims=True)


def _merge(att, h, pj, x, mod, mod_row, p, tm):
    bsz, seq, d = x.shape
    tok3 = lambda b, i: (b, i, 0)
    wr = jnp.pad(p["w_router"].astype(F32), ((0, 0), (0, LANES - N_EXPERTS)))
    br = jnp.pad(p["b_router"].astype(F32), (0, LANES - N_EXPERTS)).reshape(1, LANES)
    row = lambda v: v.astype(F32).reshape(1, -1)
    in_specs = [pl.BlockSpec((1, tm, Q_W), tok3),
                pl.BlockSpec((1, 1, tm, MLV_W), lambda b, i: (0, b, i, 0)),
                pl.BlockSpec((1, 1, tm, MLV_W), lambda b, i: (1, b, i, 0)),
                pl.BlockSpec((1, tm, MLV_W), tok3), pl.BlockSpec((1, tm, d), tok3), pl.BlockSpec((1, tm, d), tok3),
                pl.BlockSpec((1, tm, d), tok3),
                pl.BlockSpec((1, 6, d), lambda b, i: (mod_row(b), 0, 0)),
                _const_spec((1, MLV_W)), _const_spec((1, d)), _const_spec((1, d)),
                _const_spec((Q_W, d)), _const_spec((MLV_W, d)), _const_spec((d, d)),
                _const_spec((d, LANES)), _const_spec((1, LANES))]
    out_shape = [jax.ShapeDtypeStruct((bsz, seq, d), F32), jax.ShapeDtypeStruct((bsz, seq, d // 2), jnp.int32),
                 jax.ShapeDtypeStruct((bsz, seq, LANES), jnp.int32), jax.ShapeDtypeStruct((bsz, seq, LANES), F32),
                 jax.ShapeDtypeStruct((bsz * (seq // tm), tm // ROUTE_TILE, LANES), F32)]
    out_specs = [pl.BlockSpec((1, tm, d), tok3), pl.BlockSpec((1, tm, d // 2), tok3),
                 pl.BlockSpec((1, tm, LANES), tok3), pl.BlockSpec((1, tm, LANES), tok3),
                 pl.BlockSpec((1, tm // ROUTE_TILE, LANES), lambda b, i: (b * (seq // tm) + i, 0, 0))]
    return pl.pallas_call(
        _merge_kernel,
        grid=(bsz, seq // tm),
        in_specs=in_specs,
        out_specs=out_specs,
        out_shape=out_shape,
        compiler_params=_cparams(("parallel", "parallel")),
        name="merge_router",
    )(att, h, h, pj["mo"], pj["ga"], pj["gm"], x, mod, row(p["norm_ml"]), row(p["g_post_mix"]),
      row(p["g_pre_ffn"]), p["w_up_att"].astype(BF16), p["w_up_ml"].astype(BF16), p["w_out"].astype(BF16), wr, br)


def _sc_gather_rows(table, idx):
    n = idx.shape[0]
    width = table.shape[1]
    chunk = SC_GATHER_CHUNK
    n_workers = SC_CORES * SC_SUBCORES
    rows_per_worker = n // n_workers
    n_pairs = rows_per_worker // (2 * chunk)
    assert n_pairs * 2 * chunk * n_workers == n
    mesh = plsc.VectorSubcoreMesh(core_axis_name="c", subcore_axis_name="s")

    def body(table_hbm, idx_hbm, out_hbm, idx_v, rows_v, gsem, osem):
        wid = lax.axis_index("s") * SC_CORES + lax.axis_index("c")
        base0 = wid * rows_per_worker

        def gather_copy(b):
            return pltpu.make_async_copy(table_hbm.at[idx_v.at[b]], rows_v.at[b], gsem.at[b])

        def write_copy(ci, b):
            base = pl.multiple_of(base0 + ci * chunk, 8)
            return pltpu.make_async_copy(rows_v.at[b], out_hbm.at[pl.ds(base, chunk)], osem.at[b])

        def issue(ci, b):
            base = pl.multiple_of(base0 + ci * chunk, 8)
            pltpu.sync_copy(idx_hbm.at[pl.ds(base, chunk)], idx_v.at[b])
            gather_copy(b).start()

        def finish(ci, b):
            gather_copy(b).wait()
            write_copy(ci, b).start()

        issue(0, 0)

        @pl.loop(0, n_pairs)
        def _(j):
            @pl.when(j > 0)
            def _():
                write_copy(2 * j - 1, 1).wait()
            issue(2 * j + 1, 1)
            finish(2 * j, 0)

            @pl.when(j < n_pairs - 1)
            def _():
                write_copy(2 * j, 0).wait()
                issue(2 * j + 2, 0)
            finish(2 * j + 1, 1)

        write_copy(2 * n_pairs - 2, 0).wait()
        write_copy(2 * n_pairs - 1, 1).wait()

    return pl.kernel(
        body, mesh=mesh,
        out_type=jax.ShapeDtypeStruct((n, width), table.dtype),
        scratch_types=[pltpu.VMEM((2, chunk), jnp.int32), pltpu.VMEM((2, chunk, width), table.dtype),
                       pltpu.SemaphoreType.DMA((2,)), pltpu.SemaphoreType.DMA((2,))],
    )(table, idx)


def _ffn_kernel(be_ref, nu_ref, xs_ref, w1_ref, b1_ref, w2_ref, b2_ref, ys_ref, w1b_s, w2b_s):
    i = pl.program_id(0)
    e = be_ref[i]
    e_prev = be_ref[jnp.maximum(i - 1, 0)]

    @pl.when((i == 0) | (e != e_prev))
    def _():
        w1b_s[...] = w1_ref[0].astype(BF16)
        w2b_s[...] = w2_ref[0].astype(BF16)

    @pl.when(i < nu_ref[0])
    def _():
        half = D_MODEL // 2
        x_lo, x_hi = _unpack_bf16_pairs(xs_ref[...])
        hmid = (jnp.dot(x_lo.astype(BF16), w1b_s[:half, :], preferred_element_type=F32)
                + jnp.dot(x_hi.astype(BF16), w1b_s[half:, :], preferred_element_type=F32)) + b1_ref[0]
        glu = jnp.minimum(hmid[:, :D_FF], SWIGLU_LIMIT)
        lin = jnp.clip(hmid[:, D_FF:], -SWIGLU_LIMIT, SWIGLU_LIMIT)
        act = ((lin + 1.0) * glu * jax.nn.sigmoid(SWIGLU_ALPHA * glu)).astype(BF16)
        y = jnp.dot(act, w2b_s[...], preferred_element_type=F32) + b2_ref[0]
        ys_ref[...] = _pack_bf16_pairs(y)

    @pl.when(i >= nu_ref[0])
    def _():
        ys_ref[...] = jnp.zeros(ys_ref.shape, jnp.int32)


def _sc_scatter_rows(tables, positions, n_rows):
    width = tables[0].shape[1]
    chunk = SC_GATHER_CHUNK
    n_workers = SC_CORES * SC_SUBCORES
    n_streams = len(tables)
    mesh = plsc.VectorSubcoreMesh(core_axis_name="c", subcore_axis_name="s")

    def body(*refs):
        table_refs = refs[0:2 * n_streams:2]
        pos_refs = refs[1:2 * n_streams:2]
        out_hbm, idx_v, rows_v, lsem, ssem = refs[2 * n_streams:]
        wid = lax.axis_index("s") * SC_CORES + lax.axis_index("c")

        for table_hbm, pos_hbm in zip(table_refs, pos_refs):
            tok_per_worker = table_hbm.shape[0] // n_workers
            n_pairs = tok_per_worker // (2 * chunk)
            assert n_pairs * 2 * chunk * n_workers == table_hbm.shape[0]
            base0 = wid * tok_per_worker

            def load_copy(ci, b):
                base = pl.multiple_of(base0 + ci * chunk, 8)
                return pltpu.make_async_copy(table_hbm.at[pl.ds(base, chunk)], rows_v.at[b], lsem.at[b])

            def scatter_copy(b, k):
                return pltpu.make_async_copy(rows_v.at[b], out_hbm.at[idx_v.at[b, k]], ssem.at[b])

            def load(ci, b):
                base = pl.multiple_of(base0 + ci * chunk, 8)
                load_copy(ci, b).start()
                for k in range(TOP_K):
                    pltpu.sync_copy(pos_hbm.at[k, pl.ds(base, chunk)], idx_v.at[b, k])

            def scatter(ci, b):
                load_copy(ci, b).wait()
                for k in range(TOP_K):
                    scatter_copy(b, k).start()

            def drain(b):
                for k in range(TOP_K):
                    scatter_copy(b, k).wait()

            load(0, 0)

            @pl.loop(0, n_pairs)
            def _(j):
                @pl.when(j > 0)
                def _():
                    drain(1)
                load(2 * j + 1, 1)
                scatter(2 * j, 0)

                @pl.when(j < n_pairs - 1)
                def _():
                    drain(0)
                    load(2 * j + 2, 0)
                scatter(2 * j + 1, 1)

            drain(0)
            drain(1)

    args = [a for pair in zip(tables, positions) for a in pair]
    return pl.kernel(
        body, mesh=mesh,
        out_type=jax.ShapeDtypeStruct((n_rows, width), tables[0].dtype),
        scratch_types=[pltpu.VMEM((2, TOP_K, chunk), jnp.int32), pltpu.VMEM((2, chunk, width), tables[0].dtype),
                       pltpu.SemaphoreType.DMA((2,)), pltpu.SemaphoreType.DMA((2,))],
    )(*args)


def _pos_kernel(idx_ref, base_ref, pos_ref):
    idx = idx_ref[...]
    rows = idx.shape[0]
    lane = lax.broadcasted_iota(jnp.int32, idx.shape, 1)
    hots = [jnp.where(lane == idx[:, k:k + 1], 1.0, 0.0) for k in range(TOP_K)]
    cnt = hots[0] + hots[1] + hots[2] + hots[3]
    ri = lax.broadcasted_iota(jnp.int32, (rows, rows), 0)
    ci = lax.broadcasted_iota(jnp.int32, (rows, rows), 1)
    earlier = jnp.where(ci < ri, 1.0, 0.0).astype(BF16)
    ahead = jnp.dot(earlier, cnt.astype(BF16), preferred_element_type=F32) + base_ref[0]
    posmat = jnp.zeros(idx.shape, F32)
    for k in range(TOP_K):
        posmat = jnp.where(lane == k, jnp.sum(hots[k] * ahead, axis=-1, keepdims=True), posmat)
    pos_ref[...] = posmat.T[:TOP_K, :].astype(jnp.int32)


def _route(idx_list, count_list):
    tiles = [c.shape[0] * c.shape[1] for c in count_list]
    n_tok = sum(tiles) * ROUTE_TILE
    n_blocks = n_tok * TOP_K // MOE_ROWS + N_EXPERTS
    cnt = jnp.concatenate([c.reshape(-1, LANES) for c in count_list], axis=0).astype(jnp.int32)
    tile_off = jnp.cumsum(cnt, axis=0) - cnt
    total = jnp.sum(cnt, axis=0)
    padded = (total + MOE_ROWS - 1) // MOE_ROWS * MOE_ROWS
    pend = jnp.cumsum(padded)
    base = ((pend - padded)[None, :] + tile_off).astype(F32)
    starts = jnp.arange(n_blocks, dtype=jnp.int32) * MOE_ROWS
    block_e = jnp.minimum(jnp.sum(pend[None, :N_EXPERTS] <= starts[:, None], axis=1), N_EXPERTS - 1).astype(jnp.int32)
    n_used = (pend[N_EXPERTS - 1] // MOE_ROWS).astype(jnp.int32).reshape(1)
    positions, first = [], 0
    for idx, n_tiles in zip(idx_list, tiles):
        stream_tok = n_tiles * ROUTE_TILE
        positions.append(pl.pallas_call(
            _pos_kernel,
            grid=(n_tiles,),
            in_specs=[pl.BlockSpec((ROUTE_TILE, LANES), lambda i: (i, 0)),
                      pl.BlockSpec((1, 1, LANES), lambda i: (i, 0, 0))],
            out_specs=pl.BlockSpec((TOP_K, ROUTE_TILE), lambda i: (0, i)),
            out_shape=jax.ShapeDtypeStruct((TOP_K, stream_tok), jnp.int32),
            compiler_params=_cparams(("parallel",)),
            name="route_pos",
        )(idx.reshape(stream_tok, LANES), base[first:first + n_tiles].reshape(n_tiles, 1, LANES)))
        first += n_tiles
    return block_e, n_used, positions


def _moe(h2p_list, idx_list, count_list, w1, b1, w2, b2):
    half = h2p_list[0].shape[1]
    d = 2 * half
    block_e, n_used, positions = _route(idx_list, count_list)
    n_blocks = block_e.shape[0]
    xs = _sc_scatter_rows(h2p_list, positions, n_blocks * MOE_ROWS)
    grid_spec = pltpu.PrefetchScalarGridSpec(
        num_scalar_prefetch=2,
        grid=(n_blocks,),
        in_specs=[pl.BlockSpec((MOE_ROWS, half), lambda i, be, nu: (i, 0)),
                  pl.BlockSpec((1, d, 2 * D_FF), lambda i, be, nu: (be[i], 0, 0)),
                  pl.BlockSpec((1, 1, 2 * D_FF), lambda i, be, nu: (be[i], 0, 0)),
                  pl.BlockSpec((1, D_FF, d), lambda i, be, nu: (be[i], 0, 0)),
                  pl.BlockSpec((1, 1, d), lambda i, be, nu: (be[i], 0, 0))],
        out_specs=pl.BlockSpec((MOE_ROWS, half), lambda i, be, nu: (i, 0)),
        scratch_shapes=[pltpu.VMEM((d, 2 * D_FF), BF16), pltpu.VMEM((D_FF, d), BF16)],
    )
    ys = pl.pallas_call(
        _ffn_kernel,
        grid_spec=grid_spec,
        out_shape=jax.ShapeDtypeStruct((n_blocks * MOE_ROWS, half), jnp.int32),
        compiler_params=_cparams(("arbitrary",)),
        name="moe_ffn",
    )(block_e, n_used, xs, w1, b1.reshape(N_EXPERTS, 1, -1), w2, b2.reshape(N_EXPERTS, 1, -1))
    return [_sc_gather_rows(ys, pos.reshape(-1)) for pos in positions]


def _final_kernel(x1_ref, y0_ref, y1_ref, y2_ref, y3_ref, gate_ref, mod_ref, g_ref, o_ref):
    gates = gate_ref[0]
    lo = hi = None
    for k, y_ref in enumerate((y0_ref, y1_ref, y2_ref, y3_ref)):
        y_lo, y_hi = _unpack_bf16_pairs(y_ref[0, 0])
        gk = gates[:, k:k + 1]
        lo = gk * y_lo if k == 0 else lo + gk * y_lo
        hi = gk * y_hi if k == 0 else hi + gk * y_hi
    y = jnp.concatenate([lo, hi], axis=-1)
    gate2 = mod_ref[0, 5:6, :]
    o_ref[0] = x1_ref[0] + gate2 * _rms(y, g_ref[...])


def _final(x1, yg, gate, mod, mod_row, g_post_ffn, tm):
    bsz, seq, d = x1.shape
    tok3 = lambda b, i: (b, i, 0)
    yg = yg.reshape(TOP_K, bsz, seq, d // 2)
    slot_specs = [pl.BlockSpec((1, 1, tm, d // 2), functools.partial(lambda b, i, k: (k, b, i, 0), k=k))
                  for k in range(TOP_K)]
    return pl.pallas_call(
        _final_kernel,
        grid=(bsz, seq // tm),
        in_specs=[pl.BlockSpec((1, tm, d), tok3)] + slot_specs +
                 [pl.BlockSpec((1, tm, LANES), tok3),
                  pl.BlockSpec((1, 6, d), lambda b, i: (mod_row(b), 0, 0)), _const_spec((1, d))],
        out_specs=pl.BlockSpec((1, tm, d), tok3),
        out_shape=jax.ShapeDtypeStruct((bsz, seq, d), F32),
        compiler_params=_cparams(("parallel", "parallel")),
        name="final_residual",
    )(x1, yg, yg, yg, yg, gate, mod, g_post_ffn.astype(F32).reshape(1, d))


def _stream(x, mod, mod_row, p, ctx_kv, init_state, rope, tm):
    bsz, seq, d = x.shape
    is_ctx = ctx_kv is None
    pj = _project(x, mod, p["g_pre_mix"], p["w_in"], p["b_gates"], mod_row, rope, is_ctx, tm)
    if is_ctx:
        att = _attention(pj["q"], None, None, pj["kd"], pj["vd"], p["sink"], window=False)
    else:
        att = _attention(pj["q"], pj["kd"], pj["vd"], ctx_kv[0], ctx_kv[1], p["sink"], window=True)
    ml = _mlstm(pj["mq"], pj["mk"], pj["mv"], pj["gates"], init_state, emit_state=is_ctx)
    x1, h2, idx, gate, tile_counts = _merge(att, ml[0], pj, x, mod, mod_row, p, tm)
    moe_in = (h2.reshape(bsz * seq, d // 2), idx, tile_counts)
    return x1, gate, moe_in, pj, ml


def kernel(x_prompt, x_sample, c, cache_k, cache_v, state_C, state_n, state_m, c_ctx, w_ada, b_ada, g_pre_mix,
           w_in, b_gates, attn_sink, norm_mlstm, w_up_att, w_up_ml, w_out, g_post_mix, g_pre_ffn, w_router,
           b_router, w1, b1, w2, b2, g_post_ffn):
    depth = w_ada.shape[0]
    n_dec = c.shape[0]
    cond = jnp.concatenate([c_ctx[None, :], c], axis=0).astype(F32)
    cond = jnp.pad(cond, ((0, 16 - cond.shape[0]), (0, 0)))
    y_prompt, y_sample = x_prompt, x_sample
    ks_, vs_, cs_, ns_, ms_ = [], [], [], [], []
    for l in range(depth):
        p = dict(g_pre_mix=g_pre_mix[l], w_in=w_in[l], b_gates=b_gates[l], sink=attn_sink[l], norm_ml=norm_mlstm[l],
                 w_up_att=w_up_att[l], w_up_ml=w_up_ml[l], w_out=w_out[l], g_post_mix=g_post_mix[l],
                 g_pre_ffn=g_pre_ffn[l], w_router=w_router[l], b_router=b_router[l], w1=w1[l], b1=b1[l],
                 w2=w2[l], b2=b2[l], g_post_ffn=g_post_ffn[l])
        mod = _adaln(cond, w_ada[l], b_ada[l]).reshape(16, 6, D_MODEL)
        row_p, row_s = (lambda b: 0), (lambda b: b + 1)
        x1_p, gate_p, moe_p, pj, ml = _stream(y_prompt, mod, row_p, p, None, None, rope=False, tm=256)
        bsz, seq = x_prompt.shape[:2]
        nat = pj["nat"]
        ks_.append(nat[..., :KV_W].reshape(bsz, seq, ATT_KV_HEADS, HEAD_DIM))
        vs_.append(nat[..., KV_W:].reshape(bsz, seq, ATT_KV_HEADS, HEAD_DIM))
        cs_.append(ml[1])
        ns_.append(ml[2])
        ms_.append(ml[3][..., 0])
        ctx_kv = (_dup_heads(cache_k[:, l]), _dup_heads(cache_v[:, l]))
        init = (state_C[:, l], state_n[:, l], state_m[:, l])
        x1_s, gate_s, moe_s, _, _ = _stream(y_sample, mod, row_s, p, ctx_kv, init, rope=True, tm=512)
        yg_p, yg_s = _moe([moe_p[0], moe_s[0]], [moe_p[1], moe_s[1]], [moe_p[2], moe_s[2]],
                          p["w1"], p["b1"].astype(F32), p["w2"], p["b2"].astype(F32))
        y_prompt = _final(x1_p, yg_p, gate_p, mod, row_p, p["g_post_ffn"], 256)
        y_sample = _final(x1_s, yg_s, gate_s, mod, row_s, p["g_post_ffn"], 512)
    return (y_prompt, y_sample, jnp.stack(ks_, axis=1), jnp.stack(vs_, axis=1), jnp.stack(cs_, axis=1),
            jnp.stack(ns_, axis=1), jnp.stack(ms_, axis=1))
```

```python
import functools

import numpy as np
import jax
import jax.numpy as jnp
from jax import lax
from jax.experimental import pallas as pl
from jax.experimental.pallas import tpu as pltpu
from jax.experimental.pallas import tpu_sc as plsc

F32 = jnp.float32
BF16 = jnp.bfloat16

D_MODEL = 1024
GRID_W = 64
ATT_HEADS = 8
ATT_KV_HEADS = 2
ATT_GROUP = ATT_HEADS // ATT_KV_HEADS
HEAD_DIM = 64
BLOCK = 128
ROPE_THETA = 10000.0
AXIS_FREQS = HEAD_DIM // 4
ML_HEADS = 4
ML_DK = 128
ML_DV = 256
ML_CHUNK = 128
N_EXPERTS = 32
TOP_K = 4
D_FF = 1024
SWIGLU_ALPHA = 1.702
SWIGLU_LIMIT = 7.0
EPS = 1e-6
NEG = -1e30

Q_W = ATT_HEADS * HEAD_DIM
KV_W = ATT_KV_HEADS * HEAD_DIM
MLQK_W = ML_HEADS * ML_DK
MLV_W = ML_HEADS * ML_DV
GATE_W = 4 * ML_HEADS

LANES = 128
VMEM_LIMIT = 56 * 1024 * 1024
MOE_ROWS = 256
SC_CORES = 2
SC_SUBCORES = 16
SC_GATHER_CHUNK = 32
ROUTE_TILE = 256


def _cparams(sem):
    return pltpu.CompilerParams(dimension_semantics=sem, vmem_limit_bytes=VMEM_LIMIT)


def _const_spec(shape):
    nd = len(shape)
    return pl.BlockSpec(shape, lambda *_: (0,) * nd)


def _adaln_kernel(cond_ref, w_ref, b_ref, o_ref):
    cnd = cond_ref[...]
    act = cnd * jax.nn.sigmoid(cnd)
    o_ref[...] = jnp.dot(act.astype(BF16), w_ref[...].astype(BF16),
                         preferred_element_type=F32) + b_ref[...]


def _adaln(cond, w_ada, b_ada):
    rows, d = cond.shape
    n = w_ada.shape[1]
    tn = 1536
    return pl.pallas_call(
        _adaln_kernel,
        grid=(n // tn,),
        in_specs=[_const_spec((rows, d)),
                  pl.BlockSpec((d, tn), lambda j: (0, j)),
                  pl.BlockSpec((1, tn), lambda j: (0, j))],
        out_specs=pl.BlockSpec((rows, tn), lambda j: (0, j)),
        out_shape=jax.ShapeDtypeStruct((rows, n), F32),
        compiler_params=_cparams(("parallel",)),
        name="adaln",
    )(cond, w_ada, b_ada.reshape(1, n))


def _pack_w_in(w_in, b_gates, rope, with_nat):
    o_q, o_k, o_v = 0, Q_W, Q_W + KV_W
    o_mq = Q_W + 2 * KV_W
    o_mk = o_mq + MLQK_W
    o_mv = o_mk + MLQK_W
    o_g = o_mv + MLV_W
    o_mo = o_g + GATE_W
    o_ga = o_mo + MLV_W
    o_gm = o_ga + D_MODEL
    half = HEAD_DIM // 2
    zeros64 = jnp.zeros((D_MODEL, HEAD_DIM), w_in.dtype)

    def head_cols(base, h, rot):
        lo = base + h * HEAD_DIM
        if rot:
            return [w_in[:, lo + half:lo + HEAD_DIM], w_in[:, lo:lo + half]]
        return [w_in[:, lo:lo + HEAD_DIM]]

    def q_cols(rot):
        cols = []
        for h in range(ATT_HEADS):
            hc = head_cols(o_q, h, rot)
            cols += (hc + [zeros64]) if h % 2 == 0 else ([zeros64] + hc)
        return cols

    def dup_cols(base, rot):
        cols = []
        for h in range(ATT_KV_HEADS):
            hc = head_cols(base, h, rot)
            cols += hc + hc
        return cols

    segs, cols, off = {}, [], 0

    def add(name, cl):
        nonlocal off
        width = sum(c.shape[1] for c in cl)
        segs[name] = (off, off + width)
        cols.extend(cl)
        off += width

    add("q", q_cols(False))
    if rope:
        add("qrot", q_cols(True))
    add("kd", dup_cols(o_k, False))
    if rope:
        add("kdrot", dup_cols(o_k, True))
    add("vd", dup_cols(o_v, False))
    if with_nat:
        add("nat", [w_in[:, o_k:o_k + 2 * KV_W]])
    add("mq", [w_in[:, o_mq:o_mq + MLQK_W]])
    add("mk", [w_in[:, o_mk:o_mk + MLQK_W]])
    add("mv", [w_in[:, o_mv:o_mv + MLV_W]])
    add("mo", [w_in[:, o_mo:o_mo + MLV_W]])
    add("ga", [w_in[:, o_ga:o_ga + D_MODEL]])
    add("gm", [w_in[:, o_gm:o_gm + D_MODEL]])
    add("gates", [w_in[:, o_g:o_g + GATE_W], jnp.zeros((D_MODEL, LANES - GATE_W), w_in.dtype)])
    wp = jnp.concatenate(cols, axis=1).astype(BF16)
    bg = jnp.pad(b_gates.astype(F32), (0, LANES - GATE_W)).reshape(1, LANES)
    return wp, bg, segs


def _rope_tables(n_tok):
    rows = n_tok // GRID_W
    row = np.repeat(np.arange(rows), GRID_W).astype(np.float32)
    col = np.tile(np.arange(GRID_W), rows).astype(np.float32)
    inv = (np.float32(ROPE_THETA) ** (-np.arange(AXIS_FREQS, dtype=np.float32) / AXIS_FREQS)).astype(np.float32)
    ang = np.concatenate([row[:, None] * inv, col[:, None] * inv], axis=-1).astype(np.float32)
    cos, sin = np.cos(ang), np.sin(ang)
    c64 = np.concatenate([cos, cos], axis=-1)
    s64 = np.concatenate([-sin, sin], axis=-1)
    return (jnp.asarray(np.tile(c64, (1, 2)), F32), jnp.asarray(np.tile(s64, (1, 2)), F32))


def _proj_kernel(*refs, segs, rope, with_nat):
    it = iter(refs)
    x_ref, mod_ref, g_ref, w_ref, bg_ref = (next(it) for _ in range(5))
    cos_ref = sin_ref = None
    if rope:
        cos_ref, sin_ref = next(it), next(it)
    q_ref, kd_ref, vd_ref = next(it), next(it), next(it)
    nat_ref = next(it) if with_nat else None
    mq_ref, mk_ref, mv_ref, gates_ref, mo_ref, ga_ref, gm_ref = (next(it) for _ in range(7))

    x = x_ref[0]
    shift = mod_ref[0, 0:1, :]
    scale = mod_ref[0, 1:2, :]
    ms = jnp.mean(x * x, axis=-1, keepdims=True)
    xn = x * lax.rsqrt(ms + EPS) * g_ref[...]
    xb = (xn * (1.0 + scale) + shift).astype(BF16)

    def seg(name):
        lo, hi = segs[name]
        return jnp.dot(xb, w_ref[:, lo:hi], preferred_element_type=F32)

    uq = seg("q")
    ukd = seg("kd")
    if rope:
        uqr = seg("qrot")
        ukr = seg("kdrot")
        cs = cos_ref[...]
        sn = sin_ref[...]
        for h in range(ATT_HEADS):
            sl = slice(h * LANES, (h + 1) * LANES)
            q_ref[0, h] = ((uq[:, sl] * cs + uqr[:, sl] * sn) * HEAD_DIM ** -0.5).astype(BF16)
        for h in range(ATT_KV_HEADS):
            sl = slice(h * LANES, (h + 1) * LANES)
            kd_ref[0, :, sl] = (ukd[:, sl] * cs + ukr[:, sl] * sn).astype(BF16)
    else:
        for h in range(ATT_HEADS):
            sl = slice(h * LANES, (h + 1) * LANES)
            q_ref[0, h] = (uq[:, sl] * HEAD_DIM ** -0.5).astype(BF16)
        kd_ref[0] = ukd.astype(BF16)
    vd_ref[0] = seg("vd").astype(BF16)
    if with_nat:
        nat_ref[0] = seg("nat")
    mq_ref[0] = (seg("mq") * ML_DK ** -0.5).astype(BF16)
    mk_ref[0] = seg("mk").astype(BF16)
    mv_ref[0] = seg("mv").astype(BF16)
    gates_ref[0] = seg("gates") + bg_ref[...]
    mo_ref[0] = seg("mo")
    ga_ref[0] = seg("ga")
    gm_ref[0] = seg("gm")


def _project(x, mod, g_pre, w_in, b_gates, mod_row, rope, with_nat, tm):
    bsz, seq, d = x.shape
    wp, bg, segs = _pack_w_in(w_in, b_gates, rope, with_nat)
    nw = wp.shape[1]
    tok3 = lambda b, i: (b, i, 0)
    in_specs = [pl.BlockSpec((1, tm, d), tok3),
                pl.BlockSpec((1, 6, d), lambda b, i: (mod_row(b), 0, 0)),
                _const_spec((1, d)),
                pl.BlockSpec((d, nw), lambda b, i: (0, 0), pipeline_mode=pl.Buffered(1)),
                _const_spec((1, LANES))]
    args = [x, mod, g_pre.reshape(1, d), wp, bg]
    if rope:
        cos_t, sin_t = _rope_tables(seq)
        in_specs += [pl.BlockSpec((tm, LANES), lambda b, i: (i, 0))] * 2
        args += [cos_t, sin_t]

    def tok_out(width, dtype):
        return (jax.ShapeDtypeStruct((bsz, seq, width), dtype), pl.BlockSpec((1, tm, width), tok3))

    outs = [(jax.ShapeDtypeStruct((bsz, ATT_HEADS, seq, LANES), BF16),
             pl.BlockSpec((1, ATT_HEADS, tm, LANES), lambda b, i: (b, 0, i, 0))),
            tok_out(2 * LANES, BF16), tok_out(2 * LANES, BF16)]
    if with_nat:
        outs.append(tok_out(2 * KV_W, F32))
    outs += [tok_out(MLQK_W, BF16), tok_out(MLQK_W, BF16), tok_out(MLV_W, BF16), tok_out(LANES, F32),
             tok_out(MLV_W, F32), tok_out(D_MODEL, F32), tok_out(D_MODEL, F32)]
    res = pl.pallas_call(
        functools.partial(_proj_kernel, segs=segs, rope=rope, with_nat=with_nat),
        grid=(bsz, seq // tm),
        in_specs=in_specs,
        out_specs=[o[1] for o in outs],
        out_shape=[o[0] for o in outs],
        compiler_params=_cparams(("parallel", "parallel")),
        name="proj_rope" if rope else "proj_ctx",
    )(*args)
    names = ["q", "kd", "vd"] + (["nat"] if with_nat else []) + ["mq", "mk", "mv", "gates", "mo", "ga", "gm"]
    return dict(zip(names, res))


def _attn_kernel(*refs, window, n_blocks, n_ctx):
    it = iter(refs)
    sink_ref, q_ref = next(it), next(it)
    if window:
        kp_ref, kc_ref, kn_ref, vp_ref, vc_ref, vn_ref = (next(it) for _ in range(6))
    ck_ref, cv_ref, o_ref = next(it), next(it), next(it)
    j = pl.program_id(1)
    n_win = 3 * BLOCK if window else 0
    n_keys = n_win + n_ctx
    rows = ATT_GROUP * BLOCK

    row_id = lax.broadcasted_iota(jnp.int32, (rows, 1), 0)
    if window:
        rr = lax.broadcasted_iota(jnp.int32, (rows, n_keys), 0) & (BLOCK - 1)
        cc = lax.broadcasted_iota(jnp.int32, (rows, n_keys), 1)
        dd = cc - rr
        lo_b = jnp.where(j == 0, BLOCK, 0)
        hi_b = jnp.where(j == n_blocks - 1, 2 * BLOCK - 1, 3 * BLOCK - 1)
        in_band = (dd >= 0) & (dd <= 2 * BLOCK) & (cc >= lo_b) & (cc <= hi_b)
        mask = in_band | (cc >= n_win)
    lane_lo = lax.broadcasted_iota(jnp.int32, (BLOCK, LANES), 1) < HEAD_DIM

    for kvh in range(ATT_KV_HEADS):
        sl = slice(kvh * LANES, (kvh + 1) * LANES)
        if window:
            keys = jnp.concatenate([kp_ref[0, :, sl], kc_ref[0, :, sl], kn_ref[0, :, sl], ck_ref[0, :, sl]], axis=0)
            vals = jnp.concatenate([vp_ref[0, :, sl], vc_ref[0, :, sl], vn_ref[0, :, sl], cv_ref[0, :, sl]], axis=0)
        else:
            keys = ck_ref[0, :, sl]
            vals = cv_ref[0, :, sl]
        qs = q_ref[0, kvh * ATT_GROUP:(kvh + 1) * ATT_GROUP].reshape(rows, LANES)
        s = lax.dot_general(qs, keys, (((1,), (1,)), ((), ())), preferred_element_type=F32)
        if window:
            s = jnp.where(mask, s, NEG)
        snk = jnp.full((rows, 1), sink_ref[kvh * ATT_GROUP + ATT_GROUP - 1], F32)
        for g in range(ATT_GROUP - 2, -1, -1):
            snk = jnp.where(row_id < (g + 1) * BLOCK, sink_ref[kvh * ATT_GROUP + g], snk)
        m = jnp.maximum(jnp.max(s, axis=-1, keepdims=True), snk)
        p = jnp.exp(s - m)
        den = jnp.sum(p, axis=-1, keepdims=True) + jnp.exp(snk - m)
        o = jnp.dot(p.astype(BF16), vals, preferred_element_type=F32) / den
        for pair in range(ATT_GROUP // 2):
            even = o[(2 * pair) * BLOCK:(2 * pair + 1) * BLOCK]
            odd = o[(2 * pair + 1) * BLOCK:(2 * pair + 2) * BLOCK]
            col = (kvh * (ATT_GROUP // 2) + pair) * LANES
            o_ref[0, :, col:col + LANES] = jnp.where(lane_lo, even, odd).astype(BF16)


def _attention(q, kd, vd, ckd, cvd, sink, window):
    bsz, _, seq, _ = q.shape
    nb = seq // BLOCK
    n_ctx = ckd.shape[1]
    in_specs = [pl.BlockSpec(memory_space=pltpu.SMEM),
                pl.BlockSpec((1, ATT_HEADS, BLOCK, LANES), lambda b, j: (b, 0, j, 0))]
    args = [sink.astype(F32), q]
    if window:
        prev = lambda b, j: (b, jnp.maximum(j - 1, 0), 0)
        cur = lambda b, j: (b, j, 0)
        nxt = lambda b, j: (b, jnp.minimum(j + 1, nb - 1), 0)
        for arr in (kd, vd):
            in_specs += [pl.BlockSpec((1, BLOCK, 2 * LANES), f) for f in (prev, cur, nxt)]
            args += [arr, arr, arr]
    in_specs += [pl.BlockSpec((1, n_ctx, 2 * LANES), lambda b, j: (b, 0, 0))] * 2
    args += [ckd, cvd]
    return pl.pallas_call(
        functools.partial(_attn_kernel, window=window, n_blocks=nb, n_ctx=n_ctx),
        grid=(bsz, nb),
        in_specs=in_specs,
        out_specs=pl.BlockSpec((1, BLOCK, Q_W), lambda b, j: (b, j, 0)),
        out_shape=jax.ShapeDtypeStruct((bsz, seq, Q_W), BF16),
        compiler_params=_cparams(("parallel", "parallel")),
        name="attn_window" if window else "attn_ctx",
    )(*args)


def _dup_heads(t):
    b, l = t.shape[:2]
    return jnp.concatenate([t, t], axis=-1).reshape(b, l, ATT_KV_HEADS * LANES).astype(BF16)


def _split3(x):
    hi = x.astype(BF16)
    r1 = x - hi.astype(F32)
    mid = r1.astype(BF16)
    lo = (r1 - mid.astype(F32)).astype(BF16)
    return hi, mid, lo


def _log_sigmoid(x):
    return jnp.minimum(x, 0.0) - jnp.log1p(jnp.exp(-jnp.abs(x)))


def _mlstm_kernel(*refs, n_chunks, has_init, emit_state):
    it = iter(refs)
    qkvg = [[next(it) for _ in range(4)] for _ in range(2)]
    if has_init:
        c0_ref, n0_ref, m0_ref = next(it), next(it), next(it)
    h_refs = [next(it), next(it)]
    if emit_state:
        co_ref, no_ref, mo_ref = next(it), next(it), next(it)
    ct_s, n_s, m_s = next(it), next(it), next(it)
    t = pl.program_id(1)
    L = ML_CHUNK

    @pl.when(t == 0)
    def _():
        for d in range(2):
            for h in range(ML_HEADS):
                if has_init:
                    ct_s[d, h] = c0_ref[0, d, h].T
                    n_s[d, h] = n0_ref[0, d, h:h + 1, :]
                    m_s[d, h] = m0_ref[0, d, h:h + 1, :]
                else:
                    ct_s[d, h] = jnp.zeros((ML_DK, ML_DV), F32)
                    n_s[d, h] = jnp.zeros((1, ML_DK), F32)
                    m_s[d, h] = jnp.zeros((1, LANES), F32)

    ri = lax.broadcasted_iota(jnp.int32, (L, L), 0)
    ci = lax.broadcasted_iota(jnp.int32, (L, L), 1)

    for d in range(2):
        q_ref, k_ref, v_ref, g_ref = qkvg[d]
        tri = (ci <= ri) if d == 0 else (ci >= ri)
        tri_b = jnp.where(tri, 1.0, 0.0).astype(BF16)
        g = g_ref[0]
        lf = _log_sigmoid(g)
        g_t = g.T
        lf_t = lf.T
        bc = sum(jnp.dot(tri_b, part, preferred_element_type=F32) for part in _split3(lf))
        br = sum(lax.dot_general(part, tri_b, (((1,), (1,)), ((), ())), preferred_element_type=F32)
                 for part in _split3(lf_t))
        tot = jnp.sum(lf, axis=0, keepdims=True)

        for h in range(ML_HEADS):
            icol_i = 2 * ML_HEADS * d + h
            fcol = icol_i + ML_HEADS
            bcol = bc[:, fcol:fcol + 1]
            brow = br[fcol:fcol + 1, :]
            irow = g_t[icol_i:icol_i + 1, :]
            icol = g[:, icol_i:icol_i + 1]
            b_last = tot[:, fcol:fcol + 1]
            m_prev = m_s[d, h][:, 0:1]
            n_prev = n_s[d, h]
            ct_prev = ct_s[d, h]
            qh = q_ref[0, :, h * ML_DK:(h + 1) * ML_DK]
            kh = k_ref[0, :, h * ML_DK:(h + 1) * ML_DK]
            vh = v_ref[0, :, h * ML_DV:(h + 1) * ML_DV]

            rel = jnp.where(tri, irow - brow, NEG)
            inter = bcol + m_prev
            m_t = jnp.maximum(inter, bcol + jnp.max(rel, axis=-1, keepdims=True))
            w = jnp.exp(rel + (bcol - m_t))
            w_inter = jnp.exp(inter - m_t)
            s = lax.dot_general(qh, kh, (((1,), (1,)), ((), ())), preferred_element_type=F32) * w
            cq = jnp.dot(qh, ct_prev.astype(BF16), preferred_element_type=F32) * w_inter
            num = jnp.dot(s.astype(BF16), vh, preferred_element_type=F32) + cq
            qn = jnp.sum(qh.astype(F32) * n_prev, axis=-1, keepdims=True)
            den = jnp.sum(s, axis=-1, keepdims=True) + w_inter * qn
            h_refs[d][0, :, h * ML_DV:(h + 1) * ML_DV] = num / jnp.maximum(jnp.abs(den), jnp.exp(-m_t))

            a = b_last - bcol + icol
            m_new = jnp.maximum(b_last + m_prev, jnp.max(a, axis=0, keepdims=True))
            wk = jnp.exp(a - m_new)
            decay = jnp.exp(b_last + m_prev - m_new)
            kw = kh.astype(F32) * wk
            ct_new = decay * ct_prev + jnp.dot(kw.T.astype(BF16), vh, preferred_element_type=F32)
            n_new = decay * n_prev + jnp.sum(kw, axis=0, keepdims=True)
            ct_s[d, h] = ct_new
            n_s[d, h] = n_new
            m_s[d, h] = jnp.broadcast_to(m_new, (1, LANES))
            if emit_state:
                @pl.when(t == n_chunks - 1)
                def _():
                    co_ref[0, d, h] = ct_new.T
                    no_ref[0, d, h:h + 1, :] = n_new
                    mo_ref[0, d, h:h + 1, :] = jnp.broadcast_to(m_new, (1, LANES))


def _mlstm(mq, mk, mv, gates, init, emit_state):
    bsz, seq, _ = mq.shape
    nc = seq // ML_CHUNK
    fwd = lambda b, t: (b, t, 0)
    bwd = lambda b, t: (b, nc - 1 - t, 0)
    state5 = lambda b, t: (b, 0, 0, 0, 0)
    state4 = lambda b, t: (b, 0, 0, 0)
    in_specs, args = [], []
    for chunk in (fwd, bwd):
        in_specs += [pl.BlockSpec((1, ML_CHUNK, MLQK_W), chunk), pl.BlockSpec((1, ML_CHUNK, MLQK_W), chunk),
                     pl.BlockSpec((1, ML_CHUNK, MLV_W), chunk), pl.BlockSpec((1, ML_CHUNK, LANES), chunk)]
        args += [mq, mk, mv, gates]
    if init is not None:
        c0, n0, m0 = init
        in_specs += [pl.BlockSpec((1, 2, ML_HEADS, ML_DV, ML_DK), state5),
                     pl.BlockSpec((1, 2, ML_HEADS, ML_DK), state4),
                     pl.BlockSpec((1, 2, ML_HEADS, LANES), state4)]
        args += [c0.astype(F32), n0.astype(F32),
                 jnp.broadcast_to(m0.astype(F32)[..., None], m0.shape + (LANES,))]
    out_shape = [jax.ShapeDtypeStruct((bsz, seq, MLV_W), F32)] * 2
    out_specs = [pl.BlockSpec((1, ML_CHUNK, MLV_W), fwd), pl.BlockSpec((1, ML_CHUNK, MLV_W), bwd)]
    if emit_state:
        out_shape += [jax.ShapeDtypeStruct((bsz, 2, ML_HEADS, ML_DV, ML_DK), F32),
                      jax.ShapeDtypeStruct((bsz, 2, ML_HEADS, ML_DK), F32),
                      jax.ShapeDtypeStruct((bsz, 2, ML_HEADS, LANES), F32)]
        out_specs += [pl.BlockSpec((1, 2, ML_HEADS, ML_DV, ML_DK), state5),
                      pl.BlockSpec((1, 2, ML_HEADS, ML_DK), state4),
                      pl.BlockSpec((1, 2, ML_HEADS, LANES), state4)]
    return pl.pallas_call(
        functools.partial(_mlstm_kernel, n_chunks=nc, has_init=init is not None, emit_state=emit_state),
        grid=(bsz, nc),
        in_specs=in_specs,
        out_specs=out_specs,
        out_shape=out_shape,
        scratch_shapes=[pltpu.VMEM((2, ML_HEADS, ML_DK, ML_DV), F32),
                        pltpu.VMEM((2, ML_HEADS, 1, ML_DK), F32),
                        pltpu.VMEM((2, ML_HEADS, 1, LANES), F32)],
        compiler_params=_cparams(("parallel", "arbitrary")),
        name="mlstm_state" if emit_state else "mlstm",
    )(*args)


def _rms(x, g):
    return x * lax.rsqrt(jnp.mean(x * x, axis=-1, keepdims=True) + EPS) * g


def _pack_bf16_pairs(x):
    n = x.shape[1] // 2
    lo = pltpu.bitcast(x[:, :n].astype(BF16).astype(F32), jnp.int32)
    hi = pltpu.bitcast(x[:, n:].astype(BF16).astype(F32), jnp.int32)
    return lax.shift_right_logical(lo, 16) | hi


def _unpack_bf16_pairs(p):
    lo = pltpu.bitcast(lax.shift_left(p, 16), F32)
    hi = pltpu.bitcast(p & jnp.int32(-65536), F32)
    return lo, hi


def _merge_kernel(att_ref, hf_ref, hb_ref, mo_ref, ga_ref, gm_ref, x_ref, mod_ref, nml_ref, gpm_ref, gpf_ref,
                  wua_ref, wum_ref, wo_ref, wr_ref, br_ref, x1_ref, h2_ref, idx_ref, gate_ref, cnt_ref):
    hsum = hf_ref[0] + hb_ref[0]
    parts = []
    for h in range(ML_HEADS):
        hh = hsum[:, h * ML_DV:(h + 1) * ML_DV]
        parts.append(hh * lax.rsqrt(jnp.mean(hh * hh, axis=-1, keepdims=True) + EPS))
    hn = jnp.concatenate(parts, axis=-1) * nml_ref[...]
    ml = (hn * jax.nn.sigmoid(mo_ref[0])).astype(BF16)
    a = jnp.dot(att_ref[0], wua_ref[...], preferred_element_type=F32)
    m = jnp.dot(ml, wum_ref[...], preferred_element_type=F32)
    z = (jax.nn.sigmoid(ga_ref[0]) * a + jax.nn.sigmoid(gm_ref[0]) * m).astype(BF16)
    mix = jnp.dot(z, wo_ref[...], preferred_element_type=F32)
    gate1 = mod_ref[0, 2:3, :]
    shift2 = mod_ref[0, 3:4, :]
    scale2 = mod_ref[0, 4:5, :]
    x1 = x_ref[0] + gate1 * _rms(mix, gpm_ref[...])
    x1_ref[0] = x1
    h2 = _rms(x1, gpf_ref[...]) * (1.0 + scale2) + shift2
    h2_ref[0] = _pack_bf16_pairs(h2)

    h2_hi = h2.astype(BF16)
    h2_lo = (h2 - h2_hi.astype(F32)).astype(BF16)
    wr = wr_ref[...]
    wr_hi = wr.astype(BF16)
    wr_lo = (wr - wr_hi.astype(F32)).astype(BF16)
    logits = (jnp.dot(h2_hi, wr_hi, preferred_element_type=F32)
              + jnp.dot(h2_lo, wr_hi, preferred_element_type=F32)
              + jnp.dot(h2_hi, wr_lo, preferred_element_type=F32)) + br_ref[...]
    lane = lax.broadcasted_iota(jnp.int32, logits.shape, 1).astype(F32)
    work = jnp.where(lane < N_EXPERTS, logits, -jnp.inf)
    idx_out = jnp.zeros(logits.shape, F32)
    val_out = jnp.zeros(logits.shape, F32)
    picked = jnp.zeros(logits.shape, F32)
    top0 = None
    esum = None
    for k in range(TOP_K):
        mx = jnp.max(work, axis=-1, keepdims=True)
        sel = jnp.min(jnp.where(work == mx, lane, float(LANES)), axis=-1, keepdims=True)
        if k == 0:
            top0 = mx
        e = jnp.exp(mx - top0)
        esum = e if k == 0 else esum + e
        idx_out = jnp.where(lane == k, sel, idx_out)
        val_out = jnp.where(lane == k, e, val_out)
        picked = jnp.where(lane == sel, 1.0, picked)
        work = jnp.where(lane == sel, -jnp.inf, work)
    idx_ref[0] = idx_out.astype(jnp.int32)
    gate_ref[0] = val_out / esum
    for r in range(logits.shape[0] // ROUTE_TILE):
        cnt_ref[0, r:r + 1, :] = jnp.sum(picked[r * ROUTE_TILE:(r + 1) * ROUTE_TILE], axis=0, keepdims=True)


def _merge(att, hf, hb, pj, x, mod, mod_row, p, tm):
    bsz, seq, d = x.shape
    tok3 = lambda b, i: (b, i, 0)
    wr = jnp.pad(p["w_router"].astype(F32), ((0, 0), (0, LANES - N_EXPERTS)))
    br = jnp.pad(p["b_router"].astype(F32), (0, LANES - N_EXPERTS)).reshape(1, LANES)
    row = lambda v: v.astype(F32).reshape(1, -1)
    in_specs = [pl.BlockSpec((1, tm, Q_W), tok3),
                pl.BlockSpec((1, tm, MLV_W), tok3), pl.BlockSpec((1, tm, MLV_W), tok3),
                pl.BlockSpec((1, tm, MLV_W), tok3), pl.BlockSpec((1, tm, d), tok3), pl.BlockSpec((1, tm, d), tok3),
                pl.BlockSpec((1, tm, d), tok3),
                pl.BlockSpec((1, 6, d), lambda b, i: (mod_row(b), 0, 0)),
                _const_spec((1, MLV_W)), _const_spec((1, d)), _const_spec((1, d)),
                _const_spec((Q_W, d)), _const_spec((MLV_W, d)), _const_spec((d, d)),
                _const_spec((d, LANES)), _const_spec((1, LANES))]
    out_shape = [jax.ShapeDtypeStruct((bsz, seq, d), F32), jax.ShapeDtypeStruct((bsz, seq, d // 2), jnp.int32),
                 jax.ShapeDtypeStruct((bsz, seq, LANES), jnp.int32), jax.ShapeDtypeStruct((bsz, seq, LANES), F32),
                 jax.ShapeDtypeStruct((bsz * (seq // tm), tm // ROUTE_TILE, LANES), F32)]
    out_specs = [pl.BlockSpec((1, tm, d), tok3), pl.BlockSpec((1, tm, d // 2), tok3),
                 pl.BlockSpec((1, tm, LANES), tok3), pl.BlockSpec((1, tm, LANES), tok3),
                 pl.BlockSpec((1, tm // ROUTE_TILE, LANES), lambda b, i: (b * (seq // tm) + i, 0, 0))]
    return pl.pallas_call(
        _merge_kernel,
        grid=(bsz, seq // tm),
        in_specs=in_specs,
        out_specs=out_specs,
        out_shape=out_shape,
        compiler_params=_cparams(("parallel", "parallel")),
        name="merge_router",
    )(att, hf, hb, pj["mo"], pj["ga"], pj["gm"], x, mod, row(p["norm_ml"]), row(p["g_post_mix"]),
      row(p["g_pre_ffn"]), p["w_up_att"].astype(BF16), p["w_up_ml"].astype(BF16), p["w_out"].astype(BF16), wr, br)


def _sc_gather_rows(table, idx):
    n = idx.shape[0]
    width = table.shape[1]
    chunk = SC_GATHER_CHUNK
    n_workers = SC_CORES * SC_SUBCORES
    rows_per_worker = n // n_workers
    n_pairs = rows_per_worker // (2 * chunk)
    assert n_pairs * 2 * chunk * n_workers == n
    mesh = plsc.VectorSubcoreMesh(core_axis_name="c", subcore_axis_name="s")

    def body(table_hbm, idx_hbm, out_hbm, idx_v, rows_v, gsem, osem):
        wid = lax.axis_index("s") * SC_CORES + lax.axis_index("c")
        base0 = wid * rows_per_worker

        def gather_copy(b):
            return pltpu.make_async_copy(table_hbm.at[idx_v.at[b]], rows_v.at[b], gsem.at[b])

        def write_copy(ci, b):
            base = pl.multiple_of(base0 + ci * chunk, 8)
            return pltpu.make_async_copy(rows_v.at[b], out_hbm.at[pl.ds(base, chunk)], osem.at[b])

        def issue(ci, b):
            base = pl.multiple_of(base0 + ci * chunk, 8)
            pltpu.sync_copy(idx_hbm.at[pl.ds(base, chunk)], idx_v.at[b])
            gather_copy(b).start()

        def finish(ci, b):
            gather_copy(b).wait()
            write_copy(ci, b).start()

        issue(0, 0)

        @pl.loop(0, n_pairs)
        def _(j):
            @pl.when(j > 0)
            def _():
                write_copy(2 * j - 1, 1).wait()
            issue(2 * j + 1, 1)
            finish(2 * j, 0)

            @pl.when(j < n_pairs - 1)
            def _():
                write_copy(2 * j, 0).wait()
                issue(2 * j + 2, 0)
            finish(2 * j + 1, 1)

        write_copy(2 * n_pairs - 2, 0).wait()
        write_copy(2 * n_pairs - 1, 1).wait()

    return pl.kernel(
        body, mesh=mesh,
        out_type=jax.ShapeDtypeStruct((n, width), table.dtype),
        scratch_types=[pltpu.VMEM((2, chunk), jnp.int32), pltpu.VMEM((2, chunk, width), table.dtype),
                       pltpu.SemaphoreType.DMA((2,)), pltpu.SemaphoreType.DMA((2,))],
    )(table, idx)


def _ffn_kernel(be_ref, nu_ref, xs_ref, w1_ref, b1_ref, w2_ref, b2_ref, ys_ref, w1b_s, w2b_s):
    i = pl.program_id(0)
    e = be_ref[i]
    e_prev = be_ref[jnp.maximum(i - 1, 0)]

    @pl.when((i == 0) | (e != e_prev))
    def _():
        w1b_s[...] = w1_ref[0].astype(BF16)
        w2b_s[...] = w2_ref[0].astype(BF16)

    @pl.when(i < nu_ref[0])
    def _():
        half = D_MODEL // 2
        x_lo, x_hi = _unpack_bf16_pairs(xs_ref[...])
        hmid = (jnp.dot(x_lo.astype(BF16), w1b_s[:half, :], preferred_element_type=F32)
                + jnp.dot(x_hi.astype(BF16), w1b_s[half:, :], preferred_element_type=F32)) + b1_ref[0]
        glu = jnp.minimum(hmid[:, :D_FF], SWIGLU_LIMIT)
        lin = jnp.clip(hmid[:, D_FF:], -SWIGLU_LIMIT, SWIGLU_LIMIT)
        act = ((lin + 1.0) * glu * jax.nn.sigmoid(SWIGLU_ALPHA * glu)).astype(BF16)
        y = jnp.dot(act, w2b_s[...], preferred_element_type=F32) + b2_ref[0]
        ys_ref[...] = _pack_bf16_pairs(y)

    @pl.when(i >= nu_ref[0])
    def _():
        ys_ref[...] = jnp.zeros(ys_ref.shape, jnp.int32)


def _sc_scatter_rows(tables, positions, n_rows):
    width = tables[0].shape[1]
    chunk = SC_GATHER_CHUNK
    n_workers = SC_CORES * SC_SUBCORES
    n_streams = len(tables)
    mesh = plsc.VectorSubcoreMesh(core_axis_name="c", subcore_axis_name="s")

    def body(*refs):
        table_refs = refs[0:2 * n_streams:2]
        pos_refs = refs[1:2 * n_streams:2]
        out_hbm, idx_v, rows_v, lsem, ssem = refs[2 * n_streams:]
        wid = lax.axis_index("s") * SC_CORES + lax.axis_index("c")

        for table_hbm, pos_hbm in zip(table_refs, pos_refs):
            tok_per_worker = table_hbm.shape[0] // n_workers
            n_pairs = tok_per_worker // (2 * chunk)
            assert n_pairs * 2 * chunk * n_workers == table_hbm.shape[0]
            base0 = wid * tok_per_worker

            def load_copy(ci, b):
                base = pl.multiple_of(base0 + ci * chunk, 8)
                return pltpu.make_async_copy(table_hbm.at[pl.ds(base, chunk)], rows_v.at[b], lsem.at[b])

            def scatter_copy(b, k):
                return pltpu.make_async_copy(rows_v.at[b], out_hbm.at[idx_v.at[b, k]], ssem.at[b])

            def load(ci, b):
                base = pl.multiple_of(base0 + ci * chunk, 8)
                load_copy(ci, b).start()
                for k in range(TOP_K):
                    pltpu.sync_copy(pos_hbm.at[k, pl.ds(base, chunk)], idx_v.at[b, k])

            def scatter(ci, b):
                load_copy(ci, b).wait()
                for k in range(TOP_K):
                    scatter_copy(b, k).start()

            def drain(b):
                for k in range(TOP_K):
                    scatter_copy(b, k).wait()

            load(0, 0)

            @pl.loop(0, n_pairs)
            def _(j):
                @pl.when(j > 0)
                def _():
                    drain(1)
                load(2 * j + 1, 1)
                scatter(2 * j, 0)

                @pl.when(j < n_pairs - 1)
                def _():
                    drain(0)
                    load(2 * j + 2, 0)
                scatter(2 * j + 1, 1)

            drain(0)
            drain(1)

    args = [a for pair in zip(tables, positions) for a in pair]
    return pl.kernel(
        body, mesh=mesh,
        out_type=jax.ShapeDtypeStruct((n_rows, width), tables[0].dtype),
        scratch_types=[pltpu.VMEM((2, TOP_K, chunk), jnp.int32), pltpu.VMEM((2, chunk, width), tables[0].dtype),
                       pltpu.SemaphoreType.DMA((2,)), pltpu.SemaphoreType.DMA((2,))],
    )(*args)


def _pos_kernel(idx_ref, base_ref, pos_ref):
    idx = idx_ref[...]
    rows = idx.shape[0]
    lane = lax.broadcasted_iota(jnp.int32, idx.shape, 1)
    hots = [jnp.where(lane == idx[:, k:k + 1], 1.0, 0.0) for k in range(TOP_K)]
    cnt = hots[0] + hots[1] + hots[2] + hots[3]
    ri = lax.broadcasted_iota(jnp.int32, (rows, rows), 0)
    ci = lax.broadcasted_iota(jnp.int32, (rows, rows), 1)
    earlier = jnp.where(ci < ri, 1.0, 0.0).astype(BF16)
    ahead = jnp.dot(earlier, cnt.astype(BF16), preferred_element_type=F32) + base_ref[0]
    posmat = jnp.zeros(idx.shape, F32)
    for k in range(TOP_K):
        posmat = jnp.where(lane == k, jnp.sum(hots[k] * ahead, axis=-1, keepdims=True), posmat)
    pos_ref[...] = posmat.T[:TOP_K, :].astype(jnp.int32)


def _route(idx_list, count_list):
    tiles = [c.shape[0] * c.shape[1] for c in count_list]
    n_tok = sum(tiles) * ROUTE_TILE
    n_blocks = n_tok * TOP_K // MOE_ROWS + N_EXPERTS
    cnt = jnp.concatenate([c.reshape(-1, LANES) for c in count_list], axis=0).astype(jnp.int32)
    tile_off = jnp.cumsum(cnt, axis=0) - cnt
    total = jnp.sum(cnt, axis=0)
    padded = (total + MOE_ROWS - 1) // MOE_ROWS * MOE_ROWS
    pend = jnp.cumsum(padded)
    base = ((pend - padded)[None, :] + tile_off).astype(F32)
    starts = jnp.arange(n_blocks, dtype=jnp.int32) * MOE_ROWS
    block_e = jnp.minimum(jnp.sum(pend[None, :N_EXPERTS] <= starts[:, None], axis=1), N_EXPERTS - 1).astype(jnp.int32)
    n_used = (pend[N_EXPERTS - 1] // MOE_ROWS).astype(jnp.int32).reshape(1)
    positions, first = [], 0
    for idx, n_tiles in zip(idx_list, tiles):
        stream_tok = n_tiles * ROUTE_TILE
        positions.append(pl.pallas_call(
            _pos_kernel,
            grid=(n_tiles,),
            in_specs=[pl.BlockSpec((ROUTE_TILE, LANES), lambda i: (i, 0)),
                      pl.BlockSpec((1, 1, LANES), lambda i: (i, 0, 0))],
            out_specs=pl.BlockSpec((TOP_K, ROUTE_TILE), lambda i: (0, i)),
            out_shape=jax.ShapeDtypeStruct((TOP_K, stream_tok), jnp.int32),
            compiler_params=_cparams(("parallel",)),
            name="route_pos",
        )(idx.reshape(stream_tok, LANES), base[first:first + n_tiles].reshape(n_tiles, 1, LANES)))
        first += n_tiles
    return block_e, n_used, positions


def _moe(h2p_list, idx_list, count_list, w1, b1, w2, b2):
    half = h2p_list[0].shape[1]
    d = 2 * half
    block_e, n_used, positions = _route(idx_list, count_list)
    n_blocks = block_e.shape[0]
    xs = _sc_scatter_rows(h2p_list, positions, n_blocks * MOE_ROWS)
    grid_spec = pltpu.PrefetchScalarGridSpec(
        num_scalar_prefetch=2,
        grid=(n_blocks,),
        in_specs=[pl.BlockSpec((MOE_ROWS, half), lambda i, be, nu: (i, 0)),
                  pl.BlockSpec((1, d, 2 * D_FF), lambda i, be, nu: (be[i], 0, 0)),
                  pl.BlockSpec((1, 1, 2 * D_FF), lambda i, be, nu: (be[i], 0, 0)),
                  pl.BlockSpec((1, D_FF, d), lambda i, be, nu: (be[i], 0, 0)),
                  pl.BlockSpec((1, 1, d), lambda i, be, nu: (be[i], 0, 0))],
        out_specs=pl.BlockSpec((MOE_ROWS, half), lambda i, be, nu: (i, 0)),
        scratch_shapes=[pltpu.VMEM((d, 2 * D_FF), BF16), pltpu.VMEM((D_FF, d), BF16)],
    )
    ys = pl.pallas_call(
        _ffn_kernel,
        grid_spec=grid_spec,
        out_shape=jax.ShapeDtypeStruct((n_blocks * MOE_ROWS, half), jnp.int32),
        compiler_params=_cparams(("arbitrary",)),
        name="moe_ffn",
    )(block_e, n_used, xs, w1, b1.reshape(N_EXPERTS, 1, -1), w2, b2.reshape(N_EXPERTS, 1, -1))
    return [_sc_gather_rows(ys, pos.reshape(-1)) for pos in positions]


def _final_kernel(x1_ref, y0_ref, y1_ref, y2_ref, y3_ref, gate_ref, mod_ref, g_ref, o_ref):
    gates = gate_ref[0]
    lo = hi = None
    for k, y_ref in enumerate((y0_ref, y1_ref, y2_ref, y3_ref)):
        y_lo, y_hi = _unpack_bf16_pairs(y_ref[0, 0])
        gk = gates[:, k:k + 1]
        lo = gk * y_lo if k == 0 else lo + gk * y_lo
        hi = gk * y_hi if k == 0 else hi + gk * y_hi
    y = jnp.concatenate([lo, hi], axis=-1)
    gate2 = mod_ref[0, 5:6, :]
    o_ref[0] = x1_ref[0] + gate2 * _rms(y, g_ref[...])


def _final(x1, yg, gate, mod, mod_row, g_post_ffn, tm):
    bsz, seq, d = x1.shape
    tok3 = lambda b, i: (b, i, 0)
    yg = yg.reshape(TOP_K, bsz, seq, d // 2)
    slot_specs = [pl.BlockSpec((1, 1, tm, d // 2), functools.partial(lambda b, i, k: (k, b, i, 0), k=k))
                  for k in range(TOP_K)]
    return pl.pallas_call(
        _final_kernel,
        grid=(bsz, seq // tm),
        in_specs=[pl.BlockSpec((1, tm, d), tok3)] + slot_specs +
                 [pl.BlockSpec((1, tm, LANES), tok3),
                  pl.BlockSpec((1, 6, d), lambda b, i: (mod_row(b), 0, 0)), _const_spec((1, d))],
        out_specs=pl.BlockSpec((1, tm, d), tok3),
        out_shape=jax.ShapeDtypeStruct((bsz, seq, d), F32),
        compiler_params=_cparams(("parallel", "parallel")),
        name="final_residual",
    )(x1, yg, yg, yg, yg, gate, mod, g_post_ffn.astype(F32).reshape(1, d))


def _stream(x, mod, mod_row, p, ctx_kv, init_state, rope, tm):
    bsz, seq, d = x.shape
    is_ctx = ctx_kv is None
    pj = _project(x, mod, p["g_pre_mix"], p["w_in"], p["b_gates"], mod_row, rope, is_ctx, tm)
    if is_ctx:
        att = _attention(pj["q"], None, None, pj["kd"], pj["vd"], p["sink"], window=False)
    else:
        att = _attention(pj["q"], pj["kd"], pj["vd"], ctx_kv[0], ctx_kv[1], p["sink"], window=True)
    ml = _mlstm(pj["mq"], pj["mk"], pj["mv"], pj["gates"], init_state, emit_state=is_ctx)
    x1, h2, idx, gate, tile_counts = _merge(att, ml[0], ml[1], pj, x, mod, mod_row, p, tm)
    moe_in = (h2.reshape(bsz * seq, d // 2), idx, tile_counts)
    return x1, gate, moe_in, pj, ml


def kernel(x_prompt, x_sample, c, cache_k, cache_v, state_C, state_n, state_m, c_ctx, w_ada, b_ada, g_pre_mix,
           w_in, b_gates, attn_sink, norm_mlstm, w_up_att, w_up_ml, w_out, g_post_mix, g_pre_ffn, w_router,
           b_router, w1, b1, w2, b2, g_post_ffn):
    depth = w_ada.shape[0]
    n_dec = c.shape[0]
    cond = jnp.concatenate([c_ctx[None, :], c], axis=0).astype(F32)
    cond = jnp.pad(cond, ((0, 16 - cond.shape[0]), (0, 0)))
    y_prompt, y_sample = x_prompt, x_sample
    ks_, vs_, cs_, ns_, ms_ = [], [], [], [], []
    for l in range(depth):
        p = dict(g_pre_mix=g_pre_mix[l], w_in=w_in[l], b_gates=b_gates[l], sink=attn_sink[l], norm_ml=norm_mlstm[l],
                 w_up_att=w_up_att[l], w_up_ml=w_up_ml[l], w_out=w_out[l], g_post_mix=g_post_mix[l],
                 g_pre_ffn=g_pre_ffn[l], w_router=w_router[l], b_router=b_router[l], w1=w1[l], b1=b1[l],
                 w2=w2[l], b2=b2[l], g_post_ffn=g_post_ffn[l])
        mod = _adaln(cond, w_ada[l], b_ada[l]).reshape(16, 6, D_MODEL)
        row_p, row_s = (lambda b: 0), (lambda b: b + 1)
        x1_p, gate_p, moe_p, pj, ml = _stream(y_prompt, mod, row_p, p, None, None, rope=False, tm=256)
        bsz, seq = x_prompt.shape[:2]
        nat = pj["nat"]
        ks_.append(nat[..., :KV_W].reshape(bsz, seq, ATT_KV_HEADS, HEAD_DIM))
        vs_.append(nat[..., KV_W:].reshape(bsz, seq, ATT_KV_HEADS, HEAD_DIM))
        cs_.append(ml[2])
        ns_.append(ml[3])
        ms_.append(ml[4][..., 0])
        ctx_kv = (_dup_heads(cache_k[:, l]), _dup_heads(cache_v[:, l]))
        init = (state_C[:, l], state_n[:, l], state_m[:, l])
        x1_s, gate_s, moe_s, _, _ = _stream(y_sample, mod, row_s, p, ctx_kv, init, rope=True, tm=512)
        yg_p, yg_s = _moe([moe_p[0], moe_s[0]], [moe_p[1], moe_s[1]], [moe_p[2], moe_s[2]],
                          p["w1"], p["b1"].astype(F32), p["w2"], p["b2"].astype(F32))
        y_prompt = _final(x1_p, yg_p, gate_p, mod, row_p, p["g_post_ffn"], 256)
        y_sample = _final(x1_s, yg_s, gate_s, mod, row_s, p["g_post_ffn"], 512)
    return (y_prompt, y_sample, jnp.stack(ks_, axis=1), jnp.stack(vs_, axis=1), jnp.stack(cs_, axis=1),
            jnp.stack(ns_, axis=1), jnp.stack(ms_, axis=1))
```

```python
import functools

import numpy as np
import jax
import jax.numpy as jnp
from jax import lax
from jax.experimental import pallas as pl
from jax.experimental.pallas import tpu as pltpu
from jax.experimental.pallas import tpu_sc as plsc

F32 = jnp.float32
BF16 = jnp.bfloat16

D_MODEL = 1024
GRID_W = 64
ATT_HEADS = 8
ATT_KV_HEADS = 2
ATT_GROUP = ATT_HEADS // ATT_KV_HEADS
HEAD_DIM = 64
BLOCK = 128
ROPE_THETA = 10000.0
AXIS_FREQS = HEAD_DIM // 4
ML_HEADS = 4
ML_DK = 128
ML_DV = 256
ML_CHUNK = 128
N_EXPERTS = 32
TOP_K = 4
D_FF = 1024
SWIGLU_ALPHA = 1.702
SWIGLU_LIMIT = 7.0
EPS = 1e-6
NEG = -1e30

Q_W = ATT_HEADS * HEAD_DIM
KV_W = ATT_KV_HEADS * HEAD_DIM
MLQK_W = ML_HEADS * ML_DK
MLV_W = ML_HEADS * ML_DV
GATE_W = 4 * ML_HEADS

LANES = 128
VMEM_LIMIT = 56 * 1024 * 1024
MOE_ROWS = 512
SC_CORES = 2
SC_SUBCORES = 16
SC_GATHER_CHUNK = 32
ROUTE_TILE = 256


def _cparams(sem):
    return pltpu.CompilerParams(dimension_semantics=sem, vmem_limit_bytes=VMEM_LIMIT)


def _const_spec(shape):
    nd = len(shape)
    return pl.BlockSpec(shape, lambda *_: (0,) * nd)


def _adaln_kernel(cond_ref, w_ref, b_ref, o_ref):
    cnd = cond_ref[...]
    act = cnd * jax.nn.sigmoid(cnd)
    o_ref[...] = jnp.dot(act.astype(BF16), w_ref[...].astype(BF16),
                         preferred_element_type=F32) + b_ref[...]


def _adaln(cond, w_ada, b_ada):
    rows, d = cond.shape
    n = w_ada.shape[1]
    tn = 1536
    return pl.pallas_call(
        _adaln_kernel,
        grid=(n // tn,),
        in_specs=[_const_spec((rows, d)),
                  pl.BlockSpec((d, tn), lambda j: (0, j)),
                  pl.BlockSpec((1, tn), lambda j: (0, j))],
        out_specs=pl.BlockSpec((rows, tn), lambda j: (0, j)),
        out_shape=jax.ShapeDtypeStruct((rows, n), F32),
        compiler_params=_cparams(("parallel",)),
        name="adaln",
    )(cond, w_ada, b_ada.reshape(1, n))


def _pack_w_in(w_in, b_gates, rope, with_nat):
    o_q, o_k, o_v = 0, Q_W, Q_W + KV_W
    o_mq = Q_W + 2 * KV_W
    o_mk = o_mq + MLQK_W
    o_mv = o_mk + MLQK_W
    o_g = o_mv + MLV_W
    o_mo = o_g + GATE_W
    o_ga = o_mo + MLV_W
    o_gm = o_ga + D_MODEL
    half = HEAD_DIM // 2
    zeros64 = jnp.zeros((D_MODEL, HEAD_DIM), w_in.dtype)

    def head_cols(base, h, rot):
        lo = base + h * HEAD_DIM
        if rot:
            return [w_in[:, lo + half:lo + HEAD_DIM], w_in[:, lo:lo + half]]
        return [w_in[:, lo:lo + HEAD_DIM]]

    def q_cols(rot):
        cols = []
        for h in range(ATT_HEADS):
            hc = head_cols(o_q, h, rot)
            cols += (hc + [zeros64]) if h % 2 == 0 else ([zeros64] + hc)
        return cols

    def dup_cols(base, rot):
        cols = []
        for h in range(ATT_KV_HEADS):
            hc = head_cols(base, h, rot)
            cols += hc + hc
        return cols

    segs, cols, off = {}, [], 0

    def add(name, cl):
        nonlocal off
        width = sum(c.shape[1] for c in cl)
        segs[name] = (off, off + width)
        cols.extend(cl)
        off += width

    add("q", q_cols(False))
    if rope:
        add("qrot", q_cols(True))
    add("kd", dup_cols(o_k, False))
    if rope:
        add("kdrot", dup_cols(o_k, True))
    add("vd", dup_cols(o_v, False))
    if with_nat:
        add("nat", [w_in[:, o_k:o_k + 2 * KV_W]])
    add("mq", [w_in[:, o_mq:o_mq + MLQK_W]])
    add("mk", [w_in[:, o_mk:o_mk + MLQK_W]])
    add("mv", [w_in[:, o_mv:o_mv + MLV_W]])
    add("mo", [w_in[:, o_mo:o_mo + MLV_W]])
    add("ga", [w_in[:, o_ga:o_ga + D_MODEL]])
    add("gm", [w_in[:, o_gm:o_gm + D_MODEL]])
    add("gates", [w_in[:, o_g:o_g + GATE_W], jnp.zeros((D_MODEL, LANES - GATE_W), w_in.dtype)])
    wp = jnp.concatenate(cols, axis=1).astype(BF16)
    bg = jnp.pad(b_gates.astype(F32), (0, LANES - GATE_W)).reshape(1, LANES)
    return wp, bg, segs


def _rope_tables(n_tok):
    rows = n_tok // GRID_W
    row = np.repeat(np.arange(rows), GRID_W).astype(np.float32)
    col = np.tile(np.arange(GRID_W), rows).astype(np.float32)
    inv = (np.float32(ROPE_THETA) ** (-np.arange(AXIS_FREQS, dtype=np.float32) / AXIS_FREQS)).astype(np.float32)
    ang = np.concatenate([row[:, None] * inv, col[:, None] * inv], axis=-1).astype(np.float32)
    cos, sin = np.cos(ang), np.sin(ang)
    c64 = np.concatenate([cos, cos], axis=-1)
    s64 = np.concatenate([-sin, sin], axis=-1)
    return (jnp.asarray(np.tile(c64, (1, 2)), F32), jnp.asarray(np.tile(s64, (1, 2)), F32))


def _proj_kernel(*refs, segs, rope, with_nat):
    it = iter(refs)
    x_ref, mod_ref, g_ref, w_ref, bg_ref = (next(it) for _ in range(5))
    cos_ref = sin_ref = None
    if rope:
        cos_ref, sin_ref = next(it), next(it)
    q_ref, kd_ref, vd_ref = next(it), next(it), next(it)
    nat_ref = next(it) if with_nat else None
    mq_ref, mk_ref, mv_ref, gates_ref, mo_ref, ga_ref, gm_ref = (next(it) for _ in range(7))

    x = x_ref[0]
    shift = mod_ref[0, 0:1, :]
    scale = mod_ref[0, 1:2, :]
    ms = jnp.mean(x * x, axis=-1, keepdims=True)
    xn = x * lax.rsqrt(ms + EPS) * g_ref[...]
    xb = (xn * (1.0 + scale) + shift).astype(BF16)

    def seg(name):
        lo, hi = segs[name]
        return jnp.dot(xb, w_ref[:, lo:hi], preferred_element_type=F32)

    uq = seg("q")
    ukd = seg("kd")
    if rope:
        uqr = seg("qrot")
        ukr = seg("kdrot")
        cs = cos_ref[...]
        sn = sin_ref[...]
        for h in range(ATT_HEADS):
            sl = slice(h * LANES, (h + 1) * LANES)
            q_ref[0, h] = ((uq[:, sl] * cs + uqr[:, sl] * sn) * HEAD_DIM ** -0.5).astype(BF16)
        for h in range(ATT_KV_HEADS):
            sl = slice(h * LANES, (h + 1) * LANES)
            kd_ref[0, :, sl] = (ukd[:, sl] * cs + ukr[:, sl] * sn).astype(BF16)
    else:
        for h in range(ATT_HEADS):
            sl = slice(h * LANES, (h + 1) * LANES)
            q_ref[0, h] = (uq[:, sl] * HEAD_DIM ** -0.5).astype(BF16)
        kd_ref[0] = ukd.astype(BF16)
    vd_ref[0] = seg("vd").astype(BF16)
    if with_nat:
        nat_ref[0] = seg("nat")
    mq_ref[0] = (seg("mq") * ML_DK ** -0.5).astype(BF16)
    mk_ref[0] = seg("mk").astype(BF16)
    mv_ref[0] = seg("mv").astype(BF16)
    gates_ref[0] = seg("gates") + bg_ref[...]
    mo_ref[0] = seg("mo")
    ga_ref[0] = seg("ga")
    gm_ref[0] = seg("gm")


def _project(x, mod, g_pre, w_in, b_gates, mod_row, rope, with_nat, tm):
    bsz, seq, d = x.shape
    wp, bg, segs = _pack_w_in(w_in, b_gates, rope, with_nat)
    nw = wp.shape[1]
    tok3 = lambda b, i: (b, i, 0)
    in_specs = [pl.BlockSpec((1, tm, d), tok3),
                pl.BlockSpec((1, 6, d), lambda b, i: (mod_row(b), 0, 0)),
                _const_spec((1, d)),
                pl.BlockSpec((d, nw), lambda b, i: (0, 0), pipeline_mode=pl.Buffered(1)),
                _const_spec((1, LANES))]
    args = [x, mod, g_pre.reshape(1, d), wp, bg]
    if rope:
        cos_t, sin_t = _rope_tables(seq)
        in_specs += [pl.BlockSpec((tm, LANES), lambda b, i: (i, 0))] * 2
        args += [cos_t, sin_t]

    def tok_out(width, dtype):
        return (jax.ShapeDtypeStruct((bsz, seq, width), dtype), pl.BlockSpec((1, tm, width), tok3))

    outs = [(jax.ShapeDtypeStruct((bsz, ATT_HEADS, seq, LANES), BF16),
             pl.BlockSpec((1, ATT_HEADS, tm, LANES), lambda b, i: (b, 0, i, 0))),
            tok_out(2 * LANES, BF16), tok_out(2 * LANES, BF16)]
    if with_nat:
        outs.append(tok_out(2 * KV_W, F32))
    outs += [tok_out(MLQK_W, BF16), tok_out(MLQK_W, BF16), tok_out(MLV_W, BF16), tok_out(LANES, F32),
             tok_out(MLV_W, F32), tok_out(D_MODEL, F32), tok_out(D_MODEL, F32)]
    res = pl.pallas_call(
        functools.partial(_proj_kernel, segs=segs, rope=rope, with_nat=with_nat),
        grid=(bsz, seq // tm),
        in_specs=in_specs,
        out_specs=[o[1] for o in outs],
        out_shape=[o[0] for o in outs],
        compiler_params=_cparams(("parallel", "parallel")),
        name="proj_rope" if rope else "proj_ctx",
    )(*args)
    names = ["q", "kd", "vd"] + (["nat"] if with_nat else []) + ["mq", "mk", "mv", "gates", "mo", "ga", "gm"]
    return dict(zip(names, res))


def _attn_kernel(*refs, window, n_blocks, n_ctx):
    it = iter(refs)
    sink_ref, q_ref = next(it), next(it)
    if window:
        kp_ref, kc_ref, kn_ref, vp_ref, vc_ref, vn_ref = (next(it) for _ in range(6))
    ck_ref, cv_ref, o_ref = next(it), next(it), next(it)
    j = pl.program_id(1)
    n_win = 3 * BLOCK if window else 0
    n_keys = n_win + n_ctx
    rows = ATT_GROUP * BLOCK

    row_id = lax.broadcasted_iota(jnp.int32, (rows, 1), 0)
    if window:
        rl = lax.broadcasted_iota(jnp.int32, (rows, BLOCK), 0) & (BLOCK - 1)
        cl = lax.broadcasted_iota(jnp.int32, (rows, BLOCK), 1)
        prev_ok = cl >= rl + jnp.where(j == 0, BLOCK, 0)
        next_ok = cl <= rl - jnp.where(j == n_blocks - 1, BLOCK, 0)
    lane_lo = lax.broadcasted_iota(jnp.int32, (BLOCK, LANES), 1) < HEAD_DIM

    for kvh in range(ATT_KV_HEADS):
        sl = slice(kvh * LANES, (kvh + 1) * LANES)
        if window:
            keys = jnp.concatenate([kp_ref[0, :, sl], kc_ref[0, :, sl], kn_ref[0, :, sl], ck_ref[0, :, sl]], axis=0)
            vals = jnp.concatenate([vp_ref[0, :, sl], vc_ref[0, :, sl], vn_ref[0, :, sl], cv_ref[0, :, sl]], axis=0)
        else:
            keys = ck_ref[0, :, sl]
            vals = cv_ref[0, :, sl]
        qs = q_ref[0, kvh * ATT_GROUP:(kvh + 1) * ATT_GROUP].reshape(rows, LANES)
        s = lax.dot_general(qs, keys, (((1,), (1,)), ((), ())), preferred_element_type=F32)
        if window:
            s = jnp.concatenate([jnp.where(prev_ok, s[:, :BLOCK], NEG), s[:, BLOCK:2 * BLOCK],
                                 jnp.where(next_ok, s[:, 2 * BLOCK:n_win], NEG), s[:, n_win:]], axis=1)
        snk = jnp.full((rows, 1), sink_ref[kvh * ATT_GROUP + ATT_GROUP - 1], F32)
        for g in range(ATT_GROUP - 2, -1, -1):
            snk = jnp.where(row_id < (g + 1) * BLOCK, sink_ref[kvh * ATT_GROUP + g], snk)
        m = jnp.maximum(jnp.max(s, axis=-1, keepdims=True), snk)
        p = jnp.exp(s - m)
        den = jnp.sum(p, axis=-1, keepdims=True) + jnp.exp(snk - m)
        o = jnp.dot(p.astype(BF16), vals, preferred_element_type=F32) / den
        for pair in range(ATT_GROUP // 2):
            even = o[(2 * pair) * BLOCK:(2 * pair + 1) * BLOCK]
            odd = o[(2 * pair + 1) * BLOCK:(2 * pair + 2) * BLOCK]
            col = (kvh * (ATT_GROUP // 2) + pair) * LANES
            o_ref[0, :, col:col + LANES] = jnp.where(lane_lo, even, odd).astype(BF16)


def _attention(q, kd, vd, ckd, cvd, sink, window):
    bsz, _, seq, _ = q.shape
    nb = seq // BLOCK
    n_ctx = ckd.shape[1]
    in_specs = [pl.BlockSpec(memory_space=pltpu.SMEM),
                pl.BlockSpec((1, ATT_HEADS, BLOCK, LANES), lambda b, j: (b, 0, j, 0))]
    args = [sink.astype(F32), q]
    if window:
        prev = lambda b, j: (b, jnp.maximum(j - 1, 0), 0)
        cur = lambda b, j: (b, j, 0)
        nxt = lambda b, j: (b, jnp.minimum(j + 1, nb - 1), 0)
        for arr in (kd, vd):
            in_specs += [pl.BlockSpec((1, BLOCK, 2 * LANES), f) for f in (prev, cur, nxt)]
            args += [arr, arr, arr]
    in_specs += [pl.BlockSpec((1, n_ctx, 2 * LANES), lambda b, j: (b, 0, 0))] * 2
    args += [ckd, cvd]
    return pl.pallas_call(
        functools.partial(_attn_kernel, window=window, n_blocks=nb, n_ctx=n_ctx),
        grid=(bsz, nb),
        in_specs=in_specs,
        out_specs=pl.BlockSpec((1, BLOCK, Q_W), lambda b, j: (b, j, 0)),
        out_shape=jax.ShapeDtypeStruct((bsz, seq, Q_W), BF16),
        compiler_params=_cparams(("parallel", "parallel")),
        name="attn_window" if window else "attn_ctx",
    )(*args)


def _dup_heads(t):
    b, l = t.shape[:2]
    return jnp.concatenate([t, t], axis=-1).reshape(b, l, ATT_KV_HEADS * LANES).astype(BF16)


def _split3(x):
    hi = x.astype(BF16)
    r1 = x - hi.astype(F32)
    mid = r1.astype(BF16)
    lo = (r1 - mid.astype(F32)).astype(BF16)
    return hi, mid, lo


def _log_sigmoid(x):
    return jnp.minimum(x, 0.0) - jnp.log1p(jnp.exp(-jnp.abs(x)))


def _mlstm_kernel(*refs, n_chunks, has_init, emit_state):
    it = iter(refs)
    qkvg = [[next(it) for _ in range(4)] for _ in range(2)]
    if has_init:
        c0_ref, n0_ref, m0_ref = next(it), next(it), next(it)
    h_refs = [next(it), next(it)]
    if emit_state:
        co_ref, no_ref, mo_ref = next(it), next(it), next(it)
    ct_s, n_s, m_s = next(it), next(it), next(it)
    t = pl.program_id(1)
    L = ML_CHUNK

    @pl.when(t == 0)
    def _():
        for d in range(2):
            for h in range(ML_HEADS):
                if has_init:
                    ct_s[d, h] = c0_ref[0, d, h].T
                    n_s[d, h] = n0_ref[0, d, h:h + 1, :]
                    m_s[d, h] = m0_ref[0, d, h:h + 1, :]
                else:
                    ct_s[d, h] = jnp.zeros((ML_DK, ML_DV), F32)
                    n_s[d, h] = jnp.zeros((1, ML_DK), F32)
                    m_s[d, h] = jnp.zeros((1, LANES), F32)

    ri = lax.broadcasted_iota(jnp.int32, (L, L), 0)
    ci = lax.broadcasted_iota(jnp.int32, (L, L), 1)

    for d in range(2):
        q_ref, k_ref, v_ref, g_ref = qkvg[d]
        tri = (ci <= ri) if d == 0 else (ci >= ri)
        tri_b = jnp.where(tri, 1.0, 0.0).astype(BF16)
        g = g_ref[0]
        lf = _log_sigmoid(g)
        g_t = g.T
        lf_t = lf.T
        bc = sum(jnp.dot(tri_b, part, preferred_element_type=F32) for part in _split3(lf))
        br = sum(lax.dot_general(part, tri_b, (((1,), (1,)), ((), ())), preferred_element_type=F32)
                 for part in _split3(lf_t))
        tot = jnp.sum(lf, axis=0, keepdims=True)

        for h in range(ML_HEADS):
            icol_i = 2 * ML_HEADS * d + h
            fcol = icol_i + ML_HEADS
            bcol = bc[:, fcol:fcol + 1]
            brow = br[fcol:fcol + 1, :]
            irow = g_t[icol_i:icol_i + 1, :]
            icol = g[:, icol_i:icol_i + 1]
            b_last = tot[:, fcol:fcol + 1]
            m_prev = m_s[d, h][:, 0:1]
            n_prev = n_s[d, h]
            ct_prev = ct_s[d, h]
            qh = q_ref[0, :, h * ML_DK:(h + 1) * ML_DK]
            kh = k_ref[0, :, h * ML_DK:(h + 1) * ML_DK]
            vh = v_ref[0, :, h * ML_DV:(h + 1) * ML_DV]

            rel = jnp.where(tri, irow - brow, NEG)
            inter = bcol + m_prev
            m_t = jnp.maximum(inter, bcol + jnp.max(rel, axis=-1, keepdims=True))
            w = jnp.exp(rel + (bcol - m_t))
            w_inter = jnp.exp(inter - m_t)
            s = lax.dot_general(qh, kh, (((1,), (1,)), ((), ())), preferred_element_type=F32) * w
            cq = jnp.dot(qh, ct_prev.astype(BF16), preferred_element_type=F32) * w_inter
            num = jnp.dot(s.astype(BF16), vh, preferred_element_type=F32) + cq
            qn = jnp.sum(qh.astype(F32) * n_prev, axis=-1, keepdims=True)
            den = jnp.sum(s, axis=-1, keepdims=True) + w_inter * qn
            h_refs[d][0, :, h * ML_DV:(h + 1) * ML_DV] = num / jnp.maximum(jnp.abs(den), jnp.exp(-m_t))

            a = b_last - bcol + icol
            m_new = jnp.maximum(b_last + m_prev, jnp.max(a, axis=0, keepdims=True))
            wk = jnp.exp(a - m_new)
            decay = jnp.exp(b_last + m_prev - m_new)
            kw = kh.astype(F32) * wk
            ct_new = decay * ct_prev + jnp.dot(kw.T.astype(BF16), vh, preferred_element_type=F32)
            n_new = decay * n_prev + jnp.sum(kw, axis=0, keepdims=True)
            ct_s[d, h] = ct_new
            n_s[d, h] = n_new
            m_s[d, h] = jnp.broadcast_to(m_new, (1, LANES))
            if emit_state:
                @pl.when(t == n_chunks - 1)
                def _():
                    co_ref[0, d, h] = ct_new.T
                    no_ref[0, d, h:h + 1, :] = n_new
                    mo_ref[0, d, h:h + 1, :] = jnp.broadcast_to(m_new, (1, LANES))


def _mlstm(mq, mk, mv, gates, init, emit_state):
    bsz, seq, _ = mq.shape
    nc = seq // ML_CHUNK
    fwd = lambda b, t: (b, t, 0)
    bwd = lambda b, t: (b, nc - 1 - t, 0)
    state5 = lambda b, t: (b, 0, 0, 0, 0)
    state4 = lambda b, t: (b, 0, 0, 0)
    in_specs, args = [], []
    for chunk in (fwd, bwd):
        in_specs += [pl.BlockSpec((1, ML_CHUNK, MLQK_W), chunk), pl.BlockSpec((1, ML_CHUNK, MLQK_W), chunk),
                     pl.BlockSpec((1, ML_CHUNK, MLV_W), chunk), pl.BlockSpec((1, ML_CHUNK, LANES), chunk)]
        args += [mq, mk, mv, gates]
    if init is not None:
        c0, n0, m0 = init
        in_specs += [pl.BlockSpec((1, 2, ML_HEADS, ML_DV, ML_DK), state5),
                     pl.BlockSpec((1, 2, ML_HEADS, ML_DK), state4),
                     pl.BlockSpec((1, 2, ML_HEADS, LANES), state4)]
        args += [c0.astype(F32), n0.astype(F32),
                 jnp.broadcast_to(m0.astype(F32)[..., None], m0.shape + (LANES,))]
    out_shape = [jax.ShapeDtypeStruct((bsz, seq, MLV_W), F32)] * 2
    out_specs = [pl.BlockSpec((1, ML_CHUNK, MLV_W), fwd), pl.BlockSpec((1, ML_CHUNK, MLV_W), bwd)]
    if emit_state:
        out_shape += [jax.ShapeDtypeStruct((bsz, 2, ML_HEADS, ML_DV, ML_DK), F32),
                      jax.ShapeDtypeStruct((bsz, 2, ML_HEADS, ML_DK), F32),
                      jax.ShapeDtypeStruct((bsz, 2, ML_HEADS, LANES), F32)]
        out_specs += [pl.BlockSpec((1, 2, ML_HEADS, ML_DV, ML_DK), state5),
                      pl.BlockSpec((1, 2, ML_HEADS, ML_DK), state4),
                      pl.BlockSpec((1, 2, ML_HEADS, LANES), state4)]
    return pl.pallas_call(
        functools.partial(_mlstm_kernel, n_chunks=nc, has_init=init is not None, emit_state=emit_state),
        grid=(bsz, nc),
        in_specs=in_specs,
        out_specs=out_specs,
        out_shape=out_shape,
        scratch_shapes=[pltpu.VMEM((2, ML_HEADS, ML_DK, ML_DV), F32),
                        pltpu.VMEM((2, ML_HEADS, 1, ML_DK), F32),
                        pltpu.VMEM((2, ML_HEADS, 1, LANES), F32)],
        compiler_params=_cparams(("parallel", "arbitrary")),
        name="mlstm_state" if emit_state else "mlstm",
    )(*args)


def _rms(x, g):
    return x * lax.rsqrt(jnp.mean(x * x, axis=-1, keepdims=True) + EPS) * g


def _pack_bf16_pairs(x):
    n = x.shape[1] // 2
    lo = pltpu.bitcast(x[:, :n].astype(BF16).astype(F32), jnp.int32)
    hi = pltpu.bitcast(x[:, n:].astype(BF16).astype(F32), jnp.int32)
    return lax.shift_right_logical(lo, 16) | hi


def _unpack_bf16_pairs(p):
    lo = pltpu.bitcast(lax.shift_left(p, 16), F32)
    hi = pltpu.bitcast(p & jnp.int32(-65536), F32)
    return lo, hi


def _merge_kernel(att_ref, hf_ref, hb_ref, mo_ref, ga_ref, gm_ref, x_ref, mod_ref, nml_ref, gpm_ref, gpf_ref,
                  wua_ref, wum_ref, wo_ref, wr_ref, br_ref, x1_ref, h2_ref, idx_ref, gate_ref, cnt_ref):
    hsum = hf_ref[0] + hb_ref[0]
    parts = []
    for h in range(ML_HEADS):
        hh = hsum[:, h * ML_DV:(h + 1) * ML_DV]
        parts.append(hh * lax.rsqrt(jnp.mean(hh * hh, axis=-1, keepdims=True) + EPS))
    hn = jnp.concatenate(parts, axis=-1) * nml_ref[...]
    ml = (hn * jax.nn.sigmoid(mo_ref[0])).astype(BF16)
    a = jnp.dot(att_ref[0], wua_ref[...], preferred_element_type=F32)
    m = jnp.dot(ml, wum_ref[...], preferred_element_type=F32)
    z = (jax.nn.sigmoid(ga_ref[0]) * a + jax.nn.sigmoid(gm_ref[0]) * m).astype(BF16)
    mix = jnp.dot(z, wo_ref[...], preferred_element_type=F32)
    gate1 = mod_ref[0, 2:3, :]
    shift2 = mod_ref[0, 3:4, :]
    scale2 = mod_ref[0, 4:5, :]
    x1 = x_ref[0] + gate1 * _rms(mix, gpm_ref[...])
    x1_ref[0] = x1
    h2 = _rms(x1, gpf_ref[...]) * (1.0 + scale2) + shift2
    h2_ref[0] = _pack_bf16_pairs(h2)

    h2_hi = h2.astype(BF16)
    h2_lo = (h2 - h2_hi.astype(F32)).astype(BF16)
    wr = wr_ref[...]
    wr_hi = wr.astype(BF16)
    wr_lo = (wr - wr_hi.astype(F32)).astype(BF16)
    logits = (jnp.dot(h2_hi, wr_hi, preferred_element_type=F32)
              + jnp.dot(h2_lo, wr_hi, preferred_element_type=F32)
              + jnp.dot(h2_hi, wr_lo, preferred_element_type=F32)) + br_ref[...]
    lane = lax.broadcasted_iota(jnp.int32, logits.shape, 1).astype(F32)
    work = jnp.where(lane < N_EXPERTS, logits, -jnp.inf)
    idx_out = jnp.zeros(logits.shape, F32)
    val_out = jnp.zeros(logits.shape, F32)
    picked = jnp.zeros(logits.shape, F32)
    top0 = None
    esum = None
    for k in range(TOP_K):
        mx = jnp.max(work, axis=-1, keepdims=True)
        sel = jnp.min(jnp.where(work == mx, lane, float(LANES)), axis=-1, keepdims=True)
        if k == 0:
            top0 = mx
        e = jnp.exp(mx - top0)
        esum = e if k == 0 else esum + e
        idx_out = jnp.where(lane == k, sel, idx_out)
        val_out = jnp.where(lane == k, e, val_out)
        picked = jnp.where(lane == sel, 1.0, picked)
        work = jnp.where(lane == sel, -jnp.inf, work)
    idx_ref[0] = idx_out.astype(jnp.int32)
    gate_ref[0] = val_out / esum
    for r in range(logits.shape[0] // ROUTE_TILE):
        cnt_ref[0, r:r + 1, :] = jnp.sum(picked[r * ROUTE_TILE:(r + 1) * ROUTE_TILE], axis=0, keepdims=True)


def _merge(att, hf, hb, pj, x, mod, mod_row, p, tm):
    bsz, seq, d = x.shape
    tok3 = lambda b, i: (b, i, 0)
    wr = jnp.pad(p["w_router"].astype(F32), ((0, 0), (0, LANES - N_EXPERTS)))
    br = jnp.pad(p["b_router"].astype(F32), (0, LANES - N_EXPERTS)).reshape(1, LANES)
    row = lambda v: v.astype(F32).reshape(1, -1)
    in_specs = [pl.BlockSpec((1, tm, Q_W), tok3),
                pl.BlockSpec((1, tm, MLV_W), tok3), pl.BlockSpec((1, tm, MLV_W), tok3),
                pl.BlockSpec((1, tm, MLV_W), tok3), pl.BlockSpec((1, tm, d), tok3), pl.BlockSpec((1, tm, d), tok3),
                pl.BlockSpec((1, tm, d), tok3),
                pl.BlockSpec((1, 6, d), lambda b, i: (mod_row(b), 0, 0)),
                _const_spec((1, MLV_W)), _const_spec((1, d)), _const_spec((1, d)),
                _const_spec((Q_W, d)), _const_spec((MLV_W, d)), _const_spec((d, d)),
                _const_spec((d, LANES)), _const_spec((1, LANES))]
    out_shape = [jax.ShapeDtypeStruct((bsz, seq, d), F32), jax.ShapeDtypeStruct((bsz, seq, d // 2), jnp.int32),
                 jax.ShapeDtypeStruct((bsz, seq, LANES), jnp.int32), jax.ShapeDtypeStruct((bsz, seq, LANES), F32),
                 jax.ShapeDtypeStruct((bsz * (seq // tm), tm // ROUTE_TILE, LANES), F32)]
    out_specs = [pl.BlockSpec((1, tm, d), tok3), pl.BlockSpec((1, tm, d // 2), tok3),
                 pl.BlockSpec((1, tm, LANES), tok3), pl.BlockSpec((1, tm, LANES), tok3),
                 pl.BlockSpec((1, tm // ROUTE_TILE, LANES), lambda b, i: (b * (seq // tm) + i, 0, 0))]
    return pl.pallas_call(
        _merge_kernel,
        grid=(bsz, seq // tm),
        in_specs=in_specs,
        out_specs=out_specs,
        out_shape=out_shape,
        compiler_params=_cparams(("parallel", "parallel")),
        name="merge_router",
    )(att, hf, hb, pj["mo"], pj["ga"], pj["gm"], x, mod, row(p["norm_ml"]), row(p["g_post_mix"]),
      row(p["g_pre_ffn"]), p["w_up_att"].astype(BF16), p["w_up_ml"].astype(BF16), p["w_out"].astype(BF16), wr, br)


def _sc_gather_rows(table, idx):
    n = idx.shape[0]
    width = table.shape[1]
    chunk = SC_GATHER_CHUNK
    n_workers = SC_CORES * SC_SUBCORES
    rows_per_worker = n // n_workers
    n_pairs = rows_per_worker // (2 * chunk)
    assert n_pairs * 2 * chunk * n_workers == n
    mesh = plsc.VectorSubcoreMesh(core_axis_name="c", subcore_axis_name="s")

    def body(table_hbm, idx_hbm, out_hbm, idx_v, rows_v, gsem, osem):
        wid = lax.axis_index("s") * SC_CORES + lax.axis_index("c")
        base0 = wid * rows_per_worker

        def gather_copy(b):
            return pltpu.make_async_copy(table_hbm.at[idx_v.at[b]], rows_v.at[b], gsem.at[b])

        def write_copy(ci, b):
            base = pl.multiple_of(base0 + ci * chunk, 8)
            return pltpu.make_async_copy(rows_v.at[b], out_hbm.at[pl.ds(base, chunk)], osem.at[b])

        def issue(ci, b):
            base = pl.multiple_of(base0 + ci * chunk, 8)
            pltpu.sync_copy(idx_hbm.at[pl.ds(base, chunk)], idx_v.at[b])
            gather_copy(b).start()

        def finish(ci, b):
            gather_copy(b).wait()
            write_copy(ci, b).start()

        issue(0, 0)

        @pl.loop(0, n_pairs)
        def _(j):
            @pl.when(j > 0)
            def _():
                write_copy(2 * j - 1, 1).wait()
            issue(2 * j + 1, 1)
            finish(2 * j, 0)

            @pl.when(j < n_pairs - 1)
            def _():
                write_copy(2 * j, 0).wait()
                issue(2 * j + 2, 0)
            finish(2 * j + 1, 1)

        write_copy(2 * n_pairs - 2, 0).wait()
        write_copy(2 * n_pairs - 1, 1).wait()

    return pl.kernel(
        body, mesh=mesh,
        out_type=jax.ShapeDtypeStruct((n, width), table.dtype),
        scratch_types=[pltpu.VMEM((2, chunk), jnp.int32), pltpu.VMEM((2, chunk, width), table.dtype),
                       pltpu.SemaphoreType.DMA((2,)), pltpu.SemaphoreType.DMA((2,))],
    )(table, idx)


def _ffn_kernel(be_ref, nu_ref, xs_ref, w1_ref, b1_ref, w2_ref, b2_ref, ys_ref, w1b_s, w2b_s):
    i = pl.program_id(0)
    e = be_ref[i]
    e_prev = be_ref[jnp.maximum(i - 1, 0)]

    @pl.when((i == 0) | (e != e_prev))
    def _():
        w1b_s[...] = w1_ref[0].astype(BF16)
        w2b_s[...] = w2_ref[0].astype(BF16)

    @pl.when(i < nu_ref[0])
    def _():
        half = D_MODEL // 2
        x_lo, x_hi = _unpack_bf16_pairs(xs_ref[...])
        hmid = (jnp.dot(x_lo.astype(BF16), w1b_s[:half, :], preferred_element_type=F32)
                + jnp.dot(x_hi.astype(BF16), w1b_s[half:, :], preferred_element_type=F32)) + b1_ref[0]
        glu = jnp.minimum(hmid[:, :D_FF], SWIGLU_LIMIT)
        lin = jnp.clip(hmid[:, D_FF:], -SWIGLU_LIMIT, SWIGLU_LIMIT)
        act = ((lin + 1.0) * glu * jax.nn.sigmoid(SWIGLU_ALPHA * glu)).astype(BF16)
        y = jnp.dot(act, w2b_s[...], preferred_element_type=F32) + b2_ref[0]
        ys_ref[...] = _pack_bf16_pairs(y)

    @pl.when(i >= nu_ref[0])
    def _():
        ys_ref[...] = jnp.zeros(ys_ref.shape, jnp.int32)


def _sc_scatter_rows(tables, positions, n_rows):
    width = tables[0].shape[1]
    chunk = SC_GATHER_CHUNK
    n_workers = SC_CORES * SC_SUBCORES
    n_streams = len(tables)
    mesh = plsc.VectorSubcoreMesh(core_axis_name="c", subcore_axis_name="s")

    def body(*refs):
        table_refs = refs[0:2 * n_streams:2]
        pos_refs = refs[1:2 * n_streams:2]
        out_hbm, idx_v, rows_v, lsem, ssem = refs[2 * n_streams:]
        wid = lax.axis_index("s") * SC_CORES + lax.axis_index("c")

        for table_hbm, pos_hbm in zip(table_refs, pos_refs):
            tok_per_worker = table_hbm.shape[0] // n_workers
            n_pairs = tok_per_worker // (2 * chunk)
            assert n_pairs * 2 * chunk * n_workers == table_hbm.shape[0]
            base0 = wid * tok_per_worker

            def load_copy(ci, b):
                base = pl.multiple_of(base0 + ci * chunk, 8)
                return pltpu.make_async_copy(table_hbm.at[pl.ds(base, chunk)], rows_v.at[b], lsem.at[b])

            def scatter_copy(b, k):
                return pltpu.make_async_copy(rows_v.at[b], out_hbm.at[idx_v.at[b, k]], ssem.at[b])

            def load(ci, b):
                base = pl.multiple_of(base0 + ci * chunk, 8)
                load_copy(ci, b).start()
                for k in range(TOP_K):
                    pltpu.sync_copy(pos_hbm.at[k, pl.ds(base, chunk)], idx_v.at[b, k])

            def scatter(ci, b):
                load_copy(ci, b).wait()
                for k in range(TOP_K):
                    scatter_copy(b, k).start()

            def drain(b):
                for k in range(TOP_K):
                    scatter_copy(b, k).wait()

            load(0, 0)

            @pl.loop(0, n_pairs)
            def _(j):
                @pl.when(j > 0)
                def _():
                    drain(1)
                load(2 * j + 1, 1)
                scatter(2 * j, 0)

                @pl.when(j < n_pairs - 1)
                def _():
                    drain(0)
                    load(2 * j + 2, 0)
                scatter(2 * j + 1, 1)

            drain(0)
            drain(1)

    args = [a for pair in zip(tables, positions) for a in pair]
    return pl.kernel(
        body, mesh=mesh,
        out_type=jax.ShapeDtypeStruct((n_rows, width), tables[0].dtype),
        scratch_types=[pltpu.VMEM((2, TOP_K, chunk), jnp.int32), pltpu.VMEM((2, chunk, width), tables[0].dtype),
                       pltpu.SemaphoreType.DMA((2,)), pltpu.SemaphoreType.DMA((2,))],
    )(*args)


def _pos_kernel(idx_ref, base_ref, pos_ref):
    idx = idx_ref[...]
    rows = idx.shape[0]
    lane = lax.broadcasted_iota(jnp.int32, idx.shape, 1)
    hots = [jnp.where(lane == idx[:, k:k + 1], 1.0, 0.0) for k in range(TOP_K)]
    cnt = hots[0] + hots[1] + hots[2] + hots[3]
    ri = lax.broadcasted_iota(jnp.int32, (rows, rows), 0)
    ci = lax.broadcasted_iota(jnp.int32, (rows, rows), 1)
    earlier = jnp.where(ci < ri, 1.0, 0.0).astype(BF16)
    ahead = jnp.dot(earlier, cnt.astype(BF16), preferred_element_type=F32) + base_ref[0]
    posmat = jnp.zeros(idx.shape, F32)
    for k in range(TOP_K):
        posmat = jnp.where(lane == k, jnp.sum(hots[k] * ahead, axis=-1, keepdims=True), posmat)
    pos_ref[...] = posmat.T[:TOP_K, :].astype(jnp.int32)


def _route(idx_list, count_list):
    tiles = [c.shape[0] * c.shape[1] for c in count_list]
    n_tok = sum(tiles) * ROUTE_TILE
    n_blocks = n_tok * TOP_K // MOE_ROWS + N_EXPERTS
    cnt = jnp.concatenate([c.reshape(-1, LANES) for c in count_list], axis=0).astype(jnp.int32)
    tile_off = jnp.cumsum(cnt, axis=0) - cnt
    total = jnp.sum(cnt, axis=0)
    padded = (total + MOE_ROWS - 1) // MOE_ROWS * MOE_ROWS
    pend = jnp.cumsum(padded)
    base = ((pend - padded)[None, :] + tile_off).astype(F32)
    starts = jnp.arange(n_blocks, dtype=jnp.int32) * MOE_ROWS
    block_e = jnp.minimum(jnp.sum(pend[None, :N_EXPERTS] <= starts[:, None], axis=1), N_EXPERTS - 1).astype(jnp.int32)
    n_used = (pend[N_EXPERTS - 1] // MOE_ROWS).astype(jnp.int32).reshape(1)
    positions, first = [], 0
    for idx, n_tiles in zip(idx_list, tiles):
        stream_tok = n_tiles * ROUTE_TILE
        positions.append(pl.pallas_call(
            _pos_kernel,
            grid=(n_tiles,),
            in_specs=[pl.BlockSpec((ROUTE_TILE, LANES), lambda i: (i, 0)),
                      pl.BlockSpec((1, 1, LANES), lambda i: (i, 0, 0))],
            out_specs=pl.BlockSpec((TOP_K, ROUTE_TILE), lambda i: (0, i)),
            out_shape=jax.ShapeDtypeStruct((TOP_K, stream_tok), jnp.int32),
            compiler_params=_cparams(("parallel",)),
            name="route_pos",
        )(idx.reshape(stream_tok, LANES), base[first:first + n_tiles].reshape(n_tiles, 1, LANES)))
        first += n_tiles
    return block_e, n_used, positions


def _moe(h2p_list, idx_list, count_list, w1, b1, w2, b2):
    half = h2p_list[0].shape[1]
    d = 2 * half
    block_e, n_used, positions = _route(idx_list, count_list)
    n_blocks = block_e.shape[0]
    xs = _sc_scatter_rows(h2p_list, positions, n_blocks * MOE_ROWS)
    grid_spec = pltpu.PrefetchScalarGridSpec(
        num_scalar_prefetch=2,
        grid=(n_blocks,),
        in_specs=[pl.BlockSpec((MOE_ROWS, half), lambda i, be, nu: (i, 0)),
                  pl.BlockSpec((1, d, 2 * D_FF), lambda i, be, nu: (be[i], 0, 0)),
                  pl.BlockSpec((1, 1, 2 * D_FF), lambda i, be, nu: (be[i], 0, 0)),
                  pl.BlockSpec((1, D_FF, d), lambda i, be, nu: (be[i], 0, 0)),
                  pl.BlockSpec((1, 1, d), lambda i, be, nu: (be[i], 0, 0))],
        out_specs=pl.BlockSpec((MOE_ROWS, half), lambda i, be, nu: (i, 0)),
        scratch_shapes=[pltpu.VMEM((d, 2 * D_FF), BF16), pltpu.VMEM((D_FF, d), BF16)],
    )
    ys = pl.pallas_call(
        _ffn_kernel,
        grid_spec=grid_spec,
        out_shape=jax.ShapeDtypeStruct((n_blocks * MOE_ROWS, half), jnp.int32),
        compiler_params=_cparams(("arbitrary",)),
        name="moe_ffn",
    )(block_e, n_used, xs, w1, b1.reshape(N_EXPERTS, 1, -1), w2, b2.reshape(N_EXPERTS, 1, -1))
    return [_sc_gather_rows(ys, pos.reshape(-1)) for pos in positions]


def _final_kernel(x1_ref, y0_ref, y1_ref, y2_ref, y3_ref, gate_ref, mod_ref, g_ref, o_ref):
    gates = gate_ref[0]
    lo = hi = None
    for k, y_ref in enumerate((y0_ref, y1_ref, y2_ref, y3_ref)):
        y_lo, y_hi = _unpack_bf16_pairs(y_ref[0, 0])
        gk = gates[:, k:k + 1]
        lo = gk * y_lo if k == 0 else lo + gk * y_lo
        hi = gk * y_hi if k == 0 else hi + gk * y_hi
    y = jnp.concatenate([lo, hi], axis=-1)
    gate2 = mod_ref[0, 5:6, :]
    o_ref[0] = x1_ref[0] + gate2 * _rms(y, g_ref[...])


def _final(x1, yg, gate, mod, mod_row, g_post_ffn, tm):
    bsz, seq, d = x1.shape
    tok3 = lambda b, i: (b, i, 0)
    yg = yg.reshape(TOP_K, bsz, seq, d // 2)
    slot_specs = [pl.BlockSpec((1, 1, tm, d // 2), functools.partial(lambda b, i, k: (k, b, i, 0), k=k))
                  for k in range(TOP_K)]
    return pl.pallas_call(
        _final_kernel,
        grid=(bsz, seq // tm),
        in_specs=[pl.BlockSpec((1, tm, d), tok3)] + slot_specs +
                 [pl.BlockSpec((1, tm, LANES), tok3),
                  pl.BlockSpec((1, 6, d), lambda b, i: (mod_row(b), 0, 0)), _const_spec((1, d))],
        out_specs=pl.BlockSpec((1, tm, d), tok3),
        out_shape=jax.ShapeDtypeStruct((bsz, seq, d), F32),
        compiler_params=_cparams(("parallel", "parallel")),
        name="final_residual",
    )(x1, yg, yg, yg, yg, gate, mod, g_post_ffn.astype(F32).reshape(1, d))


def _stream(x, mod, mod_row, p, ctx_kv, init_state, rope, tm):
    bsz, seq, d = x.shape
    is_ctx = ctx_kv is None
    pj = _project(x, mod, p["g_pre_mix"], p["w_in"], p["b_gates"], mod_row, rope, is_ctx, tm)
    if is_ctx:
        att = _attention(pj["q"], None, None, pj["kd"], pj["vd"], p["sink"], window=False)
    else:
        att = _attention(pj["q"], pj["kd"], pj["vd"], ctx_kv[0], ctx_kv[1], p["sink"], window=True)
    ml = _mlstm(pj["mq"], pj["mk"], pj["mv"], pj["gates"], init_state, emit_state=is_ctx)
    x1, h2, idx, gate, tile_counts = _merge(att, ml[0], ml[1], pj, x, mod, mod_row, p, tm)
    moe_in = (h2.reshape(bsz * seq, d // 2), idx, tile_counts)
    return x1, gate, moe_in, pj, ml


def kernel(x_prompt, x_sample, c, cache_k, cache_v, state_C, state_n, state_m, c_ctx, w_ada, b_ada, g_pre_mix,
           w_in, b_gates, attn_sink, norm_mlstm, w_up_att, w_up_ml, w_out, g_post_mix, g_pre_ffn, w_router,
           b_router, w1, b1, w2, b2, g_post_ffn):
    depth = w_ada.shape[0]
    n_dec = c.shape[0]
    cond = jnp.concatenate([c_ctx[None, :], c], axis=0).astype(F32)
    cond = jnp.pad(cond, ((0, 16 - cond.shape[0]), (0, 0)))
    y_prompt, y_sample = x_prompt, x_sample
    ks_, vs_, cs_, ns_, ms_ = [], [], [], [], []
    for l in range(depth):
        p = dict(g_pre_mix=g_pre_mix[l], w_in=w_in[l], b_gates=b_gates[l], sink=attn_sink[l], norm_ml=norm_mlstm[l],
                 w_up_att=w_up_att[l], w_up_ml=w_up_ml[l], w_out=w_out[l], g_post_mix=g_post_mix[l],
                 g_pre_ffn=g_pre_ffn[l], w_router=w_router[l], b_router=b_router[l], w1=w1[l], b1=b1[l],
                 w2=w2[l], b2=b2[l], g_post_ffn=g_post_ffn[l])
        mod = _adaln(cond, w_ada[l], b_ada[l]).reshape(16, 6, D_MODEL)
        row_p, row_s = (lambda b: 0), (lambda b: b + 1)
        x1_p, gate_p, moe_p, pj, ml = _stream(y_prompt, mod, row_p, p, None, None, rope=False, tm=256)
        bsz, seq = x_prompt.shape[:2]
        nat = pj["nat"]
        ks_.append(nat[..., :KV_W].reshape(bsz, seq, ATT_KV_HEADS, HEAD_DIM))
        vs_.append(nat[..., KV_W:].reshape(bsz, seq, ATT_KV_HEADS, HEAD_DIM))
        cs_.append(ml[2])
        ns_.append(ml[3])
        ms_.append(ml[4][..., 0])
        ctx_kv = (_dup_heads(cache_k[:, l]), _dup_heads(cache_v[:, l]))
        init = (state_C[:, l], state_n[:, l], state_m[:, l])
        x1_s, gate_s, moe_s, _, _ = _stream(y_sample, mod, row_s, p, ctx_kv, init, rope=True, tm=512)
        yg_p, yg_s = _moe([moe_p[0], moe_s[0]], [moe_p[1], moe_s[1]], [moe_p[2], moe_s[2]],
                          p["w1"], p["b1"].astype(F32), p["w2"], p["b2"].astype(F32))
        y_prompt = _final(x1_p, yg_p, gate_p, mod, row_p, p["g_post_ffn"], 256)
        y_sample = _final(x1_s, yg_s, gate_s, mod, row_s, p["g_post_ffn"], 512)
    return (y_prompt, y_sample, jnp.stack(ks_, axis=1), jnp.stack(vs_, axis=1), jnp.stack(cs_, axis=1),
            jnp.stack(ns_, axis=1), jnp.stack(ms_, axis=1))
```

```python
import functools

import numpy as np
import jax
import jax.numpy as jnp
from jax import lax
from jax.experimental import pallas as pl
from jax.experimental.pallas import tpu as pltpu
from jax.experimental.pallas import tpu_sc as plsc

F32 = jnp.float32
BF16 = jnp.bfloat16

D_MODEL = 1024
GRID_W = 64
ATT_HEADS = 8
ATT_KV_HEADS = 2
ATT_GROUP = ATT_HEADS // ATT_KV_HEADS
HEAD_DIM = 64
BLOCK = 128
ROPE_THETA = 10000.0
AXIS_FREQS = HEAD_DIM // 4
ML_HEADS = 4
ML_DK = 128
ML_DV = 256
ML_CHUNK = 128
N_EXPERTS = 32
TOP_K = 4
D_FF = 1024
SWIGLU_ALPHA = 1.702
SWIGLU_LIMIT = 7.0
EPS = 1e-6
NEG = -1e30

Q_W = ATT_HEADS * HEAD_DIM
KV_W = ATT_KV_HEADS * HEAD_DIM
MLQK_W = ML_HEADS * ML_DK
MLV_W = ML_HEADS * ML_DV
GATE_W = 4 * ML_HEADS

LANES = 128
VMEM_LIMIT = 56 * 1024 * 1024
MOE_ROWS = 512
SC_CORES = 2
SC_SUBCORES = 16
SC_GATHER_CHUNK = 32
ROUTE_TILE = 256


def _cparams(sem):
    return pltpu.CompilerParams(dimension_semantics=sem, vmem_limit_bytes=VMEM_LIMIT)


def _const_spec(shape):
    nd = len(shape)
    return pl.BlockSpec(shape, lambda *_: (0,) * nd)


def _adaln_kernel(cond_ref, w_ref, b_ref, o_ref):
    cnd = cond_ref[...]
    act = cnd * jax.nn.sigmoid(cnd)
    o_ref[...] = jnp.dot(act.astype(BF16), w_ref[...].astype(BF16),
                         preferred_element_type=F32) + b_ref[...]


def _adaln(cond, w_ada, b_ada):
    rows, d = cond.shape
    n = w_ada.shape[1]
    tn = 1536
    return pl.pallas_call(
        _adaln_kernel,
        grid=(n // tn,),
        in_specs=[_const_spec((rows, d)),
                  pl.BlockSpec((d, tn), lambda j: (0, j)),
                  pl.BlockSpec((1, tn), lambda j: (0, j))],
        out_specs=pl.BlockSpec((rows, tn), lambda j: (0, j)),
        out_shape=jax.ShapeDtypeStruct((rows, n), F32),
        compiler_params=_cparams(("parallel",)),
        name="adaln",
    )(cond, w_ada, b_ada.reshape(1, n))


def _pack_w_in(w_in, b_gates, rope, with_nat):
    o_q, o_k, o_v = 0, Q_W, Q_W + KV_W
    o_mq = Q_W + 2 * KV_W
    o_mk = o_mq + MLQK_W
    o_mv = o_mk + MLQK_W
    o_g = o_mv + MLV_W
    o_mo = o_g + GATE_W
    o_ga = o_mo + MLV_W
    o_gm = o_ga + D_MODEL
    half = HEAD_DIM // 2

    def head_cols(base, h, rot):
        lo = base + h * HEAD_DIM
        if rot:
            return [w_in[:, lo + half:lo + HEAD_DIM], w_in[:, lo:lo + half]]
        return [w_in[:, lo:lo + HEAD_DIM]]

    def q_cols(rot):
        cols = []
        for h in range(ATT_HEADS):
            cols += head_cols(o_q, h, rot)
        return cols

    def dup_cols(base, rot):
        cols = []
        for h in range(ATT_KV_HEADS):
            hc = head_cols(base, h, rot)
            cols += hc + hc
        return cols

    segs, cols, off = {}, [], 0

    def add(name, cl):
        nonlocal off
        width = sum(c.shape[1] for c in cl)
        segs[name] = (off, off + width)
        cols.extend(cl)
        off += width

    add("q", q_cols(False))
    if rope:
        add("qrot", q_cols(True))
    add("kd", dup_cols(o_k, False))
    if rope:
        add("kdrot", dup_cols(o_k, True))
    add("vd", dup_cols(o_v, False))
    if with_nat:
        add("nat", [w_in[:, o_k:o_k + 2 * KV_W]])
    add("mq", [w_in[:, o_mq:o_mq + MLQK_W]])
    add("mk", [w_in[:, o_mk:o_mk + MLQK_W]])
    add("mv", [w_in[:, o_mv:o_mv + MLV_W]])
    add("mo", [w_in[:, o_mo:o_mo + MLV_W]])
    add("ga", [w_in[:, o_ga:o_ga + D_MODEL]])
    add("gm", [w_in[:, o_gm:o_gm + D_MODEL]])
    add("gates", [w_in[:, o_g:o_g + GATE_W], jnp.zeros((D_MODEL, LANES - GATE_W), w_in.dtype)])
    wp = jnp.concatenate(cols, axis=1).astype(BF16)
    bg = jnp.pad(b_gates.astype(F32), (0, LANES - GATE_W)).reshape(1, LANES)
    return wp, bg, segs


def _rope_tables(n_tok):
    rows = n_tok // GRID_W
    row = np.repeat(np.arange(rows), GRID_W).astype(np.float32)
    col = np.tile(np.arange(GRID_W), rows).astype(np.float32)
    inv = (np.float32(ROPE_THETA) ** (-np.arange(AXIS_FREQS, dtype=np.float32) / AXIS_FREQS)).astype(np.float32)
    ang = np.concatenate([row[:, None] * inv, col[:, None] * inv], axis=-1).astype(np.float32)
    cos, sin = np.cos(ang), np.sin(ang)
    c64 = np.concatenate([cos, cos], axis=-1)
    s64 = np.concatenate([-sin, sin], axis=-1)
    return (jnp.asarray(np.tile(c64, (1, 2)), F32), jnp.asarray(np.tile(s64, (1, 2)), F32))


def _proj_kernel(*refs, segs, rope, with_nat):
    it = iter(refs)
    x_ref, mod_ref, g_ref, w_ref, bg_ref = (next(it) for _ in range(5))
    cos_ref = sin_ref = None
    if rope:
        cos_ref, sin_ref = next(it), next(it)
    q_ref, kd_ref, vd_ref = next(it), next(it), next(it)
    nat_ref = next(it) if with_nat else None
    mq_ref, mk_ref, mv_ref, gates_ref, mo_ref, ga_ref, gm_ref = (next(it) for _ in range(7))

    x = x_ref[0]
    shift = mod_ref[0, 0:1, :]
    scale = mod_ref[0, 1:2, :]
    ms = jnp.mean(x * x, axis=-1, keepdims=True)
    xn = x * lax.rsqrt(ms + EPS) * g_ref[...]
    xb = (xn * (1.0 + scale) + shift).astype(BF16)

    def seg(name):
        lo, hi = segs[name]
        return jnp.dot(xb, w_ref[:, lo:hi], preferred_element_type=F32)

    uq = seg("q")
    ukd = seg("kd")
    if rope:
        uqr = seg("qrot")
        ukr = seg("kdrot")
        cs = cos_ref[...]
        sn = sin_ref[...]
        for h in range(ATT_HEADS // 2):
            sl = slice(h * LANES, (h + 1) * LANES)
            q_ref[0, h] = ((uq[:, sl] * cs + uqr[:, sl] * sn) * HEAD_DIM ** -0.5).astype(BF16)
        for h in range(ATT_KV_HEADS):
            sl = slice(h * LANES, (h + 1) * LANES)
            kd_ref[0, :, sl] = (ukd[:, sl] * cs + ukr[:, sl] * sn).astype(BF16)
    else:
        for h in range(ATT_HEADS // 2):
            sl = slice(h * LANES, (h + 1) * LANES)
            q_ref[0, h] = (uq[:, sl] * HEAD_DIM ** -0.5).astype(BF16)
        kd_ref[0] = ukd.astype(BF16)
    vd_ref[0] = seg("vd").astype(BF16)
    if with_nat:
        nat_ref[0] = seg("nat")
    mq_ref[0] = (seg("mq") * ML_DK ** -0.5).astype(BF16)
    mk_ref[0] = seg("mk").astype(BF16)
    mv_ref[0] = seg("mv").astype(BF16)
    gates_ref[0] = seg("gates") + bg_ref[...]
    mo_ref[0] = seg("mo")
    ga_ref[0] = seg("ga")
    gm_ref[0] = seg("gm")


def _project(x, mod, g_pre, w_in, b_gates, mod_row, rope, with_nat, tm):
    bsz, seq, d = x.shape
    wp, bg, segs = _pack_w_in(w_in, b_gates, rope, with_nat)
    nw = wp.shape[1]
    tok3 = lambda b, i: (b, i, 0)
    in_specs = [pl.BlockSpec((1, tm, d), tok3),
                pl.BlockSpec((1, 6, d), lambda b, i: (mod_row(b), 0, 0)),
                _const_spec((1, d)),
                pl.BlockSpec((d, nw), lambda b, i: (0, 0), pipeline_mode=pl.Buffered(1)),
                _const_spec((1, LANES))]
    args = [x, mod, g_pre.reshape(1, d), wp, bg]
    if rope:
        cos_t, sin_t = _rope_tables(seq)
        in_specs += [pl.BlockSpec((tm, LANES), lambda b, i: (i, 0))] * 2
        args += [cos_t, sin_t]

    def tok_out(width, dtype):
        return (jax.ShapeDtypeStruct((bsz, seq, width), dtype), pl.BlockSpec((1, tm, width), tok3))

    outs = [(jax.ShapeDtypeStruct((bsz, ATT_HEADS // 2, seq, LANES), BF16),
             pl.BlockSpec((1, ATT_HEADS // 2, tm, LANES), lambda b, i: (b, 0, i, 0))),
            tok_out(2 * LANES, BF16), tok_out(2 * LANES, BF16)]
    if with_nat:
        outs.append(tok_out(2 * KV_W, F32))
    outs += [tok_out(MLQK_W, BF16), tok_out(MLQK_W, BF16), tok_out(MLV_W, BF16), tok_out(LANES, F32),
             tok_out(MLV_W, F32), tok_out(D_MODEL, F32), tok_out(D_MODEL, F32)]
    res = pl.pallas_call(
        functools.partial(_proj_kernel, segs=segs, rope=rope, with_nat=with_nat),
        grid=(bsz, seq // tm),
        in_specs=in_specs,
        out_specs=[o[1] for o in outs],
        out_shape=[o[0] for o in outs],
        compiler_params=_cparams(("parallel", "parallel")),
        name="proj_rope" if rope else "proj_ctx",
    )(*args)
    names = ["q", "kd", "vd"] + (["nat"] if with_nat else []) + ["mq", "mk", "mv", "gates", "mo", "ga", "gm"]
    return dict(zip(names, res))


def _attn_kernel(*refs, window, n_blocks, n_ctx):
    it = iter(refs)
    sink_ref, q_ref = next(it), next(it)
    if window:
        kp_ref, kc_ref, kn_ref, vp_ref, vc_ref, vn_ref = (next(it) for _ in range(6))
    ck_ref, cv_ref, o_ref = next(it), next(it), next(it)
    j = pl.program_id(1)
    n_win = 3 * BLOCK if window else 0
    n_keys = n_win + n_ctx
    rows = ATT_GROUP * BLOCK

    row_id = lax.broadcasted_iota(jnp.int32, (rows, 1), 0)
    if window:
        rl = lax.broadcasted_iota(jnp.int32, (rows, BLOCK), 0) & (BLOCK - 1)
        cl = lax.broadcasted_iota(jnp.int32, (rows, BLOCK), 1)
        prev_ok = cl >= rl + jnp.where(j == 0, BLOCK, 0)
        next_ok = cl <= rl - jnp.where(j == n_blocks - 1, BLOCK, 0)
    lane_lo = lax.broadcasted_iota(jnp.int32, (BLOCK, LANES), 1) < HEAD_DIM
    lane_row = lax.broadcasted_iota(jnp.int32, (1, LANES), 1) < HEAD_DIM
    half_lo = jnp.where(lane_row, 1.0, 0.0).astype(BF16)
    half_hi = jnp.where(lane_row, 0.0, 1.0).astype(BF16)

    for kvh in range(ATT_KV_HEADS):
        sl = slice(kvh * LANES, (kvh + 1) * LANES)
        if window:
            keys = jnp.concatenate([kp_ref[0, :, sl], kc_ref[0, :, sl], kn_ref[0, :, sl], ck_ref[0, :, sl]], axis=0)
            vals = jnp.concatenate([vp_ref[0, :, sl], vc_ref[0, :, sl], vn_ref[0, :, sl], cv_ref[0, :, sl]], axis=0)
        else:
            keys = ck_ref[0, :, sl]
            vals = cv_ref[0, :, sl]
        qs = jnp.concatenate([q_ref[0, kvh * (ATT_GROUP // 2) + g // 2] * (half_lo if g % 2 == 0 else half_hi)
                              for g in range(ATT_GROUP)], axis=0)
        s = lax.dot_general(qs, keys, (((1,), (1,)), ((), ())), preferred_element_type=F32)
        if window:
            s = jnp.concatenate([jnp.where(prev_ok, s[:, :BLOCK], NEG), s[:, BLOCK:2 * BLOCK],
                                 jnp.where(next_ok, s[:, 2 * BLOCK:n_win], NEG), s[:, n_win:]], axis=1)
        snk = jnp.full((rows, 1), sink_ref[kvh * ATT_GROUP + ATT_GROUP - 1], F32)
        for g in range(ATT_GROUP - 2, -1, -1):
            snk = jnp.where(row_id < (g + 1) * BLOCK, sink_ref[kvh * ATT_GROUP + g], snk)
        m = jnp.maximum(jnp.max(s, axis=-1, keepdims=True), snk)
        p = jnp.exp(s - m)
        den = jnp.sum(p, axis=-1, keepdims=True) + jnp.exp(snk - m)
        o = jnp.dot(p.astype(BF16), vals, preferred_element_type=F32) / den
        for pair in range(ATT_GROUP // 2):
            even = o[(2 * pair) * BLOCK:(2 * pair + 1) * BLOCK]
            odd = o[(2 * pair + 1) * BLOCK:(2 * pair + 2) * BLOCK]
            col = (kvh * (ATT_GROUP // 2) + pair) * LANES
            o_ref[0, :, col:col + LANES] = jnp.where(lane_lo, even, odd).astype(BF16)


def _attention(q, kd, vd, ckd, cvd, sink, window):
    bsz, _, seq, _ = q.shape
    nb = seq // BLOCK
    n_ctx = ckd.shape[1]
    in_specs = [pl.BlockSpec(memory_space=pltpu.SMEM),
                pl.BlockSpec((1, ATT_HEADS // 2, BLOCK, LANES), lambda b, j: (b, 0, j, 0))]
    args = [sink.astype(F32), q]
    if window:
        prev = lambda b, j: (b, jnp.maximum(j - 1, 0), 0)
        cur = lambda b, j: (b, j, 0)
        nxt = lambda b, j: (b, jnp.minimum(j + 1, nb - 1), 0)
        for arr in (kd, vd):
            in_specs += [pl.BlockSpec((1, BLOCK, 2 * LANES), f) for f in (prev, cur, nxt)]
            args += [arr, arr, arr]
    in_specs += [pl.BlockSpec((1, n_ctx, 2 * LANES), lambda b, j: (b, 0, 0))] * 2
    args += [ckd, cvd]
    return pl.pallas_call(
        functools.partial(_attn_kernel, window=window, n_blocks=nb, n_ctx=n_ctx),
        grid=(bsz, nb),
        in_specs=in_specs,
        out_specs=pl.BlockSpec((1, BLOCK, Q_W), lambda b, j: (b, j, 0)),
        out_shape=jax.ShapeDtypeStruct((bsz, seq, Q_W), BF16),
        compiler_params=_cparams(("parallel", "parallel")),
        name="attn_window" if window else "attn_ctx",
    )(*args)


def _dup_heads(t):
    b, l = t.shape[:2]
    return jnp.concatenate([t, t], axis=-1).reshape(b, l, ATT_KV_HEADS * LANES).astype(BF16)


def _split3(x):
    hi = x.astype(BF16)
    r1 = x - hi.astype(F32)
    mid = r1.astype(BF16)
    lo = (r1 - mid.astype(F32)).astype(BF16)
    return hi, mid, lo


def _log_sigmoid(x):
    return jnp.minimum(x, 0.0) - jnp.log1p(jnp.exp(-jnp.abs(x)))


def _mlstm_kernel(*refs, n_chunks, has_init, emit_state):
    it = iter(refs)
    qkvg = [[next(it) for _ in range(4)] for _ in range(2)]
    if has_init:
        c0_ref, n0_ref, m0_ref = next(it), next(it), next(it)
    h_refs = [next(it), next(it)]
    if emit_state:
        co_ref, no_ref, mo_ref = next(it), next(it), next(it)
    ct_s, n_s, m_s = next(it), next(it), next(it)
    t = pl.program_id(1)
    L = ML_CHUNK

    @pl.when(t == 0)
    def _():
        for d in range(2):
            for h in range(ML_HEADS):
                if has_init:
                    ct_s[d, h] = c0_ref[0, d, h].T
                    n_s[d, h] = n0_ref[0, d, h:h + 1, :]
                    m_s[d, h] = m0_ref[0, d, h:h + 1, :]
                else:
                    ct_s[d, h] = jnp.zeros((ML_DK, ML_DV), F32)
                    n_s[d, h] = jnp.zeros((1, ML_DK), F32)
                    m_s[d, h] = jnp.zeros((1, LANES), F32)

    ri = lax.broadcasted_iota(jnp.int32, (L, L), 0)
    ci = lax.broadcasted_iota(jnp.int32, (L, L), 1)

    for d in range(2):
        q_ref, k_ref, v_ref, g_ref = qkvg[d]
        tri = (ci <= ri) if d == 0 else (ci >= ri)
        tri_b = jnp.where(tri, 1.0, 0.0).astype(BF16)
        g = g_ref[0]
        lf = _log_sigmoid(g)
        g_t = g.T
        lf_t = lf.T
        bc = sum(jnp.dot(tri_b, part, preferred_element_type=F32) for part in _split3(lf))
        br = sum(lax.dot_general(part, tri_b, (((1,), (1,)), ((), ())), preferred_element_type=F32)
                 for part in _split3(lf_t))
        tot = jnp.sum(lf, axis=0, keepdims=True)

        for h in range(ML_HEADS):
            icol_i = 2 * ML_HEADS * d + h
            fcol = icol_i + ML_HEADS
            bcol = bc[:, fcol:fcol + 1]
            brow = br[fcol:fcol + 1, :]
            irow = g_t[icol_i:icol_i + 1, :]
            icol = g[:, icol_i:icol_i + 1]
            b_last = tot[:, fcol:fcol + 1]
            m_prev = m_s[d, h][:, 0:1]
            n_prev = n_s[d, h]
            ct_prev = ct_s[d, h]
            qh = q_ref[0, :, h * ML_DK:(h + 1) * ML_DK]
            kh = k_ref[0, :, h * ML_DK:(h + 1) * ML_DK]
            vh = v_ref[0, :, h * ML_DV:(h + 1) * ML_DV]

            rel = jnp.where(tri, irow - brow, NEG)
            inter = bcol + m_prev
            m_t = jnp.maximum(inter, bcol + jnp.max(rel, axis=-1, keepdims=True))
            w = jnp.exp(rel + (bcol - m_t))
            w_inter = jnp.exp(inter - m_t)
            s = lax.dot_general(qh, kh, (((1,), (1,)), ((), ())), preferred_element_type=F32) * w
            cq = jnp.dot(qh, ct_prev.astype(BF16), preferred_element_type=F32) * w_inter
            num = jnp.dot(s.astype(BF16), vh, preferred_element_type=F32) + cq
            qn = jnp.sum(qh.astype(F32) * n_prev, axis=-1, keepdims=True)
            den = jnp.sum(s, axis=-1, keepdims=True) + w_inter * qn
            h_refs[d][0, :, h * ML_DV:(h + 1) * ML_DV] = num / jnp.maximum(jnp.abs(den), jnp.exp(-m_t))

            a = b_last - bcol + icol
            m_new = jnp.maximum(b_last + m_prev, jnp.max(a, axis=0, keepdims=True))
            wk = jnp.exp(a - m_new)
            decay = jnp.exp(b_last + m_prev - m_new)
            kw = kh.astype(F32) * wk
            ct_new = decay * ct_prev + jnp.dot(kw.T.astype(BF16), vh, preferred_element_type=F32)
            n_new = decay * n_prev + jnp.sum(kw, axis=0, keepdims=True)
            ct_s[d, h] = ct_new
            n_s[d, h] = n_new
            m_s[d, h] = jnp.broadcast_to(m_new, (1, LANES))
            if emit_state:
                @pl.when(t == n_chunks - 1)
                def _():
                    co_ref[0, d, h] = ct_new.T
                    no_ref[0, d, h:h + 1, :] = n_new
                    mo_ref[0, d, h:h + 1, :] = jnp.broadcast_to(m_new, (1, LANES))


def _mlstm(mq, mk, mv, gates, init, emit_state):
    bsz, seq, _ = mq.shape
    nc = seq // ML_CHUNK
    fwd = lambda b, t: (b, t, 0)
    bwd = lambda b, t: (b, nc - 1 - t, 0)
    state5 = lambda b, t: (b, 0, 0, 0, 0)
    state4 = lambda b, t: (b, 0, 0, 0)
    in_specs, args = [], []
    for chunk in (fwd, bwd):
        in_specs += [pl.BlockSpec((1, ML_CHUNK, MLQK_W), chunk), pl.BlockSpec((1, ML_CHUNK, MLQK_W), chunk),
                     pl.BlockSpec((1, ML_CHUNK, MLV_W), chunk), pl.BlockSpec((1, ML_CHUNK, LANES), chunk)]
        args += [mq, mk, mv, gates]
    if init is not None:
        c0, n0, m0 = init
        in_specs += [pl.BlockSpec((1, 2, ML_HEADS, ML_DV, ML_DK), state5),
                     pl.BlockSpec((1, 2, ML_HEADS, ML_DK), state4),
                     pl.BlockSpec((1, 2, ML_HEADS, LANES), state4)]
        args += [c0.astype(F32), n0.astype(F32),
                 jnp.broadcast_to(m0.astype(F32)[..., None], m0.shape + (LANES,))]
    out_shape = [jax.ShapeDtypeStruct((bsz, seq, MLV_W), F32)] * 2
    out_specs = [pl.BlockSpec((1, ML_CHUNK, MLV_W), fwd), pl.BlockSpec((1, ML_CHUNK, MLV_W), bwd)]
    if emit_state:
        out_shape += [jax.ShapeDtypeStruct((bsz, 2, ML_HEADS, ML_DV, ML_DK), F32),
                      jax.ShapeDtypeStruct((bsz, 2, ML_HEADS, ML_DK), F32),
                      jax.ShapeDtypeStruct((bsz, 2, ML_HEADS, LANES), F32)]
        out_specs += [pl.BlockSpec((1, 2, ML_HEADS, ML_DV, ML_DK), state5),
                      pl.BlockSpec((1, 2, ML_HEADS, ML_DK), state4),
                      pl.BlockSpec((1, 2, ML_HEADS, LANES), state4)]
    return pl.pallas_call(
        functools.partial(_mlstm_kernel, n_chunks=nc, has_init=init is not None, emit_state=emit_state),
        grid=(bsz, nc),
        in_specs=in_specs,
        out_specs=out_specs,
        out_shape=out_shape,
        scratch_shapes=[pltpu.VMEM((2, ML_HEADS, ML_DK, ML_DV), F32),
                        pltpu.VMEM((2, ML_HEADS, 1, ML_DK), F32),
                        pltpu.VMEM((2, ML_HEADS, 1, LANES), F32)],
        compiler_params=_cparams(("parallel", "arbitrary")),
        name="mlstm_state" if emit_state else "mlstm",
    )(*args)


def _rms(x, g):
    return x * lax.rsqrt(jnp.mean(x * x, axis=-1, keepdims=True) + EPS) * g


def _pack_bf16_pairs(x):
    n = x.shape[1] // 2
    lo = pltpu.bitcast(x[:, :n].astype(BF16).astype(F32), jnp.int32)
    hi = pltpu.bitcast(x[:, n:].astype(BF16).astype(F32), jnp.int32)
    return lax.shift_right_logical(lo, 16) | hi


def _unpack_bf16_pairs(p):
    lo = pltpu.bitcast(lax.shift_left(p, 16), F32)
    hi = pltpu.bitcast(p & jnp.int32(-65536), F32)
    return lo, hi


def _merge_kernel(att_ref, hf_ref, hb_ref, mo_ref, ga_ref, gm_ref, x_ref, mod_ref, nml_ref, gpm_ref, gpf_ref,
                  wua_ref, wum_ref, wo_ref, wr_ref, br_ref, x1_ref, h2_ref, idx_ref, gate_ref, cnt_ref):
    gate1 = mod_ref[0, 2:3, :]
    shift2 = mod_ref[0, 3:4, :]
    scale2 = mod_ref[0, 4:5, :]
    wr = wr_ref[...]
    wr_hi = wr.astype(BF16)
    wr_lo = (wr - wr_hi.astype(F32)).astype(BF16)

    for r in range(x_ref.shape[1] // ROUTE_TILE):
        rs = slice(r * ROUTE_TILE, (r + 1) * ROUTE_TILE)
        hsum = hf_ref[0, rs] + hb_ref[0, rs]
        parts = []
        for h in range(ML_HEADS):
            hh = hsum[:, h * ML_DV:(h + 1) * ML_DV]
            parts.append(hh * lax.rsqrt(jnp.mean(hh * hh, axis=-1, keepdims=True) + EPS))
        hn = jnp.concatenate(parts, axis=-1) * nml_ref[...]
        ml = (hn * jax.nn.sigmoid(mo_ref[0, rs])).astype(BF16)
        a = jnp.dot(att_ref[0, rs], wua_ref[...], preferred_element_type=F32)
        m = jnp.dot(ml, wum_ref[...], preferred_element_type=F32)
        z = (jax.nn.sigmoid(ga_ref[0, rs]) * a + jax.nn.sigmoid(gm_ref[0, rs]) * m).astype(BF16)
        mix = jnp.dot(z, wo_ref[...], preferred_element_type=F32)
        x1 = x_ref[0, rs] + gate1 * _rms(mix, gpm_ref[...])
        x1_ref[0, rs] = x1
        h2 = _rms(x1, gpf_ref[...]) * (1.0 + scale2) + shift2
        h2_ref[0, rs] = _pack_bf16_pairs(h2)

        h2_hi = h2.astype(BF16)
        h2_lo = (h2 - h2_hi.astype(F32)).astype(BF16)
        logits = (jnp.dot(h2_hi, wr_hi, preferred_element_type=F32)
                  + jnp.dot(h2_lo, wr_hi, preferred_element_type=F32)
                  + jnp.dot(h2_hi, wr_lo, preferred_element_type=F32)) + br_ref[...]
        lane = lax.broadcasted_iota(jnp.int32, logits.shape, 1).astype(F32)
        work = jnp.where(lane < N_EXPERTS, logits, -jnp.inf)
        idx_out = jnp.zeros(logits.shape, F32)
        val_out = jnp.zeros(logits.shape, F32)
        picked = jnp.zeros(logits.shape, F32)
        top0 = None
        esum = None
        for k in range(TOP_K):
            mx = jnp.max(work, axis=-1, keepdims=True)
            sel = jnp.min(jnp.where(work == mx, lane, float(LANES)), axis=-1, keepdims=True)
            if k == 0:
                top0 = mx
            e = jnp.exp(mx - top0)
            esum = e if k == 0 else esum + e
            idx_out = jnp.where(lane == k, sel, idx_out)
            val_out = jnp.where(lane == k, e, val_out)
            picked = jnp.where(lane == sel, 1.0, picked)
            work = jnp.where(lane == sel, -jnp.inf, work)
        idx_ref[0, rs] = idx_out.astype(jnp.int32)
        gate_ref[0, rs] = val_out / esum
        cnt_ref[0, r:r + 1, :] = jnp.sum(picked, axis=0, keepdims=True)


def _merge(att, hf, hb, pj, x, mod, mod_row, p, tm):
    bsz, seq, d = x.shape
    tok3 = lambda b, i: (b, i, 0)
    wr = jnp.pad(p["w_router"].astype(F32), ((0, 0), (0, LANES - N_EXPERTS)))
    br = jnp.pad(p["b_router"].astype(F32), (0, LANES - N_EXPERTS)).reshape(1, LANES)
    row = lambda v: v.astype(F32).reshape(1, -1)
    in_specs = [pl.BlockSpec((1, tm, Q_W), tok3),
                pl.BlockSpec((1, tm, MLV_W), tok3), pl.BlockSpec((1, tm, MLV_W), tok3),
                pl.BlockSpec((1, tm, MLV_W), tok3), pl.BlockSpec((1, tm, d), tok3), pl.BlockSpec((1, tm, d), tok3),
                pl.BlockSpec((1, tm, d), tok3),
                pl.BlockSpec((1, 6, d), lambda b, i: (mod_row(b), 0, 0)),
                _const_spec((1, MLV_W)), _const_spec((1, d)), _const_spec((1, d)),
                _const_spec((Q_W, d)), _const_spec((MLV_W, d)), _const_spec((d, d)),
                _const_spec((d, LANES)), _const_spec((1, LANES))]
    out_shape = [jax.ShapeDtypeStruct((bsz, seq, d), F32), jax.ShapeDtypeStruct((bsz, seq, d // 2), jnp.int32),
                 jax.ShapeDtypeStruct((bsz, seq, LANES), jnp.int32), jax.ShapeDtypeStruct((bsz, seq, LANES), F32),
                 jax.ShapeDtypeStruct((bsz * (seq // tm), tm // ROUTE_TILE, LANES), F32)]
    out_specs = [pl.BlockSpec((1, tm, d), tok3), pl.BlockSpec((1, tm, d // 2), tok3),
                 pl.BlockSpec((1, tm, LANES), tok3), pl.BlockSpec((1, tm, LANES), tok3),
                 pl.BlockSpec((1, tm // ROUTE_TILE, LANES), lambda b, i: (b * (seq // tm) + i, 0, 0))]
    return pl.pallas_call(
        _merge_kernel,
        grid=(bsz, seq // tm),
        in_specs=in_specs,
        out_specs=out_specs,
        out_shape=out_shape,
        compiler_params=_cparams(("parallel", "parallel")),
        name="merge_router",
    )(att, hf, hb, pj["mo"], pj["ga"], pj["gm"], x, mod, row(p["norm_ml"]), row(p["g_post_mix"]),
      row(p["g_pre_ffn"]), p["w_up_att"].astype(BF16), p["w_up_ml"].astype(BF16), p["w_out"].astype(BF16), wr, br)


def _sc_gather_rows(table, idx):
    n = idx.shape[0]
    width = table.shape[1]
    chunk = SC_GATHER_CHUNK
    n_workers = SC_CORES * SC_SUBCORES
    rows_per_worker = n // n_workers
    n_pairs = rows_per_worker // (2 * chunk)
    assert n_pairs * 2 * chunk * n_workers == n
    mesh = plsc.VectorSubcoreMesh(core_axis_name="c", subcore_axis_name="s")

    def body(table_hbm, idx_hbm, out_hbm, idx_v, rows_v, gsem, osem):
        wid = lax.axis_index("s") * SC_CORES + lax.axis_index("c")
        base0 = wid * rows_per_worker

        def gather_copy(b):
            return pltpu.make_async_copy(table_hbm.at[idx_v.at[b]], rows_v.at[b], gsem.at[b])

        def write_copy(ci, b):
            base = pl.multiple_of(base0 + ci * chunk, 8)
            return pltpu.make_async_copy(rows_v.at[b], out_hbm.at[pl.ds(base, chunk)], osem.at[b])

        def issue(ci, b):
            base = pl.multiple_of(base0 + ci * chunk, 8)
            pltpu.sync_copy(idx_hbm.at[pl.ds(base, chunk)], idx_v.at[b])
            gather_copy(b).start()

        def finish(ci, b):
            gather_copy(b).wait()
            write_copy(ci, b).start()

        issue(0, 0)

        @pl.loop(0, n_pairs)
        def _(j):
            @pl.when(j > 0)
            def _():
                write_copy(2 * j - 1, 1).wait()
            issue(2 * j + 1, 1)
            finish(2 * j, 0)

            @pl.when(j < n_pairs - 1)
            def _():
                write_copy(2 * j, 0).wait()
                issue(2 * j + 2, 0)
            finish(2 * j + 1, 1)

        write_copy(2 * n_pairs - 2, 0).wait()
        write_copy(2 * n_pairs - 1, 1).wait()

    return pl.kernel(
        body, mesh=mesh,
        out_type=jax.ShapeDtypeStruct((n, width), table.dtype),
        scratch_types=[pltpu.VMEM((2, chunk), jnp.int32), pltpu.VMEM((2, chunk, width), table.dtype),
                       pltpu.SemaphoreType.DMA((2,)), pltpu.SemaphoreType.DMA((2,))],
    )(table, idx)


def _ffn_kernel(be_ref, nu_ref, xs_ref, w1_ref, b1_ref, w2_ref, b2_ref, ys_ref, w1b_s, w2b_s):
    i = pl.program_id(0)
    e = be_ref[i]
    e_prev = be_ref[jnp.maximum(i - 1, 0)]

    @pl.when((i == 0) | (e != e_prev))
    def _():
        w1b_s[...] = w1_ref[0].astype(BF16)
        w2b_s[...] = w2_ref[0].astype(BF16)

    @pl.when(i < nu_ref[0])
    def _():
        half = D_MODEL // 2
        x_lo, x_hi = _unpack_bf16_pairs(xs_ref[...])
        hmid = (jnp.dot(x_lo.astype(BF16), w1b_s[:half, :], preferred_element_type=F32)
                + jnp.dot(x_hi.astype(BF16), w1b_s[half:, :], preferred_element_type=F32)) + b1_ref[0]
        glu = jnp.minimum(hmid[:, :D_FF], SWIGLU_LIMIT)
        lin = jnp.clip(hmid[:, D_FF:], -SWIGLU_LIMIT, SWIGLU_LIMIT)
        act = ((lin + 1.0) * glu * jax.nn.sigmoid(SWIGLU_ALPHA * glu)).astype(BF16)
        y = jnp.dot(act, w2b_s[...], preferred_element_type=F32) + b2_ref[0]
        ys_ref[...] = _pack_bf16_pairs(y)

    @pl.when(i >= nu_ref[0])
    def _():
        ys_ref[...] = jnp.zeros(ys_ref.shape, jnp.int32)


def _sc_scatter_rows(tables, positions, n_rows):
    width = tables[0].shape[1]
    chunk = SC_GATHER_CHUNK
    n_workers = SC_CORES * SC_SUBCORES
    n_streams = len(tables)
    mesh = plsc.VectorSubcoreMesh(core_axis_name="c", subcore_axis_name="s")

    def body(*refs):
        table_refs = refs[0:2 * n_streams:2]
        pos_refs = refs[1:2 * n_streams:2]
        out_hbm, idx_v, rows_v, lsem, ssem = refs[2 * n_streams:]
        wid = lax.axis_index("s") * SC_CORES + lax.axis_index("c")

        for table_hbm, pos_hbm in zip(table_refs, pos_refs):
            tok_per_worker = table_hbm.shape[0] // n_workers
            n_pairs = tok_per_worker // (2 * chunk)
            assert n_pairs * 2 * chunk * n_workers == table_hbm.shape[0]
            base0 = wid * tok_per_worker

            def load_copy(ci, b):
                base = pl.multiple_of(base0 + ci * chunk, 8)
                return pltpu.make_async_copy(table_hbm.at[pl.ds(base, chunk)], rows_v.at[b], lsem.at[b])

            def scatter_copy(b, k):
                return pltpu.make_async_copy(rows_v.at[b], out_hbm.at[idx_v.at[b, k]], ssem.at[b])

            def load(ci, b):
                base = pl.multiple_of(base0 + ci * chunk, 8)
                load_copy(ci, b).start()
                for k in range(TOP_K):
                    pltpu.sync_copy(pos_hbm.at[k, pl.ds(base, chunk)], idx_v.at[b, k])

            def scatter(ci, b):
                load_copy(ci, b).wait()
                for k in range(TOP_K):
                    scatter_copy(b, k).start()

            def drain(b):
                for k in range(TOP_K):
                    scatter_copy(b, k).wait()

            load(0, 0)

            @pl.loop(0, n_pairs)
            def _(j):
                @pl.when(j > 0)
                def _():
                    drain(1)
                load(2 * j + 1, 1)
                scatter(2 * j, 0)

                @pl.when(j < n_pairs - 1)
                def _():
                    drain(0)
                    load(2 * j + 2, 0)
                scatter(2 * j + 1, 1)

            drain(0)
            drain(1)

    args = [a for pair in zip(tables, positions) for a in pair]
    return pl.kernel(
        body, mesh=mesh,
        out_type=jax.ShapeDtypeStruct((n_rows, width), tables[0].dtype),
        scratch_types=[pltpu.VMEM((2, TOP_K, chunk), jnp.int32), pltpu.VMEM((2, chunk, width), tables[0].dtype),
                       pltpu.SemaphoreType.DMA((2,)), pltpu.SemaphoreType.DMA((2,))],
    )(*args)


def _pos_kernel(idx_ref, base_ref, pos_ref):
    idx = idx_ref[...]
    rows = idx.shape[0]
    lane = lax.broadcasted_iota(jnp.int32, idx.shape, 1)
    hots = [jnp.where(lane == idx[:, k:k + 1], 1.0, 0.0) for k in range(TOP_K)]
    cnt = hots[0] + hots[1] + hots[2] + hots[3]
    ri = lax.broadcasted_iota(jnp.int32, (rows, rows), 0)
    ci = lax.broadcasted_iota(jnp.int32, (rows, rows), 1)
    earlier = jnp.where(ci < ri, 1.0, 0.0).astype(BF16)
    ahead = jnp.dot(earlier, cnt.astype(BF16), preferred_element_type=F32) + base_ref[0]
    posmat = jnp.zeros(idx.shape, F32)
    for k in range(TOP_K):
        posmat = jnp.where(lane == k, jnp.sum(hots[k] * ahead, axis=-1, keepdims=True), posmat)
    pos_ref[...] = posmat.T[:TOP_K, :].astype(jnp.int32)


def _route(idx_list, count_list):
    tiles = [c.shape[0] * c.shape[1] for c in count_list]
    n_tok = sum(tiles) * ROUTE_TILE
    n_blocks = n_tok * TOP_K // MOE_ROWS + N_EXPERTS
    cnt = jnp.concatenate([c.reshape(-1, LANES) for c in count_list], axis=0).astype(jnp.int32)
    tile_off = jnp.cumsum(cnt, axis=0) - cnt
    total = jnp.sum(cnt, axis=0)
    padded = (total + MOE_ROWS - 1) // MOE_ROWS * MOE_ROWS
    pend = jnp.cumsum(padded)
    base = ((pend - padded)[None, :] + tile_off).astype(F32)
    starts = jnp.arange(n_blocks, dtype=jnp.int32) * MOE_ROWS
    block_e = jnp.minimum(jnp.sum(pend[None, :N_EXPERTS] <= starts[:, None], axis=1), N_EXPERTS - 1).astype(jnp.int32)
    n_used = (pend[N_EXPERTS - 1] // MOE_ROWS).astype(jnp.int32).reshape(1)
    positions, first = [], 0
    for idx, n_tiles in zip(idx_list, tiles):
        stream_tok = n_tiles * ROUTE_TILE
        positions.append(pl.pallas_call(
            _pos_kernel,
            grid=(n_tiles,),
            in_specs=[pl.BlockSpec((ROUTE_TILE, LANES), lambda i: (i, 0)),
                      pl.BlockSpec((1, 1, LANES), lambda i: (i, 0, 0))],
            out_specs=pl.BlockSpec((TOP_K, ROUTE_TILE), lambda i: (0, i)),
            out_shape=jax.ShapeDtypeStruct((TOP_K, stream_tok), jnp.int32),
            compiler_params=_cparams(("parallel",)),
            name="route_pos",
        )(idx.reshape(stream_tok, LANES), base[first:first + n_tiles].reshape(n_tiles, 1, LANES)))
        first += n_tiles
    return block_e, n_used, positions


def _moe(h2p_list, idx_list, count_list, w1, b1, w2, b2):
    half = h2p_list[0].shape[1]
    d = 2 * half
    block_e, n_used, positions = _route(idx_list, count_list)
    n_blocks = block_e.shape[0]
    xs = _sc_scatter_rows(h2p_list, positions, n_blocks * MOE_ROWS)
    grid_spec = pltpu.PrefetchScalarGridSpec(
        num_scalar_prefetch=2,
        grid=(n_blocks,),
        in_specs=[pl.BlockSpec((MOE_ROWS, half), lambda i, be, nu: (i, 0)),
                  pl.BlockSpec((1, d, 2 * D_FF), lambda i, be, nu: (be[i], 0, 0)),
                  pl.BlockSpec((1, 1, 2 * D_FF), lambda i, be, nu: (be[i], 0, 0)),
                  pl.BlockSpec((1, D_FF, d), lambda i, be, nu: (be[i], 0, 0)),
                  pl.BlockSpec((1, 1, d), lambda i, be, nu: (be[i], 0, 0))],
        out_specs=pl.BlockSpec((MOE_ROWS, half), lambda i, be, nu: (i, 0)),
        scratch_shapes=[pltpu.VMEM((d, 2 * D_FF), BF16), pltpu.VMEM((D_FF, d), BF16)],
    )
    ys = pl.pallas_call(
        _ffn_kernel,
        grid_spec=grid_spec,
        out_shape=jax.ShapeDtypeStruct((n_blocks * MOE_ROWS, half), jnp.int32),
        compiler_params=_cparams(("arbitrary",)),
        name="moe_ffn",
    )(block_e, n_used, xs, w1, b1.reshape(N_EXPERTS, 1, -1), w2, b2.reshape(N_EXPERTS, 1, -1))
    return [_sc_gather_rows(ys, pos.reshape(-1)) for pos in positions]


def _final_kernel(x1_ref, y0_ref, y1_ref, y2_ref, y3_ref, gate_ref, mod_ref, g_ref, o_ref):
    gates = gate_ref[0]
    lo = hi = None
    for k, y_ref in enumerate((y0_ref, y1_ref, y2_ref, y3_ref)):
        y_lo, y_hi = _unpack_bf16_pairs(y_ref[0, 0])
        gk = gates[:, k:k + 1]
        lo = gk * y_lo if k == 0 else lo + gk * y_lo
        hi = gk * y_hi if k == 0 else hi + gk * y_hi
    y = jnp.concatenate([lo, hi], axis=-1)
    gate2 = mod_ref[0, 5:6, :]
    o_ref[0] = x1_ref[0] + gate2 * _rms(y, g_ref[...])


def _final(x1, yg, gate, mod, mod_row, g_post_ffn, tm):
    bsz, seq, d = x1.shape
    tok3 = lambda b, i: (b, i, 0)
    yg = yg.reshape(TOP_K, bsz, seq, d // 2)
    slot_specs = [pl.BlockSpec((1, 1, tm, d // 2), functools.partial(lambda b, i, k: (k, b, i, 0), k=k))
                  for k in range(TOP_K)]
    return pl.pallas_call(
        _final_kernel,
        grid=(bsz, seq // tm),
        in_specs=[pl.BlockSpec((1, tm, d), tok3)] + slot_specs +
                 [pl.BlockSpec((1, tm, LANES), tok3),
                  pl.BlockSpec((1, 6, d), lambda b, i: (mod_row(b), 0, 0)), _const_spec((1, d))],
        out_specs=pl.BlockSpec((1, tm, d), tok3),
        out_shape=jax.ShapeDtypeStruct((bsz, seq, d), F32),
        compiler_params=_cparams(("parallel", "parallel")),
        name="final_residual",
    )(x1, yg, yg, yg, yg, gate, mod, g_post_ffn.astype(F32).reshape(1, d))


def _stream(x, mod, mod_row, p, ctx_kv, init_state, rope, tm):
    bsz, seq, d = x.shape
    is_ctx = ctx_kv is None
    pj = _project(x, mod, p["g_pre_mix"], p["w_in"], p["b_gates"], mod_row, rope, is_ctx, tm)
    if is_ctx:
        att = _attention(pj["q"], None, None, pj["kd"], pj["vd"], p["sink"], window=False)
    else:
        att = _attention(pj["q"], pj["kd"], pj["vd"], ctx_kv[0], ctx_kv[1], p["sink"], window=True)
    ml = _mlstm(pj["mq"], pj["mk"], pj["mv"], pj["gates"], init_state, emit_state=is_ctx)
    x1, h2, idx, gate, tile_counts = _merge(att, ml[0], ml[1], pj, x, mod, mod_row, p, tm)
    moe_in = (h2.reshape(bsz * seq, d // 2), idx, tile_counts)
    return x1, gate, moe_in, pj, ml


def kernel(x_prompt, x_sample, c, cache_k, cache_v, state_C, state_n, state_m, c_ctx, w_ada, b_ada, g_pre_mix,
           w_in, b_gates, attn_sink, norm_mlstm, w_up_att, w_up_ml, w_out, g_post_mix, g_pre_ffn, w_router,
           b_router, w1, b1, w2, b2, g_post_ffn):
    depth = w_ada.shape[0]
    n_dec = c.shape[0]
    cond = jnp.concatenate([c_ctx[None, :], c], axis=0).astype(F32)
    cond = jnp.pad(cond, ((0, 16 - cond.shape[0]), (0, 0)))
    y_prompt, y_sample = x_prompt, x_sample
    ks_, vs_, cs_, ns_, ms_ = [], [], [], [], []
    for l in range(depth):
        p = dict(g_pre_mix=g_pre_mix[l], w_in=w_in[l], b_gates=b_gates[l], sink=attn_sink[l], norm_ml=norm_mlstm[l],
                 w_up_att=w_up_att[l], w_up_ml=w_up_ml[l], w_out=w_out[l], g_post_mix=g_post_mix[l],
                 g_pre_ffn=g_pre_ffn[l], w_router=w_router[l], b_router=b_router[l], w1=w1[l], b1=b1[l],
                 w2=w2[l], b2=b2[l], g_post_ffn=g_post_ffn[l])
        mod = _adaln(cond, w_ada[l], b_ada[l]).reshape(16, 6, D_MODEL)
        row_p, row_s = (lambda b: 0), (lambda b: b + 1)
        x1_p, gate_p, moe_p, pj, ml = _stream(y_prompt, mod, row_p, p, None, None, rope=False, tm=256)
        bsz, seq = x_prompt.shape[:2]
        nat = pj["nat"]
        ks_.append(nat[..., :KV_W].reshape(bsz, seq, ATT_KV_HEADS, HEAD_DIM))
        vs_.append(nat[..., KV_W:].reshape(bsz, seq, ATT_KV_HEADS, HEAD_DIM))
        cs_.append(ml[2])
        ns_.append(ml[3])
        ms_.append(ml[4][..., 0])
        ctx_kv = (_dup_heads(cache_k[:, l]), _dup_heads(cache_v[:, l]))
        init = (state_C[:, l], state_n[:, l], state_m[:, l])
        x1_s, gate_s, moe_s, _, _ = _stream(y_sample, mod, row_s, p, ctx_kv, init, rope=True, tm=512)
        yg_p, yg_s = _moe([moe_p[0], moe_s[0]], [moe_p[1], moe_s[1]], [moe_p[2], moe_s[2]],
                          p["w1"], p["b1"].astype(F32), p["w2"], p["b2"].astype(F32))
        y_prompt = _final(x1_p, yg_p, gate_p, mod, row_p, p["g_post_ffn"], 256)
        y_sample = _final(x1_s, yg_s, gate_s, mod, row_s, p["g_post_ffn"], 512)
    return (y_prompt, y_sample, jnp.stack(ks_, axis=1), jnp.stack(vs_, axis=1), jnp.stack(cs_, axis=1),
            jnp.stack(ns_, axis=1), jnp.stack(ms_, axis=1))
```

```python
import functools

import numpy as np
import jax
import jax.numpy as jnp
from jax import lax
from jax.experimental import pallas as pl
from jax.experimental.pallas import tpu as pltpu
from jax.experimental.pallas import tpu_sc as plsc

F32 = jnp.float32
BF16 = jnp.bfloat16

D_MODEL = 1024
GRID_W = 64
ATT_HEADS = 8
ATT_KV_HEADS = 2
ATT_GROUP = ATT_HEADS // ATT_KV_HEADS
HEAD_DIM = 64
BLOCK = 128
ROPE_THETA = 10000.0
AXIS_FREQS = HEAD_DIM // 4
ML_HEADS = 4
ML_DK = 128
ML_DV = 256
ML_CHUNK = 128
N_EXPERTS = 32
TOP_K = 4
D_FF = 1024
SWIGLU_ALPHA = 1.702
SWIGLU_LIMIT = 7.0
EPS = 1e-6
NEG = -1e30

Q_W = ATT_HEADS * HEAD_DIM
KV_W = ATT_KV_HEADS * HEAD_DIM
MLQK_W = ML_HEADS * ML_DK
MLV_W = ML_HEADS * ML_DV
GATE_W = 4 * ML_HEADS

LANES = 128
VMEM_LIMIT = 56 * 1024 * 1024
MOE_ROWS = 512
SC_CORES = 2
SC_SUBCORES = 16
SC_GATHER_CHUNK = 32
ROUTE_TILE = 256


def _cparams(sem):
    return pltpu.CompilerParams(dimension_semantics=sem, vmem_limit_bytes=VMEM_LIMIT)


def _const_spec(shape):
    nd = len(shape)
    return pl.BlockSpec(shape, lambda *_: (0,) * nd)


def _adaln_kernel(cond_ref, w_ref, b_ref, o_ref):
    cnd = cond_ref[...]
    act = cnd * jax.nn.sigmoid(cnd)
    o_ref[...] = jnp.dot(act.astype(BF16), w_ref[...].astype(BF16),
                         preferred_element_type=F32) + b_ref[...]


def _adaln(cond, w_ada, b_ada):
    rows, d = cond.shape
    n = w_ada.shape[1]
    tn = 1536
    return pl.pallas_call(
        _adaln_kernel,
        grid=(n // tn,),
        in_specs=[_const_spec((rows, d)),
                  pl.BlockSpec((d, tn), lambda j: (0, j)),
                  pl.BlockSpec((1, tn), lambda j: (0, j))],
        out_specs=pl.BlockSpec((rows, tn), lambda j: (0, j)),
        out_shape=jax.ShapeDtypeStruct((rows, n), F32),
        compiler_params=_cparams(("parallel",)),
        name="adaln",
    )(cond, w_ada, b_ada.reshape(1, n))


def _pack_w_in(w_in, b_gates, rope, with_nat):
    o_q, o_k, o_v = 0, Q_W, Q_W + KV_W
    o_mq = Q_W + 2 * KV_W
    o_mk = o_mq + MLQK_W
    o_mv = o_mk + MLQK_W
    o_g = o_mv + MLV_W
    o_mo = o_g + GATE_W
    o_ga = o_mo + MLV_W
    o_gm = o_ga + D_MODEL
    half = HEAD_DIM // 2

    def head_cols(base, h, rot):
        lo = base + h * HEAD_DIM
        if rot:
            return [w_in[:, lo + half:lo + HEAD_DIM], w_in[:, lo:lo + half]]
        return [w_in[:, lo:lo + HEAD_DIM]]

    def q_cols(rot):
        cols = []
        for h in range(ATT_HEADS):
            cols += head_cols(o_q, h, rot)
        return cols

    def dup_cols(base, rot):
        cols = []
        for h in range(ATT_KV_HEADS):
            hc = head_cols(base, h, rot)
            cols += hc + hc
        return cols

    segs, cols, off = {}, [], 0

    def add(name, cl):
        nonlocal off
        width = sum(c.shape[1] for c in cl)
        segs[name] = (off, off + width)
        cols.extend(cl)
        off += width

    add("q", q_cols(False))
    if rope:
        add("qrot", q_cols(True))
    add("kd", dup_cols(o_k, False))
    if rope:
        add("kdrot", dup_cols(o_k, True))
    add("vd", dup_cols(o_v, False))
    if with_nat:
        add("nat", [w_in[:, o_k:o_k + 2 * KV_W]])
    add("mq", [w_in[:, o_mq:o_mq + MLQK_W]])
    add("mk", [w_in[:, o_mk:o_mk + MLQK_W]])
    add("mv", [w_in[:, o_mv:o_mv + MLV_W]])
    add("mo", [w_in[:, o_mo:o_mo + MLV_W]])
    add("ga", [w_in[:, o_ga:o_ga + D_MODEL]])
    add("gm", [w_in[:, o_gm:o_gm + D_MODEL]])
    add("gates", [w_in[:, o_g:o_g + GATE_W], jnp.zeros((D_MODEL, LANES - GATE_W), w_in.dtype)])
    wp = jnp.concatenate(cols, axis=1).astype(BF16)
    bg = jnp.pad(b_gates.astype(F32), (0, LANES - GATE_W)).reshape(1, LANES)
    return wp, bg, segs


def _rope_tables(n_tok):
    rows = n_tok // GRID_W
    row = np.repeat(np.arange(rows), GRID_W).astype(np.float32)
    col = np.tile(np.arange(GRID_W), rows).astype(np.float32)
    inv = (np.float32(ROPE_THETA) ** (-np.arange(AXIS_FREQS, dtype=np.float32) / AXIS_FREQS)).astype(np.float32)
    ang = np.concatenate([row[:, None] * inv, col[:, None] * inv], axis=-1).astype(np.float32)
    cos, sin = np.cos(ang), np.sin(ang)
    c64 = np.concatenate([cos, cos], axis=-1)
    s64 = np.concatenate([-sin, sin], axis=-1)
    return (jnp.asarray(np.tile(c64, (1, 2)), F32), jnp.asarray(np.tile(s64, (1, 2)), F32))


def _proj_kernel(*refs, segs, rope, with_nat):
    it = iter(refs)
    x_ref, mod_ref, g_ref, w_ref, bg_ref = (next(it) for _ in range(5))
    cos_ref = sin_ref = None
    if rope:
        cos_ref, sin_ref = next(it), next(it)
    q_ref, kd_ref, vd_ref = next(it), next(it), next(it)
    nat_ref = next(it) if with_nat else None
    mq_ref, mk_ref, mv_ref, gates_ref, mo_ref, ga_ref, gm_ref = (next(it) for _ in range(7))

    x = x_ref[0]
    shift = mod_ref[0, 0:1, :]
    scale = mod_ref[0, 1:2, :]
    ms = jnp.mean(x * x, axis=-1, keepdims=True)
    xn = x * lax.rsqrt(ms + EPS) * g_ref[...]
    xb = (xn * (1.0 + scale) + shift).astype(BF16)

    def seg(name):
        lo, hi = segs[name]
        return jnp.dot(xb, w_ref[:, lo:hi], preferred_element_type=F32)

    uq = seg("q")
    ukd = seg("kd")
    if rope:
        uqr = seg("qrot")
        ukr = seg("kdrot")
        cs = cos_ref[...]
        sn = sin_ref[...]
        for h in range(ATT_HEADS // 2):
            sl = slice(h * LANES, (h + 1) * LANES)
            q_ref[0, h] = ((uq[:, sl] * cs + uqr[:, sl] * sn) * HEAD_DIM ** -0.5).astype(BF16)
        for h in range(ATT_KV_HEADS):
            sl = slice(h * LANES, (h + 1) * LANES)
            kd_ref[0, :, sl] = (ukd[:, sl] * cs + ukr[:, sl] * sn).astype(BF16)
    else:
        for h in range(ATT_HEADS // 2):
            sl = slice(h * LANES, (h + 1) * LANES)
            q_ref[0, h] = (uq[:, sl] * HEAD_DIM ** -0.5).astype(BF16)
        kd_ref[0] = ukd.astype(BF16)
    vd_ref[0] = seg("vd").astype(BF16)
    if with_nat:
        nat_ref[0] = seg("nat")
    mq_ref[0] = (seg("mq") * ML_DK ** -0.5).astype(BF16)
    mk_ref[0] = seg("mk").astype(BF16)
    mv_ref[0] = seg("mv").astype(BF16)
    gates_ref[0] = seg("gates") + bg_ref[...]
    mo_ref[0] = seg("mo")
    ga_ref[0] = seg("ga")
    gm_ref[0] = seg("gm")


def _project(x, mod, g_pre, w_in, b_gates, mod_row, rope, with_nat, tm, b0, bsz):
    _, seq, d = x.shape
    wp, bg, segs = _pack_w_in(w_in, b_gates, rope, with_nat)
    nw = wp.shape[1]
    tok3 = lambda b, i: (b, i, 0)
    in_specs = [pl.BlockSpec((1, tm, d), lambda b, i: (b + b0, i, 0)),
                pl.BlockSpec((1, 6, d), lambda b, i: (mod_row(b), 0, 0)),
                _const_spec((1, d)),
                pl.BlockSpec((d, nw), lambda b, i: (0, 0), pipeline_mode=pl.Buffered(1)),
                _const_spec((1, LANES))]
    args = [x, mod, g_pre.reshape(1, d), wp, bg]
    if rope:
        cos_t, sin_t = _rope_tables(seq)
        in_specs += [pl.BlockSpec((tm, LANES), lambda b, i: (i, 0))] * 2
        args += [cos_t, sin_t]

    def tok_out(width, dtype):
        return (jax.ShapeDtypeStruct((bsz, seq, width), dtype), pl.BlockSpec((1, tm, width), tok3))

    outs = [(jax.ShapeDtypeStruct((bsz, ATT_HEADS // 2, seq, LANES), BF16),
             pl.BlockSpec((1, ATT_HEADS // 2, tm, LANES), lambda b, i: (b, 0, i, 0))),
            tok_out(2 * LANES, BF16), tok_out(2 * LANES, BF16)]
    if with_nat:
        outs.append(tok_out(2 * KV_W, F32))
    outs += [tok_out(MLQK_W, BF16), tok_out(MLQK_W, BF16), tok_out(MLV_W, BF16), tok_out(LANES, F32),
             tok_out(MLV_W, F32), tok_out(D_MODEL, F32), tok_out(D_MODEL, F32)]
    res = pl.pallas_call(
        functools.partial(_proj_kernel, segs=segs, rope=rope, with_nat=with_nat),
        grid=(bsz, seq // tm),
        in_specs=in_specs,
        out_specs=[o[1] for o in outs],
        out_shape=[o[0] for o in outs],
        compiler_params=_cparams(("parallel", "parallel")),
        name="proj_rope" if rope else "proj_ctx",
    )(*args)
    names = ["q", "kd", "vd"] + (["nat"] if with_nat else []) + ["mq", "mk", "mv", "gates", "mo", "ga", "gm"]
    return dict(zip(names, res))


def _attn_kernel(*refs, window, n_blocks, n_ctx):
    it = iter(refs)
    sink_ref, q_ref = next(it), next(it)
    if window:
        kp_ref, kc_ref, kn_ref, vp_ref, vc_ref, vn_ref = (next(it) for _ in range(6))
    ck_ref, cv_ref, o_ref = next(it), next(it), next(it)
    j = pl.program_id(1)
    n_win = 3 * BLOCK if window else 0
    n_keys = n_win + n_ctx
    rows = ATT_GROUP * BLOCK

    row_id = lax.broadcasted_iota(jnp.int32, (rows, 1), 0)
    if window:
        rl = lax.broadcasted_iota(jnp.int32, (rows, BLOCK), 0) & (BLOCK - 1)
        cl = lax.broadcasted_iota(jnp.int32, (rows, BLOCK), 1)
        prev_ok = cl >= rl + jnp.where(j == 0, BLOCK, 0)
        next_ok = cl <= rl - jnp.where(j == n_blocks - 1, BLOCK, 0)
    lane_lo = lax.broadcasted_iota(jnp.int32, (BLOCK, LANES), 1) < HEAD_DIM
    lane_row = lax.broadcasted_iota(jnp.int32, (1, LANES), 1) < HEAD_DIM
    half_lo = jnp.where(lane_row, 1.0, 0.0).astype(BF16)
    half_hi = jnp.where(lane_row, 0.0, 1.0).astype(BF16)

    for kvh in range(ATT_KV_HEADS):
        sl = slice(kvh * LANES, (kvh + 1) * LANES)
        if window:
            keys = jnp.concatenate([kp_ref[0, :, sl], kc_ref[0, :, sl], kn_ref[0, :, sl], ck_ref[0, :, sl]], axis=0)
            vals = jnp.concatenate([vp_ref[0, :, sl], vc_ref[0, :, sl], vn_ref[0, :, sl], cv_ref[0, :, sl]], axis=0)
        else:
            keys = ck_ref[0, :, sl]
            vals = cv_ref[0, :, sl]
        qs = jnp.concatenate([q_ref[0, kvh * (ATT_GROUP // 2) + g // 2] * (half_lo if g % 2 == 0 else half_hi)
                              for g in range(ATT_GROUP)], axis=0)
        s = lax.dot_general(qs, keys, (((1,), (1,)), ((), ())), preferred_element_type=F32)
        if window:
            s = jnp.concatenate([jnp.where(prev_ok, s[:, :BLOCK], NEG), s[:, BLOCK:2 * BLOCK],
                                 jnp.where(next_ok, s[:, 2 * BLOCK:n_win], NEG), s[:, n_win:]], axis=1)
        snk = jnp.full((rows, 1), sink_ref[kvh * ATT_GROUP + ATT_GROUP - 1], F32)
        for g in range(ATT_GROUP - 2, -1, -1):
            snk = jnp.where(row_id < (g + 1) * BLOCK, sink_ref[kvh * ATT_GROUP + g], snk)
        m = jnp.maximum(jnp.max(s, axis=-1, keepdims=True), snk)
        p = jnp.exp(s - m)
        den = jnp.sum(p, axis=-1, keepdims=True) + jnp.exp(snk - m)
        o = jnp.dot(p.astype(BF16), vals, preferred_element_type=F32) / den
        for pair in range(ATT_GROUP // 2):
            even = o[(2 * pair) * BLOCK:(2 * pair + 1) * BLOCK]
            odd = o[(2 * pair + 1) * BLOCK:(2 * pair + 2) * BLOCK]
            col = (kvh * (ATT_GROUP // 2) + pair) * LANES
            o_ref[0, :, col:col + LANES] = jnp.where(lane_lo, even, odd).astype(BF16)


def _attention(q, kd, vd, ckd, cvd, sink, window):
    bsz, _, seq, _ = q.shape
    nb = seq // BLOCK
    n_ctx = ckd.shape[1]
    in_specs = [pl.BlockSpec(memory_space=pltpu.SMEM),
                pl.BlockSpec((1, ATT_HEADS // 2, BLOCK, LANES), lambda b, j: (b, 0, j, 0))]
    args = [sink.astype(F32), q]
    if window:
        prev = lambda b, j: (b, jnp.maximum(j - 1, 0), 0)
        cur = lambda b, j: (b, j, 0)
        nxt = lambda b, j: (b, jnp.minimum(j + 1, nb - 1), 0)
        for arr in (kd, vd):
            in_specs += [pl.BlockSpec((1, BLOCK, 2 * LANES), f) for f in (prev, cur, nxt)]
            args += [arr, arr, arr]
    in_specs += [pl.BlockSpec((1, n_ctx, 2 * LANES), lambda b, j: (b, 0, 0))] * 2
    args += [ckd, cvd]
    return pl.pallas_call(
        functools.partial(_attn_kernel, window=window, n_blocks=nb, n_ctx=n_ctx),
        grid=(bsz, nb),
        in_specs=in_specs,
        out_specs=pl.BlockSpec((1, BLOCK, Q_W), lambda b, j: (b, j, 0)),
        out_shape=jax.ShapeDtypeStruct((bsz, seq, Q_W), BF16),
        compiler_params=_cparams(("parallel", "parallel")),
        name="attn_window" if window else "attn_ctx",
    )(*args)


def _dup_heads(t):
    b, l = t.shape[:2]
    return jnp.concatenate([t, t], axis=-1).reshape(b, l, ATT_KV_HEADS * LANES).astype(BF16)


def _split3(x):
    hi = x.astype(BF16)
    r1 = x - hi.astype(F32)
    mid = r1.astype(BF16)
    lo = (r1 - mid.astype(F32)).astype(BF16)
    return hi, mid, lo


def _log_sigmoid(x):
    return jnp.minimum(x, 0.0) - jnp.log1p(jnp.exp(-jnp.abs(x)))


def _mlstm_kernel(*refs, n_chunks, has_init, emit_state):
    it = iter(refs)
    qkvg = [[next(it) for _ in range(4)] for _ in range(2)]
    if has_init:
        c0_ref, n0_ref, m0_ref = next(it), next(it), next(it)
    h_refs = [next(it), next(it)]
    if emit_state:
        co_ref, no_ref, mo_ref = next(it), next(it), next(it)
    ct_s, n_s, m_s = next(it), next(it), next(it)
    t = pl.program_id(1)
    L = ML_CHUNK

    @pl.when(t == 0)
    def _():
        for d in range(2):
            for h in range(ML_HEADS):
                if has_init:
                    ct_s[d, h] = c0_ref[0, d, h].T
                    n_s[d, h] = n0_ref[0, d, h:h + 1, :]
                    m_s[d, h] = m0_ref[0, d, h:h + 1, :]
                else:
                    ct_s[d, h] = jnp.zeros((ML_DK, ML_DV), F32)
                    n_s[d, h] = jnp.zeros((1, ML_DK), F32)
                    m_s[d, h] = jnp.zeros((1, LANES), F32)

    ri = lax.broadcasted_iota(jnp.int32, (L, L), 0)
    ci = lax.broadcasted_iota(jnp.int32, (L, L), 1)

    for d in range(2):
        q_ref, k_ref, v_ref, g_ref = qkvg[d]
        tri = (ci <= ri) if d == 0 else (ci >= ri)
        tri_b = jnp.where(tri, 1.0, 0.0).astype(BF16)
        g = g_ref[0]
        lf = _log_sigmoid(g)
        g_t = g.T
        lf_t = lf.T
        bc = sum(jnp.dot(tri_b, part, preferred_element_type=F32) for part in _split3(lf))
        br = sum(lax.dot_general(part, tri_b, (((1,), (1,)), ((), ())), preferred_element_type=F32)
                 for part in _split3(lf_t))
        tot = jnp.sum(lf, axis=0, keepdims=True)

        for h in range(ML_HEADS):
            icol_i = 2 * ML_HEADS * d + h
            fcol = icol_i + ML_HEADS
            bcol = bc[:, fcol:fcol + 1]
            brow = br[fcol:fcol + 1, :]
            irow = g_t[icol_i:icol_i + 1, :]
            icol = g[:, icol_i:icol_i + 1]
            b_last = tot[:, fcol:fcol + 1]
            m_prev = m_s[d, h][:, 0:1]
            n_prev = n_s[d, h]
            ct_prev = ct_s[d, h]
            qh = q_ref[0, :, h * ML_DK:(h + 1) * ML_DK]
            kh = k_ref[0, :, h * ML_DK:(h + 1) * ML_DK]
            vh = v_ref[0, :, h * ML_DV:(h + 1) * ML_DV]

            rel = jnp.where(tri, irow - brow, NEG)
            inter = bcol + m_prev
            m_t = jnp.maximum(inter, bcol + jnp.max(rel, axis=-1, keepdims=True))
            w = jnp.exp(rel + (bcol - m_t))
            w_inter = jnp.exp(inter - m_t)
            s = lax.dot_general(qh, kh, (((1,), (1,)), ((), ())), preferred_element_type=F32) * w
            cq = jnp.dot(qh, ct_prev.astype(BF16), preferred_element_type=F32) * w_inter
            num = jnp.dot(s.astype(BF16), vh, preferred_element_type=F32) + cq
            qn = jnp.sum(qh.astype(F32) * n_prev, axis=-1, keepdims=True)
            den = jnp.sum(s, axis=-1, keepdims=True) + w_inter * qn
            h_refs[d][0, :, h * ML_DV:(h + 1) * ML_DV] = num / jnp.maximum(jnp.abs(den), jnp.exp(-m_t))

            a = b_last - bcol + icol
            m_new = jnp.maximum(b_last + m_prev, jnp.max(a, axis=0, keepdims=True))
            wk = jnp.exp(a - m_new)
            decay = jnp.exp(b_last + m_prev - m_new)
            kw = kh.astype(F32) * wk
            ct_new = decay * ct_prev + jnp.dot(kw.T.astype(BF16), vh, preferred_element_type=F32)
            n_new = decay * n_prev + jnp.sum(kw, axis=0, keepdims=True)
            ct_s[d, h] = ct_new
            n_s[d, h] = n_new
            m_s[d, h] = jnp.broadcast_to(m_new, (1, LANES))
            if emit_state:
                @pl.when(t == n_chunks - 1)
                def _():
                    co_ref[0, d, h] = ct_new.T
                    no_ref[0, d, h:h + 1, :] = n_new
                    mo_ref[0, d, h:h + 1, :] = jnp.broadcast_to(m_new, (1, LANES))


def _mlstm(mq, mk, mv, gates, init, emit_state):
    bsz, seq, _ = mq.shape
    nc = seq // ML_CHUNK
    fwd = lambda b, t: (b, t, 0)
    bwd = lambda b, t: (b, nc - 1 - t, 0)
    state5 = lambda b, t: (b, 0, 0, 0, 0)
    state4 = lambda b, t: (b, 0, 0, 0)
    in_specs, args = [], []
    for chunk in (fwd, bwd):
        in_specs += [pl.BlockSpec((1, ML_CHUNK, MLQK_W), chunk), pl.BlockSpec((1, ML_CHUNK, MLQK_W), chunk),
                     pl.BlockSpec((1, ML_CHUNK, MLV_W), chunk), pl.BlockSpec((1, ML_CHUNK, LANES), chunk)]
        args += [mq, mk, mv, gates]
    if init is not None:
        c0, n0, m0 = init
        in_specs += [pl.BlockSpec((1, 2, ML_HEADS, ML_DV, ML_DK), state5),
                     pl.BlockSpec((1, 2, ML_HEADS, ML_DK), state4),
                     pl.BlockSpec((1, 2, ML_HEADS, LANES), state4)]
        args += [c0.astype(F32), n0.astype(F32),
                 jnp.broadcast_to(m0.astype(F32)[..., None], m0.shape + (LANES,))]
    out_shape = [jax.ShapeDtypeStruct((bsz, seq, MLV_W), F32)] * 2
    out_specs = [pl.BlockSpec((1, ML_CHUNK, MLV_W), fwd), pl.BlockSpec((1, ML_CHUNK, MLV_W), bwd)]
    if emit_state:
        out_shape += [jax.ShapeDtypeStruct((bsz, 2, ML_HEADS, ML_DV, ML_DK), F32),
                      jax.ShapeDtypeStruct((bsz, 2, ML_HEADS, ML_DK), F32),
                      jax.ShapeDtypeStruct((bsz, 2, ML_HEADS, LANES), F32)]
        out_specs += [pl.BlockSpec((1, 2, ML_HEADS, ML_DV, ML_DK), state5),
                      pl.BlockSpec((1, 2, ML_HEADS, ML_DK), state4),
                      pl.BlockSpec((1, 2, ML_HEADS, LANES), state4)]
    return pl.pallas_call(
        functools.partial(_mlstm_kernel, n_chunks=nc, has_init=init is not None, emit_state=emit_state),
        grid=(bsz, nc),
        in_specs=in_specs,
        out_specs=out_specs,
        out_shape=out_shape,
        scratch_shapes=[pltpu.VMEM((2, ML_HEADS, ML_DK, ML_DV), F32),
                        pltpu.VMEM((2, ML_HEADS, 1, ML_DK), F32),
                        pltpu.VMEM((2, ML_HEADS, 1, LANES), F32)],
        compiler_params=_cparams(("parallel", "arbitrary")),
        name="mlstm_state" if emit_state else "mlstm",
    )(*args)


def _rms(x, g):
    return x * lax.rsqrt(jnp.mean(x * x, axis=-1, keepdims=True) + EPS) * g


def _pack_bf16_pairs(x):
    n = x.shape[1] // 2
    lo = pltpu.bitcast(x[:, :n].astype(BF16).astype(F32), jnp.int32)
    hi = pltpu.bitcast(x[:, n:].astype(BF16).astype(F32), jnp.int32)
    return lax.shift_right_logical(lo, 16) | hi


def _unpack_bf16_pairs(p):
    lo = pltpu.bitcast(lax.shift_left(p, 16), F32)
    hi = pltpu.bitcast(p & jnp.int32(-65536), F32)
    return lo, hi


def _merge_kernel(att_ref, hf_ref, hb_ref, mo_ref, ga_ref, gm_ref, x_ref, mod_ref, nml_ref, gpm_ref, gpf_ref,
                  wua_ref, wum_ref, wo_ref, wr_ref, br_ref, x1_ref, h2_ref, idx_ref, gate_ref, cnt_ref):
    hsum = hf_ref[0] + hb_ref[0]
    parts = []
    for h in range(ML_HEADS):
        hh = hsum[:, h * ML_DV:(h + 1) * ML_DV]
        parts.append(hh * lax.rsqrt(jnp.mean(hh * hh, axis=-1, keepdims=True) + EPS))
    hn = jnp.concatenate(parts, axis=-1) * nml_ref[...]
    ml = (hn * jax.nn.sigmoid(mo_ref[0])).astype(BF16)
    a = jnp.dot(att_ref[0], wua_ref[...], preferred_element_type=F32)
    m = jnp.dot(ml, wum_ref[...], preferred_element_type=F32)
    z = (jax.nn.sigmoid(ga_ref[0]) * a + jax.nn.sigmoid(gm_ref[0]) * m).astype(BF16)
    mix = jnp.dot(z, wo_ref[...], preferred_element_type=F32)
    gate1 = mod_ref[0, 2:3, :]
    shift2 = mod_ref[0, 3:4, :]
    scale2 = mod_ref[0, 4:5, :]
    x1 = x_ref[0] + gate1 * _rms(mix, gpm_ref[...])
    x1_ref[0] = x1
    h2 = _rms(x1, gpf_ref[...]) * (1.0 + scale2) + shift2
    h2_ref[0] = _pack_bf16_pairs(h2)

    h2_hi = h2.astype(BF16)
    h2_lo = (h2 - h2_hi.astype(F32)).astype(BF16)
    wr = wr_ref[...]
    wr_hi = wr.astype(BF16)
    wr_lo = (wr - wr_hi.astype(F32)).astype(BF16)
    logits = (jnp.dot(h2_hi, wr_hi, preferred_element_type=F32)
              + jnp.dot(h2_lo, wr_hi, preferred_element_type=F32)
              + jnp.dot(h2_hi, wr_lo, preferred_element_type=F32)) + br_ref[...]
    lane = lax.broadcasted_iota(jnp.int32, logits.shape, 1).astype(F32)
    work = jnp.where(lane < N_EXPERTS, logits, -jnp.inf)
    idx_out = jnp.zeros(logits.shape, F32)
    val_out = jnp.zeros(logits.shape, F32)
    picked = jnp.zeros(logits.shape, F32)
    top0 = None
    esum = None
    for k in range(TOP_K):
        mx = jnp.max(work, axis=-1, keepdims=True)
        sel = jnp.min(jnp.where(work == mx, lane, float(LANES)), axis=-1, keepdims=True)
        if k == 0:
            top0 = mx
        e = jnp.exp(mx - top0)
        esum = e if k == 0 else esum + e
        idx_out = jnp.where(lane == k, sel, idx_out)
        val_out = jnp.where(lane == k, e, val_out)
        picked = jnp.where(lane == sel, 1.0, picked)
        work = jnp.where(lane == sel, -jnp.inf, work)
    idx_ref[0] = idx_out.astype(jnp.int32)
    gate_ref[0] = val_out / esum
    for r in range(logits.shape[0] // ROUTE_TILE):
        cnt_ref[0, r:r + 1, :] = jnp.sum(picked[r * ROUTE_TILE:(r + 1) * ROUTE_TILE], axis=0, keepdims=True)


def _merge(att, hf, hb, pj, x, mod, mod_row, p, tm, b0):
    bsz = att.shape[0]
    _, seq, d = x.shape
    tok3 = lambda b, i: (b, i, 0)
    wr = jnp.pad(p["w_router"].astype(F32), ((0, 0), (0, LANES - N_EXPERTS)))
    br = jnp.pad(p["b_router"].astype(F32), (0, LANES - N_EXPERTS)).reshape(1, LANES)
    row = lambda v: v.astype(F32).reshape(1, -1)
    in_specs = [pl.BlockSpec((1, tm, Q_W), tok3),
                pl.BlockSpec((1, tm, MLV_W), tok3), pl.BlockSpec((1, tm, MLV_W), tok3),
                pl.BlockSpec((1, tm, MLV_W), tok3), pl.BlockSpec((1, tm, d), tok3), pl.BlockSpec((1, tm, d), tok3),
                pl.BlockSpec((1, tm, d), lambda b, i: (b + b0, i, 0)),
                pl.BlockSpec((1, 6, d), lambda b, i: (mod_row(b), 0, 0)),
                _const_spec((1, MLV_W)), _const_spec((1, d)), _const_spec((1, d)),
                _const_spec((Q_W, d)), _const_spec((MLV_W, d)), _const_spec((d, d)),
                _const_spec((d, LANES)), _const_spec((1, LANES))]
    out_shape = [jax.ShapeDtypeStruct((bsz, seq, d), F32), jax.ShapeDtypeStruct((bsz, seq, d // 2), jnp.int32),
                 jax.ShapeDtypeStruct((bsz, seq, LANES), jnp.int32), jax.ShapeDtypeStruct((bsz, seq, LANES), F32),
                 jax.ShapeDtypeStruct((bsz * (seq // tm), tm // ROUTE_TILE, LANES), F32)]
    out_specs = [pl.BlockSpec((1, tm, d), tok3), pl.BlockSpec((1, tm, d // 2), tok3),
                 pl.BlockSpec((1, tm, LANES), tok3), pl.BlockSpec((1, tm, LANES), tok3),
                 pl.BlockSpec((1, tm // ROUTE_TILE, LANES), lambda b, i: (b * (seq // tm) + i, 0, 0))]
    return pl.pallas_call(
        _merge_kernel,
        grid=(bsz, seq // tm),
        in_specs=in_specs,
        out_specs=out_specs,
        out_shape=out_shape,
        compiler_params=_cparams(("parallel", "parallel")),
        name="merge_router",
    )(att, hf, hb, pj["mo"], pj["ga"], pj["gm"], x, mod, row(p["norm_ml"]), row(p["g_post_mix"]),
      row(p["g_pre_ffn"]), p["w_up_att"].astype(BF16), p["w_up_ml"].astype(BF16), p["w_out"].astype(BF16), wr, br)


def _sc_gather_rows(table, idx):
    n = idx.shape[0]
    width = table.shape[1]
    chunk = SC_GATHER_CHUNK
    n_workers = SC_CORES * SC_SUBCORES
    rows_per_worker = n // n_workers
    n_pairs = rows_per_worker // (2 * chunk)
    assert n_pairs * 2 * chunk * n_workers == n
    mesh = plsc.VectorSubcoreMesh(core_axis_name="c", subcore_axis_name="s")

    def body(table_hbm, idx_hbm, out_hbm, idx_v, rows_v, gsem, osem):
        wid = lax.axis_index("s") * SC_CORES + lax.axis_index("c")
        base0 = wid * rows_per_worker

        def gather_copy(b):
            return pltpu.make_async_copy(table_hbm.at[idx_v.at[b]], rows_v.at[b], gsem.at[b])

        def write_copy(ci, b):
            base = pl.multiple_of(base0 + ci * chunk, 8)
            return pltpu.make_async_copy(rows_v.at[b], out_hbm.at[pl.ds(base, chunk)], osem.at[b])

        def issue(ci, b):
            base = pl.multiple_of(base0 + ci * chunk, 8)
            pltpu.sync_copy(idx_hbm.at[pl.ds(base, chunk)], idx_v.at[b])
            gather_copy(b).start()

        def finish(ci, b):
            gather_copy(b).wait()
            write_copy(ci, b).start()

        issue(0, 0)

        @pl.loop(0, n_pairs)
        def _(j):
            @pl.when(j > 0)
            def _():
                write_copy(2 * j - 1, 1).wait()
            issue(2 * j + 1, 1)
            finish(2 * j, 0)

            @pl.when(j < n_pairs - 1)
            def _():
                write_copy(2 * j, 0).wait()
                issue(2 * j + 2, 0)
            finish(2 * j + 1, 1)

        write_copy(2 * n_pairs - 2, 0).wait()
        write_copy(2 * n_pairs - 1, 1).wait()

    return pl.kernel(
        body, mesh=mesh,
        out_type=jax.ShapeDtypeStruct((n, width), table.dtype),
        scratch_types=[pltpu.VMEM((2, chunk), jnp.int32), pltpu.VMEM((2, chunk, width), table.dtype),
                       pltpu.SemaphoreType.DMA((2,)), pltpu.SemaphoreType.DMA((2,))],
    )(table, idx)


def _ffn_kernel(be_ref, nu_ref, xs_ref, w1_ref, b1_ref, w2_ref, b2_ref, ys_ref, w1b_s, w2b_s):
    i = pl.program_id(0)
    e = be_ref[i]
    e_prev = be_ref[jnp.maximum(i - 1, 0)]

    @pl.when((i == 0) | (e != e_prev))
    def _():
        w1b_s[...] = w1_ref[0].astype(BF16)
        w2b_s[...] = w2_ref[0].astype(BF16)

    @pl.when(i < nu_ref[0])
    def _():
        half = D_MODEL // 2
        x_lo, x_hi = _unpack_bf16_pairs(xs_ref[...])
        hmid = (jnp.dot(x_lo.astype(BF16), w1b_s[:half, :], preferred_element_type=F32)
                + jnp.dot(x_hi.astype(BF16), w1b_s[half:, :], preferred_element_type=F32)) + b1_ref[0]
        glu = jnp.minimum(hmid[:, :D_FF], SWIGLU_LIMIT)
        lin = jnp.clip(hmid[:, D_FF:], -SWIGLU_LIMIT, SWIGLU_LIMIT)
        act = ((lin + 1.0) * glu * jax.nn.sigmoid(SWIGLU_ALPHA * glu)).astype(BF16)
        y = jnp.dot(act, w2b_s[...], preferred_element_type=F32) + b2_ref[0]
        ys_ref[...] = _pack_bf16_pairs(y)

    @pl.when(i >= nu_ref[0])
    def _():
        ys_ref[...] = jnp.zeros(ys_ref.shape, jnp.int32)


def _sc_scatter_rows(tables, positions, n_rows):
    width = tables[0].shape[1]
    chunk = SC_GATHER_CHUNK
    n_workers = SC_CORES * SC_SUBCORES
    n_streams = len(tables)
    mesh = plsc.VectorSubcoreMesh(core_axis_name="c", subcore_axis_name="s")

    def body(*refs):
        table_refs = refs[0:2 * n_streams:2]
        pos_refs = refs[1:2 * n_streams:2]
        out_hbm, idx_v, rows_v, lsem, ssem = refs[2 * n_streams:]
        wid = lax.axis_index("s") * SC_CORES + lax.axis_index("c")

        for table_hbm, pos_hbm in zip(table_refs, pos_refs):
            tok_per_worker = table_hbm.shape[0] // n_workers
            n_pairs = tok_per_worker // (2 * chunk)
            assert n_pairs * 2 * chunk * n_workers == table_hbm.shape[0]
            base0 = wid * tok_per_worker

            def load_copy(ci, b):
                base = pl.multiple_of(base0 + ci * chunk, 8)
                return pltpu.make_async_copy(table_hbm.at[pl.ds(base, chunk)], rows_v.at[b], lsem.at[b])

            def scatter_copy(b, k):
                return pltpu.make_async_copy(rows_v.at[b], out_hbm.at[idx_v.at[b, k]], ssem.at[b])

            def load(ci, b):
                base = pl.multiple_of(base0 + ci * chunk, 8)
                load_copy(ci, b).start()
                for k in range(TOP_K):
                    pltpu.sync_copy(pos_hbm.at[k, pl.ds(base, chunk)], idx_v.at[b, k])

            def scatter(ci, b):
                load_copy(ci, b).wait()
                for k in range(TOP_K):
                    scatter_copy(b, k).start()

            def drain(b):
                for k in range(TOP_K):
                    scatter_copy(b, k).wait()

            load(0, 0)

            @pl.loop(0, n_pairs)
            def _(j):
                @pl.when(j > 0)
                def _():
                    drain(1)
                load(2 * j + 1, 1)
                scatter(2 * j, 0)

                @pl.when(j < n_pairs - 1)
                def _():
                    drain(0)
                    load(2 * j + 2, 0)
                scatter(2 * j + 1, 1)

            drain(0)
            drain(1)

    args = [a for pair in zip(tables, positions) for a in pair]
    return pl.kernel(
        body, mesh=mesh,
        out_type=jax.ShapeDtypeStruct((n_rows, width), tables[0].dtype),
        scratch_types=[pltpu.VMEM((2, TOP_K, chunk), jnp.int32), pltpu.VMEM((2, chunk, width), tables[0].dtype),
                       pltpu.SemaphoreType.DMA((2,)), pltpu.SemaphoreType.DMA((2,))],
    )(*args)


def _pos_kernel(idx_ref, base_ref, pos_ref):
    idx = idx_ref[...]
    rows = idx.shape[0]
    lane = lax.broadcasted_iota(jnp.int32, idx.shape, 1)
    hots = [jnp.where(lane == idx[:, k:k + 1], 1.0, 0.0) for k in range(TOP_K)]
    cnt = hots[0] + hots[1] + hots[2] + hots[3]
    ri = lax.broadcasted_iota(jnp.int32, (rows, rows), 0)
    ci = lax.broadcasted_iota(jnp.int32, (rows, rows), 1)
    earlier = jnp.where(ci < ri, 1.0, 0.0).astype(BF16)
    ahead = jnp.dot(earlier, cnt.astype(BF16), preferred_element_type=F32) + base_ref[0]
    posmat = jnp.zeros(idx.shape, F32)
    for k in range(TOP_K):
        posmat = jnp.where(lane == k, jnp.sum(hots[k] * ahead, axis=-1, keepdims=True), posmat)
    pos_ref[...] = posmat.T[:TOP_K, :].astype(jnp.int32)


def _route(idx_list, count_list):
    tiles = [c.shape[0] * c.shape[1] for c in count_list]
    n_tok = sum(tiles) * ROUTE_TILE
    n_blocks = n_tok * TOP_K // MOE_ROWS + N_EXPERTS
    cnt = jnp.concatenate([c.reshape(-1, LANES) for c in count_list], axis=0).astype(jnp.int32)
    tile_off = jnp.cumsum(cnt, axis=0) - cnt
    total = jnp.sum(cnt, axis=0)
    padded = (total + MOE_ROWS - 1) // MOE_ROWS * MOE_ROWS
    pend = jnp.cumsum(padded)
    base = ((pend - padded)[None, :] + tile_off).astype(F32)
    starts = jnp.arange(n_blocks, dtype=jnp.int32) * MOE_ROWS
    block_e = jnp.minimum(jnp.sum(pend[None, :N_EXPERTS] <= starts[:, None], axis=1), N_EXPERTS - 1).astype(jnp.int32)
    n_used = (pend[N_EXPERTS - 1] // MOE_ROWS).astype(jnp.int32).reshape(1)
    positions, first = [], 0
    for idx, n_tiles in zip(idx_list, tiles):
        stream_tok = n_tiles * ROUTE_TILE
        positions.append(pl.pallas_call(
            _pos_kernel,
            grid=(n_tiles,),
            in_specs=[pl.BlockSpec((ROUTE_TILE, LANES), lambda i: (i, 0)),
                      pl.BlockSpec((1, 1, LANES), lambda i: (i, 0, 0))],
            out_specs=pl.BlockSpec((TOP_K, ROUTE_TILE), lambda i: (0, i)),
            out_shape=jax.ShapeDtypeStruct((TOP_K, stream_tok), jnp.int32),
            compiler_params=_cparams(("parallel",)),
            name="route_pos",
        )(idx.reshape(stream_tok, LANES), base[first:first + n_tiles].reshape(n_tiles, 1, LANES)))
        first += n_tiles
    return block_e, n_used, positions


def _moe(h2p_list, idx_list, count_list, w1, b1, w2, b2):
    half = h2p_list[0].shape[1]
    d = 2 * half
    block_e, n_used, positions = _route(idx_list, count_list)
    n_blocks = block_e.shape[0]
    xs = _sc_scatter_rows(h2p_list, positions, n_blocks * MOE_ROWS)
    grid_spec = pltpu.PrefetchScalarGridSpec(
        num_scalar_prefetch=2,
        grid=(n_blocks,),
        in_specs=[pl.BlockSpec((MOE_ROWS, half), lambda i, be, nu: (i, 0)),
                  pl.BlockSpec((1, d, 2 * D_FF), lambda i, be, nu: (be[i], 0, 0)),
                  pl.BlockSpec((1, 1, 2 * D_FF), lambda i, be, nu: (be[i], 0, 0)),
                  pl.BlockSpec((1, D_FF, d), lambda i, be, nu: (be[i], 0, 0)),
                  pl.BlockSpec((1, 1, d), lambda i, be, nu: (be[i], 0, 0))],
        out_specs=pl.BlockSpec((MOE_ROWS, half), lambda i, be, nu: (i, 0)),
        scratch_shapes=[pltpu.VMEM((d, 2 * D_FF), BF16), pltpu.VMEM((D_FF, d), BF16)],
    )
    ys = pl.pallas_call(
        _ffn_kernel,
        grid_spec=grid_spec,
        out_shape=jax.ShapeDtypeStruct((n_blocks * MOE_ROWS, half), jnp.int32),
        compiler_params=_cparams(("arbitrary",)),
        name="moe_ffn",
    )(block_e, n_used, xs, w1, b1.reshape(N_EXPERTS, 1, -1), w2, b2.reshape(N_EXPERTS, 1, -1))
    return [_sc_gather_rows(ys, pos.reshape(-1)) for pos in positions]


def _final_kernel(x1_ref, y0_ref, y1_ref, y2_ref, y3_ref, gate_ref, mod_ref, g_ref, *rest):
    o_ref = rest[-1]
    gates = gate_ref[0]
    lo = hi = None
    for k, y_ref in enumerate((y0_ref, y1_ref, y2_ref, y3_ref)):
        y_lo, y_hi = _unpack_bf16_pairs(y_ref[0, 0])
        gk = gates[:, k:k + 1]
        lo = gk * y_lo if k == 0 else lo + gk * y_lo
        hi = gk * y_hi if k == 0 else hi + gk * y_hi
    y = jnp.concatenate([lo, hi], axis=-1)
    gate2 = mod_ref[0, 5:6, :]
    o_ref[0] = x1_ref[0] + gate2 * _rms(y, g_ref[...])


def _final(x1, yg, gate, mod, mod_row, g_post_ffn, tm, b0, n_batch, out_prev):
    bsz, seq, d = x1.shape
    tok3 = lambda b, i: (b, i, 0)
    yg = yg.reshape(TOP_K, bsz, seq, d // 2)
    slot_specs = [pl.BlockSpec((1, 1, tm, d // 2), functools.partial(lambda b, i, k: (k, b, i, 0), k=k))
                  for k in range(TOP_K)]
    in_specs = ([pl.BlockSpec((1, tm, d), tok3)] + slot_specs +
                [pl.BlockSpec((1, tm, LANES), tok3),
                 pl.BlockSpec((1, 6, d), lambda b, i: (mod_row(b), 0, 0)), _const_spec((1, d))])
    args = [x1, yg, yg, yg, yg, gate, mod, g_post_ffn.astype(F32).reshape(1, d)]
    aliases = {}
    if out_prev is not None:
        in_specs.append(pl.BlockSpec(memory_space=pl.ANY))
        args.append(out_prev)
        aliases = {len(args) - 1: 0}
    return pl.pallas_call(
        _final_kernel,
        grid=(bsz, seq // tm),
        in_specs=in_specs,
        out_specs=pl.BlockSpec((1, tm, d), lambda b, i: (b + b0, i, 0)),
        out_shape=jax.ShapeDtypeStruct((n_batch, seq, d), F32),
        input_output_aliases=aliases,
        compiler_params=_cparams(("parallel", "parallel")),
        name="final_residual",
    )(*args)


def _stream(x, mod, mod_row, p, ctx_kv, init_state, rope, tm, b0, bsz):
    _, seq, d = x.shape
    is_ctx = ctx_kv is None
    pj = _project(x, mod, p["g_pre_mix"], p["w_in"], p["b_gates"], mod_row, rope, is_ctx, tm, b0, bsz)
    if is_ctx:
        att = _attention(pj["q"], None, None, pj["kd"], pj["vd"], p["sink"], window=False)
    else:
        att = _attention(pj["q"], pj["kd"], pj["vd"], ctx_kv[0], ctx_kv[1], p["sink"], window=True)
    ml = _mlstm(pj["mq"], pj["mk"], pj["mv"], pj["gates"], init_state, emit_state=is_ctx)
    x1, h2, idx, gate, tile_counts = _merge(att, ml[0], ml[1], pj, x, mod, mod_row, p, tm, b0)
    moe_in = (h2.reshape(bsz * seq, d // 2), idx, tile_counts)
    return x1, gate, moe_in, pj, ml


def kernel(x_prompt, x_sample, c, cache_k, cache_v, state_C, state_n, state_m, c_ctx, w_ada, b_ada, g_pre_mix,
           w_in, b_gates, attn_sink, norm_mlstm, w_up_att, w_up_ml, w_out, g_post_mix, g_pre_ffn, w_router,
           b_router, w1, b1, w2, b2, g_post_ffn):
    depth = w_ada.shape[0]
    n_dec = c.shape[0]
    cond = jnp.concatenate([c_ctx[None, :], c], axis=0).astype(F32)
    cond = jnp.pad(cond, ((0, 16 - cond.shape[0]), (0, 0)))
    y_prompt, y_sample = x_prompt, x_sample
    ks_, vs_, cs_, ns_, ms_ = [], [], [], [], []
    for l in range(depth):
        p = dict(g_pre_mix=g_pre_mix[l], w_in=w_in[l], b_gates=b_gates[l], sink=attn_sink[l], norm_ml=norm_mlstm[l],
                 w_up_att=w_up_att[l], w_up_ml=w_up_ml[l], w_out=w_out[l], g_post_mix=g_post_mix[l],
                 g_pre_ffn=g_pre_ffn[l], w_router=w_router[l], b_router=b_router[l], w1=w1[l], b1=b1[l],
                 w2=w2[l], b2=b2[l], g_post_ffn=g_post_ffn[l])
        mod = _adaln(cond, w_ada[l], b_ada[l]).reshape(16, 6, D_MODEL)
        row_p = lambda b: 0
        bsz, seq = x_prompt.shape[:2]
        x1_p, gate_p, moe_p, pj, ml = _stream(y_prompt, mod, row_p, p, None, None, False, 256, 0, bsz)
        nat = pj["nat"]
        ks_.append(nat[..., :KV_W].reshape(bsz, seq, ATT_KV_HEADS, HEAD_DIM))
        vs_.append(nat[..., KV_W:].reshape(bsz, seq, ATT_KV_HEADS, HEAD_DIM))
        cs_.append(ml[2])
        ns_.append(ml[3])
        ms_.append(ml[4][..., 0])
        ctx_kv = (_dup_heads(cache_k[:, l]), _dup_heads(cache_v[:, l]))
        init = (state_C[:, l], state_n[:, l], state_m[:, l])
        half = n_dec // 2
        experts = (p["w1"], p["b1"].astype(F32), p["w2"], p["b2"].astype(F32))
        groups = []
        for b0 in (0, half):
            row_s = functools.partial(lambda b, off: b + off + 1, off=b0)
            sl = slice(b0, b0 + half)
            x1_s, gate_s, moe_s, _, _ = _stream(y_sample, mod, row_s, p, (ctx_kv[0][sl], ctx_kv[1][sl]),
                                                tuple(t[sl] for t in init), True, 512, b0, half)
            groups.append((x1_s, gate_s, moe_s, row_s))
        (x1_a, gate_a, moe_a, row_a), (x1_b, gate_b, moe_b, row_b) = groups
        yg_p, yg_a = _moe([moe_p[0], moe_a[0]], [moe_p[1], moe_a[1]], [moe_p[2], moe_a[2]], *experts)
        (yg_b,) = _moe([moe_b[0]], [moe_b[1]], [moe_b[2]], *experts)
        y_prompt = _final(x1_p, yg_p, gate_p, mod, row_p, p["g_post_ffn"], 256, 0, bsz, None)
        y_sample = _final(x1_a, yg_a, gate_a, mod, row_a, p["g_post_ffn"], 512, 0, n_dec, None)
        y_sample = _final(x1_b, yg_b, gate_b, mod, row_b, p["g_post_ffn"], 512, half, n_dec, y_sample)
    return (y_prompt, y_sample, jnp.stack(ks_, axis=1), jnp.stack(vs_, axis=1), jnp.stack(cs_, axis=1),
            jnp.stack(ns_, axis=1), jnp.stack(ms_, axis=1))
```

```python
import functools

import numpy as np
import jax
import jax.numpy as jnp
from jax import lax
from jax.experimental import pallas as pl
from jax.experimental.pallas import tpu as pltpu
from jax.experimental.pallas import tpu_sc as plsc

F32 = jnp.float32
BF16 = jnp.bfloat16

D_MODEL = 1024
GRID_W = 64
ATT_HEADS = 8
ATT_KV_HEADS = 2
ATT_GROUP = ATT_HEADS // ATT_KV_HEADS
HEAD_DIM = 64
BLOCK = 128
ROPE_THETA = 10000.0
AXIS_FREQS = HEAD_DIM // 4
ML_HEADS = 4
ML_DK = 128
ML_DV = 256
ML_CHUNK = 128
N_EXPERTS = 32
TOP_K = 4
D_FF = 1024
SWIGLU_ALPHA = 1.702
SWIGLU_LIMIT = 7.0
EPS = 1e-6
NEG = -1e30

Q_W = ATT_HEADS * HEAD_DIM
KV_W = ATT_KV_HEADS * HEAD_DIM
MLQK_W = ML_HEADS * ML_DK
MLV_W = ML_HEADS * ML_DV
GATE_W = 4 * ML_HEADS

LANES = 128
VMEM_LIMIT = 56 * 1024 * 1024
MOE_ROWS = 512
SC_CORES = 2
SC_SUBCORES = 16
SC_GATHER_CHUNK = 32
ROUTE_TILE = 256


def _cparams(sem):
    return pltpu.CompilerParams(dimension_semantics=sem, vmem_limit_bytes=VMEM_LIMIT)


def _const_spec(shape):
    nd = len(shape)
    return pl.BlockSpec(shape, lambda *_: (0,) * nd)


def _adaln_kernel(cond_ref, w_ref, b_ref, o_ref):
    cnd = cond_ref[...]
    act = cnd * jax.nn.sigmoid(cnd)
    o_ref[...] = jnp.dot(act.astype(BF16), w_ref[...].astype(BF16),
                         preferred_element_type=F32) + b_ref[...]


def _adaln(cond, w_ada, b_ada):
    rows, d = cond.shape
    n = w_ada.shape[1]
    tn = 1536
    return pl.pallas_call(
        _adaln_kernel,
        grid=(n // tn,),
        in_specs=[_const_spec((rows, d)),
                  pl.BlockSpec((d, tn), lambda j: (0, j)),
                  pl.BlockSpec((1, tn), lambda j: (0, j))],
        out_specs=pl.BlockSpec((rows, tn), lambda j: (0, j)),
        out_shape=jax.ShapeDtypeStruct((rows, n), F32),
        compiler_params=_cparams(("parallel",)),
        name="adaln",
    )(cond, w_ada, b_ada.reshape(1, n))


def _pack_w_in(w_in, b_gates, rope, with_nat):
    o_q, o_k, o_v = 0, Q_W, Q_W + KV_W
    o_mq = Q_W + 2 * KV_W
    o_mk = o_mq + MLQK_W
    o_mv = o_mk + MLQK_W
    o_g = o_mv + MLV_W
    o_mo = o_g + GATE_W
    o_ga = o_mo + MLV_W
    o_gm = o_ga + D_MODEL
    half = HEAD_DIM // 2

    def head_cols(base, h, rot):
        lo = base + h * HEAD_DIM
        if rot:
            return [w_in[:, lo + half:lo + HEAD_DIM], w_in[:, lo:lo + half]]
        return [w_in[:, lo:lo + HEAD_DIM]]

    def q_cols(rot):
        cols = []
        for h in range(ATT_HEADS):
            cols += head_cols(o_q, h, rot)
        return cols

    def dup_cols(base, rot):
        cols = []
        for h in range(ATT_KV_HEADS):
            hc = head_cols(base, h, rot)
            cols += hc + hc
        return cols

    segs, cols, off = {}, [], 0

    def add(name, cl):
        nonlocal off
        width = sum(c.shape[1] for c in cl)
        segs[name] = (off, off + width)
        cols.extend(cl)
        off += width

    add("q", q_cols(False))
    if rope:
        add("qrot", q_cols(True))
    add("kd", dup_cols(o_k, False))
    if rope:
        add("kdrot", dup_cols(o_k, True))
    if with_nat:
        add("nat", [w_in[:, o_k:o_k + 2 * KV_W]])
    add("mq", [w_in[:, o_mq:o_mq + MLQK_W]])
    add("mk", [w_in[:, o_mk:o_mk + MLQK_W]])
    add("mv", [w_in[:, o_mv:o_mv + MLV_W]])
    add("mo", [w_in[:, o_mo:o_mo + MLV_W]])
    add("ga", [w_in[:, o_ga:o_ga + D_MODEL]])
    add("gm", [w_in[:, o_gm:o_gm + D_MODEL]])
    add("gates", [w_in[:, o_g:o_g + GATE_W], jnp.zeros((D_MODEL, LANES - GATE_W), w_in.dtype)])
    wp = jnp.concatenate(cols, axis=1).astype(BF16)
    wvd_t = jnp.concatenate(dup_cols(o_v, False), axis=1).T.astype(BF16)
    bg = jnp.pad(b_gates.astype(F32), (0, LANES - GATE_W)).reshape(1, LANES)
    return wp, wvd_t, bg, segs


def _rope_tables(n_tok):
    rows = n_tok // GRID_W
    row = np.repeat(np.arange(rows), GRID_W).astype(np.float32)
    col = np.tile(np.arange(GRID_W), rows).astype(np.float32)
    inv = (np.float32(ROPE_THETA) ** (-np.arange(AXIS_FREQS, dtype=np.float32) / AXIS_FREQS)).astype(np.float32)
    ang = np.concatenate([row[:, None] * inv, col[:, None] * inv], axis=-1).astype(np.float32)
    cos, sin = np.cos(ang), np.sin(ang)
    c64 = np.concatenate([cos, cos], axis=-1)
    s64 = np.concatenate([-sin, sin], axis=-1)
    return (jnp.asarray(np.tile(c64, (1, 2)), F32), jnp.asarray(np.tile(s64, (1, 2)), F32))


def _proj_kernel(*refs, segs, rope, with_nat):
    it = iter(refs)
    x_ref, mod_ref, g_ref, w_ref, wvt_ref, bg_ref = (next(it) for _ in range(6))
    cos_ref = sin_ref = None
    if rope:
        cos_ref, sin_ref = next(it), next(it)
    q_ref, kd_ref, vd_ref = next(it), next(it), next(it)
    nat_ref = next(it) if with_nat else None
    mq_ref, mk_ref, mv_ref, gates_ref, mo_ref, ga_ref, gm_ref = (next(it) for _ in range(7))

    x = x_ref[0]
    shift = mod_ref[0, 0:1, :]
    scale = mod_ref[0, 1:2, :]
    ms = jnp.mean(x * x, axis=-1, keepdims=True)
    xn = x * lax.rsqrt(ms + EPS) * g_ref[...]
    xb = (xn * (1.0 + scale) + shift).astype(BF16)

    def seg(name):
        lo, hi = segs[name]
        return jnp.dot(xb, w_ref[:, lo:hi], preferred_element_type=F32)

    uq = seg("q")
    ukd = seg("kd")
    if rope:
        uqr = seg("qrot")
        ukr = seg("kdrot")
        cs = cos_ref[...]
        sn = sin_ref[...]
        for h in range(ATT_HEADS // 2):
            sl = slice(h * LANES, (h + 1) * LANES)
            q_ref[0, h] = ((uq[:, sl] * cs + uqr[:, sl] * sn) * HEAD_DIM ** -0.5).astype(BF16)
        for h in range(ATT_KV_HEADS):
            sl = slice(h * LANES, (h + 1) * LANES)
            kd_ref[0, :, sl] = (ukd[:, sl] * cs + ukr[:, sl] * sn).astype(BF16)
    else:
        for h in range(ATT_HEADS // 2):
            sl = slice(h * LANES, (h + 1) * LANES)
            q_ref[0, h] = (uq[:, sl] * HEAD_DIM ** -0.5).astype(BF16)
        kd_ref[0] = ukd.astype(BF16)
    vd_ref[0] = lax.dot_general(wvt_ref[...], xb, (((1,), (1,)), ((), ())),
                                preferred_element_type=F32).astype(BF16)
    if with_nat:
        nat_ref[0] = seg("nat")
    mq_ref[0] = (seg("mq") * ML_DK ** -0.5).astype(BF16)
    mk_ref[0] = seg("mk").astype(BF16)
    mv_ref[0] = seg("mv").astype(BF16)
    gates_ref[0] = seg("gates") + bg_ref[...]
    mo_ref[0] = seg("mo")
    ga_ref[0] = seg("ga")
    gm_ref[0] = seg("gm")


def _project(x, mod, g_pre, w_in, b_gates, mod_row, rope, with_nat, tm, b0, bsz):
    _, seq, d = x.shape
    wp, wvd_t, bg, segs = _pack_w_in(w_in, b_gates, rope, with_nat)
    nw = wp.shape[1]
    tok3 = lambda b, i: (b, i, 0)
    in_specs = [pl.BlockSpec((1, tm, d), lambda b, i: (b + b0, i, 0)),
                pl.BlockSpec((1, 6, d), lambda b, i: (mod_row(b), 0, 0)),
                _const_spec((1, d)),
                pl.BlockSpec((d, nw), lambda b, i: (0, 0), pipeline_mode=pl.Buffered(1)),
                _const_spec((2 * LANES, d)), _const_spec((1, LANES))]
    args = [x, mod, g_pre.reshape(1, d), wp, wvd_t, bg]
    if rope:
        cos_t, sin_t = _rope_tables(seq)
        in_specs += [pl.BlockSpec((tm, LANES), lambda b, i: (i, 0))] * 2
        args += [cos_t, sin_t]

    def tok_out(width, dtype):
        return (jax.ShapeDtypeStruct((bsz, seq, width), dtype), pl.BlockSpec((1, tm, width), tok3))

    outs = [(jax.ShapeDtypeStruct((bsz, ATT_HEADS // 2, seq, LANES), BF16),
             pl.BlockSpec((1, ATT_HEADS // 2, tm, LANES), lambda b, i: (b, 0, i, 0))),
            tok_out(2 * LANES, BF16),
            (jax.ShapeDtypeStruct((bsz, 2 * LANES, seq), BF16),
             pl.BlockSpec((1, 2 * LANES, tm), lambda b, i: (b, 0, i)))]
    if with_nat:
        outs.append(tok_out(2 * KV_W, F32))
    outs += [tok_out(MLQK_W, BF16), tok_out(MLQK_W, BF16), tok_out(MLV_W, BF16), tok_out(LANES, F32),
             tok_out(MLV_W, F32), tok_out(D_MODEL, F32), tok_out(D_MODEL, F32)]
    res = pl.pallas_call(
        functools.partial(_proj_kernel, segs=segs, rope=rope, with_nat=with_nat),
        grid=(bsz, seq // tm),
        in_specs=in_specs,
        out_specs=[o[1] for o in outs],
        out_shape=[o[0] for o in outs],
        compiler_params=_cparams(("parallel", "parallel")),
        name="proj_rope" if rope else "proj_ctx",
    )(*args)
    names = ["q", "kd", "vd"] + (["nat"] if with_nat else []) + ["mq", "mk", "mv", "gates", "mo", "ga", "gm"]
    return dict(zip(names, res))


def _attn_kernel(*refs, window, n_blocks, n_ctx):
    it = iter(refs)
    sink_ref, q_ref = next(it), next(it)
    if window:
        kp_ref, kc_ref, kn_ref, vp_ref, vc_ref, vn_ref = (next(it) for _ in range(6))
    ck_ref, cv_ref, o_ref = next(it), next(it), next(it)
    j = pl.program_id(1)
    n_win = 3 * BLOCK if window else 0
    cols = ATT_GROUP * BLOCK

    col_id = lax.broadcasted_iota(jnp.int32, (1, cols), 1)
    if window:
        kl = lax.broadcasted_iota(jnp.int32, (BLOCK, cols), 0)
        ql = lax.broadcasted_iota(jnp.int32, (BLOCK, cols), 1) & (BLOCK - 1)
        prev_ok = kl >= ql + jnp.where(j == 0, BLOCK, 0)
        next_ok = kl <= ql - jnp.where(j == n_blocks - 1, BLOCK, 0)
    lane_lo = lax.broadcasted_iota(jnp.int32, (BLOCK, LANES), 1) < HEAD_DIM
    lane_row = lax.broadcasted_iota(jnp.int32, (1, LANES), 1) < HEAD_DIM
    half_lo = jnp.where(lane_row, 1.0, 0.0).astype(BF16)
    half_hi = jnp.where(lane_row, 0.0, 1.0).astype(BF16)

    for kvh in range(ATT_KV_HEADS):
        sl = slice(kvh * LANES, (kvh + 1) * LANES)
        if window:
            keys = jnp.concatenate([kp_ref[0, :, sl], kc_ref[0, :, sl], kn_ref[0, :, sl], ck_ref[0, :, sl]], axis=0)
            vals_t = jnp.concatenate([vp_ref[0, sl, :], vc_ref[0, sl, :], vn_ref[0, sl, :], cv_ref[0, sl, :]], axis=1)
        else:
            keys = ck_ref[0, :, sl]
            vals_t = cv_ref[0, sl, :]
        qs = jnp.concatenate([q_ref[0, kvh * (ATT_GROUP // 2) + g // 2] * (half_lo if g % 2 == 0 else half_hi)
                              for g in range(ATT_GROUP)], axis=0)
        s = lax.dot_general(keys, qs, (((1,), (1,)), ((), ())), preferred_element_type=F32)
        if window:
            s = jnp.concatenate([jnp.where(prev_ok, s[:BLOCK], NEG), s[BLOCK:2 * BLOCK],
                                 jnp.where(next_ok, s[2 * BLOCK:n_win], NEG), s[n_win:]], axis=0)
        snk = jnp.full((1, cols), sink_ref[kvh * ATT_GROUP + ATT_GROUP - 1], F32)
        for g in range(ATT_GROUP - 2, -1, -1):
            snk = jnp.where(col_id < (g + 1) * BLOCK, sink_ref[kvh * ATT_GROUP + g], snk)
        m = jnp.maximum(jnp.max(s, axis=0, keepdims=True), snk)
        p = jnp.exp(s - m)
        den = jnp.sum(p, axis=0, keepdims=True) + jnp.exp(snk - m)
        o_t = jnp.dot(vals_t, p.astype(BF16), preferred_element_type=F32) / den
        for pair in range(ATT_GROUP // 2):
            even = o_t[:, (2 * pair) * BLOCK:(2 * pair + 1) * BLOCK].T
            odd = o_t[:, (2 * pair + 1) * BLOCK:(2 * pair + 2) * BLOCK].T
            col = (kvh * (ATT_GROUP // 2) + pair) * LANES
            o_ref[0, :, col:col + LANES] = jnp.where(lane_lo, even, odd).astype(BF16)


def _attention(q, kd, vd_t, ckd, cvd_t, sink, window):
    bsz, _, seq, _ = q.shape
    nb = seq // BLOCK
    n_ctx = ckd.shape[1]
    in_specs = [pl.BlockSpec(memory_space=pltpu.SMEM),
                pl.BlockSpec((1, ATT_HEADS // 2, BLOCK, LANES), lambda b, j: (b, 0, j, 0))]
    args = [sink.astype(F32), q]
    if window:
        in_specs += [pl.BlockSpec((1, BLOCK, 2 * LANES), lambda b, j: (b, jnp.maximum(j - 1, 0), 0)),
                     pl.BlockSpec((1, BLOCK, 2 * LANES), lambda b, j: (b, j, 0)),
                     pl.BlockSpec((1, BLOCK, 2 * LANES), lambda b, j: (b, jnp.minimum(j + 1, nb - 1), 0)),
                     pl.BlockSpec((1, 2 * LANES, BLOCK), lambda b, j: (b, 0, jnp.maximum(j - 1, 0))),
                     pl.BlockSpec((1, 2 * LANES, BLOCK), lambda b, j: (b, 0, j)),
                     pl.BlockSpec((1, 2 * LANES, BLOCK), lambda b, j: (b, 0, jnp.minimum(j + 1, nb - 1)))]
        args += [kd, kd, kd, vd_t, vd_t, vd_t]
    in_specs += [pl.BlockSpec((1, n_ctx, 2 * LANES), lambda b, j: (b, 0, 0)),
                 pl.BlockSpec((1, 2 * LANES, n_ctx), lambda b, j: (b, 0, 0))]
    args += [ckd, cvd_t]
    return pl.pallas_call(
        functools.partial(_attn_kernel, window=window, n_blocks=nb, n_ctx=n_ctx),
        grid=(bsz, nb),
        in_specs=in_specs,
        out_specs=pl.BlockSpec((1, BLOCK, Q_W), lambda b, j: (b, j, 0)),
        out_shape=jax.ShapeDtypeStruct((bsz, seq, Q_W), BF16),
        compiler_params=_cparams(("parallel", "parallel")),
        name="attn_window" if window else "attn_ctx",
    )(*args)


def _dup_heads(t, transpose):
    b, l = t.shape[:2]
    out = jnp.concatenate([t, t], axis=-1).reshape(b, l, ATT_KV_HEADS * LANES).astype(BF16)
    return jnp.swapaxes(out, 1, 2) if transpose else out


def _split3(x):
    hi = x.astype(BF16)
    r1 = x - hi.astype(F32)
    mid = r1.astype(BF16)
    lo = (r1 - mid.astype(F32)).astype(BF16)
    return hi, mid, lo


def _log_sigmoid(x):
    return jnp.minimum(x, 0.0) - jnp.log1p(jnp.exp(-jnp.abs(x)))


def _mlstm_kernel(*refs, n_chunks, has_init, emit_state):
    it = iter(refs)
    qkvg = [[next(it) for _ in range(4)] for _ in range(2)]
    if has_init:
        c0_ref, n0_ref, m0_ref = next(it), next(it), next(it)
    h_refs = [next(it), next(it)]
    if emit_state:
        co_ref, no_ref, mo_ref = next(it), next(it), next(it)
    ct_s, n_s, m_s = next(it), next(it), next(it)
    t = pl.program_id(1)
    L = ML_CHUNK

    @pl.when(t == 0)
    def _():
        for d in range(2):
            for h in range(ML_HEADS):
                if has_init:
                    ct_s[d, h] = c0_ref[0, d, h].T
                    n_s[d, h] = n0_ref[0, d, h:h + 1, :]
                    m_s[d, h] = m0_ref[0, d, h:h + 1, :]
                else:
                    ct_s[d, h] = jnp.zeros((ML_DK, ML_DV), F32)
                    n_s[d, h] = jnp.zeros((1, ML_DK), F32)
                    m_s[d, h] = jnp.zeros((1, LANES), F32)

    ri = lax.broadcasted_iota(jnp.int32, (L, L), 0)
    ci = lax.broadcasted_iota(jnp.int32, (L, L), 1)

    for d in range(2):
        q_ref, k_ref, v_ref, g_ref = qkvg[d]
        tri = (ci <= ri) if d == 0 else (ci >= ri)
        tri_b = jnp.where(tri, 1.0, 0.0).astype(BF16)
        g = g_ref[0]
        lf = _log_sigmoid(g)
        g_t = g.T
        lf_t = lf.T
        bc = sum(jnp.dot(tri_b, part, preferred_element_type=F32) for part in _split3(lf))
        br = sum(lax.dot_general(part, tri_b, (((1,), (1,)), ((), ())), preferred_element_type=F32)
                 for part in _split3(lf_t))
        tot = jnp.sum(lf, axis=0, keepdims=True)

        for h in range(ML_HEADS):
            icol_i = 2 * ML_HEADS * d + h
            fcol = icol_i + ML_HEADS
            bcol = bc[:, fcol:fcol + 1]
            brow = br[fcol:fcol + 1, :]
            irow = g_t[icol_i:icol_i + 1, :]
            icol = g[:, icol_i:icol_i + 1]
            b_last = tot[:, fcol:fcol + 1]
            m_prev = m_s[d, h][:, 0:1]
            n_prev = n_s[d, h]
            ct_prev = ct_s[d, h]
            qh = q_ref[0, :, h * ML_DK:(h + 1) * ML_DK]
            kh = k_ref[0, :, h * ML_DK:(h + 1) * ML_DK]
            vh = v_ref[0, :, h * ML_DV:(h + 1) * ML_DV]

            rel = jnp.where(tri, irow - brow, NEG)
            inter = bcol + m_prev
            m_t = jnp.maximum(inter, bcol + jnp.max(rel, axis=-1, keepdims=True))
            w = jnp.exp(rel + (bcol - m_t))
            w_inter = jnp.exp(inter - m_t)
            s = lax.dot_general(qh, kh, (((1,), (1,)), ((), ())), preferred_element_type=F32) * w
            cq = jnp.dot(qh, ct_prev.astype(BF16), preferred_element_type=F32) * w_inter
            num = jnp.dot(s.astype(BF16), vh, preferred_element_type=F32) + cq
            qn = jnp.sum(qh.astype(F32) * n_prev, axis=-1, keepdims=True)
            den = jnp.sum(s, axis=-1, keepdims=True) + w_inter * qn
            h_refs[d][0, :, h * ML_DV:(h + 1) * ML_DV] = num / jnp.maximum(jnp.abs(den), jnp.exp(-m_t))

            a = b_last - bcol + icol
            m_new = jnp.maximum(b_last + m_prev, jnp.max(a, axis=0, keepdims=True))
            wk = jnp.exp(a - m_new)
            decay = jnp.exp(b_last + m_prev - m_new)
            kw = kh.astype(F32) * wk
            ct_new = decay * ct_prev + jnp.dot(kw.T.astype(BF16), vh, preferred_element_type=F32)
            n_new = decay * n_prev + jnp.sum(kw, axis=0, keepdims=True)
            ct_s[d, h] = ct_new
            n_s[d, h] = n_new
            m_s[d, h] = jnp.broadcast_to(m_new, (1, LANES))
            if emit_state:
                @pl.when(t == n_chunks - 1)
                def _():
                    co_ref[0, d, h] = ct_new.T
                    no_ref[0, d, h:h + 1, :] = n_new
                    mo_ref[0, d, h:h + 1, :] = jnp.broadcast_to(m_new, (1, LANES))


def _mlstm(mq, mk, mv, gates, init, emit_state):
    bsz, seq, _ = mq.shape
    nc = seq // ML_CHUNK
    fwd = lambda b, t: (b, t, 0)
    bwd = lambda b, t: (b, nc - 1 - t, 0)
    state5 = lambda b, t: (b, 0, 0, 0, 0)
    state4 = lambda b, t: (b, 0, 0, 0)
    in_specs, args = [], []
    for chunk in (fwd, bwd):
        in_specs += [pl.BlockSpec((1, ML_CHUNK, MLQK_W), chunk), pl.BlockSpec((1, ML_CHUNK, MLQK_W), chunk),
                     pl.BlockSpec((1, ML_CHUNK, MLV_W), chunk), pl.BlockSpec((1, ML_CHUNK, LANES), chunk)]
        args += [mq, mk, mv, gates]
    if init is not None:
        c0, n0, m0 = init
        in_specs += [pl.BlockSpec((1, 2, ML_HEADS, ML_DV, ML_DK), state5),
                     pl.BlockSpec((1, 2, ML_HEADS, ML_DK), state4),
                     pl.BlockSpec((1, 2, ML_HEADS, LANES), state4)]
        args += [c0.astype(F32), n0.astype(F32),
                 jnp.broadcast_to(m0.astype(F32)[..., None], m0.shape + (LANES,))]
    out_shape = [jax.ShapeDtypeStruct((bsz, seq, MLV_W), F32)] * 2
    out_specs = [pl.BlockSpec((1, ML_CHUNK, MLV_W), fwd), pl.BlockSpec((1, ML_CHUNK, MLV_W), bwd)]
    if emit_state:
        out_shape += [jax.ShapeDtypeStruct((bsz, 2, ML_HEADS, ML_DV, ML_DK), F32),
                      jax.ShapeDtypeStruct((bsz, 2, ML_HEADS, ML_DK), F32),
                      jax.ShapeDtypeStruct((bsz, 2, ML_HEADS, LANES), F32)]
        out_specs += [pl.BlockSpec((1, 2, ML_HEADS, ML_DV, ML_DK), state5),
                      pl.BlockSpec((1, 2, ML_HEADS, ML_DK), state4),
                      pl.BlockSpec((1, 2, ML_HEADS, LANES), state4)]
    return pl.pallas_call(
        functools.partial(_mlstm_kernel, n_chunks=nc, has_init=init is not None, emit_state=emit_state),
        grid=(bsz, nc),
        in_specs=in_specs,
        out_specs=out_specs,
        out_shape=out_shape,
        scratch_shapes=[pltpu.VMEM((2, ML_HEADS, ML_DK, ML_DV), F32),
                        pltpu.VMEM((2, ML_HEADS, 1, ML_DK), F32),
                        pltpu.VMEM((2, ML_HEADS, 1, LANES), F32)],
        compiler_params=_cparams(("parallel", "arbitrary")),
        name="mlstm_state" if emit_state else "mlstm",
    )(*args)


def _sigmoid(x):
    return 0.5 * jnp.tanh(0.5 * x) + 0.5


def _rms(x, g):
    return x * lax.rsqrt(jnp.mean(x * x, axis=-1, keepdims=True) + EPS) * g


def _pack_bf16_pairs(x):
    n = x.shape[1] // 2
    lo = pltpu.bitcast(x[:, :n].astype(BF16).astype(F32), jnp.int32)
    hi = pltpu.bitcast(x[:, n:].astype(BF16).astype(F32), jnp.int32)
    return lax.shift_right_logical(lo, 16) | hi


def _unpack_bf16_pairs(p):
    lo = pltpu.bitcast(lax.shift_left(p, 16), F32)
    hi = pltpu.bitcast(p & jnp.int32(-65536), F32)
    return lo, hi


def _merge_kernel(att_ref, hf_ref, hb_ref, mo_ref, ga_ref, gm_ref, x_ref, mod_ref, nml_ref, gpm_ref, gpf_ref,
                  wua_ref, wum_ref, wo_ref, wr_ref, br_ref, x1_ref, h2_ref, idx_ref, gate_ref, cnt_ref):
    hsum = hf_ref[0] + hb_ref[0]
    parts = []
    for h in range(ML_HEADS):
        hh = hsum[:, h * ML_DV:(h + 1) * ML_DV]
        parts.append(hh * lax.rsqrt(jnp.mean(hh * hh, axis=-1, keepdims=True) + EPS))
    hn = jnp.concatenate(parts, axis=-1) * nml_ref[...]
    ml = (hn * _sigmoid(mo_ref[0])).astype(BF16)
    a = jnp.dot(att_ref[0], wua_ref[...], preferred_element_type=F32)
    m = jnp.dot(ml, wum_ref[...], preferred_element_type=F32)
    z = (_sigmoid(ga_ref[0]) * a + _sigmoid(gm_ref[0]) * m).astype(BF16)
    mix = jnp.dot(z, wo_ref[...], preferred_element_type=F32)
    gate1 = mod_ref[0, 2:3, :]
    shift2 = mod_ref[0, 3:4, :]
    scale2 = mod_ref[0, 4:5, :]
    x1 = x_ref[0] + gate1 * _rms(mix, gpm_ref[...])
    x1_ref[0] = x1
    h2 = _rms(x1, gpf_ref[...]) * (1.0 + scale2) + shift2
    h2_ref[0] = _pack_bf16_pairs(h2)

    h2_hi = h2.astype(BF16)
    h2_lo = (h2 - h2_hi.astype(F32)).astype(BF16)
    wr = wr_ref[...]
    wr_hi = wr.astype(BF16)
    wr_lo = (wr - wr_hi.astype(F32)).astype(BF16)
    logits = (jnp.dot(h2_hi, wr_hi, preferred_element_type=F32)
              + jnp.dot(h2_lo, wr_hi, preferred_element_type=F32)
              + jnp.dot(h2_hi, wr_lo, preferred_element_type=F32)) + br_ref[...]
    lane = lax.broadcasted_iota(jnp.int32, logits.shape, 1).astype(F32)
    work = jnp.where(lane < N_EXPERTS, logits, -jnp.inf)
    idx_out = jnp.zeros(logits.shape, F32)
    val_out = jnp.zeros(logits.shape, F32)
    picked = jnp.zeros(logits.shape, F32)
    top0 = None
    esum = None
    for k in range(TOP_K):
        mx = jnp.max(work, axis=-1, keepdims=True)
        sel = jnp.min(jnp.where(work == mx, lane, float(LANES)), axis=-1, keepdims=True)
        if k == 0:
            top0 = mx
        e = jnp.exp(mx - top0)
        esum = e if k == 0 else esum + e
        idx_out = jnp.where(lane == k, sel, idx_out)
        val_out = jnp.where(lane == k, e, val_out)
        picked = jnp.where(lane == sel, 1.0, picked)
        work = jnp.where(lane == sel, -jnp.inf, work)
    idx_ref[0] = idx_out.astype(jnp.int32)
    gate_ref[0] = val_out / esum
    for r in range(logits.shape[0] // ROUTE_TILE):
        cnt_ref[0, r:r + 1, :] = jnp.sum(picked[r * ROUTE_TILE:(r + 1) * ROUTE_TILE], axis=0, keepdims=True)


def _merge(att, hf, hb, pj, x, mod, mod_row, p, tm, b0):
    bsz = att.shape[0]
    _, seq, d = x.shape
    tok3 = lambda b, i: (b, i, 0)
    wr = jnp.pad(p["w_router"].astype(F32), ((0, 0), (0, LANES - N_EXPERTS)))
    br = jnp.pad(p["b_router"].astype(F32), (0, LANES - N_EXPERTS)).reshape(1, LANES)
    row = lambda v: v.astype(F32).reshape(1, -1)
    in_specs = [pl.BlockSpec((1, tm, Q_W), tok3),
                pl.BlockSpec((1, tm, MLV_W), tok3), pl.BlockSpec((1, tm, MLV_W), tok3),
                pl.BlockSpec((1, tm, MLV_W), tok3), pl.BlockSpec((1, tm, d), tok3), pl.BlockSpec((1, tm, d), tok3),
                pl.BlockSpec((1, tm, d), lambda b, i: (b + b0, i, 0)),
                pl.BlockSpec((1, 6, d), lambda b, i: (mod_row(b), 0, 0)),
                _const_spec((1, MLV_W)), _const_spec((1, d)), _const_spec((1, d)),
                _const_spec((Q_W, d)), _const_spec((MLV_W, d)), _const_spec((d, d)),
                _const_spec((d, LANES)), _const_spec((1, LANES))]
    out_shape = [jax.ShapeDtypeStruct((bsz, seq, d), F32), jax.ShapeDtypeStruct((bsz, seq, d // 2), jnp.int32),
                 jax.ShapeDtypeStruct((bsz, seq, LANES), jnp.int32), jax.ShapeDtypeStruct((bsz, seq, LANES), F32),
                 jax.ShapeDtypeStruct((bsz * (seq // tm), tm // ROUTE_TILE, LANES), F32)]
    out_specs = [pl.BlockSpec((1, tm, d), tok3), pl.BlockSpec((1, tm, d // 2), tok3),
                 pl.BlockSpec((1, tm, LANES), tok3), pl.BlockSpec((1, tm, LANES), tok3),
                 pl.BlockSpec((1, tm // ROUTE_TILE, LANES), lambda b, i: (b * (seq // tm) + i, 0, 0))]
    return pl.pallas_call(
        _merge_kernel,
        grid=(bsz, seq // tm),
        in_specs=in_specs,
        out_specs=out_specs,
        out_shape=out_shape,
        compiler_params=_cparams(("parallel", "parallel")),
        name="merge_router",
    )(att, hf, hb, pj["mo"], pj["ga"], pj["gm"], x, mod, row(p["norm_ml"]), row(p["g_post_mix"]),
      row(p["g_pre_ffn"]), p["w_up_att"].astype(BF16), p["w_up_ml"].astype(BF16), p["w_out"].astype(BF16), wr, br)


def _sc_gather_rows(table, idx):
    n = idx.shape[0]
    width = table.shape[1]
    chunk = SC_GATHER_CHUNK
    n_workers = SC_CORES * SC_SUBCORES
    rows_per_worker = n // n_workers
    n_pairs = rows_per_worker // (2 * chunk)
    assert n_pairs * 2 * chunk * n_workers == n
    mesh = plsc.VectorSubcoreMesh(core_axis_name="c", subcore_axis_name="s")

    def body(table_hbm, idx_hbm, out_hbm, idx_v, rows_v, gsem, osem):
        wid = lax.axis_index("s") * SC_CORES + lax.axis_index("c")
        base0 = wid * rows_per_worker

        def gather_copy(b):
            return pltpu.make_async_copy(table_hbm.at[idx_v.at[b]], rows_v.at[b], gsem.at[b])

        def write_copy(ci, b):
            base = pl.multiple_of(base0 + ci * chunk, 8)
            return pltpu.make_async_copy(rows_v.at[b], out_hbm.at[pl.ds(base, chunk)], osem.at[b])

        def issue(ci, b):
            base = pl.multiple_of(base0 + ci * chunk, 8)
            pltpu.sync_copy(idx_hbm.at[pl.ds(base, chunk)], idx_v.at[b])
            gather_copy(b).start()

        def finish(ci, b):
            gather_copy(b).wait()
            write_copy(ci, b).start()

        issue(0, 0)

        @pl.loop(0, n_pairs)
        def _(j):
            @pl.when(j > 0)
            def _():
                write_copy(2 * j - 1, 1).wait()
            issue(2 * j + 1, 1)
            finish(2 * j, 0)

            @pl.when(j < n_pairs - 1)
            def _():
                write_copy(2 * j, 0).wait()
                issue(2 * j + 2, 0)
            finish(2 * j + 1, 1)

        write_copy(2 * n_pairs - 2, 0).wait()
        write_copy(2 * n_pairs - 1, 1).wait()

    return pl.kernel(
        body, mesh=mesh,
        out_type=jax.ShapeDtypeStruct((n, width), table.dtype),
        scratch_types=[pltpu.VMEM((2, chunk), jnp.int32), pltpu.VMEM((2, chunk, width), table.dtype),
                       pltpu.SemaphoreType.DMA((2,)), pltpu.SemaphoreType.DMA((2,))],
    )(table, idx)


def _ffn_kernel(be_ref, nu_ref, xs_ref, w1_ref, b1_ref, w2_ref, b2_ref, ys_ref, w1b_s, w2b_s):
    i = pl.program_id(0)
    e = be_ref[i]
    e_prev = be_ref[jnp.maximum(i - 1, 0)]

    @pl.when((i == 0) | (e != e_prev))
    def _():
        w1b_s[...] = w1_ref[0].astype(BF16)
        w2b_s[...] = w2_ref[0].astype(BF16)

    @pl.when(i < nu_ref[0])
    def _():
        half = D_MODEL // 2
        x_lo, x_hi = _unpack_bf16_pairs(xs_ref[...])
        hmid = (jnp.dot(x_lo.astype(BF16), w1b_s[:half, :], preferred_element_type=F32)
                + jnp.dot(x_hi.astype(BF16), w1b_s[half:, :], preferred_element_type=F32)) + b1_ref[0]
        glu = jnp.minimum(hmid[:, :D_FF], SWIGLU_LIMIT)
        lin = jnp.clip(hmid[:, D_FF:], -SWIGLU_LIMIT, SWIGLU_LIMIT)
        act = ((lin + 1.0) * glu * _sigmoid(SWIGLU_ALPHA * glu)).astype(BF16)
        y = jnp.dot(act, w2b_s[...], preferred_element_type=F32) + b2_ref[0]
        ys_ref[...] = _pack_bf16_pairs(y)

    @pl.when(i >= nu_ref[0])
    def _():
        ys_ref[...] = jnp.zeros(ys_ref.shape, jnp.int32)


def _sc_scatter_rows(tables, positions, n_rows):
    width = tables[0].shape[1]
    chunk = SC_GATHER_CHUNK
    n_workers = SC_CORES * SC_SUBCORES
    n_streams = len(tables)
    mesh = plsc.VectorSubcoreMesh(core_axis_name="c", subcore_axis_name="s")

    def body(*refs):
        table_refs = refs[0:2 * n_streams:2]
        pos_refs = refs[1:2 * n_streams:2]
        out_hbm, idx_v, rows_v, lsem, ssem = refs[2 * n_streams:]
        wid = lax.axis_index("s") * SC_CORES + lax.axis_index("c")

        for table_hbm, pos_hbm in zip(table_refs, pos_refs):
            tok_per_worker = table_hbm.shape[0] // n_workers
            n_pairs = tok_per_worker // (2 * chunk)
            assert n_pairs * 2 * chunk * n_workers == table_hbm.shape[0]
            base0 = wid * tok_per_worker

            def load_copy(ci, b):
                base = pl.multiple_of(base0 + ci * chunk, 8)
                return pltpu.make_async_copy(table_hbm.at[pl.ds(base, chunk)], rows_v.at[b], lsem.at[b])

            def scatter_copy(b, k):
                return pltpu.make_async_copy(rows_v.at[b], out_hbm.at[idx_v.at[b, k]], ssem.at[b])

            def load(ci, b):
                base = pl.multiple_of(base0 + ci * chunk, 8)
                load_copy(ci, b).start()
                for k in range(TOP_K):
                    pltpu.sync_copy(pos_hbm.at[k, pl.ds(base, chunk)], idx_v.at[b, k])

            def scatter(ci, b):
                load_copy(ci, b).wait()
                for k in range(TOP_K):
                    scatter_copy(b, k).start()

            def drain(b):
                for k in range(TOP_K):
                    scatter_copy(b, k).wait()

            load(0, 0)

            @pl.loop(0, n_pairs)
            def _(j):
                @pl.when(j > 0)
                def _():
                    drain(1)
                load(2 * j + 1, 1)
                scatter(2 * j, 0)

                @pl.when(j < n_pairs - 1)
                def _():
                    drain(0)
                    load(2 * j + 2, 0)
                scatter(2 * j + 1, 1)

            drain(0)
            drain(1)

    args = [a for pair in zip(tables, positions) for a in pair]
    return pl.kernel(
        body, mesh=mesh,
        out_type=jax.ShapeDtypeStruct((n_rows, width), tables[0].dtype),
        scratch_types=[pltpu.VMEM((2, TOP_K, chunk), jnp.int32), pltpu.VMEM((2, chunk, width), tables[0].dtype),
                       pltpu.SemaphoreType.DMA((2,)), pltpu.SemaphoreType.DMA((2,))],
    )(*args)


def _pos_kernel(idx_ref, base_ref, pos_ref):
    idx = idx_ref[...]
    rows = idx.shape[0]
    lane = lax.broadcasted_iota(jnp.int32, idx.shape, 1)
    hots = [jnp.where(lane == idx[:, k:k + 1], 1.0, 0.0) for k in range(TOP_K)]
    cnt = hots[0] + hots[1] + hots[2] + hots[3]
    ri = lax.broadcasted_iota(jnp.int32, (rows, rows), 0)
    ci = lax.broadcasted_iota(jnp.int32, (rows, rows), 1)
    earlier = jnp.where(ci < ri, 1.0, 0.0).astype(BF16)
    ahead = jnp.dot(earlier, cnt.astype(BF16), preferred_element_type=F32) + base_ref[0]
    posmat = jnp.zeros(idx.shape, F32)
    for k in range(TOP_K):
        posmat = jnp.where(lane == k, jnp.sum(hots[k] * ahead, axis=-1, keepdims=True), posmat)
    pos_ref[...] = posmat.T[:TOP_K, :].astype(jnp.int32)


def _route(idx_list, count_list):
    tiles = [c.shape[0] * c.shape[1] for c in count_list]
    n_tok = sum(tiles) * ROUTE_TILE
    n_blocks = n_tok * TOP_K // MOE_ROWS + N_EXPERTS
    cnt = jnp.concatenate([c.reshape(-1, LANES) for c in count_list], axis=0).astype(jnp.int32)
    tile_off = jnp.cumsum(cnt, axis=0) - cnt
    total = jnp.sum(cnt, axis=0)
    padded = (total + MOE_ROWS - 1) // MOE_ROWS * MOE_ROWS
    pend = jnp.cumsum(padded)
    base = ((pend - padded)[None, :] + tile_off).astype(F32)
    starts = jnp.arange(n_blocks, dtype=jnp.int32) * MOE_ROWS
    block_e = jnp.minimum(jnp.sum(pend[None, :N_EXPERTS] <= starts[:, None], axis=1), N_EXPERTS - 1).astype(jnp.int32)
    n_used = (pend[N_EXPERTS - 1] // MOE_ROWS).astype(jnp.int32).reshape(1)
    positions, first = [], 0
    for idx, n_tiles in zip(idx_list, tiles):
        stream_tok = n_tiles * ROUTE_TILE
        positions.append(pl.pallas_call(
            _pos_kernel,
            grid=(n_tiles,),
            in_specs=[pl.BlockSpec((ROUTE_TILE, LANES), lambda i: (i, 0)),
                      pl.BlockSpec((1, 1, LANES), lambda i: (i, 0, 0))],
            out_specs=pl.BlockSpec((TOP_K, ROUTE_TILE), lambda i: (0, i)),
            out_shape=jax.ShapeDtypeStruct((TOP_K, stream_tok), jnp.int32),
            compiler_params=_cparams(("parallel",)),
            name="route_pos",
        )(idx.reshape(stream_tok, LANES), base[first:first + n_tiles].reshape(n_tiles, 1, LANES)))
        first += n_tiles
    return block_e, n_used, positions


def _moe(h2p_list, idx_list, count_list, w1, b1, w2, b2):
    half = h2p_list[0].shape[1]
    d = 2 * half
    block_e, n_used, positions = _route(idx_list, count_list)
    n_blocks = block_e.shape[0]
    xs = _sc_scatter_rows(h2p_list, positions, n_blocks * MOE_ROWS)
    grid_spec = pltpu.PrefetchScalarGridSpec(
        num_scalar_prefetch=2,
        grid=(n_blocks,),
        in_specs=[pl.BlockSpec((MOE_ROWS, half), lambda i, be, nu: (i, 0)),
                  pl.BlockSpec((1, d, 2 * D_FF), lambda i, be, nu: (be[i], 0, 0)),
                  pl.BlockSpec((1, 1, 2 * D_FF), lambda i, be, nu: (be[i], 0, 0)),
                  pl.BlockSpec((1, D_FF, d), lambda i, be, nu: (be[i], 0, 0)),
                  pl.BlockSpec((1, 1, d), lambda i, be, nu: (be[i], 0, 0))],
        out_specs=pl.BlockSpec((MOE_ROWS, half), lambda i, be, nu: (i, 0)),
        scratch_shapes=[pltpu.VMEM((d, 2 * D_FF), BF16), pltpu.VMEM((D_FF, d), BF16)],
    )
    ys = pl.pallas_call(
        _ffn_kernel,
        grid_spec=grid_spec,
        out_shape=jax.ShapeDtypeStruct((n_blocks * MOE_ROWS, half), jnp.int32),
        compiler_params=_cparams(("arbitrary",)),
        name="moe_ffn",
    )(block_e, n_used, xs, w1, b1.reshape(N_EXPERTS, 1, -1), w2, b2.reshape(N_EXPERTS, 1, -1))
    return [_sc_gather_rows(ys, pos.reshape(-1)) for pos in positions]


def _final_kernel(x1_ref, y0_ref, y1_ref, y2_ref, y3_ref, gate_ref, mod_ref, g_ref, *rest):
    o_ref = rest[-1]
    gates = gate_ref[0]
    lo = hi = None
    for k, y_ref in enumerate((y0_ref, y1_ref, y2_ref, y3_ref)):
        y_lo, y_hi = _unpack_bf16_pairs(y_ref[0, 0])
        gk = gates[:, k:k + 1]
        lo = gk * y_lo if k == 0 else lo + gk * y_lo
        hi = gk * y_hi if k == 0 else hi + gk * y_hi
    y = jnp.concatenate([lo, hi], axis=-1)
    gate2 = mod_ref[0, 5:6, :]
    o_ref[0] = x1_ref[0] + gate2 * _rms(y, g_ref[...])


def _final(x1, yg, gate, mod, mod_row, g_post_ffn, tm, b0, n_batch, out_prev):
    bsz, seq, d = x1.shape
    tok3 = lambda b, i: (b, i, 0)
    yg = yg.reshape(TOP_K, bsz, seq, d // 2)
    slot_specs = [pl.BlockSpec((1, 1, tm, d // 2), functools.partial(lambda b, i, k: (k, b, i, 0), k=k))
                  for k in range(TOP_K)]
    in_specs = ([pl.BlockSpec((1, tm, d), tok3)] + slot_specs +
                [pl.BlockSpec((1, tm, LANES), tok3),
                 pl.BlockSpec((1, 6, d), lambda b, i: (mod_row(b), 0, 0)), _const_spec((1, d))])
    args = [x1, yg, yg, yg, yg, gate, mod, g_post_ffn.astype(F32).reshape(1, d)]
    aliases = {}
    if out_prev is not None:
        in_specs.append(pl.BlockSpec(memory_space=pl.ANY))
        args.append(out_prev)
        aliases = {len(args) - 1: 0}
    return pl.pallas_call(
        _final_kernel,
        grid=(bsz, seq // tm),
        in_specs=in_specs,
        out_specs=pl.BlockSpec((1, tm, d), lambda b, i: (b + b0, i, 0)),
        out_shape=jax.ShapeDtypeStruct((n_batch, seq, d), F32),
        input_output_aliases=aliases,
        compiler_params=_cparams(("parallel", "parallel")),
        name="final_residual",
    )(*args)


def _stream(x, mod, mod_row, p, ctx_kv, init_state, rope, tm, b0, bsz):
    _, seq, d = x.shape
    is_ctx = ctx_kv is None
    pj = _project(x, mod, p["g_pre_mix"], p["w_in"], p["b_gates"], mod_row, rope, is_ctx, tm, b0, bsz)
    if is_ctx:
        att = _attention(pj["q"], None, None, pj["kd"], pj["vd"], p["sink"], window=False)
    else:
        att = _attention(pj["q"], pj["kd"], pj["vd"], ctx_kv[0], ctx_kv[1], p["sink"], window=True)
    ml = _mlstm(pj["mq"], pj["mk"], pj["mv"], pj["gates"], init_state, emit_state=is_ctx)
    x1, h2, idx, gate, tile_counts = _merge(att, ml[0], ml[1], pj, x, mod, mod_row, p, tm, b0)
    moe_in = (h2.reshape(bsz * seq, d // 2), idx, tile_counts)
    return x1, gate, moe_in, pj, ml


def kernel(x_prompt, x_sample, c, cache_k, cache_v, state_C, state_n, state_m, c_ctx, w_ada, b_ada, g_pre_mix,
           w_in, b_gates, attn_sink, norm_mlstm, w_up_att, w_up_ml, w_out, g_post_mix, g_pre_ffn, w_router,
           b_router, w1, b1, w2, b2, g_post_ffn):
    depth = w_ada.shape[0]
    n_dec = c.shape[0]
    cond = jnp.concatenate([c_ctx[None, :], c], axis=0).astype(F32)
    cond = jnp.pad(cond, ((0, 16 - cond.shape[0]), (0, 0)))
    y_prompt, y_sample = x_prompt, x_sample
    ks_, vs_, cs_, ns_, ms_ = [], [], [], [], []
    for l in range(depth):
        p = dict(g_pre_mix=g_pre_mix[l], w_in=w_in[l], b_gates=b_gates[l], sink=attn_sink[l], norm_ml=norm_mlstm[l],
                 w_up_att=w_up_att[l], w_up_ml=w_up_ml[l], w_out=w_out[l], g_post_mix=g_post_mix[l],
                 g_pre_ffn=g_pre_ffn[l], w_router=w_router[l], b_router=b_router[l], w1=w1[l], b1=b1[l],
                 w2=w2[l], b2=b2[l], g_post_ffn=g_post_ffn[l])
        mod = _adaln(cond, w_ada[l], b_ada[l]).reshape(16, 6, D_MODEL)
        row_p = lambda b: 0
        bsz, seq = x_prompt.shape[:2]
        x1_p, gate_p, moe_p, pj, ml = _stream(y_prompt, mod, row_p, p, None, None, False, 256, 0, bsz)
        nat = pj["nat"]
        ks_.append(nat[..., :KV_W].reshape(bsz, seq, ATT_KV_HEADS, HEAD_DIM))
        vs_.append(nat[..., KV_W:].reshape(bsz, seq, ATT_KV_HEADS, HEAD_DIM))
        cs_.append(ml[2])
        ns_.append(ml[3])
        ms_.append(ml[4][..., 0])
        ctx_kv = (_dup_heads(cache_k[:, l], False), _dup_heads(cache_v[:, l], True))
        init = (state_C[:, l], state_n[:, l], state_m[:, l])
        half = n_dec // 2
        experts = (p["w1"], p["b1"].astype(F32), p["w2"], p["b2"].astype(F32))
        groups = []
        for b0 in (0, half):
            row_s = functools.partial(lambda b, off: b + off + 1, off=b0)
            sl = slice(b0, b0 + half)
            x1_s, gate_s, moe_s, _, _ = _stream(y_sample, mod, row_s, p, (ctx_kv[0][sl], ctx_kv[1][sl]),
                                                tuple(t[sl] for t in init), True, 512, b0, half)
            groups.append((x1_s, gate_s, moe_s, row_s))
        (x1_a, gate_a, moe_a, row_a), (x1_b, gate_b, moe_b, row_b) = groups
        yg_p, yg_a = _moe([moe_p[0], moe_a[0]], [moe_p[1], moe_a[1]], [moe_p[2], moe_a[2]], *experts)
        (yg_b,) = _moe([moe_b[0]], [moe_b[1]], [moe_b[2]], *experts)
        y_prompt = _final(x1_p, yg_p, gate_p, mod, row_p, p["g_post_ffn"], 256, 0, bsz, None)
        y_sample = _final(x1_a, yg_a, gate_a, mod, row_a, p["g_post_ffn"], 512, 0, n_dec, None)
        y_sample = _final(x1_b, yg_b, gate_b, mod, row_b, p["g_post_ffn"], 512, half, n_dec, y_sample)
    return (y_prompt, y_sample, jnp.stack(ks_, axis=1), jnp.stack(vs_, axis=1), jnp.stack(cs_, axis=1),
            jnp.stack(ns_, axis=1), jnp.stack(ms_, axis=1))
```

```python
import functools

import numpy as np
import jax
import jax.numpy as jnp
from jax import lax
from jax.experimental import pallas as pl
from jax.experimental.pallas import tpu as pltpu
from jax.experimental.pallas import tpu_sc as plsc

F32 = jnp.float32
BF16 = jnp.bfloat16

D_MODEL = 1024
GRID_W = 64
ATT_HEADS = 8
ATT_KV_HEADS = 2
ATT_GROUP = ATT_HEADS // ATT_KV_HEADS
HEAD_DIM = 64
BLOCK = 128
ROPE_THETA = 10000.0
AXIS_FREQS = HEAD_DIM // 4
ML_HEADS = 4
ML_DK = 128
ML_DV = 256
ML_CHUNK = 128
N_EXPERTS = 32
TOP_K = 4
D_FF = 1024
SWIGLU_ALPHA = 1.702
SWIGLU_LIMIT = 7.0
EPS = 1e-6
NEG = -1e30

Q_W = ATT_HEADS * HEAD_DIM
KV_W = ATT_KV_HEADS * HEAD_DIM
MLQK_W = ML_HEADS * ML_DK
MLV_W = ML_HEADS * ML_DV
GATE_W = 4 * ML_HEADS

LANES = 128
VMEM_LIMIT = 56 * 1024 * 1024
MOE_ROWS = 512
SC_CORES = 2
SC_SUBCORES = 16
SC_GATHER_CHUNK = 32
ROUTE_TILE = 256


def _cparams(sem):
    return pltpu.CompilerParams(dimension_semantics=sem, vmem_limit_bytes=VMEM_LIMIT)


def _const_spec(shape):
    nd = len(shape)
    return pl.BlockSpec(shape, lambda *_: (0,) * nd)


def _adaln_kernel(cond_ref, w_ref, b_ref, o_ref):
    cnd = cond_ref[...]
    act = cnd * jax.nn.sigmoid(cnd)
    o_ref[...] = jnp.dot(act.astype(BF16), w_ref[...].astype(BF16),
                         preferred_element_type=F32) + b_ref[...]


def _adaln(cond, w_ada, b_ada):
    rows, d = cond.shape
    n = w_ada.shape[1]
    tn = 1536
    return pl.pallas_call(
        _adaln_kernel,
        grid=(n // tn,),
        in_specs=[_const_spec((rows, d)),
                  pl.BlockSpec((d, tn), lambda j: (0, j)),
                  pl.BlockSpec((1, tn), lambda j: (0, j))],
        out_specs=pl.BlockSpec((rows, tn), lambda j: (0, j)),
        out_shape=jax.ShapeDtypeStruct((rows, n), F32),
        compiler_params=_cparams(("parallel",)),
        name="adaln",
    )(cond, w_ada, b_ada.reshape(1, n))


def _pack_w_in(w_in, b_gates, rope, with_nat):
    o_q, o_k, o_v = 0, Q_W, Q_W + KV_W
    o_mq = Q_W + 2 * KV_W
    o_mk = o_mq + MLQK_W
    o_mv = o_mk + MLQK_W
    o_g = o_mv + MLV_W
    o_mo = o_g + GATE_W
    o_ga = o_mo + MLV_W
    o_gm = o_ga + D_MODEL
    half = HEAD_DIM // 2

    def head_cols(base, h, rot):
        lo = base + h * HEAD_DIM
        if rot:
            return [w_in[:, lo + half:lo + HEAD_DIM], w_in[:, lo:lo + half]]
        return [w_in[:, lo:lo + HEAD_DIM]]

    def q_cols(rot):
        cols = []
        for h in range(ATT_HEADS):
            cols += head_cols(o_q, h, rot)
        return cols

    def dup_cols(base, rot):
        cols = []
        for h in range(ATT_KV_HEADS):
            hc = head_cols(base, h, rot)
            cols += hc + hc
        return cols

    segs, cols, off = {}, [], 0

    def add(name, cl):
        nonlocal off
        width = sum(c.shape[1] for c in cl)
        segs[name] = (off, off + width)
        cols.extend(cl)
        off += width

    add("q", q_cols(False))
    if rope:
        add("qrot", q_cols(True))
    add("kd", dup_cols(o_k, False))
    if rope:
        add("kdrot", dup_cols(o_k, True))
    if with_nat:
        add("nat", [w_in[:, o_k:o_k + 2 * KV_W]])
    add("mq", [w_in[:, o_mq:o_mq + MLQK_W]])
    add("mv", [w_in[:, o_mv:o_mv + MLV_W]])
    add("mo", [w_in[:, o_mo:o_mo + MLV_W]])
    add("ga", [w_in[:, o_ga:o_ga + D_MODEL]])
    add("gm", [w_in[:, o_gm:o_gm + D_MODEL]])
    add("gates", [w_in[:, o_g:o_g + GATE_W], jnp.zeros((D_MODEL, LANES - GATE_W), w_in.dtype)])
    wp = jnp.concatenate(cols, axis=1).astype(BF16)
    wvd_t = jnp.concatenate(dup_cols(o_v, False) + [w_in[:, o_mk:o_mk + MLQK_W]], axis=1).T.astype(BF16)
    bg = jnp.pad(b_gates.astype(F32), (0, LANES - GATE_W)).reshape(1, LANES)
    return wp, wvd_t, bg, segs


def _rope_tables(n_tok):
    rows = n_tok // GRID_W
    row = np.repeat(np.arange(rows), GRID_W).astype(np.float32)
    col = np.tile(np.arange(GRID_W), rows).astype(np.float32)
    inv = (np.float32(ROPE_THETA) ** (-np.arange(AXIS_FREQS, dtype=np.float32) / AXIS_FREQS)).astype(np.float32)
    ang = np.concatenate([row[:, None] * inv, col[:, None] * inv], axis=-1).astype(np.float32)
    cos, sin = np.cos(ang), np.sin(ang)
    c64 = np.concatenate([cos, cos], axis=-1)
    s64 = np.concatenate([-sin, sin], axis=-1)
    return (jnp.asarray(np.tile(c64, (1, 2)), F32), jnp.asarray(np.tile(s64, (1, 2)), F32))


def _proj_kernel(*refs, segs, rope, with_nat):
    it = iter(refs)
    x_ref, mod_ref, g_ref, w_ref, wvt_ref, bg_ref = (next(it) for _ in range(6))
    cos_ref = sin_ref = None
    if rope:
        cos_ref, sin_ref = next(it), next(it)
    q_ref, kd_ref, vd_ref = next(it), next(it), next(it)
    nat_ref = next(it) if with_nat else None
    mq_ref, mk_ref, mv_ref, gates_ref, mo_ref, ga_ref, gm_ref = (next(it) for _ in range(7))

    x = x_ref[0]
    shift = mod_ref[0, 0:1, :]
    scale = mod_ref[0, 1:2, :]
    ms = jnp.mean(x * x, axis=-1, keepdims=True)
    xn = x * lax.rsqrt(ms + EPS) * g_ref[...]
    xb = (xn * (1.0 + scale) + shift).astype(BF16)

    def seg(name):
        lo, hi = segs[name]
        return jnp.dot(xb, w_ref[:, lo:hi], preferred_element_type=F32)

    uq = seg("q")
    ukd = seg("kd")
    if rope:
        uqr = seg("qrot")
        ukr = seg("kdrot")
        cs = cos_ref[...]
        sn = sin_ref[...]
        for h in range(ATT_HEADS // 2):
            sl = slice(h * LANES, (h + 1) * LANES)
            q_ref[0, h] = ((uq[:, sl] * cs + uqr[:, sl] * sn) * HEAD_DIM ** -0.5).astype(BF16)
        for h in range(ATT_KV_HEADS):
            sl = slice(h * LANES, (h + 1) * LANES)
            kd_ref[0, :, sl] = (ukd[:, sl] * cs + ukr[:, sl] * sn).astype(BF16)
    else:
        for h in range(ATT_HEADS // 2):
            sl = slice(h * LANES, (h + 1) * LANES)
            q_ref[0, h] = (uq[:, sl] * HEAD_DIM ** -0.5).astype(BF16)
        kd_ref[0] = ukd.astype(BF16)
    ut = lax.dot_general(wvt_ref[...], xb, (((1,), (1,)), ((), ())), preferred_element_type=F32)
    vd_ref[0] = ut[:2 * LANES].astype(BF16)
    mk_ref[0] = ut[2 * LANES:].astype(BF16)
    if with_nat:
        nat_ref[0] = seg("nat")
    mq_ref[0] = (seg("mq") * ML_DK ** -0.5).astype(BF16)
    mv_ref[0] = seg("mv").astype(BF16)
    gates_ref[0] = seg("gates") + bg_ref[...]
    mo_ref[0] = seg("mo")
    ga_ref[0] = seg("ga")
    gm_ref[0] = seg("gm")


def _project(x, mod, g_pre, w_in, b_gates, mod_row, rope, with_nat, tm, b0, bsz):
    _, seq, d = x.shape
    wp, wvd_t, bg, segs = _pack_w_in(w_in, b_gates, rope, with_nat)
    nw = wp.shape[1]
    tok3 = lambda b, i: (b, i, 0)
    in_specs = [pl.BlockSpec((1, tm, d), lambda b, i: (b + b0, i, 0)),
                pl.BlockSpec((1, 6, d), lambda b, i: (mod_row(b), 0, 0)),
                _const_spec((1, d)),
                pl.BlockSpec((d, nw), lambda b, i: (0, 0), pipeline_mode=pl.Buffered(1)),
                _const_spec((2 * LANES + MLQK_W, d)), _const_spec((1, LANES))]
    args = [x, mod, g_pre.reshape(1, d), wp, wvd_t, bg]
    if rope:
        cos_t, sin_t = _rope_tables(seq)
        in_specs += [pl.BlockSpec((tm, LANES), lambda b, i: (i, 0))] * 2
        args += [cos_t, sin_t]

    def tok_out(width, dtype):
        return (jax.ShapeDtypeStruct((bsz, seq, width), dtype), pl.BlockSpec((1, tm, width), tok3))

    outs = [(jax.ShapeDtypeStruct((bsz, ATT_HEADS // 2, seq, LANES), BF16),
             pl.BlockSpec((1, ATT_HEADS // 2, tm, LANES), lambda b, i: (b, 0, i, 0))),
            tok_out(2 * LANES, BF16),
            (jax.ShapeDtypeStruct((bsz, 2 * LANES, seq), BF16),
             pl.BlockSpec((1, 2 * LANES, tm), lambda b, i: (b, 0, i)))]
    if with_nat:
        outs.append(tok_out(2 * KV_W, F32))
    outs += [tok_out(MLQK_W, BF16),
             (jax.ShapeDtypeStruct((bsz, MLQK_W, seq), BF16),
              pl.BlockSpec((1, MLQK_W, tm), lambda b, i: (b, 0, i))),
             tok_out(MLV_W, BF16), tok_out(LANES, F32),
             tok_out(MLV_W, F32), tok_out(D_MODEL, F32), tok_out(D_MODEL, F32)]
    res = pl.pallas_call(
        functools.partial(_proj_kernel, segs=segs, rope=rope, with_nat=with_nat),
        grid=(bsz, seq // tm),
        in_specs=in_specs,
        out_specs=[o[1] for o in outs],
        out_shape=[o[0] for o in outs],
        compiler_params=_cparams(("parallel", "parallel")),
        name="proj_rope" if rope else "proj_ctx",
    )(*args)
    names = ["q", "kd", "vd"] + (["nat"] if with_nat else []) + ["mq", "mk", "mv", "gates", "mo", "ga", "gm"]
    return dict(zip(names, res))


def _attn_kernel(*refs, window, n_blocks, n_ctx):
    it = iter(refs)
    sink_ref, q_ref = next(it), next(it)
    if window:
        kp_ref, kc_ref, kn_ref, vp_ref, vc_ref, vn_ref = (next(it) for _ in range(6))
    ck_ref, cv_ref, o_ref = next(it), next(it), next(it)
    j = pl.program_id(1)
    n_win = 3 * BLOCK if window else 0
    cols = ATT_GROUP * BLOCK

    col_id = lax.broadcasted_iota(jnp.int32, (1, cols), 1)
    if window:
        kl = lax.broadcasted_iota(jnp.int32, (BLOCK, cols), 0)
        ql = lax.broadcasted_iota(jnp.int32, (BLOCK, cols), 1) & (BLOCK - 1)
        prev_ok = kl >= ql + jnp.where(j == 0, BLOCK, 0)
        next_ok = kl <= ql - jnp.where(j == n_blocks - 1, BLOCK, 0)
    lane_lo = lax.broadcasted_iota(jnp.int32, (BLOCK, LANES), 1) < HEAD_DIM
    lane_row = lax.broadcasted_iota(jnp.int32, (1, LANES), 1) < HEAD_DIM
    half_lo = jnp.where(lane_row, 1.0, 0.0).astype(BF16)
    half_hi = jnp.where(lane_row, 0.0, 1.0).astype(BF16)

    for kvh in range(ATT_KV_HEADS):
        sl = slice(kvh * LANES, (kvh + 1) * LANES)
        if window:
            keys = jnp.concatenate([kp_ref[0, :, sl], kc_ref[0, :, sl], kn_ref[0, :, sl], ck_ref[0, :, sl]], axis=0)
            vals_t = jnp.concatenate([vp_ref[0, sl, :], vc_ref[0, sl, :], vn_ref[0, sl, :], cv_ref[0, sl, :]], axis=1)
        else:
            keys = ck_ref[0, :, sl]
            vals_t = cv_ref[0, sl, :]
        qs = jnp.concatenate([q_ref[0, kvh * (ATT_GROUP // 2) + g // 2] * (half_lo if g % 2 == 0 else half_hi)
                              for g in range(ATT_GROUP)], axis=0)
        s = lax.dot_general(keys, qs, (((1,), (1,)), ((), ())), preferred_element_type=F32)
        if window:
            s = jnp.concatenate([jnp.where(prev_ok, s[:BLOCK], NEG), s[BLOCK:2 * BLOCK],
                                 jnp.where(next_ok, s[2 * BLOCK:n_win], NEG), s[n_win:]], axis=0)
        snk = jnp.full((1, cols), sink_ref[kvh * ATT_GROUP + ATT_GROUP - 1], F32)
        for g in range(ATT_GROUP - 2, -1, -1):
            snk = jnp.where(col_id < (g + 1) * BLOCK, sink_ref[kvh * ATT_GROUP + g], snk)
        m = jnp.maximum(jnp.max(s, axis=0, keepdims=True), snk)
        p = jnp.exp(s - m)
        den = jnp.sum(p, axis=0, keepdims=True) + jnp.exp(snk - m)
        o_t = jnp.dot(vals_t, p.astype(BF16), preferred_element_type=F32) / den
        for pair in range(ATT_GROUP // 2):
            even = o_t[:, (2 * pair) * BLOCK:(2 * pair + 1) * BLOCK].T
            odd = o_t[:, (2 * pair + 1) * BLOCK:(2 * pair + 2) * BLOCK].T
            col = (kvh * (ATT_GROUP // 2) + pair) * LANES
            o_ref[0, :, col:col + LANES] = jnp.where(lane_lo, even, odd).astype(BF16)


def _attention(q, kd, vd_t, ckd, cvd_t, sink, window):
    bsz, _, seq, _ = q.shape
    nb = seq // BLOCK
    n_ctx = ckd.shape[1]
    in_specs = [pl.BlockSpec(memory_space=pltpu.SMEM),
                pl.BlockSpec((1, ATT_HEADS // 2, BLOCK, LANES), lambda b, j: (b, 0, j, 0))]
    args = [sink.astype(F32), q]
    if window:
        in_specs += [pl.BlockSpec((1, BLOCK, 2 * LANES), lambda b, j: (b, jnp.maximum(j - 1, 0), 0)),
                     pl.BlockSpec((1, BLOCK, 2 * LANES), lambda b, j: (b, j, 0)),
                     pl.BlockSpec((1, BLOCK, 2 * LANES), lambda b, j: (b, jnp.minimum(j + 1, nb - 1), 0)),
                     pl.BlockSpec((1, 2 * LANES, BLOCK), lambda b, j: (b, 0, jnp.maximum(j - 1, 0))),
                     pl.BlockSpec((1, 2 * LANES, BLOCK), lambda b, j: (b, 0, j)),
                     pl.BlockSpec((1, 2 * LANES, BLOCK), lambda b, j: (b, 0, jnp.minimum(j + 1, nb - 1)))]
        args += [kd, kd, kd, vd_t, vd_t, vd_t]
    in_specs += [pl.BlockSpec((1, n_ctx, 2 * LANES), lambda b, j: (b, 0, 0)),
                 pl.BlockSpec((1, 2 * LANES, n_ctx), lambda b, j: (b, 0, 0))]
    args += [ckd, cvd_t]
    return pl.pallas_call(
        functools.partial(_attn_kernel, window=window, n_blocks=nb, n_ctx=n_ctx),
        grid=(bsz, nb),
        in_specs=in_specs,
        out_specs=pl.BlockSpec((1, BLOCK, Q_W), lambda b, j: (b, j, 0)),
        out_shape=jax.ShapeDtypeStruct((bsz, seq, Q_W), BF16),
        compiler_params=_cparams(("parallel", "parallel")),
        name="attn_window" if window else "attn_ctx",
    )(*args)


def _dup_heads(t, transpose):
    b, l = t.shape[:2]
    out = jnp.concatenate([t, t], axis=-1).reshape(b, l, ATT_KV_HEADS * LANES).astype(BF16)
    return jnp.swapaxes(out, 1, 2) if transpose else out


def _split3(x):
    hi = x.astype(BF16)
    r1 = x - hi.astype(F32)
    mid = r1.astype(BF16)
    lo = (r1 - mid.astype(F32)).astype(BF16)
    return hi, mid, lo


def _log_sigmoid(x):
    return jnp.minimum(x, 0.0) - jnp.log1p(jnp.exp(-jnp.abs(x)))


def _mlstm_kernel(*refs, n_chunks, has_init, emit_state):
    it = iter(refs)
    qkvg = [[next(it) for _ in range(4)] for _ in range(2)]
    if has_init:
        c0_ref, n0_ref, m0_ref = next(it), next(it), next(it)
    h_refs = [next(it), next(it)]
    if emit_state:
        co_ref, no_ref, mo_ref = next(it), next(it), next(it)
    st_s, m_s = next(it), next(it)
    t = pl.program_id(1)
    L = ML_CHUNK

    @pl.when(t == 0)
    def _():
        for d in range(2):
            for h in range(ML_HEADS):
                if has_init:
                    st_s[d, h, :, :ML_DV] = c0_ref[0, d, h].T
                    st_s[d, h, :, ML_DV:] = jnp.broadcast_to(n0_ref[0, d, h:h + 1, :], (LANES, ML_DK)).T
                    m_s[d, h] = m0_ref[0, d, h:h + 1, :]
                else:
                    st_s[d, h] = jnp.zeros((ML_DK, ML_DV + LANES), F32)
                    m_s[d, h] = jnp.zeros((1, LANES), F32)

    ri = lax.broadcasted_iota(jnp.int32, (L, L), 0)
    ci = lax.broadcasted_iota(jnp.int32, (L, L), 1)
    ones_blk = jnp.ones((L, LANES), BF16)

    for d in range(2):
        q_ref, kt_ref, v_ref, g_ref = qkvg[d]
        tri = (ci <= ri) if d == 0 else (ci >= ri)
        tri_b = jnp.where(tri, 1.0, 0.0).astype(BF16)
        g = g_ref[0]
        lf = _log_sigmoid(g)
        g_t = g.T
        lf_t = lf.T
        bc = sum(jnp.dot(tri_b, part, preferred_element_type=F32) for part in _split3(lf))
        br = sum(lax.dot_general(part, tri_b, (((1,), (1,)), ((), ())), preferred_element_type=F32)
                 for part in _split3(lf_t))
        tot = jnp.sum(lf, axis=0, keepdims=True)

        for h in range(ML_HEADS):
            icol_i = 2 * ML_HEADS * d + h
            fcol = icol_i + ML_HEADS
            bcol = bc[:, fcol:fcol + 1]
            brow = br[fcol:fcol + 1, :]
            irow = g_t[icol_i:icol_i + 1, :]
            b_last = tot[:, fcol:fcol + 1]
            m_prev = m_s[d, h][:, 0:1]
            st_prev = st_s[d, h]
            qh = q_ref[0, :, h * ML_DK:(h + 1) * ML_DK]
            kt = kt_ref[0, h * ML_DK:(h + 1) * ML_DK, :]
            v_aug = jnp.concatenate([v_ref[0, :, h * ML_DV:(h + 1) * ML_DV], ones_blk], axis=1)

            rel = jnp.where(tri, irow - brow, NEG)
            inter = bcol + m_prev
            m_t = jnp.maximum(inter, bcol + jnp.max(rel, axis=-1, keepdims=True))
            w = jnp.exp(rel + (bcol - m_t))
            w_inter = jnp.exp(inter - m_t)
            s = jnp.dot(qh, kt, preferred_element_type=F32) * w
            mix = (jnp.dot(s.astype(BF16), v_aug, preferred_element_type=F32)
                   + jnp.dot(qh, st_prev.astype(BF16), preferred_element_type=F32) * w_inter)
            inv = 1.0 / jnp.maximum(jnp.abs(mix[:, ML_DV:]), jnp.exp(-m_t))
            for c in range(ML_DV // LANES):
                lo = h * ML_DV + c * LANES
                h_refs[d][0, :, lo:lo + LANES] = mix[:, c * LANES:(c + 1) * LANES] * inv

            a = b_last - brow + irow
            m_new = jnp.maximum(b_last + m_prev, jnp.max(a, axis=-1, keepdims=True))
            wk = jnp.exp(a - m_new)
            decay = jnp.exp(b_last + m_prev - m_new)
            kw_t = (kt.astype(F32) * wk).astype(BF16)
            st_new = decay * st_prev + jnp.dot(kw_t, v_aug, preferred_element_type=F32)
            st_s[d, h] = st_new
            m_s[d, h] = jnp.broadcast_to(m_new, (1, LANES))
            if emit_state:
                @pl.when(t == n_chunks - 1)
                def _():
                    co_ref[0, d, h] = st_new[:, :ML_DV].T
                    no_ref[0, d, h:h + 1, :] = st_new[:, ML_DV:].T[0:1, :]
                    mo_ref[0, d, h:h + 1, :] = jnp.broadcast_to(m_new, (1, LANES))


def _mlstm(mq, mk_t, mv, gates, init, emit_state):
    bsz, seq, _ = mq.shape
    nc = seq // ML_CHUNK
    fwd = lambda b, t: (b, t, 0)
    bwd = lambda b, t: (b, nc - 1 - t, 0)
    fwd_t = lambda b, t: (b, 0, t)
    bwd_t = lambda b, t: (b, 0, nc - 1 - t)
    state5 = lambda b, t: (b, 0, 0, 0, 0)
    state4 = lambda b, t: (b, 0, 0, 0)
    in_specs, args = [], []
    for chunk, chunk_t in ((fwd, fwd_t), (bwd, bwd_t)):
        in_specs += [pl.BlockSpec((1, ML_CHUNK, MLQK_W), chunk), pl.BlockSpec((1, MLQK_W, ML_CHUNK), chunk_t),
                     pl.BlockSpec((1, ML_CHUNK, MLV_W), chunk), pl.BlockSpec((1, ML_CHUNK, LANES), chunk)]
        args += [mq, mk_t, mv, gates]
    if init is not None:
        c0, n0, m0 = init
        in_specs += [pl.BlockSpec((1, 2, ML_HEADS, ML_DV, ML_DK), state5),
                     pl.BlockSpec((1, 2, ML_HEADS, ML_DK), state4),
                     pl.BlockSpec((1, 2, ML_HEADS, LANES), state4)]
        args += [c0.astype(F32), n0.astype(F32),
                 jnp.broadcast_to(m0.astype(F32)[..., None], m0.shape + (LANES,))]
    out_shape = [jax.ShapeDtypeStruct((bsz, seq, MLV_W), F32)] * 2
    out_specs = [pl.BlockSpec((1, ML_CHUNK, MLV_W), fwd), pl.BlockSpec((1, ML_CHUNK, MLV_W), bwd)]
    if emit_state:
        out_shape += [jax.ShapeDtypeStruct((bsz, 2, ML_HEADS, ML_DV, ML_DK), F32),
                      jax.ShapeDtypeStruct((bsz, 2, ML_HEADS, ML_DK), F32),
                      jax.ShapeDtypeStruct((bsz, 2, ML_HEADS, LANES), F32)]
        out_specs += [pl.BlockSpec((1, 2, ML_HEADS, ML_DV, ML_DK), state5),
                      pl.BlockSpec((1, 2, ML_HEADS, ML_DK), state4),
                      pl.BlockSpec((1, 2, ML_HEADS, LANES), state4)]
    return pl.pallas_call(
        functools.partial(_mlstm_kernel, n_chunks=nc, has_init=init is not None, emit_state=emit_state),
        grid=(bsz, nc),
        in_specs=in_specs,
        out_specs=out_specs,
        out_shape=out_shape,
        scratch_shapes=[pltpu.VMEM((2, ML_HEADS, ML_DK, ML_DV + LANES), F32),
                        pltpu.VMEM((2, ML_HEADS, 1, LANES), F32)],
        compiler_params=_cparams(("parallel", "arbitrary")),
        name="mlstm_state" if emit_state else "mlstm",
    )(*args)


def _sigmoid(x):
    return 0.5 * jnp.tanh(0.5 * x) + 0.5


def _rms(x, g):
    return x * lax.rsqrt(jnp.mean(x * x, axis=-1, keepdims=True) + EPS) * g


def _pack_bf16_pairs(x):
    n = x.shape[1] // 2
    lo = pltpu.bitcast(x[:, :n].astype(BF16).astype(F32), jnp.int32)
    hi = pltpu.bitcast(x[:, n:].astype(BF16).astype(F32), jnp.int32)
    return lax.shift_right_logical(lo, 16) | hi


def _unpack_bf16_pairs(p):
    lo = pltpu.bitcast(lax.shift_left(p, 16), F32)
    hi = pltpu.bitcast(p & jnp.int32(-65536), F32)
    return lo, hi


def _merge_kernel(att_ref, hf_ref, hb_ref, mo_ref, ga_ref, gm_ref, x_ref, mod_ref, nml_ref, gpm_ref, gpf_ref,
                  wua_ref, wum_ref, wo_ref, wr_ref, br_ref, x1_ref, h2_ref, idx_ref, gate_ref, cnt_ref):
    hsum = hf_ref[0] + hb_ref[0]
    parts = []
    for h in range(ML_HEADS):
        hh = hsum[:, h * ML_DV:(h + 1) * ML_DV]
        parts.append(hh * lax.rsqrt(jnp.mean(hh * hh, axis=-1, keepdims=True) + EPS))
    hn = jnp.concatenate(parts, axis=-1) * nml_ref[...]
    ml = (hn * _sigmoid(mo_ref[0])).astype(BF16)
    a = jnp.dot(att_ref[0], wua_ref[...], preferred_element_type=F32)
    m = jnp.dot(ml, wum_ref[...], preferred_element_type=F32)
    z = (_sigmoid(ga_ref[0]) * a + _sigmoid(gm_ref[0]) * m).astype(BF16)
    mix = jnp.dot(z, wo_ref[...], preferred_element_type=F32)
    gate1 = mod_ref[0, 2:3, :]
    shift2 = mod_ref[0, 3:4, :]
    scale2 = mod_ref[0, 4:5, :]
    x1 = x_ref[0] + gate1 * _rms(mix, gpm_ref[...])
    x1_ref[0] = x1
    h2 = _rms(x1, gpf_ref[...]) * (1.0 + scale2) + shift2
    h2_ref[0] = _pack_bf16_pairs(h2)

    h2_hi = h2.astype(BF16)
    h2_lo = (h2 - h2_hi.astype(F32)).astype(BF16)
    wr = wr_ref[...]
    wr_hi = wr.astype(BF16)
    wr_lo = (wr - wr_hi.astype(F32)).astype(BF16)
    logits = (jnp.dot(h2_hi, wr_hi, preferred_element_type=F32)
              + jnp.dot(h2_lo, wr_hi, preferred_element_type=F32)
              + jnp.dot(h2_hi, wr_lo, preferred_element_type=F32)) + br_ref[...]
    lane = lax.broadcasted_iota(jnp.int32, logits.shape, 1).astype(F32)
    work = jnp.where(lane < N_EXPERTS, logits, -jnp.inf)
    idx_out = jnp.zeros(logits.shape, F32)
    val_out = jnp.zeros(logits.shape, F32)
    picked = jnp.zeros(logits.shape, F32)
    top0 = None
    esum = None
    for k in range(TOP_K):
        mx = jnp.max(work, axis=-1, keepdims=True)
        sel = jnp.min(jnp.where(work == mx, lane, float(LANES)), axis=-1, keepdims=True)
        if k == 0:
            top0 = mx
        e = jnp.exp(mx - top0)
        esum = e if k == 0 else esum + e
        idx_out = jnp.where(lane == k, sel, idx_out)
        val_out = jnp.where(lane == k, e, val_out)
        picked = jnp.where(lane == sel, 1.0, picked)
        work = jnp.where(lane == sel, -jnp.inf, work)
    idx_ref[0] = idx_out.astype(jnp.int32)
    gate_ref[0] = val_out / esum
    for r in range(logits.shape[0] // ROUTE_TILE):
        cnt_ref[0, r:r + 1, :] = jnp.sum(picked[r * ROUTE_TILE:(r + 1) * ROUTE_TILE], axis=0, keepdims=True)


def _merge(att, hf, hb, pj, x, mod, mod_row, p, tm, b0):
    bsz = att.shape[0]
    _, seq, d = x.shape
    tok3 = lambda b, i: (b, i, 0)
    wr = jnp.pad(p["w_router"].astype(F32), ((0, 0), (0, LANES - N_EXPERTS)))
    br = jnp.pad(p["b_router"].astype(F32), (0, LANES - N_EXPERTS)).reshape(1, LANES)
    row = lambda v: v.astype(F32).reshape(1, -1)
    in_specs = [pl.BlockSpec((1, tm, Q_W), tok3),
                pl.BlockSpec((1, tm, MLV_W), tok3), pl.BlockSpec((1, tm, MLV_W), tok3),
                pl.BlockSpec((1, tm, MLV_W), tok3), pl.BlockSpec((1, tm, d), tok3), pl.BlockSpec((1, tm, d), tok3),
                pl.BlockSpec((1, tm, d), lambda b, i: (b + b0, i, 0)),
                pl.BlockSpec((1, 6, d), lambda b, i: (mod_row(b), 0, 0)),
                _const_spec((1, MLV_W)), _const_spec((1, d)), _const_spec((1, d)),
                _const_spec((Q_W, d)), _const_spec((MLV_W, d)), _const_spec((d, d)),
                _const_spec((d, LANES)), _const_spec((1, LANES))]
    out_shape = [jax.ShapeDtypeStruct((bsz, seq, d), F32), jax.ShapeDtypeStruct((bsz, seq, d // 2), jnp.int32),
                 jax.ShapeDtypeStruct((bsz, seq, LANES), jnp.int32), jax.ShapeDtypeStruct((bsz, seq, LANES), F32),
                 jax.ShapeDtypeStruct((bsz * (seq // tm), tm // ROUTE_TILE, LANES), F32)]
    out_specs = [pl.BlockSpec((1, tm, d), tok3), pl.BlockSpec((1, tm, d // 2), tok3),
                 pl.BlockSpec((1, tm, LANES), tok3), pl.BlockSpec((1, tm, LANES), tok3),
                 pl.BlockSpec((1, tm // ROUTE_TILE, LANES), lambda b, i: (b * (seq // tm) + i, 0, 0))]
    return pl.pallas_call(
        _merge_kernel,
        grid=(bsz, seq // tm),
        in_specs=in_specs,
        out_specs=out_specs,
        out_shape=out_shape,
        compiler_params=_cparams(("parallel", "parallel")),
        name="merge_router",
    )(att, hf, hb, pj["mo"], pj["ga"], pj["gm"], x, mod, row(p["norm_ml"]), row(p["g_post_mix"]),
      row(p["g_pre_ffn"]), p["w_up_att"].astype(BF16), p["w_up_ml"].astype(BF16), p["w_out"].astype(BF16), wr, br)


def _sc_gather_rows(table, idx):
    n = idx.shape[0]
    width = table.shape[1]
    chunk = SC_GATHER_CHUNK
    n_workers = SC_CORES * SC_SUBCORES
    rows_per_worker = n // n_workers
    n_pairs = rows_per_worker // (2 * chunk)
    assert n_pairs * 2 * chunk * n_workers == n
    mesh = plsc.VectorSubcoreMesh(core_axis_name="c", subcore_axis_name="s")

    def body(table_hbm, idx_hbm, out_hbm, idx_v, rows_v, gsem, osem):
        wid = lax.axis_index("s") * SC_CORES + lax.axis_index("c")
        base0 = wid * rows_per_worker

        def gather_copy(b):
            return pltpu.make_async_copy(table_hbm.at[idx_v.at[b]], rows_v.at[b], gsem.at[b])

        def write_copy(ci, b):
            base = pl.multiple_of(base0 + ci * chunk, 8)
            return pltpu.make_async_copy(rows_v.at[b], out_hbm.at[pl.ds(base, chunk)], osem.at[b])

        def issue(ci, b):
            base = pl.multiple_of(base0 + ci * chunk, 8)
            pltpu.sync_copy(idx_hbm.at[pl.ds(base, chunk)], idx_v.at[b])
            gather_copy(b).start()

        def finish(ci, b):
            gather_copy(b).wait()
            write_copy(ci, b).start()

        issue(0, 0)

        @pl.loop(0, n_pairs)
        def _(j):
            @pl.when(j > 0)
            def _():
                write_copy(2 * j - 1, 1).wait()
            issue(2 * j + 1, 1)
            finish(2 * j, 0)

            @pl.when(j < n_pairs - 1)
            def _():
                write_copy(2 * j, 0).wait()
                issue(2 * j + 2, 0)
            finish(2 * j + 1, 1)

        write_copy(2 * n_pairs - 2, 0).wait()
        write_copy(2 * n_pairs - 1, 1).wait()

    return pl.kernel(
        body, mesh=mesh,
        out_type=jax.ShapeDtypeStruct((n, width), table.dtype),
        scratch_types=[pltpu.VMEM((2, chunk), jnp.int32), pltpu.VMEM((2, chunk, width), table.dtype),
                       pltpu.SemaphoreType.DMA((2,)), pltpu.SemaphoreType.DMA((2,))],
    )(table, idx)


def _ffn_kernel(be_ref, nu_ref, xs_ref, w1_ref, b1_ref, w2_ref, b2_ref, ys_ref, w1b_s, w2b_s):
    i = pl.program_id(0)
    e = be_ref[i]
    e_prev = be_ref[jnp.maximum(i - 1, 0)]

    @pl.when((i == 0) | (e != e_prev))
    def _():
        w1b_s[...] = w1_ref[0].astype(BF16)
        w2b_s[...] = w2_ref[0].astype(BF16)

    @pl.when(i < nu_ref[0])
    def _():
        half = D_MODEL // 2
        x_lo, x_hi = _unpack_bf16_pairs(xs_ref[...])
        hmid = (jnp.dot(x_lo.astype(BF16), w1b_s[:half, :], preferred_element_type=F32)
                + jnp.dot(x_hi.astype(BF16), w1b_s[half:, :], preferred_element_type=F32)) + b1_ref[0]
        glu = jnp.minimum(hmid[:, :D_FF], SWIGLU_LIMIT)
        lin = jnp.clip(hmid[:, D_FF:], -SWIGLU_LIMIT, SWIGLU_LIMIT)
        act = ((lin + 1.0) * glu * _sigmoid(SWIGLU_ALPHA * glu)).astype(BF16)
        y = jnp.dot(act, w2b_s[...], preferred_element_type=F32) + b2_ref[0]
        ys_ref[...] = _pack_bf16_pairs(y)

    @pl.when(i >= nu_ref[0])
    def _():
        ys_ref[...] = jnp.zeros(ys_ref.shape, jnp.int32)


def _sc_scatter_rows(tables, positions, n_rows):
    width = tables[0].shape[1]
    chunk = SC_GATHER_CHUNK
    n_workers = SC_CORES * SC_SUBCORES
    n_streams = len(tables)
    mesh = plsc.VectorSubcoreMesh(core_axis_name="c", subcore_axis_name="s")

    def body(*refs):
        table_refs = refs[0:2 * n_streams:2]
        pos_refs = refs[1:2 * n_streams:2]
        out_hbm, idx_v, rows_v, lsem, ssem = refs[2 * n_streams:]
        wid = lax.axis_index("s") * SC_CORES + lax.axis_index("c")

        for table_hbm, pos_hbm in zip(table_refs, pos_refs):
            tok_per_worker = table_hbm.shape[0] // n_workers
            n_pairs = tok_per_worker // (2 * chunk)
            assert n_pairs * 2 * chunk * n_workers == table_hbm.shape[0]
            base0 = wid * tok_per_worker

            def load_copy(ci, b):
                base = pl.multiple_of(base0 + ci * chunk, 8)
                return pltpu.make_async_copy(table_hbm.at[pl.ds(base, chunk)], rows_v.at[b], lsem.at[b])

            def scatter_copy(b, k):
                return pltpu.make_async_copy(rows_v.at[b], out_hbm.at[idx_v.at[b, k]], ssem.at[b])

            def load(ci, b):
                base = pl.multiple_of(base0 + ci * chunk, 8)
                load_copy(ci, b).start()
                for k in range(TOP_K):
                    pltpu.sync_copy(pos_hbm.at[k, pl.ds(base, chunk)], idx_v.at[b, k])

            def scatter(ci, b):
                load_copy(ci, b).wait()
                for k in range(TOP_K):
                    scatter_copy(b, k).start()

            def drain(b):
                for k in range(TOP_K):
                    scatter_copy(b, k).wait()

            load(0, 0)

            @pl.loop(0, n_pairs)
            def _(j):
                @pl.when(j > 0)
                def _():
                    drain(1)
                load(2 * j + 1, 1)
                scatter(2 * j, 0)

                @pl.when(j < n_pairs - 1)
                def _():
                    drain(0)
                    load(2 * j + 2, 0)
                scatter(2 * j + 1, 1)

            drain(0)
            drain(1)

    args = [a for pair in zip(tables, positions) for a in pair]
    return pl.kernel(
        body, mesh=mesh,
        out_type=jax.ShapeDtypeStruct((n_rows, width), tables[0].dtype),
        scratch_types=[pltpu.VMEM((2, TOP_K, chunk), jnp.int32), pltpu.VMEM((2, chunk, width), tables[0].dtype),
                       pltpu.SemaphoreType.DMA((2,)), pltpu.SemaphoreType.DMA((2,))],
    )(*args)


def _pos_kernel(idx_ref, base_ref, pos_ref):
    idx = idx_ref[...]
    rows = idx.shape[0]
    lane = lax.broadcasted_iota(jnp.int32, idx.shape, 1)
    hots = [jnp.where(lane == idx[:, k:k + 1], 1.0, 0.0) for k in range(TOP_K)]
    cnt = hots[0] + hots[1] + hots[2] + hots[3]
    ri = lax.broadcasted_iota(jnp.int32, (rows, rows), 0)
    ci = lax.broadcasted_iota(jnp.int32, (rows, rows), 1)
    earlier = jnp.where(ci < ri, 1.0, 0.0).astype(BF16)
    ahead = jnp.dot(earlier, cnt.astype(BF16), preferred_element_type=F32) + base_ref[0]
    posmat = jnp.zeros(idx.shape, F32)
    for k in range(TOP_K):
        posmat = jnp.where(lane == k, jnp.sum(hots[k] * ahead, axis=-1, keepdims=True), posmat)
    pos_ref[...] = posmat.T[:TOP_K, :].astype(jnp.int32)


def _route(idx_list, count_list):
    tiles = [c.shape[0] * c.shape[1] for c in count_list]
    n_tok = sum(tiles) * ROUTE_TILE
    n_blocks = n_tok * TOP_K // MOE_ROWS + N_EXPERTS
    cnt = jnp.concatenate([c.reshape(-1, LANES) for c in count_list], axis=0).astype(jnp.int32)
    tile_off = jnp.cumsum(cnt, axis=0) - cnt
    total = jnp.sum(cnt, axis=0)
    padded = (total + MOE_ROWS - 1) // MOE_ROWS * MOE_ROWS
    pend = jnp.cumsum(padded)
    base = ((pend - padded)[None, :] + tile_off).astype(F32)
    starts = jnp.arange(n_blocks, dtype=jnp.int32) * MOE_ROWS
    block_e = jnp.minimum(jnp.sum(pend[None, :N_EXPERTS] <= starts[:, None], axis=1), N_EXPERTS - 1).astype(jnp.int32)
    n_used = (pend[N_EXPERTS - 1] // MOE_ROWS).astype(jnp.int32).reshape(1)
    positions, first = [], 0
    for idx, n_tiles in zip(idx_list, tiles):
        stream_tok = n_tiles * ROUTE_TILE
        positions.append(pl.pallas_call(
            _pos_kernel,
            grid=(n_tiles,),
            in_specs=[pl.BlockSpec((ROUTE_TILE, LANES), lambda i: (i, 0)),
                      pl.BlockSpec((1, 1, LANES), lambda i: (i, 0, 0))],
            out_specs=pl.BlockSpec((TOP_K, ROUTE_TILE), lambda i: (0, i)),
            out_shape=jax.ShapeDtypeStruct((TOP_K, stream_tok), jnp.int32),
            compiler_params=_cparams(("parallel",)),
            name="route_pos",
        )(idx.reshape(stream_tok, LANES), base[first:first + n_tiles].reshape(n_tiles, 1, LANES)))
        first += n_tiles
    return block_e, n_used, positions


def _moe(h2p_list, idx_list, count_list, w1, b1, w2, b2):
    half = h2p_list[0].shape[1]
    d = 2 * half
    block_e, n_used, positions = _route(idx_list, count_list)
    n_blocks = block_e.shape[0]
    xs = _sc_scatter_rows(h2p_list, positions, n_blocks * MOE_ROWS)
    grid_spec = pltpu.PrefetchScalarGridSpec(
        num_scalar_prefetch=2,
        grid=(n_blocks,),
        in_specs=[pl.BlockSpec((MOE_ROWS, half), lambda i, be, nu: (i, 0)),
                  pl.BlockSpec((1, d, 2 * D_FF), lambda i, be, nu: (be[i], 0, 0)),
                  pl.BlockSpec((1, 1, 2 * D_FF), lambda i, be, nu: (be[i], 0, 0)),
                  pl.BlockSpec((1, D_FF, d), lambda i, be, nu: (be[i], 0, 0)),
                  pl.BlockSpec((1, 1, d), lambda i, be, nu: (be[i], 0, 0))],
        out_specs=pl.BlockSpec((MOE_ROWS, half), lambda i, be, nu: (i, 0)),
        scratch_shapes=[pltpu.VMEM((d, 2 * D_FF), BF16), pltpu.VMEM((D_FF, d), BF16)],
    )
    ys = pl.pallas_call(
        _ffn_kernel,
        grid_spec=grid_spec,
        out_shape=jax.ShapeDtypeStruct((n_blocks * MOE_ROWS, half), jnp.int32),
        compiler_params=_cparams(("arbitrary",)),
        name="moe_ffn",
    )(block_e, n_used, xs, w1, b1.reshape(N_EXPERTS, 1, -1), w2, b2.reshape(N_EXPERTS, 1, -1))
    return [_sc_gather_rows(ys, pos.reshape(-1)) for pos in positions]


def _final_kernel(x1_ref, y0_ref, y1_ref, y2_ref, y3_ref, gate_ref, mod_ref, g_ref, *rest):
    o_ref = rest[-1]
    gates = gate_ref[0]
    lo = hi = None
    for k, y_ref in enumerate((y0_ref, y1_ref, y2_ref, y3_ref)):
        y_lo, y_hi = _unpack_bf16_pairs(y_ref[0, 0])
        gk = gates[:, k:k + 1]
        lo = gk * y_lo if k == 0 else lo + gk * y_lo
        hi = gk * y_hi if k == 0 else hi + gk * y_hi
    y = jnp.concatenate([lo, hi], axis=-1)
    gate2 = mod_ref[0, 5:6, :]
    o_ref[0] = x1_ref[0] + gate2 * _rms(y, g_ref[...])


def _final(x1, yg, gate, mod, mod_row, g_post_ffn, tm, b0, n_batch, out_prev):
    bsz, seq, d = x1.shape
    tok3 = lambda b, i: (b, i, 0)
    yg = yg.reshape(TOP_K, bsz, seq, d // 2)
    slot_specs = [pl.BlockSpec((1, 1, tm, d // 2), functools.partial(lambda b, i, k: (k, b, i, 0), k=k))
                  for k in range(TOP_K)]
    in_specs = ([pl.BlockSpec((1, tm, d), tok3)] + slot_specs +
                [pl.BlockSpec((1, tm, LANES), tok3),
                 pl.BlockSpec((1, 6, d), lambda b, i: (mod_row(b), 0, 0)), _const_spec((1, d))])
    args = [x1, yg, yg, yg, yg, gate, mod, g_post_ffn.astype(F32).reshape(1, d)]
    aliases = {}
    if out_prev is not None:
        in_specs.append(pl.BlockSpec(memory_space=pl.ANY))
        args.append(out_prev)
        aliases = {len(args) - 1: 0}
    return pl.pallas_call(
        _final_kernel,
        grid=(bsz, seq // tm),
        in_specs=in_specs,
        out_specs=pl.BlockSpec((1, tm, d), lambda b, i: (b + b0, i, 0)),
        out_shape=jax.ShapeDtypeStruct((n_batch, seq, d), F32),
        input_output_aliases=aliases,
        compiler_params=_cparams(("parallel", "parallel")),
        name="final_residual",
    )(*args)


def _stream(x, mod, mod_row, p, ctx_kv, init_state, rope, tm, b0, bsz):
    _, seq, d = x.shape
    is_ctx = ctx_kv is None
    pj = _project(x, mod, p["g_pre_mix"], p["w_in"], p["b_gates"], mod_row, rope, is_ctx, tm, b0, bsz)
    if is_ctx:
        att = _attention(pj["q"], None, None, pj["kd"], pj["vd"], p["sink"], window=False)
    else:
        att = _attention(pj["q"], pj["kd"], pj["vd"], ctx_kv[0], ctx_kv[1], p["sink"], window=True)
    ml = _mlstm(pj["mq"], pj["mk"], pj["mv"], pj["gates"], init_state, emit_state=is_ctx)
    x1, h2, idx, gate, tile_counts = _merge(att, ml[0], ml[1], pj, x, mod, mod_row, p, tm, b0)
    moe_in = (h2.reshape(bsz * seq, d // 2), idx, tile_counts)
    return x1, gate, moe_in, pj, ml


def kernel(x_prompt, x_sample, c, cache_k, cache_v, state_C, state_n, state_m, c_ctx, w_ada, b_ada, g_pre_mix,
           w_in, b_gates, attn_sink, norm_mlstm, w_up_att, w_up_ml, w_out, g_post_mix, g_pre_ffn, w_router,
           b_router, w1, b1, w2, b2, g_post_ffn):
    depth = w_ada.shape[0]
    n_dec = c.shape[0]
    cond = jnp.concatenate([c_ctx[None, :], c], axis=0).astype(F32)
    cond = jnp.pad(cond, ((0, 16 - cond.shape[0]), (0, 0)))
    y_prompt, y_sample = x_prompt, x_sample
    ks_, vs_, cs_, ns_, ms_ = [], [], [], [], []
    for l in range(depth):
        p = dict(g_pre_mix=g_pre_mix[l], w_in=w_in[l], b_gates=b_gates[l], sink=attn_sink[l], norm_ml=norm_mlstm[l],
                 w_up_att=w_up_att[l], w_up_ml=w_up_ml[l], w_out=w_out[l], g_post_mix=g_post_mix[l],
                 g_pre_ffn=g_pre_ffn[l], w_router=w_router[l], b_router=b_router[l], w1=w1[l], b1=b1[l],
                 w2=w2[l], b2=b2[l], g_post_ffn=g_post_ffn[l])
        mod = _adaln(cond, w_ada[l], b_ada[l]).reshape(16, 6, D_MODEL)
        row_p = lambda b: 0
        bsz, seq = x_prompt.shape[:2]
        x1_p, gate_p, moe_p, pj, ml = _stream(y_prompt, mod, row_p, p, None, None, False, 256, 0, bsz)
        nat = pj["nat"]
        ks_.append(nat[..., :KV_W].reshape(bsz, seq, ATT_KV_HEADS, HEAD_DIM))
        vs_.append(nat[..., KV_W:].reshape(bsz, seq, ATT_KV_HEADS, HEAD_DIM))
        cs_.append(ml[2])
        ns_.append(ml[3])
        ms_.append(ml[4][..., 0])
        ctx_kv = (_dup_heads(cache_k[:, l], False), _dup_heads(cache_v[:, l], True))
        init = (state_C[:, l], state_n[:, l], state_m[:, l])
        half = n_dec // 2
        experts = (p["w1"], p["b1"].astype(F32), p["w2"], p["b2"].astype(F32))
        groups = []
        for b0 in (0, half):
            row_s = functools.partial(lambda b, off: b + off + 1, off=b0)
            sl = slice(b0, b0 + half)
            x1_s, gate_s, moe_s, _, _ = _stream(y_sample, mod, row_s, p, (ctx_kv[0][sl], ctx_kv[1][sl]),
                                                tuple(t[sl] for t in init), True, 512, b0, half)
            groups.append((x1_s, gate_s, moe_s, row_s))
        (x1_a, gate_a, moe_a, row_a), (x1_b, gate_b, moe_b, row_b) = groups
        yg_p, yg_a = _moe([moe_p[0], moe_a[0]], [moe_p[1], moe_a[1]], [moe_p[2], moe_a[2]], *experts)
        (yg_b,) = _moe([moe_b[0]], [moe_b[1]], [moe_b[2]], *experts)
        y_prompt = _final(x1_p, yg_p, gate_p, mod, row_p, p["g_post_ffn"], 256, 0, bsz, None)
        y_sample = _final(x1_a, yg_a, gate_a, mod, row_a, p["g_post_ffn"], 512, 0, n_dec, None)
        y_sample = _final(x1_b, yg_b, gate_b, mod, row_b, p["g_post_ffn"], 512, half, n_dec, y_sample)
    return (y_prompt, y_sample, jnp.stack(ks_, axis=1), jnp.stack(vs_, axis=1), jnp.stack(cs_, axis=1),
            jnp.stack(ns_, axis=1), jnp.stack(ms_, axis=1))
```

```python
import functools

import numpy as np
import jax
import jax.numpy as jnp
from jax import lax
from jax.experimental import pallas as pl
from jax.experimental.pallas import tpu as pltpu
from jax.experimental.pallas import tpu_sc as plsc

F32 = jnp.float32
BF16 = jnp.bfloat16

D_MODEL = 1024
GRID_W = 64
ATT_HEADS = 8
ATT_KV_HEADS = 2
ATT_GROUP = ATT_HEADS // ATT_KV_HEADS
HEAD_DIM = 64
BLOCK = 128
ROPE_THETA = 10000.0
AXIS_FREQS = HEAD_DIM // 4
ML_HEADS = 4
ML_DK = 128
ML_DV = 256
ML_CHUNK = 128
N_EXPERTS = 32
TOP_K = 4
D_FF = 1024
SWIGLU_ALPHA = 1.702
SWIGLU_LIMIT = 7.0
EPS = 1e-6
NEG = -1e30

Q_W = ATT_HEADS * HEAD_DIM
KV_W = ATT_KV_HEADS * HEAD_DIM
MLQK_W = ML_HEADS * ML_DK
MLV_W = ML_HEADS * ML_DV
GATE_W = 4 * ML_HEADS

LANES = 128
VMEM_LIMIT = 56 * 1024 * 1024
MOE_ROWS = 512
SC_CORES = 2
SC_SUBCORES = 16
SC_GATHER_CHUNK = 32
ROUTE_TILE = 256


def _cparams(sem):
    return pltpu.CompilerParams(dimension_semantics=sem, vmem_limit_bytes=VMEM_LIMIT)


def _const_spec(shape):
    nd = len(shape)
    return pl.BlockSpec(shape, lambda *_: (0,) * nd)


def _adaln_kernel(cond_ref, w_ref, b_ref, o_ref):
    cnd = cond_ref[...]
    act = cnd * jax.nn.sigmoid(cnd)
    o_ref[...] = jnp.dot(act.astype(BF16), w_ref[...].astype(BF16),
                         preferred_element_type=F32) + b_ref[...]


def _adaln(cond, w_ada, b_ada):
    rows, d = cond.shape
    n = w_ada.shape[1]
    tn = 1536
    return pl.pallas_call(
        _adaln_kernel,
        grid=(n // tn,),
        in_specs=[_const_spec((rows, d)),
                  pl.BlockSpec((d, tn), lambda j: (0, j)),
                  pl.BlockSpec((1, tn), lambda j: (0, j))],
        out_specs=pl.BlockSpec((rows, tn), lambda j: (0, j)),
        out_shape=jax.ShapeDtypeStruct((rows, n), F32),
        compiler_params=_cparams(("parallel",)),
        name="adaln",
    )(cond, w_ada, b_ada.reshape(1, n))


def _pack_w_in(w_in, b_gates, rope, with_nat):
    o_q, o_k, o_v = 0, Q_W, Q_W + KV_W
    o_mq = Q_W + 2 * KV_W
    o_mk = o_mq + MLQK_W
    o_mv = o_mk + MLQK_W
    o_g = o_mv + MLV_W
    o_mo = o_g + GATE_W
    o_ga = o_mo + MLV_W
    o_gm = o_ga + D_MODEL
    half = HEAD_DIM // 2

    def head_cols(base, h, rot):
        lo = base + h * HEAD_DIM
        if rot:
            return [w_in[:, lo + half:lo + HEAD_DIM], w_in[:, lo:lo + half]]
        return [w_in[:, lo:lo + HEAD_DIM]]

    def q_cols(rot):
        cols = []
        for h in range(ATT_HEADS):
            cols += head_cols(o_q, h, rot)
        return cols

    def dup_cols(base, rot):
        cols = []
        for h in range(ATT_KV_HEADS):
            hc = head_cols(base, h, rot)
            cols += hc + hc
        return cols

    segs, cols, off = {}, [], 0

    def add(name, cl):
        nonlocal off
        width = sum(c.shape[1] for c in cl)
        segs[name] = (off, off + width)
        cols.extend(cl)
        off += width

    add("q", q_cols(False))
    if rope:
        add("qrot", q_cols(True))
    add("kd", dup_cols(o_k, False))
    if rope:
        add("kdrot", dup_cols(o_k, True))
    if with_nat:
        add("nat", [w_in[:, o_k:o_k + 2 * KV_W]])
    add("mq", [w_in[:, o_mq:o_mq + MLQK_W]])
    add("mv", [w_in[:, o_mv:o_mv + MLV_W]])
    add("mo", [w_in[:, o_mo:o_mo + MLV_W]])
    add("ga", [w_in[:, o_ga:o_ga + D_MODEL]])
    add("gm", [w_in[:, o_gm:o_gm + D_MODEL]])
    add("gates", [w_in[:, o_g:o_g + GATE_W], jnp.zeros((D_MODEL, LANES - GATE_W), w_in.dtype)])
    wp = jnp.concatenate(cols, axis=1).astype(BF16)
    wvd_t = jnp.concatenate(dup_cols(o_v, False) + [w_in[:, o_mk:o_mk + MLQK_W]], axis=1).T.astype(BF16)
    bg = jnp.pad(b_gates.astype(F32), (0, LANES - GATE_W)).reshape(1, LANES)
    return wp, wvd_t, bg, segs


def _rope_tables(n_tok):
    rows = n_tok // GRID_W
    row = np.repeat(np.arange(rows), GRID_W).astype(np.float32)
    col = np.tile(np.arange(GRID_W), rows).astype(np.float32)
    inv = (np.float32(ROPE_THETA) ** (-np.arange(AXIS_FREQS, dtype=np.float32) / AXIS_FREQS)).astype(np.float32)
    ang = np.concatenate([row[:, None] * inv, col[:, None] * inv], axis=-1).astype(np.float32)
    cos, sin = np.cos(ang), np.sin(ang)
    c64 = np.concatenate([cos, cos], axis=-1)
    s64 = np.concatenate([-sin, sin], axis=-1)
    return (jnp.asarray(np.tile(c64, (1, 2)), F32), jnp.asarray(np.tile(s64, (1, 2)), F32))


def _proj_kernel(*refs, segs, rope, with_nat):
    it = iter(refs)
    x_ref, mod_ref, g_ref, w_ref, wvt_ref, bg_ref = (next(it) for _ in range(6))
    cos_ref = sin_ref = None
    if rope:
        cos_ref, sin_ref = next(it), next(it)
    q_ref, kd_ref, vd_ref = next(it), next(it), next(it)
    nat_ref = next(it) if with_nat else None
    mq_ref, mk_ref, mv_ref, gates_ref, mo_ref, ga_ref, gm_ref = (next(it) for _ in range(7))

    x = x_ref[0]
    shift = mod_ref[0, 0:1, :]
    scale = mod_ref[0, 1:2, :]
    ms = jnp.mean(x * x, axis=-1, keepdims=True)
    xn = x * lax.rsqrt(ms + EPS) * g_ref[...]
    xb = (xn * (1.0 + scale) + shift).astype(BF16)

    def seg(name):
        lo, hi = segs[name]
        return jnp.dot(xb, w_ref[:, lo:hi], preferred_element_type=F32)

    uq = seg("q")
    ukd = seg("kd")
    if rope:
        uqr = seg("qrot")
        ukr = seg("kdrot")
        cs = cos_ref[...]
        sn = sin_ref[...]
        for h in range(ATT_HEADS // 2):
            sl = slice(h * LANES, (h + 1) * LANES)
            q_ref[0, h] = ((uq[:, sl] * cs + uqr[:, sl] * sn) * HEAD_DIM ** -0.5).astype(BF16)
        for h in range(ATT_KV_HEADS):
            sl = slice(h * LANES, (h + 1) * LANES)
            kd_ref[0, :, sl] = (ukd[:, sl] * cs + ukr[:, sl] * sn).astype(BF16)
    else:
        for h in range(ATT_HEADS // 2):
            sl = slice(h * LANES, (h + 1) * LANES)
            q_ref[0, h] = (uq[:, sl] * HEAD_DIM ** -0.5).astype(BF16)
        kd_ref[0] = ukd.astype(BF16)
    ut = lax.dot_general(wvt_ref[...], xb, (((1,), (1,)), ((), ())), preferred_element_type=F32)
    vd_ref[0] = ut[:2 * LANES].astype(BF16)
    mk_ref[0] = ut[2 * LANES:].astype(BF16)
    if with_nat:
        nat_ref[0] = seg("nat")
    mq_ref[0] = (seg("mq") * ML_DK ** -0.5).astype(BF16)
    mv_ref[0] = seg("mv").astype(BF16)
    gates_ref[0] = seg("gates") + bg_ref[...]
    mo_ref[0] = seg("mo")
    ga_ref[0] = seg("ga")
    gm_ref[0] = seg("gm")


def _project(x, mod, g_pre, w_in, b_gates, mod_row, rope, with_nat, tm, b0, bsz):
    _, seq, d = x.shape
    wp, wvd_t, bg, segs = _pack_w_in(w_in, b_gates, rope, with_nat)
    nw = wp.shape[1]
    tok3 = lambda b, i: (b, i, 0)
    in_specs = [pl.BlockSpec((1, tm, d), lambda b, i: (b + b0, i, 0)),
                pl.BlockSpec((1, 6, d), lambda b, i: (mod_row(b), 0, 0)),
                _const_spec((1, d)),
                pl.BlockSpec((d, nw), lambda b, i: (0, 0), pipeline_mode=pl.Buffered(1)),
                _const_spec((2 * LANES + MLQK_W, d)), _const_spec((1, LANES))]
    args = [x, mod, g_pre.reshape(1, d), wp, wvd_t, bg]
    if rope:
        cos_t, sin_t = _rope_tables(seq)
        in_specs += [pl.BlockSpec((tm, LANES), lambda b, i: (i, 0))] * 2
        args += [cos_t, sin_t]

    def tok_out(width, dtype):
        return (jax.ShapeDtypeStruct((bsz, seq, width), dtype), pl.BlockSpec((1, tm, width), tok3))

    outs = [(jax.ShapeDtypeStruct((bsz, ATT_HEADS // 2, seq, LANES), BF16),
             pl.BlockSpec((1, ATT_HEADS // 2, tm, LANES), lambda b, i: (b, 0, i, 0))),
            tok_out(2 * LANES, BF16),
            (jax.ShapeDtypeStruct((bsz, 2 * LANES, seq), BF16),
             pl.BlockSpec((1, 2 * LANES, tm), lambda b, i: (b, 0, i)))]
    if with_nat:
        outs.append(tok_out(2 * KV_W, F32))
    outs += [tok_out(MLQK_W, BF16),
             (jax.ShapeDtypeStruct((bsz, MLQK_W, seq), BF16),
              pl.BlockSpec((1, MLQK_W, tm), lambda b, i: (b, 0, i))),
             tok_out(MLV_W, BF16), tok_out(LANES, F32),
             tok_out(MLV_W, F32), tok_out(D_MODEL, F32), tok_out(D_MODEL, F32)]
    res = pl.pallas_call(
        functools.partial(_proj_kernel, segs=segs, rope=rope, with_nat=with_nat),
        grid=(bsz, seq // tm),
        in_specs=in_specs,
        out_specs=[o[1] for o in outs],
        out_shape=[o[0] for o in outs],
        compiler_params=_cparams(("parallel", "parallel")),
        name="proj_rope" if rope else "proj_ctx",
    )(*args)
    names = ["q", "kd", "vd"] + (["nat"] if with_nat else []) + ["mq", "mk", "mv", "gates", "mo", "ga", "gm"]
    return dict(zip(names, res))


def _attn_kernel(*refs, window, n_blocks, n_ctx):
    it = iter(refs)
    sink_ref, q_ref = next(it), next(it)
    if window:
        kp_ref, kc_ref, kn_ref, vp_ref, vc_ref, vn_ref = (next(it) for _ in range(6))
    ck_ref, cv_ref, o_ref = next(it), next(it), next(it)
    j = pl.program_id(1)
    n_win = 3 * BLOCK if window else 0
    cols = ATT_GROUP * BLOCK

    col_id = lax.broadcasted_iota(jnp.int32, (1, cols), 1)
    if window:
        kl = lax.broadcasted_iota(jnp.int32, (BLOCK, cols), 0)
        ql = lax.broadcasted_iota(jnp.int32, (BLOCK, cols), 1) & (BLOCK - 1)
        prev_ok = kl >= ql + jnp.where(j == 0, BLOCK, 0)
        next_ok = kl <= ql - jnp.where(j == n_blocks - 1, BLOCK, 0)
    lane_lo = lax.broadcasted_iota(jnp.int32, (BLOCK, LANES), 1) < HEAD_DIM
    lane_row = lax.broadcasted_iota(jnp.int32, (1, LANES), 1) < HEAD_DIM
    half_lo = jnp.where(lane_row, 1.0, 0.0).astype(BF16)
    half_hi = jnp.where(lane_row, 0.0, 1.0).astype(BF16)

    for kvh in range(ATT_KV_HEADS):
        sl = slice(kvh * LANES, (kvh + 1) * LANES)
        if window:
            keys = jnp.concatenate([kp_ref[0, :, sl], kc_ref[0, :, sl], kn_ref[0, :, sl], ck_ref[0, :, sl]], axis=0)
            vals_t = jnp.concatenate([vp_ref[0, sl, :], vc_ref[0, sl, :], vn_ref[0, sl, :], cv_ref[0, sl, :]], axis=1)
        else:
            keys = ck_ref[0, :, sl]
            vals_t = cv_ref[0, sl, :]
        qs = jnp.concatenate([q_ref[0, kvh * (ATT_GROUP // 2) + g // 2] * (half_lo if g % 2 == 0 else half_hi)
                              for g in range(ATT_GROUP)], axis=0)
        s = lax.dot_general(keys, qs, (((1,), (1,)), ((), ())), preferred_element_type=F32)
        if window:
            s = jnp.concatenate([jnp.where(prev_ok, s[:BLOCK], NEG), s[BLOCK:2 * BLOCK],
                                 jnp.where(next_ok, s[2 * BLOCK:n_win], NEG), s[n_win:]], axis=0)
        snk = jnp.full((1, cols), sink_ref[kvh * ATT_GROUP + ATT_GROUP - 1], F32)
        for g in range(ATT_GROUP - 2, -1, -1):
            snk = jnp.where(col_id < (g + 1) * BLOCK, sink_ref[kvh * ATT_GROUP + g], snk)
        m = jnp.maximum(jnp.max(s, axis=0, keepdims=True), snk)
        p = jnp.exp(s - m)
        den = jnp.sum(p, axis=0, keepdims=True) + jnp.exp(snk - m)
        o_t = jnp.dot(vals_t, p.astype(BF16), preferred_element_type=F32) / den
        for pair in range(ATT_GROUP // 2):
            even = o_t[:, (2 * pair) * BLOCK:(2 * pair + 1) * BLOCK].T
            odd = o_t[:, (2 * pair + 1) * BLOCK:(2 * pair + 2) * BLOCK].T
            col = (kvh * (ATT_GROUP // 2) + pair) * LANES
            o_ref[0, :, col:col + LANES] = jnp.where(lane_lo, even, odd).astype(BF16)


def _attention(q, kd, vd_t, ckd, cvd_t, sink, window):
    bsz, _, seq, _ = q.shape
    nb = seq // BLOCK
    n_ctx = ckd.shape[1]
    in_specs = [pl.BlockSpec(memory_space=pltpu.SMEM),
                pl.BlockSpec((1, ATT_HEADS // 2, BLOCK, LANES), lambda b, j: (b, 0, j, 0))]
    args = [sink.astype(F32), q]
    if window:
        in_specs += [pl.BlockSpec((1, BLOCK, 2 * LANES), lambda b, j: (b, jnp.maximum(j - 1, 0), 0)),
                     pl.BlockSpec((1, BLOCK, 2 * LANES), lambda b, j: (b, j, 0)),
                     pl.BlockSpec((1, BLOCK, 2 * LANES), lambda b, j: (b, jnp.minimum(j + 1, nb - 1), 0)),
                     pl.BlockSpec((1, 2 * LANES, BLOCK), lambda b, j: (b, 0, jnp.maximum(j - 1, 0))),
                     pl.BlockSpec((1, 2 * LANES, BLOCK), lambda b, j: (b, 0, j)),
                     pl.BlockSpec((1, 2 * LANES, BLOCK), lambda b, j: (b, 0, jnp.minimum(j + 1, nb - 1)))]
        args += [kd, kd, kd, vd_t, vd_t, vd_t]
    in_specs += [pl.BlockSpec((1, n_ctx, 2 * LANES), lambda b, j: (b, 0, 0)),
                 pl.BlockSpec((1, 2 * LANES, n_ctx), lambda b, j: (b, 0, 0))]
    args += [ckd, cvd_t]
    return pl.pallas_call(
        functools.partial(_attn_kernel, window=window, n_blocks=nb, n_ctx=n_ctx),
        grid=(bsz, nb),
        in_specs=in_specs,
        out_specs=pl.BlockSpec((1, BLOCK, Q_W), lambda b, j: (b, j, 0)),
        out_shape=jax.ShapeDtypeStruct((bsz, seq, Q_W), BF16),
        compiler_params=_cparams(("parallel", "parallel")),
        name="attn_window" if window else "attn_ctx",
    )(*args)


def _dup_heads(t, transpose):
    b, l = t.shape[:2]
    out = jnp.concatenate([t, t], axis=-1).reshape(b, l, ATT_KV_HEADS * LANES).astype(BF16)
    return jnp.swapaxes(out, 1, 2) if transpose else out


def _split3(x):
    hi = x.astype(BF16)
    r1 = x - hi.astype(F32)
    mid = r1.astype(BF16)
    lo = (r1 - mid.astype(F32)).astype(BF16)
    return hi, mid, lo


def _log_sigmoid(x):
    return jnp.minimum(x, 0.0) - jnp.log1p(jnp.exp(-jnp.abs(x)))


def _mlstm_kernel(*refs, n_chunks, has_init, emit_state):
    it = iter(refs)
    qkvg = [[next(it) for _ in range(4)] for _ in range(2)]
    if has_init:
        c0_ref, n0_ref, m0_ref = next(it), next(it), next(it)
    h_refs = [next(it), next(it)]
    if emit_state:
        co_ref, no_ref, mo_ref = next(it), next(it), next(it)
    st_s, m_s = next(it), next(it)
    t = pl.program_id(1)
    L = ML_CHUNK

    @pl.when(t == 0)
    def _():
        for d in range(2):
            for h in range(ML_HEADS):
                if has_init:
                    st_s[d, h, :, :ML_DV] = c0_ref[0, d, h].T
                    st_s[d, h, :, ML_DV:] = jnp.broadcast_to(n0_ref[0, d, h:h + 1, :], (LANES, ML_DK)).T
                    m_s[d, h] = m0_ref[0, d, h:h + 1, :]
                else:
                    st_s[d, h] = jnp.zeros((ML_DK, ML_DV + LANES), F32)
                    m_s[d, h] = jnp.zeros((1, LANES), F32)

    ri = lax.broadcasted_iota(jnp.int32, (L, L), 0)
    ci = lax.broadcasted_iota(jnp.int32, (L, L), 1)
    ones_blk = jnp.ones((L, LANES), BF16)

    for d in range(2):
        q_ref, kt_ref, v_ref, g_ref = qkvg[d]
        tri = (ci <= ri) if d == 0 else (ci >= ri)
        tri_b = jnp.where(tri, 1.0, 0.0).astype(BF16)
        g = g_ref[0]
        lf = _log_sigmoid(g)
        g_t = g.T
        lf_t = lf.T
        bc = sum(jnp.dot(tri_b, part, preferred_element_type=F32) for part in _split3(lf))
        br = sum(lax.dot_general(part, tri_b, (((1,), (1,)), ((), ())), preferred_element_type=F32)
                 for part in _split3(lf_t))
        tot = jnp.sum(lf, axis=0, keepdims=True)

        for h in range(ML_HEADS):
            icol_i = 2 * ML_HEADS * d + h
            fcol = icol_i + ML_HEADS
            bcol = bc[:, fcol:fcol + 1]
            brow = br[fcol:fcol + 1, :]
            irow = g_t[icol_i:icol_i + 1, :]
            b_last = tot[:, fcol:fcol + 1]
            m_prev = m_s[d, h][:, 0:1]
            st_prev = st_s[d, h]
            qh = q_ref[0, :, h * ML_DK:(h + 1) * ML_DK]
            kt = kt_ref[0, h * ML_DK:(h + 1) * ML_DK, :]
            v_aug = jnp.concatenate([v_ref[0, :, h * ML_DV:(h + 1) * ML_DV], ones_blk], axis=1)

            rel = jnp.where(tri, irow - brow, NEG)
            inter = bcol + m_prev
            m_t = jnp.maximum(inter, bcol + jnp.max(rel, axis=-1, keepdims=True))
            w = jnp.exp(rel + (bcol - m_t))
            w_inter = jnp.exp(inter - m_t)
            s = jnp.dot(qh, kt, preferred_element_type=F32) * w
            lhs = jnp.concatenate([s.astype(BF16), (qh.astype(F32) * w_inter).astype(BF16)], axis=1)
            rhs = jnp.concatenate([v_aug, st_prev.astype(BF16)], axis=0)
            mix = jnp.dot(lhs, rhs, preferred_element_type=F32)
            inv = 1.0 / jnp.maximum(jnp.abs(mix[:, ML_DV:]), jnp.exp(-m_t))
            for c in range(ML_DV // LANES):
                lo = h * ML_DV + c * LANES
                h_refs[d][0, :, lo:lo + LANES] = mix[:, c * LANES:(c + 1) * LANES] * inv

            a = b_last - brow + irow
            m_new = jnp.maximum(b_last + m_prev, jnp.max(a, axis=-1, keepdims=True))
            wk = jnp.exp(a - m_new)
            decay = jnp.exp(b_last + m_prev - m_new)
            kw_t = (kt.astype(F32) * wk).astype(BF16)
            st_new = decay * st_prev + jnp.dot(kw_t, v_aug, preferred_element_type=F32)
            st_s[d, h] = st_new
            m_s[d, h] = jnp.broadcast_to(m_new, (1, LANES))
            if emit_state:
                @pl.when(t == n_chunks - 1)
                def _():
                    co_ref[0, d, h] = st_new[:, :ML_DV].T
                    no_ref[0, d, h:h + 1, :] = st_new[:, ML_DV:].T[0:1, :]
                    mo_ref[0, d, h:h + 1, :] = jnp.broadcast_to(m_new, (1, LANES))


def _mlstm(mq, mk_t, mv, gates, init, emit_state):
    bsz, seq, _ = mq.shape
    nc = seq // ML_CHUNK
    fwd = lambda b, t: (b, t, 0)
    bwd = lambda b, t: (b, nc - 1 - t, 0)
    fwd_t = lambda b, t: (b, 0, t)
    bwd_t = lambda b, t: (b, 0, nc - 1 - t)
    state5 = lambda b, t: (b, 0, 0, 0, 0)
    state4 = lambda b, t: (b, 0, 0, 0)
    in_specs, args = [], []
    for chunk, chunk_t in ((fwd, fwd_t), (bwd, bwd_t)):
        in_specs += [pl.BlockSpec((1, ML_CHUNK, MLQK_W), chunk), pl.BlockSpec((1, MLQK_W, ML_CHUNK), chunk_t),
                     pl.BlockSpec((1, ML_CHUNK, MLV_W), chunk), pl.BlockSpec((1, ML_CHUNK, LANES), chunk)]
        args += [mq, mk_t, mv, gates]
    if init is not None:
        c0, n0, m0 = init
        in_specs += [pl.BlockSpec((1, 2, ML_HEADS, ML_DV, ML_DK), state5),
                     pl.BlockSpec((1, 2, ML_HEADS, ML_DK), state4),
                     pl.BlockSpec((1, 2, ML_HEADS, LANES), state4)]
        args += [c0.astype(F32), n0.astype(F32),
                 jnp.broadcast_to(m0.astype(F32)[..., None], m0.shape + (LANES,))]
    out_shape = [jax.ShapeDtypeStruct((bsz, seq, MLV_W), F32)] * 2
    out_specs = [pl.BlockSpec((1, ML_CHUNK, MLV_W), fwd), pl.BlockSpec((1, ML_CHUNK, MLV_W), bwd)]
    if emit_state:
        out_shape += [jax.ShapeDtypeStruct((bsz, 2, ML_HEADS, ML_DV, ML_DK), F32),
                      jax.ShapeDtypeStruct((bsz, 2, ML_HEADS, ML_DK), F32),
                      jax.ShapeDtypeStruct((bsz, 2, ML_HEADS, LANES), F32)]
        out_specs += [pl.BlockSpec((1, 2, ML_HEADS, ML_DV, ML_DK), state5),
                      pl.BlockSpec((1, 2, ML_HEADS, ML_DK), state4),
                      pl.BlockSpec((1, 2, ML_HEADS, LANES), state4)]
    return pl.pallas_call(
        functools.partial(_mlstm_kernel, n_chunks=nc, has_init=init is not None, emit_state=emit_state),
        grid=(bsz, nc),
        in_specs=in_specs,
        out_specs=out_specs,
        out_shape=out_shape,
        scratch_shapes=[pltpu.VMEM((2, ML_HEADS, ML_DK, ML_DV + LANES), F32),
                        pltpu.VMEM((2, ML_HEADS, 1, LANES), F32)],
        compiler_params=_cparams(("parallel", "arbitrary")),
        name="mlstm_state" if emit_state else "mlstm",
    )(*args)


def _sigmoid(x):
    return 0.5 * jnp.tanh(0.5 * x) + 0.5


def _rms(x, g):
    return x * lax.rsqrt(jnp.mean(x * x, axis=-1, keepdims=True) + EPS) * g


def _pack_bf16_pairs(x):
    n = x.shape[1] // 2
    lo = pltpu.bitcast(x[:, :n].astype(BF16).astype(F32), jnp.int32)
    hi = pltpu.bitcast(x[:, n:].astype(BF16).astype(F32), jnp.int32)
    return lax.shift_right_logical(lo, 16) | hi


def _unpack_bf16_pairs(p):
    lo = pltpu.bitcast(lax.shift_left(p, 16), F32)
    hi = pltpu.bitcast(p & jnp.int32(-65536), F32)
    return lo, hi


def _merge_kernel(att_ref, hf_ref, hb_ref, mo_ref, ga_ref, gm_ref, x_ref, mod_ref, nml_ref, gpm_ref, gpf_ref,
                  wua_ref, wum_ref, wo_ref, wr_ref, br_ref, x1_ref, h2_ref, idx_ref, gate_ref, cnt_ref):
    hsum = hf_ref[0] + hb_ref[0]
    parts = []
    for h in range(ML_HEADS):
        hh = hsum[:, h * ML_DV:(h + 1) * ML_DV]
        parts.append(hh * lax.rsqrt(jnp.mean(hh * hh, axis=-1, keepdims=True) + EPS))
    hn = jnp.concatenate(parts, axis=-1) * nml_ref[...]
    ml = (hn * _sigmoid(mo_ref[0])).astype(BF16)
    a = jnp.dot(att_ref[0], wua_ref[...], preferred_element_type=F32)
    m = jnp.dot(ml, wum_ref[...], preferred_element_type=F32)
    z = (_sigmoid(ga_ref[0]) * a + _sigmoid(gm_ref[0]) * m).astype(BF16)
    mix = jnp.dot(z, wo_ref[...], preferred_element_type=F32)
    gate1 = mod_ref[0, 2:3, :]
    shift2 = mod_ref[0, 3:4, :]
    scale2 = mod_ref[0, 4:5, :]
    x1 = x_ref[0] + gate1 * _rms(mix, gpm_ref[...])
    x1_ref[0] = x1
    h2 = _rms(x1, gpf_ref[...]) * (1.0 + scale2) + shift2
    h2_ref[0] = _pack_bf16_pairs(h2)

    h2_hi = h2.astype(BF16)
    h2_lo = (h2 - h2_hi.astype(F32)).astype(BF16)
    wr = wr_ref[...]
    wr_hi = wr.astype(BF16)
    wr_lo = (wr - wr_hi.astype(F32)).astype(BF16)
    logits = (jnp.dot(h2_hi, wr_hi, preferred_element_type=F32)
              + jnp.dot(h2_lo, wr_hi, preferred_element_type=F32)
              + jnp.dot(h2_hi, wr_lo, preferred_element_type=F32)) + br_ref[...]
    lane = lax.broadcasted_iota(jnp.int32, logits.shape, 1).astype(F32)
    work = jnp.where(lane < N_EXPERTS, logits, -jnp.inf)
    idx_out = jnp.zeros(logits.shape, F32)
    val_out = jnp.zeros(logits.shape, F32)
    picked = jnp.zeros(logits.shape, F32)
    top0 = None
    esum = None
    for k in range(TOP_K):
        mx = jnp.max(work, axis=-1, keepdims=True)
        sel = jnp.min(jnp.where(work == mx, lane, float(LANES)), axis=-1, keepdims=True)
        if k == 0:
            top0 = mx
        e = jnp.exp(mx - top0)
        esum = e if k == 0 else esum + e
        idx_out = jnp.where(lane == k, sel, idx_out)
        val_out = jnp.where(lane == k, e, val_out)
        picked = jnp.where(lane == sel, 1.0, picked)
        work = jnp.where(lane == sel, -jnp.inf, work)
    idx_ref[0] = idx_out.astype(jnp.int32)
    gate_ref[0] = val_out / esum
    for r in range(logits.shape[0] // ROUTE_TILE):
        cnt_ref[0, r:r + 1, :] = jnp.sum(picked[r * ROUTE_TILE:(r + 1) * ROUTE_TILE], axis=0, keepdims=True)


def _merge(att, hf, hb, pj, x, mod, mod_row, p, tm, b0):
    bsz = att.shape[0]
    _, seq, d = x.shape
    tok3 = lambda b, i: (b, i, 0)
    wr = jnp.pad(p["w_router"].astype(F32), ((0, 0), (0, LANES - N_EXPERTS)))
    br = jnp.pad(p["b_router"].astype(F32), (0, LANES - N_EXPERTS)).reshape(1, LANES)
    row = lambda v: v.astype(F32).reshape(1, -1)
    in_specs = [pl.BlockSpec((1, tm, Q_W), tok3),
                pl.BlockSpec((1, tm, MLV_W), tok3), pl.BlockSpec((1, tm, MLV_W), tok3),
                pl.BlockSpec((1, tm, MLV_W), tok3), pl.BlockSpec((1, tm, d), tok3), pl.BlockSpec((1, tm, d), tok3),
                pl.BlockSpec((1, tm, d), lambda b, i: (b + b0, i, 0)),
                pl.BlockSpec((1, 6, d), lambda b, i: (mod_row(b), 0, 0)),
                _const_spec((1, MLV_W)), _const_spec((1, d)), _const_spec((1, d)),
                _const_spec((Q_W, d)), _const_spec((MLV_W, d)), _const_spec((d, d)),
                _const_spec((d, LANES)), _const_spec((1, LANES))]
    out_shape = [jax.ShapeDtypeStruct((bsz, seq, d), F32), jax.ShapeDtypeStruct((bsz, seq, d // 2), jnp.int32),
                 jax.ShapeDtypeStruct((bsz, seq, LANES), jnp.int32), jax.ShapeDtypeStruct((bsz, seq, LANES), F32),
                 jax.ShapeDtypeStruct((bsz * (seq // tm), tm // ROUTE_TILE, LANES), F32)]
    out_specs = [pl.BlockSpec((1, tm, d), tok3), pl.BlockSpec((1, tm, d // 2), tok3),
                 pl.BlockSpec((1, tm, LANES), tok3), pl.BlockSpec((1, tm, LANES), tok3),
                 pl.BlockSpec((1, tm // ROUTE_TILE, LANES), lambda b, i: (b * (seq // tm) + i, 0, 0))]
    return pl.pallas_call(
        _merge_kernel,
        grid=(bsz, seq // tm),
        in_specs=in_specs,
        out_specs=out_specs,
        out_shape=out_shape,
        compiler_params=_cparams(("parallel", "parallel")),
        name="merge_router",
    )(att, hf, hb, pj["mo"], pj["ga"], pj["gm"], x, mod, row(p["norm_ml"]), row(p["g_post_mix"]),
      row(p["g_pre_ffn"]), p["w_up_att"].astype(BF16), p["w_up_ml"].astype(BF16), p["w_out"].astype(BF16), wr, br)


def _sc_gather_rows(table, idx):
    n = idx.shape[0]
    width = table.shape[1]
    chunk = SC_GATHER_CHUNK
    n_workers = SC_CORES * SC_SUBCORES
    rows_per_worker = n // n_workers
    n_pairs = rows_per_worker // (2 * chunk)
    assert n_pairs * 2 * chunk * n_workers == n
    mesh = plsc.VectorSubcoreMesh(core_axis_name="c", subcore_axis_name="s")

    def body(table_hbm, idx_hbm, out_hbm, idx_v, rows_v, gsem, osem):
        wid = lax.axis_index("s") * SC_CORES + lax.axis_index("c")
        base0 = wid * rows_per_worker

        def gather_copy(b):
            return pltpu.make_async_copy(table_hbm.at[idx_v.at[b]], rows_v.at[b], gsem.at[b])

        def write_copy(ci, b):
            base = pl.multiple_of(base0 + ci * chunk, 8)
            return pltpu.make_async_copy(rows_v.at[b], out_hbm.at[pl.ds(base, chunk)], osem.at[b])

        def issue(ci, b):
            base = pl.multiple_of(base0 + ci * chunk, 8)
            pltpu.sync_copy(idx_hbm.at[pl.ds(base, chunk)], idx_v.at[b])
            gather_copy(b).start()

        def finish(ci, b):
            gather_copy(b).wait()
            write_copy(ci, b).start()

        issue(0, 0)

        @pl.loop(0, n_pairs)
        def _(j):
            @pl.when(j > 0)
            def _():
                write_copy(2 * j - 1, 1).wait()
            issue(2 * j + 1, 1)
            finish(2 * j, 0)

            @pl.when(j < n_pairs - 1)
            def _():
                write_copy(2 * j, 0).wait()
                issue(2 * j + 2, 0)
            finish(2 * j + 1, 1)

        write_copy(2 * n_pairs - 2, 0).wait()
        write_copy(2 * n_pairs - 1, 1).wait()

    return pl.kernel(
        body, mesh=mesh,
        out_type=jax.ShapeDtypeStruct((n, width), table.dtype),
        scratch_types=[pltpu.VMEM((2, chunk), jnp.int32), pltpu.VMEM((2, chunk, width), table.dtype),
                       pltpu.SemaphoreType.DMA((2,)), pltpu.SemaphoreType.DMA((2,))],
    )(table, idx)


def _ffn_kernel(be_ref, nu_ref, xs_ref, w1_ref, b1_ref, w2_ref, b2_ref, ys_ref, w1b_s, w2b_s):
    i = pl.program_id(0)
    e = be_ref[i]
    e_prev = be_ref[jnp.maximum(i - 1, 0)]

    @pl.when((i == 0) | (e != e_prev))
    def _():
        w1b_s[...] = w1_ref[0].astype(BF16)
        w2b_s[...] = w2_ref[0].astype(BF16)

    @pl.when(i < nu_ref[0])
    def _():
        half = D_MODEL // 2
        x_lo, x_hi = _unpack_bf16_pairs(xs_ref[...])
        hmid = (jnp.dot(x_lo.astype(BF16), w1b_s[:half, :], preferred_element_type=F32)
                + jnp.dot(x_hi.astype(BF16), w1b_s[half:, :], preferred_element_type=F32)) + b1_ref[0]
        glu = jnp.minimum(hmid[:, :D_FF], SWIGLU_LIMIT)
        lin = jnp.clip(hmid[:, D_FF:], -SWIGLU_LIMIT, SWIGLU_LIMIT)
        act = ((lin + 1.0) * glu * _sigmoid(SWIGLU_ALPHA * glu)).astype(BF16)
        y = jnp.dot(act, w2b_s[...], preferred_element_type=F32) + b2_ref[0]
        ys_ref[...] = _pack_bf16_pairs(y)

    @pl.when(i >= nu_ref[0])
    def _():
        ys_ref[...] = jnp.zeros(ys_ref.shape, jnp.int32)


def _sc_scatter_rows(tables, positions, n_rows):
    width = tables[0].shape[1]
    chunk = SC_GATHER_CHUNK
    n_workers = SC_CORES * SC_SUBCORES
    n_streams = len(tables)
    mesh = plsc.VectorSubcoreMesh(core_axis_name="c", subcore_axis_name="s")

    def body(*refs):
        table_refs = refs[0:2 * n_streams:2]
        pos_refs = refs[1:2 * n_streams:2]
        out_hbm, idx_v, rows_v, lsem, ssem = refs[2 * n_streams:]
        wid = lax.axis_index("s") * SC_CORES + lax.axis_index("c")

        for table_hbm, pos_hbm in zip(table_refs, pos_refs):
            tok_per_worker = table_hbm.shape[0] // n_workers
            n_pairs = tok_per_worker // (2 * chunk)
            assert n_pairs * 2 * chunk * n_workers == table_hbm.shape[0]
            base0 = wid * tok_per_worker

            def load_copy(ci, b):
                base = pl.multiple_of(base0 + ci * chunk, 8)
                return pltpu.make_async_copy(table_hbm.at[pl.ds(base, chunk)], rows_v.at[b], lsem.at[b])

            def scatter_copy(b, k):
                return pltpu.make_async_copy(rows_v.at[b], out_hbm.at[idx_v.at[b, k]], ssem.at[b])

            def load(ci, b):
                base = pl.multiple_of(base0 + ci * chunk, 8)
                load_copy(ci, b).start()
                for k in range(TOP_K):
                    pltpu.sync_copy(pos_hbm.at[k, pl.ds(base, chunk)], idx_v.at[b, k])

            def scatter(ci, b):
                load_copy(ci, b).wait()
                for k in range(TOP_K):
                    scatter_copy(b, k).start()

            def drain(b):
                for k in range(TOP_K):
                    scatter_copy(b, k).wait()

            load(0, 0)

            @pl.loop(0, n_pairs)
            def _(j):
                @pl.when(j > 0)
                def _():
                    drain(1)
                load(2 * j + 1, 1)
                scatter(2 * j, 0)

                @pl.when(j < n_pairs - 1)
                def _():
                    drain(0)
                    load(2 * j + 2, 0)
                scatter(2 * j + 1, 1)

            drain(0)
            drain(1)

    args = [a for pair in zip(tables, positions) for a in pair]
    return pl.kernel(
        body, mesh=mesh,
        out_type=jax.ShapeDtypeStruct((n_rows, width), tables[0].dtype),
        scratch_types=[pltpu.VMEM((2, TOP_K, chunk), jnp.int32), pltpu.VMEM((2, chunk, width), tables[0].dtype),
                       pltpu.SemaphoreType.DMA((2,)), pltpu.SemaphoreType.DMA((2,))],
    )(*args)


def _pos_kernel(idx_ref, base_ref, pos_ref):
    idx = idx_ref[...]
    rows = idx.shape[0]
    lane = lax.broadcasted_iota(jnp.int32, idx.shape, 1)
    hots = [jnp.where(lane == idx[:, k:k + 1], 1.0, 0.0) for k in range(TOP_K)]
    cnt = hots[0] + hots[1] + hots[2] + hots[3]
    ri = lax.broadcasted_iota(jnp.int32, (rows, rows), 0)
    ci = lax.broadcasted_iota(jnp.int32, (rows, rows), 1)
    earlier = jnp.where(ci < ri, 1.0, 0.0).astype(BF16)
    ahead = jnp.dot(earlier, cnt.astype(BF16), preferred_element_type=F32) + base_ref[0]
    posmat = jnp.zeros(idx.shape, F32)
    for k in range(TOP_K):
        posmat = jnp.where(lane == k, jnp.sum(hots[k] * ahead, axis=-1, keepdims=True), posmat)
    pos_ref[...] = posmat.T[:TOP_K, :].astype(jnp.int32)


def _route(idx_list, count_list):
    tiles = [c.shape[0] * c.shape[1] for c in count_list]
    n_tok = sum(tiles) * ROUTE_TILE
    n_blocks = n_tok * TOP_K // MOE_ROWS + N_EXPERTS
    cnt = jnp.concatenate([c.reshape(-1, LANES) for c in count_list], axis=0).astype(jnp.int32)
    tile_off = jnp.cumsum(cnt, axis=0) - cnt
    total = jnp.sum(cnt, axis=0)
    padded = (total + MOE_ROWS - 1) // MOE_ROWS * MOE_ROWS
    pend = jnp.cumsum(padded)
    base = ((pend - padded)[None, :] + tile_off).astype(F32)
    starts = jnp.arange(n_blocks, dtype=jnp.int32) * MOE_ROWS
    block_e = jnp.minimum(jnp.sum(pend[None, :N_EXPERTS] <= starts[:, None], axis=1), N_EXPERTS - 1).astype(jnp.int32)
    n_used = (pend[N_EXPERTS - 1] // MOE_ROWS).astype(jnp.int32).reshape(1)
    positions, first = [], 0
    for idx, n_tiles in zip(idx_list, tiles):
        stream_tok = n_tiles * ROUTE_TILE
        positions.append(pl.pallas_call(
            _pos_kernel,
            grid=(n_tiles,),
            in_specs=[pl.BlockSpec((ROUTE_TILE, LANES), lambda i: (i, 0)),
                      pl.BlockSpec((1, 1, LANES), lambda i: (i, 0, 0))],
            out_specs=pl.BlockSpec((TOP_K, ROUTE_TILE), lambda i: (0, i)),
            out_shape=jax.ShapeDtypeStruct((TOP_K, stream_tok), jnp.int32),
            compiler_params=_cparams(("parallel",)),
            name="route_pos",
        )(idx.reshape(stream_tok, LANES), base[first:first + n_tiles].reshape(n_tiles, 1, LANES)))
        first += n_tiles
    return block_e, n_used, positions


def _moe(h2p_list, idx_list, count_list, w1, b1, w2, b2):
    half = h2p_list[0].shape[1]
    d = 2 * half
    block_e, n_used, positions = _route(idx_list, count_list)
    n_blocks = block_e.shape[0]
    xs = _sc_scatter_rows(h2p_list, positions, n_blocks * MOE_ROWS)
    grid_spec = pltpu.PrefetchScalarGridSpec(
        num_scalar_prefetch=2,
        grid=(n_blocks,),
        in_specs=[pl.BlockSpec((MOE_ROWS, half), lambda i, be, nu: (i, 0)),
                  pl.BlockSpec((1, d, 2 * D_FF), lambda i, be, nu: (be[i], 0, 0)),
                  pl.BlockSpec((1, 1, 2 * D_FF), lambda i, be, nu: (be[i], 0, 0)),
                  pl.BlockSpec((1, D_FF, d), lambda i, be, nu: (be[i], 0, 0)),
                  pl.BlockSpec((1, 1, d), lambda i, be, nu: (be[i], 0, 0))],
        out_specs=pl.BlockSpec((MOE_ROWS, half), lambda i, be, nu: (i, 0)),
        scratch_shapes=[pltpu.VMEM((d, 2 * D_FF), BF16), pltpu.VMEM((D_FF, d), BF16)],
    )
    ys = pl.pallas_call(
        _ffn_kernel,
        grid_spec=grid_spec,
        out_shape=jax.ShapeDtypeStruct((n_blocks * MOE_ROWS, half), jnp.int32),
        compiler_params=_cparams(("arbitrary",)),
        name="moe_ffn",
    )(block_e, n_used, xs, w1, b1.reshape(N_EXPERTS, 1, -1), w2, b2.reshape(N_EXPERTS, 1, -1))
    return [_sc_gather_rows(ys, pos.reshape(-1)) for pos in positions]


def _final_kernel(x1_ref, y0_ref, y1_ref, y2_ref, y3_ref, gate_ref, mod_ref, g_ref, *rest):
    o_ref = rest[-1]
    gates = gate_ref[0]
    lo = hi = None
    for k, y_ref in enumerate((y0_ref, y1_ref, y2_ref, y3_ref)):
        y_lo, y_hi = _unpack_bf16_pairs(y_ref[0, 0])
        gk = gates[:, k:k + 1]
        lo = gk * y_lo if k == 0 else lo + gk * y_lo
        hi = gk * y_hi if k == 0 else hi + gk * y_hi
    y = jnp.concatenate([lo, hi], axis=-1)
    gate2 = mod_ref[0, 5:6, :]
    o_ref[0] = x1_ref[0] + gate2 * _rms(y, g_ref[...])


def _final(x1, yg, gate, mod, mod_row, g_post_ffn, tm, b0, n_batch, out_prev):
    bsz, seq, d = x1.shape
    tok3 = lambda b, i: (b, i, 0)
    yg = yg.reshape(TOP_K, bsz, seq, d // 2)
    slot_specs = [pl.BlockSpec((1, 1, tm, d // 2), functools.partial(lambda b, i, k: (k, b, i, 0), k=k))
                  for k in range(TOP_K)]
    in_specs = ([pl.BlockSpec((1, tm, d), tok3)] + slot_specs +
                [pl.BlockSpec((1, tm, LANES), tok3),
                 pl.BlockSpec((1, 6, d), lambda b, i: (mod_row(b), 0, 0)), _const_spec((1, d))])
    args = [x1, yg, yg, yg, yg, gate, mod, g_post_ffn.astype(F32).reshape(1, d)]
    aliases = {}
    if out_prev is not None:
        in_specs.append(pl.BlockSpec(memory_space=pl.ANY))
        args.append(out_prev)
        aliases = {len(args) - 1: 0}
    return pl.pallas_call(
        _final_kernel,
        grid=(bsz, seq // tm),
        in_specs=in_specs,
        out_specs=pl.BlockSpec((1, tm, d), lambda b, i: (b + b0, i, 0)),
        out_shape=jax.ShapeDtypeStruct((n_batch, seq, d), F32),
        input_output_aliases=aliases,
        compiler_params=_cparams(("parallel", "parallel")),
        name="final_residual",
    )(*args)


def _stream(x, mod, mod_row, p, ctx_kv, init_state, rope, tm, b0, bsz):
    _, seq, d = x.shape
    is_ctx = ctx_kv is None
    pj = _project(x, mod, p["g_pre_mix"], p["w_in"], p["b_gates"], mod_row, rope, is_ctx, tm, b0, bsz)
    if is_ctx:
        att = _attention(pj["q"], None, None, pj["kd"], pj["vd"], p["sink"], window=False)
    else:
        att = _attention(pj["q"], pj["kd"], pj["vd"], ctx_kv[0], ctx_kv[1], p["sink"], window=True)
    ml = _mlstm(pj["mq"], pj["mk"], pj["mv"], pj["gates"], init_state, emit_state=is_ctx)
    x1, h2, idx, gate, tile_counts = _merge(att, ml[0], ml[1], pj, x, mod, mod_row, p, tm, b0)
    moe_in = (h2.reshape(bsz * seq, d // 2), idx, tile_counts)
    return x1, gate, moe_in, pj, ml


def kernel(x_prompt, x_sample, c, cache_k, cache_v, state_C, state_n, state_m, c_ctx, w_ada, b_ada, g_pre_mix,
           w_in, b_gates, attn_sink, norm_mlstm, w_up_att, w_up_ml, w_out, g_post_mix, g_pre_ffn, w_router,
           b_router, w1, b1, w2, b2, g_post_ffn):
    depth = w_ada.shape[0]
    n_dec = c.shape[0]
    cond = jnp.concatenate([c_ctx[None, :], c], axis=0).astype(F32)
    cond = jnp.pad(cond, ((0, 16 - cond.shape[0]), (0, 0)))
    y_prompt, y_sample = x_prompt, x_sample
    ks_, vs_, cs_, ns_, ms_ = [], [], [], [], []
    for l in range(depth):
        p = dict(g_pre_mix=g_pre_mix[l], w_in=w_in[l], b_gates=b_gates[l], sink=attn_sink[l], norm_ml=norm_mlstm[l],
                 w_up_att=w_up_att[l], w_up_ml=w_up_ml[l], w_out=w_out[l], g_post_mix=g_post_mix[l],
                 g_pre_ffn=g_pre_ffn[l], w_router=w_router[l], b_router=b_router[l], w1=w1[l], b1=b1[l],
                 w2=w2[l], b2=b2[l], g_post_ffn=g_post_ffn[l])
        mod = _adaln(cond, w_ada[l], b_ada[l]).reshape(16, 6, D_MODEL)
        row_p = lambda b: 0
        bsz, seq = x_prompt.shape[:2]
        x1_p, gate_p, moe_p, pj, ml = _stream(y_prompt, mod, row_p, p, None, None, False, 256, 0, bsz)
        nat = pj["nat"]
        ks_.append(nat[..., :KV_W].reshape(bsz, seq, ATT_KV_HEADS, HEAD_DIM))
        vs_.append(nat[..., KV_W:].reshape(bsz, seq, ATT_KV_HEADS, HEAD_DIM))
        cs_.append(ml[2])
        ns_.append(ml[3])
        ms_.append(ml[4][..., 0])
        ctx_kv = (_dup_heads(cache_k[:, l], False), _dup_heads(cache_v[:, l], True))
        init = (state_C[:, l], state_n[:, l], state_m[:, l])
        half = n_dec // 2
        experts = (p["w1"], p["b1"].astype(F32), p["w2"], p["b2"].astype(F32))
        groups = []
        for b0 in (0, half):
            row_s = functools.partial(lambda b, off: b + off + 1, off=b0)
            sl = slice(b0, b0 + half)
            x1_s, gate_s, moe_s, _, _ = _stream(y_sample, mod, row_s, p, (ctx_kv[0][sl], ctx_kv[1][sl]),
                                                tuple(t[sl] for t in init), True, 512, b0, half)
            groups.append((x1_s, gate_s, moe_s, row_s))
        (x1_a, gate_a, moe_a, row_a), (x1_b, gate_b, moe_b, row_b) = groups
        yg_p, yg_a = _moe([moe_p[0], moe_a[0]], [moe_p[1], moe_a[1]], [moe_p[2], moe_a[2]], *experts)
        (yg_b,) = _moe([moe_b[0]], [moe_b[1]], [moe_b[2]], *experts)
        y_prompt = _final(x1_p, yg_p, gate_p, mod, row_p, p["g_post_ffn"], 256, 0, bsz, None)
        y_sample = _final(x1_a, yg_a, gate_a, mod, row_a, p["g_post_ffn"], 512, 0, n_dec, None)
        y_sample = _final(x1_b, yg_b, gate_b, mod, row_b, p["g_post_ffn"], 512, half, n_dec, y_sample)
    return (y_prompt, y_sample, jnp.stack(ks_, axis=1), jnp.stack(vs_, axis=1), jnp.stack(cs_, axis=1),
            jnp.stack(ns_, axis=1), jnp.stack(ms_, axis=1))
```

```python
import functools

import numpy as np
import jax
import jax.numpy as jnp
from jax import lax
from jax.experimental import pallas as pl
from jax.experimental.pallas import tpu as pltpu
from jax.experimental.pallas import tpu_sc as plsc

F32 = jnp.float32
BF16 = jnp.bfloat16

D_MODEL = 1024
GRID_W = 64
ATT_HEADS = 8
ATT_KV_HEADS = 2
ATT_GROUP = ATT_HEADS // ATT_KV_HEADS
HEAD_DIM = 64
BLOCK = 128
ROPE_THETA = 10000.0
AXIS_FREQS = HEAD_DIM // 4
ML_HEADS = 4
ML_DK = 128
ML_DV = 256
ML_CHUNK = 128
N_EXPERTS = 32
TOP_K = 4
D_FF = 1024
SWIGLU_ALPHA = 1.702
SWIGLU_LIMIT = 7.0
EPS = 1e-6
NEG = -1e30

Q_W = ATT_HEADS * HEAD_DIM
KV_W = ATT_KV_HEADS * HEAD_DIM
MLQK_W = ML_HEADS * ML_DK
MLV_W = ML_HEADS * ML_DV
GATE_W = 4 * ML_HEADS

LANES = 128
VMEM_LIMIT = 56 * 1024 * 1024
MOE_ROWS = 512
SC_CORES = 2
SC_SUBCORES = 16
SC_GATHER_CHUNK = 32


def _cparams(sem):
    return pltpu.CompilerParams(dimension_semantics=sem, vmem_limit_bytes=VMEM_LIMIT)


def _const_spec(shape):
    nd = len(shape)
    return pl.BlockSpec(shape, lambda *_: (0,) * nd)


def _adaln_kernel(cond_ref, w_ref, b_ref, o_ref):
    cnd = cond_ref[...]
    act = cnd * jax.nn.sigmoid(cnd)
    o_ref[...] = jnp.dot(act.astype(BF16), w_ref[...].astype(BF16),
                         preferred_element_type=F32) + b_ref[...]


def _adaln(cond, w_ada, b_ada):
    rows, d = cond.shape
    n = w_ada.shape[1]
    tn = 1536
    return pl.pallas_call(
        _adaln_kernel,
        grid=(n // tn,),
        in_specs=[_const_spec((rows, d)),
                  pl.BlockSpec((d, tn), lambda j: (0, j)),
                  pl.BlockSpec((1, tn), lambda j: (0, j))],
        out_specs=pl.BlockSpec((rows, tn), lambda j: (0, j)),
        out_shape=jax.ShapeDtypeStruct((rows, n), F32),
        compiler_params=_cparams(("parallel",)),
        name="adaln",
    )(cond, w_ada, b_ada.reshape(1, n))


def _pack_w_in(w_in, b_gates, rope, with_nat):
    o_q, o_k, o_v = 0, Q_W, Q_W + KV_W
    o_mq = Q_W + 2 * KV_W
    o_mk = o_mq + MLQK_W
    o_mv = o_mk + MLQK_W
    o_g = o_mv + MLV_W
    o_mo = o_g + GATE_W
    o_ga = o_mo + MLV_W
    o_gm = o_ga + D_MODEL
    half = HEAD_DIM // 2

    def head_cols(base, h, rot):
        lo = base + h * HEAD_DIM
        if rot:
            return [w_in[:, lo + half:lo + HEAD_DIM], w_in[:, lo:lo + half]]
        return [w_in[:, lo:lo + HEAD_DIM]]

    def q_cols(rot):
        cols = []
        for h in range(ATT_HEADS):
            cols += head_cols(o_q, h, rot)
        return cols

    def dup_cols(base, rot):
        cols = []
        for h in range(ATT_KV_HEADS):
            hc = head_cols(base, h, rot)
            cols += hc + hc
        return cols

    segs, cols, off = {}, [], 0

    def add(name, cl):
        nonlocal off
        width = sum(c.shape[1] for c in cl)
        segs[name] = (off, off + width)
        cols.extend(cl)
        off += width

    add("q", q_cols(False))
    if rope:
        add("qrot", q_cols(True))
    add("kd", dup_cols(o_k, False))
    if rope:
        add("kdrot", dup_cols(o_k, True))
    if with_nat:
        add("nat", [w_in[:, o_k:o_k + 2 * KV_W]])
    add("mq", [w_in[:, o_mq:o_mq + MLQK_W]])
    add("mv", [w_in[:, o_mv:o_mv + MLV_W]])
    add("mo", [w_in[:, o_mo:o_mo + MLV_W]])
    add("ga", [w_in[:, o_ga:o_ga + D_MODEL]])
    add("gm", [w_in[:, o_gm:o_gm + D_MODEL]])
    add("gates", [w_in[:, o_g:o_g + GATE_W], jnp.zeros((D_MODEL, LANES - GATE_W), w_in.dtype)])
    wp = jnp.concatenate(cols, axis=1).astype(BF16)
    wvd_t = jnp.concatenate(dup_cols(o_v, False) + [w_in[:, o_mk:o_mk + MLQK_W]], axis=1).T.astype(BF16)
    bg = jnp.pad(b_gates.astype(F32), (0, LANES - GATE_W)).reshape(1, LANES)
    return wp, wvd_t, bg, segs


def _rope_tables(n_tok):
    rows = n_tok // GRID_W
    row = np.repeat(np.arange(rows), GRID_W).astype(np.float32)
    col = np.tile(np.arange(GRID_W), rows).astype(np.float32)
    inv = (np.float32(ROPE_THETA) ** (-np.arange(AXIS_FREQS, dtype=np.float32) / AXIS_FREQS)).astype(np.float32)
    ang = np.concatenate([row[:, None] * inv, col[:, None] * inv], axis=-1).astype(np.float32)
    cos, sin = np.cos(ang), np.sin(ang)
    c64 = np.concatenate([cos, cos], axis=-1)
    s64 = np.concatenate([-sin, sin], axis=-1)
    return (jnp.asarray(np.tile(c64, (1, 2)), F32), jnp.asarray(np.tile(s64, (1, 2)), F32))


def _proj_kernel(*refs, segs, rope, with_nat):
    it = iter(refs)
    x_ref, mod_ref, g_ref, w_ref, wvt_ref, bg_ref = (next(it) for _ in range(6))
    cos_ref = sin_ref = None
    if rope:
        cos_ref, sin_ref = next(it), next(it)
    q_ref, kd_ref, vd_ref = next(it), next(it), next(it)
    nat_ref = next(it) if with_nat else None
    mq_ref, mk_ref, mv_ref, gates_ref, mo_ref, ga_ref, gm_ref = (next(it) for _ in range(7))

    x = x_ref[0]
    shift = mod_ref[0, 0:1, :]
    scale = mod_ref[0, 1:2, :]
    ms = jnp.mean(x * x, axis=-1, keepdims=True)
    xn = x * lax.rsqrt(ms + EPS) * g_ref[...]
    xb = (xn * (1.0 + scale) + shift).astype(BF16)

    def seg(name):
        lo, hi = segs[name]
        return jnp.dot(xb, w_ref[:, lo:hi], preferred_element_type=F32)

    uq = seg("q")
    ukd = seg("kd")
    if rope:
        uqr = seg("qrot")
        ukr = seg("kdrot")
        cs = cos_ref[...]
        sn = sin_ref[...]
        for h in range(ATT_HEADS // 2):
            sl = slice(h * LANES, (h + 1) * LANES)
            q_ref[0, h] = ((uq[:, sl] * cs + uqr[:, sl] * sn) * HEAD_DIM ** -0.5).astype(BF16)
        for h in range(ATT_KV_HEADS):
            sl = slice(h * LANES, (h + 1) * LANES)
            kd_ref[0, :, sl] = (ukd[:, sl] * cs + ukr[:, sl] * sn).astype(BF16)
    else:
        for h in range(ATT_HEADS // 2):
            sl = slice(h * LANES, (h + 1) * LANES)
            q_ref[0, h] = (uq[:, sl] * HEAD_DIM ** -0.5).astype(BF16)
        kd_ref[0] = ukd.astype(BF16)
    ut = lax.dot_general(wvt_ref[...], xb, (((1,), (1,)), ((), ())), preferred_element_type=F32)
    vd_ref[0] = ut[:2 * LANES].astype(BF16)
    mk_ref[0] = ut[2 * LANES:].astype(BF16)
    if with_nat:
        nat_ref[0] = seg("nat")
    mq_ref[0] = (seg("mq") * ML_DK ** -0.5).astype(BF16)
    mv_ref[0] = seg("mv").astype(BF16)
    gates_ref[0] = seg("gates") + bg_ref[...]
    mo_ref[0] = seg("mo")
    ga_ref[0] = seg("ga")
    gm_ref[0] = seg("gm")


def _project(x, mod, g_pre, w_in, b_gates, mod_row, rope, with_nat, tm, b0, bsz):
    _, seq, d = x.shape
    wp, wvd_t, bg, segs = _pack_w_in(w_in, b_gates, rope, with_nat)
    nw = wp.shape[1]
    tok3 = lambda b, i: (b, i, 0)
    in_specs = [pl.BlockSpec((1, tm, d), lambda b, i: (b + b0, i, 0)),
                pl.BlockSpec((1, 6, d), lambda b, i: (mod_row(b), 0, 0)),
                _const_spec((1, d)),
                pl.BlockSpec((d, nw), lambda b, i: (0, 0), pipeline_mode=pl.Buffered(1)),
                _const_spec((2 * LANES + MLQK_W, d)), _const_spec((1, LANES))]
    args = [x, mod, g_pre.reshape(1, d), wp, wvd_t, bg]
    if rope:
        cos_t, sin_t = _rope_tables(seq)
        in_specs += [pl.BlockSpec((tm, LANES), lambda b, i: (i, 0))] * 2
        args += [cos_t, sin_t]

    def tok_out(width, dtype):
        return (jax.ShapeDtypeStruct((bsz, seq, width), dtype), pl.BlockSpec((1, tm, width), tok3))

    outs = [(jax.ShapeDtypeStruct((bsz, ATT_HEADS // 2, seq, LANES), BF16),
             pl.BlockSpec((1, ATT_HEADS // 2, tm, LANES), lambda b, i: (b, 0, i, 0))),
            tok_out(2 * LANES, BF16),
            (jax.ShapeDtypeStruct((bsz, 2 * LANES, seq), BF16),
             pl.BlockSpec((1, 2 * LANES, tm), lambda b, i: (b, 0, i)))]
    if with_nat:
        outs.append(tok_out(2 * KV_W, F32))
    outs += [tok_out(MLQK_W, BF16),
             (jax.ShapeDtypeStruct((bsz, MLQK_W, seq), BF16),
              pl.BlockSpec((1, MLQK_W, tm), lambda b, i: (b, 0, i))),
             tok_out(MLV_W, BF16), tok_out(LANES, F32),
             tok_out(MLV_W, F32), tok_out(D_MODEL, F32), tok_out(D_MODEL, F32)]
    res = pl.pallas_call(
        functools.partial(_proj_kernel, segs=segs, rope=rope, with_nat=with_nat),
        grid=(bsz, seq // tm),
        in_specs=in_specs,
        out_specs=[o[1] for o in outs],
        out_shape=[o[0] for o in outs],
        compiler_params=_cparams(("parallel", "parallel")),
        name="proj_rope" if rope else "proj_ctx",
    )(*args)
    names = ["q", "kd", "vd"] + (["nat"] if with_nat else []) + ["mq", "mk", "mv", "gates", "mo", "ga", "gm"]
    return dict(zip(names, res))


def _attn_kernel(*refs, window, n_blocks, n_ctx):
    it = iter(refs)
    sink_ref, q_ref = next(it), next(it)
    if window:
        kp_ref, kc_ref, kn_ref, vp_ref, vc_ref, vn_ref = (next(it) for _ in range(6))
    ck_ref, cv_ref, o_ref = next(it), next(it), next(it)
    j = pl.program_id(1)
    n_win = 3 * BLOCK if window else 0
    cols = ATT_GROUP * BLOCK

    col_id = lax.broadcasted_iota(jnp.int32, (1, cols), 1)
    if window:
        kl = lax.broadcasted_iota(jnp.int32, (BLOCK, cols), 0)
        ql = lax.broadcasted_iota(jnp.int32, (BLOCK, cols), 1) & (BLOCK - 1)
        prev_ok = kl >= ql + jnp.where(j == 0, BLOCK, 0)
        next_ok = kl <= ql - jnp.where(j == n_blocks - 1, BLOCK, 0)
    lane_lo = lax.broadcasted_iota(jnp.int32, (BLOCK, LANES), 1) < HEAD_DIM
    lane_row = lax.broadcasted_iota(jnp.int32, (1, LANES), 1) < HEAD_DIM
    half_lo = jnp.where(lane_row, 1.0, 0.0).astype(BF16)
    half_hi = jnp.where(lane_row, 0.0, 1.0).astype(BF16)

    for kvh in range(ATT_KV_HEADS):
        sl = slice(kvh * LANES, (kvh + 1) * LANES)
        if window:
            keys = jnp.concatenate([kp_ref[0, :, sl], kc_ref[0, :, sl], kn_ref[0, :, sl], ck_ref[0, :, sl]], axis=0)
            vals_t = jnp.concatenate([vp_ref[0, sl, :], vc_ref[0, sl, :], vn_ref[0, sl, :], cv_ref[0, sl, :]], axis=1)
        else:
            keys = ck_ref[0, :, sl]
            vals_t = cv_ref[0, sl, :]
        qs = jnp.concatenate([q_ref[0, kvh * (ATT_GROUP // 2) + g // 2] * (half_lo if g % 2 == 0 else half_hi)
                              for g in range(ATT_GROUP)], axis=0)
        s = lax.dot_general(keys, qs, (((1,), (1,)), ((), ())), preferred_element_type=F32)
        if window:
            s = jnp.concatenate([jnp.where(prev_ok, s[:BLOCK], NEG), s[BLOCK:2 * BLOCK],
                                 jnp.where(next_ok, s[2 * BLOCK:n_win], NEG), s[n_win:]], axis=0)
        snk = jnp.full((1, cols), sink_ref[kvh * ATT_GROUP + ATT_GROUP - 1], F32)
        for g in range(ATT_GROUP - 2, -1, -1):
            snk = jnp.where(col_id < (g + 1) * BLOCK, sink_ref[kvh * ATT_GROUP + g], snk)
        m = jnp.maximum(jnp.max(s, axis=0, keepdims=True), snk)
        p = jnp.exp(s - m)
        den = jnp.sum(p, axis=0, keepdims=True) + jnp.exp(snk - m)
        o_t = jnp.dot(vals_t, p.astype(BF16), preferred_element_type=F32) / den
        for pair in range(ATT_GROUP // 2):
            even = o_t[:, (2 * pair) * BLOCK:(2 * pair + 1) * BLOCK].T
            odd = o_t[:, (2 * pair + 1) * BLOCK:(2 * pair + 2) * BLOCK].T
            col = (kvh * (ATT_GROUP // 2) + pair) * LANES
            o_ref[0, :, col:col + LANES] = jnp.where(lane_lo, even, odd).astype(BF16)


def _attention(q, kd, vd_t, ckd, cvd_t, sink, window):
    bsz, _, seq, _ = q.shape
    nb = seq // BLOCK
    n_ctx = ckd.shape[1]
    in_specs = [pl.BlockSpec(memory_space=pltpu.SMEM),
                pl.BlockSpec((1, ATT_HEADS // 2, BLOCK, LANES), lambda b, j: (b, 0, j, 0))]
    args = [sink.astype(F32), q]
    if window:
        in_specs += [pl.BlockSpec((1, BLOCK, 2 * LANES), lambda b, j: (b, jnp.maximum(j - 1, 0), 0)),
                     pl.BlockSpec((1, BLOCK, 2 * LANES), lambda b, j: (b, j, 0)),
                     pl.BlockSpec((1, BLOCK, 2 * LANES), lambda b, j: (b, jnp.minimum(j + 1, nb - 1), 0)),
                     pl.BlockSpec((1, 2 * LANES, BLOCK), lambda b, j: (b, 0, jnp.maximum(j - 1, 0))),
                     pl.BlockSpec((1, 2 * LANES, BLOCK), lambda b, j: (b, 0, j)),
                     pl.BlockSpec((1, 2 * LANES, BLOCK), lambda b, j: (b, 0, jnp.minimum(j + 1, nb - 1)))]
        args += [kd, kd, kd, vd_t, vd_t, vd_t]
    in_specs += [pl.BlockSpec((1, n_ctx, 2 * LANES), lambda b, j: (b, 0, 0)),
                 pl.BlockSpec((1, 2 * LANES, n_ctx), lambda b, j: (b, 0, 0))]
    args += [ckd, cvd_t]
    return pl.pallas_call(
        functools.partial(_attn_kernel, window=window, n_blocks=nb, n_ctx=n_ctx),
        grid=(bsz, nb),
        in_specs=in_specs,
        out_specs=pl.BlockSpec((1, BLOCK, Q_W), lambda b, j: (b, j, 0)),
        out_shape=jax.ShapeDtypeStruct((bsz, seq, Q_W), BF16),
        compiler_params=_cparams(("parallel", "parallel")),
        name="attn_window" if window else "attn_ctx",
    )(*args)


def _dup_heads(t, transpose):
    b, l = t.shape[:2]
    out = jnp.concatenate([t, t], axis=-1).reshape(b, l, ATT_KV_HEADS * LANES).astype(BF16)
    return jnp.swapaxes(out, 1, 2) if transpose else out


def _split3(x):
    hi = x.astype(BF16)
    r1 = x - hi.astype(F32)
    mid = r1.astype(BF16)
    lo = (r1 - mid.astype(F32)).astype(BF16)
    return hi, mid, lo


def _log_sigmoid(x):
    return jnp.minimum(x, 0.0) - jnp.log1p(jnp.exp(-jnp.abs(x)))


def _mlstm_kernel(*refs, n_chunks, has_init, emit_state):
    it = iter(refs)
    qkvg = [[next(it) for _ in range(4)] for _ in range(2)]
    if has_init:
        c0_ref, n0_ref, m0_ref = next(it), next(it), next(it)
    h_refs = [next(it), next(it)]
    if emit_state:
        co_ref, no_ref, mo_ref = next(it), next(it), next(it)
    st_s, m_s = next(it), next(it)
    t = pl.program_id(1)
    L = ML_CHUNK

    @pl.when(t == 0)
    def _():
        for d in range(2):
            for h in range(ML_HEADS):
                if has_init:
                    st_s[d, h, :, :ML_DV] = c0_ref[0, d, h].T
                    st_s[d, h, :, ML_DV:] = jnp.broadcast_to(n0_ref[0, d, h:h + 1, :], (LANES, ML_DK)).T
                    m_s[d, h] = m0_ref[0, d, h:h + 1, :]
                else:
                    st_s[d, h] = jnp.zeros((ML_DK, ML_DV + LANES), F32)
                    m_s[d, h] = jnp.zeros((1, LANES), F32)

    ri = lax.broadcasted_iota(jnp.int32, (L, L), 0)
    ci = lax.broadcasted_iota(jnp.int32, (L, L), 1)
    ones_blk = jnp.ones((L, LANES), BF16)

    for d in range(2):
        q_ref, kt_ref, v_ref, g_ref = qkvg[d]
        tri = (ci <= ri) if d == 0 else (ci >= ri)
        tri_b = jnp.where(tri, 1.0, 0.0).astype(BF16)
        g = g_ref[0]
        lf = _log_sigmoid(g)
        g_t = g.T
        lf_t = lf.T
        bc = sum(jnp.dot(tri_b, part, preferred_element_type=F32) for part in _split3(lf))
        br = sum(lax.dot_general(part, tri_b, (((1,), (1,)), ((), ())), preferred_element_type=F32)
                 for part in _split3(lf_t))
        tot = jnp.sum(lf, axis=0, keepdims=True)

        for h in range(ML_HEADS):
            icol_i = 2 * ML_HEADS * d + h
            fcol = icol_i + ML_HEADS
            bcol = bc[:, fcol:fcol + 1]
            brow = br[fcol:fcol + 1, :]
            irow = g_t[icol_i:icol_i + 1, :]
            b_last = tot[:, fcol:fcol + 1]
            m_prev = m_s[d, h][:, 0:1]
            st_prev = st_s[d, h]
            qh = q_ref[0, :, h * ML_DK:(h + 1) * ML_DK]
            kt = kt_ref[0, h * ML_DK:(h + 1) * ML_DK, :]
            v_aug = jnp.concatenate([v_ref[0, :, h * ML_DV:(h + 1) * ML_DV], ones_blk], axis=1)

            rel = jnp.where(tri, irow - brow, NEG)
            inter = bcol + m_prev
            m_t = jnp.maximum(inter, bcol + jnp.max(rel, axis=-1, keepdims=True))
            w = jnp.exp(rel + (bcol - m_t))
            w_inter = jnp.exp(inter - m_t)
            s = jnp.dot(qh, kt, preferred_element_type=F32) * w
            lhs = jnp.concatenate([s.astype(BF16), (qh.astype(F32) * w_inter).astype(BF16)], axis=1)
            rhs = jnp.concatenate([v_aug, st_prev.astype(BF16)], axis=0)
            mix = jnp.dot(lhs, rhs, preferred_element_type=F32)
            inv = 1.0 / jnp.maximum(jnp.abs(mix[:, ML_DV:]), jnp.exp(-m_t))
            for c in range(ML_DV // LANES):
                lo = h * ML_DV + c * LANES
                h_refs[d][0, :, lo:lo + LANES] = mix[:, c * LANES:(c + 1) * LANES] * inv

            a = b_last - brow + irow
            m_new = jnp.maximum(b_last + m_prev, jnp.max(a, axis=-1, keepdims=True))
            wk = jnp.exp(a - m_new)
            decay = jnp.exp(b_last + m_prev - m_new)
            kw_t = (kt.astype(F32) * wk).astype(BF16)
            st_new = decay * st_prev + jnp.dot(kw_t, v_aug, preferred_element_type=F32)
            st_s[d, h] = st_new
            m_s[d, h] = jnp.broadcast_to(m_new, (1, LANES))
            if emit_state:
                @pl.when(t == n_chunks - 1)
                def _():
                    co_ref[0, d, h] = st_new[:, :ML_DV].T
                    no_ref[0, d, h:h + 1, :] = st_new[:, ML_DV:].T[0:1, :]
                    mo_ref[0, d, h:h + 1, :] = jnp.broadcast_to(m_new, (1, LANES))


def _mlstm(mq, mk_t, mv, gates, init, emit_state, b0=0):
    bsz, seq, _ = mq.shape
    nc = seq // ML_CHUNK
    fwd = lambda b, t: (b, t, 0)
    bwd = lambda b, t: (b, nc - 1 - t, 0)
    fwd_t = lambda b, t: (b, 0, t)
    bwd_t = lambda b, t: (b, 0, nc - 1 - t)
    state5 = lambda b, t: (b, 0, 0, 0, 0)
    state4 = lambda b, t: (b, 0, 0, 0)
    init5 = lambda b, t: (b + b0, 0, 0, 0, 0)
    init4 = lambda b, t: (b + b0, 0, 0, 0)
    in_specs, args = [], []
    for chunk, chunk_t in ((fwd, fwd_t), (bwd, bwd_t)):
        in_specs += [pl.BlockSpec((1, ML_CHUNK, MLQK_W), chunk), pl.BlockSpec((1, MLQK_W, ML_CHUNK), chunk_t),
                     pl.BlockSpec((1, ML_CHUNK, MLV_W), chunk), pl.BlockSpec((1, ML_CHUNK, LANES), chunk)]
        args += [mq, mk_t, mv, gates]
    if init is not None:
        c0, n0, m0 = init
        in_specs += [pl.BlockSpec((1, 2, ML_HEADS, ML_DV, ML_DK), init5),
                     pl.BlockSpec((1, 2, ML_HEADS, ML_DK), init4),
                     pl.BlockSpec((1, 2, ML_HEADS, LANES), init4)]
        args += [c0.astype(F32), n0.astype(F32),
                 jnp.broadcast_to(m0.astype(F32)[..., None], m0.shape + (LANES,))]
    out_shape = [jax.ShapeDtypeStruct((bsz, seq, MLV_W), F32)] * 2
    out_specs = [pl.BlockSpec((1, ML_CHUNK, MLV_W), fwd), pl.BlockSpec((1, ML_CHUNK, MLV_W), bwd)]
    if emit_state:
        out_shape += [jax.ShapeDtypeStruct((bsz, 2, ML_HEADS, ML_DV, ML_DK), F32),
                      jax.ShapeDtypeStruct((bsz, 2, ML_HEADS, ML_DK), F32),
                      jax.ShapeDtypeStruct((bsz, 2, ML_HEADS, LANES), F32)]
        out_specs += [pl.BlockSpec((1, 2, ML_HEADS, ML_DV, ML_DK), state5),
                      pl.BlockSpec((1, 2, ML_HEADS, ML_DK), state4),
                      pl.BlockSpec((1, 2, ML_HEADS, LANES), state4)]
    return pl.pallas_call(
        functools.partial(_mlstm_kernel, n_chunks=nc, has_init=init is not None, emit_state=emit_state),
        grid=(bsz, nc),
        in_specs=in_specs,
        out_specs=out_specs,
        out_shape=out_shape,
        scratch_shapes=[pltpu.VMEM((2, ML_HEADS, ML_DK, ML_DV + LANES), F32),
                        pltpu.VMEM((2, ML_HEADS, 1, LANES), F32)],
        compiler_params=_cparams(("parallel", "arbitrary")),
        name="mlstm_state" if emit_state else "mlstm",
    )(*args)


def _sigmoid(x):
    return 0.5 * jnp.tanh(0.5 * x) + 0.5


def _rms(x, g):
    return x * lax.rsqrt(jnp.mean(x * x, axis=-1, keepdims=True) + EPS) * g


def _pack_bf16_pairs(x):
    n = x.shape[1] // 2
    lo = pltpu.bitcast(x[:, :n].astype(BF16).astype(F32), jnp.int32)
    hi = pltpu.bitcast(x[:, n:].astype(BF16).astype(F32), jnp.int32)
    return lax.shift_right_logical(lo, 16) | hi


def _unpack_bf16_pairs(p):
    lo = pltpu.bitcast(lax.shift_left(p, 16), F32)
    hi = pltpu.bitcast(p & jnp.int32(-65536), F32)
    return lo, hi


def _merge_kernel(att_ref, hf_ref, hb_ref, mo_ref, ga_ref, gm_ref, x_ref, mod_ref, nml_ref, gpm_ref, gpf_ref,
                  wua_ref, wum_ref, wo_ref, wr_ref, br_ref, x1_ref, h2_ref, idx_ref, gate_ref, cnt_ref):
    hsum = hf_ref[0] + hb_ref[0]
    parts = []
    for h in range(ML_HEADS):
        hh = hsum[:, h * ML_DV:(h + 1) * ML_DV]
        parts.append(hh * lax.rsqrt(jnp.mean(hh * hh, axis=-1, keepdims=True) + EPS))
    hn = jnp.concatenate(parts, axis=-1) * nml_ref[...]
    ml = (hn * _sigmoid(mo_ref[0])).astype(BF16)
    a = jnp.dot(att_ref[0], wua_ref[...], preferred_element_type=F32)
    m = jnp.dot(ml, wum_ref[...], preferred_element_type=F32)
    z = (_sigmoid(ga_ref[0]) * a + _sigmoid(gm_ref[0]) * m).astype(BF16)
    mix = jnp.dot(z, wo_ref[...], preferred_element_type=F32)
    gate1 = mod_ref[0, 2:3, :]
    shift2 = mod_ref[0, 3:4, :]
    scale2 = mod_ref[0, 4:5, :]
    x1 = x_ref[0] + gate1 * _rms(mix, gpm_ref[...])
    x1_ref[0] = x1
    h2 = _rms(x1, gpf_ref[...]) * (1.0 + scale2) + shift2
    h2_ref[0] = _pack_bf16_pairs(h2)

    h2_hi = h2.astype(BF16)
    h2_lo = (h2 - h2_hi.astype(F32)).astype(BF16)
    wr = wr_ref[...]
    wr_hi = wr.astype(BF16)
    wr_lo = (wr - wr_hi.astype(F32)).astype(BF16)
    logits = (jnp.dot(h2_hi, wr_hi, preferred_element_type=F32)
              + jnp.dot(h2_lo, wr_hi, preferred_element_type=F32)
              + jnp.dot(h2_hi, wr_lo, preferred_element_type=F32)) + br_ref[...]
    lane = lax.broadcasted_iota(jnp.int32, logits.shape, 1).astype(F32)
    work = jnp.where(lane < N_EXPERTS, logits, -jnp.inf)
    idx_out = jnp.zeros(logits.shape, F32)
    val_out = jnp.zeros(logits.shape, F32)
    picked = jnp.zeros(logits.shape, F32)
    top0 = None
    esum = None
    for k in range(TOP_K):
        mx = jnp.max(work, axis=-1, keepdims=True)
        sel = jnp.min(jnp.where(work == mx, lane, float(LANES)), axis=-1, keepdims=True)
        if k == 0:
            top0 = mx
        e = jnp.exp(mx - top0)
        esum = e if k == 0 else esum + e
        idx_out = jnp.where(lane == k, sel, idx_out)
        val_out = jnp.where(lane == k, e, val_out)
        picked = jnp.where(lane == sel, 1.0, picked)
        work = jnp.where(lane == sel, -jnp.inf, work)
    idx_ref[0] = idx_out.astype(jnp.int32)
    gate_ref[0] = val_out / esum
    cnt_ref[0] = jnp.sum(picked, axis=0, keepdims=True)


def _merge(att, hf, hb, pj, x, mod, mod_row, p, tm, b0):
    bsz = att.shape[0]
    _, seq, d = x.shape
    tok3 = lambda b, i: (b, i, 0)
    wr = jnp.pad(p["w_router"].astype(F32), ((0, 0), (0, LANES - N_EXPERTS)))
    br = jnp.pad(p["b_router"].astype(F32), (0, LANES - N_EXPERTS)).reshape(1, LANES)
    row = lambda v: v.astype(F32).reshape(1, -1)
    in_specs = [pl.BlockSpec((1, tm, Q_W), tok3),
                pl.BlockSpec((1, tm, MLV_W), tok3), pl.BlockSpec((1, tm, MLV_W), tok3),
                pl.BlockSpec((1, tm, MLV_W), tok3), pl.BlockSpec((1, tm, d), tok3), pl.BlockSpec((1, tm, d), tok3),
                pl.BlockSpec((1, tm, d), lambda b, i: (b + b0, i, 0)),
                pl.BlockSpec((1, 6, d), lambda b, i: (mod_row(b), 0, 0)),
                _const_spec((1, MLV_W)), _const_spec((1, d)), _const_spec((1, d)),
                _const_spec((Q_W, d)), _const_spec((MLV_W, d)), _const_spec((d, d)),
                _const_spec((d, LANES)), _const_spec((1, LANES))]
    out_shape = [jax.ShapeDtypeStruct((bsz, seq, d), F32), jax.ShapeDtypeStruct((bsz, seq, d // 2), jnp.int32),
                 jax.ShapeDtypeStruct((bsz, seq, LANES), jnp.int32), jax.ShapeDtypeStruct((bsz, seq, LANES), F32),
                 jax.ShapeDtypeStruct((bsz * (seq // tm), 1, LANES), F32)]
    out_specs = [pl.BlockSpec((1, tm, d), tok3), pl.BlockSpec((1, tm, d // 2), tok3),
                 pl.BlockSpec((1, tm, LANES), tok3), pl.BlockSpec((1, tm, LANES), tok3),
                 pl.BlockSpec((1, 1, LANES), lambda b, i: (b * (seq // tm) + i, 0, 0))]
    return pl.pallas_call(
        _merge_kernel,
        grid=(bsz, seq // tm),
        in_specs=in_specs,
        out_specs=out_specs,
        out_shape=out_shape,
        compiler_params=_cparams(("parallel", "parallel")),
        name="merge_router",
    )(att, hf, hb, pj["mo"], pj["ga"], pj["gm"], x, mod, row(p["norm_ml"]), row(p["g_post_mix"]),
      row(p["g_pre_ffn"]), p["w_up_att"].astype(BF16), p["w_up_ml"].astype(BF16), p["w_out"].astype(BF16), wr, br)


def _sc_gather_rows(table, idx):
    n = idx.shape[0]
    width = table.shape[1]
    chunk = SC_GATHER_CHUNK
    n_workers = SC_CORES * SC_SUBCORES
    rows_per_worker = n // n_workers
    n_pairs = rows_per_worker // (2 * chunk)
    assert n_pairs * 2 * chunk * n_workers == n
    mesh = plsc.VectorSubcoreMesh(core_axis_name="c", subcore_axis_name="s")

    def body(table_hbm, idx_hbm, out_hbm, idx_v, rows_v, gsem, osem):
        wid = lax.axis_index("s") * SC_CORES + lax.axis_index("c")
        base0 = wid * rows_per_worker

        def gather_copy(b):
            return pltpu.make_async_copy(table_hbm.at[idx_v.at[b]], rows_v.at[b], gsem.at[b])

        def write_copy(ci, b):
            base = pl.multiple_of(base0 + ci * chunk, 8)
            return pltpu.make_async_copy(rows_v.at[b], out_hbm.at[pl.ds(base, chunk)], osem.at[b])

        def issue(ci, b):
            base = pl.multiple_of(base0 + ci * chunk, 8)
            pltpu.sync_copy(idx_hbm.at[pl.ds(base, chunk)], idx_v.at[b])
            gather_copy(b).start()

        def finish(ci, b):
            gather_copy(b).wait()
            write_copy(ci, b).start()

        issue(0, 0)

        @pl.loop(0, n_pairs)
        def _(j):
            @pl.when(j > 0)
            def _():
                write_copy(2 * j - 1, 1).wait()
            issue(2 * j + 1, 1)
            finish(2 * j, 0)

            @pl.when(j < n_pairs - 1)
            def _():
                write_copy(2 * j, 0).wait()
                issue(2 * j + 2, 0)
            finish(2 * j + 1, 1)

        write_copy(2 * n_pairs - 2, 0).wait()
        write_copy(2 * n_pairs - 1, 1).wait()

    return pl.kernel(
        body, mesh=mesh,
        out_type=jax.ShapeDtypeStruct((n, width), table.dtype),
        scratch_types=[pltpu.VMEM((2, chunk), jnp.int32), pltpu.VMEM((2, chunk, width), table.dtype),
                       pltpu.SemaphoreType.DMA((2,)), pltpu.SemaphoreType.DMA((2,))],
    )(table, idx)


def _ffn_kernel(be_ref, nu_ref, xs_ref, w1_ref, b1_ref, w2_ref, b2_ref, ys_ref, w1b_s, w2b_s):
    i = pl.program_id(0)
    e = be_ref[i]
    e_prev = be_ref[jnp.maximum(i - 1, 0)]

    @pl.when((i == 0) | (e != e_prev))
    def _():
        w1b_s[...] = w1_ref[0].astype(BF16)
        w2b_s[...] = w2_ref[0].astype(BF16)

    @pl.when(i < nu_ref[0])
    def _():
        half = D_MODEL // 2
        x_lo, x_hi = _unpack_bf16_pairs(xs_ref[...])
        hmid = (jnp.dot(x_lo.astype(BF16), w1b_s[:half, :], preferred_element_type=F32)
                + jnp.dot(x_hi.astype(BF16), w1b_s[half:, :], preferred_element_type=F32)) + b1_ref[0]
        glu = jnp.minimum(hmid[:, :D_FF], SWIGLU_LIMIT)
        lin = jnp.clip(hmid[:, D_FF:], -SWIGLU_LIMIT, SWIGLU_LIMIT)
        act = ((lin + 1.0) * glu * _sigmoid(SWIGLU_ALPHA * glu)).astype(BF16)
        y = jnp.dot(act, w2b_s[...], preferred_element_type=F32) + b2_ref[0]
        ys_ref[...] = _pack_bf16_pairs(y)

    @pl.when(i >= nu_ref[0])
    def _():
        ys_ref[...] = jnp.zeros(ys_ref.shape, jnp.int32)


def _sc_scatter_rows(tables, positions, n_rows):
    width = tables[0].shape[1]
    chunk = SC_GATHER_CHUNK
    n_workers = SC_CORES * SC_SUBCORES
    n_streams = len(tables)
    mesh = plsc.VectorSubcoreMesh(core_axis_name="c", subcore_axis_name="s")

    def body(*refs):
        table_refs = refs[0:2 * n_streams:2]
        pos_refs = refs[1:2 * n_streams:2]
        out_hbm, idx_v, rows_v, lsem, ssem = refs[2 * n_streams:]
        wid = lax.axis_index("s") * SC_CORES + lax.axis_index("c")

        for table_hbm, pos_hbm in zip(table_refs, pos_refs):
            tok_per_worker = table_hbm.shape[0] // n_workers
            n_pairs = tok_per_worker // (2 * chunk)
            assert n_pairs * 2 * chunk * n_workers == table_hbm.shape[0]
            base0 = wid * tok_per_worker

            def load_copy(ci, b):
                base = pl.multiple_of(base0 + ci * chunk, 8)
                return pltpu.make_async_copy(table_hbm.at[pl.ds(base, chunk)], rows_v.at[b], lsem.at[b])

            def scatter_copy(b, k):
                return pltpu.make_async_copy(rows_v.at[b], out_hbm.at[idx_v.at[b, k]], ssem.at[b])

            def load(ci, b):
                base = pl.multiple_of(base0 + ci * chunk, 8)
                load_copy(ci, b).start()
                for k in range(TOP_K):
                    pltpu.sync_copy(pos_hbm.at[k, pl.ds(base, chunk)], idx_v.at[b, k])

            def scatter(ci, b):
                load_copy(ci, b).wait()
                for k in range(TOP_K):
                    scatter_copy(b, k).start()

            def drain(b):
                for k in range(TOP_K):
                    scatter_copy(b, k).wait()

            load(0, 0)

            @pl.loop(0, n_pairs)
            def _(j):
                @pl.when(j > 0)
                def _():
                    drain(1)
                load(2 * j + 1, 1)
                scatter(2 * j, 0)

                @pl.when(j < n_pairs - 1)
                def _():
                    drain(0)
                    load(2 * j + 2, 0)
                scatter(2 * j + 1, 1)

            drain(0)
            drain(1)

    args = [a for pair in zip(tables, positions) for a in pair]
    return pl.kernel(
        body, mesh=mesh,
        out_type=jax.ShapeDtypeStruct((n_rows, width), tables[0].dtype),
        scratch_types=[pltpu.VMEM((2, TOP_K, chunk), jnp.int32), pltpu.VMEM((2, chunk, width), tables[0].dtype),
                       pltpu.SemaphoreType.DMA((2,)), pltpu.SemaphoreType.DMA((2,))],
    )(*args)


def _pos_kernel(idx_ref, base_ref, pos_ref):
    idx = idx_ref[...]
    rows = idx.shape[0]
    lane = lax.broadcasted_iota(jnp.int32, idx.shape, 1)
    hots = [jnp.where(lane == idx[:, k:k + 1], 1.0, 0.0) for k in range(TOP_K)]
    cnt = hots[0] + hots[1] + hots[2] + hots[3]
    ri = lax.broadcasted_iota(jnp.int32, (rows, rows), 0)
    ci = lax.broadcasted_iota(jnp.int32, (rows, rows), 1)
    earlier = jnp.where(ci < ri, 1.0, 0.0).astype(BF16)
    ahead = jnp.dot(earlier, cnt.astype(BF16), preferred_element_type=F32) + base_ref[0]
    posmat = jnp.zeros(idx.shape, F32)
    for k in range(TOP_K):
        posmat = jnp.where(lane == k, jnp.sum(hots[k] * ahead, axis=-1, keepdims=True), posmat)
    pos_ref[...] = posmat.T[:TOP_K, :].astype(jnp.int32)


def _route(idx_list, count_list):
    tiles = [c.shape[0] for c in count_list]
    tile_tok = [idx.size // LANES // n for idx, n in zip(idx_list, tiles)]
    n_tok = sum(n * sz for n, sz in zip(tiles, tile_tok))
    n_blocks = n_tok * TOP_K // MOE_ROWS + N_EXPERTS
    cnt = jnp.concatenate([c.reshape(-1, LANES) for c in count_list], axis=0).astype(jnp.int32)
    tile_off = jnp.cumsum(cnt, axis=0) - cnt
    total = jnp.sum(cnt, axis=0)
    padded = (total + MOE_ROWS - 1) // MOE_ROWS * MOE_ROWS
    pend = jnp.cumsum(padded)
    base = ((pend - padded)[None, :] + tile_off).astype(F32)
    starts = jnp.arange(n_blocks, dtype=jnp.int32) * MOE_ROWS
    block_e = jnp.minimum(jnp.sum(pend[None, :N_EXPERTS] <= starts[:, None], axis=1), N_EXPERTS - 1).astype(jnp.int32)
    n_used = (pend[N_EXPERTS - 1] // MOE_ROWS).astype(jnp.int32).reshape(1)
    positions, first = [], 0
    for idx, n_tiles, tile in zip(idx_list, tiles, tile_tok):
        stream_tok = n_tiles * tile
        positions.append(pl.pallas_call(
            _pos_kernel,
            grid=(n_tiles,),
            in_specs=[pl.BlockSpec((tile, LANES), lambda i: (i, 0)),
                      pl.BlockSpec((1, 1, LANES), lambda i: (i, 0, 0))],
            out_specs=pl.BlockSpec((TOP_K, tile), lambda i: (0, i)),
            out_shape=jax.ShapeDtypeStruct((TOP_K, stream_tok), jnp.int32),
            compiler_params=_cparams(("parallel",)),
            name="route_pos",
        )(idx.reshape(stream_tok, LANES), base[first:first + n_tiles].reshape(n_tiles, 1, LANES)))
        first += n_tiles
    return block_e, n_used, positions


def _moe(h2p_list, idx_list, count_list, w1, b1, w2, b2):
    half = h2p_list[0].shape[1]
    d = 2 * half
    block_e, n_used, positions = _route(idx_list, count_list)
    n_blocks = block_e.shape[0]
    xs = _sc_scatter_rows(h2p_list, positions, n_blocks * MOE_ROWS)
    grid_spec = pltpu.PrefetchScalarGridSpec(
        num_scalar_prefetch=2,
        grid=(n_blocks,),
        in_specs=[pl.BlockSpec((MOE_ROWS, half), lambda i, be, nu: (i, 0)),
                  pl.BlockSpec((1, d, 2 * D_FF), lambda i, be, nu: (be[i], 0, 0)),
                  pl.BlockSpec((1, 1, 2 * D_FF), lambda i, be, nu: (be[i], 0, 0)),
                  pl.BlockSpec((1, D_FF, d), lambda i, be, nu: (be[i], 0, 0)),
                  pl.BlockSpec((1, 1, d), lambda i, be, nu: (be[i], 0, 0))],
        out_specs=pl.BlockSpec((MOE_ROWS, half), lambda i, be, nu: (i, 0)),
        scratch_shapes=[pltpu.VMEM((d, 2 * D_FF), BF16), pltpu.VMEM((D_FF, d), BF16)],
    )
    ys = pl.pallas_call(
        _ffn_kernel,
        grid_spec=grid_spec,
        out_shape=jax.ShapeDtypeStruct((n_blocks * MOE_ROWS, half), jnp.int32),
        compiler_params=_cparams(("arbitrary",)),
        name="moe_ffn",
    )(block_e, n_used, xs, w1, b1.reshape(N_EXPERTS, 1, -1), w2, b2.reshape(N_EXPERTS, 1, -1))
    return [_sc_gather_rows(ys, pos.reshape(-1)) for pos in positions]


def _final_kernel(x1_ref, y0_ref, y1_ref, y2_ref, y3_ref, gate_ref, mod_ref, g_ref, *rest):
    o_ref = rest[-1]
    gates = gate_ref[0]
    lo = hi = None
    for k, y_ref in enumerate((y0_ref, y1_ref, y2_ref, y3_ref)):
        y_lo, y_hi = _unpack_bf16_pairs(y_ref[0, 0])
        gk = gates[:, k:k + 1]
        lo = gk * y_lo if k == 0 else lo + gk * y_lo
        hi = gk * y_hi if k == 0 else hi + gk * y_hi
    y = jnp.concatenate([lo, hi], axis=-1)
    gate2 = mod_ref[0, 5:6, :]
    o_ref[0] = x1_ref[0] + gate2 * _rms(y, g_ref[...])


def _final(x1, yg, gate, mod, mod_row, g_post_ffn, tm, b0, n_batch, out_prev):
    bsz, seq, d = x1.shape
    tok3 = lambda b, i: (b, i, 0)
    yg = yg.reshape(TOP_K, bsz, seq, d // 2)
    slot_specs = [pl.BlockSpec((1, 1, tm, d // 2), functools.partial(lambda b, i, k: (k, b, i, 0), k=k))
                  for k in range(TOP_K)]
    in_specs = ([pl.BlockSpec((1, tm, d), tok3)] + slot_specs +
                [pl.BlockSpec((1, tm, LANES), tok3),
                 pl.BlockSpec((1, 6, d), lambda b, i: (mod_row(b), 0, 0)), _const_spec((1, d))])
    args = [x1, yg, yg, yg, yg, gate, mod, g_post_ffn.astype(F32).reshape(1, d)]
    aliases = {}
    if out_prev is not None:
        in_specs.append(pl.BlockSpec(memory_space=pl.ANY))
        args.append(out_prev)
        aliases = {len(args) - 1: 0}
    return pl.pallas_call(
        _final_kernel,
        grid=(bsz, seq // tm),
        in_specs=in_specs,
        out_specs=pl.BlockSpec((1, tm, d), lambda b, i: (b + b0, i, 0)),
        out_shape=jax.ShapeDtypeStruct((n_batch, seq, d), F32),
        input_output_aliases=aliases,
        compiler_params=_cparams(("parallel", "parallel")),
        name="final_residual",
    )(*args)


def _stream(x, mod, mod_row, p, ctx_kv, init_state, rope, tm, b0, bsz):
    _, seq, d = x.shape
    is_ctx = ctx_kv is None
    pj = _project(x, mod, p["g_pre_mix"], p["w_in"], p["b_gates"], mod_row, rope, is_ctx, tm, b0, bsz)
    if is_ctx:
        att = _attention(pj["q"], None, None, pj["kd"], pj["vd"], p["sink"], window=False)
    else:
        att = _attention(pj["q"], pj["kd"], pj["vd"], ctx_kv[0], ctx_kv[1], p["sink"], window=True)
    ml = _mlstm(pj["mq"], pj["mk"], pj["mv"], pj["gates"], init_state, emit_state=is_ctx, b0=b0)
    x1, h2, idx, gate, tile_counts = _merge(att, ml[0], ml[1], pj, x, mod, mod_row, p, tm, b0)
    moe_in = (h2.reshape(bsz * seq, d // 2), idx, tile_counts)
    return x1, gate, moe_in, pj, ml


def kernel(x_prompt, x_sample, c, cache_k, cache_v, state_C, state_n, state_m, c_ctx, w_ada, b_ada, g_pre_mix,
           w_in, b_gates, attn_sink, norm_mlstm, w_up_att, w_up_ml, w_out, g_post_mix, g_pre_ffn, w_router,
           b_router, w1, b1, w2, b2, g_post_ffn):
    depth = w_ada.shape[0]
    n_dec = c.shape[0]
    cond = jnp.concatenate([c_ctx[None, :], c], axis=0).astype(F32)
    cond = jnp.pad(cond, ((0, 16 - cond.shape[0]), (0, 0)))
    y_prompt, y_sample = x_prompt, x_sample
    ks_, vs_, cs_, ns_, ms_ = [], [], [], [], []
    for l in range(depth):
        p = dict(g_pre_mix=g_pre_mix[l], w_in=w_in[l], b_gates=b_gates[l], sink=attn_sink[l], norm_ml=norm_mlstm[l],
                 w_up_att=w_up_att[l], w_up_ml=w_up_ml[l], w_out=w_out[l], g_post_mix=g_post_mix[l],
                 g_pre_ffn=g_pre_ffn[l], w_router=w_router[l], b_router=b_router[l], w1=w1[l], b1=b1[l],
                 w2=w2[l], b2=b2[l], g_post_ffn=g_post_ffn[l])
        mod = _adaln(cond, w_ada[l], b_ada[l]).reshape(16, 6, D_MODEL)
        row_p = lambda b: 0
        bsz, seq = x_prompt.shape[:2]
        x1_p, gate_p, moe_p, pj, ml = _stream(y_prompt, mod, row_p, p, None, None, False, 256, 0, bsz)
        nat = pj["nat"]
        ks_.append(nat[..., :KV_W].reshape(bsz, seq, ATT_KV_HEADS, HEAD_DIM))
        vs_.append(nat[..., KV_W:].reshape(bsz, seq, ATT_KV_HEADS, HEAD_DIM))
        cs_.append(ml[2])
        ns_.append(ml[3])
        ms_.append(ml[4][..., 0])
        ctx_kv = (_dup_heads(cache_k[:, l], False), _dup_heads(cache_v[:, l], True))
        init = (state_C[:, l], state_n[:, l], state_m[:, l])
        half = n_dec // 2
        experts = (p["w1"], p["b1"].astype(F32), p["w2"], p["b2"].astype(F32))
        groups = []
        for b0 in (0, half):
            row_s = functools.partial(lambda b, off: b + off + 1, off=b0)
            sl = slice(b0, b0 + half)
            x1_s, gate_s, moe_s, _, _ = _stream(y_sample, mod, row_s, p, (ctx_kv[0][sl], ctx_kv[1][sl]),
                                                init, True, 512, b0, half)
            groups.append((x1_s, gate_s, moe_s, row_s))
        (x1_a, gate_a, moe_a, row_a), (x1_b, gate_b, moe_b, row_b) = groups
        yg_p, yg_a = _moe([moe_p[0], moe_a[0]], [moe_p[1], moe_a[1]], [moe_p[2], moe_a[2]], *experts)
        (yg_b,) = _moe([moe_b[0]], [moe_b[1]], [moe_b[2]], *experts)
        y_prompt = _final(x1_p, yg_p, gate_p, mod, row_p, p["g_post_ffn"], 256, 0, bsz, None)
        y_sample = _final(x1_a, yg_a, gate_a, mod, row_a, p["g_post_ffn"], 512, 0, n_dec, None)
        y_sample = _final(x1_b, yg_b, gate_b, mod, row_b, p["g_post_ffn"], 512, half, n_dec, y_sample)
    return (y_prompt, y_sample, jnp.stack(ks_, axis=1), jnp.stack(vs_, axis=1), jnp.stack(cs_, axis=1),
            jnp.stack(ns_, axis=1), jnp.stack(ms_, axis=1))
```

```python
import functools

import numpy as np
import jax
import jax.numpy as jnp
from jax import lax
from jax.experimental import pallas as pl
from jax.experimental.pallas import tpu as pltpu
from jax.experimental.pallas import tpu_sc as plsc

F32 = jnp.float32
BF16 = jnp.bfloat16

D_MODEL = 1024
GRID_W = 64
ATT_HEADS = 8
ATT_KV_HEADS = 2
ATT_GROUP = ATT_HEADS // ATT_KV_HEADS
HEAD_DIM = 64
BLOCK = 128
ROPE_THETA = 10000.0
AXIS_FREQS = HEAD_DIM // 4
ML_HEADS = 4
ML_DK = 128
ML_DV = 256
ML_CHUNK = 256
N_EXPERTS = 32
TOP_K = 4
D_FF = 1024
SWIGLU_ALPHA = 1.702
SWIGLU_LIMIT = 7.0
EPS = 1e-6
NEG = -1e30

Q_W = ATT_HEADS * HEAD_DIM
KV_W = ATT_KV_HEADS * HEAD_DIM
MLQK_W = ML_HEADS * ML_DK
MLV_W = ML_HEADS * ML_DV
GATE_W = 4 * ML_HEADS

LANES = 128
VMEM_LIMIT = 56 * 1024 * 1024
MOE_ROWS = 512
SC_CORES = 2
SC_SUBCORES = 16
SC_GATHER_CHUNK = 32


def _cparams(sem):
    return pltpu.CompilerParams(dimension_semantics=sem, vmem_limit_bytes=VMEM_LIMIT)


def _const_spec(shape):
    nd = len(shape)
    return pl.BlockSpec(shape, lambda *_: (0,) * nd)


def _adaln_kernel(cond_ref, w_ref, b_ref, o_ref):
    cnd = cond_ref[...]
    act = cnd * jax.nn.sigmoid(cnd)
    o_ref[...] = jnp.dot(act.astype(BF16), w_ref[...].astype(BF16),
                         preferred_element_type=F32) + b_ref[...]


def _adaln(cond, w_ada, b_ada):
    rows, d = cond.shape
    n = w_ada.shape[1]
    tn = 1536
    return pl.pallas_call(
        _adaln_kernel,
        grid=(n // tn,),
        in_specs=[_const_spec((rows, d)),
                  pl.BlockSpec((d, tn), lambda j: (0, j)),
                  pl.BlockSpec((1, tn), lambda j: (0, j))],
        out_specs=pl.BlockSpec((rows, tn), lambda j: (0, j)),
        out_shape=jax.ShapeDtypeStruct((rows, n), F32),
        compiler_params=_cparams(("parallel",)),
        name="adaln",
    )(cond, w_ada, b_ada.reshape(1, n))


def _pack_w_in(w_in, b_gates, rope, with_nat):
    o_q, o_k, o_v = 0, Q_W, Q_W + KV_W
    o_mq = Q_W + 2 * KV_W
    o_mk = o_mq + MLQK_W
    o_mv = o_mk + MLQK_W
    o_g = o_mv + MLV_W
    o_mo = o_g + GATE_W
    o_ga = o_mo + MLV_W
    o_gm = o_ga + D_MODEL
    half = HEAD_DIM // 2

    def head_cols(base, h, rot):
        lo = base + h * HEAD_DIM
        if rot:
            return [w_in[:, lo + half:lo + HEAD_DIM], w_in[:, lo:lo + half]]
        return [w_in[:, lo:lo + HEAD_DIM]]

    def q_cols(rot):
        cols = []
        for h in range(ATT_HEADS):
            cols += head_cols(o_q, h, rot)
        return cols

    def dup_cols(base, rot):
        cols = []
        for h in range(ATT_KV_HEADS):
            hc = head_cols(base, h, rot)
            cols += hc + hc
        return cols

    segs, cols, off = {}, [], 0

    def add(name, cl):
        nonlocal off
        width = sum(c.shape[1] for c in cl)
        segs[name] = (off, off + width)
        cols.extend(cl)
        off += width

    add("q", q_cols(False))
    if rope:
        add("qrot", q_cols(True))
    add("kd", dup_cols(o_k, False))
    if rope:
        add("kdrot", dup_cols(o_k, True))
    if with_nat:
        add("nat", [w_in[:, o_k:o_k + 2 * KV_W]])
    add("mq", [w_in[:, o_mq:o_mq + MLQK_W]])
    add("mv", [w_in[:, o_mv:o_mv + MLV_W]])
    add("mo", [w_in[:, o_mo:o_mo + MLV_W]])
    add("ga", [w_in[:, o_ga:o_ga + D_MODEL]])
    add("gm", [w_in[:, o_gm:o_gm + D_MODEL]])
    add("gates", [w_in[:, o_g:o_g + GATE_W], jnp.zeros((D_MODEL, LANES - GATE_W), w_in.dtype)])
    wp = jnp.concatenate(cols, axis=1).astype(BF16)
    wvd_t = jnp.concatenate(dup_cols(o_v, False) + [w_in[:, o_mk:o_mk + MLQK_W]], axis=1).T.astype(BF16)
    bg = jnp.pad(b_gates.astype(F32), (0, LANES - GATE_W)).reshape(1, LANES)
    return wp, wvd_t, bg, segs


def _rope_tables(n_tok):
    rows = n_tok // GRID_W
    row = np.repeat(np.arange(rows), GRID_W).astype(np.float32)
    col = np.tile(np.arange(GRID_W), rows).astype(np.float32)
    inv = (np.float32(ROPE_THETA) ** (-np.arange(AXIS_FREQS, dtype=np.float32) / AXIS_FREQS)).astype(np.float32)
    ang = np.concatenate([row[:, None] * inv, col[:, None] * inv], axis=-1).astype(np.float32)
    cos, sin = np.cos(ang), np.sin(ang)
    c64 = np.concatenate([cos, cos], axis=-1)
    s64 = np.concatenate([-sin, sin], axis=-1)
    return (jnp.asarray(np.tile(c64, (1, 2)), F32), jnp.asarray(np.tile(s64, (1, 2)), F32))


def _proj_kernel(*refs, segs, rope, with_nat):
    it = iter(refs)
    x_ref, mod_ref, g_ref, w_ref, wvt_ref, bg_ref = (next(it) for _ in range(6))
    cos_ref = sin_ref = None
    if rope:
        cos_ref, sin_ref = next(it), next(it)
    q_ref, kd_ref, vd_ref = next(it), next(it), next(it)
    nat_ref = next(it) if with_nat else None
    mq_ref, mk_ref, mv_ref, gates_ref, mo_ref, ga_ref, gm_ref = (next(it) for _ in range(7))

    x = x_ref[0]
    shift = mod_ref[0, 0:1, :]
    scale = mod_ref[0, 1:2, :]
    ms = jnp.mean(x * x, axis=-1, keepdims=True)
    xn = x * lax.rsqrt(ms + EPS) * g_ref[...]
    xb = (xn * (1.0 + scale) + shift).astype(BF16)

    def seg(name):
        lo, hi = segs[name]
        return jnp.dot(xb, w_ref[:, lo:hi], preferred_element_type=F32)

    uq = seg("q")
    ukd = seg("kd")
    if rope:
        uqr = seg("qrot")
        ukr = seg("kdrot")
        cs = cos_ref[...]
        sn = sin_ref[...]
        for h in range(ATT_HEADS // 2):
            sl = slice(h * LANES, (h + 1) * LANES)
            q_ref[0, h] = ((uq[:, sl] * cs + uqr[:, sl] * sn) * HEAD_DIM ** -0.5).astype(BF16)
        for h in range(ATT_KV_HEADS):
            sl = slice(h * LANES, (h + 1) * LANES)
            kd_ref[0, :, sl] = (ukd[:, sl] * cs + ukr[:, sl] * sn).astype(BF16)
    else:
        for h in range(ATT_HEADS // 2):
            sl = slice(h * LANES, (h + 1) * LANES)
            q_ref[0, h] = (uq[:, sl] * HEAD_DIM ** -0.5).astype(BF16)
        kd_ref[0] = ukd.astype(BF16)
    ut = lax.dot_general(wvt_ref[...], xb, (((1,), (1,)), ((), ())), preferred_element_type=F32)
    vd_ref[0] = ut[:2 * LANES].astype(BF16)
    mk_ref[0] = ut[2 * LANES:].astype(BF16)
    if with_nat:
        nat_ref[0] = seg("nat")
    mq_ref[0] = (seg("mq") * ML_DK ** -0.5).astype(BF16)
    mv_ref[0] = seg("mv").astype(BF16)
    gates_ref[0] = seg("gates") + bg_ref[...]
    mo_ref[0] = seg("mo")
    ga_ref[0] = seg("ga")
    gm_ref[0] = seg("gm")


def _project(x, mod, g_pre, w_in, b_gates, mod_row, rope, with_nat, tm, b0, bsz):
    _, seq, d = x.shape
    wp, wvd_t, bg, segs = _pack_w_in(w_in, b_gates, rope, with_nat)
    nw = wp.shape[1]
    tok3 = lambda b, i: (b, i, 0)
    in_specs = [pl.BlockSpec((1, tm, d), lambda b, i: (b + b0, i, 0)),
                pl.BlockSpec((1, 6, d), lambda b, i: (mod_row(b), 0, 0)),
                _const_spec((1, d)),
                pl.BlockSpec((d, nw), lambda b, i: (0, 0), pipeline_mode=pl.Buffered(1)),
                _const_spec((2 * LANES + MLQK_W, d)), _const_spec((1, LANES))]
    args = [x, mod, g_pre.reshape(1, d), wp, wvd_t, bg]
    if rope:
        cos_t, sin_t = _rope_tables(seq)
        in_specs += [pl.BlockSpec((tm, LANES), lambda b, i: (i, 0))] * 2
        args += [cos_t, sin_t]

    def tok_out(width, dtype):
        return (jax.ShapeDtypeStruct((bsz, seq, width), dtype), pl.BlockSpec((1, tm, width), tok3))

    outs = [(jax.ShapeDtypeStruct((bsz, ATT_HEADS // 2, seq, LANES), BF16),
             pl.BlockSpec((1, ATT_HEADS // 2, tm, LANES), lambda b, i: (b, 0, i, 0))),
            tok_out(2 * LANES, BF16),
            (jax.ShapeDtypeStruct((bsz, 2 * LANES, seq), BF16),
             pl.BlockSpec((1, 2 * LANES, tm), lambda b, i: (b, 0, i)))]
    if with_nat:
        outs.append(tok_out(2 * KV_W, F32))
    outs += [tok_out(MLQK_W, BF16),
             (jax.ShapeDtypeStruct((bsz, MLQK_W, seq), BF16),
              pl.BlockSpec((1, MLQK_W, tm), lambda b, i: (b, 0, i))),
             tok_out(MLV_W, BF16), tok_out(LANES, F32),
             tok_out(MLV_W, F32), tok_out(D_MODEL, F32), tok_out(D_MODEL, F32)]
    res = pl.pallas_call(
        functools.partial(_proj_kernel, segs=segs, rope=rope, with_nat=with_nat),
        grid=(bsz, seq // tm),
        in_specs=in_specs,
        out_specs=[o[1] for o in outs],
        out_shape=[o[0] for o in outs],
        compiler_params=_cparams(("parallel", "parallel")),
        name="proj_rope" if rope else "proj_ctx",
    )(*args)
    names = ["q", "kd", "vd"] + (["nat"] if with_nat else []) + ["mq", "mk", "mv", "gates", "mo", "ga", "gm"]
    return dict(zip(names, res))


def _attn_kernel(*refs, window, n_blocks, n_ctx):
    it = iter(refs)
    sink_ref, q_ref = next(it), next(it)
    if window:
        kp_ref, kc_ref, kn_ref, vp_ref, vc_ref, vn_ref = (next(it) for _ in range(6))
    ck_ref, cv_ref, o_ref = next(it), next(it), next(it)
    j = pl.program_id(1)
    n_win = 3 * BLOCK if window else 0
    cols = ATT_GROUP * BLOCK

    col_id = lax.broadcasted_iota(jnp.int32, (1, cols), 1)
    if window:
        kl = lax.broadcasted_iota(jnp.int32, (BLOCK, cols), 0)
        ql = lax.broadcasted_iota(jnp.int32, (BLOCK, cols), 1) & (BLOCK - 1)
        prev_ok = kl >= ql + jnp.where(j == 0, BLOCK, 0)
        next_ok = kl <= ql - jnp.where(j == n_blocks - 1, BLOCK, 0)
    lane_lo = lax.broadcasted_iota(jnp.int32, (BLOCK, LANES), 1) < HEAD_DIM
    lane_row = lax.broadcasted_iota(jnp.int32, (1, LANES), 1) < HEAD_DIM
    half_lo = jnp.where(lane_row, 1.0, 0.0).astype(BF16)
    half_hi = jnp.where(lane_row, 0.0, 1.0).astype(BF16)

    for kvh in range(ATT_KV_HEADS):
        sl = slice(kvh * LANES, (kvh + 1) * LANES)
        if window:
            keys = jnp.concatenate([kp_ref[0, :, sl], kc_ref[0, :, sl], kn_ref[0, :, sl], ck_ref[0, :, sl]], axis=0)
            vals_t = jnp.concatenate([vp_ref[0, sl, :], vc_ref[0, sl, :], vn_ref[0, sl, :], cv_ref[0, sl, :]], axis=1)
        else:
            keys = ck_ref[0, :, sl]
            vals_t = cv_ref[0, sl, :]
        qs = jnp.concatenate([q_ref[0, kvh * (ATT_GROUP // 2) + g // 2] * (half_lo if g % 2 == 0 else half_hi)
                              for g in range(ATT_GROUP)], axis=0)
        s = lax.dot_general(keys, qs, (((1,), (1,)), ((), ())), preferred_element_type=F32)
        if window:
            s = jnp.concatenate([jnp.where(prev_ok, s[:BLOCK], NEG), s[BLOCK:2 * BLOCK],
                                 jnp.where(next_ok, s[2 * BLOCK:n_win], NEG), s[n_win:]], axis=0)
        snk = jnp.full((1, cols), sink_ref[kvh * ATT_GROUP + ATT_GROUP - 1], F32)
        for g in range(ATT_GROUP - 2, -1, -1):
            snk = jnp.where(col_id < (g + 1) * BLOCK, sink_ref[kvh * ATT_GROUP + g], snk)
        m = jnp.maximum(jnp.max(s, axis=0, keepdims=True), snk)
        p = jnp.exp(s - m)
        den = jnp.sum(p, axis=0, keepdims=True) + jnp.exp(snk - m)
        o_t = jnp.dot(vals_t, p.astype(BF16), preferred_element_type=F32) / den
        for pair in range(ATT_GROUP // 2):
            even = o_t[:, (2 * pair) * BLOCK:(2 * pair + 1) * BLOCK].T
            odd = o_t[:, (2 * pair + 1) * BLOCK:(2 * pair + 2) * BLOCK].T
            col = (kvh * (ATT_GROUP // 2) + pair) * LANES
            o_ref[0, :, col:col + LANES] = jnp.where(lane_lo, even, odd).astype(BF16)


def _attention(q, kd, vd_t, ckd, cvd_t, sink, window):
    bsz, _, seq, _ = q.shape
    nb = seq // BLOCK
    n_ctx = ckd.shape[1]
    in_specs = [pl.BlockSpec(memory_space=pltpu.SMEM),
                pl.BlockSpec((1, ATT_HEADS // 2, BLOCK, LANES), lambda b, j: (b, 0, j, 0))]
    args = [sink.astype(F32), q]
    if window:
        in_specs += [pl.BlockSpec((1, BLOCK, 2 * LANES), lambda b, j: (b, jnp.maximum(j - 1, 0), 0)),
                     pl.BlockSpec((1, BLOCK, 2 * LANES), lambda b, j: (b, j, 0)),
                     pl.BlockSpec((1, BLOCK, 2 * LANES), lambda b, j: (b, jnp.minimum(j + 1, nb - 1), 0)),
                     pl.BlockSpec((1, 2 * LANES, BLOCK), lambda b, j: (b, 0, jnp.maximum(j - 1, 0))),
                     pl.BlockSpec((1, 2 * LANES, BLOCK), lambda b, j: (b, 0, j)),
                     pl.BlockSpec((1, 2 * LANES, BLOCK), lambda b, j: (b, 0, jnp.minimum(j + 1, nb - 1)))]
        args += [kd, kd, kd, vd_t, vd_t, vd_t]
    in_specs += [pl.BlockSpec((1, n_ctx, 2 * LANES), lambda b, j: (b, 0, 0)),
                 pl.BlockSpec((1, 2 * LANES, n_ctx), lambda b, j: (b, 0, 0))]
    args += [ckd, cvd_t]
    return pl.pallas_call(
        functools.partial(_attn_kernel, window=window, n_blocks=nb, n_ctx=n_ctx),
        grid=(bsz, nb),
        in_specs=in_specs,
        out_specs=pl.BlockSpec((1, BLOCK, Q_W), lambda b, j: (b, j, 0)),
        out_shape=jax.ShapeDtypeStruct((bsz, seq, Q_W), BF16),
        compiler_params=_cparams(("parallel", "parallel")),
        name="attn_window" if window else "attn_ctx",
    )(*args)


def _dup_heads(t, transpose):
    b, l = t.shape[:2]
    out = jnp.concatenate([t, t], axis=-1).reshape(b, l, ATT_KV_HEADS * LANES).astype(BF16)
    return jnp.swapaxes(out, 1, 2) if transpose else out


def _split3(x):
    hi = x.astype(BF16)
    r1 = x - hi.astype(F32)
    mid = r1.astype(BF16)
    lo = (r1 - mid.astype(F32)).astype(BF16)
    return hi, mid, lo


def _log_sigmoid(x):
    return jnp.minimum(x, 0.0) - jnp.log1p(jnp.exp(-jnp.abs(x)))


def _mlstm_kernel(*refs, n_chunks, has_init, emit_state):
    it = iter(refs)
    qkvg = [[next(it) for _ in range(4)] for _ in range(2)]
    if has_init:
        c0_ref, n0_ref, m0_ref = next(it), next(it), next(it)
    h_refs = [next(it), next(it)]
    if emit_state:
        co_ref, no_ref, mo_ref = next(it), next(it), next(it)
    st_s, m_s = next(it), next(it)
    t = pl.program_id(1)
    L = ML_CHUNK

    @pl.when(t == 0)
    def _():
        for d in range(2):
            for h in range(ML_HEADS):
                if has_init:
                    st_s[d, h, :, :ML_DV] = c0_ref[0, d, h].T
                    st_s[d, h, :, ML_DV:] = jnp.broadcast_to(n0_ref[0, d, h:h + 1, :], (LANES, ML_DK)).T
                    m_s[d, h] = m0_ref[0, d, h:h + 1, :]
                else:
                    st_s[d, h] = jnp.zeros((ML_DK, ML_DV + LANES), F32)
                    m_s[d, h] = jnp.zeros((1, LANES), F32)

    ri = lax.broadcasted_iota(jnp.int32, (L, L), 0)
    ci = lax.broadcasted_iota(jnp.int32, (L, L), 1)
    ones_blk = jnp.ones((L, LANES), BF16)

    for d in range(2):
        q_ref, kt_ref, v_ref, g_ref = qkvg[d]
        tri = (ci <= ri) if d == 0 else (ci >= ri)
        tri_b = jnp.where(tri, 1.0, 0.0).astype(BF16)
        g = g_ref[0]
        lf = _log_sigmoid(g)
        g_t = g.T
        lf_t = lf.T
        bc = sum(jnp.dot(tri_b, part, preferred_element_type=F32) for part in _split3(lf))
        br = sum(lax.dot_general(part, tri_b, (((1,), (1,)), ((), ())), preferred_element_type=F32)
                 for part in _split3(lf_t))
        tot = jnp.sum(lf, axis=0, keepdims=True)

        for h in range(ML_HEADS):
            icol_i = 2 * ML_HEADS * d + h
            fcol = icol_i + ML_HEADS
            bcol = bc[:, fcol:fcol + 1]
            brow = br[fcol:fcol + 1, :]
            irow = g_t[icol_i:icol_i + 1, :]
            b_last = tot[:, fcol:fcol + 1]
            m_prev = m_s[d, h][:, 0:1]
            st_prev = st_s[d, h]
            qh = q_ref[0, :, h * ML_DK:(h + 1) * ML_DK]
            kt = kt_ref[0, h * ML_DK:(h + 1) * ML_DK, :]
            v_aug = jnp.concatenate([v_ref[0, :, h * ML_DV:(h + 1) * ML_DV], ones_blk], axis=1)

            rel = jnp.where(tri, irow - brow, NEG)
            inter = bcol + m_prev
            m_t = jnp.maximum(inter, bcol + jnp.max(rel, axis=-1, keepdims=True))
            w = jnp.exp(rel + (bcol - m_t))
            w_inter = jnp.exp(inter - m_t)
            s = jnp.dot(qh, kt, preferred_element_type=F32) * w
            lhs = jnp.concatenate([s.astype(BF16), (qh.astype(F32) * w_inter).astype(BF16)], axis=1)
            rhs = jnp.concatenate([v_aug, st_prev.astype(BF16)], axis=0)
            mix = jnp.dot(lhs, rhs, preferred_element_type=F32)
            inv = 1.0 / jnp.maximum(jnp.abs(mix[:, ML_DV:]), jnp.exp(-m_t))
            for c in range(ML_DV // LANES):
                lo = h * ML_DV + c * LANES
                h_refs[d][0, :, lo:lo + LANES] = mix[:, c * LANES:(c + 1) * LANES] * inv

            a = b_last - brow + irow
            m_new = jnp.maximum(b_last + m_prev, jnp.max(a, axis=-1, keepdims=True))
            wk = jnp.exp(a - m_new)
            decay = jnp.exp(b_last + m_prev - m_new)
            kw_t = (kt.astype(F32) * wk).astype(BF16)
            st_new = decay * st_prev + jnp.dot(kw_t, v_aug, preferred_element_type=F32)
            st_s[d, h] = st_new
            m_s[d, h] = jnp.broadcast_to(m_new, (1, LANES))
            if emit_state:
                @pl.when(t == n_chunks - 1)
                def _():
                    co_ref[0, d, h] = st_new[:, :ML_DV].T
                    no_ref[0, d, h:h + 1, :] = st_new[:, ML_DV:].T[0:1, :]
                    mo_ref[0, d, h:h + 1, :] = jnp.broadcast_to(m_new, (1, LANES))


def _mlstm(mq, mk_t, mv, gates, init, emit_state, b0=0):
    bsz, seq, _ = mq.shape
    nc = seq // ML_CHUNK
    fwd = lambda b, t: (b, t, 0)
    bwd = lambda b, t: (b, nc - 1 - t, 0)
    fwd_t = lambda b, t: (b, 0, t)
    bwd_t = lambda b, t: (b, 0, nc - 1 - t)
    state5 = lambda b, t: (b, 0, 0, 0, 0)
    state4 = lambda b, t: (b, 0, 0, 0)
    init5 = lambda b, t: (b + b0, 0, 0, 0, 0)
    init4 = lambda b, t: (b + b0, 0, 0, 0)
    in_specs, args = [], []
    for chunk, chunk_t in ((fwd, fwd_t), (bwd, bwd_t)):
        in_specs += [pl.BlockSpec((1, ML_CHUNK, MLQK_W), chunk), pl.BlockSpec((1, MLQK_W, ML_CHUNK), chunk_t),
                     pl.BlockSpec((1, ML_CHUNK, MLV_W), chunk), pl.BlockSpec((1, ML_CHUNK, LANES), chunk)]
        args += [mq, mk_t, mv, gates]
    if init is not None:
        c0, n0, m0 = init
        in_specs += [pl.BlockSpec((1, 2, ML_HEADS, ML_DV, ML_DK), init5),
                     pl.BlockSpec((1, 2, ML_HEADS, ML_DK), init4),
                     pl.BlockSpec((1, 2, ML_HEADS, LANES), init4)]
        args += [c0.astype(F32), n0.astype(F32),
                 jnp.broadcast_to(m0.astype(F32)[..., None], m0.shape + (LANES,))]
    out_shape = [jax.ShapeDtypeStruct((bsz, seq, MLV_W), F32)] * 2
    out_specs = [pl.BlockSpec((1, ML_CHUNK, MLV_W), fwd), pl.BlockSpec((1, ML_CHUNK, MLV_W), bwd)]
    if emit_state:
        out_shape += [jax.ShapeDtypeStruct((bsz, 2, ML_HEADS, ML_DV, ML_DK), F32),
                      jax.ShapeDtypeStruct((bsz, 2, ML_HEADS, ML_DK), F32),
                      jax.ShapeDtypeStruct((bsz, 2, ML_HEADS, LANES), F32)]
        out_specs += [pl.BlockSpec((1, 2, ML_HEADS, ML_DV, ML_DK), state5),
                      pl.BlockSpec((1, 2, ML_HEADS, ML_DK), state4),
                      pl.BlockSpec((1, 2, ML_HEADS, LANES), state4)]
    return pl.pallas_call(
        functools.partial(_mlstm_kernel, n_chunks=nc, has_init=init is not None, emit_state=emit_state),
        grid=(bsz, nc),
        in_specs=in_specs,
        out_specs=out_specs,
        out_shape=out_shape,
        scratch_shapes=[pltpu.VMEM((2, ML_HEADS, ML_DK, ML_DV + LANES), F32),
                        pltpu.VMEM((2, ML_HEADS, 1, LANES), F32)],
        compiler_params=_cparams(("parallel", "arbitrary")),
        name="mlstm_state" if emit_state else "mlstm",
    )(*args)


def _sigmoid(x):
    return 0.5 * jnp.tanh(0.5 * x) + 0.5


def _rms(x, g):
    return x * lax.rsqrt(jnp.mean(x * x, axis=-1, keepdims=True) + EPS) * g


def _pack_bf16_pairs(x):
    n = x.shape[1] // 2
    lo = pltpu.bitcast(x[:, :n].astype(BF16).astype(F32), jnp.int32)
    hi = pltpu.bitcast(x[:, n:].astype(BF16).astype(F32), jnp.int32)
    return lax.shift_right_logical(lo, 16) | hi


def _unpack_bf16_pairs(p):
    lo = pltpu.bitcast(lax.shift_left(p, 16), F32)
    hi = pltpu.bitcast(p & jnp.int32(-65536), F32)
    return lo, hi


def _merge_kernel(att_ref, hf_ref, hb_ref, mo_ref, ga_ref, gm_ref, x_ref, mod_ref, nml_ref, gpm_ref, gpf_ref,
                  wua_ref, wum_ref, wo_ref, wr_ref, br_ref, x1_ref, h2_ref, idx_ref, gate_ref, cnt_ref):
    hsum = hf_ref[0] + hb_ref[0]
    parts = []
    for h in range(ML_HEADS):
        hh = hsum[:, h * ML_DV:(h + 1) * ML_DV]
        parts.append(hh * lax.rsqrt(jnp.mean(hh * hh, axis=-1, keepdims=True) + EPS))
    hn = jnp.concatenate(parts, axis=-1) * nml_ref[...]
    ml = (hn * _sigmoid(mo_ref[0])).astype(BF16)
    a = jnp.dot(att_ref[0], wua_ref[...], preferred_element_type=F32)
    m = jnp.dot(ml, wum_ref[...], preferred_element_type=F32)
    z = (_sigmoid(ga_ref[0]) * a + _sigmoid(gm_ref[0]) * m).astype(BF16)
    mix = jnp.dot(z, wo_ref[...], preferred_element_type=F32)
    gate1 = mod_ref[0, 2:3, :]
    shift2 = mod_ref[0, 3:4, :]
    scale2 = mod_ref[0, 4:5, :]
    x1 = x_ref[0] + gate1 * _rms(mix, gpm_ref[...])
    x1_ref[0] = x1
    h2 = _rms(x1, gpf_ref[...]) * (1.0 + scale2) + shift2
    h2_ref[0] = _pack_bf16_pairs(h2)

    h2_hi = h2.astype(BF16)
    h2_lo = (h2 - h2_hi.astype(F32)).astype(BF16)
    wr = wr_ref[...]
    wr_hi = wr.astype(BF16)
    wr_lo = (wr - wr_hi.astype(F32)).astype(BF16)
    logits = (jnp.dot(h2_hi, wr_hi, preferred_element_type=F32)
              + jnp.dot(h2_lo, wr_hi, preferred_element_type=F32)
              + jnp.dot(h2_hi, wr_lo, preferred_element_type=F32)) + br_ref[...]
    lane = lax.broadcasted_iota(jnp.int32, logits.shape, 1).astype(F32)
    work = jnp.where(lane < N_EXPERTS, logits, -jnp.inf)
    idx_out = jnp.zeros(logits.shape, F32)
    val_out = jnp.zeros(logits.shape, F32)
    picked = jnp.zeros(logits.shape, F32)
    top0 = None
    esum = None
    for k in range(TOP_K):
        mx = jnp.max(work, axis=-1, keepdims=True)
        sel = jnp.min(jnp.where(work == mx, lane, float(LANES)), axis=-1, keepdims=True)
        if k == 0:
            top0 = mx
        e = jnp.exp(mx - top0)
        esum = e if k == 0 else esum + e
        idx_out = jnp.where(lane == k, sel, idx_out)
        val_out = jnp.where(lane == k, e, val_out)
        picked = jnp.where(lane == sel, 1.0, picked)
        work = jnp.where(lane == sel, -jnp.inf, work)
    idx_ref[0] = idx_out.astype(jnp.int32)
    gate_ref[0] = val_out / esum
    cnt_ref[0] = jnp.sum(picked, axis=0, keepdims=True)


def _merge(att, hf, hb, pj, x, mod, mod_row, p, tm, b0):
    bsz = att.shape[0]
    _, seq, d = x.shape
    tok3 = lambda b, i: (b, i, 0)
    wr = jnp.pad(p["w_router"].astype(F32), ((0, 0), (0, LANES - N_EXPERTS)))
    br = jnp.pad(p["b_router"].astype(F32), (0, LANES - N_EXPERTS)).reshape(1, LANES)
    row = lambda v: v.astype(F32).reshape(1, -1)
    in_specs = [pl.BlockSpec((1, tm, Q_W), tok3),
                pl.BlockSpec((1, tm, MLV_W), tok3), pl.BlockSpec((1, tm, MLV_W), tok3),
                pl.BlockSpec((1, tm, MLV_W), tok3), pl.BlockSpec((1, tm, d), tok3), pl.BlockSpec((1, tm, d), tok3),
                pl.BlockSpec((1, tm, d), lambda b, i: (b + b0, i, 0)),
                pl.BlockSpec((1, 6, d), lambda b, i: (mod_row(b), 0, 0)),
                _const_spec((1, MLV_W)), _const_spec((1, d)), _const_spec((1, d)),
                _const_spec((Q_W, d)), _const_spec((MLV_W, d)), _const_spec((d, d)),
                _const_spec((d, LANES)), _const_spec((1, LANES))]
    out_shape = [jax.ShapeDtypeStruct((bsz, seq, d), F32), jax.ShapeDtypeStruct((bsz, seq, d // 2), jnp.int32),
                 jax.ShapeDtypeStruct((bsz, seq, LANES), jnp.int32), jax.ShapeDtypeStruct((bsz, seq, LANES), F32),
                 jax.ShapeDtypeStruct((bsz * (seq // tm), 1, LANES), F32)]
    out_specs = [pl.BlockSpec((1, tm, d), tok3), pl.BlockSpec((1, tm, d // 2), tok3),
                 pl.BlockSpec((1, tm, LANES), tok3), pl.BlockSpec((1, tm, LANES), tok3),
                 pl.BlockSpec((1, 1, LANES), lambda b, i: (b * (seq // tm) + i, 0, 0))]
    return pl.pallas_call(
        _merge_kernel,
        grid=(bsz, seq // tm),
        in_specs=in_specs,
        out_specs=out_specs,
        out_shape=out_shape,
        compiler_params=_cparams(("parallel", "parallel")),
        name="merge_router",
    )(att, hf, hb, pj["mo"], pj["ga"], pj["gm"], x, mod, row(p["norm_ml"]), row(p["g_post_mix"]),
      row(p["g_pre_ffn"]), p["w_up_att"].astype(BF16), p["w_up_ml"].astype(BF16), p["w_out"].astype(BF16), wr, br)


def _sc_gather_rows(table, idx):
    n = idx.shape[0]
    width = table.shape[1]
    chunk = SC_GATHER_CHUNK
    n_workers = SC_CORES * SC_SUBCORES
    rows_per_worker = n // n_workers
    n_pairs = rows_per_worker // (2 * chunk)
    assert n_pairs * 2 * chunk * n_workers == n
    mesh = plsc.VectorSubcoreMesh(core_axis_name="c", subcore_axis_name="s")

    def body(table_hbm, idx_hbm, out_hbm, idx_v, rows_v, gsem, osem):
        wid = lax.axis_index("s") * SC_CORES + lax.axis_index("c")
        base0 = wid * rows_per_worker

        def gather_copy(b):
            return pltpu.make_async_copy(table_hbm.at[idx_v.at[b]], rows_v.at[b], gsem.at[b])

        def write_copy(ci, b):
            base = pl.multiple_of(base0 + ci * chunk, 8)
            return pltpu.make_async_copy(rows_v.at[b], out_hbm.at[pl.ds(base, chunk)], osem.at[b])

        def issue(ci, b):
            base = pl.multiple_of(base0 + ci * chunk, 8)
            pltpu.sync_copy(idx_hbm.at[pl.ds(base, chunk)], idx_v.at[b])
            gather_copy(b).start()

        def finish(ci, b):
            gather_copy(b).wait()
            write_copy(ci, b).start()

        issue(0, 0)

        @pl.loop(0, n_pairs)
        def _(j):
            @pl.when(j > 0)
            def _():
                write_copy(2 * j - 1, 1).wait()
            issue(2 * j + 1, 1)
            finish(2 * j, 0)

            @pl.when(j < n_pairs - 1)
            def _():
                write_copy(2 * j, 0).wait()
                issue(2 * j + 2, 0)
            finish(2 * j + 1, 1)

        write_copy(2 * n_pairs - 2, 0).wait()
        write_copy(2 * n_pairs - 1, 1).wait()

    return pl.kernel(
        body, mesh=mesh,
        out_type=jax.ShapeDtypeStruct((n, width), table.dtype),
        scratch_types=[pltpu.VMEM((2, chunk), jnp.int32), pltpu.VMEM((2, chunk, width), table.dtype),
                       pltpu.SemaphoreType.DMA((2,)), pltpu.SemaphoreType.DMA((2,))],
    )(table, idx)


def _ffn_kernel(be_ref, nu_ref, xs_ref, w1_ref, b1_ref, w2_ref, b2_ref, ys_ref, w1b_s, w2b_s):
    i = pl.program_id(0)
    e = be_ref[i]
    e_prev = be_ref[jnp.maximum(i - 1, 0)]

    @pl.when((i == 0) | (e != e_prev))
    def _():
        w1b_s[...] = w1_ref[0].astype(BF16)
        w2b_s[...] = w2_ref[0].astype(BF16)

    @pl.when(i < nu_ref[0])
    def _():
        half = D_MODEL // 2
        x_lo, x_hi = _unpack_bf16_pairs(xs_ref[...])
        hmid = (jnp.dot(x_lo.astype(BF16), w1b_s[:half, :], preferred_element_type=F32)
                + jnp.dot(x_hi.astype(BF16), w1b_s[half:, :], preferred_element_type=F32)) + b1_ref[0]
        glu = jnp.minimum(hmid[:, :D_FF], SWIGLU_LIMIT)
        lin = jnp.clip(hmid[:, D_FF:], -SWIGLU_LIMIT, SWIGLU_LIMIT)
        act = ((lin + 1.0) * glu * _sigmoid(SWIGLU_ALPHA * glu)).astype(BF16)
        y = jnp.dot(act, w2b_s[...], preferred_element_type=F32) + b2_ref[0]
        ys_ref[...] = _pack_bf16_pairs(y)

    @pl.when(i >= nu_ref[0])
    def _():
        ys_ref[...] = jnp.zeros(ys_ref.shape, jnp.int32)


def _sc_scatter_rows(tables, positions, n_rows):
    width = tables[0].shape[1]
    chunk = SC_GATHER_CHUNK
    n_workers = SC_CORES * SC_SUBCORES
    n_streams = len(tables)
    mesh = plsc.VectorSubcoreMesh(core_axis_name="c", subcore_axis_name="s")

    def body(*refs):
        table_refs = refs[0:2 * n_streams:2]
        pos_refs = refs[1:2 * n_streams:2]
        out_hbm, idx_v, rows_v, lsem, ssem = refs[2 * n_streams:]
        wid = lax.axis_index("s") * SC_CORES + lax.axis_index("c")

        for table_hbm, pos_hbm in zip(table_refs, pos_refs):
            tok_per_worker = table_hbm.shape[0] // n_workers
            n_pairs = tok_per_worker // (2 * chunk)
            assert n_pairs * 2 * chunk * n_workers == table_hbm.shape[0]
            base0 = wid * tok_per_worker

            def load_copy(ci, b):
                base = pl.multiple_of(base0 + ci * chunk, 8)
                return pltpu.make_async_copy(table_hbm.at[pl.ds(base, chunk)], rows_v.at[b], lsem.at[b])

            def scatter_copy(b, k):
                return pltpu.make_async_copy(rows_v.at[b], out_hbm.at[idx_v.at[b, k]], ssem.at[b])

            def load(ci, b):
                base = pl.multiple_of(base0 + ci * chunk, 8)
                load_copy(ci, b).start()
                for k in range(TOP_K):
                    pltpu.sync_copy(pos_hbm.at[k, pl.ds(base, chunk)], idx_v.at[b, k])

            def scatter(ci, b):
                load_copy(ci, b).wait()
                for k in range(TOP_K):
                    scatter_copy(b, k).start()

            def drain(b):
                for k in range(TOP_K):
                    scatter_copy(b, k).wait()

            load(0, 0)

            @pl.loop(0, n_pairs)
            def _(j):
                @pl.when(j > 0)
                def _():
                    drain(1)
                load(2 * j + 1, 1)
                scatter(2 * j, 0)

                @pl.when(j < n_pairs - 1)
                def _():
                    drain(0)
                    load(2 * j + 2, 0)
                scatter(2 * j + 1, 1)

            drain(0)
            drain(1)

    args = [a for pair in zip(tables, positions) for a in pair]
    return pl.kernel(
        body, mesh=mesh,
        out_type=jax.ShapeDtypeStruct((n_rows, width), tables[0].dtype),
        scratch_types=[pltpu.VMEM((2, TOP_K, chunk), jnp.int32), pltpu.VMEM((2, chunk, width), tables[0].dtype),
                       pltpu.SemaphoreType.DMA((2,)), pltpu.SemaphoreType.DMA((2,))],
    )(*args)


def _pos_kernel(idx_ref, base_ref, pos_ref):
    idx = idx_ref[...]
    rows = idx.shape[0]
    lane = lax.broadcasted_iota(jnp.int32, idx.shape, 1)
    hots = [jnp.where(lane == idx[:, k:k + 1], 1.0, 0.0) for k in range(TOP_K)]
    cnt = hots[0] + hots[1] + hots[2] + hots[3]
    ri = lax.broadcasted_iota(jnp.int32, (rows, rows), 0)
    ci = lax.broadcasted_iota(jnp.int32, (rows, rows), 1)
    earlier = jnp.where(ci < ri, 1.0, 0.0).astype(BF16)
    ahead = jnp.dot(earlier, cnt.astype(BF16), preferred_element_type=F32) + base_ref[0]
    posmat = jnp.zeros(idx.shape, F32)
    for k in range(TOP_K):
        posmat = jnp.where(lane == k, jnp.sum(hots[k] * ahead, axis=-1, keepdims=True), posmat)
    pos_ref[...] = posmat.T[:TOP_K, :].astype(jnp.int32)


def _route(idx_list, count_list):
    tiles = [c.shape[0] for c in count_list]
    tile_tok = [idx.size // LANES // n for idx, n in zip(idx_list, tiles)]
    n_tok = sum(n * sz for n, sz in zip(tiles, tile_tok))
    n_blocks = n_tok * TOP_K // MOE_ROWS + N_EXPERTS
    cnt = jnp.concatenate([c.reshape(-1, LANES) for c in count_list], axis=0).astype(jnp.int32)
    tile_off = jnp.cumsum(cnt, axis=0) - cnt
    total = jnp.sum(cnt, axis=0)
    padded = (total + MOE_ROWS - 1) // MOE_ROWS * MOE_ROWS
    pend = jnp.cumsum(padded)
    base = ((pend - padded)[None, :] + tile_off).astype(F32)
    starts = jnp.arange(n_blocks, dtype=jnp.int32) * MOE_ROWS
    block_e = jnp.minimum(jnp.sum(pend[None, :N_EXPERTS] <= starts[:, None], axis=1), N_EXPERTS - 1).astype(jnp.int32)
    n_used = (pend[N_EXPERTS - 1] // MOE_ROWS).astype(jnp.int32).reshape(1)
    positions, first = [], 0
    for idx, n_tiles, tile in zip(idx_list, tiles, tile_tok):
        stream_tok = n_tiles * tile
        positions.append(pl.pallas_call(
            _pos_kernel,
            grid=(n_tiles,),
            in_specs=[pl.BlockSpec((tile, LANES), lambda i: (i, 0)),
                      pl.BlockSpec((1, 1, LANES), lambda i: (i, 0, 0))],
            out_specs=pl.BlockSpec((TOP_K, tile), lambda i: (0, i)),
            out_shape=jax.ShapeDtypeStruct((TOP_K, stream_tok), jnp.int32),
            compiler_params=_cparams(("parallel",)),
            name="route_pos",
        )(idx.reshape(stream_tok, LANES), base[first:first + n_tiles].reshape(n_tiles, 1, LANES)))
        first += n_tiles
    return block_e, n_used, positions


def _moe(h2p_list, idx_list, count_list, w1, b1, w2, b2):
    half = h2p_list[0].shape[1]
    d = 2 * half
    block_e, n_used, positions = _route(idx_list, count_list)
    n_blocks = block_e.shape[0]
    xs = _sc_scatter_rows(h2p_list, positions, n_blocks * MOE_ROWS)
    grid_spec = pltpu.PrefetchScalarGridSpec(
        num_scalar_prefetch=2,
        grid=(n_blocks,),
        in_specs=[pl.BlockSpec((MOE_ROWS, half), lambda i, be, nu: (i, 0)),
                  pl.BlockSpec((1, d, 2 * D_FF), lambda i, be, nu: (be[i], 0, 0)),
                  pl.BlockSpec((1, 1, 2 * D_FF), lambda i, be, nu: (be[i], 0, 0)),
                  pl.BlockSpec((1, D_FF, d), lambda i, be, nu: (be[i], 0, 0)),
                  pl.BlockSpec((1, 1, d), lambda i, be, nu: (be[i], 0, 0))],
        out_specs=pl.BlockSpec((MOE_ROWS, half), lambda i, be, nu: (i, 0)),
        scratch_shapes=[pltpu.VMEM((d, 2 * D_FF), BF16), pltpu.VMEM((D_FF, d), BF16)],
    )
    ys = pl.pallas_call(
        _ffn_kernel,
        grid_spec=grid_spec,
        out_shape=jax.ShapeDtypeStruct((n_blocks * MOE_ROWS, half), jnp.int32),
        compiler_params=_cparams(("arbitrary",)),
        name="moe_ffn",
    )(block_e, n_used, xs, w1, b1.reshape(N_EXPERTS, 1, -1), w2, b2.reshape(N_EXPERTS, 1, -1))
    return [_sc_gather_rows(ys, pos.reshape(-1)) for pos in positions]


def _final_kernel(x1_ref, y0_ref, y1_ref, y2_ref, y3_ref, gate_ref, mod_ref, g_ref, *rest):
    o_ref = rest[-1]
    gates = gate_ref[0]
    lo = hi = None
    for k, y_ref in enumerate((y0_ref, y1_ref, y2_ref, y3_ref)):
        y_lo, y_hi = _unpack_bf16_pairs(y_ref[0, 0])
        gk = gates[:, k:k + 1]
        lo = gk * y_lo if k == 0 else lo + gk * y_lo
        hi = gk * y_hi if k == 0 else hi + gk * y_hi
    y = jnp.concatenate([lo, hi], axis=-1)
    gate2 = mod_ref[0, 5:6, :]
    o_ref[0] = x1_ref[0] + gate2 * _rms(y, g_ref[...])


def _final(x1, yg, gate, mod, mod_row, g_post_ffn, tm, b0, n_batch, out_prev):
    bsz, seq, d = x1.shape
    tok3 = lambda b, i: (b, i, 0)
    yg = yg.reshape(TOP_K, bsz, seq, d // 2)
    slot_specs = [pl.BlockSpec((1, 1, tm, d // 2), functools.partial(lambda b, i, k: (k, b, i, 0), k=k))
                  for k in range(TOP_K)]
    in_specs = ([pl.BlockSpec((1, tm, d), tok3)] + slot_specs +
                [pl.BlockSpec((1, tm, LANES), tok3),
                 pl.BlockSpec((1, 6, d), lambda b, i: (mod_row(b), 0, 0)), _const_spec((1, d))])
    args = [x1, yg, yg, yg, yg, gate, mod, g_post_ffn.astype(F32).reshape(1, d)]
    aliases = {}
    if out_prev is not None:
        in_specs.append(pl.BlockSpec(memory_space=pl.ANY))
        args.append(out_prev)
        aliases = {len(args) - 1: 0}
    return pl.pallas_call(
        _final_kernel,
        grid=(bsz, seq // tm),
        in_specs=in_specs,
        out_specs=pl.BlockSpec((1, tm, d), lambda b, i: (b + b0, i, 0)),
        out_shape=jax.ShapeDtypeStruct((n_batch, seq, d), F32),
        input_output_aliases=aliases,
        compiler_params=_cparams(("parallel", "parallel")),
        name="final_residual",
    )(*args)


def _stream(x, mod, mod_row, p, ctx_kv, init_state, rope, tm, b0, bsz):
    _, seq, d = x.shape
    is_ctx = ctx_kv is None
    pj = _project(x, mod, p["g_pre_mix"], p["w_in"], p["b_gates"], mod_row, rope, is_ctx, tm, b0, bsz)
    if is_ctx:
        att = _attention(pj["q"], None, None, pj["kd"], pj["vd"], p["sink"], window=False)
    else:
        att = _attention(pj["q"], pj["kd"], pj["vd"], ctx_kv[0], ctx_kv[1], p["sink"], window=True)
    ml = _mlstm(pj["mq"], pj["mk"], pj["mv"], pj["gates"], init_state, emit_state=is_ctx, b0=b0)
    x1, h2, idx, gate, tile_counts = _merge(att, ml[0], ml[1], pj, x, mod, mod_row, p, tm, b0)
    moe_in = (h2.reshape(bsz * seq, d // 2), idx, tile_counts)
    return x1, gate, moe_in, pj, ml


def kernel(x_prompt, x_sample, c, cache_k, cache_v, state_C, state_n, state_m, c_ctx, w_ada, b_ada, g_pre_mix,
           w_in, b_gates, attn_sink, norm_mlstm, w_up_att, w_up_ml, w_out, g_post_mix, g_pre_ffn, w_router,
           b_router, w1, b1, w2, b2, g_post_ffn):
    depth = w_ada.shape[0]
    n_dec = c.shape[0]
    cond = jnp.concatenate([c_ctx[None, :], c], axis=0).astype(F32)
    cond = jnp.pad(cond, ((0, 16 - cond.shape[0]), (0, 0)))
    y_prompt, y_sample = x_prompt, x_sample
    ks_, vs_, cs_, ns_, ms_ = [], [], [], [], []
    for l in range(depth):
        p = dict(g_pre_mix=g_pre_mix[l], w_in=w_in[l], b_gates=b_gates[l], sink=attn_sink[l], norm_ml=norm_mlstm[l],
                 w_up_att=w_up_att[l], w_up_ml=w_up_ml[l], w_out=w_out[l], g_post_mix=g_post_mix[l],
                 g_pre_ffn=g_pre_ffn[l], w_router=w_router[l], b_router=b_router[l], w1=w1[l], b1=b1[l],
                 w2=w2[l], b2=b2[l], g_post_ffn=g_post_ffn[l])
        mod = _adaln(cond, w_ada[l], b_ada[l]).reshape(16, 6, D_MODEL)
        row_p = lambda b: 0
        bsz, seq = x_prompt.shape[:2]
        x1_p, gate_p, moe_p, pj, ml = _stream(y_prompt, mod, row_p, p, None, None, False, 256, 0, bsz)
        nat = pj["nat"]
        ks_.append(nat[..., :KV_W].reshape(bsz, seq, ATT_KV_HEADS, HEAD_DIM))
        vs_.append(nat[..., KV_W:].reshape(bsz, seq, ATT_KV_HEADS, HEAD_DIM))
        cs_.append(ml[2])
        ns_.append(ml[3])
        ms_.append(ml[4][..., 0])
        ctx_kv = (_dup_heads(cache_k[:, l], False), _dup_heads(cache_v[:, l], True))
        init = (state_C[:, l], state_n[:, l], state_m[:, l])
        half = n_dec // 2
        experts = (p["w1"], p["b1"].astype(F32), p["w2"], p["b2"].astype(F32))
        groups = []
        for b0 in (0, half):
            row_s = functools.partial(lambda b, off: b + off + 1, off=b0)
            sl = slice(b0, b0 + half)
            x1_s, gate_s, moe_s, _, _ = _stream(y_sample, mod, row_s, p, (ctx_kv[0][sl], ctx_kv[1][sl]),
                                                init, True, 512, b0, half)
            groups.append((x1_s, gate_s, moe_s, row_s))
        (x1_a, gate_a, moe_a, row_a), (x1_b, gate_b, moe_b, row_b) = groups
        yg_p, yg_a = _moe([moe_p[0], moe_a[0]], [moe_p[1], moe_a[1]], [moe_p[2], moe_a[2]], *experts)
        (yg_b,) = _moe([moe_b[0]], [moe_b[1]], [moe_b[2]], *experts)
        y_prompt = _final(x1_p, yg_p, gate_p, mod, row_p, p["g_post_ffn"], 256, 0, bsz, None)
        y_sample = _final(x1_a, yg_a, gate_a, mod, row_a, p["g_post_ffn"], 512, 0, n_dec, None)
        y_sample = _final(x1_b, yg_b, gate_b, mod, row_b, p["g_post_ffn"], 512, half, n_dec, y_sample)
    return (y_prompt, y_sample, jnp.stack(ks_, axis=1), jnp.stack(vs_, axis=1), jnp.stack(cs_, axis=1),
            jnp.stack(ns_, axis=1), jnp.stack(ms_, axis=1))
```

```python
import functools

import numpy as np
import jax
import jax.numpy as jnp
from jax import lax
from jax.experimental import pallas as pl
from jax.experimental.pallas import tpu as pltpu
from jax.experimental.pallas import tpu_sc as plsc

F32 = jnp.float32
BF16 = jnp.bfloat16

D_MODEL = 1024
GRID_W = 64
ATT_HEADS = 8
ATT_KV_HEADS = 2
ATT_GROUP = ATT_HEADS // ATT_KV_HEADS
HEAD_DIM = 64
BLOCK = 128
ROPE_THETA = 10000.0
AXIS_FREQS = HEAD_DIM // 4
ML_HEADS = 4
ML_DK = 128
ML_DV = 256
ML_CHUNK = 512
N_EXPERTS = 32
TOP_K = 4
D_FF = 1024
SWIGLU_ALPHA = 1.702
SWIGLU_LIMIT = 7.0
EPS = 1e-6
NEG = -1e30

Q_W = ATT_HEADS * HEAD_DIM
KV_W = ATT_KV_HEADS * HEAD_DIM
MLQK_W = ML_HEADS * ML_DK
MLV_W = ML_HEADS * ML_DV
GATE_W = 4 * ML_HEADS

LANES = 128
VMEM_LIMIT = 56 * 1024 * 1024
MOE_ROWS = 512
SC_CORES = 2
SC_SUBCORES = 16
SC_GATHER_CHUNK = 32


def _cparams(sem):
    return pltpu.CompilerParams(dimension_semantics=sem, vmem_limit_bytes=VMEM_LIMIT)


def _const_spec(shape):
    nd = len(shape)
    return pl.BlockSpec(shape, lambda *_: (0,) * nd)


def _adaln_kernel(cond_ref, w_ref, b_ref, o_ref):
    cnd = cond_ref[...]
    act = cnd * jax.nn.sigmoid(cnd)
    o_ref[...] = jnp.dot(act.astype(BF16), w_ref[...].astype(BF16),
                         preferred_element_type=F32) + b_ref[...]


def _adaln(cond, w_ada, b_ada):
    rows, d = cond.shape
    n = w_ada.shape[1]
    tn = 1536
    return pl.pallas_call(
        _adaln_kernel,
        grid=(n // tn,),
        in_specs=[_const_spec((rows, d)),
                  pl.BlockSpec((d, tn), lambda j: (0, j)),
                  pl.BlockSpec((1, tn), lambda j: (0, j))],
        out_specs=pl.BlockSpec((rows, tn), lambda j: (0, j)),
        out_shape=jax.ShapeDtypeStruct((rows, n), F32),
        compiler_params=_cparams(("parallel",)),
        name="adaln",
    )(cond, w_ada, b_ada.reshape(1, n))


def _pack_w_in(w_in, b_gates, rope, with_nat):
    o_q, o_k, o_v = 0, Q_W, Q_W + KV_W
    o_mq = Q_W + 2 * KV_W
    o_mk = o_mq + MLQK_W
    o_mv = o_mk + MLQK_W
    o_g = o_mv + MLV_W
    o_mo = o_g + GATE_W
    o_ga = o_mo + MLV_W
    o_gm = o_ga + D_MODEL
    half = HEAD_DIM // 2

    def head_cols(base, h, rot):
        lo = base + h * HEAD_DIM
        if rot:
            return [w_in[:, lo + half:lo + HEAD_DIM], w_in[:, lo:lo + half]]
        return [w_in[:, lo:lo + HEAD_DIM]]

    def q_cols(rot):
        cols = []
        for h in range(ATT_HEADS):
            cols += head_cols(o_q, h, rot)
        return cols

    def dup_cols(base, rot):
        cols = []
        for h in range(ATT_KV_HEADS):
            hc = head_cols(base, h, rot)
            cols += hc + hc
        return cols

    segs, cols, off = {}, [], 0

    def add(name, cl):
        nonlocal off
        width = sum(c.shape[1] for c in cl)
        segs[name] = (off, off + width)
        cols.extend(cl)
        off += width

    add("q", q_cols(False))
    if rope:
        add("qrot", q_cols(True))
    add("kd", dup_cols(o_k, False))
    if rope:
        add("kdrot", dup_cols(o_k, True))
    if with_nat:
        add("nat", [w_in[:, o_k:o_k + 2 * KV_W]])
    add("mq", [w_in[:, o_mq:o_mq + MLQK_W]])
    add("mv", [w_in[:, o_mv:o_mv + MLV_W]])
    add("mo", [w_in[:, o_mo:o_mo + MLV_W]])
    add("ga", [w_in[:, o_ga:o_ga + D_MODEL]])
    add("gm", [w_in[:, o_gm:o_gm + D_MODEL]])
    add("gates", [w_in[:, o_g:o_g + GATE_W], jnp.zeros((D_MODEL, LANES - GATE_W), w_in.dtype)])
    wp = jnp.concatenate(cols, axis=1).astype(BF16)
    wvd_t = jnp.concatenate(dup_cols(o_v, False) + [w_in[:, o_mk:o_mk + MLQK_W]], axis=1).T.astype(BF16)
    bg = jnp.pad(b_gates.astype(F32), (0, LANES - GATE_W)).reshape(1, LANES)
    return wp, wvd_t, bg, segs


def _rope_tables(n_tok):
    rows = n_tok // GRID_W
    row = np.repeat(np.arange(rows), GRID_W).astype(np.float32)
    col = np.tile(np.arange(GRID_W), rows).astype(np.float32)
    inv = (np.float32(ROPE_THETA) ** (-np.arange(AXIS_FREQS, dtype=np.float32) / AXIS_FREQS)).astype(np.float32)
    ang = np.concatenate([row[:, None] * inv, col[:, None] * inv], axis=-1).astype(np.float32)
    cos, sin = np.cos(ang), np.sin(ang)
    c64 = np.concatenate([cos, cos], axis=-1)
    s64 = np.concatenate([-sin, sin], axis=-1)
    return (jnp.asarray(np.tile(c64, (1, 2)), F32), jnp.asarray(np.tile(s64, (1, 2)), F32))


def _proj_kernel(*refs, segs, rope, with_nat):
    it = iter(refs)
    x_ref, mod_ref, g_ref, w_ref, wvt_ref, bg_ref = (next(it) for _ in range(6))
    cos_ref = sin_ref = None
    if rope:
        cos_ref, sin_ref = next(it), next(it)
    q_ref, kd_ref, vd_ref = next(it), next(it), next(it)
    nat_ref = next(it) if with_nat else None
    mq_ref, mk_ref, mv_ref, gates_ref, mo_ref, ga_ref, gm_ref = (next(it) for _ in range(7))

    x = x_ref[0]
    shift = mod_ref[0, 0:1, :]
    scale = mod_ref[0, 1:2, :]
    ms = jnp.mean(x * x, axis=-1, keepdims=True)
    xn = x * lax.rsqrt(ms + EPS) * g_ref[...]
    xb = (xn * (1.0 + scale) + shift).astype(BF16)

    def seg(name):
        lo, hi = segs[name]
        return jnp.dot(xb, w_ref[:, lo:hi], preferred_element_type=F32)

    uq = seg("q")
    ukd = seg("kd")
    if rope:
        uqr = seg("qrot")
        ukr = seg("kdrot")
        cs = cos_ref[...]
        sn = sin_ref[...]
        for h in range(ATT_HEADS // 2):
            sl = slice(h * LANES, (h + 1) * LANES)
            q_ref[0, h] = ((uq[:, sl] * cs + uqr[:, sl] * sn) * HEAD_DIM ** -0.5).astype(BF16)
        for h in range(ATT_KV_HEADS):
            sl = slice(h * LANES, (h + 1) * LANES)
            kd_ref[0, :, sl] = (ukd[:, sl] * cs + ukr[:, sl] * sn).astype(BF16)
    else:
        for h in range(ATT_HEADS // 2):
            sl = slice(h * LANES, (h + 1) * LANES)
            q_ref[0, h] = (uq[:, sl] * HEAD_DIM ** -0.5).astype(BF16)
        kd_ref[0] = ukd.astype(BF16)
    ut = lax.dot_general(wvt_ref[...], xb, (((1,), (1,)), ((), ())), preferred_element_type=F32)
    vd_ref[0] = ut[:2 * LANES].astype(BF16)
    mk_ref[0] = ut[2 * LANES:].astype(BF16)
    if with_nat:
        nat_ref[0] = seg("nat")
    mq_ref[0] = (seg("mq") * ML_DK ** -0.5).astype(BF16)
    mv_ref[0] = seg("mv").astype(BF16)
    gates_ref[0] = seg("gates") + bg_ref[...]
    mo_ref[0] = seg("mo")
    ga_ref[0] = seg("ga")
    gm_ref[0] = seg("gm")


def _project(x, mod, g_pre, w_in, b_gates, mod_row, rope, with_nat, tm, b0, bsz):
    _, seq, d = x.shape
    wp, wvd_t, bg, segs = _pack_w_in(w_in, b_gates, rope, with_nat)
    nw = wp.shape[1]
    tok3 = lambda b, i: (b, i, 0)
    in_specs = [pl.BlockSpec((1, tm, d), lambda b, i: (b + b0, i, 0)),
                pl.BlockSpec((1, 6, d), lambda b, i: (mod_row(b), 0, 0)),
                _const_spec((1, d)),
                pl.BlockSpec((d, nw), lambda b, i: (0, 0), pipeline_mode=pl.Buffered(1)),
                _const_spec((2 * LANES + MLQK_W, d)), _const_spec((1, LANES))]
    args = [x, mod, g_pre.reshape(1, d), wp, wvd_t, bg]
    if rope:
        cos_t, sin_t = _rope_tables(seq)
        in_specs += [pl.BlockSpec((tm, LANES), lambda b, i: (i, 0))] * 2
        args += [cos_t, sin_t]

    def tok_out(width, dtype):
        return (jax.ShapeDtypeStruct((bsz, seq, width), dtype), pl.BlockSpec((1, tm, width), tok3))

    outs = [(jax.ShapeDtypeStruct((bsz, ATT_HEADS // 2, seq, LANES), BF16),
             pl.BlockSpec((1, ATT_HEADS // 2, tm, LANES), lambda b, i: (b, 0, i, 0))),
            tok_out(2 * LANES, BF16),
            (jax.ShapeDtypeStruct((bsz, 2 * LANES, seq), BF16),
             pl.BlockSpec((1, 2 * LANES, tm), lambda b, i: (b, 0, i)))]
    if with_nat:
        outs.append(tok_out(2 * KV_W, F32))
    outs += [tok_out(MLQK_W, BF16),
             (jax.ShapeDtypeStruct((bsz, MLQK_W, seq), BF16),
              pl.BlockSpec((1, MLQK_W, tm), lambda b, i: (b, 0, i))),
             tok_out(MLV_W, BF16), tok_out(LANES, F32),
             tok_out(MLV_W, F32), tok_out(D_MODEL, F32), tok_out(D_MODEL, F32)]
    res = pl.pallas_call(
        functools.partial(_proj_kernel, segs=segs, rope=rope, with_nat=with_nat),
        grid=(bsz, seq // tm),
        in_specs=in_specs,
        out_specs=[o[1] for o in outs],
        out_shape=[o[0] for o in outs],
        compiler_params=_cparams(("parallel", "parallel")),
        name="proj_rope" if rope else "proj_ctx",
    )(*args)
    names = ["q", "kd", "vd"] + (["nat"] if with_nat else []) + ["mq", "mk", "mv", "gates", "mo", "ga", "gm"]
    return dict(zip(names, res))


def _attn_kernel(*refs, window, n_blocks, n_ctx):
    it = iter(refs)
    sink_ref, q_ref = next(it), next(it)
    if window:
        kp_ref, kc_ref, kn_ref, vp_ref, vc_ref, vn_ref = (next(it) for _ in range(6))
    ck_ref, cv_ref, o_ref = next(it), next(it), next(it)
    j = pl.program_id(1)
    n_win = 3 * BLOCK if window else 0
    cols = ATT_GROUP * BLOCK

    col_id = lax.broadcasted_iota(jnp.int32, (1, cols), 1)
    if window:
        kl = lax.broadcasted_iota(jnp.int32, (BLOCK, cols), 0)
        ql = lax.broadcasted_iota(jnp.int32, (BLOCK, cols), 1) & (BLOCK - 1)
        prev_ok = kl >= ql + jnp.where(j == 0, BLOCK, 0)
        next_ok = kl <= ql - jnp.where(j == n_blocks - 1, BLOCK, 0)
    lane_lo = lax.broadcasted_iota(jnp.int32, (BLOCK, LANES), 1) < HEAD_DIM
    lane_row = lax.broadcasted_iota(jnp.int32, (1, LANES), 1) < HEAD_DIM
    half_lo = jnp.where(lane_row, 1.0, 0.0).astype(BF16)
    half_hi = jnp.where(lane_row, 0.0, 1.0).astype(BF16)

    for kvh in range(ATT_KV_HEADS):
        sl = slice(kvh * LANES, (kvh + 1) * LANES)
        if window:
            keys = jnp.concatenate([kp_ref[0, :, sl], kc_ref[0, :, sl], kn_ref[0, :, sl], ck_ref[0, :, sl]], axis=0)
            vals_t = jnp.concatenate([vp_ref[0, sl, :], vc_ref[0, sl, :], vn_ref[0, sl, :], cv_ref[0, sl, :]], axis=1)
        else:
            keys = ck_ref[0, :, sl]
            vals_t = cv_ref[0, sl, :]
        qs = jnp.concatenate([q_ref[0, kvh * (ATT_GROUP // 2) + g // 2] * (half_lo if g % 2 == 0 else half_hi)
                              for g in range(ATT_GROUP)], axis=0)
        s = lax.dot_general(keys, qs, (((1,), (1,)), ((), ())), preferred_element_type=F32)
        if window:
            s = jnp.concatenate([jnp.where(prev_ok, s[:BLOCK], NEG), s[BLOCK:2 * BLOCK],
                                 jnp.where(next_ok, s[2 * BLOCK:n_win], NEG), s[n_win:]], axis=0)
        snk = jnp.full((1, cols), sink_ref[kvh * ATT_GROUP + ATT_GROUP - 1], F32)
        for g in range(ATT_GROUP - 2, -1, -1):
            snk = jnp.where(col_id < (g + 1) * BLOCK, sink_ref[kvh * ATT_GROUP + g], snk)
        m = jnp.maximum(jnp.max(s, axis=0, keepdims=True), snk)
        p = jnp.exp(s - m)
        den = jnp.sum(p, axis=0, keepdims=True) + jnp.exp(snk - m)
        o_t = jnp.dot(vals_t, p.astype(BF16), preferred_element_type=F32) / den
        for pair in range(ATT_GROUP // 2):
            even = o_t[:, (2 * pair) * BLOCK:(2 * pair + 1) * BLOCK].T
            odd = o_t[:, (2 * pair + 1) * BLOCK:(2 * pair + 2) * BLOCK].T
            col = (kvh * (ATT_GROUP // 2) + pair) * LANES
            o_ref[0, :, col:col + LANES] = jnp.where(lane_lo, even, odd).astype(BF16)


def _attention(q, kd, vd_t, ckd, cvd_t, sink, window):
    bsz, _, seq, _ = q.shape
    nb = seq // BLOCK
    n_ctx = ckd.shape[1]
    in_specs = [pl.BlockSpec(memory_space=pltpu.SMEM),
                pl.BlockSpec((1, ATT_HEADS // 2, BLOCK, LANES), lambda b, j: (b, 0, j, 0))]
    args = [sink.astype(F32), q]
    if window:
        in_specs += [pl.BlockSpec((1, BLOCK, 2 * LANES), lambda b, j: (b, jnp.maximum(j - 1, 0), 0)),
                     pl.BlockSpec((1, BLOCK, 2 * LANES), lambda b, j: (b, j, 0)),
                     pl.BlockSpec((1, BLOCK, 2 * LANES), lambda b, j: (b, jnp.minimum(j + 1, nb - 1), 0)),
                     pl.BlockSpec((1, 2 * LANES, BLOCK), lambda b, j: (b, 0, jnp.maximum(j - 1, 0))),
                     pl.BlockSpec((1, 2 * LANES, BLOCK), lambda b, j: (b, 0, j)),
                     pl.BlockSpec((1, 2 * LANES, BLOCK), lambda b, j: (b, 0, jnp.minimum(j + 1, nb - 1)))]
        args += [kd, kd, kd, vd_t, vd_t, vd_t]
    in_specs += [pl.BlockSpec((1, n_ctx, 2 * LANES), lambda b, j: (b, 0, 0)),
                 pl.BlockSpec((1, 2 * LANES, n_ctx), lambda b, j: (b, 0, 0))]
    args += [ckd, cvd_t]
    return pl.pallas_call(
        functools.partial(_attn_kernel, window=window, n_blocks=nb, n_ctx=n_ctx),
        grid=(bsz, nb),
        in_specs=in_specs,
        out_specs=pl.BlockSpec((1, BLOCK, Q_W), lambda b, j: (b, j, 0)),
        out_shape=jax.ShapeDtypeStruct((bsz, seq, Q_W), BF16),
        compiler_params=_cparams(("parallel", "parallel")),
        name="attn_window" if window else "attn_ctx",
    )(*args)


def _dup_heads(t, transpose):
    b, l = t.shape[:2]
    out = jnp.concatenate([t, t], axis=-1).reshape(b, l, ATT_KV_HEADS * LANES).astype(BF16)
    return jnp.swapaxes(out, 1, 2) if transpose else out


def _split3(x):
    hi = x.astype(BF16)
    r1 = x - hi.astype(F32)
    mid = r1.astype(BF16)
    lo = (r1 - mid.astype(F32)).astype(BF16)
    return hi, mid, lo


def _log_sigmoid(x):
    return jnp.minimum(x, 0.0) - jnp.log1p(jnp.exp(-jnp.abs(x)))


def _mlstm_kernel(*refs, chunk_len, n_chunks, has_init, emit_state):
    it = iter(refs)
    qkvg = [[next(it) for _ in range(4)] for _ in range(2)]
    if has_init:
        c0_ref, n0_ref, m0_ref = next(it), next(it), next(it)
    h_refs = [next(it), next(it)]
    if emit_state:
        co_ref, no_ref, mo_ref = next(it), next(it), next(it)
    st_s, m_s = next(it), next(it)
    t = pl.program_id(1)
    L = chunk_len

    @pl.when(t == 0)
    def _():
        for d in range(2):
            for h in range(ML_HEADS):
                if has_init:
                    st_s[d, h, :, :ML_DV] = c0_ref[0, d, h].T
                    st_s[d, h, :, ML_DV:] = jnp.broadcast_to(n0_ref[0, d, h:h + 1, :], (LANES, ML_DK)).T
                    m_s[d, h] = m0_ref[0, d, h:h + 1, :]
                else:
                    st_s[d, h] = jnp.zeros((ML_DK, ML_DV + LANES), F32)
                    m_s[d, h] = jnp.zeros((1, LANES), F32)

    ri = lax.broadcasted_iota(jnp.int32, (L, L), 0)
    ci = lax.broadcasted_iota(jnp.int32, (L, L), 1)
    ones_blk = jnp.ones((L, LANES), BF16)

    for d in range(2):
        q_ref, kt_ref, v_ref, g_ref = qkvg[d]
        tri = (ci <= ri) if d == 0 else (ci >= ri)
        tri_b = jnp.where(tri, 1.0, 0.0).astype(BF16)
        g = g_ref[0]
        lf = _log_sigmoid(g)
        g_t = g.T
        lf_t = lf.T
        bc = sum(jnp.dot(tri_b, part, preferred_element_type=F32) for part in _split3(lf))
        br = sum(lax.dot_general(part, tri_b, (((1,), (1,)), ((), ())), preferred_element_type=F32)
                 for part in _split3(lf_t))
        tot = jnp.sum(lf, axis=0, keepdims=True)

        for h in range(ML_HEADS):
            icol_i = 2 * ML_HEADS * d + h
            fcol = icol_i + ML_HEADS
            bcol = bc[:, fcol:fcol + 1]
            brow = br[fcol:fcol + 1, :]
            irow = g_t[icol_i:icol_i + 1, :]
            b_last = tot[:, fcol:fcol + 1]
            m_prev = m_s[d, h][:, 0:1]
            st_prev = st_s[d, h]
            qh = q_ref[0, :, h * ML_DK:(h + 1) * ML_DK]
            kt = kt_ref[0, h * ML_DK:(h + 1) * ML_DK, :]
            v_aug = jnp.concatenate([v_ref[0, :, h * ML_DV:(h + 1) * ML_DV], ones_blk], axis=1)

            rel = jnp.where(tri, irow - brow, NEG)
            inter = bcol + m_prev
            m_t = jnp.maximum(inter, bcol + jnp.max(rel, axis=-1, keepdims=True))
            w = jnp.exp(rel + (bcol - m_t))
            w_inter = jnp.exp(inter - m_t)
            s = jnp.dot(qh, kt, preferred_element_type=F32) * w
            lhs = jnp.concatenate([s.astype(BF16), (qh.astype(F32) * w_inter).astype(BF16)], axis=1)
            rhs = jnp.concatenate([v_aug, st_prev.astype(BF16)], axis=0)
            mix = jnp.dot(lhs, rhs, preferred_element_type=F32)
            inv = 1.0 / jnp.maximum(jnp.abs(mix[:, ML_DV:]), jnp.exp(-m_t))
            for c in range(ML_DV // LANES):
                lo = h * ML_DV + c * LANES
                h_refs[d][0, :, lo:lo + LANES] = mix[:, c * LANES:(c + 1) * LANES] * inv

            a = b_last - brow + irow
            m_new = jnp.maximum(b_last + m_prev, jnp.max(a, axis=-1, keepdims=True))
            wk = jnp.exp(a - m_new)
            decay = jnp.exp(b_last + m_prev - m_new)
            kw_t = (kt.astype(F32) * wk).astype(BF16)
            st_new = decay * st_prev + jnp.dot(kw_t, v_aug, preferred_element_type=F32)
            st_s[d, h] = st_new
            m_s[d, h] = jnp.broadcast_to(m_new, (1, LANES))
            if emit_state:
                @pl.when(t == n_chunks - 1)
                def _():
                    co_ref[0, d, h] = st_new[:, :ML_DV].T
                    no_ref[0, d, h:h + 1, :] = st_new[:, ML_DV:].T[0:1, :]
                    mo_ref[0, d, h:h + 1, :] = jnp.broadcast_to(m_new, (1, LANES))


def _mlstm(mq, mk_t, mv, gates, init, emit_state, b0=0):
    bsz, seq, _ = mq.shape
    chunk_len = min(ML_CHUNK, seq)
    nc = seq // chunk_len
    fwd = lambda b, t: (b, t, 0)
    bwd = lambda b, t: (b, nc - 1 - t, 0)
    fwd_t = lambda b, t: (b, 0, t)
    bwd_t = lambda b, t: (b, 0, nc - 1 - t)
    state5 = lambda b, t: (b, 0, 0, 0, 0)
    state4 = lambda b, t: (b, 0, 0, 0)
    init5 = lambda b, t: (b + b0, 0, 0, 0, 0)
    init4 = lambda b, t: (b + b0, 0, 0, 0)
    in_specs, args = [], []
    for chunk, chunk_t in ((fwd, fwd_t), (bwd, bwd_t)):
        in_specs += [pl.BlockSpec((1, chunk_len, MLQK_W), chunk), pl.BlockSpec((1, MLQK_W, chunk_len), chunk_t),
                     pl.BlockSpec((1, chunk_len, MLV_W), chunk), pl.BlockSpec((1, chunk_len, LANES), chunk)]
        args += [mq, mk_t, mv, gates]
    if init is not None:
        c0, n0, m0 = init
        in_specs += [pl.BlockSpec((1, 2, ML_HEADS, ML_DV, ML_DK), init5),
                     pl.BlockSpec((1, 2, ML_HEADS, ML_DK), init4),
                     pl.BlockSpec((1, 2, ML_HEADS, LANES), init4)]
        args += [c0.astype(F32), n0.astype(F32),
                 jnp.broadcast_to(m0.astype(F32)[..., None], m0.shape + (LANES,))]
    out_shape = [jax.ShapeDtypeStruct((bsz, seq, MLV_W), F32)] * 2
    out_specs = [pl.BlockSpec((1, chunk_len, MLV_W), fwd), pl.BlockSpec((1, chunk_len, MLV_W), bwd)]
    if emit_state:
        out_shape += [jax.ShapeDtypeStruct((bsz, 2, ML_HEADS, ML_DV, ML_DK), F32),
                      jax.ShapeDtypeStruct((bsz, 2, ML_HEADS, ML_DK), F32),
                      jax.ShapeDtypeStruct((bsz, 2, ML_HEADS, LANES), F32)]
        out_specs += [pl.BlockSpec((1, 2, ML_HEADS, ML_DV, ML_DK), state5),
                      pl.BlockSpec((1, 2, ML_HEADS, ML_DK), state4),
                      pl.BlockSpec((1, 2, ML_HEADS, LANES), state4)]
    return pl.pallas_call(
        functools.partial(_mlstm_kernel, chunk_len=chunk_len, n_chunks=nc, has_init=init is not None, emit_state=emit_state),
        grid=(bsz, nc),
        in_specs=in_specs,
        out_specs=out_specs,
        out_shape=out_shape,
        scratch_shapes=[pltpu.VMEM((2, ML_HEADS, ML_DK, ML_DV + LANES), F32),
                        pltpu.VMEM((2, ML_HEADS, 1, LANES), F32)],
        compiler_params=_cparams(("parallel", "arbitrary")),
        name="mlstm_state" if emit_state else "mlstm",
    )(*args)


def _sigmoid(x):
    return 0.5 * jnp.tanh(0.5 * x) + 0.5


def _rms(x, g):
    return x * lax.rsqrt(jnp.mean(x * x, axis=-1, keepdims=True) + EPS) * g


def _pack_bf16_pairs(x):
    n = x.shape[1] // 2
    lo = pltpu.bitcast(x[:, :n].astype(BF16).astype(F32), jnp.int32)
    hi = pltpu.bitcast(x[:, n:].astype(BF16).astype(F32), jnp.int32)
    return lax.shift_right_logical(lo, 16) | hi


def _unpack_bf16_pairs(p):
    lo = pltpu.bitcast(lax.shift_left(p, 16), F32)
    hi = pltpu.bitcast(p & jnp.int32(-65536), F32)
    return lo, hi


def _merge_kernel(att_ref, hf_ref, hb_ref, mo_ref, ga_ref, gm_ref, x_ref, mod_ref, nml_ref, gpm_ref, gpf_ref,
                  wua_ref, wum_ref, wo_ref, wr_ref, br_ref, x1_ref, h2_ref, idx_ref, gate_ref, cnt_ref):
    hsum = hf_ref[0] + hb_ref[0]
    parts = []
    for h in range(ML_HEADS):
        hh = hsum[:, h * ML_DV:(h + 1) * ML_DV]
        parts.append(hh * lax.rsqrt(jnp.mean(hh * hh, axis=-1, keepdims=True) + EPS))
    hn = jnp.concatenate(parts, axis=-1) * nml_ref[...]
    ml = (hn * _sigmoid(mo_ref[0])).astype(BF16)
    a = jnp.dot(att_ref[0], wua_ref[...], preferred_element_type=F32)
    m = jnp.dot(ml, wum_ref[...], preferred_element_type=F32)
    z = (_sigmoid(ga_ref[0]) * a + _sigmoid(gm_ref[0]) * m).astype(BF16)
    mix = jnp.dot(z, wo_ref[...], preferred_element_type=F32)
    gate1 = mod_ref[0, 2:3, :]
    shift2 = mod_ref[0, 3:4, :]
    scale2 = mod_ref[0, 4:5, :]
    x1 = x_ref[0] + gate1 * _rms(mix, gpm_ref[...])
    x1_ref[0] = x1
    h2 = _rms(x1, gpf_ref[...]) * (1.0 + scale2) + shift2
    h2_ref[0] = _pack_bf16_pairs(h2)

    h2_hi = h2.astype(BF16)
    h2_lo = (h2 - h2_hi.astype(F32)).astype(BF16)
    wr = wr_ref[...]
    wr_hi = wr.astype(BF16)
    wr_lo = (wr - wr_hi.astype(F32)).astype(BF16)
    logits = (jnp.dot(h2_hi, wr_hi, preferred_element_type=F32)
              + jnp.dot(h2_lo, wr_hi, preferred_element_type=F32)
              + jnp.dot(h2_hi, wr_lo, preferred_element_type=F32)) + br_ref[...]
    lane = lax.broadcasted_iota(jnp.int32, logits.shape, 1).astype(F32)
    work = jnp.where(lane < N_EXPERTS, logits, -jnp.inf)
    idx_out = jnp.zeros(logits.shape, F32)
    val_out = jnp.zeros(logits.shape, F32)
    picked = jnp.zeros(logits.shape, F32)
    top0 = None
    esum = None
    for k in range(TOP_K):
        mx = jnp.max(work, axis=-1, keepdims=True)
        sel = jnp.min(jnp.where(work == mx, lane, float(LANES)), axis=-1, keepdims=True)
        if k == 0:
            top0 = mx
        e = jnp.exp(mx - top0)
        esum = e if k == 0 else esum + e
        idx_out = jnp.where(lane == k, sel, idx_out)
        val_out = jnp.where(lane == k, e, val_out)
        picked = jnp.where(lane == sel, 1.0, picked)
        work = jnp.where(lane == sel, -jnp.inf, work)
    idx_ref[0] = idx_out.astype(jnp.int32)
    gate_ref[0] = val_out / esum
    cnt_ref[0] = jnp.sum(picked, axis=0, keepdims=True)


def _merge(att, hf, hb, pj, x, mod, mod_row, p, tm, b0):
    bsz = att.shape[0]
    _, seq, d = x.shape
    tok3 = lambda b, i: (b, i, 0)
    wr = jnp.pad(p["w_router"].astype(F32), ((0, 0), (0, LANES - N_EXPERTS)))
    br = jnp.pad(p["b_router"].astype(F32), (0, LANES - N_EXPERTS)).reshape(1, LANES)
    row = lambda v: v.astype(F32).reshape(1, -1)
    in_specs = [pl.BlockSpec((1, tm, Q_W), tok3),
                pl.BlockSpec((1, tm, MLV_W), tok3), pl.BlockSpec((1, tm, MLV_W), tok3),
                pl.BlockSpec((1, tm, MLV_W), tok3), pl.BlockSpec((1, tm, d), tok3), pl.BlockSpec((1, tm, d), tok3),
                pl.BlockSpec((1, tm, d), lambda b, i: (b + b0, i, 0)),
                pl.BlockSpec((1, 6, d), lambda b, i: (mod_row(b), 0, 0)),
                _const_spec((1, MLV_W)), _const_spec((1, d)), _const_spec((1, d)),
                _const_spec((Q_W, d)), _const_spec((MLV_W, d)), _const_spec((d, d)),
                _const_spec((d, LANES)), _const_spec((1, LANES))]
    out_shape = [jax.ShapeDtypeStruct((bsz, seq, d), F32), jax.ShapeDtypeStruct((bsz, seq, d // 2), jnp.int32),
                 jax.ShapeDtypeStruct((bsz, seq, LANES), jnp.int32), jax.ShapeDtypeStruct((bsz, seq, LANES), F32),
                 jax.ShapeDtypeStruct((bsz * (seq // tm), 1, LANES), F32)]
    out_specs = [pl.BlockSpec((1, tm, d), tok3), pl.BlockSpec((1, tm, d // 2), tok3),
                 pl.BlockSpec((1, tm, LANES), tok3), pl.BlockSpec((1, tm, LANES), tok3),
                 pl.BlockSpec((1, 1, LANES), lambda b, i: (b * (seq // tm) + i, 0, 0))]
    return pl.pallas_call(
        _merge_kernel,
        grid=(bsz, seq // tm),
        in_specs=in_specs,
        out_specs=out_specs,
        out_shape=out_shape,
        compiler_params=_cparams(("parallel", "parallel")),
        name="merge_router",
    )(att, hf, hb, pj["mo"], pj["ga"], pj["gm"], x, mod, row(p["norm_ml"]), row(p["g_post_mix"]),
      row(p["g_pre_ffn"]), p["w_up_att"].astype(BF16), p["w_up_ml"].astype(BF16), p["w_out"].astype(BF16), wr, br)


def _sc_gather_rows(table, idx):
    n = idx.shape[0]
    width = table.shape[1]
    chunk = SC_GATHER_CHUNK
    n_workers = SC_CORES * SC_SUBCORES
    rows_per_worker = n // n_workers
    n_pairs = rows_per_worker // (2 * chunk)
    assert n_pairs * 2 * chunk * n_workers == n
    mesh = plsc.VectorSubcoreMesh(core_axis_name="c", subcore_axis_name="s")

    def body(table_hbm, idx_hbm, out_hbm, idx_v, rows_v, gsem, osem):
        wid = lax.axis_index("s") * SC_CORES + lax.axis_index("c")
        base0 = wid * rows_per_worker

        def gather_copy(b):
            return pltpu.make_async_copy(table_hbm.at[idx_v.at[b]], rows_v.at[b], gsem.at[b])

        def write_copy(ci, b):
            base = pl.multiple_of(base0 + ci * chunk, 8)
            return pltpu.make_async_copy(rows_v.at[b], out_hbm.at[pl.ds(base, chunk)], osem.at[b])

        def issue(ci, b):
            base = pl.multiple_of(base0 + ci * chunk, 8)
            pltpu.sync_copy(idx_hbm.at[pl.ds(base, chunk)], idx_v.at[b])
            gather_copy(b).start()

        def finish(ci, b):
            gather_copy(b).wait()
            write_copy(ci, b).start()

        issue(0, 0)

        @pl.loop(0, n_pairs)
        def _(j):
            @pl.when(j > 0)
            def _():
                write_copy(2 * j - 1, 1).wait()
            issue(2 * j + 1, 1)
            finish(2 * j, 0)

            @pl.when(j < n_pairs - 1)
            def _():
                write_copy(2 * j, 0).wait()
                issue(2 * j + 2, 0)
            finish(2 * j + 1, 1)

        write_copy(2 * n_pairs - 2, 0).wait()
        write_copy(2 * n_pairs - 1, 1).wait()

    return pl.kernel(
        body, mesh=mesh,
        out_type=jax.ShapeDtypeStruct((n, width), table.dtype),
        scratch_types=[pltpu.VMEM((2, chunk), jnp.int32), pltpu.VMEM((2, chunk, width), table.dtype),
                       pltpu.SemaphoreType.DMA((2,)), pltpu.SemaphoreType.DMA((2,))],
    )(table, idx)


def _ffn_kernel(be_ref, nu_ref, xs_ref, w1_ref, b1_ref, w2_ref, b2_ref, ys_ref, w1b_s, w2b_s):
    i = pl.program_id(0)
    e = be_ref[i]
    e_prev = be_ref[jnp.maximum(i - 1, 0)]

    @pl.when((i == 0) | (e != e_prev))
    def _():
        w1b_s[...] = w1_ref[0].astype(BF16)
        w2b_s[...] = w2_ref[0].astype(BF16)

    @pl.when(i < nu_ref[0])
    def _():
        half = D_MODEL // 2
        x_lo, x_hi = _unpack_bf16_pairs(xs_ref[...])
        hmid = (jnp.dot(x_lo.astype(BF16), w1b_s[:half, :], preferred_element_type=F32)
                + jnp.dot(x_hi.astype(BF16), w1b_s[half:, :], preferred_element_type=F32)) + b1_ref[0]
        glu = jnp.minimum(hmid[:, :D_FF], SWIGLU_LIMIT)
        lin = jnp.clip(hmid[:, D_FF:], -SWIGLU_LIMIT, SWIGLU_LIMIT)
        act = ((lin + 1.0) * glu * _sigmoid(SWIGLU_ALPHA * glu)).astype(BF16)
        y = jnp.dot(act, w2b_s[...], preferred_element_type=F32) + b2_ref[0]
        ys_ref[...] = _pack_bf16_pairs(y)

    @pl.when(i >= nu_ref[0])
    def _():
        ys_ref[...] = jnp.zeros(ys_ref.shape, jnp.int32)


def _sc_scatter_rows(tables, positions, n_rows):
    width = tables[0].shape[1]
    chunk = SC_GATHER_CHUNK
    n_workers = SC_CORES * SC_SUBCORES
    n_streams = len(tables)
    mesh = plsc.VectorSubcoreMesh(core_axis_name="c", subcore_axis_name="s")

    def body(*refs):
        table_refs = refs[0:2 * n_streams:2]
        pos_refs = refs[1:2 * n_streams:2]
        out_hbm, idx_v, rows_v, lsem, ssem = refs[2 * n_streams:]
        wid = lax.axis_index("s") * SC_CORES + lax.axis_index("c")

        for table_hbm, pos_hbm in zip(table_refs, pos_refs):
            tok_per_worker = table_hbm.shape[0] // n_workers
            n_pairs = tok_per_worker // (2 * chunk)
            assert n_pairs * 2 * chunk * n_workers == table_hbm.shape[0]
            base0 = wid * tok_per_worker

            def load_copy(ci, b):
                base = pl.multiple_of(base0 + ci * chunk, 8)
                return pltpu.make_async_copy(table_hbm.at[pl.ds(base, chunk)], rows_v.at[b], lsem.at[b])

            def scatter_copy(b, k):
                return pltpu.make_async_copy(rows_v.at[b], out_hbm.at[idx_v.at[b, k]], ssem.at[b])

            def load(ci, b):
                base = pl.multiple_of(base0 + ci * chunk, 8)
                load_copy(ci, b).start()
                for k in range(TOP_K):
                    pltpu.sync_copy(pos_hbm.at[k, pl.ds(base, chunk)], idx_v.at[b, k])

            def scatter(ci, b):
                load_copy(ci, b).wait()
                for k in range(TOP_K):
                    scatter_copy(b, k).start()

            def drain(b):
                for k in range(TOP_K):
                    scatter_copy(b, k).wait()

            load(0, 0)

            @pl.loop(0, n_pairs)
            def _(j):
                @pl.when(j > 0)
                def _():
                    drain(1)
                load(2 * j + 1, 1)
                scatter(2 * j, 0)

                @pl.when(j < n_pairs - 1)
                def _():
                    drain(0)
                    load(2 * j + 2, 0)
                scatter(2 * j + 1, 1)

            drain(0)
            drain(1)

    args = [a for pair in zip(tables, positions) for a in pair]
    return pl.kernel(
        body, mesh=mesh,
        out_type=jax.ShapeDtypeStruct((n_rows, width), tables[0].dtype),
        scratch_types=[pltpu.VMEM((2, TOP_K, chunk), jnp.int32), pltpu.VMEM((2, chunk, width), tables[0].dtype),
                       pltpu.SemaphoreType.DMA((2,)), pltpu.SemaphoreType.DMA((2,))],
    )(*args)


def _pos_kernel(idx_ref, base_ref, pos_ref):
    idx = idx_ref[...]
    rows = idx.shape[0]
    lane = lax.broadcasted_iota(jnp.int32, idx.shape, 1)
    hots = [jnp.where(lane == idx[:, k:k + 1], 1.0, 0.0) for k in range(TOP_K)]
    cnt = hots[0] + hots[1] + hots[2] + hots[3]
    ri = lax.broadcasted_iota(jnp.int32, (rows, rows), 0)
    ci = lax.broadcasted_iota(jnp.int32, (rows, rows), 1)
    earlier = jnp.where(ci < ri, 1.0, 0.0).astype(BF16)
    ahead = jnp.dot(earlier, cnt.astype(BF16), preferred_element_type=F32) + base_ref[0]
    posmat = jnp.zeros(idx.shape, F32)
    for k in range(TOP_K):
        posmat = jnp.where(lane == k, jnp.sum(hots[k] * ahead, axis=-1, keepdims=True), posmat)
    pos_ref[...] = posmat.T[:TOP_K, :].astype(jnp.int32)


def _route(idx_list, count_list):
    tiles = [c.shape[0] for c in count_list]
    tile_tok = [idx.size // LANES // n for idx, n in zip(idx_list, tiles)]
    n_tok = sum(n * sz for n, sz in zip(tiles, tile_tok))
    n_blocks = n_tok * TOP_K // MOE_ROWS + N_EXPERTS
    cnt = jnp.concatenate([c.reshape(-1, LANES) for c in count_list], axis=0).astype(jnp.int32)
    tile_off = jnp.cumsum(cnt, axis=0) - cnt
    total = jnp.sum(cnt, axis=0)
    padded = (total + MOE_ROWS - 1) // MOE_ROWS * MOE_ROWS
    pend = jnp.cumsum(padded)
    base = ((pend - padded)[None, :] + tile_off).astype(F32)
    starts = jnp.arange(n_blocks, dtype=jnp.int32) * MOE_ROWS
    block_e = jnp.minimum(jnp.sum(pend[None, :N_EXPERTS] <= starts[:, None], axis=1), N_EXPERTS - 1).astype(jnp.int32)
    n_used = (pend[N_EXPERTS - 1] // MOE_ROWS).astype(jnp.int32).reshape(1)
    positions, first = [], 0
    for idx, n_tiles, tile in zip(idx_list, tiles, tile_tok):
        stream_tok = n_tiles * tile
        positions.append(pl.pallas_call(
            _pos_kernel,
            grid=(n_tiles,),
            in_specs=[pl.BlockSpec((tile, LANES), lambda i: (i, 0)),
                      pl.BlockSpec((1, 1, LANES), lambda i: (i, 0, 0))],
            out_specs=pl.BlockSpec((TOP_K, tile), lambda i: (0, i)),
            out_shape=jax.ShapeDtypeStruct((TOP_K, stream_tok), jnp.int32),
            compiler_params=_cparams(("parallel",)),
            name="route_pos",
        )(idx.reshape(stream_tok, LANES), base[first:first + n_tiles].reshape(n_tiles, 1, LANES)))
        first += n_tiles
    return block_e, n_used, positions


def _moe(h2p_list, idx_list, count_list, w1, b1, w2, b2):
    half = h2p_list[0].shape[1]
    d = 2 * half
    block_e, n_used, positions = _route(idx_list, count_list)
    n_blocks = block_e.shape[0]
    xs = _sc_scatter_rows(h2p_list, positions, n_blocks * MOE_ROWS)
    grid_spec = pltpu.PrefetchScalarGridSpec(
        num_scalar_prefetch=2,
        grid=(n_blocks,),
        in_specs=[pl.BlockSpec((MOE_ROWS, half), lambda i, be, nu: (i, 0)),
                  pl.BlockSpec((1, d, 2 * D_FF), lambda i, be, nu: (be[i], 0, 0)),
                  pl.BlockSpec((1, 1, 2 * D_FF), lambda i, be, nu: (be[i], 0, 0)),
                  pl.BlockSpec((1, D_FF, d), lambda i, be, nu: (be[i], 0, 0)),
                  pl.BlockSpec((1, 1, d), lambda i, be, nu: (be[i], 0, 0))],
        out_specs=pl.BlockSpec((MOE_ROWS, half), lambda i, be, nu: (i, 0)),
        scratch_shapes=[pltpu.VMEM((d, 2 * D_FF), BF16), pltpu.VMEM((D_FF, d), BF16)],
    )
    ys = pl.pallas_call(
        _ffn_kernel,
        grid_spec=grid_spec,
        out_shape=jax.ShapeDtypeStruct((n_blocks * MOE_ROWS, half), jnp.int32),
        compiler_params=_cparams(("arbitrary",)),
        name="moe_ffn",
    )(block_e, n_used, xs, w1, b1.reshape(N_EXPERTS, 1, -1), w2, b2.reshape(N_EXPERTS, 1, -1))
    return [_sc_gather_rows(ys, pos.reshape(-1)) for pos in positions]


def _final_kernel(x1_ref, y0_ref, y1_ref, y2_ref, y3_ref, gate_ref, mod_ref, g_ref, *rest):
    o_ref = rest[-1]
    gates = gate_ref[0]
    lo = hi = None
    for k, y_ref in enumerate((y0_ref, y1_ref, y2_ref, y3_ref)):
        y_lo, y_hi = _unpack_bf16_pairs(y_ref[0, 0])
        gk = gates[:, k:k + 1]
        lo = gk * y_lo if k == 0 else lo + gk * y_lo
        hi = gk * y_hi if k == 0 else hi + gk * y_hi
    y = jnp.concatenate([lo, hi], axis=-1)
    gate2 = mod_ref[0, 5:6, :]
    o_ref[0] = x1_ref[0] + gate2 * _rms(y, g_ref[...])


def _final(x1, yg, gate, mod, mod_row, g_post_ffn, tm, b0, n_batch, out_prev):
    bsz, seq, d = x1.shape
    tok3 = lambda b, i: (b, i, 0)
    yg = yg.reshape(TOP_K, bsz, seq, d // 2)
    slot_specs = [pl.BlockSpec((1, 1, tm, d // 2), functools.partial(lambda b, i, k: (k, b, i, 0), k=k))
                  for k in range(TOP_K)]
    in_specs = ([pl.BlockSpec((1, tm, d), tok3)] + slot_specs +
                [pl.BlockSpec((1, tm, LANES), tok3),
                 pl.BlockSpec((1, 6, d), lambda b, i: (mod_row(b), 0, 0)), _const_spec((1, d))])
    args = [x1, yg, yg, yg, yg, gate, mod, g_post_ffn.astype(F32).reshape(1, d)]
    aliases = {}
    if out_prev is not None:
        in_specs.append(pl.BlockSpec(memory_space=pl.ANY))
        args.append(out_prev)
        aliases = {len(args) - 1: 0}
    return pl.pallas_call(
        _final_kernel,
        grid=(bsz, seq // tm),
        in_specs=in_specs,
        out_specs=pl.BlockSpec((1, tm, d), lambda b, i: (b + b0, i, 0)),
        out_shape=jax.ShapeDtypeStruct((n_batch, seq, d), F32),
        input_output_aliases=aliases,
        compiler_params=_cparams(("parallel", "parallel")),
        name="final_residual",
    )(*args)


def _stream(x, mod, mod_row, p, ctx_kv, init_state, rope, tm, b0, bsz):
    _, seq, d = x.shape
    is_ctx = ctx_kv is None
    pj = _project(x, mod, p["g_pre_mix"], p["w_in"], p["b_gates"], mod_row, rope, is_ctx, tm, b0, bsz)
    if is_ctx:
        att = _attention(pj["q"], None, None, pj["kd"], pj["vd"], p["sink"], window=False)
    else:
        att = _attention(pj["q"], pj["kd"], pj["vd"], ctx_kv[0], ctx_kv[1], p["sink"], window=True)
    ml = _mlstm(pj["mq"], pj["mk"], pj["mv"], pj["gates"], init_state, emit_state=is_ctx, b0=b0)
    x1, h2, idx, gate, tile_counts = _merge(att, ml[0], ml[1], pj, x, mod, mod_row, p, tm, b0)
    moe_in = (h2.reshape(bsz * seq, d // 2), idx, tile_counts)
    return x1, gate, moe_in, pj, ml


def kernel(x_prompt, x_sample, c, cache_k, cache_v, state_C, state_n, state_m, c_ctx, w_ada, b_ada, g_pre_mix,
           w_in, b_gates, attn_sink, norm_mlstm, w_up_att, w_up_ml, w_out, g_post_mix, g_pre_ffn, w_router,
           b_router, w1, b1, w2, b2, g_post_ffn):
    depth = w_ada.shape[0]
    n_dec = c.shape[0]
    cond = jnp.concatenate([c_ctx[None, :], c], axis=0).astype(F32)
    cond = jnp.pad(cond, ((0, 16 - cond.shape[0]), (0, 0)))
    y_prompt, y_sample = x_prompt, x_sample
    ks_, vs_, cs_, ns_, ms_ = [], [], [], [], []
    for l in range(depth):
        p = dict(g_pre_mix=g_pre_mix[l], w_in=w_in[l], b_gates=b_gates[l], sink=attn_sink[l], norm_ml=norm_mlstm[l],
                 w_up_att=w_up_att[l], w_up_ml=w_up_ml[l], w_out=w_out[l], g_post_mix=g_post_mix[l],
                 g_pre_ffn=g_pre_ffn[l], w_router=w_router[l], b_router=b_router[l], w1=w1[l], b1=b1[l],
                 w2=w2[l], b2=b2[l], g_post_ffn=g_post_ffn[l])
        mod = _adaln(cond, w_ada[l], b_ada[l]).reshape(16, 6, D_MODEL)
        row_p = lambda b: 0
        bsz, seq = x_prompt.shape[:2]
        x1_p, gate_p, moe_p, pj, ml = _stream(y_prompt, mod, row_p, p, None, None, False, 256, 0, bsz)
        nat = pj["nat"]
        ks_.append(nat[..., :KV_W].reshape(bsz, seq, ATT_KV_HEADS, HEAD_DIM))
        vs_.append(nat[..., KV_W:].reshape(bsz, seq, ATT_KV_HEADS, HEAD_DIM))
        cs_.append(ml[2])
        ns_.append(ml[3])
        ms_.append(ml[4][..., 0])
        ctx_kv = (_dup_heads(cache_k[:, l], False), _dup_heads(cache_v[:, l], True))
        init = (state_C[:, l], state_n[:, l], state_m[:, l])
        half = n_dec // 2
        experts = (p["w1"], p["b1"].astype(F32), p["w2"], p["b2"].astype(F32))
        groups = []
        for b0 in (0, half):
            row_s = functools.partial(lambda b, off: b + off + 1, off=b0)
            sl = slice(b0, b0 + half)
            x1_s, gate_s, moe_s, _, _ = _stream(y_sample, mod, row_s, p, (ctx_kv[0][sl], ctx_kv[1][sl]),
                                                init, True, 512, b0, half)
            groups.append((x1_s, gate_s, moe_s, row_s))
        (x1_a, gate_a, moe_a, row_a), (x1_b, gate_b, moe_b, row_b) = groups
        yg_p, yg_a = _moe([moe_p[0], moe_a[0]], [moe_p[1], moe_a[1]], [moe_p[2], moe_a[2]], *experts)
        (yg_b,) = _moe([moe_b[0]], [moe_b[1]], [moe_b[2]], *experts)
        y_prompt = _final(x1_p, yg_p, gate_p, mod, row_p, p["g_post_ffn"], 256, 0, bsz, None)
        y_sample = _final(x1_a, yg_a, gate_a, mod, row_a, p["g_post_ffn"], 512, 0, n_dec, None)
        y_sample = _final(x1_b, yg_b, gate_b, mod, row_b, p["g_post_ffn"], 512, half, n_dec, y_sample)
    return (y_prompt, y_sample, jnp.stack(ks_, axis=1), jnp.stack(vs_, axis=1), jnp.stack(cs_, axis=1),
            jnp.stack(ns_, axis=1), jnp.stack(ms_, axis=1))
```

```python
import functools

import numpy as np
import jax
import jax.numpy as jnp
from jax import lax
from jax.experimental import pallas as pl
from jax.experimental.pallas import tpu as pltpu
from jax.experimental.pallas import tpu_sc as plsc

F32 = jnp.float32
BF16 = jnp.bfloat16

D_MODEL = 1024
GRID_W = 64
ATT_HEADS = 8
ATT_KV_HEADS = 2
ATT_GROUP = ATT_HEADS // ATT_KV_HEADS
HEAD_DIM = 64
BLOCK = 128
ROPE_THETA = 10000.0
AXIS_FREQS = HEAD_DIM // 4
ML_HEADS = 4
ML_DK = 128
ML_DV = 256
ML_CHUNK = 512
N_EXPERTS = 32
TOP_K = 4
D_FF = 1024
SWIGLU_ALPHA = 1.702
SWIGLU_LIMIT = 7.0
EPS = 1e-6
NEG = -1e30

Q_W = ATT_HEADS * HEAD_DIM
KV_W = ATT_KV_HEADS * HEAD_DIM
MLQK_W = ML_HEADS * ML_DK
MLV_W = ML_HEADS * ML_DV
GATE_W = 4 * ML_HEADS

LANES = 128
VMEM_LIMIT = 56 * 1024 * 1024
MOE_ROWS = 512
SC_CORES = 2
SC_SUBCORES = 16
SC_GATHER_CHUNK = 32


def _cparams(sem):
    return pltpu.CompilerParams(dimension_semantics=sem, vmem_limit_bytes=VMEM_LIMIT)


def _const_spec(shape):
    nd = len(shape)
    return pl.BlockSpec(shape, lambda *_: (0,) * nd)


def _adaln_kernel(cond_ref, w_ref, b_ref, o_ref):
    cnd = cond_ref[...]
    act = cnd * jax.nn.sigmoid(cnd)
    o_ref[...] = jnp.dot(act.astype(BF16), w_ref[...].astype(BF16),
                         preferred_element_type=F32) + b_ref[...]


def _adaln(cond, w_ada, b_ada):
    rows, d = cond.shape
    n = w_ada.shape[1]
    tn = 1536
    return pl.pallas_call(
        _adaln_kernel,
        grid=(n // tn,),
        in_specs=[_const_spec((rows, d)),
                  pl.BlockSpec((d, tn), lambda j: (0, j)),
                  pl.BlockSpec((1, tn), lambda j: (0, j))],
        out_specs=pl.BlockSpec((rows, tn), lambda j: (0, j)),
        out_shape=jax.ShapeDtypeStruct((rows, n), F32),
        compiler_params=_cparams(("parallel",)),
        name="adaln",
    )(cond, w_ada, b_ada.reshape(1, n))


def _pack_w_in(w_in, b_gates, rope, with_nat):
    o_q, o_k, o_v = 0, Q_W, Q_W + KV_W
    o_mq = Q_W + 2 * KV_W
    o_mk = o_mq + MLQK_W
    o_mv = o_mk + MLQK_W
    o_g = o_mv + MLV_W
    o_mo = o_g + GATE_W
    o_ga = o_mo + MLV_W
    o_gm = o_ga + D_MODEL
    half = HEAD_DIM // 2

    def head_cols(base, h, rot):
        lo = base + h * HEAD_DIM
        if rot:
            return [w_in[:, lo + half:lo + HEAD_DIM], w_in[:, lo:lo + half]]
        return [w_in[:, lo:lo + HEAD_DIM]]

    def q_cols(rot):
        cols = []
        for h in range(ATT_HEADS):
            cols += head_cols(o_q, h, rot)
        return cols

    def dup_cols(base, rot):
        cols = []
        for h in range(ATT_KV_HEADS):
            hc = head_cols(base, h, rot)
            cols += hc + hc
        return cols

    segs, cols, off = {}, [], 0

    def add(name, cl):
        nonlocal off
        width = sum(c.shape[1] for c in cl)
        segs[name] = (off, off + width)
        cols.extend(cl)
        off += width

    add("q", q_cols(False))
    if rope:
        add("qrot", q_cols(True))
    add("kd", dup_cols(o_k, False))
    if rope:
        add("kdrot", dup_cols(o_k, True))
    if with_nat:
        add("nat", [w_in[:, o_k:o_k + 2 * KV_W]])
    add("mq", [w_in[:, o_mq:o_mq + MLQK_W]])
    add("mv", [w_in[:, o_mv:o_mv + MLV_W]])
    add("mo", [w_in[:, o_mo:o_mo + MLV_W]])
    add("ga", [w_in[:, o_ga:o_ga + D_MODEL]])
    add("gm", [w_in[:, o_gm:o_gm + D_MODEL]])
    add("gates", [w_in[:, o_g:o_g + GATE_W], jnp.zeros((D_MODEL, LANES - GATE_W), w_in.dtype)])
    wp = jnp.concatenate(cols, axis=1).astype(BF16)
    wvd_t = jnp.concatenate(dup_cols(o_v, False) + [w_in[:, o_mk:o_mk + MLQK_W]], axis=1).T.astype(BF16)
    bg = jnp.pad(b_gates.astype(F32), (0, LANES - GATE_W)).reshape(1, LANES)
    return wp, wvd_t, bg, segs


def _rope_tables(n_tok):
    rows = n_tok // GRID_W
    row = np.repeat(np.arange(rows), GRID_W).astype(np.float32)
    col = np.tile(np.arange(GRID_W), rows).astype(np.float32)
    inv = (np.float32(ROPE_THETA) ** (-np.arange(AXIS_FREQS, dtype=np.float32) / AXIS_FREQS)).astype(np.float32)
    ang = np.concatenate([row[:, None] * inv, col[:, None] * inv], axis=-1).astype(np.float32)
    cos, sin = np.cos(ang), np.sin(ang)
    c64 = np.concatenate([cos, cos], axis=-1)
    s64 = np.concatenate([-sin, sin], axis=-1)
    return (jnp.asarray(np.tile(c64, (1, 2)), F32), jnp.asarray(np.tile(s64, (1, 2)), F32))


def _proj_kernel(*refs, segs, rope, with_nat):
    it = iter(refs)
    x_ref, mod_ref, g_ref, w_ref, wvt_ref, bg_ref = (next(it) for _ in range(6))
    cos_ref = sin_ref = None
    if rope:
        cos_ref, sin_ref = next(it), next(it)
    q_ref, kd_ref, vd_ref = next(it), next(it), next(it)
    nat_ref = next(it) if with_nat else None
    mq_ref, mk_ref, mv_ref, gates_ref, mo_ref, ga_ref, gm_ref = (next(it) for _ in range(7))

    x = x_ref[0]
    shift = mod_ref[0, 0:1, :]
    scale = mod_ref[0, 1:2, :]
    ms = jnp.mean(x * x, axis=-1, keepdims=True)
    xn = x * lax.rsqrt(ms + EPS) * g_ref[...]
    xb = (xn * (1.0 + scale) + shift).astype(BF16)

    def seg(name):
        lo, hi = segs[name]
        return jnp.dot(xb, w_ref[:, lo:hi], preferred_element_type=F32)

    uq = seg("q")
    ukd = seg("kd")
    if rope:
        uqr = seg("qrot")
        ukr = seg("kdrot")
        cs = cos_ref[...]
        sn = sin_ref[...]
        for h in range(ATT_HEADS // 2):
            sl = slice(h * LANES, (h + 1) * LANES)
            q_ref[0, h] = ((uq[:, sl] * cs + uqr[:, sl] * sn) * HEAD_DIM ** -0.5).astype(BF16)
        for h in range(ATT_KV_HEADS):
            sl = slice(h * LANES, (h + 1) * LANES)
            kd_ref[0, :, sl] = (ukd[:, sl] * cs + ukr[:, sl] * sn).astype(BF16)
    else:
        for h in range(ATT_HEADS // 2):
            sl = slice(h * LANES, (h + 1) * LANES)
            q_ref[0, h] = (uq[:, sl] * HEAD_DIM ** -0.5).astype(BF16)
        kd_ref[0] = ukd.astype(BF16)
    ut = lax.dot_general(wvt_ref[...], xb, (((1,), (1,)), ((), ())), preferred_element_type=F32)
    vd_ref[0] = ut[:2 * LANES].astype(BF16)
    mk_ref[0] = ut[2 * LANES:].astype(BF16)
    if with_nat:
        nat_ref[0] = seg("nat")
    mq_ref[0] = (seg("mq") * ML_DK ** -0.5).astype(BF16)
    mv_ref[0] = seg("mv").astype(BF16)
    gates_ref[0] = seg("gates") + bg_ref[...]
    mo_ref[0] = seg("mo")
    ga_ref[0] = seg("ga")
    gm_ref[0] = seg("gm")


def _project(x, mod, g_pre, w_in, b_gates, mod_row, rope, with_nat, tm, b0, bsz):
    _, seq, d = x.shape
    wp, wvd_t, bg, segs = _pack_w_in(w_in, b_gates, rope, with_nat)
    nw = wp.shape[1]
    tok3 = lambda b, i: (b, i, 0)
    in_specs = [pl.BlockSpec((1, tm, d), lambda b, i: (b + b0, i, 0)),
                pl.BlockSpec((1, 6, d), lambda b, i: (mod_row(b), 0, 0)),
                _const_spec((1, d)),
                pl.BlockSpec((d, nw), lambda b, i: (0, 0), pipeline_mode=pl.Buffered(1)),
                _const_spec((2 * LANES + MLQK_W, d)), _const_spec((1, LANES))]
    args = [x, mod, g_pre.reshape(1, d), wp, wvd_t, bg]
    if rope:
        cos_t, sin_t = _rope_tables(seq)
        in_specs += [pl.BlockSpec((tm, LANES), lambda b, i: (i, 0))] * 2
        args += [cos_t, sin_t]

    def tok_out(width, dtype):
        return (jax.ShapeDtypeStruct((bsz, seq, width), dtype), pl.BlockSpec((1, tm, width), tok3))

    outs = [(jax.ShapeDtypeStruct((bsz, ATT_HEADS // 2, seq, LANES), BF16),
             pl.BlockSpec((1, ATT_HEADS // 2, tm, LANES), lambda b, i: (b, 0, i, 0))),
            tok_out(2 * LANES, BF16),
            (jax.ShapeDtypeStruct((bsz, 2 * LANES, seq), BF16),
             pl.BlockSpec((1, 2 * LANES, tm), lambda b, i: (b, 0, i)))]
    if with_nat:
        outs.append(tok_out(2 * KV_W, F32))
    outs += [tok_out(MLQK_W, BF16),
             (jax.ShapeDtypeStruct((bsz, MLQK_W, seq), BF16),
              pl.BlockSpec((1, MLQK_W, tm), lambda b, i: (b, 0, i))),
             tok_out(MLV_W, BF16), tok_out(LANES, F32),
             tok_out(MLV_W, F32), tok_out(D_MODEL, F32), tok_out(D_MODEL, F32)]
    res = pl.pallas_call(
        functools.partial(_proj_kernel, segs=segs, rope=rope, with_nat=with_nat),
        grid=(bsz, seq // tm),
        in_specs=in_specs,
        out_specs=[o[1] for o in outs],
        out_shape=[o[0] for o in outs],
        compiler_params=_cparams(("parallel", "parallel")),
        name="proj_rope" if rope else "proj_ctx",
    )(*args)
    names = ["q", "kd", "vd"] + (["nat"] if with_nat else []) + ["mq", "mk", "mv", "gates", "mo", "ga", "gm"]
    return dict(zip(names, res))


def _attn_kernel(*refs, window, n_blocks, n_ctx):
    it = iter(refs)
    sink_ref, q_ref = next(it), next(it)
    if window:
        kp_ref, kc_ref, kn_ref, vp_ref, vc_ref, vn_ref = (next(it) for _ in range(6))
    ck_ref, cv_ref, o_ref = next(it), next(it), next(it)
    j = pl.program_id(1)
    n_win = 3 * BLOCK if window else 0
    cols = 2 * BLOCK

    col_id = lax.broadcasted_iota(jnp.int32, (1, cols), 1)
    if window:
        kl = lax.broadcasted_iota(jnp.int32, (BLOCK, cols), 0)
        ql = lax.broadcasted_iota(jnp.int32, (BLOCK, cols), 1) & (BLOCK - 1)
        prev_ok = kl >= ql + jnp.where(j == 0, BLOCK, 0)
        next_ok = kl <= ql - jnp.where(j == n_blocks - 1, BLOCK, 0)
    lane_lo = lax.broadcasted_iota(jnp.int32, (BLOCK, LANES), 1) < HEAD_DIM
    lane_row = lax.broadcasted_iota(jnp.int32, (1, LANES), 1) < HEAD_DIM
    half_lo = jnp.where(lane_row, 1.0, 0.0).astype(BF16)
    half_hi = jnp.where(lane_row, 0.0, 1.0).astype(BF16)

    for kvh in range(ATT_KV_HEADS):
        sl = slice(kvh * LANES, (kvh + 1) * LANES)
        if window:
            keys = jnp.concatenate([kp_ref[0, :, sl], kc_ref[0, :, sl], kn_ref[0, :, sl], ck_ref[0, :, sl]], axis=0)
            vals_t = jnp.concatenate([vp_ref[0, sl, :], vc_ref[0, sl, :], vn_ref[0, sl, :], cv_ref[0, sl, :]], axis=1)
        else:
            keys = ck_ref[0, :, sl]
            vals_t = cv_ref[0, sl, :]
        for pair in range(ATT_GROUP // 2):
            pair_id = kvh * (ATT_GROUP // 2) + pair
            qp = q_ref[0, pair_id]
            qs = jnp.concatenate([qp * half_lo, qp * half_hi], axis=0)
            s = lax.dot_general(keys, qs, (((1,), (1,)), ((), ())), preferred_element_type=F32)
            if window:
                s = jnp.concatenate([jnp.where(prev_ok, s[:BLOCK], NEG), s[BLOCK:2 * BLOCK],
                                     jnp.where(next_ok, s[2 * BLOCK:n_win], NEG), s[n_win:]], axis=0)
            snk = jnp.where(col_id < BLOCK, sink_ref[2 * pair_id], sink_ref[2 * pair_id + 1])
            m = jnp.maximum(jnp.max(s, axis=0, keepdims=True), snk)
            p = jnp.exp(s - m)
            den = jnp.sum(p, axis=0, keepdims=True) + jnp.exp(snk - m)
            o_t = jnp.dot(vals_t, p.astype(BF16), preferred_element_type=F32) / den
            o_ref[0, :, pair_id * LANES:(pair_id + 1) * LANES] = jnp.where(
                lane_lo, o_t[:, :BLOCK].T, o_t[:, BLOCK:].T).astype(BF16)


def _attention(q, kd, vd_t, ckd, cvd_t, sink, window):
    bsz, _, seq, _ = q.shape
    nb = seq // BLOCK
    n_ctx = ckd.shape[1]
    in_specs = [pl.BlockSpec(memory_space=pltpu.SMEM),
                pl.BlockSpec((1, ATT_HEADS // 2, BLOCK, LANES), lambda b, j: (b, 0, j, 0))]
    args = [sink.astype(F32), q]
    if window:
        in_specs += [pl.BlockSpec((1, BLOCK, 2 * LANES), lambda b, j: (b, jnp.maximum(j - 1, 0), 0)),
                     pl.BlockSpec((1, BLOCK, 2 * LANES), lambda b, j: (b, j, 0)),
                     pl.BlockSpec((1, BLOCK, 2 * LANES), lambda b, j: (b, jnp.minimum(j + 1, nb - 1), 0)),
                     pl.BlockSpec((1, 2 * LANES, BLOCK), lambda b, j: (b, 0, jnp.maximum(j - 1, 0))),
                     pl.BlockSpec((1, 2 * LANES, BLOCK), lambda b, j: (b, 0, j)),
                     pl.BlockSpec((1, 2 * LANES, BLOCK), lambda b, j: (b, 0, jnp.minimum(j + 1, nb - 1)))]
        args += [kd, kd, kd, vd_t, vd_t, vd_t]
    in_specs += [pl.BlockSpec((1, n_ctx, 2 * LANES), lambda b, j: (b, 0, 0)),
                 pl.BlockSpec((1, 2 * LANES, n_ctx), lambda b, j: (b, 0, 0))]
    args += [ckd, cvd_t]
    return pl.pallas_call(
        functools.partial(_attn_kernel, window=window, n_blocks=nb, n_ctx=n_ctx),
        grid=(bsz, nb),
        in_specs=in_specs,
        out_specs=pl.BlockSpec((1, BLOCK, Q_W), lambda b, j: (b, j, 0)),
        out_shape=jax.ShapeDtypeStruct((bsz, seq, Q_W), BF16),
        compiler_params=_cparams(("parallel", "parallel")),
        name="attn_window" if window else "attn_ctx",
    )(*args)


def _dup_heads(t, transpose):
    b, l = t.shape[:2]
    out = jnp.concatenate([t, t], axis=-1).reshape(b, l, ATT_KV_HEADS * LANES).astype(BF16)
    return jnp.swapaxes(out, 1, 2) if transpose else out


def _split3(x):
    hi = x.astype(BF16)
    r1 = x - hi.astype(F32)
    mid = r1.astype(BF16)
    lo = (r1 - mid.astype(F32)).astype(BF16)
    return hi, mid, lo


def _log_sigmoid(x):
    return jnp.minimum(x, 0.0) - jnp.log1p(jnp.exp(-jnp.abs(x)))


def _mlstm_kernel(*refs, chunk_len, n_chunks, has_init, emit_state):
    it = iter(refs)
    qkvg = [[next(it) for _ in range(4)] for _ in range(2)]
    if has_init:
        c0_ref, n0_ref, m0_ref = next(it), next(it), next(it)
    h_refs = [next(it), next(it)]
    if emit_state:
        co_ref, no_ref, mo_ref = next(it), next(it), next(it)
    st_s, m_s = next(it), next(it)
    t = pl.program_id(1)
    L = chunk_len

    @pl.when(t == 0)
    def _():
        for d in range(2):
            for h in range(ML_HEADS):
                if has_init:
                    st_s[d, h, :, :ML_DV] = c0_ref[0, d, h].T
                    st_s[d, h, :, ML_DV:] = jnp.broadcast_to(n0_ref[0, d, h:h + 1, :], (LANES, ML_DK)).T
                    m_s[d, h] = m0_ref[0, d, h:h + 1, :]
                else:
                    st_s[d, h] = jnp.zeros((ML_DK, ML_DV + LANES), F32)
                    m_s[d, h] = jnp.zeros((1, LANES), F32)

    ri = lax.broadcasted_iota(jnp.int32, (L, L), 0)
    ci = lax.broadcasted_iota(jnp.int32, (L, L), 1)
    ones_blk = jnp.ones((L, LANES), BF16)

    for d in range(2):
        q_ref, kt_ref, v_ref, g_ref = qkvg[d]
        tri = (ci <= ri) if d == 0 else (ci >= ri)
        tri_b = jnp.where(tri, 1.0, 0.0).astype(BF16)
        g = g_ref[0]
        lf = _log_sigmoid(g)
        g_t = g.T
        lf_t = lf.T
        bc = sum(jnp.dot(tri_b, part, preferred_element_type=F32) for part in _split3(lf))
        br = sum(lax.dot_general(part, tri_b, (((1,), (1,)), ((), ())), preferred_element_type=F32)
                 for part in _split3(lf_t))
        tot = jnp.sum(lf, axis=0, keepdims=True)

        for h in range(ML_HEADS):
            icol_i = 2 * ML_HEADS * d + h
            fcol = icol_i + ML_HEADS
            bcol = bc[:, fcol:fcol + 1]
            brow = br[fcol:fcol + 1, :]
            irow = g_t[icol_i:icol_i + 1, :]
            b_last = tot[:, fcol:fcol + 1]
            m_prev = m_s[d, h][:, 0:1]
            st_prev = st_s[d, h]
            qh = q_ref[0, :, h * ML_DK:(h + 1) * ML_DK]
            kt = kt_ref[0, h * ML_DK:(h + 1) * ML_DK, :]
            v_aug = jnp.concatenate([v_ref[0, :, h * ML_DV:(h + 1) * ML_DV], ones_blk], axis=1)

            rel = jnp.where(tri, irow - brow, NEG)
            inter = bcol + m_prev
            m_t = jnp.maximum(inter, bcol + jnp.max(rel, axis=-1, keepdims=True))
            w = jnp.exp(rel + (bcol - m_t))
            w_inter = jnp.exp(inter - m_t)
            s = jnp.dot(qh, kt, preferred_element_type=F32) * w
            lhs = jnp.concatenate([s.astype(BF16), (qh.astype(F32) * w_inter).astype(BF16)], axis=1)
            rhs = jnp.concatenate([v_aug, st_prev.astype(BF16)], axis=0)
            mix = jnp.dot(lhs, rhs, preferred_element_type=F32)
            inv = 1.0 / jnp.maximum(jnp.abs(mix[:, ML_DV:]), jnp.exp(-m_t))
            for c in range(ML_DV // LANES):
                lo = h * ML_DV + c * LANES
                h_refs[d][0, :, lo:lo + LANES] = mix[:, c * LANES:(c + 1) * LANES] * inv

            a = b_last - brow + irow
            m_new = jnp.maximum(b_last + m_prev, jnp.max(a, axis=-1, keepdims=True))
            wk = jnp.exp(a - m_new)
            decay = jnp.exp(b_last + m_prev - m_new)
            kw_t = (kt.astype(F32) * wk).astype(BF16)
            st_new = decay * st_prev + jnp.dot(kw_t, v_aug, preferred_element_type=F32)
            st_s[d, h] = st_new
            m_s[d, h] = jnp.broadcast_to(m_new, (1, LANES))
            if emit_state:
                @pl.when(t == n_chunks - 1)
                def _():
                    co_ref[0, d, h] = st_new[:, :ML_DV].T
                    no_ref[0, d, h:h + 1, :] = st_new[:, ML_DV:].T[0:1, :]
                    mo_ref[0, d, h:h + 1, :] = jnp.broadcast_to(m_new, (1, LANES))


def _mlstm(mq, mk_t, mv, gates, init, emit_state, b0=0):
    bsz, seq, _ = mq.shape
    chunk_len = min(ML_CHUNK, seq)
    nc = seq // chunk_len
    fwd = lambda b, t: (b, t, 0)
    bwd = lambda b, t: (b, nc - 1 - t, 0)
    fwd_t = lambda b, t: (b, 0, t)
    bwd_t = lambda b, t: (b, 0, nc - 1 - t)
    state5 = lambda b, t: (b, 0, 0, 0, 0)
    state4 = lambda b, t: (b, 0, 0, 0)
    init5 = lambda b, t: (b + b0, 0, 0, 0, 0)
    init4 = lambda b, t: (b + b0, 0, 0, 0)
    in_specs, args = [], []
    for chunk, chunk_t in ((fwd, fwd_t), (bwd, bwd_t)):
        in_specs += [pl.BlockSpec((1, chunk_len, MLQK_W), chunk), pl.BlockSpec((1, MLQK_W, chunk_len), chunk_t),
                     pl.BlockSpec((1, chunk_len, MLV_W), chunk), pl.BlockSpec((1, chunk_len, LANES), chunk)]
        args += [mq, mk_t, mv, gates]
    if init is not None:
        c0, n0, m0 = init
        in_specs += [pl.BlockSpec((1, 2, ML_HEADS, ML_DV, ML_DK), init5),
                     pl.BlockSpec((1, 2, ML_HEADS, ML_DK), init4),
                     pl.BlockSpec((1, 2, ML_HEADS, LANES), init4)]
        args += [c0.astype(F32), n0.astype(F32),
                 jnp.broadcast_to(m0.astype(F32)[..., None], m0.shape + (LANES,))]
    out_shape = [jax.ShapeDtypeStruct((bsz, seq, MLV_W), F32)] * 2
    out_specs = [pl.BlockSpec((1, chunk_len, MLV_W), fwd), pl.BlockSpec((1, chunk_len, MLV_W), bwd)]
    if emit_state:
        out_shape += [jax.ShapeDtypeStruct((bsz, 2, ML_HEADS, ML_DV, ML_DK), F32),
                      jax.ShapeDtypeStruct((bsz, 2, ML_HEADS, ML_DK), F32),
                      jax.ShapeDtypeStruct((bsz, 2, ML_HEADS, LANES), F32)]
        out_specs += [pl.BlockSpec((1, 2, ML_HEADS, ML_DV, ML_DK), state5),
                      pl.BlockSpec((1, 2, ML_HEADS, ML_DK), state4),
                      pl.BlockSpec((1, 2, ML_HEADS, LANES), state4)]
    return pl.pallas_call(
        functools.partial(_mlstm_kernel, chunk_len=chunk_len, n_chunks=nc, has_init=init is not None, emit_state=emit_state),
        grid=(bsz, nc),
        in_specs=in_specs,
        out_specs=out_specs,
        out_shape=out_shape,
        scratch_shapes=[pltpu.VMEM((2, ML_HEADS, ML_DK, ML_DV + LANES), F32),
                        pltpu.VMEM((2, ML_HEADS, 1, LANES), F32)],
        compiler_params=_cparams(("parallel", "arbitrary")),
        name="mlstm_state" if emit_state else "mlstm",
    )(*args)


def _sigmoid(x):
    return 0.5 * jnp.tanh(0.5 * x) + 0.5


def _rms(x, g):
    return x * lax.rsqrt(jnp.mean(x * x, axis=-1, keepdims=True) + EPS) * g


def _pack_bf16_pairs(x):
    n = x.shape[1] // 2
    lo = pltpu.bitcast(x[:, :n].astype(BF16).astype(F32), jnp.int32)
    hi = pltpu.bitcast(x[:, n:].astype(BF16).astype(F32), jnp.int32)
    return lax.shift_right_logical(lo, 16) | hi


def _unpack_bf16_pairs(p):
    lo = pltpu.bitcast(lax.shift_left(p, 16), F32)
    hi = pltpu.bitcast(p & jnp.int32(-65536), F32)
    return lo, hi


def _merge_kernel(att_ref, hf_ref, hb_ref, mo_ref, ga_ref, gm_ref, x_ref, mod_ref, nml_ref, gpm_ref, gpf_ref,
                  wua_ref, wum_ref, wo_ref, wr_ref, br_ref, x1_ref, h2_ref, idx_ref, gate_ref, cnt_ref):
    hsum = hf_ref[0] + hb_ref[0]
    parts = []
    for h in range(ML_HEADS):
        hh = hsum[:, h * ML_DV:(h + 1) * ML_DV]
        parts.append(hh * lax.rsqrt(jnp.mean(hh * hh, axis=-1, keepdims=True) + EPS))
    hn = jnp.concatenate(parts, axis=-1) * nml_ref[...]
    ml = (hn * _sigmoid(mo_ref[0])).astype(BF16)
    a = jnp.dot(att_ref[0], wua_ref[...], preferred_element_type=F32)
    m = jnp.dot(ml, wum_ref[...], preferred_element_type=F32)
    z = (_sigmoid(ga_ref[0]) * a + _sigmoid(gm_ref[0]) * m).astype(BF16)
    mix = jnp.dot(z, wo_ref[...], preferred_element_type=F32)
    gate1 = mod_ref[0, 2:3, :]
    shift2 = mod_ref[0, 3:4, :]
    scale2 = mod_ref[0, 4:5, :]
    x1 = x_ref[0] + gate1 * _rms(mix, gpm_ref[...])
    x1_ref[0] = x1
    h2 = _rms(x1, gpf_ref[...]) * (1.0 + scale2) + shift2
    h2_ref[0] = _pack_bf16_pairs(h2)

    h2_hi = h2.astype(BF16)
    h2_lo = (h2 - h2_hi.astype(F32)).astype(BF16)
    wr = wr_ref[...]
    wr_hi = wr.astype(BF16)
    wr_lo = (wr - wr_hi.astype(F32)).astype(BF16)
    logits = (jnp.dot(h2_hi, wr_hi, preferred_element_type=F32)
              + jnp.dot(h2_lo, wr_hi, preferred_element_type=F32)
              + jnp.dot(h2_hi, wr_lo, preferred_element_type=F32)) + br_ref[...]
    lane = lax.broadcasted_iota(jnp.int32, logits.shape, 1).astype(F32)
    work = jnp.where(lane < N_EXPERTS, logits, -jnp.inf)
    idx_out = jnp.zeros(logits.shape, F32)
    val_out = jnp.zeros(logits.shape, F32)
    picked = jnp.zeros(logits.shape, F32)
    top0 = None
    esum = None
    for k in range(TOP_K):
        mx = jnp.max(work, axis=-1, keepdims=True)
        sel = jnp.min(jnp.where(work == mx, lane, float(LANES)), axis=-1, keepdims=True)
        if k == 0:
            top0 = mx
        e = jnp.exp(mx - top0)
        esum = e if k == 0 else esum + e
        idx_out = jnp.where(lane == k, sel, idx_out)
        val_out = jnp.where(lane == k, e, val_out)
        picked = jnp.where(lane == sel, 1.0, picked)
        work = jnp.where(lane == sel, -jnp.inf, work)
    idx_ref[0] = idx_out.astype(jnp.int32)
    gate_ref[0] = val_out / esum
    cnt_ref[0] = jnp.sum(picked, axis=0, keepdims=True)


def _merge(att, hf, hb, pj, x, mod, mod_row, p, tm, b0):
    bsz = att.shape[0]
    _, seq, d = x.shape
    tok3 = lambda b, i: (b, i, 0)
    wr = jnp.pad(p["w_router"].astype(F32), ((0, 0), (0, LANES - N_EXPERTS)))
    br = jnp.pad(p["b_router"].astype(F32), (0, LANES - N_EXPERTS)).reshape(1, LANES)
    row = lambda v: v.astype(F32).reshape(1, -1)
    in_specs = [pl.BlockSpec((1, tm, Q_W), tok3),
                pl.BlockSpec((1, tm, MLV_W), tok3), pl.BlockSpec((1, tm, MLV_W), tok3),
                pl.BlockSpec((1, tm, MLV_W), tok3), pl.BlockSpec((1, tm, d), tok3), pl.BlockSpec((1, tm, d), tok3),
                pl.BlockSpec((1, tm, d), lambda b, i: (b + b0, i, 0)),
                pl.BlockSpec((1, 6, d), lambda b, i: (mod_row(b), 0, 0)),
                _const_spec((1, MLV_W)), _const_spec((1, d)), _const_spec((1, d)),
                _const_spec((Q_W, d)), _const_spec((MLV_W, d)), _const_spec((d, d)),
                _const_spec((d, LANES)), _const_spec((1, LANES))]
    out_shape = [jax.ShapeDtypeStruct((bsz, seq, d), F32), jax.ShapeDtypeStruct((bsz, seq, d // 2), jnp.int32),
                 jax.ShapeDtypeStruct((bsz, seq, LANES), jnp.int32), jax.ShapeDtypeStruct((bsz, seq, LANES), F32),
                 jax.ShapeDtypeStruct((bsz * (seq // tm), 1, LANES), F32)]
    out_specs = [pl.BlockSpec((1, tm, d), tok3), pl.BlockSpec((1, tm, d // 2), tok3),
                 pl.BlockSpec((1, tm, LANES), tok3), pl.BlockSpec((1, tm, LANES), tok3),
                 pl.BlockSpec((1, 1, LANES), lambda b, i: (b * (seq // tm) + i, 0, 0))]
    return pl.pallas_call(
        _merge_kernel,
        grid=(bsz, seq // tm),
        in_specs=in_specs,
        out_specs=out_specs,
        out_shape=out_shape,
        compiler_params=_cparams(("parallel", "parallel")),
        name="merge_router",
    )(att, hf, hb, pj["mo"], pj["ga"], pj["gm"], x, mod, row(p["norm_ml"]), row(p["g_post_mix"]),
      row(p["g_pre_ffn"]), p["w_up_att"].astype(BF16), p["w_up_ml"].astype(BF16), p["w_out"].astype(BF16), wr, br)


def _sc_gather_rows(table, idx):
    n = idx.shape[0]
    width = table.shape[1]
    chunk = SC_GATHER_CHUNK
    n_workers = SC_CORES * SC_SUBCORES
    rows_per_worker = n // n_workers
    n_pairs = rows_per_worker // (2 * chunk)
    assert n_pairs * 2 * chunk * n_workers == n
    mesh = plsc.VectorSubcoreMesh(core_axis_name="c", subcore_axis_name="s")

    def body(table_hbm, idx_hbm, out_hbm, idx_v, rows_v, gsem, osem):
        wid = lax.axis_index("s") * SC_CORES + lax.axis_index("c")
        base0 = wid * rows_per_worker

        def gather_copy(b):
            return pltpu.make_async_copy(table_hbm.at[idx_v.at[b]], rows_v.at[b], gsem.at[b])

        def write_copy(ci, b):
            base = pl.multiple_of(base0 + ci * chunk, 8)
            return pltpu.make_async_copy(rows_v.at[b], out_hbm.at[pl.ds(base, chunk)], osem.at[b])

        def issue(ci, b):
            base = pl.multiple_of(base0 + ci * chunk, 8)
            pltpu.sync_copy(idx_hbm.at[pl.ds(base, chunk)], idx_v.at[b])
            gather_copy(b).start()

        def finish(ci, b):
            gather_copy(b).wait()
            write_copy(ci, b).start()

        issue(0, 0)

        @pl.loop(0, n_pairs)
        def _(j):
            @pl.when(j > 0)
            def _():
                write_copy(2 * j - 1, 1).wait()
            issue(2 * j + 1, 1)
            finish(2 * j, 0)

            @pl.when(j < n_pairs - 1)
            def _():
                write_copy(2 * j, 0).wait()
                issue(2 * j + 2, 0)
            finish(2 * j + 1, 1)

        write_copy(2 * n_pairs - 2, 0).wait()
        write_copy(2 * n_pairs - 1, 1).wait()

    return pl.kernel(
        body, mesh=mesh,
        out_type=jax.ShapeDtypeStruct((n, width), table.dtype),
        scratch_types=[pltpu.VMEM((2, chunk), jnp.int32), pltpu.VMEM((2, chunk, width), table.dtype),
                       pltpu.SemaphoreType.DMA((2,)), pltpu.SemaphoreType.DMA((2,))],
    )(table, idx)


def _ffn_kernel(be_ref, nu_ref, xs_ref, w1_ref, b1_ref, w2_ref, b2_ref, ys_ref, w1b_s, w2b_s):
    i = pl.program_id(0)
    e = be_ref[i]
    e_prev = be_ref[jnp.maximum(i - 1, 0)]

    @pl.when((i == 0) | (e != e_prev))
    def _():
        w1b_s[...] = w1_ref[0].astype(BF16)
        w2b_s[...] = w2_ref[0].astype(BF16)

    @pl.when(i < nu_ref[0])
    def _():
        x_lo, x_hi = _unpack_bf16_pairs(xs_ref[...])
        xb = jnp.concatenate([x_lo.astype(BF16), x_hi.astype(BF16)], axis=1)
        hmid = jnp.dot(xb, w1b_s[...], preferred_element_type=F32) + b1_ref[0]
        glu = jnp.minimum(hmid[:, :D_FF], SWIGLU_LIMIT)
        lin = jnp.clip(hmid[:, D_FF:], -SWIGLU_LIMIT, SWIGLU_LIMIT)
        act = ((lin + 1.0) * glu * _sigmoid(SWIGLU_ALPHA * glu)).astype(BF16)
        y = jnp.dot(act, w2b_s[...], preferred_element_type=F32) + b2_ref[0]
        ys_ref[...] = _pack_bf16_pairs(y)

    @pl.when(i >= nu_ref[0])
    def _():
        ys_ref[...] = jnp.zeros(ys_ref.shape, jnp.int32)


def _sc_scatter_rows(tables, positions, n_rows):
    width = tables[0].shape[1]
    chunk = SC_GATHER_CHUNK
    n_workers = SC_CORES * SC_SUBCORES
    n_streams = len(tables)
    mesh = plsc.VectorSubcoreMesh(core_axis_name="c", subcore_axis_name="s")

    def body(*refs):
        table_refs = refs[0:2 * n_streams:2]
        pos_refs = refs[1:2 * n_streams:2]
        out_hbm, idx_v, rows_v, lsem, ssem = refs[2 * n_streams:]
        wid = lax.axis_index("s") * SC_CORES + lax.axis_index("c")

        for table_hbm, pos_hbm in zip(table_refs, pos_refs):
            tok_per_worker = table_hbm.shape[0] // n_workers
            n_pairs = tok_per_worker // (2 * chunk)
            assert n_pairs * 2 * chunk * n_workers == table_hbm.shape[0]
            base0 = wid * tok_per_worker

            def load_copy(ci, b):
                base = pl.multiple_of(base0 + ci * chunk, 8)
                return pltpu.make_async_copy(table_hbm.at[pl.ds(base, chunk)], rows_v.at[b], lsem.at[b])

            def scatter_copy(b, k):
                return pltpu.make_async_copy(rows_v.at[b], out_hbm.at[idx_v.at[b, k]], ssem.at[b])

            def load(ci, b):
                base = pl.multiple_of(base0 + ci * chunk, 8)
                load_copy(ci, b).start()
                for k in range(TOP_K):
                    pltpu.sync_copy(pos_hbm.at[k, pl.ds(base, chunk)], idx_v.at[b, k])

            def scatter(ci, b):
                load_copy(ci, b).wait()
                for k in range(TOP_K):
                    scatter_copy(b, k).start()

            def drain(b):
                for k in range(TOP_K):
                    scatter_copy(b, k).wait()

            load(0, 0)

            @pl.loop(0, n_pairs)
            def _(j):
                @pl.when(j > 0)
                def _():
                    drain(1)
                load(2 * j + 1, 1)
                scatter(2 * j, 0)

                @pl.when(j < n_pairs - 1)
                def _():
                    drain(0)
                    load(2 * j + 2, 0)
                scatter(2 * j + 1, 1)

            drain(0)
            drain(1)

    args = [a for pair in zip(tables, positions) for a in pair]
    return pl.kernel(
        body, mesh=mesh,
        out_type=jax.ShapeDtypeStruct((n_rows, width), tables[0].dtype),
        scratch_types=[pltpu.VMEM((2, TOP_K, chunk), jnp.int32), pltpu.VMEM((2, chunk, width), tables[0].dtype),
                       pltpu.SemaphoreType.DMA((2,)), pltpu.SemaphoreType.DMA((2,))],
    )(*args)


def _pos_kernel(idx_ref, base_ref, pos_ref):
    idx = idx_ref[...]
    rows = idx.shape[0]
    lane = lax.broadcasted_iota(jnp.int32, idx.shape, 1)
    hots = [jnp.where(lane == idx[:, k:k + 1], 1.0, 0.0) for k in range(TOP_K)]
    cnt = hots[0] + hots[1] + hots[2] + hots[3]
    ri = lax.broadcasted_iota(jnp.int32, (rows, rows), 0)
    ci = lax.broadcasted_iota(jnp.int32, (rows, rows), 1)
    earlier = jnp.where(ci < ri, 1.0, 0.0).astype(BF16)
    ahead = jnp.dot(earlier, cnt.astype(BF16), preferred_element_type=F32) + base_ref[0]
    posmat = jnp.zeros(idx.shape, F32)
    for k in range(TOP_K):
        posmat = jnp.where(lane == k, jnp.sum(hots[k] * ahead, axis=-1, keepdims=True), posmat)
    pos_ref[...] = posmat.T[:TOP_K, :].astype(jnp.int32)


def _route(idx_list, count_list):
    tiles = [c.shape[0] for c in count_list]
    tile_tok = [idx.size // LANES // n for idx, n in zip(idx_list, tiles)]
    n_tok = sum(n * sz for n, sz in zip(tiles, tile_tok))
    n_blocks = n_tok * TOP_K // MOE_ROWS + N_EXPERTS
    cnt = jnp.concatenate([c.reshape(-1, LANES) for c in count_list], axis=0).astype(jnp.int32)
    tile_off = jnp.cumsum(cnt, axis=0) - cnt
    total = jnp.sum(cnt, axis=0)
    padded = (total + MOE_ROWS - 1) // MOE_ROWS * MOE_ROWS
    pend = jnp.cumsum(padded)
    base = ((pend - padded)[None, :] + tile_off).astype(F32)
    starts = jnp.arange(n_blocks, dtype=jnp.int32) * MOE_ROWS
    block_e = jnp.minimum(jnp.sum(pend[None, :N_EXPERTS] <= starts[:, None], axis=1), N_EXPERTS - 1).astype(jnp.int32)
    n_used = (pend[N_EXPERTS - 1] // MOE_ROWS).astype(jnp.int32).reshape(1)
    positions, first = [], 0
    for idx, n_tiles, tile in zip(idx_list, tiles, tile_tok):
        stream_tok = n_tiles * tile
        positions.append(pl.pallas_call(
            _pos_kernel,
            grid=(n_tiles,),
            in_specs=[pl.BlockSpec((tile, LANES), lambda i: (i, 0)),
                      pl.BlockSpec((1, 1, LANES), lambda i: (i, 0, 0))],
            out_specs=pl.BlockSpec((TOP_K, tile), lambda i: (0, i)),
            out_shape=jax.ShapeDtypeStruct((TOP_K, stream_tok), jnp.int32),
            compiler_params=_cparams(("parallel",)),
            name="route_pos",
        )(idx.reshape(stream_tok, LANES), base[first:first + n_tiles].reshape(n_tiles, 1, LANES)))
        first += n_tiles
    return block_e, n_used, positions


def _moe(h2p_list, idx_list, count_list, w1, b1, w2, b2):
    half = h2p_list[0].shape[1]
    d = 2 * half
    block_e, n_used, positions = _route(idx_list, count_list)
    n_blocks = block_e.shape[0]
    xs = _sc_scatter_rows(h2p_list, positions, n_blocks * MOE_ROWS)
    grid_spec = pltpu.PrefetchScalarGridSpec(
        num_scalar_prefetch=2,
        grid=(n_blocks,),
        in_specs=[pl.BlockSpec((MOE_ROWS, half), lambda i, be, nu: (i, 0)),
                  pl.BlockSpec((1, d, 2 * D_FF), lambda i, be, nu: (be[i], 0, 0)),
                  pl.BlockSpec((1, 1, 2 * D_FF), lambda i, be, nu: (be[i], 0, 0)),
                  pl.BlockSpec((1, D_FF, d), lambda i, be, nu: (be[i], 0, 0)),
                  pl.BlockSpec((1, 1, d), lambda i, be, nu: (be[i], 0, 0))],
        out_specs=pl.BlockSpec((MOE_ROWS, half), lambda i, be, nu: (i, 0)),
        scratch_shapes=[pltpu.VMEM((d, 2 * D_FF), BF16), pltpu.VMEM((D_FF, d), BF16)],
    )
    ys = pl.pallas_call(
        _ffn_kernel,
        grid_spec=grid_spec,
        out_shape=jax.ShapeDtypeStruct((n_blocks * MOE_ROWS, half), jnp.int32),
        compiler_params=_cparams(("arbitrary",)),
        name="moe_ffn",
    )(block_e, n_used, xs, w1, b1.reshape(N_EXPERTS, 1, -1), w2, b2.reshape(N_EXPERTS, 1, -1))
    return [_sc_gather_rows(ys, pos.reshape(-1)) for pos in positions]


def _final_kernel(x1_ref, y0_ref, y1_ref, y2_ref, y3_ref, gate_ref, mod_ref, g_ref, *rest):
    o_ref = rest[-1]
    gates = gate_ref[0]
    lo = hi = None
    for k, y_ref in enumerate((y0_ref, y1_ref, y2_ref, y3_ref)):
        y_lo, y_hi = _unpack_bf16_pairs(y_ref[0, 0])
        gk = gates[:, k:k + 1]
        lo = gk * y_lo if k == 0 else lo + gk * y_lo
        hi = gk * y_hi if k == 0 else hi + gk * y_hi
    y = jnp.concatenate([lo, hi], axis=-1)
    gate2 = mod_ref[0, 5:6, :]
    o_ref[0] = x1_ref[0] + gate2 * _rms(y, g_ref[...])


def _final(x1, yg, gate, mod, mod_row, g_post_ffn, tm, b0, n_batch, out_prev):
    bsz, seq, d = x1.shape
    tok3 = lambda b, i: (b, i, 0)
    yg = yg.reshape(TOP_K, bsz, seq, d // 2)
    slot_specs = [pl.BlockSpec((1, 1, tm, d // 2), functools.partial(lambda b, i, k: (k, b, i, 0), k=k))
                  for k in range(TOP_K)]
    in_specs = ([pl.BlockSpec((1, tm, d), tok3)] + slot_specs +
                [pl.BlockSpec((1, tm, LANES), tok3),
                 pl.BlockSpec((1, 6, d), lambda b, i: (mod_row(b), 0, 0)), _const_spec((1, d))])
    args = [x1, yg, yg, yg, yg, gate, mod, g_post_ffn.astype(F32).reshape(1, d)]
    aliases = {}
    if out_prev is not None:
        in_specs.append(pl.BlockSpec(memory_space=pl.ANY))
        args.append(out_prev)
        aliases = {len(args) - 1: 0}
    return pl.pallas_call(
        _final_kernel,
        grid=(bsz, seq // tm),
        in_specs=in_specs,
        out_specs=pl.BlockSpec((1, tm, d), lambda b, i: (b + b0, i, 0)),
        out_shape=jax.ShapeDtypeStruct((n_batch, seq, d), F32),
        input_output_aliases=aliases,
        compiler_params=_cparams(("parallel", "parallel")),
        name="final_residual",
    )(*args)


def _stream(x, mod, mod_row, p, ctx_kv, init_state, rope, tm, b0, bsz):
    _, seq, d = x.shape
    is_ctx = ctx_kv is None
    pj = _project(x, mod, p["g_pre_mix"], p["w_in"], p["b_gates"], mod_row, rope, is_ctx, tm, b0, bsz)
    if is_ctx:
        att = _attention(pj["q"], None, None, pj["kd"], pj["vd"], p["sink"], window=False)
    else:
        att = _attention(pj["q"], pj["kd"], pj["vd"], ctx_kv[0], ctx_kv[1], p["sink"], window=True)
    ml = _mlstm(pj["mq"], pj["mk"], pj["mv"], pj["gates"], init_state, emit_state=is_ctx, b0=b0)
    x1, h2, idx, gate, tile_counts = _merge(att, ml[0], ml[1], pj, x, mod, mod_row, p, tm, b0)
    moe_in = (h2.reshape(bsz * seq, d // 2), idx, tile_counts)
    return x1, gate, moe_in, pj, ml


def kernel(x_prompt, x_sample, c, cache_k, cache_v, state_C, state_n, state_m, c_ctx, w_ada, b_ada, g_pre_mix,
           w_in, b_gates, attn_sink, norm_mlstm, w_up_att, w_up_ml, w_out, g_post_mix, g_pre_ffn, w_router,
           b_router, w1, b1, w2, b2, g_post_ffn):
    depth = w_ada.shape[0]
    n_dec = c.shape[0]
    cond = jnp.concatenate([c_ctx[None, :], c], axis=0).astype(F32)
    cond = jnp.pad(cond, ((0, 16 - cond.shape[0]), (0, 0)))
    y_prompt, y_sample = x_prompt, x_sample
    ks_, vs_, cs_, ns_, ms_ = [], [], [], [], []
    for l in range(depth):
        p = dict(g_pre_mix=g_pre_mix[l], w_in=w_in[l], b_gates=b_gates[l], sink=attn_sink[l], norm_ml=norm_mlstm[l],
                 w_up_att=w_up_att[l], w_up_ml=w_up_ml[l], w_out=w_out[l], g_post_mix=g_post_mix[l],
                 g_pre_ffn=g_pre_ffn[l], w_router=w_router[l], b_router=b_router[l], w1=w1[l], b1=b1[l],
                 w2=w2[l], b2=b2[l], g_post_ffn=g_post_ffn[l])
        mod = _adaln(cond, w_ada[l], b_ada[l]).reshape(16, 6, D_MODEL)
        row_p = lambda b: 0
        bsz, seq = x_prompt.shape[:2]
        x1_p, gate_p, moe_p, pj, ml = _stream(y_prompt, mod, row_p, p, None, None, False, 256, 0, bsz)
        nat = pj["nat"]
        ks_.append(nat[..., :KV_W].reshape(bsz, seq, ATT_KV_HEADS, HEAD_DIM))
        vs_.append(nat[..., KV_W:].reshape(bsz, seq, ATT_KV_HEADS, HEAD_DIM))
        cs_.append(ml[2])
        ns_.append(ml[3])
        ms_.append(ml[4][..., 0])
        ctx_kv = (_dup_heads(cache_k[:, l], False), _dup_heads(cache_v[:, l], True))
        init = (state_C[:, l], state_n[:, l], state_m[:, l])
        half = n_dec // 2
        experts = (p["w1"], p["b1"].astype(F32), p["w2"], p["b2"].astype(F32))
        groups = []
        for b0 in (0, half):
            row_s = functools.partial(lambda b, off: b + off + 1, off=b0)
            sl = slice(b0, b0 + half)
            x1_s, gate_s, moe_s, _, _ = _stream(y_sample, mod, row_s, p, (ctx_kv[0][sl], ctx_kv[1][sl]),
                                                init, True, 512, b0, half)
            groups.append((x1_s, gate_s, moe_s, row_s))
        (x1_a, gate_a, moe_a, row_a), (x1_b, gate_b, moe_b, row_b) = groups
        yg_p, yg_a = _moe([moe_p[0], moe_a[0]], [moe_p[1], moe_a[1]], [moe_p[2], moe_a[2]], *experts)
        (yg_b,) = _moe([moe_b[0]], [moe_b[1]], [moe_b[2]], *experts)
        y_prompt = _final(x1_p, yg_p, gate_p, mod, row_p, p["g_post_ffn"], 256, 0, bsz, None)
        y_sample = _final(x1_a, yg_a, gate_a, mod, row_a, p["g_post_ffn"], 512, 0, n_dec, None)
        y_sample = _final(x1_b, yg_b, gate_b, mod, row_b, p["g_post_ffn"], 512, half, n_dec, y_sample)
    return (y_prompt, y_sample, jnp.stack(ks_, axis=1), jnp.stack(vs_, axis=1), jnp.stack(cs_, axis=1),
            jnp.stack(ns_, axis=1), jnp.stack(ms_, axis=1))
```

```python
import functools

import numpy as np
import jax
import jax.numpy as jnp
from jax import lax
from jax.experimental import pallas as pl
from jax.experimental.pallas import tpu as pltpu
from jax.experimental.pallas import tpu_sc as plsc

F32 = jnp.float32
BF16 = jnp.bfloat16

D_MODEL = 1024
GRID_W = 64
ATT_HEADS = 8
ATT_KV_HEADS = 2
ATT_GROUP = ATT_HEADS // ATT_KV_HEADS
HEAD_DIM = 64
BLOCK = 128
ROPE_THETA = 10000.0
AXIS_FREQS = HEAD_DIM // 4
ML_HEADS = 4
ML_DK = 128
ML_DV = 256
ML_CHUNK = 512
N_EXPERTS = 32
TOP_K = 4
D_FF = 1024
SWIGLU_ALPHA = 1.702
SWIGLU_LIMIT = 7.0
EPS = 1e-6
NEG = -1e30

Q_W = ATT_HEADS * HEAD_DIM
KV_W = ATT_KV_HEADS * HEAD_DIM
MLQK_W = ML_HEADS * ML_DK
MLV_W = ML_HEADS * ML_DV
GATE_W = 4 * ML_HEADS

LANES = 128
VMEM_LIMIT = 56 * 1024 * 1024
MOE_ROWS = 512
SC_CORES = 2
SC_SUBCORES = 16
SC_GATHER_CHUNK = 32


def _cparams(sem):
    return pltpu.CompilerParams(dimension_semantics=sem, vmem_limit_bytes=VMEM_LIMIT)


def _const_spec(shape):
    nd = len(shape)
    return pl.BlockSpec(shape, lambda *_: (0,) * nd)


def _adaln_kernel(cond_ref, w_ref, b_ref, o_ref):
    cnd = cond_ref[...]
    act = cnd * jax.nn.sigmoid(cnd)
    o_ref[...] = jnp.dot(act.astype(BF16), w_ref[...].astype(BF16),
                         preferred_element_type=F32) + b_ref[...]


def _adaln(cond, w_ada, b_ada):
    rows, d = cond.shape
    n = w_ada.shape[1]
    tn = 1536
    return pl.pallas_call(
        _adaln_kernel,
        grid=(n // tn,),
        in_specs=[_const_spec((rows, d)),
                  pl.BlockSpec((d, tn), lambda j: (0, j)),
                  pl.BlockSpec((1, tn), lambda j: (0, j))],
        out_specs=pl.BlockSpec((rows, tn), lambda j: (0, j)),
        out_shape=jax.ShapeDtypeStruct((rows, n), F32),
        compiler_params=_cparams(("parallel",)),
        name="adaln",
    )(cond, w_ada, b_ada.reshape(1, n))


def _pack_w_in(w_in, b_gates, rope, with_nat):
    o_q, o_k, o_v = 0, Q_W, Q_W + KV_W
    o_mq = Q_W + 2 * KV_W
    o_mk = o_mq + MLQK_W
    o_mv = o_mk + MLQK_W
    o_g = o_mv + MLV_W
    o_mo = o_g + GATE_W
    o_ga = o_mo + MLV_W
    o_gm = o_ga + D_MODEL
    half = HEAD_DIM // 2

    def head_cols(base, h, rot):
        lo = base + h * HEAD_DIM
        if rot:
            return [w_in[:, lo + half:lo + HEAD_DIM], w_in[:, lo:lo + half]]
        return [w_in[:, lo:lo + HEAD_DIM]]

    def q_cols(rot):
        cols = []
        for h in range(ATT_HEADS):
            cols += head_cols(o_q, h, rot)
        return cols

    def dup_cols(base, rot):
        cols = []
        for h in range(ATT_KV_HEADS):
            hc = head_cols(base, h, rot)
            cols += hc + hc
        return cols

    segs, cols, off = {}, [], 0

    def add(name, cl):
        nonlocal off
        width = sum(c.shape[1] for c in cl)
        segs[name] = (off, off + width)
        cols.extend(cl)
        off += width

    add("q", q_cols(False))
    if rope:
        add("qrot", q_cols(True))
    add("kd", dup_cols(o_k, False))
    if rope:
        add("kdrot", dup_cols(o_k, True))
    if with_nat:
        add("nat", [w_in[:, o_k:o_k + 2 * KV_W]])
    add("mq", [w_in[:, o_mq:o_mq + MLQK_W]])
    add("mv", [w_in[:, o_mv:o_mv + MLV_W]])
    add("mo", [w_in[:, o_mo:o_mo + MLV_W]])
    add("ga", [w_in[:, o_ga:o_ga + D_MODEL]])
    add("gm", [w_in[:, o_gm:o_gm + D_MODEL]])
    add("gates", [w_in[:, o_g:o_g + GATE_W], jnp.zeros((D_MODEL, LANES - GATE_W), w_in.dtype)])
    wp = jnp.concatenate(cols, axis=1).astype(BF16)
    wvd_t = jnp.concatenate(dup_cols(o_v, False) + [w_in[:, o_mk:o_mk + MLQK_W]], axis=1).T.astype(BF16)
    bg = jnp.pad(b_gates.astype(F32), (0, LANES - GATE_W)).reshape(1, LANES)
    return wp, wvd_t, bg, segs


def _rope_tables(n_tok):
    rows = n_tok // GRID_W
    row = np.repeat(np.arange(rows), GRID_W).astype(np.float32)
    col = np.tile(np.arange(GRID_W), rows).astype(np.float32)
    inv = (np.float32(ROPE_THETA) ** (-np.arange(AXIS_FREQS, dtype=np.float32) / AXIS_FREQS)).astype(np.float32)
    ang = np.concatenate([row[:, None] * inv, col[:, None] * inv], axis=-1).astype(np.float32)
    cos, sin = np.cos(ang), np.sin(ang)
    c64 = np.concatenate([cos, cos], axis=-1)
    s64 = np.concatenate([-sin, sin], axis=-1)
    return (jnp.asarray(np.tile(c64, (1, 2)), F32), jnp.asarray(np.tile(s64, (1, 2)), F32))


def _proj_kernel(*refs, segs, rope, with_nat):
    it = iter(refs)
    x_ref, mod_ref, g_ref, w_ref, wvt_ref, bg_ref = (next(it) for _ in range(6))
    cos_ref = sin_ref = None
    if rope:
        cos_ref, sin_ref = next(it), next(it)
    q_ref, kd_ref, vd_ref = next(it), next(it), next(it)
    nat_ref = next(it) if with_nat else None
    mq_ref, mk_ref, mv_ref, gates_ref, mo_ref, ga_ref, gm_ref = (next(it) for _ in range(7))

    x = x_ref[0]
    shift = mod_ref[0, 0:1, :]
    scale = mod_ref[0, 1:2, :]
    ms = jnp.mean(x * x, axis=-1, keepdims=True)
    xn = x * lax.rsqrt(ms + EPS) * g_ref[...]
    xb = (xn * (1.0 + scale) + shift).astype(BF16)

    def seg(name):
        lo, hi = segs[name]
        return jnp.dot(xb, w_ref[:, lo:hi], preferred_element_type=F32)

    uq = seg("q")
    ukd = seg("kd")
    if rope:
        uqr = seg("qrot")
        ukr = seg("kdrot")
        cs = cos_ref[...]
        sn = sin_ref[...]
        for h in range(ATT_HEADS // 2):
            sl = slice(h * LANES, (h + 1) * LANES)
            q_ref[0, h] = ((uq[:, sl] * cs + uqr[:, sl] * sn) * HEAD_DIM ** -0.5).astype(BF16)
        for h in range(ATT_KV_HEADS):
            sl = slice(h * LANES, (h + 1) * LANES)
            kd_ref[0, :, sl] = (ukd[:, sl] * cs + ukr[:, sl] * sn).astype(BF16)
    else:
        for h in range(ATT_HEADS // 2):
            sl = slice(h * LANES, (h + 1) * LANES)
            q_ref[0, h] = (uq[:, sl] * HEAD_DIM ** -0.5).astype(BF16)
        kd_ref[0] = ukd.astype(BF16)
    ut = lax.dot_general(wvt_ref[...], xb, (((1,), (1,)), ((), ())), preferred_element_type=F32)
    vd_ref[0] = ut[:2 * LANES].astype(BF16)
    mk_ref[0] = ut[2 * LANES:].astype(BF16)
    if with_nat:
        nat_ref[0] = seg("nat")
    mq_ref[0] = (seg("mq") * ML_DK ** -0.5).astype(BF16)
    mv_ref[0] = seg("mv").astype(BF16)
    gates_ref[0] = seg("gates") + bg_ref[...]
    mo_ref[0] = seg("mo")
    ga_ref[0] = seg("ga")
    gm_ref[0] = seg("gm")


def _project(x, mod, g_pre, w_in, b_gates, mod_row, rope, with_nat, tm, b0, bsz):
    _, seq, d = x.shape
    wp, wvd_t, bg, segs = _pack_w_in(w_in, b_gates, rope, with_nat)
    nw = wp.shape[1]
    tok3 = lambda b, i: (b, i, 0)
    in_specs = [pl.BlockSpec((1, tm, d), lambda b, i: (b + b0, i, 0)),
                pl.BlockSpec((1, 6, d), lambda b, i: (mod_row(b), 0, 0)),
                _const_spec((1, d)),
                pl.BlockSpec((d, nw), lambda b, i: (0, 0), pipeline_mode=pl.Buffered(1)),
                _const_spec((2 * LANES + MLQK_W, d)), _const_spec((1, LANES))]
    args = [x, mod, g_pre.reshape(1, d), wp, wvd_t, bg]
    if rope:
        cos_t, sin_t = _rope_tables(seq)
        in_specs += [pl.BlockSpec((tm, LANES), lambda b, i: (i, 0))] * 2
        args += [cos_t, sin_t]

    def tok_out(width, dtype):
        return (jax.ShapeDtypeStruct((bsz, seq, width), dtype), pl.BlockSpec((1, tm, width), tok3))

    outs = [(jax.ShapeDtypeStruct((bsz, ATT_HEADS // 2, seq, LANES), BF16),
             pl.BlockSpec((1, ATT_HEADS // 2, tm, LANES), lambda b, i: (b, 0, i, 0))),
            tok_out(2 * LANES, BF16),
            (jax.ShapeDtypeStruct((bsz, 2 * LANES, seq), BF16),
             pl.BlockSpec((1, 2 * LANES, tm), lambda b, i: (b, 0, i)))]
    if with_nat:
        outs.append(tok_out(2 * KV_W, F32))
    outs += [tok_out(MLQK_W, BF16),
             (jax.ShapeDtypeStruct((bsz, MLQK_W, seq), BF16),
              pl.BlockSpec((1, MLQK_W, tm), lambda b, i: (b, 0, i))),
             tok_out(MLV_W, BF16), tok_out(LANES, F32),
             tok_out(MLV_W, F32), tok_out(D_MODEL, F32), tok_out(D_MODEL, F32)]
    res = pl.pallas_call(
        functools.partial(_proj_kernel, segs=segs, rope=rope, with_nat=with_nat),
        grid=(bsz, seq // tm),
        in_specs=in_specs,
        out_specs=[o[1] for o in outs],
        out_shape=[o[0] for o in outs],
        compiler_params=_cparams(("parallel", "parallel")),
        name="proj_rope" if rope else "proj_ctx",
    )(*args)
    names = ["q", "kd", "vd"] + (["nat"] if with_nat else []) + ["mq", "mk", "mv", "gates", "mo", "ga", "gm"]
    return dict(zip(names, res))


def _attn_kernel(*refs, window, n_blocks, n_ctx):
    it = iter(refs)
    sink_ref, q_ref = next(it), next(it)
    if window:
        kp_ref, kc_ref, kn_ref, vp_ref, vc_ref, vn_ref = (next(it) for _ in range(6))
    ck_ref, cv_ref, o_ref = next(it), next(it), next(it)
    j = pl.program_id(1)
    n_win = 3 * BLOCK if window else 0
    cols = ATT_GROUP * BLOCK

    col_id = lax.broadcasted_iota(jnp.int32, (1, cols), 1)
    if window:
        kl = lax.broadcasted_iota(jnp.int32, (BLOCK, cols), 0)
        ql = lax.broadcasted_iota(jnp.int32, (BLOCK, cols), 1) & (BLOCK - 1)
        prev_ok = kl >= ql + jnp.where(j == 0, BLOCK, 0)
        next_ok = kl <= ql - jnp.where(j == n_blocks - 1, BLOCK, 0)
    lane_lo = lax.broadcasted_iota(jnp.int32, (BLOCK, LANES), 1) < HEAD_DIM
    lane_row = lax.broadcasted_iota(jnp.int32, (1, LANES), 1) < HEAD_DIM
    half_lo = jnp.where(lane_row, 1.0, 0.0).astype(BF16)
    half_hi = jnp.where(lane_row, 0.0, 1.0).astype(BF16)

    for kvh in range(ATT_KV_HEADS):
        sl = slice(kvh * LANES, (kvh + 1) * LANES)
        if window:
            keys = jnp.concatenate([kp_ref[0, :, sl], kc_ref[0, :, sl], kn_ref[0, :, sl], ck_ref[0, :, sl]], axis=0)
            vals_t = jnp.concatenate([vp_ref[0, sl, :], vc_ref[0, sl, :], vn_ref[0, sl, :], cv_ref[0, sl, :]], axis=1)
        else:
            keys = ck_ref[0, :, sl]
            vals_t = cv_ref[0, sl, :]
        qs = jnp.concatenate([q_ref[0, kvh * (ATT_GROUP // 2) + g // 2] * (half_lo if g % 2 == 0 else half_hi)
                              for g in range(ATT_GROUP)], axis=0)
        s = lax.dot_general(keys, qs, (((1,), (1,)), ((), ())), preferred_element_type=F32)
        if window:
            s = jnp.concatenate([jnp.where(prev_ok, s[:BLOCK], NEG), s[BLOCK:2 * BLOCK],
                                 jnp.where(next_ok, s[2 * BLOCK:n_win], NEG), s[n_win:]], axis=0)
        snk = jnp.full((1, cols), sink_ref[kvh * ATT_GROUP + ATT_GROUP - 1], F32)
        for g in range(ATT_GROUP - 2, -1, -1):
            snk = jnp.where(col_id < (g + 1) * BLOCK, sink_ref[kvh * ATT_GROUP + g], snk)
        m = jnp.maximum(jnp.max(s, axis=0, keepdims=True), snk)
        p = jnp.exp(s - m)
        den = jnp.sum(p, axis=0, keepdims=True) + jnp.exp(snk - m)
        o_t = jnp.dot(vals_t, p.astype(BF16), preferred_element_type=F32) / den
        for pair in range(ATT_GROUP // 2):
            even = o_t[:, (2 * pair) * BLOCK:(2 * pair + 1) * BLOCK].T
            odd = o_t[:, (2 * pair + 1) * BLOCK:(2 * pair + 2) * BLOCK].T
            col = (kvh * (ATT_GROUP // 2) + pair) * LANES
            o_ref[0, :, col:col + LANES] = jnp.where(lane_lo, even, odd).astype(BF16)


def _attention(q, kd, vd_t, ckd, cvd_t, sink, window):
    bsz, _, seq, _ = q.shape
    nb = seq // BLOCK
    n_ctx = ckd.shape[1]
    in_specs = [pl.BlockSpec(memory_space=pltpu.SMEM),
                pl.BlockSpec((1, ATT_HEADS // 2, BLOCK, LANES), lambda b, j: (b, 0, j, 0))]
    args = [sink.astype(F32), q]
    if window:
        in_specs += [pl.BlockSpec((1, BLOCK, 2 * LANES), lambda b, j: (b, jnp.maximum(j - 1, 0), 0)),
                     pl.BlockSpec((1, BLOCK, 2 * LANES), lambda b, j: (b, j, 0)),
                     pl.BlockSpec((1, BLOCK, 2 * LANES), lambda b, j: (b, jnp.minimum(j + 1, nb - 1), 0)),
                     pl.BlockSpec((1, 2 * LANES, BLOCK), lambda b, j: (b, 0, jnp.maximum(j - 1, 0))),
                     pl.BlockSpec((1, 2 * LANES, BLOCK), lambda b, j: (b, 0, j)),
                     pl.BlockSpec((1, 2 * LANES, BLOCK), lambda b, j: (b, 0, jnp.minimum(j + 1, nb - 1)))]
        args += [kd, kd, kd, vd_t, vd_t, vd_t]
    in_specs += [pl.BlockSpec((1, n_ctx, 2 * LANES), lambda b, j: (b, 0, 0)),
                 pl.BlockSpec((1, 2 * LANES, n_ctx), lambda b, j: (b, 0, 0))]
    args += [ckd, cvd_t]
    return pl.pallas_call(
        functools.partial(_attn_kernel, window=window, n_blocks=nb, n_ctx=n_ctx),
        grid=(bsz, nb),
        in_specs=in_specs,
        out_specs=pl.BlockSpec((1, BLOCK, Q_W), lambda b, j: (b, j, 0)),
        out_shape=jax.ShapeDtypeStruct((bsz, seq, Q_W), BF16),
        compiler_params=_cparams(("parallel", "parallel")),
        name="attn_window" if window else "attn_ctx",
    )(*args)


def _dup_heads(t, transpose):
    b, l = t.shape[:2]
    out = jnp.concatenate([t, t], axis=-1).reshape(b, l, ATT_KV_HEADS * LANES).astype(BF16)
    return jnp.swapaxes(out, 1, 2) if transpose else out


def _split3(x):
    hi = x.astype(BF16)
    r1 = x - hi.astype(F32)
    mid = r1.astype(BF16)
    lo = (r1 - mid.astype(F32)).astype(BF16)
    return hi, mid, lo


def _log_sigmoid(x):
    return jnp.minimum(x, 0.0) - jnp.log1p(jnp.exp(-jnp.abs(x)))


def _mlstm_kernel(*refs, chunk_len, n_chunks, has_init, emit_state):
    it = iter(refs)
    qkvg = [[next(it) for _ in range(4)] for _ in range(2)]
    if has_init:
        c0_ref, n0_ref, m0_ref = next(it), next(it), next(it)
    h_refs = [next(it), next(it)]
    if emit_state:
        co_ref, no_ref, mo_ref = next(it), next(it), next(it)
    st_s, m_s = next(it), next(it)
    t = pl.program_id(1)
    L = chunk_len

    @pl.when(t == 0)
    def _():
        for d in range(2):
            for h in range(ML_HEADS):
                if has_init:
                    st_s[d, h, :, :ML_DV] = c0_ref[0, d, h].T
                    st_s[d, h, :, ML_DV:] = jnp.broadcast_to(n0_ref[0, d, h:h + 1, :], (LANES, ML_DK)).T
                    m_s[d, h] = m0_ref[0, d, h:h + 1, :]
                else:
                    st_s[d, h] = jnp.zeros((ML_DK, ML_DV + LANES), F32)
                    m_s[d, h] = jnp.zeros((1, LANES), F32)

    ri = lax.broadcasted_iota(jnp.int32, (L, L), 0)
    ci = lax.broadcasted_iota(jnp.int32, (L, L), 1)
    ones_blk = jnp.ones((L, LANES), BF16)

    for d in range(2):
        q_ref, kt_ref, v_ref, g_ref = qkvg[d]
        tri = (ci <= ri) if d == 0 else (ci >= ri)
        tri_b = jnp.where(tri, 1.0, 0.0).astype(BF16)
        g = g_ref[0]
        lf = _log_sigmoid(g)
        g_t = g.T
        lf_t = lf.T
        bc = sum(jnp.dot(tri_b, part, preferred_element_type=F32) for part in _split3(lf))
        br = sum(lax.dot_general(part, tri_b, (((1,), (1,)), ((), ())), preferred_element_type=F32)
                 for part in _split3(lf_t))
        tot = jnp.sum(lf, axis=0, keepdims=True)

        for h in range(ML_HEADS):
            icol_i = 2 * ML_HEADS * d + h
            fcol = icol_i + ML_HEADS
            bcol = bc[:, fcol:fcol + 1]
            brow = br[fcol:fcol + 1, :]
            irow = g_t[icol_i:icol_i + 1, :]
            b_last = tot[:, fcol:fcol + 1]
            m_prev = m_s[d, h][:, 0:1]
            st_prev = st_s[d, h]
            qh = q_ref[0, :, h * ML_DK:(h + 1) * ML_DK]
            kt = kt_ref[0, h * ML_DK:(h + 1) * ML_DK, :]
            v_aug = jnp.concatenate([v_ref[0, :, h * ML_DV:(h + 1) * ML_DV], ones_blk], axis=1)

            rel = jnp.where(tri, irow - brow, NEG)
            inter = bcol + m_prev
            m_t = jnp.maximum(inter, bcol + jnp.max(rel, axis=-1, keepdims=True))
            w = jnp.exp(rel + (bcol - m_t))
            w_inter = jnp.exp(inter - m_t)
            s = jnp.dot(qh, kt, preferred_element_type=F32) * w
            lhs = jnp.concatenate([s.astype(BF16), (qh.astype(F32) * w_inter).astype(BF16)], axis=1)
            rhs = jnp.concatenate([v_aug, st_prev.astype(BF16)], axis=0)
            mix = jnp.dot(lhs, rhs, preferred_element_type=F32)
            inv = 1.0 / jnp.maximum(jnp.abs(mix[:, ML_DV:]), jnp.exp(-m_t))
            for c in range(ML_DV // LANES):
                lo = h * ML_DV + c * LANES
                h_refs[d][0, :, lo:lo + LANES] = mix[:, c * LANES:(c + 1) * LANES] * inv

            a = b_last - brow + irow
            m_new = jnp.maximum(b_last + m_prev, jnp.max(a, axis=-1, keepdims=True))
            wk = jnp.exp(a - m_new)
            decay = jnp.exp(b_last + m_prev - m_new)
            kw_t = (kt.astype(F32) * wk).astype(BF16)
            st_new = decay * st_prev + jnp.dot(kw_t, v_aug, preferred_element_type=F32)
            st_s[d, h] = st_new
            m_s[d, h] = jnp.broadcast_to(m_new, (1, LANES))
            if emit_state:
                @pl.when(t == n_chunks - 1)
                def _():
                    co_ref[0, d, h] = st_new[:, :ML_DV].T
                    no_ref[0, d, h:h + 1, :] = st_new[:, ML_DV:].T[0:1, :]
                    mo_ref[0, d, h:h + 1, :] = jnp.broadcast_to(m_new, (1, LANES))


def _mlstm(mq, mk_t, mv, gates, init, emit_state, b0=0):
    bsz, seq, _ = mq.shape
    chunk_len = min(ML_CHUNK, seq)
    nc = seq // chunk_len
    fwd = lambda b, t: (b, t, 0)
    bwd = lambda b, t: (b, nc - 1 - t, 0)
    fwd_t = lambda b, t: (b, 0, t)
    bwd_t = lambda b, t: (b, 0, nc - 1 - t)
    state5 = lambda b, t: (b, 0, 0, 0, 0)
    state4 = lambda b, t: (b, 0, 0, 0)
    init5 = lambda b, t: (b + b0, 0, 0, 0, 0)
    init4 = lambda b, t: (b + b0, 0, 0, 0)
    in_specs, args = [], []
    for chunk, chunk_t in ((fwd, fwd_t), (bwd, bwd_t)):
        in_specs += [pl.BlockSpec((1, chunk_len, MLQK_W), chunk), pl.BlockSpec((1, MLQK_W, chunk_len), chunk_t),
                     pl.BlockSpec((1, chunk_len, MLV_W), chunk), pl.BlockSpec((1, chunk_len, LANES), chunk)]
        args += [mq, mk_t, mv, gates]
    if init is not None:
        c0, n0, m0 = init
        in_specs += [pl.BlockSpec((1, 2, ML_HEADS, ML_DV, ML_DK), init5),
                     pl.BlockSpec((1, 2, ML_HEADS, ML_DK), init4),
                     pl.BlockSpec((1, 2, ML_HEADS, LANES), init4)]
        args += [c0.astype(F32), n0.astype(F32),
                 jnp.broadcast_to(m0.astype(F32)[..., None], m0.shape + (LANES,))]
    out_shape = [jax.ShapeDtypeStruct((bsz, seq, MLV_W), F32)] * 2
    out_specs = [pl.BlockSpec((1, chunk_len, MLV_W), fwd), pl.BlockSpec((1, chunk_len, MLV_W), bwd)]
    if emit_state:
        out_shape += [jax.ShapeDtypeStruct((bsz, 2, ML_HEADS, ML_DV, ML_DK), F32),
                      jax.ShapeDtypeStruct((bsz, 2, ML_HEADS, ML_DK), F32),
                      jax.ShapeDtypeStruct((bsz, 2, ML_HEADS, LANES), F32)]
        out_specs += [pl.BlockSpec((1, 2, ML_HEADS, ML_DV, ML_DK), state5),
                      pl.BlockSpec((1, 2, ML_HEADS, ML_DK), state4),
                      pl.BlockSpec((1, 2, ML_HEADS, LANES), state4)]
    return pl.pallas_call(
        functools.partial(_mlstm_kernel, chunk_len=chunk_len, n_chunks=nc, has_init=init is not None, emit_state=emit_state),
        grid=(bsz, nc),
        in_specs=in_specs,
        out_specs=out_specs,
        out_shape=out_shape,
        scratch_shapes=[pltpu.VMEM((2, ML_HEADS, ML_DK, ML_DV + LANES), F32),
                        pltpu.VMEM((2, ML_HEADS, 1, LANES), F32)],
        compiler_params=_cparams(("parallel", "arbitrary")),
        name="mlstm_state" if emit_state else "mlstm",
    )(*args)


def _sigmoid(x):
    return 0.5 * jnp.tanh(0.5 * x) + 0.5


def _rms(x, g):
    return x * lax.rsqrt(jnp.mean(x * x, axis=-1, keepdims=True) + EPS) * g


def _pack_bf16_pairs(x):
    n = x.shape[1] // 2
    lo = pltpu.bitcast(x[:, :n].astype(BF16).astype(F32), jnp.int32)
    hi = pltpu.bitcast(x[:, n:].astype(BF16).astype(F32), jnp.int32)
    return lax.shift_right_logical(lo, 16) | hi


def _unpack_bf16_pairs(p):
    lo = pltpu.bitcast(lax.shift_left(p, 16), F32)
    hi = pltpu.bitcast(p & jnp.int32(-65536), F32)
    return lo, hi


def _merge_kernel(att_ref, hf_ref, hb_ref, mo_ref, ga_ref, gm_ref, x_ref, mod_ref, nml_ref, gpm_ref, gpf_ref,
                  wua_ref, wum_ref, wo_ref, wr_ref, br_ref, x1_ref, h2_ref, idx_ref, gate_ref, cnt_ref):
    hsum = hf_ref[0] + hb_ref[0]
    parts = []
    for h in range(ML_HEADS):
        hh = hsum[:, h * ML_DV:(h + 1) * ML_DV]
        parts.append(hh * lax.rsqrt(jnp.mean(hh * hh, axis=-1, keepdims=True) + EPS))
    hn = jnp.concatenate(parts, axis=-1) * nml_ref[...]
    ml = (hn * _sigmoid(mo_ref[0])).astype(BF16)
    a = jnp.dot(att_ref[0], wua_ref[...], preferred_element_type=F32)
    m = jnp.dot(ml, wum_ref[...], preferred_element_type=F32)
    z = (_sigmoid(ga_ref[0]) * a + _sigmoid(gm_ref[0]) * m).astype(BF16)
    mix = jnp.dot(z, wo_ref[...], preferred_element_type=F32)
    gate1 = mod_ref[0, 2:3, :]
    shift2 = mod_ref[0, 3:4, :]
    scale2 = mod_ref[0, 4:5, :]
    x1 = x_ref[0] + gate1 * _rms(mix, gpm_ref[...])
    x1_ref[0] = x1
    h2 = _rms(x1, gpf_ref[...]) * (1.0 + scale2) + shift2
    h2_ref[0] = _pack_bf16_pairs(h2)

    h2_hi = h2.astype(BF16)
    h2_lo = (h2 - h2_hi.astype(F32)).astype(BF16)
    wr = wr_ref[...]
    wr_hi = wr.astype(BF16)
    wr_lo = (wr - wr_hi.astype(F32)).astype(BF16)
    logits = (jnp.dot(h2_hi, wr_hi, preferred_element_type=F32)
              + jnp.dot(h2_lo, wr_hi, preferred_element_type=F32)
              + jnp.dot(h2_hi, wr_lo, preferred_element_type=F32)) + br_ref[...]
    lane = lax.broadcasted_iota(jnp.int32, logits.shape, 1).astype(F32)
    work = jnp.where(lane < N_EXPERTS, logits, -jnp.inf)
    idx_out = jnp.zeros(logits.shape, F32)
    val_out = jnp.zeros(logits.shape, F32)
    picked = jnp.zeros(logits.shape, F32)
    top0 = None
    esum = None
    for k in range(TOP_K):
        mx = jnp.max(work, axis=-1, keepdims=True)
        sel = jnp.min(jnp.where(work == mx, lane, float(LANES)), axis=-1, keepdims=True)
        if k == 0:
            top0 = mx
        e = jnp.exp(mx - top0)
        esum = e if k == 0 else esum + e
        idx_out = jnp.where(lane == k, sel, idx_out)
        val_out = jnp.where(lane == k, e, val_out)
        picked = jnp.where(lane == sel, 1.0, picked)
        work = jnp.where(lane == sel, -jnp.inf, work)
    idx_ref[0] = idx_out.astype(jnp.int32)
    gate_ref[0] = val_out / esum
    cnt_ref[0] = jnp.sum(picked, axis=0, keepdims=True)


def _merge(att, hf, hb, pj, x, mod, mod_row, p, tm, b0):
    bsz = att.shape[0]
    _, seq, d = x.shape
    tok3 = lambda b, i: (b, i, 0)
    wr = jnp.pad(p["w_router"].astype(F32), ((0, 0), (0, LANES - N_EXPERTS)))
    br = jnp.pad(p["b_router"].astype(F32), (0, LANES - N_EXPERTS)).reshape(1, LANES)
    row = lambda v: v.astype(F32).reshape(1, -1)
    in_specs = [pl.BlockSpec((1, tm, Q_W), tok3),
                pl.BlockSpec((1, tm, MLV_W), tok3), pl.BlockSpec((1, tm, MLV_W), tok3),
                pl.BlockSpec((1, tm, MLV_W), tok3), pl.BlockSpec((1, tm, d), tok3), pl.BlockSpec((1, tm, d), tok3),
                pl.BlockSpec((1, tm, d), lambda b, i: (b + b0, i, 0)),
                pl.BlockSpec((1, 6, d), lambda b, i: (mod_row(b), 0, 0)),
                _const_spec((1, MLV_W)), _const_spec((1, d)), _const_spec((1, d)),
                _const_spec((Q_W, d)), _const_spec((MLV_W, d)), _const_spec((d, d)),
                _const_spec((d, LANES)), _const_spec((1, LANES))]
    out_shape = [jax.ShapeDtypeStruct((bsz, seq, d), F32), jax.ShapeDtypeStruct((bsz, seq, d // 2), jnp.int32),
                 jax.ShapeDtypeStruct((bsz, seq, LANES), jnp.int32), jax.ShapeDtypeStruct((bsz, seq, LANES), F32),
                 jax.ShapeDtypeStruct((bsz * (seq // tm), 1, LANES), F32)]
    out_specs = [pl.BlockSpec((1, tm, d), tok3), pl.BlockSpec((1, tm, d // 2), tok3),
                 pl.BlockSpec((1, tm, LANES), tok3), pl.BlockSpec((1, tm, LANES), tok3),
                 pl.BlockSpec((1, 1, LANES), lambda b, i: (b * (seq // tm) + i, 0, 0))]
    return pl.pallas_call(
        _merge_kernel,
        grid=(bsz, seq // tm),
        in_specs=in_specs,
        out_specs=out_specs,
        out_shape=out_shape,
        compiler_params=_cparams(("parallel", "parallel")),
        name="merge_router",
    )(att, hf, hb, pj["mo"], pj["ga"], pj["gm"], x, mod, row(p["norm_ml"]), row(p["g_post_mix"]),
      row(p["g_pre_ffn"]), p["w_up_att"].astype(BF16), p["w_up_ml"].astype(BF16), p["w_out"].astype(BF16), wr, br)


def _sc_gather_rows(table, idx):
    n = idx.shape[0]
    width = table.shape[1]
    chunk = SC_GATHER_CHUNK
    n_workers = SC_CORES * SC_SUBCORES
    rows_per_worker = n // n_workers
    n_pairs = rows_per_worker // (2 * chunk)
    assert n_pairs * 2 * chunk * n_workers == n
    mesh = plsc.VectorSubcoreMesh(core_axis_name="c", subcore_axis_name="s")

    def body(table_hbm, idx_hbm, out_hbm, idx_v, rows_v, gsem, osem):
        wid = lax.axis_index("s") * SC_CORES + lax.axis_index("c")
        base0 = wid * rows_per_worker

        def gather_copy(b):
            return pltpu.make_async_copy(table_hbm.at[idx_v.at[b]], rows_v.at[b], gsem.at[b])

        def write_copy(ci, b):
            base = pl.multiple_of(base0 + ci * chunk, 8)
            return pltpu.make_async_copy(rows_v.at[b], out_hbm.at[pl.ds(base, chunk)], osem.at[b])

        def issue(ci, b):
            base = pl.multiple_of(base0 + ci * chunk, 8)
            pltpu.sync_copy(idx_hbm.at[pl.ds(base, chunk)], idx_v.at[b])
            gather_copy(b).start()

        def finish(ci, b):
            gather_copy(b).wait()
            write_copy(ci, b).start()

        issue(0, 0)

        @pl.loop(0, n_pairs)
        def _(j):
            @pl.when(j > 0)
            def _():
                write_copy(2 * j - 1, 1).wait()
            issue(2 * j + 1, 1)
            finish(2 * j, 0)

            @pl.when(j < n_pairs - 1)
            def _():
                write_copy(2 * j, 0).wait()
                issue(2 * j + 2, 0)
            finish(2 * j + 1, 1)

        write_copy(2 * n_pairs - 2, 0).wait()
        write_copy(2 * n_pairs - 1, 1).wait()

    return pl.kernel(
        body, mesh=mesh,
        out_type=jax.ShapeDtypeStruct((n, width), table.dtype),
        scratch_types=[pltpu.VMEM((2, chunk), jnp.int32), pltpu.VMEM((2, chunk, width), table.dtype),
                       pltpu.SemaphoreType.DMA((2,)), pltpu.SemaphoreType.DMA((2,))],
    )(table, idx)


def _ffn_kernel(be_ref, nu_ref, xs_ref, w1_ref, b1_ref, w2_ref, b2_ref, ys_ref, w1b_s, w2b_s):
    i = pl.program_id(0)
    e = be_ref[i]
    e_prev = be_ref[jnp.maximum(i - 1, 0)]

    @pl.when((i == 0) | (e != e_prev))
    def _():
        w1b_s[...] = w1_ref[0].astype(BF16)
        w2b_s[...] = w2_ref[0].astype(BF16)

    @pl.when(i < nu_ref[0])
    def _():
        x_lo, x_hi = _unpack_bf16_pairs(xs_ref[...])
        xb = jnp.concatenate([x_lo.astype(BF16), x_hi.astype(BF16)], axis=1)
        hmid = jnp.dot(xb, w1b_s[...], preferred_element_type=F32) + b1_ref[0]
        glu = jnp.minimum(hmid[:, :D_FF], SWIGLU_LIMIT)
        lin = jnp.clip(hmid[:, D_FF:], -SWIGLU_LIMIT, SWIGLU_LIMIT)
        act = ((lin + 1.0) * glu * _sigmoid(SWIGLU_ALPHA * glu)).astype(BF16)
        y = jnp.dot(act, w2b_s[...], preferred_element_type=F32) + b2_ref[0]
        ys_ref[...] = _pack_bf16_pairs(y)

    @pl.when(i >= nu_ref[0])
    def _():
        ys_ref[...] = jnp.zeros(ys_ref.shape, jnp.int32)


def _sc_scatter_rows(tables, positions, n_rows):
    width = tables[0].shape[1]
    chunk = SC_GATHER_CHUNK
    n_workers = SC_CORES * SC_SUBCORES
    n_streams = len(tables)
    mesh = plsc.VectorSubcoreMesh(core_axis_name="c", subcore_axis_name="s")

    def body(*refs):
        table_refs = refs[0:2 * n_streams:2]
        pos_refs = refs[1:2 * n_streams:2]
        out_hbm, idx_v, rows_v, lsem, ssem = refs[2 * n_streams:]
        wid = lax.axis_index("s") * SC_CORES + lax.axis_index("c")

        for table_hbm, pos_hbm in zip(table_refs, pos_refs):
            tok_per_worker = table_hbm.shape[0] // n_workers
            n_pairs = tok_per_worker // (2 * chunk)
            assert n_pairs * 2 * chunk * n_workers == table_hbm.shape[0]
            base0 = wid * tok_per_worker

            def load_copy(ci, b):
                base = pl.multiple_of(base0 + ci * chunk, 8)
                return pltpu.make_async_copy(table_hbm.at[pl.ds(base, chunk)], rows_v.at[b], lsem.at[b])

            def scatter_copy(b, k):
                return pltpu.make_async_copy(rows_v.at[b], out_hbm.at[idx_v.at[b, k]], ssem.at[b])

            def load(ci, b):
                base = pl.multiple_of(base0 + ci * chunk, 8)
                load_copy(ci, b).start()
                for k in range(TOP_K):
                    pltpu.sync_copy(pos_hbm.at[k, pl.ds(base, chunk)], idx_v.at[b, k])

            def scatter(ci, b):
                load_copy(ci, b).wait()
                for k in range(TOP_K):
                    scatter_copy(b, k).start()

            def drain(b):
                for k in range(TOP_K):
                    scatter_copy(b, k).wait()

            load(0, 0)

            @pl.loop(0, n_pairs)
            def _(j):
                @pl.when(j > 0)
                def _():
                    drain(1)
                load(2 * j + 1, 1)
                scatter(2 * j, 0)

                @pl.when(j < n_pairs - 1)
                def _():
                    drain(0)
                    load(2 * j + 2, 0)
                scatter(2 * j + 1, 1)

            drain(0)
            drain(1)

    args = [a for pair in zip(tables, positions) for a in pair]
    return pl.kernel(
        body, mesh=mesh,
        out_type=jax.ShapeDtypeStruct((n_rows, width), tables[0].dtype),
        scratch_types=[pltpu.VMEM((2, TOP_K, chunk), jnp.int32), pltpu.VMEM((2, chunk, width), tables[0].dtype),
                       pltpu.SemaphoreType.DMA((2,)), pltpu.SemaphoreType.DMA((2,))],
    )(*args)


def _pos_kernel(idx_ref, base_ref, pos_ref):
    idx = idx_ref[...]
    rows = idx.shape[0]
    lane = lax.broadcasted_iota(jnp.int32, idx.shape, 1)
    hots = [jnp.where(lane == idx[:, k:k + 1], 1.0, 0.0) for k in range(TOP_K)]
    cnt = hots[0] + hots[1] + hots[2] + hots[3]
    ri = lax.broadcasted_iota(jnp.int32, (rows, rows), 0)
    ci = lax.broadcasted_iota(jnp.int32, (rows, rows), 1)
    earlier = jnp.where(ci < ri, 1.0, 0.0).astype(BF16)
    ahead = jnp.dot(earlier, cnt.astype(BF16), preferred_element_type=F32) + base_ref[0]
    posmat = jnp.zeros(idx.shape, F32)
    for k in range(TOP_K):
        posmat = jnp.where(lane == k, jnp.sum(hots[k] * ahead, axis=-1, keepdims=True), posmat)
    pos_ref[...] = posmat.T[:TOP_K, :].astype(jnp.int32)


def _route(idx_list, count_list):
    tiles = [c.shape[0] for c in count_list]
    tile_tok = [idx.size // LANES // n for idx, n in zip(idx_list, tiles)]
    n_tok = sum(n * sz for n, sz in zip(tiles, tile_tok))
    n_blocks = n_tok * TOP_K // MOE_ROWS + N_EXPERTS
    cnt = jnp.concatenate([c.reshape(-1, LANES) for c in count_list], axis=0).astype(jnp.int32)
    tile_off = jnp.cumsum(cnt, axis=0) - cnt
    total = jnp.sum(cnt, axis=0)
    padded = (total + MOE_ROWS - 1) // MOE_ROWS * MOE_ROWS
    pend = jnp.cumsum(padded)
    base = ((pend - padded)[None, :] + tile_off).astype(F32)
    starts = jnp.arange(n_blocks, dtype=jnp.int32) * MOE_ROWS
    block_e = jnp.minimum(jnp.sum(pend[None, :N_EXPERTS] <= starts[:, None], axis=1), N_EXPERTS - 1).astype(jnp.int32)
    n_used = (pend[N_EXPERTS - 1] // MOE_ROWS).astype(jnp.int32).reshape(1)
    positions, first = [], 0
    for idx, n_tiles, tile in zip(idx_list, tiles, tile_tok):
        stream_tok = n_tiles * tile
        positions.append(pl.pallas_call(
            _pos_kernel,
            grid=(n_tiles,),
            in_specs=[pl.BlockSpec((tile, LANES), lambda i: (i, 0)),
                      pl.BlockSpec((1, 1, LANES), lambda i: (i, 0, 0))],
            out_specs=pl.BlockSpec((TOP_K, tile), lambda i: (0, i)),
            out_shape=jax.ShapeDtypeStruct((TOP_K, stream_tok), jnp.int32),
            compiler_params=_cparams(("parallel",)),
            name="route_pos",
        )(idx.reshape(stream_tok, LANES), base[first:first + n_tiles].reshape(n_tiles, 1, LANES)))
        first += n_tiles
    return block_e, n_used, positions


def _moe(h2p_list, idx_list, count_list, w1, b1, w2, b2):
    half = h2p_list[0].shape[1]
    d = 2 * half
    block_e, n_used, positions = _route(idx_list, count_list)
    n_blocks = block_e.shape[0]
    xs = _sc_scatter_rows(h2p_list, positions, n_blocks * MOE_ROWS)
    grid_spec = pltpu.PrefetchScalarGridSpec(
        num_scalar_prefetch=2,
        grid=(n_blocks,),
        in_specs=[pl.BlockSpec((MOE_ROWS, half), lambda i, be, nu: (i, 0)),
                  pl.BlockSpec((1, d, 2 * D_FF), lambda i, be, nu: (be[i], 0, 0)),
                  pl.BlockSpec((1, 1, 2 * D_FF), lambda i, be, nu: (be[i], 0, 0)),
                  pl.BlockSpec((1, D_FF, d), lambda i, be, nu: (be[i], 0, 0)),
                  pl.BlockSpec((1, 1, d), lambda i, be, nu: (be[i], 0, 0))],
        out_specs=pl.BlockSpec((MOE_ROWS, half), lambda i, be, nu: (i, 0)),
        scratch_shapes=[pltpu.VMEM((d, 2 * D_FF), BF16), pltpu.VMEM((D_FF, d), BF16)],
    )
    ys = pl.pallas_call(
        _ffn_kernel,
        grid_spec=grid_spec,
        out_shape=jax.ShapeDtypeStruct((n_blocks * MOE_ROWS, half), jnp.int32),
        compiler_params=_cparams(("arbitrary",)),
        name="moe_ffn",
    )(block_e, n_used, xs, w1, b1.reshape(N_EXPERTS, 1, -1), w2, b2.reshape(N_EXPERTS, 1, -1))
    return [_sc_gather_rows(ys, pos.reshape(-1)) for pos in positions]


def _final_kernel(x1_ref, y0_ref, y1_ref, y2_ref, y3_ref, gate_ref, mod_ref, g_ref, *rest):
    o_ref = rest[-1]
    gates = gate_ref[0]
    lo = hi = None
    for k, y_ref in enumerate((y0_ref, y1_ref, y2_ref, y3_ref)):
        y_lo, y_hi = _unpack_bf16_pairs(y_ref[0, 0])
        gk = gates[:, k:k + 1]
        lo = gk * y_lo if k == 0 else lo + gk * y_lo
        hi = gk * y_hi if k == 0 else hi + gk * y_hi
    y = jnp.concatenate([lo, hi], axis=-1)
    gate2 = mod_ref[0, 5:6, :]
    o_ref[0] = x1_ref[0] + gate2 * _rms(y, g_ref[...])


def _final(x1, yg, gate, mod, mod_row, g_post_ffn, tm, b0, n_batch, out_prev):
    bsz, seq, d = x1.shape
    tok3 = lambda b, i: (b, i, 0)
    yg = yg.reshape(TOP_K, bsz, seq, d // 2)
    slot_specs = [pl.BlockSpec((1, 1, tm, d // 2), functools.partial(lambda b, i, k: (k, b, i, 0), k=k))
                  for k in range(TOP_K)]
    in_specs = ([pl.BlockSpec((1, tm, d), tok3)] + slot_specs +
                [pl.BlockSpec((1, tm, LANES), tok3),
                 pl.BlockSpec((1, 6, d), lambda b, i: (mod_row(b), 0, 0)), _const_spec((1, d))])
    args = [x1, yg, yg, yg, yg, gate, mod, g_post_ffn.astype(F32).reshape(1, d)]
    aliases = {}
    if out_prev is not None:
        in_specs.append(pl.BlockSpec(memory_space=pl.ANY))
        args.append(out_prev)
        aliases = {len(args) - 1: 0}
    return pl.pallas_call(
        _final_kernel,
        grid=(bsz, seq // tm),
        in_specs=in_specs,
        out_specs=pl.BlockSpec((1, tm, d), lambda b, i: (b + b0, i, 0)),
        out_shape=jax.ShapeDtypeStruct((n_batch, seq, d), F32),
        input_output_aliases=aliases,
        compiler_params=_cparams(("parallel", "parallel")),
        name="final_residual",
    )(*args)


def _stream(x, mod, mod_row, p, ctx_kv, init_state, rope, tm, b0, bsz):
    _, seq, d = x.shape
    is_ctx = ctx_kv is None
    pj = _project(x, mod, p["g_pre_mix"], p["w_in"], p["b_gates"], mod_row, rope, is_ctx, tm, b0, bsz)
    if is_ctx:
        att = _attention(pj["q"], None, None, pj["kd"], pj["vd"], p["sink"], window=False)
    else:
        att = _attention(pj["q"], pj["kd"], pj["vd"], ctx_kv[0], ctx_kv[1], p["sink"], window=True)
    ml = _mlstm(pj["mq"], pj["mk"], pj["mv"], pj["gates"], init_state, emit_state=is_ctx, b0=b0)
    x1, h2, idx, gate, tile_counts = _merge(att, ml[0], ml[1], pj, x, mod, mod_row, p, tm, b0)
    moe_in = (h2.reshape(bsz * seq, d // 2), idx, tile_counts)
    return x1, gate, moe_in, pj, ml


def kernel(x_prompt, x_sample, c, cache_k, cache_v, state_C, state_n, state_m, c_ctx, w_ada, b_ada, g_pre_mix,
           w_in, b_gates, attn_sink, norm_mlstm, w_up_att, w_up_ml, w_out, g_post_mix, g_pre_ffn, w_router,
           b_router, w1, b1, w2, b2, g_post_ffn):
    depth = w_ada.shape[0]
    n_dec = c.shape[0]
    cond = jnp.concatenate([c_ctx[None, :], c], axis=0).astype(F32)
    cond = jnp.pad(cond, ((0, 16 - cond.shape[0]), (0, 0)))
    y_prompt, y_sample = x_prompt, x_sample
    ks_, vs_, cs_, ns_, ms_ = [], [], [], [], []
    for l in range(depth):
        p = dict(g_pre_mix=g_pre_mix[l], w_in=w_in[l], b_gates=b_gates[l], sink=attn_sink[l], norm_ml=norm_mlstm[l],
                 w_up_att=w_up_att[l], w_up_ml=w_up_ml[l], w_out=w_out[l], g_post_mix=g_post_mix[l],
                 g_pre_ffn=g_pre_ffn[l], w_router=w_router[l], b_router=b_router[l], w1=w1[l], b1=b1[l],
                 w2=w2[l], b2=b2[l], g_post_ffn=g_post_ffn[l])
        mod = _adaln(cond, w_ada[l], b_ada[l]).reshape(16, 6, D_MODEL)
        row_p = lambda b: 0
        bsz, seq = x_prompt.shape[:2]
        x1_p, gate_p, moe_p, pj, ml = _stream(y_prompt, mod, row_p, p, None, None, False, 256, 0, bsz)
        nat = pj["nat"]
        ks_.append(nat[..., :KV_W].reshape(bsz, seq, ATT_KV_HEADS, HEAD_DIM))
        vs_.append(nat[..., KV_W:].reshape(bsz, seq, ATT_KV_HEADS, HEAD_DIM))
        cs_.append(ml[2])
        ns_.append(ml[3])
        ms_.append(ml[4][..., 0])
        ctx_kv = (_dup_heads(cache_k[:, l], False), _dup_heads(cache_v[:, l], True))
        init = (state_C[:, l], state_n[:, l], state_m[:, l])
        half = n_dec // 2
        experts = (p["w1"], p["b1"].astype(F32), p["w2"], p["b2"].astype(F32))
        groups = []
        for b0 in (0, half):
            row_s = functools.partial(lambda b, off: b + off + 1, off=b0)
            sl = slice(b0, b0 + half)
            x1_s, gate_s, moe_s, _, _ = _stream(y_sample, mod, row_s, p, (ctx_kv[0][sl], ctx_kv[1][sl]),
                                                init, True, 512, b0, half)
            groups.append((x1_s, gate_s, moe_s, row_s))
        (x1_a, gate_a, moe_a, row_a), (x1_b, gate_b, moe_b, row_b) = groups
        yg_p, yg_a = _moe([moe_p[0], moe_a[0]], [moe_p[1], moe_a[1]], [moe_p[2], moe_a[2]], *experts)
        (yg_b,) = _moe([moe_b[0]], [moe_b[1]], [moe_b[2]], *experts)
        y_prompt = _final(x1_p, yg_p, gate_p, mod, row_p, p["g_post_ffn"], 256, 0, bsz, None)
        y_sample = _final(x1_a, yg_a, gate_a, mod, row_a, p["g_post_ffn"], 512, 0, n_dec, None)
        y_sample = _final(x1_b, yg_b, gate_b, mod, row_b, p["g_post_ffn"], 512, half, n_dec, y_sample)
    return (y_prompt, y_sample, jnp.stack(ks_, axis=1), jnp.stack(vs_, axis=1), jnp.stack(cs_, axis=1),
            jnp.stack(ns_, axis=1), jnp.stack(ms_, axis=1))
```

```python
import functools

import numpy as np
import jax
import jax.numpy as jnp
from jax import lax
from jax.experimental import pallas as pl
from jax.experimental.pallas import tpu as pltpu
from jax.experimental.pallas import tpu_sc as plsc

F32 = jnp.float32
BF16 = jnp.bfloat16

D_MODEL = 1024
GRID_W = 64
ATT_HEADS = 8
ATT_KV_HEADS = 2
ATT_GROUP = ATT_HEADS // ATT_KV_HEADS
HEAD_DIM = 64
BLOCK = 128
ROPE_THETA = 10000.0
AXIS_FREQS = HEAD_DIM // 4
ML_HEADS = 4
ML_DK = 128
ML_DV = 256
ML_CHUNK = 512
N_EXPERTS = 32
TOP_K = 4
D_FF = 1024
SWIGLU_ALPHA = 1.702
SWIGLU_LIMIT = 7.0
EPS = 1e-6
NEG = -1e30

Q_W = ATT_HEADS * HEAD_DIM
KV_W = ATT_KV_HEADS * HEAD_DIM
MLQK_W = ML_HEADS * ML_DK
MLV_W = ML_HEADS * ML_DV
GATE_W = 4 * ML_HEADS

LANES = 128
VMEM_LIMIT = 56 * 1024 * 1024
MOE_ROWS = 512
SC_CORES = 2
SC_SUBCORES = 16
SC_GATHER_CHUNK = 32


def _cparams(sem):
    return pltpu.CompilerParams(dimension_semantics=sem, vmem_limit_bytes=VMEM_LIMIT)


def _const_spec(shape):
    nd = len(shape)
    return pl.BlockSpec(shape, lambda *_: (0,) * nd)


def _adaln_kernel(cond_ref, w_ref, b_ref, o_ref):
    cnd = cond_ref[...]
    act = cnd * jax.nn.sigmoid(cnd)
    o_ref[...] = jnp.dot(act.astype(BF16), w_ref[...].astype(BF16),
                         preferred_element_type=F32) + b_ref[...]


def _adaln(cond, w_ada, b_ada):
    rows, d = cond.shape
    n = w_ada.shape[1]
    tn = 1536
    return pl.pallas_call(
        _adaln_kernel,
        grid=(n // tn,),
        in_specs=[_const_spec((rows, d)),
                  pl.BlockSpec((d, tn), lambda j: (0, j)),
                  pl.BlockSpec((1, tn), lambda j: (0, j))],
        out_specs=pl.BlockSpec((rows, tn), lambda j: (0, j)),
        out_shape=jax.ShapeDtypeStruct((rows, n), F32),
        compiler_params=_cparams(("parallel",)),
        name="adaln",
    )(cond, w_ada, b_ada.reshape(1, n))


def _pack_w_in(w_in, b_gates, rope, with_nat):
    o_q, o_k, o_v = 0, Q_W, Q_W + KV_W
    o_mq = Q_W + 2 * KV_W
    o_mk = o_mq + MLQK_W
    o_mv = o_mk + MLQK_W
    o_g = o_mv + MLV_W
    o_mo = o_g + GATE_W
    o_ga = o_mo + MLV_W
    o_gm = o_ga + D_MODEL
    half = HEAD_DIM // 2

    def head_cols(base, h, rot):
        lo = base + h * HEAD_DIM
        if rot:
            return [w_in[:, lo + half:lo + HEAD_DIM], w_in[:, lo:lo + half]]
        return [w_in[:, lo:lo + HEAD_DIM]]

    def q_cols(rot):
        cols = []
        for h in range(ATT_HEADS):
            cols += head_cols(o_q, h, rot)
        return cols

    def dup_cols(base, rot):
        cols = []
        for h in range(ATT_KV_HEADS):
            hc = head_cols(base, h, rot)
            cols += hc + hc
        return cols

    segs, cols, off = {}, [], 0

    def add(name, cl):
        nonlocal off
        width = sum(c.shape[1] for c in cl)
        segs[name] = (off, off + width)
        cols.extend(cl)
        off += width

    add("q", q_cols(False))
    if rope:
        add("qrot", q_cols(True))
    add("kd", dup_cols(o_k, False))
    if rope:
        add("kdrot", dup_cols(o_k, True))
    if with_nat:
        add("nat", [w_in[:, o_k:o_k + 2 * KV_W]])
    add("mq", [w_in[:, o_mq:o_mq + MLQK_W]])
    add("mv", [w_in[:, o_mv:o_mv + MLV_W]])
    add("mo", [w_in[:, o_mo:o_mo + MLV_W]])
    add("ga", [w_in[:, o_ga:o_ga + D_MODEL]])
    add("gm", [w_in[:, o_gm:o_gm + D_MODEL]])
    add("gates", [w_in[:, o_g:o_g + GATE_W], jnp.zeros((D_MODEL, LANES - GATE_W), w_in.dtype)])
    wp = jnp.concatenate(cols, axis=1).astype(BF16)
    wvd_t = jnp.concatenate(dup_cols(o_v, False) + [w_in[:, o_mk:o_mk + MLQK_W]], axis=1).T.astype(BF16)
    bg = jnp.pad(b_gates.astype(F32), (0, LANES - GATE_W)).reshape(1, LANES)
    return wp, wvd_t, bg, segs


def _rope_tables(n_tok):
    rows = n_tok // GRID_W
    row = np.repeat(np.arange(rows), GRID_W).astype(np.float32)
    col = np.tile(np.arange(GRID_W), rows).astype(np.float32)
    inv = (np.float32(ROPE_THETA) ** (-np.arange(AXIS_FREQS, dtype=np.float32) / AXIS_FREQS)).astype(np.float32)
    ang = np.concatenate([row[:, None] * inv, col[:, None] * inv], axis=-1).astype(np.float32)
    cos, sin = np.cos(ang), np.sin(ang)
    c64 = np.concatenate([cos, cos], axis=-1)
    s64 = np.concatenate([-sin, sin], axis=-1)
    return (jnp.asarray(np.tile(c64, (1, 2)), F32), jnp.asarray(np.tile(s64, (1, 2)), F32))


def _proj_kernel(*refs, segs, rope, with_nat):
    it = iter(refs)
    x_ref, mod_ref, g_ref, w_ref, wvt_ref, bg_ref = (next(it) for _ in range(6))
    cos_ref = sin_ref = None
    if rope:
        cos_ref, sin_ref = next(it), next(it)
    q_ref, kd_ref, vd_ref = next(it), next(it), next(it)
    nat_ref = next(it) if with_nat else None
    mq_ref, mk_ref, mv_ref, gates_ref, mo_ref, ga_ref, gm_ref = (next(it) for _ in range(7))

    x = x_ref[0]
    shift = mod_ref[0, 0:1, :]
    scale = mod_ref[0, 1:2, :]
    ms = jnp.mean(x * x, axis=-1, keepdims=True)
    xn = x * lax.rsqrt(ms + EPS) * g_ref[...]
    xb = (xn * (1.0 + scale) + shift).astype(BF16)

    def seg(name):
        lo, hi = segs[name]
        return jnp.dot(xb, w_ref[:, lo:hi], preferred_element_type=F32)

    uq = seg("q")
    ukd = seg("kd")
    if rope:
        uqr = seg("qrot")
        ukr = seg("kdrot")
        cs = cos_ref[...]
        sn = sin_ref[...]
        for h in range(ATT_HEADS // 2):
            sl = slice(h * LANES, (h + 1) * LANES)
            q_ref[0, h] = ((uq[:, sl] * cs + uqr[:, sl] * sn) * HEAD_DIM ** -0.5).astype(BF16)
        for h in range(ATT_KV_HEADS):
            sl = slice(h * LANES, (h + 1) * LANES)
            kd_ref[0, :, sl] = (ukd[:, sl] * cs + ukr[:, sl] * sn).astype(BF16)
    else:
        for h in range(ATT_HEADS // 2):
            sl = slice(h * LANES, (h + 1) * LANES)
            q_ref[0, h] = (uq[:, sl] * HEAD_DIM ** -0.5).astype(BF16)
        kd_ref[0] = ukd.astype(BF16)
    ut = lax.dot_general(wvt_ref[...], xb, (((1,), (1,)), ((), ())), preferred_element_type=F32)
    vd_ref[0] = ut[:2 * LANES].astype(BF16)
    mk_ref[0] = ut[2 * LANES:].astype(BF16)
    if with_nat:
        nat_ref[0] = seg("nat")
    mq_ref[0] = (seg("mq") * ML_DK ** -0.5).astype(BF16)
    mv_ref[0] = seg("mv").astype(BF16)
    gates_ref[0] = seg("gates") + bg_ref[...]
    mo_ref[0] = seg("mo")
    ga_ref[0] = seg("ga")
    gm_ref[0] = seg("gm")


def _project(x, mod, g_pre, w_in, b_gates, mod_row, rope, with_nat, tm, b0, bsz):
    _, seq, d = x.shape
    wp, wvd_t, bg, segs = _pack_w_in(w_in, b_gates, rope, with_nat)
    nw = wp.shape[1]
    tok3 = lambda b, i: (b, i, 0)
    in_specs = [pl.BlockSpec((1, tm, d), lambda b, i: (b + b0, i, 0)),
                pl.BlockSpec((1, 6, d), lambda b, i: (mod_row(b), 0, 0)),
                _const_spec((1, d)),
                pl.BlockSpec((d, nw), lambda b, i: (0, 0), pipeline_mode=pl.Buffered(1)),
                _const_spec((2 * LANES + MLQK_W, d)), _const_spec((1, LANES))]
    args = [x, mod, g_pre.reshape(1, d), wp, wvd_t, bg]
    if rope:
        cos_t, sin_t = _rope_tables(seq)
        in_specs += [pl.BlockSpec((tm, LANES), lambda b, i: (i, 0))] * 2
        args += [cos_t, sin_t]

    def tok_out(width, dtype):
        return (jax.ShapeDtypeStruct((bsz, seq, width), dtype), pl.BlockSpec((1, tm, width), tok3))

    outs = [(jax.ShapeDtypeStruct((bsz, ATT_HEADS // 2, seq, LANES), BF16),
             pl.BlockSpec((1, ATT_HEADS // 2, tm, LANES), lambda b, i: (b, 0, i, 0))),
            tok_out(2 * LANES, BF16),
            (jax.ShapeDtypeStruct((bsz, 2 * LANES, seq), BF16),
             pl.BlockSpec((1, 2 * LANES, tm), lambda b, i: (b, 0, i)))]
    if with_nat:
        outs.append(tok_out(2 * KV_W, F32))
    outs += [tok_out(MLQK_W, BF16),
             (jax.ShapeDtypeStruct((bsz, MLQK_W, seq), BF16),
              pl.BlockSpec((1, MLQK_W, tm), lambda b, i: (b, 0, i))),
             tok_out(MLV_W, BF16), tok_out(LANES, F32),
             tok_out(MLV_W, F32), tok_out(D_MODEL, F32), tok_out(D_MODEL, F32)]
    res = pl.pallas_call(
        functools.partial(_proj_kernel, segs=segs, rope=rope, with_nat=with_nat),
        grid=(bsz, seq // tm),
        in_specs=in_specs,
        out_specs=[o[1] for o in outs],
        out_shape=[o[0] for o in outs],
        compiler_params=_cparams(("parallel", "parallel")),
        name="proj_rope" if rope else "proj_ctx",
    )(*args)
    names = ["q", "kd", "vd"] + (["nat"] if with_nat else []) + ["mq", "mk", "mv", "gates", "mo", "ga", "gm"]
    return dict(zip(names, res))


def _attn_kernel(*refs, window, n_blocks, n_ctx):
    it = iter(refs)
    sink_ref, q_ref = next(it), next(it)
    if window:
        kp_ref, kc_ref, kn_ref, vp_ref, vc_ref, vn_ref = (next(it) for _ in range(6))
    ck_ref, cv_ref, o_ref = next(it), next(it), next(it)
    j = pl.program_id(1)
    n_win = 3 * BLOCK if window else 0
    cols = ATT_GROUP * BLOCK

    col_id = lax.broadcasted_iota(jnp.int32, (1, cols), 1)
    if window:
        kl = lax.broadcasted_iota(jnp.int32, (BLOCK, cols), 0)
        ql = lax.broadcasted_iota(jnp.int32, (BLOCK, cols), 1) & (BLOCK - 1)
        prev_ok = kl >= ql + jnp.where(j == 0, BLOCK, 0)
        next_ok = kl <= ql - jnp.where(j == n_blocks - 1, BLOCK, 0)
    lane_lo = lax.broadcasted_iota(jnp.int32, (BLOCK, LANES), 1) < HEAD_DIM
    lane_row = lax.broadcasted_iota(jnp.int32, (1, LANES), 1) < HEAD_DIM
    half_lo = jnp.where(lane_row, 1.0, 0.0).astype(BF16)
    half_hi = jnp.where(lane_row, 0.0, 1.0).astype(BF16)

    for kvh in range(ATT_KV_HEADS):
        sl = slice(kvh * LANES, (kvh + 1) * LANES)
        if window:
            keys = jnp.concatenate([kp_ref[0, :, sl], kc_ref[0, :, sl], kn_ref[0, :, sl], ck_ref[0, :, sl]], axis=0)
            vals_t = jnp.concatenate([vp_ref[0, sl, :], vc_ref[0, sl, :], vn_ref[0, sl, :], cv_ref[0, sl, :]], axis=1)
        else:
            keys = ck_ref[0, :, sl]
            vals_t = cv_ref[0, sl, :]
        qs = jnp.concatenate([q_ref[0, kvh * (ATT_GROUP // 2) + g // 2] * (half_lo if g % 2 == 0 else half_hi)
                              for g in range(ATT_GROUP)], axis=0)
        s = lax.dot_general(keys, qs, (((1,), (1,)), ((), ())), preferred_element_type=F32)
        if window:
            s = jnp.concatenate([jnp.where(prev_ok, s[:BLOCK], NEG), s[BLOCK:2 * BLOCK],
                                 jnp.where(next_ok, s[2 * BLOCK:n_win], NEG), s[n_win:]], axis=0)
        snk = jnp.full((1, cols), sink_ref[kvh * ATT_GROUP + ATT_GROUP - 1], F32)
        for g in range(ATT_GROUP - 2, -1, -1):
            snk = jnp.where(col_id < (g + 1) * BLOCK, sink_ref[kvh * ATT_GROUP + g], snk)
        m = jnp.maximum(jnp.max(s, axis=0, keepdims=True), snk)
        p = jnp.exp(s - m)
        den = jnp.sum(p, axis=0, keepdims=True) + jnp.exp(snk - m)
        o_t = jnp.dot(vals_t, p.astype(BF16), preferred_element_type=F32) / den
        for pair in range(ATT_GROUP // 2):
            even = o_t[:, (2 * pair) * BLOCK:(2 * pair + 1) * BLOCK].T
            odd = o_t[:, (2 * pair + 1) * BLOCK:(2 * pair + 2) * BLOCK].T
            col = (kvh * (ATT_GROUP // 2) + pair) * LANES
            o_ref[0, :, col:col + LANES] = jnp.where(lane_lo, even, odd).astype(BF16)


def _attention(q, kd, vd_t, ckd, cvd_t, sink, window):
    bsz, _, seq, _ = q.shape
    nb = seq // BLOCK
    n_ctx = ckd.shape[1]
    in_specs = [pl.BlockSpec(memory_space=pltpu.SMEM),
                pl.BlockSpec((1, ATT_HEADS // 2, BLOCK, LANES), lambda b, j: (b, 0, j, 0))]
    args = [sink.astype(F32), q]
    if window:
        in_specs += [pl.BlockSpec((1, BLOCK, 2 * LANES), lambda b, j: (b, jnp.maximum(j - 1, 0), 0)),
                     pl.BlockSpec((1, BLOCK, 2 * LANES), lambda b, j: (b, j, 0)),
                     pl.BlockSpec((1, BLOCK, 2 * LANES), lambda b, j: (b, jnp.minimum(j + 1, nb - 1), 0)),
                     pl.BlockSpec((1, 2 * LANES, BLOCK), lambda b, j: (b, 0, jnp.maximum(j - 1, 0))),
                     pl.BlockSpec((1, 2 * LANES, BLOCK), lambda b, j: (b, 0, j)),
                     pl.BlockSpec((1, 2 * LANES, BLOCK), lambda b, j: (b, 0, jnp.minimum(j + 1, nb - 1)))]
        args += [kd, kd, kd, vd_t, vd_t, vd_t]
    in_specs += [pl.BlockSpec((1, n_ctx, 2 * LANES), lambda b, j: (b, 0, 0)),
                 pl.BlockSpec((1, 2 * LANES, n_ctx), lambda b, j: (b, 0, 0))]
    args += [ckd, cvd_t]
    return pl.pallas_call(
        functools.partial(_attn_kernel, window=window, n_blocks=nb, n_ctx=n_ctx),
        grid=(bsz, nb),
        in_specs=in_specs,
        out_specs=pl.BlockSpec((1, BLOCK, Q_W), lambda b, j: (b, j, 0)),
        out_shape=jax.ShapeDtypeStruct((bsz, seq, Q_W), BF16),
        compiler_params=_cparams(("parallel", "parallel")),
        name="attn_window" if window else "attn_ctx",
    )(*args)


def _dup_heads(t, transpose):
    b, l = t.shape[:2]
    out = jnp.concatenate([t, t], axis=-1).reshape(b, l, ATT_KV_HEADS * LANES).astype(BF16)
    return jnp.swapaxes(out, 1, 2) if transpose else out


def _split3(x):
    hi = x.astype(BF16)
    r1 = x - hi.astype(F32)
    mid = r1.astype(BF16)
    lo = (r1 - mid.astype(F32)).astype(BF16)
    return hi, mid, lo


def _log_sigmoid(x):
    return jnp.minimum(x, 0.0) - jnp.log1p(jnp.exp(-jnp.abs(x)))


def _mlstm_kernel(*refs, chunk_len, n_chunks, has_init, emit_state):
    it = iter(refs)
    qkvg = [[next(it) for _ in range(4)] for _ in range(2)]
    if has_init:
        c0_ref, n0_ref, m0_ref = next(it), next(it), next(it)
    h_refs = [next(it), next(it)]
    if emit_state:
        co_ref, no_ref, mo_ref = next(it), next(it), next(it)
    st_s, m_s = next(it), next(it)
    t = pl.program_id(1)
    L = chunk_len

    @pl.when(t == 0)
    def _():
        for d in range(2):
            for h in range(ML_HEADS):
                if has_init:
                    st_s[d, h, :, :ML_DV] = c0_ref[0, d, h].T
                    st_s[d, h, :, ML_DV:] = jnp.broadcast_to(n0_ref[0, d, h:h + 1, :], (LANES, ML_DK)).T
                    m_s[d, h] = m0_ref[0, d, h:h + 1, :]
                else:
                    st_s[d, h] = jnp.zeros((ML_DK, ML_DV + LANES), F32)
                    m_s[d, h] = jnp.zeros((1, LANES), F32)

    ri = lax.broadcasted_iota(jnp.int32, (L, L), 0)
    ci = lax.broadcasted_iota(jnp.int32, (L, L), 1)
    ones_blk = jnp.ones((L, LANES), BF16)

    for d in range(2):
        q_ref, kt_ref, v_ref, g_ref = qkvg[d]
        tri = (ci <= ri) if d == 0 else (ci >= ri)
        tri_b = jnp.where(tri, 1.0, 0.0).astype(BF16)
        g = g_ref[0]
        lf = _log_sigmoid(g)
        g_t = g.T
        lf_t = lf.T
        bc = sum(jnp.dot(tri_b, part, preferred_element_type=F32) for part in _split3(lf))
        br = sum(lax.dot_general(part, tri_b, (((1,), (1,)), ((), ())), preferred_element_type=F32)
                 for part in _split3(lf_t))
        tot = jnp.sum(lf, axis=0, keepdims=True)

        for h in range(ML_HEADS):
            icol_i = 2 * ML_HEADS * d + h
            fcol = icol_i + ML_HEADS
            bcol = bc[:, fcol:fcol + 1]
            brow = br[fcol:fcol + 1, :]
            irow = g_t[icol_i:icol_i + 1, :]
            b_last = tot[:, fcol:fcol + 1]
            m_prev = m_s[d, h][:, 0:1]
            st_prev = st_s[d, h]
            qh = q_ref[0, :, h * ML_DK:(h + 1) * ML_DK]
            kt = kt_ref[0, h * ML_DK:(h + 1) * ML_DK, :]
            v_aug = jnp.concatenate([v_ref[0, :, h * ML_DV:(h + 1) * ML_DV], ones_blk], axis=1)

            rel = jnp.where(tri, irow - brow, NEG)
            inter = bcol + m_prev
            m_t = jnp.maximum(inter, bcol + jnp.max(rel, axis=-1, keepdims=True))
            w = jnp.exp(rel + (bcol - m_t))
            w_inter = jnp.exp(inter - m_t)
            s = jnp.dot(qh, kt, preferred_element_type=F32) * w
            lhs = jnp.concatenate([s.astype(BF16), (qh.astype(F32) * w_inter).astype(BF16)], axis=1)
            rhs = jnp.concatenate([v_aug, st_prev.astype(BF16)], axis=0)
            mix = jnp.dot(lhs, rhs, preferred_element_type=F32)
            inv = 1.0 / jnp.maximum(jnp.abs(mix[:, ML_DV:]), jnp.exp(-m_t))
            for c in range(ML_DV // LANES):
                lo = h * ML_DV + c * LANES
                h_refs[d][0, :, lo:lo + LANES] = mix[:, c * LANES:(c + 1) * LANES] * inv

            a = b_last - brow + irow
            m_new = jnp.maximum(b_last + m_prev, jnp.max(a, axis=-1, keepdims=True))
            wk = jnp.exp(a - m_new)
            decay = jnp.exp(b_last + m_prev - m_new)
            kw_t = (kt.astype(F32) * wk).astype(BF16)
            st_new = decay * st_prev + jnp.dot(kw_t, v_aug, preferred_element_type=F32)
            st_s[d, h] = st_new
            m_s[d, h] = jnp.broadcast_to(m_new, (1, LANES))
            if emit_state:
                @pl.when(t == n_chunks - 1)
                def _():
                    co_ref[0, d, h] = st_new[:, :ML_DV].T
                    no_ref[0, d, h:h + 1, :] = st_new[:, ML_DV:].T[0:1, :]
                    mo_ref[0, d, h:h + 1, :] = jnp.broadcast_to(m_new, (1, LANES))


def _mlstm(mq, mk_t, mv, gates, init, emit_state, b0=0):
    bsz, seq, _ = mq.shape
    chunk_len = min(ML_CHUNK, seq)
    nc = seq // chunk_len
    fwd = lambda b, t: (b, t, 0)
    bwd = lambda b, t: (b, nc - 1 - t, 0)
    fwd_t = lambda b, t: (b, 0, t)
    bwd_t = lambda b, t: (b, 0, nc - 1 - t)
    state5 = lambda b, t: (b, 0, 0, 0, 0)
    state4 = lambda b, t: (b, 0, 0, 0)
    init5 = lambda b, t: (b + b0, 0, 0, 0, 0)
    init4 = lambda b, t: (b + b0, 0, 0, 0)
    in_specs, args = [], []
    for chunk, chunk_t in ((fwd, fwd_t), (bwd, bwd_t)):
        in_specs += [pl.BlockSpec((1, chunk_len, MLQK_W), chunk), pl.BlockSpec((1, MLQK_W, chunk_len), chunk_t),
                     pl.BlockSpec((1, chunk_len, MLV_W), chunk), pl.BlockSpec((1, chunk_len, LANES), chunk)]
        args += [mq, mk_t, mv, gates]
    if init is not None:
        c0, n0, m0 = init
        in_specs += [pl.BlockSpec((1, 2, ML_HEADS, ML_DV, ML_DK), init5),
                     pl.BlockSpec((1, 2, ML_HEADS, ML_DK), init4),
                     pl.BlockSpec((1, 2, ML_HEADS, LANES), init4)]
        args += [c0.astype(F32), n0.astype(F32),
                 jnp.broadcast_to(m0.astype(F32)[..., None], m0.shape + (LANES,))]
    out_shape = [jax.ShapeDtypeStruct((bsz, seq, MLV_W), F32)] * 2
    out_specs = [pl.BlockSpec((1, chunk_len, MLV_W), fwd), pl.BlockSpec((1, chunk_len, MLV_W), bwd)]
    if emit_state:
        out_shape += [jax.ShapeDtypeStruct((bsz, 2, ML_HEADS, ML_DV, ML_DK), F32),
                      jax.ShapeDtypeStruct((bsz, 2, ML_HEADS, ML_DK), F32),
                      jax.ShapeDtypeStruct((bsz, 2, ML_HEADS, LANES), F32)]
        out_specs += [pl.BlockSpec((1, 2, ML_HEADS, ML_DV, ML_DK), state5),
                      pl.BlockSpec((1, 2, ML_HEADS, ML_DK), state4),
                      pl.BlockSpec((1, 2, ML_HEADS, LANES), state4)]
    return pl.pallas_call(
        functools.partial(_mlstm_kernel, chunk_len=chunk_len, n_chunks=nc, has_init=init is not None, emit_state=emit_state),
        grid=(bsz, nc),
        in_specs=in_specs,
        out_specs=out_specs,
        out_shape=out_shape,
        scratch_shapes=[pltpu.VMEM((2, ML_HEADS, ML_DK, ML_DV + LANES), F32),
                        pltpu.VMEM((2, ML_HEADS, 1, LANES), F32)],
        compiler_params=_cparams(("parallel", "arbitrary")),
        name="mlstm_state" if emit_state else "mlstm",
    )(*args)


def _sigmoid(x):
    return 0.5 * jnp.tanh(0.5 * x) + 0.5


def _rms(x, g):
    return x * lax.rsqrt(jnp.mean(x * x, axis=-1, keepdims=True) + EPS) * g


def _pack_bf16_pairs(x):
    n = x.shape[1] // 2
    lo = pltpu.bitcast(x[:, :n].astype(BF16).astype(F32), jnp.int32)
    hi = pltpu.bitcast(x[:, n:].astype(BF16).astype(F32), jnp.int32)
    return lax.shift_right_logical(lo, 16) | hi


def _unpack_bf16_pairs(p):
    lo = pltpu.bitcast(lax.shift_left(p, 16), F32)
    hi = pltpu.bitcast(p & jnp.int32(-65536), F32)
    return lo, hi


def _merge_kernel(att_ref, hf_ref, hb_ref, mo_ref, ga_ref, gm_ref, x_ref, mod_ref, nml_ref, gpm_ref, gpf_ref,
                  wua_ref, wum_ref, wo_ref, wr_ref, br_ref, x1_ref, h2_ref, idx_ref, gate_ref, cnt_ref):
    hsum = hf_ref[0] + hb_ref[0]
    parts = []
    for h in range(ML_HEADS):
        hh = hsum[:, h * ML_DV:(h + 1) * ML_DV]
        parts.append(hh * lax.rsqrt(jnp.mean(hh * hh, axis=-1, keepdims=True) + EPS))
    hn = jnp.concatenate(parts, axis=-1) * nml_ref[...]
    ml = (hn * _sigmoid(mo_ref[0])).astype(BF16)
    a = jnp.dot(att_ref[0], wua_ref[...], preferred_element_type=F32)
    m = jnp.dot(ml, wum_ref[...], preferred_element_type=F32)
    z = (_sigmoid(ga_ref[0]) * a + _sigmoid(gm_ref[0]) * m).astype(BF16)
    mix = jnp.dot(z, wo_ref[...], preferred_element_type=F32)
    gate1 = mod_ref[0, 2:3, :]
    shift2 = mod_ref[0, 3:4, :]
    scale2 = mod_ref[0, 4:5, :]
    x1 = x_ref[0] + gate1 * _rms(mix, gpm_ref[...])
    x1_ref[0] = x1
    h2 = _rms(x1, gpf_ref[...]) * (1.0 + scale2) + shift2
    h2_ref[0] = _pack_bf16_pairs(h2)

    h2_hi = h2.astype(BF16)
    h2_lo = (h2 - h2_hi.astype(F32)).astype(BF16)
    wr = wr_ref[...]
    wr_hi = wr.astype(BF16)
    wr_lo = (wr - wr_hi.astype(F32)).astype(BF16)
    logits = (jnp.dot(h2_hi, wr_hi, preferred_element_type=F32)
              + jnp.dot(h2_lo, wr_hi, preferred_element_type=F32)
              + jnp.dot(h2_hi, wr_lo, preferred_element_type=F32)) + br_ref[...]
    lane = lax.broadcasted_iota(jnp.int32, logits.shape, 1).astype(F32)
    work = jnp.where(lane < N_EXPERTS, logits, -jnp.inf)
    idx_out = jnp.zeros(logits.shape, F32)
    val_out = jnp.zeros(logits.shape, F32)
    picked = jnp.zeros(logits.shape, F32)
    top0 = None
    esum = None
    for k in range(TOP_K):
        mx = jnp.max(work, axis=-1, keepdims=True)
        sel = jnp.min(jnp.where(work == mx, lane, float(LANES)), axis=-1, keepdims=True)
        if k == 0:
            top0 = mx
        e = jnp.exp(mx - top0)
        esum = e if k == 0 else esum + e
        idx_out = jnp.where(lane == k, sel, idx_out)
        val_out = jnp.where(lane == k, e, val_out)
        picked = jnp.where(lane == sel, 1.0, picked)
        work = jnp.where(lane == sel, -jnp.inf, work)
    idx_ref[0] = idx_out.astype(jnp.int32)
    gate_ref[0] = val_out / esum
    cnt_ref[0] = jnp.sum(picked, axis=0, keepdims=True)


def _merge(att, hf, hb, pj, x, mod, mod_row, p, tm, b0):
    bsz = att.shape[0]
    _, seq, d = x.shape
    tok3 = lambda b, i: (b, i, 0)
    wr = jnp.pad(p["w_router"].astype(F32), ((0, 0), (0, LANES - N_EXPERTS)))
    br = jnp.pad(p["b_router"].astype(F32), (0, LANES - N_EXPERTS)).reshape(1, LANES)
    row = lambda v: v.astype(F32).reshape(1, -1)
    in_specs = [pl.BlockSpec((1, tm, Q_W), tok3),
                pl.BlockSpec((1, tm, MLV_W), tok3), pl.BlockSpec((1, tm, MLV_W), tok3),
                pl.BlockSpec((1, tm, MLV_W), tok3), pl.BlockSpec((1, tm, d), tok3), pl.BlockSpec((1, tm, d), tok3),
                pl.BlockSpec((1, tm, d), lambda b, i: (b + b0, i, 0)),
                pl.BlockSpec((1, 6, d), lambda b, i: (mod_row(b), 0, 0)),
                _const_spec((1, MLV_W)), _const_spec((1, d)), _const_spec((1, d)),
                _const_spec((Q_W, d)), _const_spec((MLV_W, d)), _const_spec((d, d)),
                _const_spec((d, LANES)), _const_spec((1, LANES))]
    out_shape = [jax.ShapeDtypeStruct((bsz, seq, d), F32), jax.ShapeDtypeStruct((bsz, seq, d // 2), jnp.int32),
                 jax.ShapeDtypeStruct((bsz, seq, LANES), jnp.int32), jax.ShapeDtypeStruct((bsz, seq, LANES), F32),
                 jax.ShapeDtypeStruct((bsz * (seq // tm), 1, LANES), F32)]
    out_specs = [pl.BlockSpec((1, tm, d), tok3), pl.BlockSpec((1, tm, d // 2), tok3),
                 pl.BlockSpec((1, tm, LANES), tok3), pl.BlockSpec((1, tm, LANES), tok3),
                 pl.BlockSpec((1, 1, LANES), lambda b, i: (b * (seq // tm) + i, 0, 0))]
    return pl.pallas_call(
        _merge_kernel,
        grid=(bsz, seq // tm),
        in_specs=in_specs,
        out_specs=out_specs,
        out_shape=out_shape,
        compiler_params=_cparams(("parallel", "parallel")),
        name="merge_router",
    )(att, hf, hb, pj["mo"], pj["ga"], pj["gm"], x, mod, row(p["norm_ml"]), row(p["g_post_mix"]),
      row(p["g_pre_ffn"]), p["w_up_att"].astype(BF16), p["w_up_ml"].astype(BF16), p["w_out"].astype(BF16), wr, br)


def _sc_gather_rows(table, idx):
    n = idx.shape[0]
    width = table.shape[1]
    chunk = SC_GATHER_CHUNK
    n_workers = SC_CORES * SC_SUBCORES
    rows_per_worker = n // n_workers
    n_pairs = rows_per_worker // (2 * chunk)
    assert n_pairs * 2 * chunk * n_workers == n
    mesh = plsc.VectorSubcoreMesh(core_axis_name="c", subcore_axis_name="s")

    def body(table_hbm, idx_hbm, out_hbm, idx_v, rows_v, gsem, osem):
        wid = lax.axis_index("s") * SC_CORES + lax.axis_index("c")
        base0 = wid * rows_per_worker

        def gather_copy(b):
            return pltpu.make_async_copy(table_hbm.at[idx_v.at[b]], rows_v.at[b], gsem.at[b])

        def write_copy(ci, b):
            base = pl.multiple_of(base0 + ci * chunk, 8)
            return pltpu.make_async_copy(rows_v.at[b], out_hbm.at[pl.ds(base, chunk)], osem.at[b])

        def issue(ci, b):
            base = pl.multiple_of(base0 + ci * chunk, 8)
            pltpu.sync_copy(idx_hbm.at[pl.ds(base, chunk)], idx_v.at[b])
            gather_copy(b).start()

        def finish(ci, b):
            gather_copy(b).wait()
            write_copy(ci, b).start()

        issue(0, 0)

        @pl.loop(0, n_pairs)
        def _(j):
            @pl.when(j > 0)
            def _():
                write_copy(2 * j - 1, 1).wait()
            issue(2 * j + 1, 1)
            finish(2 * j, 0)

            @pl.when(j < n_pairs - 1)
            def _():
                write_copy(2 * j, 0).wait()
                issue(2 * j + 2, 0)
            finish(2 * j + 1, 1)

        write_copy(2 * n_pairs - 2, 0).wait()
        write_copy(2 * n_pairs - 1, 1).wait()

    return pl.kernel(
        body, mesh=mesh,
        out_type=jax.ShapeDtypeStruct((n, width), table.dtype),
        scratch_types=[pltpu.VMEM((2, chunk), jnp.int32), pltpu.VMEM((2, chunk, width), table.dtype),
                       pltpu.SemaphoreType.DMA((2,)), pltpu.SemaphoreType.DMA((2,))],
    )(table, idx)


def _ffn_kernel(be_ref, nv_ref, xs_ref, w1_ref, b1_ref, w2_ref, b2_ref, ys_ref, w1b_s, w2b_s):
    i = pl.program_id(0)
    e = be_ref[i]
    e_prev = be_ref[jnp.maximum(i - 1, 0)]
    n_valid = nv_ref[i]
    half_rows = MOE_ROWS // 2

    @pl.when((i == 0) | (e != e_prev))
    def _():
        w1b_s[...] = w1_ref[0].astype(BF16)
        w2b_s[...] = w2_ref[0].astype(BF16)

    def ffn(x_packed):
        x_lo, x_hi = _unpack_bf16_pairs(x_packed)
        xb = jnp.concatenate([x_lo.astype(BF16), x_hi.astype(BF16)], axis=1)
        hmid = jnp.dot(xb, w1b_s[...], preferred_element_type=F32) + b1_ref[0]
        glu = jnp.minimum(hmid[:, :D_FF], SWIGLU_LIMIT)
        lin = jnp.clip(hmid[:, D_FF:], -SWIGLU_LIMIT, SWIGLU_LIMIT)
        act = ((lin + 1.0) * glu * _sigmoid(SWIGLU_ALPHA * glu)).astype(BF16)
        return _pack_bf16_pairs(jnp.dot(act, w2b_s[...], preferred_element_type=F32) + b2_ref[0])

    @pl.when(n_valid > half_rows)
    def _():
        ys_ref[...] = ffn(xs_ref[...])

    @pl.when((n_valid > 0) & (n_valid <= half_rows))
    def _():
        ys_ref[:half_rows] = ffn(xs_ref[:half_rows])
        ys_ref[half_rows:] = jnp.zeros((MOE_ROWS - half_rows, ys_ref.shape[1]), jnp.int32)

    @pl.when(n_valid == 0)
    def _():
        ys_ref[...] = jnp.zeros(ys_ref.shape, jnp.int32)


def _sc_scatter_rows(tables, positions, n_rows):
    width = tables[0].shape[1]
    chunk = SC_GATHER_CHUNK
    n_workers = SC_CORES * SC_SUBCORES
    n_streams = len(tables)
    mesh = plsc.VectorSubcoreMesh(core_axis_name="c", subcore_axis_name="s")

    def body(*refs):
        table_refs = refs[0:2 * n_streams:2]
        pos_refs = refs[1:2 * n_streams:2]
        out_hbm, idx_v, rows_v, lsem, ssem = refs[2 * n_streams:]
        wid = lax.axis_index("s") * SC_CORES + lax.axis_index("c")

        for table_hbm, pos_hbm in zip(table_refs, pos_refs):
            tok_per_worker = table_hbm.shape[0] // n_workers
            n_pairs = tok_per_worker // (2 * chunk)
            assert n_pairs * 2 * chunk * n_workers == table_hbm.shape[0]
            base0 = wid * tok_per_worker

            def load_copy(ci, b):
                base = pl.multiple_of(base0 + ci * chunk, 8)
                return pltpu.make_async_copy(table_hbm.at[pl.ds(base, chunk)], rows_v.at[b], lsem.at[b])

            def scatter_copy(b, k):
                return pltpu.make_async_copy(rows_v.at[b], out_hbm.at[idx_v.at[b, k]], ssem.at[b])

            def load(ci, b):
                base = pl.multiple_of(base0 + ci * chunk, 8)
                load_copy(ci, b).start()
                for k in range(TOP_K):
                    pltpu.sync_copy(pos_hbm.at[k, pl.ds(base, chunk)], idx_v.at[b, k])

            def scatter(ci, b):
                load_copy(ci, b).wait()
                for k in range(TOP_K):
                    scatter_copy(b, k).start()

            def drain(b):
                for k in range(TOP_K):
                    scatter_copy(b, k).wait()

            load(0, 0)

            @pl.loop(0, n_pairs)
            def _(j):
                @pl.when(j > 0)
                def _():
                    drain(1)
                load(2 * j + 1, 1)
                scatter(2 * j, 0)

                @pl.when(j < n_pairs - 1)
                def _():
                    drain(0)
                    load(2 * j + 2, 0)
                scatter(2 * j + 1, 1)

            drain(0)
            drain(1)

    args = [a for pair in zip(tables, positions) for a in pair]
    return pl.kernel(
        body, mesh=mesh,
        out_type=jax.ShapeDtypeStruct((n_rows, width), tables[0].dtype),
        scratch_types=[pltpu.VMEM((2, TOP_K, chunk), jnp.int32), pltpu.VMEM((2, chunk, width), tables[0].dtype),
                       pltpu.SemaphoreType.DMA((2,)), pltpu.SemaphoreType.DMA((2,))],
    )(*args)


def _pos_kernel(idx_ref, base_ref, pos_ref):
    idx = idx_ref[...]
    rows = idx.shape[0]
    lane = lax.broadcasted_iota(jnp.int32, idx.shape, 1)
    hots = [jnp.where(lane == idx[:, k:k + 1], 1.0, 0.0) for k in range(TOP_K)]
    cnt = hots[0] + hots[1] + hots[2] + hots[3]
    ri = lax.broadcasted_iota(jnp.int32, (rows, rows), 0)
    ci = lax.broadcasted_iota(jnp.int32, (rows, rows), 1)
    earlier = jnp.where(ci < ri, 1.0, 0.0).astype(BF16)
    ahead = jnp.dot(earlier, cnt.astype(BF16), preferred_element_type=F32) + base_ref[0]
    posmat = jnp.zeros(idx.shape, F32)
    for k in range(TOP_K):
        posmat = jnp.where(lane == k, jnp.sum(hots[k] * ahead, axis=-1, keepdims=True), posmat)
    pos_ref[...] = posmat.T[:TOP_K, :].astype(jnp.int32)


def _route(idx_list, count_list):
    tiles = [c.shape[0] for c in count_list]
    tile_tok = [idx.size // LANES // n for idx, n in zip(idx_list, tiles)]
    n_tok = sum(n * sz for n, sz in zip(tiles, tile_tok))
    n_blocks = n_tok * TOP_K // MOE_ROWS + N_EXPERTS
    cnt = jnp.concatenate([c.reshape(-1, LANES) for c in count_list], axis=0).astype(jnp.int32)
    n_all = cnt.shape[0]
    before = np.tril(np.ones((n_all, n_all), bool), -1)
    tile_off = jnp.sum(jnp.where(before[:, :, None], cnt[None, :, :], 0), axis=1)
    total = jnp.sum(cnt, axis=0)
    padded = (total + MOE_ROWS - 1) // MOE_ROWS * MOE_ROWS
    upto = np.tril(np.ones((LANES, LANES), bool))
    pend = jnp.sum(jnp.where(upto, padded[None, :], 0), axis=1)
    base = ((pend - padded)[None, :] + tile_off).astype(F32)
    starts = jnp.arange(n_blocks, dtype=jnp.int32) * MOE_ROWS
    block_e = jnp.minimum(jnp.sum(pend[None, :N_EXPERTS] <= starts[:, None], axis=1), N_EXPERTS - 1).astype(jnp.int32)
    rows_valid = jnp.clip(jnp.take(total, block_e) - (starts - jnp.take(pend - padded, block_e)), 0,
                          MOE_ROWS).astype(jnp.int32)
    positions, first = [], 0
    for idx, n_tiles, tile in zip(idx_list, tiles, tile_tok):
        stream_tok = n_tiles * tile
        positions.append(pl.pallas_call(
            _pos_kernel,
            grid=(n_tiles,),
            in_specs=[pl.BlockSpec((tile, LANES), lambda i: (i, 0)),
                      pl.BlockSpec((1, 1, LANES), lambda i: (i, 0, 0))],
            out_specs=pl.BlockSpec((TOP_K, tile), lambda i: (0, i)),
            out_shape=jax.ShapeDtypeStruct((TOP_K, stream_tok), jnp.int32),
            compiler_params=_cparams(("parallel",)),
            name="route_pos",
        )(idx.reshape(stream_tok, LANES), base[first:first + n_tiles].reshape(n_tiles, 1, LANES)))
        first += n_tiles
    return block_e, rows_valid, positions


def _moe(h2p_list, idx_list, count_list, w1, b1, w2, b2):
    half = h2p_list[0].shape[1]
    d = 2 * half
    block_e, rows_valid, positions = _route(idx_list, count_list)
    n_blocks = block_e.shape[0]
    xs = _sc_scatter_rows(h2p_list, positions, n_blocks * MOE_ROWS)
    grid_spec = pltpu.PrefetchScalarGridSpec(
        num_scalar_prefetch=2,
        grid=(n_blocks,),
        in_specs=[pl.BlockSpec((MOE_ROWS, half), lambda i, be, nu: (i, 0)),
                  pl.BlockSpec((1, d, 2 * D_FF), lambda i, be, nu: (be[i], 0, 0)),
                  pl.BlockSpec((1, 1, 2 * D_FF), lambda i, be, nu: (be[i], 0, 0)),
                  pl.BlockSpec((1, D_FF, d), lambda i, be, nu: (be[i], 0, 0)),
                  pl.BlockSpec((1, 1, d), lambda i, be, nu: (be[i], 0, 0))],
        out_specs=pl.BlockSpec((MOE_ROWS, half), lambda i, be, nu: (i, 0)),
        scratch_shapes=[pltpu.VMEM((d, 2 * D_FF), BF16), pltpu.VMEM((D_FF, d), BF16)],
    )
    ys = pl.pallas_call(
        _ffn_kernel,
        grid_spec=grid_spec,
        out_shape=jax.ShapeDtypeStruct((n_blocks * MOE_ROWS, half), jnp.int32),
        compiler_params=_cparams(("arbitrary",)),
        name="moe_ffn",
    )(block_e, rows_valid, xs, w1, b1.reshape(N_EXPERTS, 1, -1), w2, b2.reshape(N_EXPERTS, 1, -1))
    return [_sc_gather_rows(ys, pos.reshape(-1)) for pos in positions]


def _final_kernel(x1_ref, y0_ref, y1_ref, y2_ref, y3_ref, gate_ref, mod_ref, g_ref, *rest):
    o_ref = rest[-1]
    gates = gate_ref[0]
    lo = hi = None
    for k, y_ref in enumerate((y0_ref, y1_ref, y2_ref, y3_ref)):
        y_lo, y_hi = _unpack_bf16_pairs(y_ref[0, 0])
        gk = gates[:, k:k + 1]
        lo = gk * y_lo if k == 0 else lo + gk * y_lo
        hi = gk * y_hi if k == 0 else hi + gk * y_hi
    y = jnp.concatenate([lo, hi], axis=-1)
    gate2 = mod_ref[0, 5:6, :]
    o_ref[0] = x1_ref[0] + gate2 * _rms(y, g_ref[...])


def _final(x1, yg, gate, mod, mod_row, g_post_ffn, tm, b0, n_batch, out_prev):
    bsz, seq, d = x1.shape
    tok3 = lambda b, i: (b, i, 0)
    yg = yg.reshape(TOP_K, bsz, seq, d // 2)
    slot_specs = [pl.BlockSpec((1, 1, tm, d // 2), functools.partial(lambda b, i, k: (k, b, i, 0), k=k))
                  for k in range(TOP_K)]
    in_specs = ([pl.BlockSpec((1, tm, d), tok3)] + slot_specs +
                [pl.BlockSpec((1, tm, LANES), tok3),
                 pl.BlockSpec((1, 6, d), lambda b, i: (mod_row(b), 0, 0)), _const_spec((1, d))])
    args = [x1, yg, yg, yg, yg, gate, mod, g_post_ffn.astype(F32).reshape(1, d)]
    aliases = {}
    if out_prev is not None:
        in_specs.append(pl.BlockSpec(memory_space=pl.ANY))
        args.append(out_prev)
        aliases = {len(args) - 1: 0}
    return pl.pallas_call(
        _final_kernel,
        grid=(bsz, seq // tm),
        in_specs=in_specs,
        out_specs=pl.BlockSpec((1, tm, d), lambda b, i: (b + b0, i, 0)),
        out_shape=jax.ShapeDtypeStruct((n_batch, seq, d), F32),
        input_output_aliases=aliases,
        compiler_params=_cparams(("parallel", "parallel")),
        name="final_residual",
    )(*args)


def _stream(x, mod, mod_row, p, ctx_kv, init_state, rope, tm, b0, bsz):
    _, seq, d = x.shape
    is_ctx = ctx_kv is None
    pj = _project(x, mod, p["g_pre_mix"], p["w_in"], p["b_gates"], mod_row, rope, is_ctx, tm, b0, bsz)
    if is_ctx:
        att = _attention(pj["q"], None, None, pj["kd"], pj["vd"], p["sink"], window=False)
    else:
        att = _attention(pj["q"], pj["kd"], pj["vd"], ctx_kv[0], ctx_kv[1], p["sink"], window=True)
    ml = _mlstm(pj["mq"], pj["mk"], pj["mv"], pj["gates"], init_state, emit_state=is_ctx, b0=b0)
    x1, h2, idx, gate, tile_counts = _merge(att, ml[0], ml[1], pj, x, mod, mod_row, p, tm, b0)
    moe_in = (h2.reshape(bsz * seq, d // 2), idx, tile_counts)
    return x1, gate, moe_in, pj, ml


def kernel(x_prompt, x_sample, c, cache_k, cache_v, state_C, state_n, state_m, c_ctx, w_ada, b_ada, g_pre_mix,
           w_in, b_gates, attn_sink, norm_mlstm, w_up_att, w_up_ml, w_out, g_post_mix, g_pre_ffn, w_router,
           b_router, w1, b1, w2, b2, g_post_ffn):
    depth = w_ada.shape[0]
    n_dec = c.shape[0]
    cond = jnp.concatenate([c_ctx[None, :], c], axis=0).astype(F32)
    cond = jnp.pad(cond, ((0, 16 - cond.shape[0]), (0, 0)))
    y_prompt, y_sample = x_prompt, x_sample
    ks_, vs_, cs_, ns_, ms_ = [], [], [], [], []
    for l in range(depth):
        p = dict(g_pre_mix=g_pre_mix[l], w_in=w_in[l], b_gates=b_gates[l], sink=attn_sink[l], norm_ml=norm_mlstm[l],
                 w_up_att=w_up_att[l], w_up_ml=w_up_ml[l], w_out=w_out[l], g_post_mix=g_post_mix[l],
                 g_pre_ffn=g_pre_ffn[l], w_router=w_router[l], b_router=b_router[l], w1=w1[l], b1=b1[l],
                 w2=w2[l], b2=b2[l], g_post_ffn=g_post_ffn[l])
        mod = _adaln(cond, w_ada[l], b_ada[l]).reshape(16, 6, D_MODEL)
        row_p = lambda b: 0
        bsz, seq = x_prompt.shape[:2]
        x1_p, gate_p, moe_p, pj, ml = _stream(y_prompt, mod, row_p, p, None, None, False, 256, 0, bsz)
        nat = pj["nat"]
        ks_.append(nat[..., :KV_W].reshape(bsz, seq, ATT_KV_HEADS, HEAD_DIM))
        vs_.append(nat[..., KV_W:].reshape(bsz, seq, ATT_KV_HEADS, HEAD_DIM))
        cs_.append(ml[2])
        ns_.append(ml[3])
        ms_.append(ml[4][..., 0])
        ctx_kv = (_dup_heads(cache_k[:, l], False), _dup_heads(cache_v[:, l], True))
        init = (state_C[:, l], state_n[:, l], state_m[:, l])
        half = n_dec // 2
        experts = (p["w1"], p["b1"].astype(F32), p["w2"], p["b2"].astype(F32))
        groups = []
        for b0 in (0, half):
            row_s = functools.partial(lambda b, off: b + off + 1, off=b0)
            sl = slice(b0, b0 + half)
            x1_s, gate_s, moe_s, _, _ = _stream(y_sample, mod, row_s, p, (ctx_kv[0][sl], ctx_kv[1][sl]),
                                                init, True, 512, b0, half)
            groups.append((x1_s, gate_s, moe_s, row_s))
        (x1_a, gate_a, moe_a, row_a), (x1_b, gate_b, moe_b, row_b) = groups
        yg_p, yg_a = _moe([moe_p[0], moe_a[0]], [moe_p[1], moe_a[1]], [moe_p[2], moe_a[2]], *experts)
        (yg_b,) = _moe([moe_b[0]], [moe_b[1]], [moe_b[2]], *experts)
        y_prompt = _final(x1_p, yg_p, gate_p, mod, row_p, p["g_post_ffn"], 256, 0, bsz, None)
        y_sample = _final(x1_a, yg_a, gate_a, mod, row_a, p["g_post_ffn"], 512, 0, n_dec, None)
        y_sample = _final(x1_b, yg_b, gate_b, mod, row_b, p["g_post_ffn"], 512, half, n_dec, y_sample)
    return (y_prompt, y_sample, jnp.stack(ks_, axis=1), jnp.stack(vs_, axis=1), jnp.stack(cs_, axis=1),
            jnp.stack(ns_, axis=1), jnp.stack(ms_, axis=1))
```

```python
import functools

import numpy as np
import jax
import jax.numpy as jnp
from jax import lax
from jax.experimental import pallas as pl
from jax.experimental.pallas import tpu as pltpu
from jax.experimental.pallas import tpu_sc as plsc

F32 = jnp.float32
BF16 = jnp.bfloat16

D_MODEL = 1024
GRID_W = 64
ATT_HEADS = 8
ATT_KV_HEADS = 2
ATT_GROUP = ATT_HEADS // ATT_KV_HEADS
HEAD_DIM = 64
BLOCK = 128
ROPE_THETA = 10000.0
AXIS_FREQS = HEAD_DIM // 4
ML_HEADS = 4
ML_DK = 128
ML_DV = 256
ML_CHUNK = 512
N_EXPERTS = 32
TOP_K = 4
D_FF = 1024
SWIGLU_ALPHA = 1.702
SWIGLU_LIMIT = 7.0
EPS = 1e-6
NEG = -1e30

Q_W = ATT_HEADS * HEAD_DIM
KV_W = ATT_KV_HEADS * HEAD_DIM
MLQK_W = ML_HEADS * ML_DK
MLV_W = ML_HEADS * ML_DV
GATE_W = 4 * ML_HEADS

LANES = 128
VMEM_LIMIT = 56 * 1024 * 1024
MOE_ROWS = 512
SC_CORES = 2
SC_SUBCORES = 16
SC_GATHER_CHUNK = 32


def _cparams(sem):
    return pltpu.CompilerParams(dimension_semantics=sem, vmem_limit_bytes=VMEM_LIMIT)


def _const_spec(shape):
    nd = len(shape)
    return pl.BlockSpec(shape, lambda *_: (0,) * nd)


def _adaln_kernel(cond_ref, w_ref, b_ref, o_ref):
    cnd = cond_ref[...]
    act = cnd * jax.nn.sigmoid(cnd)
    o_ref[...] = jnp.dot(act.astype(BF16), w_ref[...].astype(BF16),
                         preferred_element_type=F32) + b_ref[...]


def _adaln(cond, w_ada, b_ada):
    rows, d = cond.shape
    n = w_ada.shape[1]
    tn = 1536
    return pl.pallas_call(
        _adaln_kernel,
        grid=(n // tn,),
        in_specs=[_const_spec((rows, d)),
                  pl.BlockSpec((d, tn), lambda j: (0, j)),
                  pl.BlockSpec((1, tn), lambda j: (0, j))],
        out_specs=pl.BlockSpec((rows, tn), lambda j: (0, j)),
        out_shape=jax.ShapeDtypeStruct((rows, n), F32),
        compiler_params=_cparams(("parallel",)),
        name="adaln",
    )(cond, w_ada, b_ada.reshape(1, n))


def _pack_w_in(w_in, b_gates, rope, with_nat):
    o_q, o_k, o_v = 0, Q_W, Q_W + KV_W
    o_mq = Q_W + 2 * KV_W
    o_mk = o_mq + MLQK_W
    o_mv = o_mk + MLQK_W
    o_g = o_mv + MLV_W
    o_mo = o_g + GATE_W
    o_ga = o_mo + MLV_W
    o_gm = o_ga + D_MODEL
    half = HEAD_DIM // 2

    def head_cols(base, h, rot):
        lo = base + h * HEAD_DIM
        if rot:
            return [w_in[:, lo + half:lo + HEAD_DIM], w_in[:, lo:lo + half]]
        return [w_in[:, lo:lo + HEAD_DIM]]

    def q_cols(rot):
        cols = []
        for h in range(ATT_HEADS):
            cols += head_cols(o_q, h, rot)
        return cols

    def dup_cols(base, rot):
        cols = []
        for h in range(ATT_KV_HEADS):
            hc = head_cols(base, h, rot)
            cols += hc + hc
        return cols

    segs, cols, off = {}, [], 0

    def add(name, cl):
        nonlocal off
        width = sum(c.shape[1] for c in cl)
        segs[name] = (off, off + width)
        cols.extend(cl)
        off += width

    add("q", q_cols(False))
    if rope:
        add("qrot", q_cols(True))
    add("kd", dup_cols(o_k, False))
    if rope:
        add("kdrot", dup_cols(o_k, True))
    if with_nat:
        add("nat", [w_in[:, o_k:o_k + 2 * KV_W]])
    add("mq", [w_in[:, o_mq:o_mq + MLQK_W]])
    add("mv", [w_in[:, o_mv:o_mv + MLV_W]])
    add("mo", [w_in[:, o_mo:o_mo + MLV_W]])
    add("ga", [w_in[:, o_ga:o_ga + D_MODEL]])
    add("gm", [w_in[:, o_gm:o_gm + D_MODEL]])
    add("gates", [w_in[:, o_g:o_g + GATE_W], jnp.zeros((D_MODEL, LANES - GATE_W), w_in.dtype)])
    wp = jnp.concatenate(cols, axis=1).astype(BF16)
    wvd_t = jnp.concatenate(dup_cols(o_v, False) + [w_in[:, o_mk:o_mk + MLQK_W]], axis=1).T.astype(BF16)
    bg = jnp.pad(b_gates.astype(F32), (0, LANES - GATE_W)).reshape(1, LANES)
    return wp, wvd_t, bg, segs


def _rope_tables(n_tok):
    rows = n_tok // GRID_W
    row = np.repeat(np.arange(rows), GRID_W).astype(np.float32)
    col = np.tile(np.arange(GRID_W), rows).astype(np.float32)
    inv = (np.float32(ROPE_THETA) ** (-np.arange(AXIS_FREQS, dtype=np.float32) / AXIS_FREQS)).astype(np.float32)
    ang = np.concatenate([row[:, None] * inv, col[:, None] * inv], axis=-1).astype(np.float32)
    cos, sin = np.cos(ang), np.sin(ang)
    c64 = np.concatenate([cos, cos], axis=-1)
    s64 = np.concatenate([-sin, sin], axis=-1)
    return (jnp.asarray(np.tile(c64, (1, 2)), F32), jnp.asarray(np.tile(s64, (1, 2)), F32))


def _proj_kernel(*refs, segs, rope, with_nat):
    it = iter(refs)
    x_ref, mod_ref, g_ref, w_ref, wvt_ref, bg_ref = (next(it) for _ in range(6))
    cos_ref = sin_ref = None
    if rope:
        cos_ref, sin_ref = next(it), next(it)
    q_ref, kd_ref, vd_ref = next(it), next(it), next(it)
    nat_ref = next(it) if with_nat else None
    mq_ref, mk_ref, mv_ref, gates_ref, mo_ref, ga_ref, gm_ref = (next(it) for _ in range(7))

    x = x_ref[0]
    shift = mod_ref[0, 0:1, :]
    scale = mod_ref[0, 1:2, :]
    ms = jnp.mean(x * x, axis=-1, keepdims=True)
    xn = x * lax.rsqrt(ms + EPS) * g_ref[...]
    xb = (xn * (1.0 + scale) + shift).astype(BF16)

    def seg(name):
        lo, hi = segs[name]
        return jnp.dot(xb, w_ref[:, lo:hi], preferred_element_type=F32)

    uq = seg("q")
    ukd = seg("kd")
    if rope:
        uqr = seg("qrot")
        ukr = seg("kdrot")
        cs = cos_ref[...]
        sn = sin_ref[...]
        for h in range(ATT_HEADS // 2):
            sl = slice(h * LANES, (h + 1) * LANES)
            q_ref[0, h] = ((uq[:, sl] * cs + uqr[:, sl] * sn) * HEAD_DIM ** -0.5).astype(BF16)
        for h in range(ATT_KV_HEADS):
            sl = slice(h * LANES, (h + 1) * LANES)
            kd_ref[0, :, sl] = (ukd[:, sl] * cs + ukr[:, sl] * sn).astype(BF16)
    else:
        for h in range(ATT_HEADS // 2):
            sl = slice(h * LANES, (h + 1) * LANES)
            q_ref[0, h] = (uq[:, sl] * HEAD_DIM ** -0.5).astype(BF16)
        kd_ref[0] = ukd.astype(BF16)
    ut = lax.dot_general(wvt_ref[...], xb, (((1,), (1,)), ((), ())), preferred_element_type=F32)
    vd_ref[0] = ut[:2 * LANES].astype(BF16)
    mk_ref[0] = ut[2 * LANES:].astype(BF16)
    if with_nat:
        nat_ref[0] = seg("nat")
    mq_ref[0] = (seg("mq") * ML_DK ** -0.5).astype(BF16)
    mv_ref[0] = seg("mv").astype(BF16)
    gates_ref[0] = seg("gates") + bg_ref[...]
    mo_ref[0] = _sigmoid(seg("mo")).astype(BF16)
    ga_ref[0] = _sigmoid(seg("ga")).astype(BF16)
    gm_ref[0] = _sigmoid(seg("gm")).astype(BF16)


def _project(x, mod, g_pre, w_in, b_gates, mod_row, rope, with_nat, tm, b0, bsz):
    _, seq, d = x.shape
    wp, wvd_t, bg, segs = _pack_w_in(w_in, b_gates, rope, with_nat)
    nw = wp.shape[1]
    tok3 = lambda b, i: (b, i, 0)
    in_specs = [pl.BlockSpec((1, tm, d), lambda b, i: (b + b0, i, 0)),
                pl.BlockSpec((1, 6, d), lambda b, i: (mod_row(b), 0, 0)),
                _const_spec((1, d)),
                pl.BlockSpec((d, nw), lambda b, i: (0, 0), pipeline_mode=pl.Buffered(1)),
                _const_spec((2 * LANES + MLQK_W, d)), _const_spec((1, LANES))]
    args = [x, mod, g_pre.reshape(1, d), wp, wvd_t, bg]
    if rope:
        cos_t, sin_t = _rope_tables(seq)
        in_specs += [pl.BlockSpec((tm, LANES), lambda b, i: (i, 0))] * 2
        args += [cos_t, sin_t]

    def tok_out(width, dtype):
        return (jax.ShapeDtypeStruct((bsz, seq, width), dtype), pl.BlockSpec((1, tm, width), tok3))

    outs = [(jax.ShapeDtypeStruct((bsz, ATT_HEADS // 2, seq, LANES), BF16),
             pl.BlockSpec((1, ATT_HEADS // 2, tm, LANES), lambda b, i: (b, 0, i, 0))),
            tok_out(2 * LANES, BF16),
            (jax.ShapeDtypeStruct((bsz, 2 * LANES, seq), BF16),
             pl.BlockSpec((1, 2 * LANES, tm), lambda b, i: (b, 0, i)))]
    if with_nat:
        outs.append(tok_out(2 * KV_W, F32))
    outs += [tok_out(MLQK_W, BF16),
             (jax.ShapeDtypeStruct((bsz, MLQK_W, seq), BF16),
              pl.BlockSpec((1, MLQK_W, tm), lambda b, i: (b, 0, i))),
             tok_out(MLV_W, BF16), tok_out(LANES, F32),
             tok_out(MLV_W, BF16), tok_out(D_MODEL, BF16), tok_out(D_MODEL, BF16)]
    res = pl.pallas_call(
        functools.partial(_proj_kernel, segs=segs, rope=rope, with_nat=with_nat),
        grid=(bsz, seq // tm),
        in_specs=in_specs,
        out_specs=[o[1] for o in outs],
        out_shape=[o[0] for o in outs],
        compiler_params=_cparams(("parallel", "parallel")),
        name="proj_rope" if rope else "proj_ctx",
    )(*args)
    names = ["q", "kd", "vd"] + (["nat"] if with_nat else []) + ["mq", "mk", "mv", "gates", "mo", "ga", "gm"]
    return dict(zip(names, res))


def _attn_kernel(*refs, window, n_blocks, n_ctx):
    it = iter(refs)
    sink_ref, q_ref = next(it), next(it)
    if window:
        kp_ref, kc_ref, kn_ref, vp_ref, vc_ref, vn_ref = (next(it) for _ in range(6))
    ck_ref, cv_ref, o_ref = next(it), next(it), next(it)
    j = pl.program_id(1)
    n_win = 3 * BLOCK if window else 0
    cols = ATT_GROUP * BLOCK

    col_id = lax.broadcasted_iota(jnp.int32, (1, cols), 1)
    if window:
        kl = lax.broadcasted_iota(jnp.int32, (BLOCK, cols), 0)
        ql = lax.broadcasted_iota(jnp.int32, (BLOCK, cols), 1) & (BLOCK - 1)
        prev_ok = kl >= ql + jnp.where(j == 0, BLOCK, 0)
        next_ok = kl <= ql - jnp.where(j == n_blocks - 1, BLOCK, 0)
    lane_lo = lax.broadcasted_iota(jnp.int32, (BLOCK, LANES), 1) < HEAD_DIM
    lane_row = lax.broadcasted_iota(jnp.int32, (1, LANES), 1) < HEAD_DIM
    half_lo = jnp.where(lane_row, 1.0, 0.0).astype(BF16)
    half_hi = jnp.where(lane_row, 0.0, 1.0).astype(BF16)

    for kvh in range(ATT_KV_HEADS):
        sl = slice(kvh * LANES, (kvh + 1) * LANES)
        if window:
            keys = jnp.concatenate([kp_ref[0, :, sl], kc_ref[0, :, sl], kn_ref[0, :, sl], ck_ref[0, :, sl]], axis=0)
            vals_t = jnp.concatenate([vp_ref[0, sl, :], vc_ref[0, sl, :], vn_ref[0, sl, :], cv_ref[0, sl, :]], axis=1)
        else:
            keys = ck_ref[0, :, sl]
            vals_t = cv_ref[0, sl, :]
        qs = jnp.concatenate([q_ref[0, kvh * (ATT_GROUP // 2) + g // 2] * (half_lo if g % 2 == 0 else half_hi)
                              for g in range(ATT_GROUP)], axis=0)
        s = lax.dot_general(keys, qs, (((1,), (1,)), ((), ())), preferred_element_type=F32)
        if window:
            s = jnp.concatenate([jnp.where(prev_ok, s[:BLOCK], NEG), s[BLOCK:2 * BLOCK],
                                 jnp.where(next_ok, s[2 * BLOCK:n_win], NEG), s[n_win:]], axis=0)
        snk = jnp.full((1, cols), sink_ref[kvh * ATT_GROUP + ATT_GROUP - 1], F32)
        for g in range(ATT_GROUP - 2, -1, -1):
            snk = jnp.where(col_id < (g + 1) * BLOCK, sink_ref[kvh * ATT_GROUP + g], snk)
        m = jnp.maximum(jnp.max(s, axis=0, keepdims=True), snk)
        p = jnp.exp(s - m)
        den = jnp.sum(p, axis=0, keepdims=True) + jnp.exp(snk - m)
        o_t = jnp.dot(vals_t, p.astype(BF16), preferred_element_type=F32) / den
        for pair in range(ATT_GROUP // 2):
            even = o_t[:, (2 * pair) * BLOCK:(2 * pair + 1) * BLOCK].T
            odd = o_t[:, (2 * pair + 1) * BLOCK:(2 * pair + 2) * BLOCK].T
            col = (kvh * (ATT_GROUP // 2) + pair) * LANES
            o_ref[0, :, col:col + LANES] = jnp.where(lane_lo, even, odd).astype(BF16)


def _attention(q, kd, vd_t, ckd, cvd_t, sink, window):
    bsz, _, seq, _ = q.shape
    nb = seq // BLOCK
    n_ctx = ckd.shape[1]
    in_specs = [pl.BlockSpec(memory_space=pltpu.SMEM),
                pl.BlockSpec((1, ATT_HEADS // 2, BLOCK, LANES), lambda b, j: (b, 0, j, 0))]
    args = [sink.astype(F32), q]
    if window:
        in_specs += [pl.BlockSpec((1, BLOCK, 2 * LANES), lambda b, j: (b, jnp.maximum(j - 1, 0), 0)),
                     pl.BlockSpec((1, BLOCK, 2 * LANES), lambda b, j: (b, j, 0)),
                     pl.BlockSpec((1, BLOCK, 2 * LANES), lambda b, j: (b, jnp.minimum(j + 1, nb - 1), 0)),
                     pl.BlockSpec((1, 2 * LANES, BLOCK), lambda b, j: (b, 0, jnp.maximum(j - 1, 0))),
                     pl.BlockSpec((1, 2 * LANES, BLOCK), lambda b, j: (b, 0, j)),
                     pl.BlockSpec((1, 2 * LANES, BLOCK), lambda b, j: (b, 0, jnp.minimum(j + 1, nb - 1)))]
        args += [kd, kd, kd, vd_t, vd_t, vd_t]
    in_specs += [pl.BlockSpec((1, n_ctx, 2 * LANES), lambda b, j: (b, 0, 0)),
                 pl.BlockSpec((1, 2 * LANES, n_ctx), lambda b, j: (b, 0, 0))]
    args += [ckd, cvd_t]
    return pl.pallas_call(
        functools.partial(_attn_kernel, window=window, n_blocks=nb, n_ctx=n_ctx),
        grid=(bsz, nb),
        in_specs=in_specs,
        out_specs=pl.BlockSpec((1, BLOCK, Q_W), lambda b, j: (b, j, 0)),
        out_shape=jax.ShapeDtypeStruct((bsz, seq, Q_W), BF16),
        compiler_params=_cparams(("parallel", "parallel")),
        name="attn_window" if window else "attn_ctx",
    )(*args)


def _dup_heads(t, transpose):
    b, l = t.shape[:2]
    out = jnp.concatenate([t, t], axis=-1).reshape(b, l, ATT_KV_HEADS * LANES).astype(BF16)
    return jnp.swapaxes(out, 1, 2) if transpose else out


def _split3(x):
    hi = x.astype(BF16)
    r1 = x - hi.astype(F32)
    mid = r1.astype(BF16)
    lo = (r1 - mid.astype(F32)).astype(BF16)
    return hi, mid, lo


def _log_sigmoid(x):
    return jnp.minimum(x, 0.0) - jnp.log1p(jnp.exp(-jnp.abs(x)))


def _mlstm_kernel(*refs, chunk_len, n_chunks, has_init, emit_state):
    it = iter(refs)
    qkvg = [[next(it) for _ in range(4)] for _ in range(2)]
    if has_init:
        c0_ref, n0_ref, m0_ref = next(it), next(it), next(it)
    h_refs = [next(it), next(it)]
    if emit_state:
        co_ref, no_ref, mo_ref = next(it), next(it), next(it)
    st_s, m_s = next(it), next(it)
    t = pl.program_id(1)
    L = chunk_len

    @pl.when(t == 0)
    def _():
        for d in range(2):
            for h in range(ML_HEADS):
                if has_init:
                    st_s[d, h, :, :ML_DV] = c0_ref[0, d, h].T
                    st_s[d, h, :, ML_DV:] = jnp.broadcast_to(n0_ref[0, d, h:h + 1, :], (LANES, ML_DK)).T
                    m_s[d, h] = m0_ref[0, d, h:h + 1, :]
                else:
                    st_s[d, h] = jnp.zeros((ML_DK, ML_DV + LANES), F32)
                    m_s[d, h] = jnp.zeros((1, LANES), F32)

    ri = lax.broadcasted_iota(jnp.int32, (L, L), 0)
    ci = lax.broadcasted_iota(jnp.int32, (L, L), 1)
    ones_blk = jnp.ones((L, LANES), BF16)

    for d in range(2):
        q_ref, kt_ref, v_ref, g_ref = qkvg[d]
        tri = (ci <= ri) if d == 0 else (ci >= ri)
        tri_b = jnp.where(tri, 1.0, 0.0).astype(BF16)
        g = g_ref[0]
        lf = _log_sigmoid(g)
        g_t = g.T
        lf_t = lf.T
        bc = sum(jnp.dot(tri_b, part, preferred_element_type=F32) for part in _split3(lf))
        br = sum(lax.dot_general(part, tri_b, (((1,), (1,)), ((), ())), preferred_element_type=F32)
                 for part in _split3(lf_t))
        tot = jnp.sum(lf, axis=0, keepdims=True)

        for h in range(ML_HEADS):
            icol_i = 2 * ML_HEADS * d + h
            fcol = icol_i + ML_HEADS
            bcol = bc[:, fcol:fcol + 1]
            brow = br[fcol:fcol + 1, :]
            irow = g_t[icol_i:icol_i + 1, :]
            b_last = tot[:, fcol:fcol + 1]
            m_prev = m_s[d, h][:, 0:1]
            st_prev = st_s[d, h]
            qh = q_ref[0, :, h * ML_DK:(h + 1) * ML_DK]
            kt = kt_ref[0, h * ML_DK:(h + 1) * ML_DK, :]
            v_aug = jnp.concatenate([v_ref[0, :, h * ML_DV:(h + 1) * ML_DV], ones_blk], axis=1)

            rel = jnp.where(tri, irow - brow, NEG)
            inter = bcol + m_prev
            m_t = jnp.maximum(inter, bcol + jnp.max(rel, axis=-1, keepdims=True))
            w = jnp.exp(rel + (bcol - m_t))
            w_inter = jnp.exp(inter - m_t)
            s = jnp.dot(qh, kt, preferred_element_type=F32) * w
            lhs = jnp.concatenate([s.astype(BF16), (qh.astype(F32) * w_inter).astype(BF16)], axis=1)
            rhs = jnp.concatenate([v_aug, st_prev.astype(BF16)], axis=0)
            mix = jnp.dot(lhs, rhs, preferred_element_type=F32)
            inv = 1.0 / jnp.maximum(jnp.abs(mix[:, ML_DV:]), jnp.exp(-m_t))
            for c in range(ML_DV // LANES):
                lo = h * ML_DV + c * LANES
                h_refs[d][0, :, lo:lo + LANES] = mix[:, c * LANES:(c + 1) * LANES] * inv

            a = b_last - brow + irow
            m_new = jnp.maximum(b_last + m_prev, jnp.max(a, axis=-1, keepdims=True))
            wk = jnp.exp(a - m_new)
            decay = jnp.exp(b_last + m_prev - m_new)
            kw_t = (kt.astype(F32) * wk).astype(BF16)
            st_new = decay * st_prev + jnp.dot(kw_t, v_aug, preferred_element_type=F32)
            st_s[d, h] = st_new
            m_s[d, h] = jnp.broadcast_to(m_new, (1, LANES))
            if emit_state:
                @pl.when(t == n_chunks - 1)
                def _():
                    co_ref[0, d, h] = st_new[:, :ML_DV].T
                    no_ref[0, d, h:h + 1, :] = st_new[:, ML_DV:].T[0:1, :]
                    mo_ref[0, d, h:h + 1, :] = jnp.broadcast_to(m_new, (1, LANES))


def _mlstm(mq, mk_t, mv, gates, init, emit_state, b0=0):
    bsz, seq, _ = mq.shape
    chunk_len = min(ML_CHUNK, seq)
    nc = seq // chunk_len
    fwd = lambda b, t: (b, t, 0)
    bwd = lambda b, t: (b, nc - 1 - t, 0)
    fwd_t = lambda b, t: (b, 0, t)
    bwd_t = lambda b, t: (b, 0, nc - 1 - t)
    state5 = lambda b, t: (b, 0, 0, 0, 0)
    state4 = lambda b, t: (b, 0, 0, 0)
    init5 = lambda b, t: (b + b0, 0, 0, 0, 0)
    init4 = lambda b, t: (b + b0, 0, 0, 0)
    in_specs, args = [], []
    for chunk, chunk_t in ((fwd, fwd_t), (bwd, bwd_t)):
        in_specs += [pl.BlockSpec((1, chunk_len, MLQK_W), chunk), pl.BlockSpec((1, MLQK_W, chunk_len), chunk_t),
                     pl.BlockSpec((1, chunk_len, MLV_W), chunk), pl.BlockSpec((1, chunk_len, LANES), chunk)]
        args += [mq, mk_t, mv, gates]
    if init is not None:
        c0, n0, m0 = init
        in_specs += [pl.BlockSpec((1, 2, ML_HEADS, ML_DV, ML_DK), init5),
                     pl.BlockSpec((1, 2, ML_HEADS, ML_DK), init4),
                     pl.BlockSpec((1, 2, ML_HEADS, LANES), init4)]
        args += [c0.astype(F32), n0.astype(F32),
                 jnp.broadcast_to(m0.astype(F32)[..., None], m0.shape + (LANES,))]
    out_shape = [jax.ShapeDtypeStruct((bsz, seq, MLV_W), F32)] * 2
    out_specs = [pl.BlockSpec((1, chunk_len, MLV_W), fwd), pl.BlockSpec((1, chunk_len, MLV_W), bwd)]
    if emit_state:
        out_shape += [jax.ShapeDtypeStruct((bsz, 2, ML_HEADS, ML_DV, ML_DK), F32),
                      jax.ShapeDtypeStruct((bsz, 2, ML_HEADS, ML_DK), F32),
                      jax.ShapeDtypeStruct((bsz, 2, ML_HEADS, LANES), F32)]
        out_specs += [pl.BlockSpec((1, 2, ML_HEADS, ML_DV, ML_DK), state5),
                      pl.BlockSpec((1, 2, ML_HEADS, ML_DK), state4),
                      pl.BlockSpec((1, 2, ML_HEADS, LANES), state4)]
    return pl.pallas_call(
        functools.partial(_mlstm_kernel, chunk_len=chunk_len, n_chunks=nc, has_init=init is not None, emit_state=emit_state),
        grid=(bsz, nc),
        in_specs=in_specs,
        out_specs=out_specs,
        out_shape=out_shape,
        scratch_shapes=[pltpu.VMEM((2, ML_HEADS, ML_DK, ML_DV + LANES), F32),
                        pltpu.VMEM((2, ML_HEADS, 1, LANES), F32)],
        compiler_params=_cparams(("parallel", "arbitrary")),
        name="mlstm_state" if emit_state else "mlstm",
    )(*args)


def _sigmoid(x):
    return 0.5 * jnp.tanh(0.5 * x) + 0.5


def _rms(x, g):
    return x * lax.rsqrt(jnp.mean(x * x, axis=-1, keepdims=True) + EPS) * g


def _pack_bf16_pairs(x):
    n = x.shape[1] // 2
    lo = pltpu.bitcast(x[:, :n].astype(BF16).astype(F32), jnp.int32)
    hi = pltpu.bitcast(x[:, n:].astype(BF16).astype(F32), jnp.int32)
    return lax.shift_right_logical(lo, 16) | hi


def _unpack_bf16_pairs(p):
    lo = pltpu.bitcast(lax.shift_left(p, 16), F32)
    hi = pltpu.bitcast(p & jnp.int32(-65536), F32)
    return lo, hi


def _merge_kernel(att_ref, hf_ref, hb_ref, mo_ref, ga_ref, gm_ref, x_ref, mod_ref, nml_ref, gpm_ref, gpf_ref,
                  wua_ref, wum_ref, wo_ref, wr_ref, br_ref, x1_ref, h2_ref, idx_ref, gate_ref, cnt_ref):
    hsum = hf_ref[0] + hb_ref[0]
    parts = []
    for h in range(ML_HEADS):
        hh = hsum[:, h * ML_DV:(h + 1) * ML_DV]
        parts.append(hh * lax.rsqrt(jnp.mean(hh * hh, axis=-1, keepdims=True) + EPS))
    hn = jnp.concatenate(parts, axis=-1) * nml_ref[...]
    ml = (hn * mo_ref[0].astype(F32)).astype(BF16)
    a = jnp.dot(att_ref[0], wua_ref[...], preferred_element_type=F32)
    m = jnp.dot(ml, wum_ref[...], preferred_element_type=F32)
    z = (ga_ref[0].astype(F32) * a + gm_ref[0].astype(F32) * m).astype(BF16)
    mix = jnp.dot(z, wo_ref[...], preferred_element_type=F32)
    gate1 = mod_ref[0, 2:3, :]
    shift2 = mod_ref[0, 3:4, :]
    scale2 = mod_ref[0, 4:5, :]
    x1 = x_ref[0] + gate1 * _rms(mix, gpm_ref[...])
    x1_ref[0] = x1
    h2 = _rms(x1, gpf_ref[...]) * (1.0 + scale2) + shift2
    h2_ref[0] = _pack_bf16_pairs(h2)

    h2_hi = h2.astype(BF16)
    h2_lo = (h2 - h2_hi.astype(F32)).astype(BF16)
    wr = wr_ref[...]
    wr_hi = wr.astype(BF16)
    wr_lo = (wr - wr_hi.astype(F32)).astype(BF16)
    hi_terms = jnp.dot(h2_hi, jnp.concatenate([wr_hi, wr_lo], axis=1), preferred_element_type=F32)
    logits = (hi_terms[:, :LANES] + hi_terms[:, LANES:]
              + jnp.dot(h2_lo, wr_hi, preferred_element_type=F32)) + br_ref[...]
    lane = lax.broadcasted_iota(jnp.int32, logits.shape, 1).astype(F32)
    work = jnp.where(lane < N_EXPERTS, logits, -jnp.inf)
    idx_out = jnp.zeros(logits.shape, F32)
    val_out = jnp.zeros(logits.shape, F32)
    picked = jnp.zeros(logits.shape, F32)
    top0 = None
    esum = None
    for k in range(TOP_K):
        mx = jnp.max(work, axis=-1, keepdims=True)
        sel = jnp.min(jnp.where(work == mx, lane, float(LANES)), axis=-1, keepdims=True)
        if k == 0:
            top0 = mx
        e = jnp.exp(mx - top0)
        esum = e if k == 0 else esum + e
        idx_out = jnp.where(lane == k, sel, idx_out)
        val_out = jnp.where(lane == k, e, val_out)
        picked = jnp.where(lane == sel, 1.0, picked)
        work = jnp.where(lane == sel, -jnp.inf, work)
    idx_ref[0] = idx_out.astype(jnp.int32)
    gate_ref[0] = val_out / esum
    cnt_ref[0] = jnp.sum(picked, axis=0, keepdims=True)


def _merge(att, hf, hb, pj, x, mod, mod_row, p, tm, b0):
    bsz = att.shape[0]
    _, seq, d = x.shape
    tok3 = lambda b, i: (b, i, 0)
    wr = jnp.pad(p["w_router"].astype(F32), ((0, 0), (0, LANES - N_EXPERTS)))
    br = jnp.pad(p["b_router"].astype(F32), (0, LANES - N_EXPERTS)).reshape(1, LANES)
    row = lambda v: v.astype(F32).reshape(1, -1)
    in_specs = [pl.BlockSpec((1, tm, Q_W), tok3),
                pl.BlockSpec((1, tm, MLV_W), tok3), pl.BlockSpec((1, tm, MLV_W), tok3),
                pl.BlockSpec((1, tm, MLV_W), tok3), pl.BlockSpec((1, tm, d), tok3), pl.BlockSpec((1, tm, d), tok3),
                pl.BlockSpec((1, tm, d), lambda b, i: (b + b0, i, 0)),
                pl.BlockSpec((1, 6, d), lambda b, i: (mod_row(b), 0, 0)),
                _const_spec((1, MLV_W)), _const_spec((1, d)), _const_spec((1, d)),
                _const_spec((Q_W, d)), _const_spec((MLV_W, d)), _const_spec((d, d)),
                _const_spec((d, LANES)), _const_spec((1, LANES))]
    out_shape = [jax.ShapeDtypeStruct((bsz, seq, d), F32), jax.ShapeDtypeStruct((bsz, seq, d // 2), jnp.int32),
                 jax.ShapeDtypeStruct((bsz, seq, LANES), jnp.int32), jax.ShapeDtypeStruct((bsz, seq, LANES), F32),
                 jax.ShapeDtypeStruct((bsz * (seq // tm), 1, LANES), F32)]
    out_specs = [pl.BlockSpec((1, tm, d), tok3), pl.BlockSpec((1, tm, d // 2), tok3),
                 pl.BlockSpec((1, tm, LANES), tok3), pl.BlockSpec((1, tm, LANES), tok3),
                 pl.BlockSpec((1, 1, LANES), lambda b, i: (b * (seq // tm) + i, 0, 0))]
    return pl.pallas_call(
        _merge_kernel,
        grid=(bsz, seq // tm),
        in_specs=in_specs,
        out_specs=out_specs,
        out_shape=out_shape,
        compiler_params=_cparams(("parallel", "parallel")),
        name="merge_router",
    )(att, hf, hb, pj["mo"], pj["ga"], pj["gm"], x, mod, row(p["norm_ml"]), row(p["g_post_mix"]),
      row(p["g_pre_ffn"]), p["w_up_att"].astype(BF16), p["w_up_ml"].astype(BF16), p["w_out"].astype(BF16), wr, br)


def _sc_gather_rows(table, idx):
    n = idx.shape[0]
    width = table.shape[1]
    chunk = SC_GATHER_CHUNK
    n_workers = SC_CORES * SC_SUBCORES
    rows_per_worker = n // n_workers
    n_pairs = rows_per_worker // (2 * chunk)
    assert n_pairs * 2 * chunk * n_workers == n
    mesh = plsc.VectorSubcoreMesh(core_axis_name="c", subcore_axis_name="s")

    def body(table_hbm, idx_hbm, out_hbm, idx_v, rows_v, gsem, osem):
        wid = lax.axis_index("s") * SC_CORES + lax.axis_index("c")
        base0 = wid * rows_per_worker

        def gather_copy(b):
            return pltpu.make_async_copy(table_hbm.at[idx_v.at[b]], rows_v.at[b], gsem.at[b])

        def write_copy(ci, b):
            base = pl.multiple_of(base0 + ci * chunk, 8)
            return pltpu.make_async_copy(rows_v.at[b], out_hbm.at[pl.ds(base, chunk)], osem.at[b])

        def issue(ci, b):
            base = pl.multiple_of(base0 + ci * chunk, 8)
            pltpu.sync_copy(idx_hbm.at[pl.ds(base, chunk)], idx_v.at[b])
            gather_copy(b).start()

        def finish(ci, b):
            gather_copy(b).wait()
            write_copy(ci, b).start()

        issue(0, 0)

        @pl.loop(0, n_pairs)
        def _(j):
            @pl.when(j > 0)
            def _():
                write_copy(2 * j - 1, 1).wait()
            issue(2 * j + 1, 1)
            finish(2 * j, 0)

            @pl.when(j < n_pairs - 1)
            def _():
                write_copy(2 * j, 0).wait()
                issue(2 * j + 2, 0)
            finish(2 * j + 1, 1)

        write_copy(2 * n_pairs - 2, 0).wait()
        write_copy(2 * n_pairs - 1, 1).wait()

    return pl.kernel(
        body, mesh=mesh,
        out_type=jax.ShapeDtypeStruct((n, width), table.dtype),
        scratch_types=[pltpu.VMEM((2, chunk), jnp.int32), pltpu.VMEM((2, chunk, width), table.dtype),
                       pltpu.SemaphoreType.DMA((2,)), pltpu.SemaphoreType.DMA((2,))],
    )(table, idx)


def _ffn_kernel(be_ref, nu_ref, xs_ref, w1_ref, b1_ref, w2_ref, b2_ref, ys_ref, w1b_s, w2b_s):
    i = pl.program_id(0)
    e = be_ref[i]
    e_prev = be_ref[jnp.maximum(i - 1, 0)]

    @pl.when((i == 0) | (e != e_prev))
    def _():
        w1b_s[...] = w1_ref[0].astype(BF16)
        w2b_s[...] = w2_ref[0].astype(BF16)

    @pl.when(i < nu_ref[0])
    def _():
        x_lo, x_hi = _unpack_bf16_pairs(xs_ref[...])
        xb = jnp.concatenate([x_lo.astype(BF16), x_hi.astype(BF16)], axis=1)
        hmid = jnp.dot(xb, w1b_s[...], preferred_element_type=F32) + b1_ref[0]
        glu = jnp.minimum(hmid[:, :D_FF], SWIGLU_LIMIT)
        lin = jnp.clip(hmid[:, D_FF:], -SWIGLU_LIMIT, SWIGLU_LIMIT)
        act = ((lin + 1.0) * glu * _sigmoid(SWIGLU_ALPHA * glu)).astype(BF16)
        y = jnp.dot(act, w2b_s[...], preferred_element_type=F32) + b2_ref[0]
        ys_ref[...] = _pack_bf16_pairs(y)

    @pl.when(i >= nu_ref[0])
    def _():
        ys_ref[...] = jnp.zeros(ys_ref.shape, jnp.int32)


def _sc_scatter_rows(tables, positions, n_rows):
    width = tables[0].shape[1]
    chunk = SC_GATHER_CHUNK
    n_workers = SC_CORES * SC_SUBCORES
    n_streams = len(tables)
    mesh = plsc.VectorSubcoreMesh(core_axis_name="c", subcore_axis_name="s")

    def body(*refs):
        table_refs = refs[0:2 * n_streams:2]
        pos_refs = refs[1:2 * n_streams:2]
        out_hbm, idx_v, rows_v, lsem, ssem = refs[2 * n_streams:]
        wid = lax.axis_index("s") * SC_CORES + lax.axis_index("c")

        for table_hbm, pos_hbm in zip(table_refs, pos_refs):
            tok_per_worker = table_hbm.shape[0] // n_workers
            n_pairs = tok_per_worker // (2 * chunk)
            assert n_pairs * 2 * chunk * n_workers == table_hbm.shape[0]
            base0 = wid * tok_per_worker

            def load_copy(ci, b):
                base = pl.multiple_of(base0 + ci * chunk, 8)
                return pltpu.make_async_copy(table_hbm.at[pl.ds(base, chunk)], rows_v.at[b], lsem.at[b])

            def scatter_copy(b, k):
                return pltpu.make_async_copy(rows_v.at[b], out_hbm.at[idx_v.at[b, k]], ssem.at[b])

            def load(ci, b):
                base = pl.multiple_of(base0 + ci * chunk, 8)
                load_copy(ci, b).start()
                for k in range(TOP_K):
                    pltpu.sync_copy(pos_hbm.at[k, pl.ds(base, chunk)], idx_v.at[b, k])

            def scatter(ci, b):
                load_copy(ci, b).wait()
                for k in range(TOP_K):
                    scatter_copy(b, k).start()

            def drain(b):
                for k in range(TOP_K):
                    scatter_copy(b, k).wait()

            load(0, 0)

            @pl.loop(0, n_pairs)
            def _(j):
                @pl.when(j > 0)
                def _():
                    drain(1)
                load(2 * j + 1, 1)
                scatter(2 * j, 0)

                @pl.when(j < n_pairs - 1)
                def _():
                    drain(0)
                    load(2 * j + 2, 0)
                scatter(2 * j + 1, 1)

            drain(0)
            drain(1)

    args = [a for pair in zip(tables, positions) for a in pair]
    return pl.kernel(
        body, mesh=mesh,
        out_type=jax.ShapeDtypeStruct((n_rows, width), tables[0].dtype),
        scratch_types=[pltpu.VMEM((2, TOP_K, chunk), jnp.int32), pltpu.VMEM((2, chunk, width), tables[0].dtype),
                       pltpu.SemaphoreType.DMA((2,)), pltpu.SemaphoreType.DMA((2,))],
    )(*args)


def _pos_kernel(idx_ref, base_ref, pos_ref):
    idx = idx_ref[...]
    rows = idx.shape[0]
    lane = lax.broadcasted_iota(jnp.int32, idx.shape, 1)
    hots = [jnp.where(lane == idx[:, k:k + 1], 1.0, 0.0) for k in range(TOP_K)]
    cnt = hots[0] + hots[1] + hots[2] + hots[3]
    ri = lax.broadcasted_iota(jnp.int32, (rows, rows), 0)
    ci = lax.broadcasted_iota(jnp.int32, (rows, rows), 1)
    earlier = jnp.where(ci < ri, 1.0, 0.0).astype(BF16)
    ahead = jnp.dot(earlier, cnt.astype(BF16), preferred_element_type=F32) + base_ref[0]
    posmat = jnp.zeros(idx.shape, F32)
    for k in range(TOP_K):
        posmat = jnp.where(lane == k, jnp.sum(hots[k] * ahead, axis=-1, keepdims=True), posmat)
    pos_ref[...] = posmat.T[:TOP_K, :].astype(jnp.int32)


def _route(idx_list, count_list):
    tiles = [c.shape[0] for c in count_list]
    tile_tok = [idx.size // LANES // n for idx, n in zip(idx_list, tiles)]
    n_tok = sum(n * sz for n, sz in zip(tiles, tile_tok))
    n_blocks = n_tok * TOP_K // MOE_ROWS + N_EXPERTS
    cnt = jnp.concatenate([c.reshape(-1, LANES) for c in count_list], axis=0).astype(jnp.int32)
    tile_off = jnp.cumsum(cnt, axis=0) - cnt
    total = jnp.sum(cnt, axis=0)
    padded = (total + MOE_ROWS - 1) // MOE_ROWS * MOE_ROWS
    pend = jnp.cumsum(padded)
    base = ((pend - padded)[None, :] + tile_off).astype(F32)
    starts = jnp.arange(n_blocks, dtype=jnp.int32) * MOE_ROWS
    block_e = jnp.minimum(jnp.sum(pend[None, :N_EXPERTS] <= starts[:, None], axis=1), N_EXPERTS - 1).astype(jnp.int32)
    n_used = (pend[N_EXPERTS - 1] // MOE_ROWS).astype(jnp.int32).reshape(1)
    positions, first = [], 0
    for idx, n_tiles, tile in zip(idx_list, tiles, tile_tok):
        stream_tok = n_tiles * tile
        positions.append(pl.pallas_call(
            _pos_kernel,
            grid=(n_tiles,),
            in_specs=[pl.BlockSpec((tile, LANES), lambda i: (i, 0)),
                      pl.BlockSpec((1, 1, LANES), lambda i: (i, 0, 0))],
            out_specs=pl.BlockSpec((TOP_K, tile), lambda i: (0, i)),
            out_shape=jax.ShapeDtypeStruct((TOP_K, stream_tok), jnp.int32),
            compiler_params=_cparams(("parallel",)),
            name="route_pos",
        )(idx.reshape(stream_tok, LANES), base[first:first + n_tiles].reshape(n_tiles, 1, LANES)))
        first += n_tiles
    return block_e, n_used, positions


def _moe(h2p_list, idx_list, count_list, w1, b1, w2, b2):
    half = h2p_list[0].shape[1]
    d = 2 * half
    block_e, n_used, positions = _route(idx_list, count_list)
    n_blocks = block_e.shape[0]
    xs = _sc_scatter_rows(h2p_list, positions, n_blocks * MOE_ROWS)
    grid_spec = pltpu.PrefetchScalarGridSpec(
        num_scalar_prefetch=2,
        grid=(n_blocks,),
        in_specs=[pl.BlockSpec((MOE_ROWS, half), lambda i, be, nu: (i, 0)),
                  pl.BlockSpec((1, d, 2 * D_FF), lambda i, be, nu: (be[i], 0, 0)),
                  pl.BlockSpec((1, 1, 2 * D_FF), lambda i, be, nu: (be[i], 0, 0)),
                  pl.BlockSpec((1, D_FF, d), lambda i, be, nu: (be[i], 0, 0)),
                  pl.BlockSpec((1, 1, d), lambda i, be, nu: (be[i], 0, 0))],
        out_specs=pl.BlockSpec((MOE_ROWS, half), lambda i, be, nu: (i, 0)),
        scratch_shapes=[pltpu.VMEM((d, 2 * D_FF), BF16), pltpu.VMEM((D_FF, d), BF16)],
    )
    ys = pl.pallas_call(
        _ffn_kernel,
        grid_spec=grid_spec,
        out_shape=jax.ShapeDtypeStruct((n_blocks * MOE_ROWS, half), jnp.int32),
        compiler_params=_cparams(("arbitrary",)),
        name="moe_ffn",
    )(block_e, n_used, xs, w1, b1.reshape(N_EXPERTS, 1, -1), w2, b2.reshape(N_EXPERTS, 1, -1))
    return [_sc_gather_rows(ys, pos.reshape(-1)) for pos in positions]


def _final_kernel(x1_ref, y0_ref, y1_ref, y2_ref, y3_ref, gate_ref, mod_ref, g_ref, *rest):
    o_ref = rest[-1]
    gates = gate_ref[0]
    lo = hi = None
    for k, y_ref in enumerate((y0_ref, y1_ref, y2_ref, y3_ref)):
        y_lo, y_hi = _unpack_bf16_pairs(y_ref[0, 0])
        gk = gates[:, k:k + 1]
        lo = gk * y_lo if k == 0 else lo + gk * y_lo
        hi = gk * y_hi if k == 0 else hi + gk * y_hi
    y = jnp.concatenate([lo, hi], axis=-1)
    gate2 = mod_ref[0, 5:6, :]
    o_ref[0] = x1_ref[0] + gate2 * _rms(y, g_ref[...])


def _final(x1, yg, gate, mod, mod_row, g_post_ffn, tm, b0, n_batch, out_prev):
    bsz, seq, d = x1.shape
    tok3 = lambda b, i: (b, i, 0)
    yg = yg.reshape(TOP_K, bsz, seq, d // 2)
    slot_specs = [pl.BlockSpec((1, 1, tm, d // 2), functools.partial(lambda b, i, k: (k, b, i, 0), k=k))
                  for k in range(TOP_K)]
    in_specs = ([pl.BlockSpec((1, tm, d), tok3)] + slot_specs +
                [pl.BlockSpec((1, tm, LANES), tok3),
                 pl.BlockSpec((1, 6, d), lambda b, i: (mod_row(b), 0, 0)), _const_spec((1, d))])
    args = [x1, yg, yg, yg, yg, gate, mod, g_post_ffn.astype(F32).reshape(1, d)]
    aliases = {}
    if out_prev is not None:
        in_specs.append(pl.BlockSpec(memory_space=pl.ANY))
        args.append(out_prev)
        aliases = {len(args) - 1: 0}
    return pl.pallas_call(
        _final_kernel,
        grid=(bsz, seq // tm),
        in_specs=in_specs,
        out_specs=pl.BlockSpec((1, tm, d), lambda b, i: (b + b0, i, 0)),
        out_shape=jax.ShapeDtypeStruct((n_batch, seq, d), F32),
        input_output_aliases=aliases,
        compiler_params=_cparams(("parallel", "parallel")),
        name="final_residual",
    )(*args)


def _stream(x, mod, mod_row, p, ctx_kv, init_state, rope, tm, b0, bsz):
    _, seq, d = x.shape
    is_ctx = ctx_kv is None
    pj = _project(x, mod, p["g_pre_mix"], p["w_in"], p["b_gates"], mod_row, rope, is_ctx, tm, b0, bsz)
    if is_ctx:
        att = _attention(pj["q"], None, None, pj["kd"], pj["vd"], p["sink"], window=False)
    else:
        att = _attention(pj["q"], pj["kd"], pj["vd"], ctx_kv[0], ctx_kv[1], p["sink"], window=True)
    ml = _mlstm(pj["mq"], pj["mk"], pj["mv"], pj["gates"], init_state, emit_state=is_ctx, b0=b0)
    x1, h2, idx, gate, tile_counts = _merge(att, ml[0], ml[1], pj, x, mod, mod_row, p, tm, b0)
    moe_in = (h2.reshape(bsz * seq, d // 2), idx, tile_counts)
    return x1, gate, moe_in, pj, ml


def kernel(x_prompt, x_sample, c, cache_k, cache_v, state_C, state_n, state_m, c_ctx, w_ada, b_ada, g_pre_mix,
           w_in, b_gates, attn_sink, norm_mlstm, w_up_att, w_up_ml, w_out, g_post_mix, g_pre_ffn, w_router,
           b_router, w1, b1, w2, b2, g_post_ffn):
    depth = w_ada.shape[0]
    n_dec = c.shape[0]
    cond = jnp.concatenate([c_ctx[None, :], c], axis=0).astype(F32)
    cond = jnp.pad(cond, ((0, 16 - cond.shape[0]), (0, 0)))
    y_prompt, y_sample = x_prompt, x_sample
    ks_, vs_, cs_, ns_, ms_ = [], [], [], [], []
    for l in range(depth):
        p = dict(g_pre_mix=g_pre_mix[l], w_in=w_in[l], b_gates=b_gates[l], sink=attn_sink[l], norm_ml=norm_mlstm[l],
                 w_up_att=w_up_att[l], w_up_ml=w_up_ml[l], w_out=w_out[l], g_post_mix=g_post_mix[l],
                 g_pre_ffn=g_pre_ffn[l], w_router=w_router[l], b_router=b_router[l], w1=w1[l], b1=b1[l],
                 w2=w2[l], b2=b2[l], g_post_ffn=g_post_ffn[l])
        mod = _adaln(cond, w_ada[l], b_ada[l]).reshape(16, 6, D_MODEL)
        row_p = lambda b: 0
        bsz, seq = x_prompt.shape[:2]
        x1_p, gate_p, moe_p, pj, ml = _stream(y_prompt, mod, row_p, p, None, None, False, 256, 0, bsz)
        nat = pj["nat"]
        ks_.append(nat[..., :KV_W].reshape(bsz, seq, ATT_KV_HEADS, HEAD_DIM))
        vs_.append(nat[..., KV_W:].reshape(bsz, seq, ATT_KV_HEADS, HEAD_DIM))
        cs_.append(ml[2])
        ns_.append(ml[3])
        ms_.append(ml[4][..., 0])
        ctx_kv = (_dup_heads(cache_k[:, l], False), _dup_heads(cache_v[:, l], True))
        init = (state_C[:, l], state_n[:, l], state_m[:, l])
        half = n_dec // 2
        experts = (p["w1"], p["b1"].astype(F32), p["w2"], p["b2"].astype(F32))
        groups = []
        for b0 in (0, half):
            row_s = functools.partial(lambda b, off: b + off + 1, off=b0)
            sl = slice(b0, b0 + half)
            x1_s, gate_s, moe_s, _, _ = _stream(y_sample, mod, row_s, p, (ctx_kv[0][sl], ctx_kv[1][sl]),
                                                init, True, 512, b0, half)
            groups.append((x1_s, gate_s, moe_s, row_s))
        (x1_a, gate_a, moe_a, row_a), (x1_b, gate_b, moe_b, row_b) = groups
        yg_p, yg_a = _moe([moe_p[0], moe_a[0]], [moe_p[1], moe_a[1]], [moe_p[2], moe_a[2]], *experts)
        (yg_b,) = _moe([moe_b[0]], [moe_b[1]], [moe_b[2]], *experts)
        y_prompt = _final(x1_p, yg_p, gate_p, mod, row_p, p["g_post_ffn"], 256, 0, bsz, None)
        y_sample = _final(x1_a, yg_a, gate_a, mod, row_a, p["g_post_ffn"], 512, 0, n_dec, None)
        y_sample = _final(x1_b, yg_b, gate_b, mod, row_b, p["g_post_ffn"], 512, half, n_dec, y_sample)
    return (y_prompt, y_sample, jnp.stack(ks_, axis=1), jnp.stack(vs_, axis=1), jnp.stack(cs_, axis=1),
            jnp.stack(ns_, axis=1), jnp.stack(ms_, axis=1))
```

```python
import functools

import numpy as np
import jax
import jax.numpy as jnp
from jax import lax
from jax.experimental import pallas as pl
from jax.experimental.pallas import tpu as pltpu
from jax.experimental.pallas import tpu_sc as plsc

F32 = jnp.float32
BF16 = jnp.bfloat16

D_MODEL = 1024
GRID_W = 64
ATT_HEADS = 8
ATT_KV_HEADS = 2
ATT_GROUP = ATT_HEADS // ATT_KV_HEADS
HEAD_DIM = 64
BLOCK = 128
ROPE_THETA = 10000.0
AXIS_FREQS = HEAD_DIM // 4
ML_HEADS = 4
ML_DK = 128
ML_DV = 256
ML_CHUNK = 512
N_EXPERTS = 32
TOP_K = 4
D_FF = 1024
SWIGLU_ALPHA = 1.702
SWIGLU_LIMIT = 7.0
EPS = 1e-6
NEG = -1e30

Q_W = ATT_HEADS * HEAD_DIM
KV_W = ATT_KV_HEADS * HEAD_DIM
MLQK_W = ML_HEADS * ML_DK
MLV_W = ML_HEADS * ML_DV
GATE_W = 4 * ML_HEADS

LANES = 128
VMEM_LIMIT = 56 * 1024 * 1024
MOE_ROWS = 512
SC_CORES = 2
SC_SUBCORES = 16
SC_GATHER_CHUNK = 32


def _cparams(sem):
    return pltpu.CompilerParams(dimension_semantics=sem, vmem_limit_bytes=VMEM_LIMIT)


def _const_spec(shape):
    nd = len(shape)
    return pl.BlockSpec(shape, lambda *_: (0,) * nd)


def _adaln_kernel(cond_ref, w_ref, b_ref, o_ref):
    cnd = cond_ref[...]
    act = cnd * jax.nn.sigmoid(cnd)
    o_ref[...] = jnp.dot(act.astype(BF16), w_ref[...].astype(BF16),
                         preferred_element_type=F32) + b_ref[...]


def _adaln(cond, w_ada, b_ada):
    rows, d = cond.shape
    n = w_ada.shape[1]
    tn = 1536
    return pl.pallas_call(
        _adaln_kernel,
        grid=(n // tn,),
        in_specs=[_const_spec((rows, d)),
                  pl.BlockSpec((d, tn), lambda j: (0, j)),
                  pl.BlockSpec((1, tn), lambda j: (0, j))],
        out_specs=pl.BlockSpec((rows, tn), lambda j: (0, j)),
        out_shape=jax.ShapeDtypeStruct((rows, n), F32),
        compiler_params=_cparams(("parallel",)),
        name="adaln",
    )(cond, w_ada, b_ada.reshape(1, n))


def _pack_w_in(w_in, b_gates, rope, with_nat):
    o_q, o_k, o_v = 0, Q_W, Q_W + KV_W
    o_mq = Q_W + 2 * KV_W
    o_mk = o_mq + MLQK_W
    o_mv = o_mk + MLQK_W
    o_g = o_mv + MLV_W
    o_mo = o_g + GATE_W
    o_ga = o_mo + MLV_W
    o_gm = o_ga + D_MODEL
    half = HEAD_DIM // 2

    def head_cols(base, h, rot):
        lo = base + h * HEAD_DIM
        if rot:
            return [w_in[:, lo + half:lo + HEAD_DIM], w_in[:, lo:lo + half]]
        return [w_in[:, lo:lo + HEAD_DIM]]

    def q_cols(rot):
        cols = []
        for h in range(ATT_HEADS):
            cols += head_cols(o_q, h, rot)
        return cols

    def dup_cols(base, rot):
        cols = []
        for h in range(ATT_KV_HEADS):
            hc = head_cols(base, h, rot)
            cols += hc + hc
        return cols

    segs, cols, off = {}, [], 0

    def add(name, cl):
        nonlocal off
        width = sum(c.shape[1] for c in cl)
        segs[name] = (off, off + width)
        cols.extend(cl)
        off += width

    add("q", q_cols(False))
    if rope:
        add("qrot", q_cols(True))
    add("kd", dup_cols(o_k, False))
    if rope:
        add("kdrot", dup_cols(o_k, True))
    if with_nat:
        add("nat", [w_in[:, o_k:o_k + 2 * KV_W]])
    add("mq", [w_in[:, o_mq:o_mq + MLQK_W]])
    add("mv", [w_in[:, o_mv:o_mv + MLV_W]])
    add("mo", [w_in[:, o_mo:o_mo + MLV_W]])
    add("ga", [w_in[:, o_ga:o_ga + D_MODEL]])
    add("gm", [w_in[:, o_gm:o_gm + D_MODEL]])
    add("gates", [w_in[:, o_g:o_g + GATE_W], jnp.zeros((D_MODEL, LANES - GATE_W), w_in.dtype)])
    wp = jnp.concatenate(cols, axis=1).astype(BF16)
    wvd_t = jnp.concatenate(dup_cols(o_v, False) + [w_in[:, o_mk:o_mk + MLQK_W]], axis=1).T.astype(BF16)
    bg = jnp.pad(b_gates.astype(F32), (0, LANES - GATE_W)).reshape(1, LANES)
    return wp, wvd_t, bg, segs


def _rope_tables(n_tok):
    rows = n_tok // GRID_W
    row = np.repeat(np.arange(rows), GRID_W).astype(np.float32)
    col = np.tile(np.arange(GRID_W), rows).astype(np.float32)
    inv = (np.float32(ROPE_THETA) ** (-np.arange(AXIS_FREQS, dtype=np.float32) / AXIS_FREQS)).astype(np.float32)
    ang = np.concatenate([row[:, None] * inv, col[:, None] * inv], axis=-1).astype(np.float32)
    cos, sin = np.cos(ang), np.sin(ang)
    c64 = np.concatenate([cos, cos], axis=-1)
    s64 = np.concatenate([-sin, sin], axis=-1)
    return (jnp.asarray(np.tile(c64, (1, 2)), F32), jnp.asarray(np.tile(s64, (1, 2)), F32))


def _log_sigmoid(x):
    return jnp.minimum(x, 0.0) - jnp.log1p(jnp.exp(-jnp.abs(x)))


def _prefix_sum_rows(x):
    row = lax.broadcasted_iota(jnp.int32, x.shape, 0)
    k = 1
    while k < x.shape[0]:
        x = x + jnp.where(row >= k, pltpu.roll(x, k, 0), 0.0)
        k *= 2
    return x


def _proj_kernel(*refs, segs, rope, with_nat):
    it = iter(refs)
    x_ref, mod_ref, g_ref, w_ref, wvt_ref, bg_ref = (next(it) for _ in range(6))
    cos_ref = sin_ref = None
    if rope:
        cos_ref, sin_ref = next(it), next(it)
    q_ref, kd_ref, vd_ref = next(it), next(it), next(it)
    nat_ref = next(it) if with_nat else None
    mq_ref, mk_ref, mv_ref, gates_ref, gates_t_ref, mo_ref, ga_ref, gm_ref = (next(it) for _ in range(8))

    x = x_ref[0]
    shift = mod_ref[0, 0:1, :]
    scale = mod_ref[0, 1:2, :]
    ms = jnp.mean(x * x, axis=-1, keepdims=True)
    xn = x * lax.rsqrt(ms + EPS) * g_ref[...]
    xb = (xn * (1.0 + scale) + shift).astype(BF16)

    def seg(name):
        lo, hi = segs[name]
        return jnp.dot(xb, w_ref[:, lo:hi], preferred_element_type=F32)

    uq = seg("q")
    ukd = seg("kd")
    if rope:
        uqr = seg("qrot")
        ukr = seg("kdrot")
        cs = cos_ref[...]
        sn = sin_ref[...]
        for h in range(ATT_HEADS // 2):
            sl = slice(h * LANES, (h + 1) * LANES)
            q_ref[0, h] = ((uq[:, sl] * cs + uqr[:, sl] * sn) * HEAD_DIM ** -0.5).astype(BF16)
        for h in range(ATT_KV_HEADS):
            sl = slice(h * LANES, (h + 1) * LANES)
            kd_ref[0, :, sl] = (ukd[:, sl] * cs + ukr[:, sl] * sn).astype(BF16)
    else:
        for h in range(ATT_HEADS // 2):
            sl = slice(h * LANES, (h + 1) * LANES)
            q_ref[0, h] = (uq[:, sl] * HEAD_DIM ** -0.5).astype(BF16)
        kd_ref[0] = ukd.astype(BF16)
    ut = lax.dot_general(wvt_ref[...], xb, (((1,), (1,)), ((), ())), preferred_element_type=F32)
    vd_ref[0] = ut[:2 * LANES].astype(BF16)
    mk_ref[0] = ut[2 * LANES:].astype(BF16)
    if with_nat:
        nat_ref[0] = seg("nat")
    mq_ref[0] = (seg("mq") * ML_DK ** -0.5).astype(BF16)
    mv_ref[0] = seg("mv").astype(BF16)
    gates = seg("gates") + bg_ref[...]
    lf = _log_sigmoid(gates)
    csum = _prefix_sum_rows(lf)
    lane = lax.broadcasted_iota(jnp.int32, gates.shape, 1)
    cum = jnp.where(lane >= 2 * ML_HEADS, csum[-1:, :] - csum + lf, csum)
    gates_ref[0] = cum
    gates_t_ref[0, :GATE_W, :] = gates.T[:GATE_W, :]
    gates_t_ref[0, GATE_W:, :] = cum.T[:GATE_W, :]
    mo_ref[0] = _sigmoid(seg("mo")).astype(BF16)
    ga_ref[0] = _sigmoid(seg("ga")).astype(BF16)
    gm_ref[0] = _sigmoid(seg("gm")).astype(BF16)


def _project(x, mod, g_pre, w_in, b_gates, mod_row, rope, with_nat, tm, b0, bsz):
    _, seq, d = x.shape
    wp, wvd_t, bg, segs = _pack_w_in(w_in, b_gates, rope, with_nat)
    nw = wp.shape[1]
    tok3 = lambda b, i: (b, i, 0)
    in_specs = [pl.BlockSpec((1, tm, d), lambda b, i: (b + b0, i, 0)),
                pl.BlockSpec((1, 6, d), lambda b, i: (mod_row(b), 0, 0)),
                _const_spec((1, d)),
                pl.BlockSpec((d, nw), lambda b, i: (0, 0), pipeline_mode=pl.Buffered(1)),
                _const_spec((2 * LANES + MLQK_W, d)), _const_spec((1, LANES))]
    args = [x, mod, g_pre.reshape(1, d), wp, wvd_t, bg]
    if rope:
        cos_t, sin_t = _rope_tables(seq)
        in_specs += [pl.BlockSpec((tm, LANES), lambda b, i: (i, 0))] * 2
        args += [cos_t, sin_t]

    def tok_out(width, dtype):
        return (jax.ShapeDtypeStruct((bsz, seq, width), dtype), pl.BlockSpec((1, tm, width), tok3))

    outs = [(jax.ShapeDtypeStruct((bsz, ATT_HEADS // 2, seq, LANES), BF16),
             pl.BlockSpec((1, ATT_HEADS // 2, tm, LANES), lambda b, i: (b, 0, i, 0))),
            tok_out(2 * LANES, BF16),
            (jax.ShapeDtypeStruct((bsz, 2 * LANES, seq), BF16),
             pl.BlockSpec((1, 2 * LANES, tm), lambda b, i: (b, 0, i)))]
    if with_nat:
        outs.append(tok_out(2 * KV_W, F32))
    outs += [tok_out(MLQK_W, BF16),
             (jax.ShapeDtypeStruct((bsz, MLQK_W, seq), BF16),
              pl.BlockSpec((1, MLQK_W, tm), lambda b, i: (b, 0, i))),
             tok_out(MLV_W, BF16), tok_out(LANES, F32),
             (jax.ShapeDtypeStruct((bsz, 2 * GATE_W, seq), F32),
              pl.BlockSpec((1, 2 * GATE_W, tm), lambda b, i: (b, 0, i))),
             tok_out(MLV_W, BF16), tok_out(D_MODEL, BF16), tok_out(D_MODEL, BF16)]
    res = pl.pallas_call(
        functools.partial(_proj_kernel, segs=segs, rope=rope, with_nat=with_nat),
        grid=(bsz, seq // tm),
        in_specs=in_specs,
        out_specs=[o[1] for o in outs],
        out_shape=[o[0] for o in outs],
        compiler_params=_cparams(("parallel", "parallel")),
        name="proj_rope" if rope else "proj_ctx",
    )(*args)
    names = ["q", "kd", "vd"] + (["nat"] if with_nat else []) + ["mq", "mk", "mv", "gates", "gates_t", "mo", "ga", "gm"]
    return dict(zip(names, res))


def _attn_kernel(*refs, window, n_blocks, n_ctx):
    it = iter(refs)
    sink_ref, q_ref = next(it), next(it)
    if window:
        kp_ref, kc_ref, kn_ref, vp_ref, vc_ref, vn_ref = (next(it) for _ in range(6))
    ck_ref, cv_ref, o_ref = next(it), next(it), next(it)
    j = pl.program_id(1)
    n_win = 3 * BLOCK if window else 0
    cols = ATT_GROUP * BLOCK

    col_id = lax.broadcasted_iota(jnp.int32, (1, cols), 1)
    if window:
        kl = lax.broadcasted_iota(jnp.int32, (BLOCK, cols), 0)
        ql = lax.broadcasted_iota(jnp.int32, (BLOCK, cols), 1) & (BLOCK - 1)
        prev_ok = kl >= ql + jnp.where(j == 0, BLOCK, 0)
        next_ok = kl <= ql - jnp.where(j == n_blocks - 1, BLOCK, 0)
    lane_lo = lax.broadcasted_iota(jnp.int32, (BLOCK, LANES), 1) < HEAD_DIM
    lane_row = lax.broadcasted_iota(jnp.int32, (1, LANES), 1) < HEAD_DIM
    half_lo = jnp.where(lane_row, 1.0, 0.0).astype(BF16)
    half_hi = jnp.where(lane_row, 0.0, 1.0).astype(BF16)

    for kvh in range(ATT_KV_HEADS):
        sl = slice(kvh * LANES, (kvh + 1) * LANES)
        if window:
            keys = jnp.concatenate([kp_ref[0, :, sl], kc_ref[0, :, sl], kn_ref[0, :, sl], ck_ref[0, :, sl]], axis=0)
            vals_t = jnp.concatenate([vp_ref[0, sl, :], vc_ref[0, sl, :], vn_ref[0, sl, :], cv_ref[0, sl, :]], axis=1)
        else:
            keys = ck_ref[0, :, sl]
            vals_t = cv_ref[0, sl, :]
        qs = jnp.concatenate([q_ref[0, kvh * (ATT_GROUP // 2) + g // 2] * (half_lo if g % 2 == 0 else half_hi)
                              for g in range(ATT_GROUP)], axis=0)
        s = lax.dot_general(keys, qs, (((1,), (1,)), ((), ())), preferred_element_type=F32)
        if window:
            s = jnp.concatenate([jnp.where(prev_ok, s[:BLOCK], NEG), s[BLOCK:2 * BLOCK],
                                 jnp.where(next_ok, s[2 * BLOCK:n_win], NEG), s[n_win:]], axis=0)
        snk = jnp.full((1, cols), sink_ref[kvh * ATT_GROUP + ATT_GROUP - 1], F32)
        for g in range(ATT_GROUP - 2, -1, -1):
            snk = jnp.where(col_id < (g + 1) * BLOCK, sink_ref[kvh * ATT_GROUP + g], snk)
        m = jnp.maximum(jnp.max(s, axis=0, keepdims=True), snk)
        p = jnp.exp(s - m)
        den = jnp.sum(p, axis=0, keepdims=True) + jnp.exp(snk - m)
        o_t = jnp.dot(vals_t, p.astype(BF16), preferred_element_type=F32) / den
        for pair in range(ATT_GROUP // 2):
            even = o_t[:, (2 * pair) * BLOCK:(2 * pair + 1) * BLOCK].T
            odd = o_t[:, (2 * pair + 1) * BLOCK:(2 * pair + 2) * BLOCK].T
            col = (kvh * (ATT_GROUP // 2) + pair) * LANES
            o_ref[0, :, col:col + LANES] = jnp.where(lane_lo, even, odd).astype(BF16)


def _attention(q, kd, vd_t, ckd, cvd_t, sink, window):
    bsz, _, seq, _ = q.shape
    nb = seq // BLOCK
    n_ctx = ckd.shape[1]
    in_specs = [pl.BlockSpec(memory_space=pltpu.SMEM),
                pl.BlockSpec((1, ATT_HEADS // 2, BLOCK, LANES), lambda b, j: (b, 0, j, 0))]
    args = [sink.astype(F32), q]
    if window:
        in_specs += [pl.BlockSpec((1, BLOCK, 2 * LANES), lambda b, j: (b, jnp.maximum(j - 1, 0), 0)),
                     pl.BlockSpec((1, BLOCK, 2 * LANES), lambda b, j: (b, j, 0)),
                     pl.BlockSpec((1, BLOCK, 2 * LANES), lambda b, j: (b, jnp.minimum(j + 1, nb - 1), 0)),
                     pl.BlockSpec((1, 2 * LANES, BLOCK), lambda b, j: (b, 0, jnp.maximum(j - 1, 0))),
                     pl.BlockSpec((1, 2 * LANES, BLOCK), lambda b, j: (b, 0, j)),
                     pl.BlockSpec((1, 2 * LANES, BLOCK), lambda b, j: (b, 0, jnp.minimum(j + 1, nb - 1)))]
        args += [kd, kd, kd, vd_t, vd_t, vd_t]
    in_specs += [pl.BlockSpec((1, n_ctx, 2 * LANES), lambda b, j: (b, 0, 0)),
                 pl.BlockSpec((1, 2 * LANES, n_ctx), lambda b, j: (b, 0, 0))]
    args += [ckd, cvd_t]
    return pl.pallas_call(
        functools.partial(_attn_kernel, window=window, n_blocks=nb, n_ctx=n_ctx),
        grid=(bsz, nb),
        in_specs=in_specs,
        out_specs=pl.BlockSpec((1, BLOCK, Q_W), lambda b, j: (b, j, 0)),
        out_shape=jax.ShapeDtypeStruct((bsz, seq, Q_W), BF16),
        compiler_params=_cparams(("parallel", "parallel")),
        name="attn_window" if window else "attn_ctx",
    )(*args)


def _dup_heads(t, transpose):
    b, l = t.shape[:2]
    out = jnp.concatenate([t, t], axis=-1).reshape(b, l, ATT_KV_HEADS * LANES).astype(BF16)
    return jnp.swapaxes(out, 1, 2) if transpose else out


def _mlstm_kernel(*refs, chunk_len, n_chunks, has_init, emit_state):
    it = iter(refs)
    qkvg = [[next(it) for _ in range(5)] for _ in range(2)]
    if has_init:
        c0_ref, n0_ref, m0_ref = next(it), next(it), next(it)
    h_refs = [next(it), next(it)]
    if emit_state:
        co_ref, no_ref, mo_ref = next(it), next(it), next(it)
    st_s, m_s = next(it), next(it)
    t = pl.program_id(1)
    L = chunk_len

    @pl.when(t == 0)
    def _():
        for d in range(2):
            for h in range(ML_HEADS):
                if has_init:
                    st_s[d, h, :, :ML_DV] = c0_ref[0, d, h].T
                    st_s[d, h, :, ML_DV:] = jnp.broadcast_to(n0_ref[0, d, h:h + 1, :], (LANES, ML_DK)).T
                    m_s[d, h] = m0_ref[0, d, h:h + 1, :]
                else:
                    st_s[d, h] = jnp.zeros((ML_DK, ML_DV + LANES), F32)
                    m_s[d, h] = jnp.zeros((1, LANES), F32)

    ri = lax.broadcasted_iota(jnp.int32, (L, L), 0)
    ci = lax.broadcasted_iota(jnp.int32, (L, L), 1)
    ones_blk = jnp.ones((L, LANES), BF16)

    for d in range(2):
        q_ref, kt_ref, v_ref, cum_ref, rows_ref = qkvg[d]
        tri = (ci <= ri) if d == 0 else (ci >= ri)
        bc = cum_ref[0]
        g_t = rows_ref[0, :GATE_W, :]
        br = rows_ref[0, GATE_W:, :]
        last = L - 1 if d == 0 else 0

        for h in range(ML_HEADS):
            icol_i = 2 * ML_HEADS * d + h
            fcol = icol_i + ML_HEADS
            bcol = bc[:, fcol:fcol + 1]
            brow = br[fcol:fcol + 1, :]
            irow = g_t[icol_i:icol_i + 1, :]
            b_last = bc[last:last + 1, fcol:fcol + 1]
            m_prev = m_s[d, h][:, 0:1]
            st_prev = st_s[d, h]
            qh = q_ref[0, :, h * ML_DK:(h + 1) * ML_DK]
            kt = kt_ref[0, h * ML_DK:(h + 1) * ML_DK, :]
            v_aug = jnp.concatenate([v_ref[0, :, h * ML_DV:(h + 1) * ML_DV], ones_blk], axis=1)

            rel = jnp.where(tri, irow - brow, NEG)
            inter = bcol + m_prev
            m_t = jnp.maximum(inter, bcol + jnp.max(rel, axis=-1, keepdims=True))
            w = jnp.exp(rel + (bcol - m_t))
            w_inter = jnp.exp(inter - m_t)
            s = jnp.dot(qh, kt, preferred_element_type=F32) * w
            lhs = jnp.concatenate([s.astype(BF16), (qh.astype(F32) * w_inter).astype(BF16)], axis=1)
            rhs = jnp.concatenate([v_aug, st_prev.astype(BF16)], axis=0)
            mix = jnp.dot(lhs, rhs, preferred_element_type=F32)
            inv = 1.0 / jnp.maximum(jnp.abs(mix[:, ML_DV:]), jnp.exp(-m_t))
            for c in range(ML_DV // LANES):
                lo = h * ML_DV + c * LANES
                h_refs[d][0, :, lo:lo + LANES] = mix[:, c * LANES:(c + 1) * LANES] * inv

            a = b_last - brow + irow
            m_new = jnp.maximum(b_last + m_prev, jnp.max(a, axis=-1, keepdims=True))
            wk = jnp.exp(a - m_new)
            decay = jnp.exp(b_last + m_prev - m_new)
            kw_t = (kt.astype(F32) * wk).astype(BF16)
            st_new = decay * st_prev + jnp.dot(kw_t, v_aug, preferred_element_type=F32)
            st_s[d, h] = st_new
            m_s[d, h] = jnp.broadcast_to(m_new, (1, LANES))
            if emit_state:
                @pl.when(t == n_chunks - 1)
                def _():
                    co_ref[0, d, h] = st_new[:, :ML_DV].T
                    no_ref[0, d, h:h + 1, :] = st_new[:, ML_DV:].T[0:1, :]
                    mo_ref[0, d, h:h + 1, :] = jnp.broadcast_to(m_new, (1, LANES))


def _mlstm(mq, mk_t, mv, gate_cum, gate_rows, init, emit_state, b0=0):
    bsz, seq, _ = mq.shape
    chunk_len = min(ML_CHUNK, seq)
    nc = seq // chunk_len
    fwd = lambda b, t: (b, t, 0)
    bwd = lambda b, t: (b, nc - 1 - t, 0)
    fwd_t = lambda b, t: (b, 0, t)
    bwd_t = lambda b, t: (b, 0, nc - 1 - t)
    state5 = lambda b, t: (b, 0, 0, 0, 0)
    state4 = lambda b, t: (b, 0, 0, 0)
    init5 = lambda b, t: (b + b0, 0, 0, 0, 0)
    init4 = lambda b, t: (b + b0, 0, 0, 0)
    in_specs, args = [], []
    for chunk, chunk_t in ((fwd, fwd_t), (bwd, bwd_t)):
        in_specs += [pl.BlockSpec((1, chunk_len, MLQK_W), chunk), pl.BlockSpec((1, MLQK_W, chunk_len), chunk_t),
                     pl.BlockSpec((1, chunk_len, MLV_W), chunk), pl.BlockSpec((1, chunk_len, LANES), chunk),
                     pl.BlockSpec((1, 2 * GATE_W, chunk_len), chunk_t)]
        args += [mq, mk_t, mv, gate_cum, gate_rows]
    if init is not None:
        c0, n0, m0 = init
        in_specs += [pl.BlockSpec((1, 2, ML_HEADS, ML_DV, ML_DK), init5),
                     pl.BlockSpec((1, 2, ML_HEADS, ML_DK), init4),
                     pl.BlockSpec((1, 2, ML_HEADS, LANES), init4)]
        args += [c0.astype(F32), n0.astype(F32),
                 jnp.broadcast_to(m0.astype(F32)[..., None], m0.shape + (LANES,))]
    out_shape = [jax.ShapeDtypeStruct((bsz, seq, MLV_W), F32)] * 2
    out_specs = [pl.BlockSpec((1, chunk_len, MLV_W), fwd), pl.BlockSpec((1, chunk_len, MLV_W), bwd)]
    if emit_state:
        out_shape += [jax.ShapeDtypeStruct((bsz, 2, ML_HEADS, ML_DV, ML_DK), F32),
                      jax.ShapeDtypeStruct((bsz, 2, ML_HEADS, ML_DK), F32),
                      jax.ShapeDtypeStruct((bsz, 2, ML_HEADS, LANES), F32)]
        out_specs += [pl.BlockSpec((1, 2, ML_HEADS, ML_DV, ML_DK), state5),
                      pl.BlockSpec((1, 2, ML_HEADS, ML_DK), state4),
                      pl.BlockSpec((1, 2, ML_HEADS, LANES), state4)]
    return pl.pallas_call(
        functools.partial(_mlstm_kernel, chunk_len=chunk_len, n_chunks=nc, has_init=init is not None, emit_state=emit_state),
        grid=(bsz, nc),
        in_specs=in_specs,
        out_specs=out_specs,
        out_shape=out_shape,
        scratch_shapes=[pltpu.VMEM((2, ML_HEADS, ML_DK, ML_DV + LANES), F32),
                        pltpu.VMEM((2, ML_HEADS, 1, LANES), F32)],
        compiler_params=_cparams(("parallel", "arbitrary")),
        name="mlstm_state" if emit_state else "mlstm",
    )(*args)


def _sigmoid(x):
    return 0.5 * jnp.tanh(0.5 * x) + 0.5


def _rms(x, g):
    return x * lax.rsqrt(jnp.mean(x * x, axis=-1, keepdims=True) + EPS) * g


def _pack_bf16_pairs(x):
    n = x.shape[1] // 2
    lo = pltpu.bitcast(x[:, :n].astype(BF16).astype(F32), jnp.int32)
    hi = pltpu.bitcast(x[:, n:].astype(BF16).astype(F32), jnp.int32)
    return lax.shift_right_logical(lo, 16) | hi


def _unpack_bf16_pairs(p):
    lo = pltpu.bitcast(lax.shift_left(p, 16), F32)
    hi = pltpu.bitcast(p & jnp.int32(-65536), F32)
    return lo, hi


def _merge_kernel(att_ref, hf_ref, hb_ref, mo_ref, ga_ref, gm_ref, x_ref, mod_ref, nml_ref, gpm_ref, gpf_ref,
                  wua_ref, wum_ref, wo_ref, wr_ref, br_ref, x1_ref, h2_ref, idx_ref, gate_ref, cnt_ref):
    hsum = hf_ref[0] + hb_ref[0]
    parts = []
    for h in range(ML_HEADS):
        hh = hsum[:, h * ML_DV:(h + 1) * ML_DV]
        parts.append(hh * lax.rsqrt(jnp.mean(hh * hh, axis=-1, keepdims=True) + EPS))
    hn = jnp.concatenate(parts, axis=-1) * nml_ref[...]
    ml = (hn * mo_ref[0].astype(F32)).astype(BF16)
    a = jnp.dot(att_ref[0], wua_ref[...], preferred_element_type=F32)
    m = jnp.dot(ml, wum_ref[...], preferred_element_type=F32)
    z = (ga_ref[0].astype(F32) * a + gm_ref[0].astype(F32) * m).astype(BF16)
    mix = jnp.dot(z, wo_ref[...], preferred_element_type=F32)
    gate1 = mod_ref[0, 2:3, :]
    shift2 = mod_ref[0, 3:4, :]
    scale2 = mod_ref[0, 4:5, :]
    x1 = x_ref[0] + gate1 * _rms(mix, gpm_ref[...])
    x1_ref[0] = x1
    h2 = _rms(x1, gpf_ref[...]) * (1.0 + scale2) + shift2
    h2_ref[0] = _pack_bf16_pairs(h2)

    h2_hi = h2.astype(BF16)
    h2_lo = (h2 - h2_hi.astype(F32)).astype(BF16)
    wr = wr_ref[...]
    wr_hi = wr.astype(BF16)
    wr_lo = (wr - wr_hi.astype(F32)).astype(BF16)
    hi_terms = jnp.dot(h2_hi, jnp.concatenate([wr_hi, wr_lo], axis=1), preferred_element_type=F32)
    logits = (hi_terms[:, :LANES] + hi_terms[:, LANES:]
              + jnp.dot(h2_lo, wr_hi, preferred_element_type=F32)) + br_ref[...]
    lane = lax.broadcasted_iota(jnp.int32, logits.shape, 1).astype(F32)
    work = jnp.where(lane < N_EXPERTS, logits, -jnp.inf)
    idx_out = jnp.zeros(logits.shape, F32)
    val_out = jnp.zeros(logits.shape, F32)
    picked = jnp.zeros(logits.shape, F32)
    top0 = None
    esum = None
    for k in range(TOP_K):
        mx = jnp.max(work, axis=-1, keepdims=True)
        sel = jnp.min(jnp.where(work == mx, lane, float(LANES)), axis=-1, keepdims=True)
        if k == 0:
            top0 = mx
        e = jnp.exp(mx - top0)
        esum = e if k == 0 else esum + e
        idx_out = jnp.where(lane == k, sel, idx_out)
        val_out = jnp.where(lane == k, e, val_out)
        picked = jnp.where(lane == sel, 1.0, picked)
        work = jnp.where(lane == sel, -jnp.inf, work)
    idx_ref[0] = idx_out.astype(jnp.int32)
    gate_ref[0] = val_out / esum
    cnt_ref[0] = jnp.sum(picked, axis=0, keepdims=True)


def _merge(att, hf, hb, pj, x, mod, mod_row, p, tm, b0):
    bsz = att.shape[0]
    _, seq, d = x.shape
    tok3 = lambda b, i: (b, i, 0)
    wr = jnp.pad(p["w_router"].astype(F32), ((0, 0), (0, LANES - N_EXPERTS)))
    br = jnp.pad(p["b_router"].astype(F32), (0, LANES - N_EXPERTS)).reshape(1, LANES)
    row = lambda v: v.astype(F32).reshape(1, -1)
    in_specs = [pl.BlockSpec((1, tm, Q_W), tok3),
                pl.BlockSpec((1, tm, MLV_W), tok3), pl.BlockSpec((1, tm, MLV_W), tok3),
                pl.BlockSpec((1, tm, MLV_W), tok3), pl.BlockSpec((1, tm, d), tok3), pl.BlockSpec((1, tm, d), tok3),
                pl.BlockSpec((1, tm, d), lambda b, i: (b + b0, i, 0)),
                pl.BlockSpec((1, 6, d), lambda b, i: (mod_row(b), 0, 0)),
                _const_spec((1, MLV_W)), _const_spec((1, d)), _const_spec((1, d)),
                _const_spec((Q_W, d)), _const_spec((MLV_W, d)), _const_spec((d, d)),
                _const_spec((d, LANES)), _const_spec((1, LANES))]
    out_shape = [jax.ShapeDtypeStruct((bsz, seq, d), F32), jax.ShapeDtypeStruct((bsz, seq, d // 2), jnp.int32),
                 jax.ShapeDtypeStruct((bsz, seq, LANES), jnp.int32), jax.ShapeDtypeStruct((bsz, seq, LANES), F32),
                 jax.ShapeDtypeStruct((bsz * (seq // tm), 1, LANES), F32)]
    out_specs = [pl.BlockSpec((1, tm, d), tok3), pl.BlockSpec((1, tm, d // 2), tok3),
                 pl.BlockSpec((1, tm, LANES), tok3), pl.BlockSpec((1, tm, LANES), tok3),
                 pl.BlockSpec((1, 1, LANES), lambda b, i: (b * (seq // tm) + i, 0, 0))]
    return pl.pallas_call(
        _merge_kernel,
        grid=(bsz, seq // tm),
        in_specs=in_specs,
        out_specs=out_specs,
        out_shape=out_shape,
        compiler_params=_cparams(("parallel", "parallel")),
        name="merge_router",
    )(att, hf, hb, pj["mo"], pj["ga"], pj["gm"], x, mod, row(p["norm_ml"]), row(p["g_post_mix"]),
      row(p["g_pre_ffn"]), p["w_up_att"].astype(BF16), p["w_up_ml"].astype(BF16), p["w_out"].astype(BF16), wr, br)


def _sc_gather_rows(table, idx):
    n = idx.shape[0]
    width = table.shape[1]
    chunk = SC_GATHER_CHUNK
    n_workers = SC_CORES * SC_SUBCORES
    rows_per_worker = n // n_workers
    n_pairs = rows_per_worker // (2 * chunk)
    assert n_pairs * 2 * chunk * n_workers == n
    mesh = plsc.VectorSubcoreMesh(core_axis_name="c", subcore_axis_name="s")

    def body(table_hbm, idx_hbm, out_hbm, idx_v, rows_v, gsem, osem):
        wid = lax.axis_index("s") * SC_CORES + lax.axis_index("c")
        base0 = wid * rows_per_worker

        def gather_copy(b):
            return pltpu.make_async_copy(table_hbm.at[idx_v.at[b]], rows_v.at[b], gsem.at[b])

        def write_copy(ci, b):
            base = pl.multiple_of(base0 + ci * chunk, 8)
            return pltpu.make_async_copy(rows_v.at[b], out_hbm.at[pl.ds(base, chunk)], osem.at[b])

        def issue(ci, b):
            base = pl.multiple_of(base0 + ci * chunk, 8)
            pltpu.sync_copy(idx_hbm.at[pl.ds(base, chunk)], idx_v.at[b])
            gather_copy(b).start()

        def finish(ci, b):
            gather_copy(b).wait()
            write_copy(ci, b).start()

        issue(0, 0)

        @pl.loop(0, n_pairs)
        def _(j):
            @pl.when(j > 0)
            def _():
                write_copy(2 * j - 1, 1).wait()
            issue(2 * j + 1, 1)
            finish(2 * j, 0)

            @pl.when(j < n_pairs - 1)
            def _():
                write_copy(2 * j, 0).wait()
                issue(2 * j + 2, 0)
            finish(2 * j + 1, 1)

        write_copy(2 * n_pairs - 2, 0).wait()
        write_copy(2 * n_pairs - 1, 1).wait()

    return pl.kernel(
        body, mesh=mesh,
        out_type=jax.ShapeDtypeStruct((n, width), table.dtype),
        scratch_types=[pltpu.VMEM((2, chunk), jnp.int32), pltpu.VMEM((2, chunk, width), table.dtype),
                       pltpu.SemaphoreType.DMA((2,)), pltpu.SemaphoreType.DMA((2,))],
    )(table, idx)


def _ffn_kernel(be_ref, nu_ref, xs_ref, w1_ref, b1_ref, w2_ref, b2_ref, ys_ref, w1b_s, w2b_s):
    i = pl.program_id(0)
    e = be_ref[i]
    e_prev = be_ref[jnp.maximum(i - 1, 0)]

    @pl.when((i == 0) | (e != e_prev))
    def _():
        w1b_s[...] = w1_ref[0].astype(BF16)
        w2b_s[...] = w2_ref[0].astype(BF16)

    @pl.when(i < nu_ref[0])
    def _():
        x_lo, x_hi = _unpack_bf16_pairs(xs_ref[...])
        xb = jnp.concatenate([x_lo.astype(BF16), x_hi.astype(BF16)], axis=1)
        hmid = jnp.dot(xb, w1b_s[...], preferred_element_type=F32) + b1_ref[0]
        glu = jnp.minimum(hmid[:, :D_FF], SWIGLU_LIMIT)
        lin = jnp.clip(hmid[:, D_FF:], -SWIGLU_LIMIT, SWIGLU_LIMIT)
        act = ((lin + 1.0) * glu * _sigmoid(SWIGLU_ALPHA * glu)).astype(BF16)
        y = jnp.dot(act, w2b_s[...], preferred_element_type=F32) + b2_ref[0]
        ys_ref[...] = _pack_bf16_pairs(y)

    @pl.when(i >= nu_ref[0])
    def _():
        ys_ref[...] = jnp.zeros(ys_ref.shape, jnp.int32)


def _sc_scatter_rows(tables, positions, n_rows):
    width = tables[0].shape[1]
    chunk = SC_GATHER_CHUNK
    n_workers = SC_CORES * SC_SUBCORES
    n_streams = len(tables)
    mesh = plsc.VectorSubcoreMesh(core_axis_name="c", subcore_axis_name="s")

    def body(*refs):
        table_refs = refs[0:2 * n_streams:2]
        pos_refs = refs[1:2 * n_streams:2]
        out_hbm, idx_v, rows_v, lsem, ssem = refs[2 * n_streams:]
        wid = lax.axis_index("s") * SC_CORES + lax.axis_index("c")

        for table_hbm, pos_hbm in zip(table_refs, pos_refs):
            tok_per_worker = table_hbm.shape[0] // n_workers
            n_pairs = tok_per_worker // (2 * chunk)
            assert n_pairs * 2 * chunk * n_workers == table_hbm.shape[0]
            base0 = wid * tok_per_worker

            def load_copy(ci, b):
                base = pl.multiple_of(base0 + ci * chunk, 8)
                return pltpu.make_async_copy(table_hbm.at[pl.ds(base, chunk)], rows_v.at[b], lsem.at[b])

            def scatter_copy(b, k):
                return pltpu.make_async_copy(rows_v.at[b], out_hbm.at[idx_v.at[b, k]], ssem.at[b])

            def load(ci, b):
                base = pl.multiple_of(base0 + ci * chunk, 8)
                load_copy(ci, b).start()
                for k in range(TOP_K):
                    pltpu.sync_copy(pos_hbm.at[k, pl.ds(base, chunk)], idx_v.at[b, k])

            def scatter(ci, b):
                load_copy(ci, b).wait()
                for k in range(TOP_K):
                    scatter_copy(b, k).start()

            def drain(b):
                for k in range(TOP_K):
                    scatter_copy(b, k).wait()

            load(0, 0)

            @pl.loop(0, n_pairs)
            def _(j):
                @pl.when(j > 0)
                def _():
                    drain(1)
                load(2 * j + 1, 1)
                scatter(2 * j, 0)

                @pl.when(j < n_pairs - 1)
                def _():
                    drain(0)
                    load(2 * j + 2, 0)
                scatter(2 * j + 1, 1)

            drain(0)
            drain(1)

    args = [a for pair in zip(tables, positions) for a in pair]
    return pl.kernel(
        body, mesh=mesh,
        out_type=jax.ShapeDtypeStruct((n_rows, width), tables[0].dtype),
        scratch_types=[pltpu.VMEM((2, TOP_K, chunk), jnp.int32), pltpu.VMEM((2, chunk, width), tables[0].dtype),
                       pltpu.SemaphoreType.DMA((2,)), pltpu.SemaphoreType.DMA((2,))],
    )(*args)


def _pos_kernel(idx_ref, base_ref, pos_ref):
    idx = idx_ref[...]
    rows = idx.shape[0]
    lane = lax.broadcasted_iota(jnp.int32, idx.shape, 1)
    hots = [jnp.where(lane == idx[:, k:k + 1], 1.0, 0.0) for k in range(TOP_K)]
    cnt = hots[0] + hots[1] + hots[2] + hots[3]
    ri = lax.broadcasted_iota(jnp.int32, (rows, rows), 0)
    ci = lax.broadcasted_iota(jnp.int32, (rows, rows), 1)
    earlier = jnp.where(ci < ri, 1.0, 0.0).astype(BF16)
    ahead = jnp.dot(earlier, cnt.astype(BF16), preferred_element_type=F32) + base_ref[0]
    posmat = jnp.zeros(idx.shape, F32)
    for k in range(TOP_K):
        posmat = jnp.where(lane == k, jnp.sum(hots[k] * ahead, axis=-1, keepdims=True), posmat)
    pos_ref[...] = posmat.T[:TOP_K, :].astype(jnp.int32)


def _route(idx_list, count_list):
    tiles = [c.shape[0] for c in count_list]
    tile_tok = [idx.size // LANES // n for idx, n in zip(idx_list, tiles)]
    n_tok = sum(n * sz for n, sz in zip(tiles, tile_tok))
    n_blocks = n_tok * TOP_K // MOE_ROWS + N_EXPERTS
    cnt = jnp.concatenate([c.reshape(-1, LANES) for c in count_list], axis=0).astype(jnp.int32)
    tile_off = jnp.cumsum(cnt, axis=0) - cnt
    total = jnp.sum(cnt, axis=0)
    padded = (total + MOE_ROWS - 1) // MOE_ROWS * MOE_ROWS
    pend = jnp.cumsum(padded)
    base = ((pend - padded)[None, :] + tile_off).astype(F32)
    starts = jnp.arange(n_blocks, dtype=jnp.int32) * MOE_ROWS
    block_e = jnp.minimum(jnp.sum(pend[None, :N_EXPERTS] <= starts[:, None], axis=1), N_EXPERTS - 1).astype(jnp.int32)
    n_used = (pend[N_EXPERTS - 1] // MOE_ROWS).astype(jnp.int32).reshape(1)
    positions, first = [], 0
    for idx, n_tiles, tile in zip(idx_list, tiles, tile_tok):
        stream_tok = n_tiles * tile
        positions.append(pl.pallas_call(
            _pos_kernel,
            grid=(n_tiles,),
            in_specs=[pl.BlockSpec((tile, LANES), lambda i: (i, 0)),
                      pl.BlockSpec((1, 1, LANES), lambda i: (i, 0, 0))],
            out_specs=pl.BlockSpec((TOP_K, tile), lambda i: (0, i)),
            out_shape=jax.ShapeDtypeStruct((TOP_K, stream_tok), jnp.int32),
            compiler_params=_cparams(("parallel",)),
            name="route_pos",
        )(idx.reshape(stream_tok, LANES), base[first:first + n_tiles].reshape(n_tiles, 1, LANES)))
        first += n_tiles
    return block_e, n_used, positions


def _moe(h2p_list, idx_list, count_list, w1, b1, w2, b2):
    half = h2p_list[0].shape[1]
    d = 2 * half
    block_e, n_used, positions = _route(idx_list, count_list)
    n_blocks = block_e.shape[0]
    xs = _sc_scatter_rows(h2p_list, positions, n_blocks * MOE_ROWS)
    grid_spec = pltpu.PrefetchScalarGridSpec(
        num_scalar_prefetch=2,
        grid=(n_blocks,),
        in_specs=[pl.BlockSpec((MOE_ROWS, half), lambda i, be, nu: (i, 0)),
                  pl.BlockSpec((1, d, 2 * D_FF), lambda i, be, nu: (be[i], 0, 0)),
                  pl.BlockSpec((1, 1, 2 * D_FF), lambda i, be, nu: (be[i], 0, 0)),
                  pl.BlockSpec((1, D_FF, d), lambda i, be, nu: (be[i], 0, 0)),
                  pl.BlockSpec((1, 1, d), lambda i, be, nu: (be[i], 0, 0))],
        out_specs=pl.BlockSpec((MOE_ROWS, half), lambda i, be, nu: (i, 0)),
        scratch_shapes=[pltpu.VMEM((d, 2 * D_FF), BF16), pltpu.VMEM((D_FF, d), BF16)],
    )
    ys = pl.pallas_call(
        _ffn_kernel,
        grid_spec=grid_spec,
        out_shape=jax.ShapeDtypeStruct((n_blocks * MOE_ROWS, half), jnp.int32),
        compiler_params=_cparams(("arbitrary",)),
        name="moe_ffn",
    )(block_e, n_used, xs, w1, b1.reshape(N_EXPERTS, 1, -1), w2, b2.reshape(N_EXPERTS, 1, -1))
    return [_sc_gather_rows(ys, pos.reshape(-1)) for pos in positions]


def _final_kernel(x1_ref, y0_ref, y1_ref, y2_ref, y3_ref, gate_ref, mod_ref, g_ref, *rest):
    o_ref = rest[-1]
    gates = gate_ref[0]
    lo = hi = None
    for k, y_ref in enumerate((y0_ref, y1_ref, y2_ref, y3_ref)):
        y_lo, y_hi = _unpack_bf16_pairs(y_ref[0, 0])
        gk = gates[:, k:k + 1]
        lo = gk * y_lo if k == 0 else lo + gk * y_lo
        hi = gk * y_hi if k == 0 else hi + gk * y_hi
    y = jnp.concatenate([lo, hi], axis=-1)
    gate2 = mod_ref[0, 5:6, :]
    o_ref[0] = x1_ref[0] + gate2 * _rms(y, g_ref[...])


def _final(x1, yg, gate, mod, mod_row, g_post_ffn, tm, b0, n_batch, out_prev):
    bsz, seq, d = x1.shape
    tok3 = lambda b, i: (b, i, 0)
    yg = yg.reshape(TOP_K, bsz, seq, d // 2)
    slot_specs = [pl.BlockSpec((1, 1, tm, d // 2), functools.partial(lambda b, i, k: (k, b, i, 0), k=k))
                  for k in range(TOP_K)]
    in_specs = ([pl.BlockSpec((1, tm, d), tok3)] + slot_specs +
                [pl.BlockSpec((1, tm, LANES), tok3),
                 pl.BlockSpec((1, 6, d), lambda b, i: (mod_row(b), 0, 0)), _const_spec((1, d))])
    args = [x1, yg, yg, yg, yg, gate, mod, g_post_ffn.astype(F32).reshape(1, d)]
    aliases = {}
    if out_prev is not None:
        in_specs.append(pl.BlockSpec(memory_space=pl.ANY))
        args.append(out_prev)
        aliases = {len(args) - 1: 0}
    return pl.pallas_call(
        _final_kernel,
        grid=(bsz, seq // tm),
        in_specs=in_specs,
        out_specs=pl.BlockSpec((1, tm, d), lambda b, i: (b + b0, i, 0)),
        out_shape=jax.ShapeDtypeStruct((n_batch, seq, d), F32),
        input_output_aliases=aliases,
        compiler_params=_cparams(("parallel", "parallel")),
        name="final_residual",
    )(*args)


def _stream(x, mod, mod_row, p, ctx_kv, init_state, rope, tm, b0, bsz):
    _, seq, d = x.shape
    is_ctx = ctx_kv is None
    pj = _project(x, mod, p["g_pre_mix"], p["w_in"], p["b_gates"], mod_row, rope, is_ctx, tm, b0, bsz)
    if is_ctx:
        att = _attention(pj["q"], None, None, pj["kd"], pj["vd"], p["sink"], window=False)
    else:
        att = _attention(pj["q"], pj["kd"], pj["vd"], ctx_kv[0], ctx_kv[1], p["sink"], window=True)
    assert min(ML_CHUNK, seq) == tm
    ml = _mlstm(pj["mq"], pj["mk"], pj["mv"], pj["gates"], pj["gates_t"], init_state, emit_state=is_ctx, b0=b0)
    x1, h2, idx, gate, tile_counts = _merge(att, ml[0], ml[1], pj, x, mod, mod_row, p, tm, b0)
    moe_in = (h2.reshape(bsz * seq, d // 2), idx, tile_counts)
    return x1, gate, moe_in, pj, ml


def kernel(x_prompt, x_sample, c, cache_k, cache_v, state_C, state_n, state_m, c_ctx, w_ada, b_ada, g_pre_mix,
           w_in, b_gates, attn_sink, norm_mlstm, w_up_att, w_up_ml, w_out, g_post_mix, g_pre_ffn, w_router,
           b_router, w1, b1, w2, b2, g_post_ffn):
    depth = w_ada.shape[0]
    n_dec = c.shape[0]
    cond = jnp.concatenate([c_ctx[None, :], c], axis=0).astype(F32)
    cond = jnp.pad(cond, ((0, 16 - cond.shape[0]), (0, 0)))
    y_prompt, y_sample = x_prompt, x_sample
    ks_, vs_, cs_, ns_, ms_ = [], [], [], [], []
    for l in range(depth):
        p = dict(g_pre_mix=g_pre_mix[l], w_in=w_in[l], b_gates=b_gates[l], sink=attn_sink[l], norm_ml=norm_mlstm[l],
                 w_up_att=w_up_att[l], w_up_ml=w_up_ml[l], w_out=w_out[l], g_post_mix=g_post_mix[l],
                 g_pre_ffn=g_pre_ffn[l], w_router=w_router[l], b_router=b_router[l], w1=w1[l], b1=b1[l],
                 w2=w2[l], b2=b2[l], g_post_ffn=g_post_ffn[l])
        mod = _adaln(cond, w_ada[l], b_ada[l]).reshape(16, 6, D_MODEL)
        row_p = lambda b: 0
        bsz, seq = x_prompt.shape[:2]
        x1_p, gate_p, moe_p, pj, ml = _stream(y_prompt, mod, row_p, p, None, None, False, 256, 0, bsz)
        nat = pj["nat"]
        ks_.append(nat[..., :KV_W].reshape(bsz, seq, ATT_KV_HEADS, HEAD_DIM))
        vs_.append(nat[..., KV_W:].reshape(bsz, seq, ATT_KV_HEADS, HEAD_DIM))
        cs_.append(ml[2])
        ns_.append(ml[3])
        ms_.append(ml[4][..., 0])
        ctx_kv = (_dup_heads(cache_k[:, l], False), _dup_heads(cache_v[:, l], True))
        init = (state_C[:, l], state_n[:, l], state_m[:, l])
        half = n_dec // 2
        experts = (p["w1"], p["b1"].astype(F32), p["w2"], p["b2"].astype(F32))
        groups = []
        for b0 in (0, half):
            row_s = functools.partial(lambda b, off: b + off + 1, off=b0)
            sl = slice(b0, b0 + half)
            x1_s, gate_s, moe_s, _, _ = _stream(y_sample, mod, row_s, p, (ctx_kv[0][sl], ctx_kv[1][sl]),
                                                init, True, 512, b0, half)
            groups.append((x1_s, gate_s, moe_s, row_s))
        (x1_a, gate_a, moe_a, row_a), (x1_b, gate_b, moe_b, row_b) = groups
        yg_p, yg_a = _moe([moe_p[0], moe_a[0]], [moe_p[1], moe_a[1]], [moe_p[2], moe_a[2]], *experts)
        (yg_b,) = _moe([moe_b[0]], [moe_b[1]], [moe_b[2]], *experts)
        y_prompt = _final(x1_p, yg_p, gate_p, mod, row_p, p["g_post_ffn"], 256, 0, bsz, None)
        y_sample = _final(x1_a, yg_a, gate_a, mod, row_a, p["g_post_ffn"], 512, 0, n_dec, None)
        y_sample = _final(x1_b, yg_b, gate_b, mod, row_b, p["g_post_ffn"], 512, half, n_dec, y_sample)
    return (y_prompt, y_sample, jnp.stack(ks_, axis=1), jnp.stack(vs_, axis=1), jnp.stack(cs_, axis=1),
            jnp.stack(ns_, axis=1), jnp.stack(ms_, axis=1))
```

```python
import functools

import numpy as np
import jax
import jax.numpy as jnp
from jax import lax
from jax.experimental import pallas as pl
from jax.experimental.pallas import tpu as pltpu
from jax.experimental.pallas import tpu_sc as plsc

F32 = jnp.float32
BF16 = jnp.bfloat16

D_MODEL = 1024
GRID_W = 64
ATT_HEADS = 8
ATT_KV_HEADS = 2
ATT_GROUP = ATT_HEADS // ATT_KV_HEADS
HEAD_DIM = 64
BLOCK = 128
ROPE_THETA = 10000.0
AXIS_FREQS = HEAD_DIM // 4
ML_HEADS = 4
ML_DK = 128
ML_DV = 256
ML_CHUNK = 512
N_EXPERTS = 32
TOP_K = 4
D_FF = 1024
SWIGLU_ALPHA = 1.702
SWIGLU_LIMIT = 7.0
EPS = 1e-6
NEG = -1e30

Q_W = ATT_HEADS * HEAD_DIM
KV_W = ATT_KV_HEADS * HEAD_DIM
MLQK_W = ML_HEADS * ML_DK
MLV_W = ML_HEADS * ML_DV
GATE_W = 4 * ML_HEADS

LANES = 128
VMEM_LIMIT = 56 * 1024 * 1024
MOE_ROWS = 512
SC_CORES = 2
SC_SUBCORES = 16
SC_GATHER_CHUNK = 64


def _cparams(sem):
    return pltpu.CompilerParams(dimension_semantics=sem, vmem_limit_bytes=VMEM_LIMIT)


def _const_spec(shape):
    nd = len(shape)
    return pl.BlockSpec(shape, lambda *_: (0,) * nd)


def _adaln_kernel(cond_ref, w_ref, b_ref, o_ref):
    cnd = cond_ref[...]
    act = cnd * jax.nn.sigmoid(cnd)
    o_ref[...] = jnp.dot(act.astype(BF16), w_ref[...].astype(BF16),
                         preferred_element_type=F32) + b_ref[...]


def _adaln(cond, w_ada, b_ada):
    rows, d = cond.shape
    n = w_ada.shape[1]
    tn = 1536
    return pl.pallas_call(
        _adaln_kernel,
        grid=(n // tn,),
        in_specs=[_const_spec((rows, d)),
                  pl.BlockSpec((d, tn), lambda j: (0, j)),
                  pl.BlockSpec((1, tn), lambda j: (0, j))],
        out_specs=pl.BlockSpec((rows, tn), lambda j: (0, j)),
        out_shape=jax.ShapeDtypeStruct((rows, n), F32),
        compiler_params=_cparams(("parallel",)),
        name="adaln",
    )(cond, w_ada, b_ada.reshape(1, n))


def _pack_w_in(w_in, b_gates, rope, with_nat):
    o_q, o_k, o_v = 0, Q_W, Q_W + KV_W
    o_mq = Q_W + 2 * KV_W
    o_mk = o_mq + MLQK_W
    o_mv = o_mk + MLQK_W
    o_g = o_mv + MLV_W
    o_mo = o_g + GATE_W
    o_ga = o_mo + MLV_W
    o_gm = o_ga + D_MODEL
    half = HEAD_DIM // 2

    def head_cols(base, h, rot):
        lo = base + h * HEAD_DIM
        if rot:
            return [w_in[:, lo + half:lo + HEAD_DIM], w_in[:, lo:lo + half]]
        return [w_in[:, lo:lo + HEAD_DIM]]

    def q_cols(rot):
        cols = []
        for h in range(ATT_HEADS):
            cols += head_cols(o_q, h, rot)
        return cols

    def dup_cols(base, rot):
        cols = []
        for h in range(ATT_KV_HEADS):
            hc = head_cols(base, h, rot)
            cols += hc + hc
        return cols

    segs, cols, off = {}, [], 0

    def add(name, cl):
        nonlocal off
        width = sum(c.shape[1] for c in cl)
        segs[name] = (off, off + width)
        cols.extend(cl)
        off += width

    add("q", q_cols(False))
    if rope:
        add("qrot", q_cols(True))
    add("kd", dup_cols(o_k, False))
    if rope:
        add("kdrot", dup_cols(o_k, True))
    if with_nat:
        add("nat", [w_in[:, o_k:o_k + 2 * KV_W]])
    add("mq", [w_in[:, o_mq:o_mq + MLQK_W]])
    add("mv", [w_in[:, o_mv:o_mv + MLV_W]])
    add("mo", [w_in[:, o_mo:o_mo + MLV_W]])
    add("ga", [w_in[:, o_ga:o_ga + D_MODEL]])
    add("gm", [w_in[:, o_gm:o_gm + D_MODEL]])
    add("gates", [w_in[:, o_g:o_g + GATE_W], jnp.zeros((D_MODEL, LANES - GATE_W), w_in.dtype)])
    wp = jnp.concatenate(cols, axis=1).astype(BF16)
    wvd_t = jnp.concatenate(dup_cols(o_v, False) + [w_in[:, o_mk:o_mk + MLQK_W]], axis=1).T.astype(BF16)
    bg = jnp.pad(b_gates.astype(F32), (0, LANES - GATE_W)).reshape(1, LANES)
    return wp, wvd_t, bg, segs


def _rope_tables(n_tok):
    rows = n_tok // GRID_W
    row = np.repeat(np.arange(rows), GRID_W).astype(np.float32)
    col = np.tile(np.arange(GRID_W), rows).astype(np.float32)
    inv = (np.float32(ROPE_THETA) ** (-np.arange(AXIS_FREQS, dtype=np.float32) / AXIS_FREQS)).astype(np.float32)
    ang = np.concatenate([row[:, None] * inv, col[:, None] * inv], axis=-1).astype(np.float32)
    cos, sin = np.cos(ang), np.sin(ang)
    c64 = np.concatenate([cos, cos], axis=-1)
    s64 = np.concatenate([-sin, sin], axis=-1)
    return (jnp.asarray(np.tile(c64, (1, 2)), F32), jnp.asarray(np.tile(s64, (1, 2)), F32))


def _log_sigmoid(x):
    return jnp.minimum(x, 0.0) - jnp.log1p(jnp.exp(-jnp.abs(x)))


def _prefix_sum_rows(x):
    row = lax.broadcasted_iota(jnp.int32, x.shape, 0)
    k = 1
    while k < x.shape[0]:
        x = x + jnp.where(row >= k, pltpu.roll(x, k, 0), 0.0)
        k *= 2
    return x


def _proj_kernel(*refs, segs, rope, with_nat):
    it = iter(refs)
    x_ref, mod_ref, g_ref, w_ref, wvt_ref, bg_ref = (next(it) for _ in range(6))
    cos_ref = sin_ref = None
    if rope:
        cos_ref, sin_ref = next(it), next(it)
    q_ref, kd_ref, vd_ref = next(it), next(it), next(it)
    nat_ref = next(it) if with_nat else None
    mq_ref, mk_ref, mv_ref, gates_ref, gates_t_ref, mo_ref, ga_ref, gm_ref = (next(it) for _ in range(8))

    x = x_ref[0]
    shift = mod_ref[0, 0:1, :]
    scale = mod_ref[0, 1:2, :]
    ms = jnp.mean(x * x, axis=-1, keepdims=True)
    xn = x * lax.rsqrt(ms + EPS) * g_ref[...]
    xb = (xn * (1.0 + scale) + shift).astype(BF16)

    def seg(name):
        lo, hi = segs[name]
        return jnp.dot(xb, w_ref[:, lo:hi], preferred_element_type=F32)

    uq = seg("q")
    ukd = seg("kd")
    if rope:
        uqr = seg("qrot")
        ukr = seg("kdrot")
        cs = cos_ref[...]
        sn = sin_ref[...]
        for h in range(ATT_HEADS // 2):
            sl = slice(h * LANES, (h + 1) * LANES)
            q_ref[0, h] = ((uq[:, sl] * cs + uqr[:, sl] * sn) * HEAD_DIM ** -0.5).astype(BF16)
        for h in range(ATT_KV_HEADS):
            sl = slice(h * LANES, (h + 1) * LANES)
            kd_ref[0, :, sl] = (ukd[:, sl] * cs + ukr[:, sl] * sn).astype(BF16)
    else:
        for h in range(ATT_HEADS // 2):
            sl = slice(h * LANES, (h + 1) * LANES)
            q_ref[0, h] = (uq[:, sl] * HEAD_DIM ** -0.5).astype(BF16)
        kd_ref[0] = ukd.astype(BF16)
    ut = lax.dot_general(wvt_ref[...], xb, (((1,), (1,)), ((), ())), preferred_element_type=F32)
    vd_ref[0] = ut[:2 * LANES].astype(BF16)
    mk_ref[0] = ut[2 * LANES:].astype(BF16)
    if with_nat:
        nat_ref[0] = seg("nat")
    mq_ref[0] = (seg("mq") * ML_DK ** -0.5).astype(BF16)
    mv_ref[0] = seg("mv").astype(BF16)
    gates = seg("gates") + bg_ref[...]
    lf = _log_sigmoid(gates)
    csum = _prefix_sum_rows(lf)
    lane = lax.broadcasted_iota(jnp.int32, gates.shape, 1)
    cum = jnp.where(lane >= 2 * ML_HEADS, csum[-1:, :] - csum + lf, csum)
    gates_ref[0] = cum
    gates_t_ref[0, :GATE_W, :] = gates.T[:GATE_W, :]
    gates_t_ref[0, GATE_W:, :] = cum.T[:GATE_W, :]
    mo_ref[0] = _sigmoid(seg("mo")).astype(BF16)
    ga_ref[0] = _sigmoid(seg("ga")).astype(BF16)
    gm_ref[0] = _sigmoid(seg("gm")).astype(BF16)


def _project(x, mod, g_pre, w_in, b_gates, mod_row, rope, with_nat, tm, b0, bsz):
    _, seq, d = x.shape
    wp, wvd_t, bg, segs = _pack_w_in(w_in, b_gates, rope, with_nat)
    nw = wp.shape[1]
    tok3 = lambda b, i: (b, i, 0)
    in_specs = [pl.BlockSpec((1, tm, d), lambda b, i: (b + b0, i, 0)),
                pl.BlockSpec((1, 6, d), lambda b, i: (mod_row(b), 0, 0)),
                _const_spec((1, d)),
                pl.BlockSpec((d, nw), lambda b, i: (0, 0), pipeline_mode=pl.Buffered(1)),
                _const_spec((2 * LANES + MLQK_W, d)), _const_spec((1, LANES))]
    args = [x, mod, g_pre.reshape(1, d), wp, wvd_t, bg]
    if rope:
        cos_t, sin_t = _rope_tables(seq)
        in_specs += [pl.BlockSpec((tm, LANES), lambda b, i: (i, 0))] * 2
        args += [cos_t, sin_t]

    def tok_out(width, dtype):
        return (jax.ShapeDtypeStruct((bsz, seq, width), dtype), pl.BlockSpec((1, tm, width), tok3))

    outs = [(jax.ShapeDtypeStruct((bsz, ATT_HEADS // 2, seq, LANES), BF16),
             pl.BlockSpec((1, ATT_HEADS // 2, tm, LANES), lambda b, i: (b, 0, i, 0))),
            tok_out(2 * LANES, BF16),
            (jax.ShapeDtypeStruct((bsz, 2 * LANES, seq), BF16),
             pl.BlockSpec((1, 2 * LANES, tm), lambda b, i: (b, 0, i)))]
    if with_nat:
        outs.append(tok_out(2 * KV_W, F32))
    outs += [tok_out(MLQK_W, BF16),
             (jax.ShapeDtypeStruct((bsz, MLQK_W, seq), BF16),
              pl.BlockSpec((1, MLQK_W, tm), lambda b, i: (b, 0, i))),
             tok_out(MLV_W, BF16), tok_out(LANES, F32),
             (jax.ShapeDtypeStruct((bsz, 2 * GATE_W, seq), F32),
              pl.BlockSpec((1, 2 * GATE_W, tm), lambda b, i: (b, 0, i))),
             tok_out(MLV_W, BF16), tok_out(D_MODEL, BF16), tok_out(D_MODEL, BF16)]
    res = pl.pallas_call(
        functools.partial(_proj_kernel, segs=segs, rope=rope, with_nat=with_nat),
        grid=(bsz, seq // tm),
        in_specs=in_specs,
        out_specs=[o[1] for o in outs],
        out_shape=[o[0] for o in outs],
        compiler_params=_cparams(("parallel", "parallel")),
        name="proj_rope" if rope else "proj_ctx",
    )(*args)
    names = ["q", "kd", "vd"] + (["nat"] if with_nat else []) + ["mq", "mk", "mv", "gates", "gates_t", "mo", "ga", "gm"]
    return dict(zip(names, res))


def _attn_kernel(*refs, window, n_blocks, n_ctx):
    it = iter(refs)
    sink_ref, q_ref = next(it), next(it)
    if window:
        kp_ref, kc_ref, kn_ref, vp_ref, vc_ref, vn_ref = (next(it) for _ in range(6))
    ck_ref, cv_ref, o_ref = next(it), next(it), next(it)
    j = pl.program_id(1)
    n_win = 3 * BLOCK if window else 0
    cols = ATT_GROUP * BLOCK

    col_id = lax.broadcasted_iota(jnp.int32, (1, cols), 1)
    if window:
        kl = lax.broadcasted_iota(jnp.int32, (BLOCK, cols), 0)
        ql = lax.broadcasted_iota(jnp.int32, (BLOCK, cols), 1) & (BLOCK - 1)
        prev_ok = kl >= ql + jnp.where(j == 0, BLOCK, 0)
        next_ok = kl <= ql - jnp.where(j == n_blocks - 1, BLOCK, 0)
    lane_lo = lax.broadcasted_iota(jnp.int32, (BLOCK, LANES), 1) < HEAD_DIM
    lane_row = lax.broadcasted_iota(jnp.int32, (1, LANES), 1) < HEAD_DIM
    half_lo = jnp.where(lane_row, 1.0, 0.0).astype(BF16)
    half_hi = jnp.where(lane_row, 0.0, 1.0).astype(BF16)

    for kvh in range(ATT_KV_HEADS):
        sl = slice(kvh * LANES, (kvh + 1) * LANES)
        if window:
            keys = jnp.concatenate([kp_ref[0, :, sl], kc_ref[0, :, sl], kn_ref[0, :, sl], ck_ref[0, :, sl]], axis=0)
            vals_t = jnp.concatenate([vp_ref[0, sl, :], vc_ref[0, sl, :], vn_ref[0, sl, :], cv_ref[0, sl, :]], axis=1)
        else:
            keys = ck_ref[0, :, sl]
            vals_t = cv_ref[0, sl, :]
        qs = jnp.concatenate([q_ref[0, kvh * (ATT_GROUP // 2) + g // 2] * (half_lo if g % 2 == 0 else half_hi)
                              for g in range(ATT_GROUP)], axis=0)
        s = lax.dot_general(keys, qs, (((1,), (1,)), ((), ())), preferred_element_type=F32)
        if window:
            s = jnp.concatenate([jnp.where(prev_ok, s[:BLOCK], NEG), s[BLOCK:2 * BLOCK],
                                 jnp.where(next_ok, s[2 * BLOCK:n_win], NEG), s[n_win:]], axis=0)
        snk = jnp.full((1, cols), sink_ref[kvh * ATT_GROUP + ATT_GROUP - 1], F32)
        for g in range(ATT_GROUP - 2, -1, -1):
            snk = jnp.where(col_id < (g + 1) * BLOCK, sink_ref[kvh * ATT_GROUP + g], snk)
        m = jnp.maximum(jnp.max(s, axis=0, keepdims=True), snk)
        p = jnp.exp(s - m)
        den = jnp.sum(p, axis=0, keepdims=True) + jnp.exp(snk - m)
        o_t = jnp.dot(vals_t, p.astype(BF16), preferred_element_type=F32) / den
        for pair in range(ATT_GROUP // 2):
            even = o_t[:, (2 * pair) * BLOCK:(2 * pair + 1) * BLOCK].T
            odd = o_t[:, (2 * pair + 1) * BLOCK:(2 * pair + 2) * BLOCK].T
            col = (kvh * (ATT_GROUP // 2) + pair) * LANES
            o_ref[0, :, col:col + LANES] = jnp.where(lane_lo, even, odd).astype(BF16)


def _attention(q, kd, vd_t, ckd, cvd_t, sink, window):
    bsz, _, seq, _ = q.shape
    nb = seq // BLOCK
    n_ctx = ckd.shape[1]
    in_specs = [pl.BlockSpec(memory_space=pltpu.SMEM),
                pl.BlockSpec((1, ATT_HEADS // 2, BLOCK, LANES), lambda b, j: (b, 0, j, 0))]
    args = [sink.astype(F32), q]
    if window:
        in_specs += [pl.BlockSpec((1, BLOCK, 2 * LANES), lambda b, j: (b, jnp.maximum(j - 1, 0), 0)),
                     pl.BlockSpec((1, BLOCK, 2 * LANES), lambda b, j: (b, j, 0)),
                     pl.BlockSpec((1, BLOCK, 2 * LANES), lambda b, j: (b, jnp.minimum(j + 1, nb - 1), 0)),
                     pl.BlockSpec((1, 2 * LANES, BLOCK), lambda b, j: (b, 0, jnp.maximum(j - 1, 0))),
                     pl.BlockSpec((1, 2 * LANES, BLOCK), lambda b, j: (b, 0, j)),
                     pl.BlockSpec((1, 2 * LANES, BLOCK), lambda b, j: (b, 0, jnp.minimum(j + 1, nb - 1)))]
        args += [kd, kd, kd, vd_t, vd_t, vd_t]
    in_specs += [pl.BlockSpec((1, n_ctx, 2 * LANES), lambda b, j: (b, 0, 0)),
                 pl.BlockSpec((1, 2 * LANES, n_ctx), lambda b, j: (b, 0, 0))]
    args += [ckd, cvd_t]
    return pl.pallas_call(
        functools.partial(_attn_kernel, window=window, n_blocks=nb, n_ctx=n_ctx),
        grid=(bsz, nb),
        in_specs=in_specs,
        out_specs=pl.BlockSpec((1, BLOCK, Q_W), lambda b, j: (b, j, 0)),
        out_shape=jax.ShapeDtypeStruct((bsz, seq, Q_W), BF16),
        compiler_params=_cparams(("parallel", "parallel")),
        name="attn_window" if window else "attn_ctx",
    )(*args)


def _dup_heads(t, transpose):
    b, l = t.shape[:2]
    out = jnp.concatenate([t, t], axis=-1).reshape(b, l, ATT_KV_HEADS * LANES).astype(BF16)
    return jnp.swapaxes(out, 1, 2) if transpose else out


def _mlstm_kernel(*refs, chunk_len, n_chunks, has_init, emit_state):
    it = iter(refs)
    qkvg = [[next(it) for _ in range(5)] for _ in range(2)]
    if has_init:
        c0_ref, n0_ref, m0_ref = next(it), next(it), next(it)
    h_refs = [next(it), next(it)]
    if emit_state:
        co_ref, no_ref, mo_ref = next(it), next(it), next(it)
    st_s, m_s = next(it), next(it)
    t = pl.program_id(1)
    L = chunk_len

    @pl.when(t == 0)
    def _():
        for d in range(2):
            for h in range(ML_HEADS):
                if has_init:
                    st_s[d, h, :, :ML_DV] = c0_ref[0, d, h].T
                    st_s[d, h, :, ML_DV:] = jnp.broadcast_to(n0_ref[0, d, h:h + 1, :], (LANES, ML_DK)).T
                    m_s[d, h] = m0_ref[0, d, h:h + 1, :]
                else:
                    st_s[d, h] = jnp.zeros((ML_DK, ML_DV + LANES), F32)
                    m_s[d, h] = jnp.zeros((1, LANES), F32)

    ri = lax.broadcasted_iota(jnp.int32, (L, L), 0)
    ci = lax.broadcasted_iota(jnp.int32, (L, L), 1)
    ones_blk = jnp.ones((L, LANES), BF16)

    for d in range(2):
        q_ref, kt_ref, v_ref, cum_ref, rows_ref = qkvg[d]
        tri = (ci <= ri) if d == 0 else (ci >= ri)
        bc = cum_ref[0]
        g_t = rows_ref[0, :GATE_W, :]
        br = rows_ref[0, GATE_W:, :]
        last = L - 1 if d == 0 else 0

        for h in range(ML_HEADS):
            icol_i = 2 * ML_HEADS * d + h
            fcol = icol_i + ML_HEADS
            bcol = bc[:, fcol:fcol + 1]
            brow = br[fcol:fcol + 1, :]
            irow = g_t[icol_i:icol_i + 1, :]
            b_last = bc[last:last + 1, fcol:fcol + 1]
            m_prev = m_s[d, h][:, 0:1]
            st_prev = st_s[d, h]
            qh = q_ref[0, :, h * ML_DK:(h + 1) * ML_DK]
            kt = kt_ref[0, h * ML_DK:(h + 1) * ML_DK, :]
            v_aug = jnp.concatenate([v_ref[0, :, h * ML_DV:(h + 1) * ML_DV], ones_blk], axis=1)

            rel = jnp.where(tri, irow - brow, NEG)
            inter = bcol + m_prev
            m_t = jnp.maximum(inter, bcol + jnp.max(rel, axis=-1, keepdims=True))
            w = jnp.exp(rel + (bcol - m_t))
            w_inter = jnp.exp(inter - m_t)
            s = jnp.dot(qh, kt, preferred_element_type=F32) * w
            lhs = jnp.concatenate([s.astype(BF16), (qh.astype(F32) * w_inter).astype(BF16)], axis=1)
            rhs = jnp.concatenate([v_aug, st_prev.astype(BF16)], axis=0)
            mix = jnp.dot(lhs, rhs, preferred_element_type=F32)
            inv = 1.0 / jnp.maximum(jnp.abs(mix[:, ML_DV:]), jnp.exp(-m_t))
            for c in range(ML_DV // LANES):
                lo = h * ML_DV + c * LANES
                h_refs[d][0, :, lo:lo + LANES] = mix[:, c * LANES:(c + 1) * LANES] * inv

            a = b_last - brow + irow
            m_new = jnp.maximum(b_last + m_prev, jnp.max(a, axis=-1, keepdims=True))
            wk = jnp.exp(a - m_new)
            decay = jnp.exp(b_last + m_prev - m_new)
            kw_t = (kt.astype(F32) * wk).astype(BF16)
            st_new = decay * st_prev + jnp.dot(kw_t, v_aug, preferred_element_type=F32)
            st_s[d, h] = st_new
            m_s[d, h] = jnp.broadcast_to(m_new, (1, LANES))
            if emit_state:
                @pl.when(t == n_chunks - 1)
                def _():
                    co_ref[0, d, h] = st_new[:, :ML_DV].T
                    no_ref[0, d, h:h + 1, :] = st_new[:, ML_DV:].T[0:1, :]
                    mo_ref[0, d, h:h + 1, :] = jnp.broadcast_to(m_new, (1, LANES))


def _mlstm(mq, mk_t, mv, gate_cum, gate_rows, init, emit_state, b0=0):
    bsz, seq, _ = mq.shape
    chunk_len = min(ML_CHUNK, seq)
    nc = seq // chunk_len
    fwd = lambda b, t: (b, t, 0)
    bwd = lambda b, t: (b, nc - 1 - t, 0)
    fwd_t = lambda b, t: (b, 0, t)
    bwd_t = lambda b, t: (b, 0, nc - 1 - t)
    state5 = lambda b, t: (b, 0, 0, 0, 0)
    state4 = lambda b, t: (b, 0, 0, 0)
    init5 = lambda b, t: (b + b0, 0, 0, 0, 0)
    init4 = lambda b, t: (b + b0, 0, 0, 0)
    in_specs, args = [], []
    for chunk, chunk_t in ((fwd, fwd_t), (bwd, bwd_t)):
        in_specs += [pl.BlockSpec((1, chunk_len, MLQK_W), chunk), pl.BlockSpec((1, MLQK_W, chunk_len), chunk_t),
                     pl.BlockSpec((1, chunk_len, MLV_W), chunk), pl.BlockSpec((1, chunk_len, LANES), chunk),
                     pl.BlockSpec((1, 2 * GATE_W, chunk_len), chunk_t)]
        args += [mq, mk_t, mv, gate_cum, gate_rows]
    if init is not None:
        c0, n0, m0 = init
        in_specs += [pl.BlockSpec((1, 2, ML_HEADS, ML_DV, ML_DK), init5),
                     pl.BlockSpec((1, 2, ML_HEADS, ML_DK), init4),
                     pl.BlockSpec((1, 2, ML_HEADS, LANES), init4)]
        args += [c0.astype(F32), n0.astype(F32),
                 jnp.broadcast_to(m0.astype(F32)[..., None], m0.shape + (LANES,))]
    out_shape = [jax.ShapeDtypeStruct((bsz, seq, MLV_W), F32)] * 2
    out_specs = [pl.BlockSpec((1, chunk_len, MLV_W), fwd), pl.BlockSpec((1, chunk_len, MLV_W), bwd)]
    if emit_state:
        out_shape += [jax.ShapeDtypeStruct((bsz, 2, ML_HEADS, ML_DV, ML_DK), F32),
                      jax.ShapeDtypeStruct((bsz, 2, ML_HEADS, ML_DK), F32),
                      jax.ShapeDtypeStruct((bsz, 2, ML_HEADS, LANES), F32)]
        out_specs += [pl.BlockSpec((1, 2, ML_HEADS, ML_DV, ML_DK), state5),
                      pl.BlockSpec((1, 2, ML_HEADS, ML_DK), state4),
                      pl.BlockSpec((1, 2, ML_HEADS, LANES), state4)]
    return pl.pallas_call(
        functools.partial(_mlstm_kernel, chunk_len=chunk_len, n_chunks=nc, has_init=init is not None, emit_state=emit_state),
        grid=(bsz, nc),
        in_specs=in_specs,
        out_specs=out_specs,
        out_shape=out_shape,
        scratch_shapes=[pltpu.VMEM((2, ML_HEADS, ML_DK, ML_DV + LANES), F32),
                        pltpu.VMEM((2, ML_HEADS, 1, LANES), F32)],
        compiler_params=_cparams(("parallel", "arbitrary")),
        name="mlstm_state" if emit_state else "mlstm",
    )(*args)


def _sigmoid(x):
    return 0.5 * jnp.tanh(0.5 * x) + 0.5


def _rms(x, g):
    return x * lax.rsqrt(jnp.mean(x * x, axis=-1, keepdims=True) + EPS) * g


def _pack_bf16_pairs(x):
    n = x.shape[1] // 2
    lo = pltpu.bitcast(x[:, :n].astype(BF16).astype(F32), jnp.int32)
    hi = pltpu.bitcast(x[:, n:].astype(BF16).astype(F32), jnp.int32)
    return lax.shift_right_logical(lo, 16) | hi


def _unpack_bf16_pairs(p):
    lo = pltpu.bitcast(lax.shift_left(p, 16), F32)
    hi = pltpu.bitcast(p & jnp.int32(-65536), F32)
    return lo, hi


def _merge_kernel(att_ref, hf_ref, hb_ref, mo_ref, ga_ref, gm_ref, x_ref, mod_ref, nml_ref, gpm_ref, gpf_ref,
                  wua_ref, wum_ref, wo_ref, wr_ref, br_ref, x1_ref, h2_ref, idx_ref, gate_ref, cnt_ref):
    hsum = hf_ref[0] + hb_ref[0]
    parts = []
    for h in range(ML_HEADS):
        hh = hsum[:, h * ML_DV:(h + 1) * ML_DV]
        parts.append(hh * lax.rsqrt(jnp.mean(hh * hh, axis=-1, keepdims=True) + EPS))
    hn = jnp.concatenate(parts, axis=-1) * nml_ref[...]
    ml = (hn * mo_ref[0].astype(F32)).astype(BF16)
    a = jnp.dot(att_ref[0], wua_ref[...], preferred_element_type=F32)
    m = jnp.dot(ml, wum_ref[...], preferred_element_type=F32)
    z = (ga_ref[0].astype(F32) * a + gm_ref[0].astype(F32) * m).astype(BF16)
    mix = jnp.dot(z, wo_ref[...], preferred_element_type=F32)
    gate1 = mod_ref[0, 2:3, :]
    shift2 = mod_ref[0, 3:4, :]
    scale2 = mod_ref[0, 4:5, :]
    x1 = x_ref[0] + gate1 * _rms(mix, gpm_ref[...])
    x1_ref[0] = x1
    h2 = _rms(x1, gpf_ref[...]) * (1.0 + scale2) + shift2
    h2_ref[0] = _pack_bf16_pairs(h2)

    h2_hi = h2.astype(BF16)
    h2_lo = (h2 - h2_hi.astype(F32)).astype(BF16)
    wr = wr_ref[...]
    wr_hi = wr.astype(BF16)
    wr_lo = (wr - wr_hi.astype(F32)).astype(BF16)
    hi_terms = jnp.dot(h2_hi, jnp.concatenate([wr_hi, wr_lo], axis=1), preferred_element_type=F32)
    logits = (hi_terms[:, :LANES] + hi_terms[:, LANES:]
              + jnp.dot(h2_lo, wr_hi, preferred_element_type=F32)) + br_ref[...]
    lane = lax.broadcasted_iota(jnp.int32, logits.shape, 1).astype(F32)
    work = jnp.where(lane < N_EXPERTS, logits, -jnp.inf)
    idx_out = jnp.zeros(logits.shape, F32)
    val_out = jnp.zeros(logits.shape, F32)
    picked = jnp.zeros(logits.shape, F32)
    top0 = None
    esum = None
    for k in range(TOP_K):
        mx = jnp.max(work, axis=-1, keepdims=True)
        sel = jnp.min(jnp.where(work == mx, lane, float(LANES)), axis=-1, keepdims=True)
        if k == 0:
            top0 = mx
        e = jnp.exp(mx - top0)
        esum = e if k == 0 else esum + e
        idx_out = jnp.where(lane == k, sel, idx_out)
        val_out = jnp.where(lane == k, e, val_out)
        picked = jnp.where(lane == sel, 1.0, picked)
        work = jnp.where(lane == sel, -jnp.inf, work)
    idx_ref[0] = idx_out.astype(jnp.int32)
    gate_ref[0] = val_out / esum
    cnt_ref[0] = jnp.sum(picked, axis=0, keepdims=True)


def _merge(att, hf, hb, pj, x, mod, mod_row, p, tm, b0):
    bsz = att.shape[0]
    _, seq, d = x.shape
    tok3 = lambda b, i: (b, i, 0)
    wr = jnp.pad(p["w_router"].astype(F32), ((0, 0), (0, LANES - N_EXPERTS)))
    br = jnp.pad(p["b_router"].astype(F32), (0, LANES - N_EXPERTS)).reshape(1, LANES)
    row = lambda v: v.astype(F32).reshape(1, -1)
    in_specs = [pl.BlockSpec((1, tm, Q_W), tok3),
                pl.BlockSpec((1, tm, MLV_W), tok3), pl.BlockSpec((1, tm, MLV_W), tok3),
                pl.BlockSpec((1, tm, MLV_W), tok3), pl.BlockSpec((1, tm, d), tok3), pl.BlockSpec((1, tm, d), tok3),
                pl.BlockSpec((1, tm, d), lambda b, i: (b + b0, i, 0)),
                pl.BlockSpec((1, 6, d), lambda b, i: (mod_row(b), 0, 0)),
                _const_spec((1, MLV_W)), _const_spec((1, d)), _const_spec((1, d)),
                _const_spec((Q_W, d)), _const_spec((MLV_W, d)), _const_spec((d, d)),
                _const_spec((d, LANES)), _const_spec((1, LANES))]
    out_shape = [jax.ShapeDtypeStruct((bsz, seq, d), F32), jax.ShapeDtypeStruct((bsz, seq, d // 2), jnp.int32),
                 jax.ShapeDtypeStruct((bsz, seq, LANES), jnp.int32), jax.ShapeDtypeStruct((bsz, seq, LANES), F32),
                 jax.ShapeDtypeStruct((bsz * (seq // tm), 1, LANES), F32)]
    out_specs = [pl.BlockSpec((1, tm, d), tok3), pl.BlockSpec((1, tm, d // 2), tok3),
                 pl.BlockSpec((1, tm, LANES), tok3), pl.BlockSpec((1, tm, LANES), tok3),
                 pl.BlockSpec((1, 1, LANES), lambda b, i: (b * (seq // tm) + i, 0, 0))]
    return pl.pallas_call(
        _merge_kernel,
        grid=(bsz, seq // tm),
        in_specs=in_specs,
        out_specs=out_specs,
        out_shape=out_shape,
        compiler_params=_cparams(("parallel", "parallel")),
        name="merge_router",
    )(att, hf, hb, pj["mo"], pj["ga"], pj["gm"], x, mod, row(p["norm_ml"]), row(p["g_post_mix"]),
      row(p["g_pre_ffn"]), p["w_up_att"].astype(BF16), p["w_up_ml"].astype(BF16), p["w_out"].astype(BF16), wr, br)


def _sc_gather_rows(table, idx):
    n = idx.shape[0]
    width = table.shape[1]
    chunk = SC_GATHER_CHUNK
    n_workers = SC_CORES * SC_SUBCORES
    rows_per_worker = n // n_workers
    n_pairs = rows_per_worker // (2 * chunk)
    assert n_pairs * 2 * chunk * n_workers == n
    mesh = plsc.VectorSubcoreMesh(core_axis_name="c", subcore_axis_name="s")

    def body(table_hbm, idx_hbm, out_hbm, idx_v, rows_v, gsem, osem):
        wid = lax.axis_index("s") * SC_CORES + lax.axis_index("c")
        base0 = wid * rows_per_worker

        def gather_copy(b):
            return pltpu.make_async_copy(table_hbm.at[idx_v.at[b]], rows_v.at[b], gsem.at[b])

        def write_copy(ci, b):
            base = pl.multiple_of(base0 + ci * chunk, 8)
            return pltpu.make_async_copy(rows_v.at[b], out_hbm.at[pl.ds(base, chunk)], osem.at[b])

        def issue(ci, b):
            base = pl.multiple_of(base0 + ci * chunk, 8)
            pltpu.sync_copy(idx_hbm.at[pl.ds(base, chunk)], idx_v.at[b])
            gather_copy(b).start()

        def finish(ci, b):
            gather_copy(b).wait()
            write_copy(ci, b).start()

        issue(0, 0)

        @pl.loop(0, n_pairs)
        def _(j):
            @pl.when(j > 0)
            def _():
                write_copy(2 * j - 1, 1).wait()
            issue(2 * j + 1, 1)
            finish(2 * j, 0)

            @pl.when(j < n_pairs - 1)
            def _():
                write_copy(2 * j, 0).wait()
                issue(2 * j + 2, 0)
            finish(2 * j + 1, 1)

        write_copy(2 * n_pairs - 2, 0).wait()
        write_copy(2 * n_pairs - 1, 1).wait()

    return pl.kernel(
        body, mesh=mesh,
        out_type=jax.ShapeDtypeStruct((n, width), table.dtype),
        scratch_types=[pltpu.VMEM((2, chunk), jnp.int32), pltpu.VMEM((2, chunk, width), table.dtype),
                       pltpu.SemaphoreType.DMA((2,)), pltpu.SemaphoreType.DMA((2,))],
    )(table, idx)


def _ffn_kernel(be_ref, nu_ref, xs_ref, w1_ref, b1_ref, w2_ref, b2_ref, ys_ref, w1b_s, w2b_s):
    i = pl.program_id(0)
    e = be_ref[i]
    e_prev = be_ref[jnp.maximum(i - 1, 0)]

    @pl.when((i == 0) | (e != e_prev))
    def _():
        w1b_s[...] = w1_ref[0].astype(BF16)
        w2b_s[...] = w2_ref[0].astype(BF16)

    @pl.when(i < nu_ref[0])
    def _():
        x_lo, x_hi = _unpack_bf16_pairs(xs_ref[...])
        xb = jnp.concatenate([x_lo.astype(BF16), x_hi.astype(BF16)], axis=1)
        hmid = jnp.dot(xb, w1b_s[...], preferred_element_type=F32) + b1_ref[0]
        glu = jnp.minimum(hmid[:, :D_FF], SWIGLU_LIMIT)
        lin = jnp.clip(hmid[:, D_FF:], -SWIGLU_LIMIT, SWIGLU_LIMIT)
        act = ((lin + 1.0) * glu * _sigmoid(SWIGLU_ALPHA * glu)).astype(BF16)
        y = jnp.dot(act, w2b_s[...], preferred_element_type=F32) + b2_ref[0]
        ys_ref[...] = _pack_bf16_pairs(y)

    @pl.when(i >= nu_ref[0])
    def _():
        ys_ref[...] = jnp.zeros(ys_ref.shape, jnp.int32)


def _sc_scatter_rows(tables, positions, n_rows):
    width = tables[0].shape[1]
    chunk = SC_GATHER_CHUNK
    n_workers = SC_CORES * SC_SUBCORES
    n_streams = len(tables)
    mesh = plsc.VectorSubcoreMesh(core_axis_name="c", subcore_axis_name="s")

    def body(*refs):
        table_refs = refs[0:2 * n_streams:2]
        pos_refs = refs[1:2 * n_streams:2]
        out_hbm, idx_v, rows_v, lsem, ssem = refs[2 * n_streams:]
        wid = lax.axis_index("s") * SC_CORES + lax.axis_index("c")

        for table_hbm, pos_hbm in zip(table_refs, pos_refs):
            tok_per_worker = table_hbm.shape[0] // n_workers
            n_pairs = tok_per_worker // (2 * chunk)
            assert n_pairs * 2 * chunk * n_workers == table_hbm.shape[0]
            base0 = wid * tok_per_worker

            def load_copy(ci, b):
                base = pl.multiple_of(base0 + ci * chunk, 8)
                return pltpu.make_async_copy(table_hbm.at[pl.ds(base, chunk)], rows_v.at[b], lsem.at[b])

            def scatter_copy(b, k):
                return pltpu.make_async_copy(rows_v.at[b], out_hbm.at[idx_v.at[b, k]], ssem.at[b])

            def load(ci, b):
                base = pl.multiple_of(base0 + ci * chunk, 8)
                load_copy(ci, b).start()
                for k in range(TOP_K):
                    pltpu.sync_copy(pos_hbm.at[k, pl.ds(base, chunk)], idx_v.at[b, k])

            def scatter(ci, b):
                load_copy(ci, b).wait()
                for k in range(TOP_K):
                    scatter_copy(b, k).start()

            def drain(b):
                for k in range(TOP_K):
                    scatter_copy(b, k).wait()

            load(0, 0)

            @pl.loop(0, n_pairs)
            def _(j):
                @pl.when(j > 0)
                def _():
                    drain(1)
                load(2 * j + 1, 1)
                scatter(2 * j, 0)

                @pl.when(j < n_pairs - 1)
                def _():
                    drain(0)
                    load(2 * j + 2, 0)
                scatter(2 * j + 1, 1)

            drain(0)
            drain(1)

    args = [a for pair in zip(tables, positions) for a in pair]
    return pl.kernel(
        body, mesh=mesh,
        out_type=jax.ShapeDtypeStruct((n_rows, width), tables[0].dtype),
        scratch_types=[pltpu.VMEM((2, TOP_K, chunk), jnp.int32), pltpu.VMEM((2, chunk, width), tables[0].dtype),
                       pltpu.SemaphoreType.DMA((2,)), pltpu.SemaphoreType.DMA((2,))],
    )(*args)


def _pos_kernel(idx_ref, base_ref, pos_ref):
    idx = idx_ref[...]
    rows = idx.shape[0]
    lane = lax.broadcasted_iota(jnp.int32, idx.shape, 1)
    hots = [jnp.where(lane == idx[:, k:k + 1], 1.0, 0.0) for k in range(TOP_K)]
    cnt = hots[0] + hots[1] + hots[2] + hots[3]
    ri = lax.broadcasted_iota(jnp.int32, (rows, rows), 0)
    ci = lax.broadcasted_iota(jnp.int32, (rows, rows), 1)
    earlier = jnp.where(ci < ri, 1.0, 0.0).astype(BF16)
    ahead = jnp.dot(earlier, cnt.astype(BF16), preferred_element_type=F32) + base_ref[0]
    posmat = jnp.zeros(idx.shape, F32)
    for k in range(TOP_K):
        posmat = jnp.where(lane == k, jnp.sum(hots[k] * ahead, axis=-1, keepdims=True), posmat)
    pos_ref[...] = posmat.T[:TOP_K, :].astype(jnp.int32)


def _route(idx_list, count_list):
    tiles = [c.shape[0] for c in count_list]
    tile_tok = [idx.size // LANES // n for idx, n in zip(idx_list, tiles)]
    n_tok = sum(n * sz for n, sz in zip(tiles, tile_tok))
    n_blocks = n_tok * TOP_K // MOE_ROWS + N_EXPERTS
    cnt = jnp.concatenate([c.reshape(-1, LANES) for c in count_list], axis=0).astype(jnp.int32)
    tile_off = jnp.cumsum(cnt, axis=0) - cnt
    total = jnp.sum(cnt, axis=0)
    padded = (total + MOE_ROWS - 1) // MOE_ROWS * MOE_ROWS
    pend = jnp.cumsum(padded)
    base = ((pend - padded)[None, :] + tile_off).astype(F32)
    starts = jnp.arange(n_blocks, dtype=jnp.int32) * MOE_ROWS
    block_e = jnp.minimum(jnp.sum(pend[None, :N_EXPERTS] <= starts[:, None], axis=1), N_EXPERTS - 1).astype(jnp.int32)
    n_used = (pend[N_EXPERTS - 1] // MOE_ROWS).astype(jnp.int32).reshape(1)
    positions, first = [], 0
    for idx, n_tiles, tile in zip(idx_list, tiles, tile_tok):
        stream_tok = n_tiles * tile
        positions.append(pl.pallas_call(
            _pos_kernel,
            grid=(n_tiles,),
            in_specs=[pl.BlockSpec((tile, LANES), lambda i: (i, 0)),
                      pl.BlockSpec((1, 1, LANES), lambda i: (i, 0, 0))],
            out_specs=pl.BlockSpec((TOP_K, tile), lambda i: (0, i)),
            out_shape=jax.ShapeDtypeStruct((TOP_K, stream_tok), jnp.int32),
            compiler_params=_cparams(("parallel",)),
            name="route_pos",
        )(idx.reshape(stream_tok, LANES), base[first:first + n_tiles].reshape(n_tiles, 1, LANES)))
        first += n_tiles
    return block_e, n_used, positions


def _moe(h2p_list, idx_list, count_list, w1, b1, w2, b2):
    half = h2p_list[0].shape[1]
    d = 2 * half
    block_e, n_used, positions = _route(idx_list, count_list)
    n_blocks = block_e.shape[0]
    xs = _sc_scatter_rows(h2p_list, positions, n_blocks * MOE_ROWS)
    grid_spec = pltpu.PrefetchScalarGridSpec(
        num_scalar_prefetch=2,
        grid=(n_blocks,),
        in_specs=[pl.BlockSpec((MOE_ROWS, half), lambda i, be, nu: (i, 0)),
                  pl.BlockSpec((1, d, 2 * D_FF), lambda i, be, nu: (be[i], 0, 0)),
                  pl.BlockSpec((1, 1, 2 * D_FF), lambda i, be, nu: (be[i], 0, 0)),
                  pl.BlockSpec((1, D_FF, d), lambda i, be, nu: (be[i], 0, 0)),
                  pl.BlockSpec((1, 1, d), lambda i, be, nu: (be[i], 0, 0))],
        out_specs=pl.BlockSpec((MOE_ROWS, half), lambda i, be, nu: (i, 0)),
        scratch_shapes=[pltpu.VMEM((d, 2 * D_FF), BF16), pltpu.VMEM((D_FF, d), BF16)],
    )
    ys = pl.pallas_call(
        _ffn_kernel,
        grid_spec=grid_spec,
        out_shape=jax.ShapeDtypeStruct((n_blocks * MOE_ROWS, half), jnp.int32),
        compiler_params=_cparams(("arbitrary",)),
        name="moe_ffn",
    )(block_e, n_used, xs, w1, b1.reshape(N_EXPERTS, 1, -1), w2, b2.reshape(N_EXPERTS, 1, -1))
    return [_sc_gather_rows(ys, pos.reshape(-1)) for pos in positions]


def _final_kernel(x1_ref, y0_ref, y1_ref, y2_ref, y3_ref, gate_ref, mod_ref, g_ref, *rest):
    o_ref = rest[-1]
    gates = gate_ref[0]
    lo = hi = None
    for k, y_ref in enumerate((y0_ref, y1_ref, y2_ref, y3_ref)):
        y_lo, y_hi = _unpack_bf16_pairs(y_ref[0, 0])
        gk = gates[:, k:k + 1]
        lo = gk * y_lo if k == 0 else lo + gk * y_lo
        hi = gk * y_hi if k == 0 else hi + gk * y_hi
    y = jnp.concatenate([lo, hi], axis=-1)
    gate2 = mod_ref[0, 5:6, :]
    o_ref[0] = x1_ref[0] + gate2 * _rms(y, g_ref[...])


def _final(x1, yg, gate, mod, mod_row, g_post_ffn, tm, b0, n_batch, out_prev):
    bsz, seq, d = x1.shape
    tok3 = lambda b, i: (b, i, 0)
    yg = yg.reshape(TOP_K, bsz, seq, d // 2)
    slot_specs = [pl.BlockSpec((1, 1, tm, d // 2), functools.partial(lambda b, i, k: (k, b, i, 0), k=k))
                  for k in range(TOP_K)]
    in_specs = ([pl.BlockSpec((1, tm, d), tok3)] + slot_specs +
                [pl.BlockSpec((1, tm, LANES), tok3),
                 pl.BlockSpec((1, 6, d), lambda b, i: (mod_row(b), 0, 0)), _const_spec((1, d))])
    args = [x1, yg, yg, yg, yg, gate, mod, g_post_ffn.astype(F32).reshape(1, d)]
    aliases = {}
    if out_prev is not None:
        in_specs.append(pl.BlockSpec(memory_space=pl.ANY))
        args.append(out_prev)
        aliases = {len(args) - 1: 0}
    return pl.pallas_call(
        _final_kernel,
        grid=(bsz, seq // tm),
        in_specs=in_specs,
        out_specs=pl.BlockSpec((1, tm, d), lambda b, i: (b + b0, i, 0)),
        out_shape=jax.ShapeDtypeStruct((n_batch, seq, d), F32),
        input_output_aliases=aliases,
        compiler_params=_cparams(("parallel", "parallel")),
        name="final_residual",
    )(*args)


def _stream(x, mod, mod_row, p, ctx_kv, init_state, rope, tm, b0, bsz):
    _, seq, d = x.shape
    is_ctx = ctx_kv is None
    pj = _project(x, mod, p["g_pre_mix"], p["w_in"], p["b_gates"], mod_row, rope, is_ctx, tm, b0, bsz)
    if is_ctx:
        att = _attention(pj["q"], None, None, pj["kd"], pj["vd"], p["sink"], window=False)
    else:
        att = _attention(pj["q"], pj["kd"], pj["vd"], ctx_kv[0], ctx_kv[1], p["sink"], window=True)
    assert min(ML_CHUNK, seq) == tm
    ml = _mlstm(pj["mq"], pj["mk"], pj["mv"], pj["gates"], pj["gates_t"], init_state, emit_state=is_ctx, b0=b0)
    x1, h2, idx, gate, tile_counts = _merge(att, ml[0], ml[1], pj, x, mod, mod_row, p, tm, b0)
    moe_in = (h2.reshape(bsz * seq, d // 2), idx, tile_counts)
    return x1, gate, moe_in, pj, ml


def kernel(x_prompt, x_sample, c, cache_k, cache_v, state_C, state_n, state_m, c_ctx, w_ada, b_ada, g_pre_mix,
           w_in, b_gates, attn_sink, norm_mlstm, w_up_att, w_up_ml, w_out, g_post_mix, g_pre_ffn, w_router,
           b_router, w1, b1, w2, b2, g_post_ffn):
    depth = w_ada.shape[0]
    n_dec = c.shape[0]
    cond = jnp.concatenate([c_ctx[None, :], c], axis=0).astype(F32)
    cond = jnp.pad(cond, ((0, 16 - cond.shape[0]), (0, 0)))
    y_prompt, y_sample = x_prompt, x_sample
    ks_, vs_, cs_, ns_, ms_ = [], [], [], [], []
    for l in range(depth):
        p = dict(g_pre_mix=g_pre_mix[l], w_in=w_in[l], b_gates=b_gates[l], sink=attn_sink[l], norm_ml=norm_mlstm[l],
                 w_up_att=w_up_att[l], w_up_ml=w_up_ml[l], w_out=w_out[l], g_post_mix=g_post_mix[l],
                 g_pre_ffn=g_pre_ffn[l], w_router=w_router[l], b_router=b_router[l], w1=w1[l], b1=b1[l],
                 w2=w2[l], b2=b2[l], g_post_ffn=g_post_ffn[l])
        mod = _adaln(cond, w_ada[l], b_ada[l]).reshape(16, 6, D_MODEL)
        row_p = lambda b: 0
        bsz, seq = x_prompt.shape[:2]
        x1_p, gate_p, moe_p, pj, ml = _stream(y_prompt, mod, row_p, p, None, None, False, 256, 0, bsz)
        nat = pj["nat"]
        ks_.append(nat[..., :KV_W].reshape(bsz, seq, ATT_KV_HEADS, HEAD_DIM))
        vs_.append(nat[..., KV_W:].reshape(bsz, seq, ATT_KV_HEADS, HEAD_DIM))
        cs_.append(ml[2])
        ns_.append(ml[3])
        ms_.append(ml[4][..., 0])
        ctx_kv = (_dup_heads(cache_k[:, l], False), _dup_heads(cache_v[:, l], True))
        init = (state_C[:, l], state_n[:, l], state_m[:, l])
        half = n_dec // 2
        experts = (p["w1"], p["b1"].astype(F32), p["w2"], p["b2"].astype(F32))
        groups = []
        for b0 in (0, half):
            row_s = functools.partial(lambda b, off: b + off + 1, off=b0)
            sl = slice(b0, b0 + half)
            x1_s, gate_s, moe_s, _, _ = _stream(y_sample, mod, row_s, p, (ctx_kv[0][sl], ctx_kv[1][sl]),
                                                init, True, 512, b0, half)
            groups.append((x1_s, gate_s, moe_s, row_s))
        (x1_a, gate_a, moe_a, row_a), (x1_b, gate_b, moe_b, row_b) = groups
        yg_p, yg_a = _moe([moe_p[0], moe_a[0]], [moe_p[1], moe_a[1]], [moe_p[2], moe_a[2]], *experts)
        (yg_b,) = _moe([moe_b[0]], [moe_b[1]], [moe_b[2]], *experts)
        y_prompt = _final(x1_p, yg_p, gate_p, mod, row_p, p["g_post_ffn"], 256, 0, bsz, None)
        y_sample = _final(x1_a, yg_a, gate_a, mod, row_a, p["g_post_ffn"], 1024, 0, n_dec, None)
        y_sample = _final(x1_b, yg_b, gate_b, mod, row_b, p["g_post_ffn"], 1024, half, n_dec, y_sample)
    return (y_prompt, y_sample, jnp.stack(ks_, axis=1), jnp.stack(vs_, axis=1), jnp.stack(cs_, axis=1),
            jnp.stack(ns_, axis=1), jnp.stack(ms_, axis=1))
```

```python
import functools

import numpy as np
import jax
import jax.numpy as jnp
from jax import lax
from jax.experimental import pallas as pl
from jax.experimental.pallas import tpu as pltpu
from jax.experimental.pallas import tpu_sc as plsc

F32 = jnp.float32
BF16 = jnp.bfloat16

D_MODEL = 1024
GRID_W = 64
ATT_HEADS = 8
ATT_KV_HEADS = 2
ATT_GROUP = ATT_HEADS // ATT_KV_HEADS
HEAD_DIM = 64
BLOCK = 128
ROPE_THETA = 10000.0
AXIS_FREQS = HEAD_DIM // 4
ML_HEADS = 4
ML_DK = 128
ML_DV = 256
ML_CHUNK = 512
N_EXPERTS = 32
TOP_K = 4
D_FF = 1024
SWIGLU_ALPHA = 1.702
SWIGLU_LIMIT = 7.0
EPS = 1e-6
NEG = -1e30

Q_W = ATT_HEADS * HEAD_DIM
KV_W = ATT_KV_HEADS * HEAD_DIM
MLQK_W = ML_HEADS * ML_DK
MLV_W = ML_HEADS * ML_DV
GATE_W = 4 * ML_HEADS

LANES = 128
VMEM_LIMIT = 56 * 1024 * 1024
MOE_ROWS = 512
SC_CORES = 2
SC_SUBCORES = 16
SC_GATHER_CHUNK = 32


def _cparams(sem):
    return pltpu.CompilerParams(dimension_semantics=sem, vmem_limit_bytes=VMEM_LIMIT)


def _const_spec(shape):
    nd = len(shape)
    return pl.BlockSpec(shape, lambda *_: (0,) * nd)


def _adaln_kernel(cond_ref, w_ref, b_ref, o_ref):
    cnd = cond_ref[...]
    act = cnd * jax.nn.sigmoid(cnd)
    o_ref[...] = jnp.dot(act.astype(BF16), w_ref[...].astype(BF16),
                         preferred_element_type=F32) + b_ref[...]


def _adaln(cond, w_ada, b_ada):
    rows, d = cond.shape
    n = w_ada.shape[1]
    tn = 1536
    return pl.pallas_call(
        _adaln_kernel,
        grid=(n // tn,),
        in_specs=[_const_spec((rows, d)),
                  pl.BlockSpec((d, tn), lambda j: (0, j)),
                  pl.BlockSpec((1, tn), lambda j: (0, j))],
        out_specs=pl.BlockSpec((rows, tn), lambda j: (0, j)),
        out_shape=jax.ShapeDtypeStruct((rows, n), F32),
        compiler_params=_cparams(("parallel",)),
        name="adaln",
    )(cond, w_ada, b_ada.reshape(1, n))


def _pack_w_in(w_in, b_gates, rope, with_nat):
    o_q, o_k, o_v = 0, Q_W, Q_W + KV_W
    o_mq = Q_W + 2 * KV_W
    o_mk = o_mq + MLQK_W
    o_mv = o_mk + MLQK_W
    o_g = o_mv + MLV_W
    o_mo = o_g + GATE_W
    o_ga = o_mo + MLV_W
    o_gm = o_ga + D_MODEL
    half = HEAD_DIM // 2

    def head_cols(base, h, rot):
        lo = base + h * HEAD_DIM
        if rot:
            return [w_in[:, lo + half:lo + HEAD_DIM], w_in[:, lo:lo + half]]
        return [w_in[:, lo:lo + HEAD_DIM]]

    def q_cols(rot):
        cols = []
        for h in range(ATT_HEADS):
            cols += head_cols(o_q, h, rot)
        return cols

    def dup_cols(base, rot):
        cols = []
        for h in range(ATT_KV_HEADS):
            hc = head_cols(base, h, rot)
            cols += hc + hc
        return cols

    segs, cols, off = {}, [], 0

    def add(name, cl):
        nonlocal off
        width = sum(c.shape[1] for c in cl)
        segs[name] = (off, off + width)
        cols.extend(cl)
        off += width

    add("q", q_cols(False))
    if rope:
        add("qrot", q_cols(True))
    add("kd", dup_cols(o_k, False))
    if rope:
        add("kdrot", dup_cols(o_k, True))
    if with_nat:
        add("nat", [w_in[:, o_k:o_k + 2 * KV_W]])
    add("mq", [w_in[:, o_mq:o_mq + MLQK_W]])
    add("mv", [w_in[:, o_mv:o_mv + MLV_W]])
    add("mo", [w_in[:, o_mo:o_mo + MLV_W]])
    add("ga", [w_in[:, o_ga:o_ga + D_MODEL]])
    add("gm", [w_in[:, o_gm:o_gm + D_MODEL]])
    add("gates", [w_in[:, o_g:o_g + GATE_W], jnp.zeros((D_MODEL, LANES - GATE_W), w_in.dtype)])
    wp = jnp.concatenate(cols, axis=1).astype(BF16)
    wvd_t = jnp.concatenate(dup_cols(o_v, False) + [w_in[:, o_mk:o_mk + MLQK_W]], axis=1).T.astype(BF16)
    bg = jnp.pad(b_gates.astype(F32), (0, LANES - GATE_W)).reshape(1, LANES)
    return wp, wvd_t, bg, segs


def _rope_tables(n_tok):
    rows = n_tok // GRID_W
    row = np.repeat(np.arange(rows), GRID_W).astype(np.float32)
    col = np.tile(np.arange(GRID_W), rows).astype(np.float32)
    inv = (np.float32(ROPE_THETA) ** (-np.arange(AXIS_FREQS, dtype=np.float32) / AXIS_FREQS)).astype(np.float32)
    ang = np.concatenate([row[:, None] * inv, col[:, None] * inv], axis=-1).astype(np.float32)
    cos, sin = np.cos(ang), np.sin(ang)
    c64 = np.concatenate([cos, cos], axis=-1)
    s64 = np.concatenate([-sin, sin], axis=-1)
    return (jnp.asarray(np.tile(c64, (1, 2)), F32), jnp.asarray(np.tile(s64, (1, 2)), F32))


def _log_sigmoid(x):
    return jnp.minimum(x, 0.0) - jnp.log1p(jnp.exp(-jnp.abs(x)))


def _prefix_sum_rows(x):
    row = lax.broadcasted_iota(jnp.int32, x.shape, 0)
    k = 1
    while k < x.shape[0]:
        x = x + jnp.where(row >= k, pltpu.roll(x, k, 0), 0.0)
        k *= 2
    return x


def _proj_kernel(*refs, segs, rope, with_nat):
    it = iter(refs)
    x_ref, mod_ref, g_ref, w_ref, wvt_ref, bg_ref = (next(it) for _ in range(6))
    cos_ref = sin_ref = None
    if rope:
        cos_ref, sin_ref = next(it), next(it)
    q_ref, kd_ref, vd_ref = next(it), next(it), next(it)
    nat_ref = next(it) if with_nat else None
    mq_ref, mk_ref, mv_ref, gates_ref, gates_t_ref, mo_ref, ga_ref, gm_ref = (next(it) for _ in range(8))

    x = x_ref[0]
    shift = mod_ref[0, 0:1, :]
    scale = mod_ref[0, 1:2, :]
    ms = jnp.mean(x * x, axis=-1, keepdims=True)
    xn = x * lax.rsqrt(ms + EPS) * g_ref[...]
    xb = (xn * (1.0 + scale) + shift).astype(BF16)

    def seg(name):
        lo, hi = segs[name]
        return jnp.dot(xb, w_ref[:, lo:hi], preferred_element_type=F32)

    uq = seg("q")
    ukd = seg("kd")
    if rope:
        uqr = seg("qrot")
        ukr = seg("kdrot")
        cs = cos_ref[...]
        sn = sin_ref[...]
        for h in range(ATT_HEADS // 2):
            sl = slice(h * LANES, (h + 1) * LANES)
            q_ref[0, h] = ((uq[:, sl] * cs + uqr[:, sl] * sn) * HEAD_DIM ** -0.5).astype(BF16)
        for h in range(ATT_KV_HEADS):
            sl = slice(h * LANES, (h + 1) * LANES)
            kd_ref[0, :, sl] = (ukd[:, sl] * cs + ukr[:, sl] * sn).astype(BF16)
    else:
        for h in range(ATT_HEADS // 2):
            sl = slice(h * LANES, (h + 1) * LANES)
            q_ref[0, h] = (uq[:, sl] * HEAD_DIM ** -0.5).astype(BF16)
        kd_ref[0] = ukd.astype(BF16)
    ut = lax.dot_general(wvt_ref[...], xb, (((1,), (1,)), ((), ())), preferred_element_type=F32)
    vd_ref[0] = ut[:2 * LANES].astype(BF16)
    mk_ref[0] = ut[2 * LANES:].astype(BF16)
    if with_nat:
        nat_ref[0] = seg("nat")
    mq_ref[0] = (seg("mq") * ML_DK ** -0.5).astype(BF16)
    mv_ref[0] = seg("mv").astype(BF16)
    gates = seg("gates") + bg_ref[...]
    lf = _log_sigmoid(gates)
    csum = _prefix_sum_rows(lf)
    lane = lax.broadcasted_iota(jnp.int32, gates.shape, 1)
    cum = jnp.where(lane >= 2 * ML_HEADS, csum[-1:, :] - csum + lf, csum)
    gates_ref[0] = cum
    gates_t_ref[0, :GATE_W, :] = gates.T[:GATE_W, :]
    gates_t_ref[0, GATE_W:, :] = cum.T[:GATE_W, :]
    mo_ref[0] = _sigmoid(seg("mo")).astype(BF16)
    ga_ref[0] = _sigmoid(seg("ga")).astype(BF16)
    gm_ref[0] = _sigmoid(seg("gm")).astype(BF16)


def _project(x, mod, g_pre, w_in, b_gates, mod_row, rope, with_nat, tm, b0, bsz):
    _, seq, d = x.shape
    wp, wvd_t, bg, segs = _pack_w_in(w_in, b_gates, rope, with_nat)
    nw = wp.shape[1]
    tok3 = lambda b, i: (b, i, 0)
    in_specs = [pl.BlockSpec((1, tm, d), lambda b, i: (b + b0, i, 0)),
                pl.BlockSpec((1, 6, d), lambda b, i: (mod_row(b), 0, 0)),
                _const_spec((1, d)),
                pl.BlockSpec((d, nw), lambda b, i: (0, 0), pipeline_mode=pl.Buffered(1)),
                _const_spec((2 * LANES + MLQK_W, d)), _const_spec((1, LANES))]
    args = [x, mod, g_pre.reshape(1, d), wp, wvd_t, bg]
    if rope:
        cos_t, sin_t = _rope_tables(seq)
        in_specs += [pl.BlockSpec((tm, LANES), lambda b, i: (i, 0))] * 2
        args += [cos_t, sin_t]

    def tok_out(width, dtype):
        return (jax.ShapeDtypeStruct((bsz, seq, width), dtype), pl.BlockSpec((1, tm, width), tok3))

    outs = [(jax.ShapeDtypeStruct((bsz, ATT_HEADS // 2, seq, LANES), BF16),
             pl.BlockSpec((1, ATT_HEADS // 2, tm, LANES), lambda b, i: (b, 0, i, 0))),
            tok_out(2 * LANES, BF16),
            (jax.ShapeDtypeStruct((bsz, 2 * LANES, seq), BF16),
             pl.BlockSpec((1, 2 * LANES, tm), lambda b, i: (b, 0, i)))]
    if with_nat:
        outs.append(tok_out(2 * KV_W, F32))
    outs += [tok_out(MLQK_W, BF16),
             (jax.ShapeDtypeStruct((bsz, MLQK_W, seq), BF16),
              pl.BlockSpec((1, MLQK_W, tm), lambda b, i: (b, 0, i))),
             tok_out(MLV_W, BF16), tok_out(LANES, F32),
             (jax.ShapeDtypeStruct((bsz, 2 * GATE_W, seq), F32),
              pl.BlockSpec((1, 2 * GATE_W, tm), lambda b, i: (b, 0, i))),
             tok_out(MLV_W, BF16), tok_out(D_MODEL, BF16), tok_out(D_MODEL, BF16)]
    res = pl.pallas_call(
        functools.partial(_proj_kernel, segs=segs, rope=rope, with_nat=with_nat),
        grid=(bsz, seq // tm),
        in_specs=in_specs,
        out_specs=[o[1] for o in outs],
        out_shape=[o[0] for o in outs],
        compiler_params=_cparams(("parallel", "parallel")),
        name="proj_rope" if rope else "proj_ctx",
    )(*args)
    names = ["q", "kd", "vd"] + (["nat"] if with_nat else []) + ["mq", "mk", "mv", "gates", "gates_t", "mo", "ga", "gm"]
    return dict(zip(names, res))


def _attn_kernel(*refs, window, n_blocks, n_ctx):
    it = iter(refs)
    sink_ref, q_ref = next(it), next(it)
    if window:
        kp_ref, kc_ref, kn_ref, vp_ref, vc_ref, vn_ref = (next(it) for _ in range(6))
    ck_ref, cv_ref, o_ref = next(it), next(it), next(it)
    j = pl.program_id(1)
    n_win = 3 * BLOCK if window else 0
    cols = ATT_GROUP * BLOCK

    col_id = lax.broadcasted_iota(jnp.int32, (1, cols), 1)
    if window:
        kl = lax.broadcasted_iota(jnp.int32, (BLOCK, cols), 0)
        ql = lax.broadcasted_iota(jnp.int32, (BLOCK, cols), 1) & (BLOCK - 1)
        prev_ok = kl >= ql + jnp.where(j == 0, BLOCK, 0)
        next_ok = kl <= ql - jnp.where(j == n_blocks - 1, BLOCK, 0)
    lane_lo = lax.broadcasted_iota(jnp.int32, (BLOCK, LANES), 1) < HEAD_DIM
    lane_row = lax.broadcasted_iota(jnp.int32, (1, LANES), 1) < HEAD_DIM
    half_lo = jnp.where(lane_row, 1.0, 0.0).astype(BF16)
    half_hi = jnp.where(lane_row, 0.0, 1.0).astype(BF16)

    for kvh in range(ATT_KV_HEADS):
        sl = slice(kvh * LANES, (kvh + 1) * LANES)
        if window:
            keys = jnp.concatenate([kp_ref[0, :, sl], kc_ref[0, :, sl], kn_ref[0, :, sl], ck_ref[0, :, sl]], axis=0)
            vals_t = jnp.concatenate([vp_ref[0, sl, :], vc_ref[0, sl, :], vn_ref[0, sl, :], cv_ref[0, sl, :]], axis=1)
        else:
            keys = ck_ref[0, :, sl]
            vals_t = cv_ref[0, sl, :]
        qs = jnp.concatenate([q_ref[0, kvh * (ATT_GROUP // 2) + g // 2] * (half_lo if g % 2 == 0 else half_hi)
                              for g in range(ATT_GROUP)], axis=0)
        s = lax.dot_general(keys, qs, (((1,), (1,)), ((), ())), preferred_element_type=F32)
        if window:
            s = jnp.concatenate([jnp.where(prev_ok, s[:BLOCK], NEG), s[BLOCK:2 * BLOCK],
                                 jnp.where(next_ok, s[2 * BLOCK:n_win], NEG), s[n_win:]], axis=0)
        snk = jnp.full((1, cols), sink_ref[kvh * ATT_GROUP + ATT_GROUP - 1], F32)
        for g in range(ATT_GROUP - 2, -1, -1):
            snk = jnp.where(col_id < (g + 1) * BLOCK, sink_ref[kvh * ATT_GROUP + g], snk)
        m = jnp.maximum(jnp.max(s, axis=0, keepdims=True), snk)
        p = jnp.exp(s - m)
        den = jnp.sum(p, axis=0, keepdims=True) + jnp.exp(snk - m)
        o_t = jnp.dot(vals_t, p.astype(BF16), preferred_element_type=F32) / den
        for pair in range(ATT_GROUP // 2):
            even = o_t[:, (2 * pair) * BLOCK:(2 * pair + 1) * BLOCK].T
            odd = o_t[:, (2 * pair + 1) * BLOCK:(2 * pair + 2) * BLOCK].T
            col = (kvh * (ATT_GROUP // 2) + pair) * LANES
            o_ref[0, :, col:col + LANES] = jnp.where(lane_lo, even, odd).astype(BF16)


def _attention(q, kd, vd_t, ckd, cvd_t, sink, window):
    bsz, _, seq, _ = q.shape
    nb = seq // BLOCK
    n_ctx = ckd.shape[1]
    in_specs = [pl.BlockSpec(memory_space=pltpu.SMEM),
                pl.BlockSpec((1, ATT_HEADS // 2, BLOCK, LANES), lambda b, j: (b, 0, j, 0))]
    args = [sink.astype(F32), q]
    if window:
        in_specs += [pl.BlockSpec((1, BLOCK, 2 * LANES), lambda b, j: (b, jnp.maximum(j - 1, 0), 0)),
                     pl.BlockSpec((1, BLOCK, 2 * LANES), lambda b, j: (b, j, 0)),
                     pl.BlockSpec((1, BLOCK, 2 * LANES), lambda b, j: (b, jnp.minimum(j + 1, nb - 1), 0)),
                     pl.BlockSpec((1, 2 * LANES, BLOCK), lambda b, j: (b, 0, jnp.maximum(j - 1, 0))),
                     pl.BlockSpec((1, 2 * LANES, BLOCK), lambda b, j: (b, 0, j)),
                     pl.BlockSpec((1, 2 * LANES, BLOCK), lambda b, j: (b, 0, jnp.minimum(j + 1, nb - 1)))]
        args += [kd, kd, kd, vd_t, vd_t, vd_t]
    in_specs += [pl.BlockSpec((1, n_ctx, 2 * LANES), lambda b, j: (b, 0, 0)),
                 pl.BlockSpec((1, 2 * LANES, n_ctx), lambda b, j: (b, 0, 0))]
    args += [ckd, cvd_t]
    return pl.pallas_call(
        functools.partial(_attn_kernel, window=window, n_blocks=nb, n_ctx=n_ctx),
        grid=(bsz, nb),
        in_specs=in_specs,
        out_specs=pl.BlockSpec((1, BLOCK, Q_W), lambda b, j: (b, j, 0)),
        out_shape=jax.ShapeDtypeStruct((bsz, seq, Q_W), BF16),
        compiler_params=_cparams(("parallel", "parallel")),
        name="attn_window" if window else "attn_ctx",
    )(*args)


def _dup_heads(t, transpose):
    b, l = t.shape[:2]
    out = jnp.concatenate([t, t], axis=-1).reshape(b, l, ATT_KV_HEADS * LANES).astype(BF16)
    return jnp.swapaxes(out, 1, 2) if transpose else out


def _mlstm_kernel(*refs, chunk_len, n_chunks, has_init, emit_state):
    it = iter(refs)
    qkvg = [[next(it) for _ in range(5)] for _ in range(2)]
    if has_init:
        c0_ref, n0_ref, m0_ref = next(it), next(it), next(it)
    h_refs = [next(it), next(it)]
    if emit_state:
        co_ref, no_ref, mo_ref = next(it), next(it), next(it)
    st_s, m_s = next(it), next(it)
    t = pl.program_id(1)
    L = chunk_len

    @pl.when(t == 0)
    def _():
        for d in range(2):
            for h in range(ML_HEADS):
                if has_init:
                    st_s[d, h, :, :ML_DV] = c0_ref[0, d, h].T
                    st_s[d, h, :, ML_DV:] = jnp.broadcast_to(n0_ref[0, d, h:h + 1, :], (LANES, ML_DK)).T
                    m_s[d, h] = m0_ref[0, d, h:h + 1, :]
                else:
                    st_s[d, h] = jnp.zeros((ML_DK, ML_DV + LANES), F32)
                    m_s[d, h] = jnp.zeros((1, LANES), F32)

    ri = lax.broadcasted_iota(jnp.int32, (L, L), 0)
    ci = lax.broadcasted_iota(jnp.int32, (L, L), 1)
    ones_blk = jnp.ones((L, LANES), BF16)

    for d in range(2):
        q_ref, kt_ref, v_ref, cum_ref, rows_ref = qkvg[d]
        tri = (ci <= ri) if d == 0 else (ci >= ri)
        bc = cum_ref[0]
        g_t = rows_ref[0, :GATE_W, :]
        br = rows_ref[0, GATE_W:, :]
        last = L - 1 if d == 0 else 0

        for h in range(ML_HEADS):
            icol_i = 2 * ML_HEADS * d + h
            fcol = icol_i + ML_HEADS
            bcol = bc[:, fcol:fcol + 1]
            brow = br[fcol:fcol + 1, :]
            irow = g_t[icol_i:icol_i + 1, :]
            b_last = bc[last:last + 1, fcol:fcol + 1]
            m_prev = m_s[d, h][:, 0:1]
            st_prev = st_s[d, h]
            qh = q_ref[0, :, h * ML_DK:(h + 1) * ML_DK]
            kt = kt_ref[0, h * ML_DK:(h + 1) * ML_DK, :]
            v_aug = jnp.concatenate([v_ref[0, :, h * ML_DV:(h + 1) * ML_DV], ones_blk], axis=1)

            rel = jnp.where(tri, irow - brow, NEG)
            inter = bcol + m_prev
            m_t = jnp.maximum(inter, bcol + jnp.max(rel, axis=-1, keepdims=True))
            w = jnp.exp(rel + (bcol - m_t))
            w_inter = jnp.exp(inter - m_t)
            s = jnp.dot(qh, kt, preferred_element_type=F32) * w
            lhs = jnp.concatenate([s.astype(BF16), (qh.astype(F32) * w_inter).astype(BF16)], axis=1)
            rhs = jnp.concatenate([v_aug, st_prev.astype(BF16)], axis=0)
            mix = jnp.dot(lhs, rhs, preferred_element_type=F32)
            inv = 1.0 / jnp.maximum(jnp.abs(mix[:, ML_DV:]), jnp.exp(-m_t))
            for c in range(ML_DV // LANES):
                lo = h * ML_DV + c * LANES
                h_refs[d][0, :, lo:lo + LANES] = mix[:, c * LANES:(c + 1) * LANES] * inv

            a = b_last - brow + irow
            m_new = jnp.maximum(b_last + m_prev, jnp.max(a, axis=-1, keepdims=True))
            wk = jnp.exp(a - m_new)
            decay = jnp.exp(b_last + m_prev - m_new)
            kw_t = (kt.astype(F32) * wk).astype(BF16)
            st_new = decay * st_prev + jnp.dot(kw_t, v_aug, preferred_element_type=F32)
            st_s[d, h] = st_new
            m_s[d, h] = jnp.broadcast_to(m_new, (1, LANES))
            if emit_state:
                @pl.when(t == n_chunks - 1)
                def _():
                    co_ref[0, d, h] = st_new[:, :ML_DV].T
                    no_ref[0, d, h:h + 1, :] = st_new[:, ML_DV:].T[0:1, :]
                    mo_ref[0, d, h:h + 1, :] = jnp.broadcast_to(m_new, (1, LANES))


def _mlstm(mq, mk_t, mv, gate_cum, gate_rows, init, emit_state, b0=0):
    bsz, seq, _ = mq.shape
    chunk_len = min(ML_CHUNK, seq)
    nc = seq // chunk_len
    fwd = lambda b, t: (b, t, 0)
    bwd = lambda b, t: (b, nc - 1 - t, 0)
    fwd_t = lambda b, t: (b, 0, t)
    bwd_t = lambda b, t: (b, 0, nc - 1 - t)
    state5 = lambda b, t: (b, 0, 0, 0, 0)
    state4 = lambda b, t: (b, 0, 0, 0)
    init5 = lambda b, t: (b + b0, 0, 0, 0, 0)
    init4 = lambda b, t: (b + b0, 0, 0, 0)
    in_specs, args = [], []
    for chunk, chunk_t in ((fwd, fwd_t), (bwd, bwd_t)):
        in_specs += [pl.BlockSpec((1, chunk_len, MLQK_W), chunk), pl.BlockSpec((1, MLQK_W, chunk_len), chunk_t),
                     pl.BlockSpec((1, chunk_len, MLV_W), chunk), pl.BlockSpec((1, chunk_len, LANES), chunk),
                     pl.BlockSpec((1, 2 * GATE_W, chunk_len), chunk_t)]
        args += [mq, mk_t, mv, gate_cum, gate_rows]
    if init is not None:
        c0, n0, m0 = init
        in_specs += [pl.BlockSpec((1, 2, ML_HEADS, ML_DV, ML_DK), init5),
                     pl.BlockSpec((1, 2, ML_HEADS, ML_DK), init4),
                     pl.BlockSpec((1, 2, ML_HEADS, LANES), init4)]
        args += [c0.astype(F32), n0.astype(F32),
                 jnp.broadcast_to(m0.astype(F32)[..., None], m0.shape + (LANES,))]
    out_shape = [jax.ShapeDtypeStruct((bsz, seq, MLV_W), F32)] * 2
    out_specs = [pl.BlockSpec((1, chunk_len, MLV_W), fwd), pl.BlockSpec((1, chunk_len, MLV_W), bwd)]
    if emit_state:
        out_shape += [jax.ShapeDtypeStruct((bsz, 2, ML_HEADS, ML_DV, ML_DK), F32),
                      jax.ShapeDtypeStruct((bsz, 2, ML_HEADS, ML_DK), F32),
                      jax.ShapeDtypeStruct((bsz, 2, ML_HEADS, LANES), F32)]
        out_specs += [pl.BlockSpec((1, 2, ML_HEADS, ML_DV, ML_DK), state5),
                      pl.BlockSpec((1, 2, ML_HEADS, ML_DK), state4),
                      pl.BlockSpec((1, 2, ML_HEADS, LANES), state4)]
    return pl.pallas_call(
        functools.partial(_mlstm_kernel, chunk_len=chunk_len, n_chunks=nc, has_init=init is not None, emit_state=emit_state),
        grid=(bsz, nc),
        in_specs=in_specs,
        out_specs=out_specs,
        out_shape=out_shape,
        scratch_shapes=[pltpu.VMEM((2, ML_HEADS, ML_DK, ML_DV + LANES), F32),
                        pltpu.VMEM((2, ML_HEADS, 1, LANES), F32)],
        compiler_params=_cparams(("parallel", "arbitrary")),
        name="mlstm_state" if emit_state else "mlstm",
    )(*args)


def _sigmoid(x):
    return 0.5 * jnp.tanh(0.5 * x) + 0.5


def _rms(x, g):
    return x * lax.rsqrt(jnp.mean(x * x, axis=-1, keepdims=True) + EPS) * g


def _pack_bf16_pairs(x):
    n = x.shape[1] // 2
    lo = pltpu.bitcast(x[:, :n].astype(BF16).astype(F32), jnp.int32)
    hi = pltpu.bitcast(x[:, n:].astype(BF16).astype(F32), jnp.int32)
    return lax.shift_right_logical(lo, 16) | hi


def _unpack_bf16_pairs(p):
    lo = pltpu.bitcast(lax.shift_left(p, 16), F32)
    hi = pltpu.bitcast(p & jnp.int32(-65536), F32)
    return lo, hi


def _merge_kernel(att_ref, hf_ref, hb_ref, mo_ref, ga_ref, gm_ref, x_ref, mod_ref, nml_ref, gpm_ref, gpf_ref,
                  wua_ref, wum_ref, wo_ref, wr_ref, br_ref, x1_ref, h2_ref, idx_ref, gate_ref, cnt_ref):
    hsum = hf_ref[0] + hb_ref[0]
    parts = []
    for h in range(ML_HEADS):
        hh = hsum[:, h * ML_DV:(h + 1) * ML_DV]
        parts.append(hh * lax.rsqrt(jnp.mean(hh * hh, axis=-1, keepdims=True) + EPS))
    hn = jnp.concatenate(parts, axis=-1) * nml_ref[...]
    ml = (hn * mo_ref[0].astype(F32)).astype(BF16)
    a = jnp.dot(att_ref[0], wua_ref[...], preferred_element_type=F32)
    m = jnp.dot(ml, wum_ref[...], preferred_element_type=F32)
    z = (ga_ref[0].astype(F32) * a + gm_ref[0].astype(F32) * m).astype(BF16)
    mix = jnp.dot(z, wo_ref[...], preferred_element_type=F32)
    gate1 = mod_ref[0, 2:3, :]
    shift2 = mod_ref[0, 3:4, :]
    scale2 = mod_ref[0, 4:5, :]
    x1 = x_ref[0] + gate1 * _rms(mix, gpm_ref[...])
    x1_ref[0] = x1
    h2 = _rms(x1, gpf_ref[...]) * (1.0 + scale2) + shift2
    h2_ref[0] = _pack_bf16_pairs(h2)

    h2_hi = h2.astype(BF16)
    h2_lo = (h2 - h2_hi.astype(F32)).astype(BF16)
    wr = wr_ref[...]
    wr_hi = wr.astype(BF16)
    wr_lo = (wr - wr_hi.astype(F32)).astype(BF16)
    hi_terms = jnp.dot(h2_hi, jnp.concatenate([wr_hi, wr_lo], axis=1), preferred_element_type=F32)
    logits = (hi_terms[:, :LANES] + hi_terms[:, LANES:]
              + jnp.dot(h2_lo, wr_hi, preferred_element_type=F32)) + br_ref[...]
    lane = lax.broadcasted_iota(jnp.int32, logits.shape, 1).astype(F32)
    work = jnp.where(lane < N_EXPERTS, logits, -jnp.inf)
    idx_out = jnp.zeros(logits.shape, F32)
    val_out = jnp.zeros(logits.shape, F32)
    picked = jnp.zeros(logits.shape, F32)
    top0 = None
    esum = None
    for k in range(TOP_K):
        mx = jnp.max(work, axis=-1, keepdims=True)
        sel = jnp.min(jnp.where(work == mx, lane, float(LANES)), axis=-1, keepdims=True)
        if k == 0:
            top0 = mx
        e = jnp.exp(mx - top0)
        esum = e if k == 0 else esum + e
        idx_out = jnp.where(lane == k, sel, idx_out)
        val_out = jnp.where(lane == k, e, val_out)
        picked = jnp.where(lane == sel, 1.0, picked)
        work = jnp.where(lane == sel, -jnp.inf, work)
    idx_ref[0] = idx_out.astype(jnp.int32)
    gate_ref[0] = val_out / esum
    cnt_ref[0] = jnp.sum(picked, axis=0, keepdims=True)


def _merge(att, hf, hb, pj, x, mod, mod_row, p, tm, b0):
    bsz = att.shape[0]
    _, seq, d = x.shape
    tok3 = lambda b, i: (b, i, 0)
    wr = jnp.pad(p["w_router"].astype(F32), ((0, 0), (0, LANES - N_EXPERTS)))
    br = jnp.pad(p["b_router"].astype(F32), (0, LANES - N_EXPERTS)).reshape(1, LANES)
    row = lambda v: v.astype(F32).reshape(1, -1)
    in_specs = [pl.BlockSpec((1, tm, Q_W), tok3),
                pl.BlockSpec((1, tm, MLV_W), tok3), pl.BlockSpec((1, tm, MLV_W), tok3),
                pl.BlockSpec((1, tm, MLV_W), tok3), pl.BlockSpec((1, tm, d), tok3), pl.BlockSpec((1, tm, d), tok3),
                pl.BlockSpec((1, tm, d), lambda b, i: (b + b0, i, 0)),
                pl.BlockSpec((1, 6, d), lambda b, i: (mod_row(b), 0, 0)),
                _const_spec((1, MLV_W)), _const_spec((1, d)), _const_spec((1, d)),
                _const_spec((Q_W, d)), _const_spec((MLV_W, d)), _const_spec((d, d)),
                _const_spec((d, LANES)), _const_spec((1, LANES))]
    out_shape = [jax.ShapeDtypeStruct((bsz, seq, d), F32), jax.ShapeDtypeStruct((bsz, seq, d // 2), jnp.int32),
                 jax.ShapeDtypeStruct((bsz, seq, LANES), jnp.int32), jax.ShapeDtypeStruct((bsz, seq, LANES), F32),
                 jax.ShapeDtypeStruct((bsz * (seq // tm), 1, LANES), F32)]
    out_specs = [pl.BlockSpec((1, tm, d), tok3), pl.BlockSpec((1, tm, d // 2), tok3),
                 pl.BlockSpec((1, tm, LANES), tok3), pl.BlockSpec((1, tm, LANES), tok3),
                 pl.BlockSpec((1, 1, LANES), lambda b, i: (b * (seq // tm) + i, 0, 0))]
    return pl.pallas_call(
        _merge_kernel,
        grid=(bsz, seq // tm),
        in_specs=in_specs,
        out_specs=out_specs,
        out_shape=out_shape,
        compiler_params=_cparams(("parallel", "parallel")),
        name="merge_router",
    )(att, hf, hb, pj["mo"], pj["ga"], pj["gm"], x, mod, row(p["norm_ml"]), row(p["g_post_mix"]),
      row(p["g_pre_ffn"]), p["w_up_att"].astype(BF16), p["w_up_ml"].astype(BF16), p["w_out"].astype(BF16), wr, br)


def _sc_gather_rows(table, idx):
    n = idx.shape[0]
    width = table.shape[1]
    chunk = SC_GATHER_CHUNK
    n_workers = SC_CORES * SC_SUBCORES
    rows_per_worker = n // n_workers
    n_pairs = rows_per_worker // (2 * chunk)
    assert n_pairs * 2 * chunk * n_workers == n
    mesh = plsc.VectorSubcoreMesh(core_axis_name="c", subcore_axis_name="s")

    def body(table_hbm, idx_hbm, out_hbm, idx_v, rows_v, gsem, osem):
        wid = lax.axis_index("s") * SC_CORES + lax.axis_index("c")
        base0 = wid * rows_per_worker

        def gather_copy(b):
            return pltpu.make_async_copy(table_hbm.at[idx_v.at[b]], rows_v.at[b], gsem.at[b])

        def write_copy(ci, b):
            base = pl.multiple_of(base0 + ci * chunk, 8)
            return pltpu.make_async_copy(rows_v.at[b], out_hbm.at[pl.ds(base, chunk)], osem.at[b])

        def issue(ci, b):
            base = pl.multiple_of(base0 + ci * chunk, 8)
            pltpu.sync_copy(idx_hbm.at[pl.ds(base, chunk)], idx_v.at[b])
            gather_copy(b).start()

        def finish(ci, b):
            gather_copy(b).wait()
            write_copy(ci, b).start()

        issue(0, 0)

        @pl.loop(0, n_pairs)
        def _(j):
            @pl.when(j > 0)
            def _():
                write_copy(2 * j - 1, 1).wait()
            issue(2 * j + 1, 1)
            finish(2 * j, 0)

            @pl.when(j < n_pairs - 1)
            def _():
                write_copy(2 * j, 0).wait()
                issue(2 * j + 2, 0)
            finish(2 * j + 1, 1)

        write_copy(2 * n_pairs - 2, 0).wait()
        write_copy(2 * n_pairs - 1, 1).wait()

    return pl.kernel(
        body, mesh=mesh,
        out_type=jax.ShapeDtypeStruct((n, width), table.dtype),
        scratch_types=[pltpu.VMEM((2, chunk), jnp.int32), pltpu.VMEM((2, chunk, width), table.dtype),
                       pltpu.SemaphoreType.DMA((2,)), pltpu.SemaphoreType.DMA((2,))],
    )(table, idx)


def _ffn_kernel(be_ref, nu_ref, xs_ref, w1_ref, b1_ref, w2_ref, b2_ref, ys_ref, *bf16_refs, weights_are_bf16):
    i = pl.program_id(0)
    e = be_ref[i]
    e_prev = be_ref[jnp.maximum(i - 1, 0)]
    if weights_are_bf16:
        w1b_ref, w2b_ref = w1_ref, w2_ref
    else:
        w1b_ref, w2b_ref = bf16_refs

        @pl.when((i == 0) | (e != e_prev))
        def _():
            w1b_ref[0] = w1_ref[0].astype(BF16)
            w2b_ref[0] = w2_ref[0].astype(BF16)

    @pl.when(i < nu_ref[0])
    def _():
        x_lo, x_hi = _unpack_bf16_pairs(xs_ref[...])
        xb = jnp.concatenate([x_lo.astype(BF16), x_hi.astype(BF16)], axis=1)
        hmid = jnp.dot(xb, w1b_ref[0], preferred_element_type=F32) + b1_ref[0]
        glu = jnp.minimum(hmid[:, :D_FF], SWIGLU_LIMIT)
        lin = jnp.clip(hmid[:, D_FF:], -SWIGLU_LIMIT, SWIGLU_LIMIT)
        act = ((lin + 1.0) * glu * _sigmoid(SWIGLU_ALPHA * glu)).astype(BF16)
        y = jnp.dot(act, w2b_ref[0], preferred_element_type=F32) + b2_ref[0]
        ys_ref[...] = _pack_bf16_pairs(y)

    @pl.when(i >= nu_ref[0])
    def _():
        ys_ref[...] = jnp.zeros(ys_ref.shape, jnp.int32)


def _sc_scatter_rows(tables, positions, n_rows):
    width = tables[0].shape[1]
    chunk = SC_GATHER_CHUNK
    n_workers = SC_CORES * SC_SUBCORES
    n_streams = len(tables)
    mesh = plsc.VectorSubcoreMesh(core_axis_name="c", subcore_axis_name="s")

    def body(*refs):
        table_refs = refs[0:2 * n_streams:2]
        pos_refs = refs[1:2 * n_streams:2]
        out_hbm, idx_v, rows_v, lsem, ssem = refs[2 * n_streams:]
        wid = lax.axis_index("s") * SC_CORES + lax.axis_index("c")

        for table_hbm, pos_hbm in zip(table_refs, pos_refs):
            tok_per_worker = table_hbm.shape[0] // n_workers
            n_pairs = tok_per_worker // (2 * chunk)
            assert n_pairs * 2 * chunk * n_workers == table_hbm.shape[0]
            base0 = wid * tok_per_worker

            def load_copy(ci, b):
                base = pl.multiple_of(base0 + ci * chunk, 8)
                return pltpu.make_async_copy(table_hbm.at[pl.ds(base, chunk)], rows_v.at[b], lsem.at[b])

            def scatter_copy(b, k):
                return pltpu.make_async_copy(rows_v.at[b], out_hbm.at[idx_v.at[b, k]], ssem.at[b])

            def load(ci, b):
                base = pl.multiple_of(base0 + ci * chunk, 8)
                load_copy(ci, b).start()
                for k in range(TOP_K):
                    pltpu.sync_copy(pos_hbm.at[k, pl.ds(base, chunk)], idx_v.at[b, k])

            def scatter(ci, b):
                load_copy(ci, b).wait()
                for k in range(TOP_K):
                    scatter_copy(b, k).start()

            def drain(b):
                for k in range(TOP_K):
                    scatter_copy(b, k).wait()

            load(0, 0)

            @pl.loop(0, n_pairs)
            def _(j):
                @pl.when(j > 0)
                def _():
                    drain(1)
                load(2 * j + 1, 1)
                scatter(2 * j, 0)

                @pl.when(j < n_pairs - 1)
                def _():
                    drain(0)
                    load(2 * j + 2, 0)
                scatter(2 * j + 1, 1)

            drain(0)
            drain(1)

    args = [a for pair in zip(tables, positions) for a in pair]
    return pl.kernel(
        body, mesh=mesh,
        out_type=jax.ShapeDtypeStruct((n_rows, width), tables[0].dtype),
        scratch_types=[pltpu.VMEM((2, TOP_K, chunk), jnp.int32), pltpu.VMEM((2, chunk, width), tables[0].dtype),
                       pltpu.SemaphoreType.DMA((2,)), pltpu.SemaphoreType.DMA((2,))],
    )(*args)


def _pos_kernel(idx_ref, base_ref, pos_ref):
    idx = idx_ref[...]
    rows = idx.shape[0]
    lane = lax.broadcasted_iota(jnp.int32, idx.shape, 1)
    hots = [jnp.where(lane == idx[:, k:k + 1], 1.0, 0.0) for k in range(TOP_K)]
    cnt = hots[0] + hots[1] + hots[2] + hots[3]
    ri = lax.broadcasted_iota(jnp.int32, (rows, rows), 0)
    ci = lax.broadcasted_iota(jnp.int32, (rows, rows), 1)
    earlier = jnp.where(ci < ri, 1.0, 0.0).astype(BF16)
    ahead = jnp.dot(earlier, cnt.astype(BF16), preferred_element_type=F32) + base_ref[0]
    posmat = jnp.zeros(idx.shape, F32)
    for k in range(TOP_K):
        posmat = jnp.where(lane == k, jnp.sum(hots[k] * ahead, axis=-1, keepdims=True), posmat)
    pos_ref[...] = posmat.T[:TOP_K, :].astype(jnp.int32)


def _route(idx_list, count_list, every_expert):
    tiles = [c.shape[0] for c in count_list]
    tile_tok = [idx.size // LANES // n for idx, n in zip(idx_list, tiles)]
    n_tok = sum(n * sz for n, sz in zip(tiles, tile_tok))
    n_blocks = n_tok * TOP_K // MOE_ROWS + N_EXPERTS
    cnt = jnp.concatenate([c.reshape(-1, LANES) for c in count_list], axis=0).astype(jnp.int32)
    tile_off = jnp.cumsum(cnt, axis=0) - cnt
    total = jnp.sum(cnt, axis=0)
    padded = (total + MOE_ROWS - 1) // MOE_ROWS * MOE_ROWS
    if every_expert:
        padded = jnp.where(jnp.arange(LANES) < N_EXPERTS, jnp.maximum(padded, MOE_ROWS), 0)
    pend = jnp.cumsum(padded)
    base = ((pend - padded)[None, :] + tile_off).astype(F32)
    starts = jnp.arange(n_blocks, dtype=jnp.int32) * MOE_ROWS
    block_e = jnp.minimum(jnp.sum(pend[None, :N_EXPERTS] <= starts[:, None], axis=1), N_EXPERTS - 1).astype(jnp.int32)
    n_used = (pend[N_EXPERTS - 1] // MOE_ROWS).astype(jnp.int32).reshape(1)
    positions, first = [], 0
    for idx, n_tiles, tile in zip(idx_list, tiles, tile_tok):
        stream_tok = n_tiles * tile
        positions.append(pl.pallas_call(
            _pos_kernel,
            grid=(n_tiles,),
            in_specs=[pl.BlockSpec((tile, LANES), lambda i: (i, 0)),
                      pl.BlockSpec((1, 1, LANES), lambda i: (i, 0, 0))],
            out_specs=pl.BlockSpec((TOP_K, tile), lambda i: (0, i)),
            out_shape=jax.ShapeDtypeStruct((TOP_K, stream_tok), jnp.int32),
            compiler_params=_cparams(("parallel",)),
            name="route_pos",
        )(idx.reshape(stream_tok, LANES), base[first:first + n_tiles].reshape(n_tiles, 1, LANES)))
        first += n_tiles
    return block_e, n_used, positions


def _moe(h2p_list, idx_list, count_list, w1, b1, w2, b2):
    half = h2p_list[0].shape[1]
    d = 2 * half
    cast_here = w1.dtype != BF16
    block_e, n_used, positions = _route(idx_list, count_list, every_expert=cast_here)
    n_blocks = block_e.shape[0]
    xs = _sc_scatter_rows(h2p_list, positions, n_blocks * MOE_ROWS)
    grid_spec = pltpu.PrefetchScalarGridSpec(
        num_scalar_prefetch=2,
        grid=(n_blocks,),
        in_specs=[pl.BlockSpec((MOE_ROWS, half), lambda i, be, nu: (i, 0)),
                  pl.BlockSpec((1, d, 2 * D_FF), lambda i, be, nu: (be[i], 0, 0)),
                  pl.BlockSpec((1, 1, 2 * D_FF), lambda i, be, nu: (be[i], 0, 0)),
                  pl.BlockSpec((1, D_FF, d), lambda i, be, nu: (be[i], 0, 0)),
                  pl.BlockSpec((1, 1, d), lambda i, be, nu: (be[i], 0, 0))],
        out_specs=[pl.BlockSpec((MOE_ROWS, half), lambda i, be, nu: (i, 0))] + (
            [pl.BlockSpec((1, d, 2 * D_FF), lambda i, be, nu: (be[i], 0, 0)),
             pl.BlockSpec((1, D_FF, d), lambda i, be, nu: (be[i], 0, 0))] if cast_here else []),
    )
    out_shape = [jax.ShapeDtypeStruct((n_blocks * MOE_ROWS, half), jnp.int32)]
    if cast_here:
        out_shape += [jax.ShapeDtypeStruct(w1.shape, BF16), jax.ShapeDtypeStruct(w2.shape, BF16)]
    res = pl.pallas_call(
        functools.partial(_ffn_kernel, weights_are_bf16=not cast_here),
        grid_spec=grid_spec,
        out_shape=out_shape,
        compiler_params=_cparams(("arbitrary",)),
        name="moe_ffn_cast" if cast_here else "moe_ffn",
    )(block_e, n_used, xs, w1, b1.reshape(N_EXPERTS, 1, -1), w2, b2.reshape(N_EXPERTS, 1, -1))
    w1b, w2b = (res[1], res[2]) if cast_here else (w1, w2)
    return [_sc_gather_rows(res[0], pos.reshape(-1)) for pos in positions], w1b, w2b


def _final_kernel(x1_ref, y0_ref, y1_ref, y2_ref, y3_ref, gate_ref, mod_ref, g_ref, *rest):
    o_ref = rest[-1]
    gates = gate_ref[0]
    lo = hi = None
    for k, y_ref in enumerate((y0_ref, y1_ref, y2_ref, y3_ref)):
        y_lo, y_hi = _unpack_bf16_pairs(y_ref[0, 0])
        gk = gates[:, k:k + 1]
        lo = gk * y_lo if k == 0 else lo + gk * y_lo
        hi = gk * y_hi if k == 0 else hi + gk * y_hi
    y = jnp.concatenate([lo, hi], axis=-1)
    gate2 = mod_ref[0, 5:6, :]
    o_ref[0] = x1_ref[0] + gate2 * _rms(y, g_ref[...])


def _final(x1, yg, gate, mod, mod_row, g_post_ffn, tm, b0, n_batch, out_prev):
    bsz, seq, d = x1.shape
    tok3 = lambda b, i: (b, i, 0)
    yg = yg.reshape(TOP_K, bsz, seq, d // 2)
    slot_specs = [pl.BlockSpec((1, 1, tm, d // 2), functools.partial(lambda b, i, k: (k, b, i, 0), k=k))
                  for k in range(TOP_K)]
    in_specs = ([pl.BlockSpec((1, tm, d), tok3)] + slot_specs +
                [pl.BlockSpec((1, tm, LANES), tok3),
                 pl.BlockSpec((1, 6, d), lambda b, i: (mod_row(b), 0, 0)), _const_spec((1, d))])
    args = [x1, yg, yg, yg, yg, gate, mod, g_post_ffn.astype(F32).reshape(1, d)]
    aliases = {}
    if out_prev is not None:
        in_specs.append(pl.BlockSpec(memory_space=pl.ANY))
        args.append(out_prev)
        aliases = {len(args) - 1: 0}
    return pl.pallas_call(
        _final_kernel,
        grid=(bsz, seq // tm),
        in_specs=in_specs,
        out_specs=pl.BlockSpec((1, tm, d), lambda b, i: (b + b0, i, 0)),
        out_shape=jax.ShapeDtypeStruct((n_batch, seq, d), F32),
        input_output_aliases=aliases,
        compiler_params=_cparams(("parallel", "parallel")),
        name="final_residual",
    )(*args)


def _stream(x, mod, mod_row, p, ctx_kv, init_state, rope, tm, b0, bsz):
    _, seq, d = x.shape
    is_ctx = ctx_kv is None
    pj = _project(x, mod, p["g_pre_mix"], p["w_in"], p["b_gates"], mod_row, rope, is_ctx, tm, b0, bsz)
    if is_ctx:
        att = _attention(pj["q"], None, None, pj["kd"], pj["vd"], p["sink"], window=False)
    else:
        att = _attention(pj["q"], pj["kd"], pj["vd"], ctx_kv[0], ctx_kv[1], p["sink"], window=True)
    assert min(ML_CHUNK, seq) == tm
    ml = _mlstm(pj["mq"], pj["mk"], pj["mv"], pj["gates"], pj["gates_t"], init_state, emit_state=is_ctx, b0=b0)
    x1, h2, idx, gate, tile_counts = _merge(att, ml[0], ml[1], pj, x, mod, mod_row, p, tm, b0)
    moe_in = (h2.reshape(bsz * seq, d // 2), idx, tile_counts)
    return x1, gate, moe_in, pj, ml


def kernel(x_prompt, x_sample, c, cache_k, cache_v, state_C, state_n, state_m, c_ctx, w_ada, b_ada, g_pre_mix,
           w_in, b_gates, attn_sink, norm_mlstm, w_up_att, w_up_ml, w_out, g_post_mix, g_pre_ffn, w_router,
           b_router, w1, b1, w2, b2, g_post_ffn):
    depth = w_ada.shape[0]
    n_dec = c.shape[0]
    cond = jnp.concatenate([c_ctx[None, :], c], axis=0).astype(F32)
    cond = jnp.pad(cond, ((0, 16 - cond.shape[0]), (0, 0)))
    y_prompt, y_sample = x_prompt, x_sample
    ks_, vs_, cs_, ns_, ms_ = [], [], [], [], []
    for l in range(depth):
        p = dict(g_pre_mix=g_pre_mix[l], w_in=w_in[l], b_gates=b_gates[l], sink=attn_sink[l], norm_ml=norm_mlstm[l],
                 w_up_att=w_up_att[l], w_up_ml=w_up_ml[l], w_out=w_out[l], g_post_mix=g_post_mix[l],
                 g_pre_ffn=g_pre_ffn[l], w_router=w_router[l], b_router=b_router[l], w1=w1[l], b1=b1[l],
                 w2=w2[l], b2=b2[l], g_post_ffn=g_post_ffn[l])
        mod = _adaln(cond, w_ada[l], b_ada[l]).reshape(16, 6, D_MODEL)
        row_p = lambda b: 0
        bsz, seq = x_prompt.shape[:2]
        x1_p, gate_p, moe_p, pj, ml = _stream(y_prompt, mod, row_p, p, None, None, False, 256, 0, bsz)
        nat = pj["nat"]
        ks_.append(nat[..., :KV_W].reshape(bsz, seq, ATT_KV_HEADS, HEAD_DIM))
        vs_.append(nat[..., KV_W:].reshape(bsz, seq, ATT_KV_HEADS, HEAD_DIM))
        cs_.append(ml[2])
        ns_.append(ml[3])
        ms_.append(ml[4][..., 0])
        ctx_kv = (_dup_heads(cache_k[:, l], False), _dup_heads(cache_v[:, l], True))
        init = (state_C[:, l], state_n[:, l], state_m[:, l])
        half = n_dec // 2
        experts = (p["w1"], p["b1"].astype(F32), p["w2"], p["b2"].astype(F32))
        groups = []
        for b0 in (0, half):
            row_s = functools.partial(lambda b, off: b + off + 1, off=b0)
            sl = slice(b0, b0 + half)
            x1_s, gate_s, moe_s, _, _ = _stream(y_sample, mod, row_s, p, (ctx_kv[0][sl], ctx_kv[1][sl]),
                                                init, True, 512, b0, half)
            groups.append((x1_s, gate_s, moe_s, row_s))
        (x1_a, gate_a, moe_a, row_a), (x1_b, gate_b, moe_b, row_b) = groups
        (yg_p, yg_a), w1b, w2b = _moe([moe_p[0], moe_a[0]], [moe_p[1], moe_a[1]], [moe_p[2], moe_a[2]], *experts)
        (yg_b,), _, _ = _moe([moe_b[0]], [moe_b[1]], [moe_b[2]], w1b, experts[1], w2b, experts[3])
        y_prompt = _final(x1_p, yg_p, gate_p, mod, row_p, p["g_post_ffn"], 256, 0, bsz, None)
        y_sample = _final(x1_a, yg_a, gate_a, mod, row_a, p["g_post_ffn"], 512, 0, n_dec, None)
        y_sample = _final(x1_b, yg_b, gate_b, mod, row_b, p["g_post_ffn"], 512, half, n_dec, y_sample)
    return (y_prompt, y_sample, jnp.stack(ks_, axis=1), jnp.stack(vs_, axis=1), jnp.stack(cs_, axis=1),
            jnp.stack(ns_, axis=1), jnp.stack(ms_, axis=1))
```

```python
import functools

import numpy as np
import jax
import jax.numpy as jnp
from jax import lax
from jax.experimental import pallas as pl
from jax.experimental.pallas import tpu as pltpu
from jax.experimental.pallas import tpu_sc as plsc

F32 = jnp.float32
BF16 = jnp.bfloat16

D_MODEL = 1024
GRID_W = 64
ATT_HEADS = 8
ATT_KV_HEADS = 2
ATT_GROUP = ATT_HEADS // ATT_KV_HEADS
HEAD_DIM = 64
BLOCK = 128
ROPE_THETA = 10000.0
AXIS_FREQS = HEAD_DIM // 4
ML_HEADS = 4
ML_DK = 128
ML_DV = 256
ML_CHUNK = 512
N_EXPERTS = 32
TOP_K = 4
D_FF = 1024
SWIGLU_ALPHA = 1.702
SWIGLU_LIMIT = 7.0
EPS = 1e-6
NEG = -1e30

Q_W = ATT_HEADS * HEAD_DIM
KV_W = ATT_KV_HEADS * HEAD_DIM
MLQK_W = ML_HEADS * ML_DK
MLV_W = ML_HEADS * ML_DV
GATE_W = 4 * ML_HEADS

LANES = 128
VMEM_LIMIT = 56 * 1024 * 1024
MOE_ROWS = 512
SC_CORES = 2
SC_SUBCORES = 16
SC_GATHER_CHUNK = 32


def _cparams(sem):
    return pltpu.CompilerParams(dimension_semantics=sem, vmem_limit_bytes=VMEM_LIMIT)


def _const_spec(shape):
    nd = len(shape)
    return pl.BlockSpec(shape, lambda *_: (0,) * nd)


def _adaln_kernel(cond_ref, w_ref, b_ref, o_ref):
    cnd = cond_ref[...]
    act = cnd * jax.nn.sigmoid(cnd)
    o_ref[...] = jnp.dot(act.astype(BF16), w_ref[...].astype(BF16),
                         preferred_element_type=F32) + b_ref[...]


def _adaln(cond, w_ada, b_ada):
    rows, d = cond.shape
    n = w_ada.shape[1]
    tn = 1536
    return pl.pallas_call(
        _adaln_kernel,
        grid=(n // tn,),
        in_specs=[_const_spec((rows, d)),
                  pl.BlockSpec((d, tn), lambda j: (0, j)),
                  pl.BlockSpec((1, tn), lambda j: (0, j))],
        out_specs=pl.BlockSpec((rows, tn), lambda j: (0, j)),
        out_shape=jax.ShapeDtypeStruct((rows, n), F32),
        compiler_params=_cparams(("parallel",)),
        name="adaln",
    )(cond, w_ada, b_ada.reshape(1, n))


def _pack_w_in(w_in, b_gates, rope, with_nat):
    o_q, o_k, o_v = 0, Q_W, Q_W + KV_W
    o_mq = Q_W + 2 * KV_W
    o_mk = o_mq + MLQK_W
    o_mv = o_mk + MLQK_W
    o_g = o_mv + MLV_W
    o_mo = o_g + GATE_W
    o_ga = o_mo + MLV_W
    o_gm = o_ga + D_MODEL
    half = HEAD_DIM // 2

    def head_cols(base, h, rot):
        lo = base + h * HEAD_DIM
        if rot:
            return [w_in[:, lo + half:lo + HEAD_DIM], w_in[:, lo:lo + half]]
        return [w_in[:, lo:lo + HEAD_DIM]]

    def q_cols(rot):
        cols = []
        for g in range(ATT_GROUP):
            cols += head_cols(o_q, g, rot) + head_cols(o_q, g + ATT_GROUP, rot)
        return cols

    def kv_cols(base, rot):
        cols = []
        for h in range(ATT_KV_HEADS):
            cols += head_cols(base, h, rot)
        return cols

    segs, cols, off = {}, [], 0

    def add(name, cl):
        nonlocal off
        width = sum(c.shape[1] for c in cl)
        segs[name] = (off, off + width)
        cols.extend(cl)
        off += width

    add("q", q_cols(False))
    if rope:
        add("qrot", q_cols(True))
    add("kd", kv_cols(o_k, False))
    if rope:
        add("kdrot", kv_cols(o_k, True))
    if with_nat:
        add("nat", [w_in[:, o_k:o_k + 2 * KV_W]])
    add("mq", [w_in[:, o_mq:o_mq + MLQK_W]])
    add("mv", [w_in[:, o_mv:o_mv + MLV_W]])
    add("mo", [w_in[:, o_mo:o_mo + MLV_W]])
    add("ga", [w_in[:, o_ga:o_ga + D_MODEL]])
    add("gm", [w_in[:, o_gm:o_gm + D_MODEL]])
    add("gates", [w_in[:, o_g:o_g + GATE_W], jnp.zeros((D_MODEL, LANES - GATE_W), w_in.dtype)])
    wp = jnp.concatenate(cols, axis=1).astype(BF16)
    wvd_t = jnp.concatenate(kv_cols(o_v, False) + [w_in[:, o_mk:o_mk + MLQK_W]], axis=1).T.astype(BF16)
    bg = jnp.pad(b_gates.astype(F32), (0, LANES - GATE_W)).reshape(1, LANES)
    return wp, wvd_t, bg, segs


def _rope_tables(n_tok):
    rows = n_tok // GRID_W
    row = np.repeat(np.arange(rows), GRID_W).astype(np.float32)
    col = np.tile(np.arange(GRID_W), rows).astype(np.float32)
    inv = (np.float32(ROPE_THETA) ** (-np.arange(AXIS_FREQS, dtype=np.float32) / AXIS_FREQS)).astype(np.float32)
    ang = np.concatenate([row[:, None] * inv, col[:, None] * inv], axis=-1).astype(np.float32)
    cos, sin = np.cos(ang), np.sin(ang)
    c64 = np.concatenate([cos, cos], axis=-1)
    s64 = np.concatenate([-sin, sin], axis=-1)
    return (jnp.asarray(np.tile(c64, (1, 2)), F32), jnp.asarray(np.tile(s64, (1, 2)), F32))


def _log_sigmoid(x):
    return jnp.minimum(x, 0.0) - jnp.log1p(jnp.exp(-jnp.abs(x)))


def _prefix_sum_rows(x):
    row = lax.broadcasted_iota(jnp.int32, x.shape, 0)
    k = 1
    while k < x.shape[0]:
        x = x + jnp.where(row >= k, pltpu.roll(x, k, 0), 0.0)
        k *= 2
    return x


def _proj_kernel(*refs, segs, rope, with_nat):
    it = iter(refs)
    x_ref, mod_ref, g_ref, w_ref, wvt_ref, bg_ref = (next(it) for _ in range(6))
    cos_ref = sin_ref = None
    if rope:
        cos_ref, sin_ref = next(it), next(it)
    q_ref, kd_ref, vd_ref = next(it), next(it), next(it)
    nat_ref = next(it) if with_nat else None
    mq_ref, mk_ref, mv_ref, gates_ref, gates_t_ref, mo_ref, ga_ref, gm_ref = (next(it) for _ in range(8))

    x = x_ref[0]
    shift = mod_ref[0, 0:1, :]
    scale = mod_ref[0, 1:2, :]
    ms = jnp.mean(x * x, axis=-1, keepdims=True)
    xn = x * lax.rsqrt(ms + EPS) * g_ref[...]
    xb = (xn * (1.0 + scale) + shift).astype(BF16)

    def seg(name):
        lo, hi = segs[name]
        return jnp.dot(xb, w_ref[:, lo:hi], preferred_element_type=F32)

    uq = seg("q")
    ukd = seg("kd")
    if rope:
        uqr = seg("qrot")
        ukr = seg("kdrot")
        cs = cos_ref[...]
        sn = sin_ref[...]
        for h in range(ATT_HEADS // 2):
            sl = slice(h * LANES, (h + 1) * LANES)
            q_ref[0, h] = ((uq[:, sl] * cs + uqr[:, sl] * sn) * HEAD_DIM ** -0.5).astype(BF16)
        kd_ref[0] = (ukd * cs + ukr * sn).astype(BF16)
    else:
        for h in range(ATT_HEADS // 2):
            sl = slice(h * LANES, (h + 1) * LANES)
            q_ref[0, h] = (uq[:, sl] * HEAD_DIM ** -0.5).astype(BF16)
        kd_ref[0] = ukd.astype(BF16)
    ut = lax.dot_general(wvt_ref[...], xb, (((1,), (1,)), ((), ())), preferred_element_type=F32)
    vd_ref[0] = ut[:KV_W].astype(BF16)
    mk_ref[0] = ut[KV_W:].astype(BF16)
    if with_nat:
        nat_ref[0] = seg("nat")
    mq_ref[0] = (seg("mq") * ML_DK ** -0.5).astype(BF16)
    mv_ref[0] = seg("mv").astype(BF16)
    gates = seg("gates") + bg_ref[...]
    lf = _log_sigmoid(gates)
    csum = _prefix_sum_rows(lf)
    lane = lax.broadcasted_iota(jnp.int32, gates.shape, 1)
    cum = jnp.where(lane >= 2 * ML_HEADS, csum[-1:, :] - csum + lf, csum)
    gates_ref[0] = cum
    gates_t_ref[0, :GATE_W, :] = gates.T[:GATE_W, :]
    gates_t_ref[0, GATE_W:, :] = cum.T[:GATE_W, :]
    mo_ref[0] = _sigmoid(seg("mo")).astype(BF16)
    ga_ref[0] = _sigmoid(seg("ga")).astype(BF16)
    gm_ref[0] = _sigmoid(seg("gm")).astype(BF16)


def _project(x, mod, g_pre, w_in, b_gates, mod_row, rope, with_nat, tm, b0, bsz):
    _, seq, d = x.shape
    wp, wvd_t, bg, segs = _pack_w_in(w_in, b_gates, rope, with_nat)
    nw = wp.shape[1]
    tok3 = lambda b, i: (b, i, 0)
    in_specs = [pl.BlockSpec((1, tm, d), lambda b, i: (b + b0, i, 0)),
                pl.BlockSpec((1, 6, d), lambda b, i: (mod_row(b), 0, 0)),
                _const_spec((1, d)),
                pl.BlockSpec((d, nw), lambda b, i: (0, 0), pipeline_mode=pl.Buffered(1)),
                _const_spec((KV_W + MLQK_W, d)), _const_spec((1, LANES))]
    args = [x, mod, g_pre.reshape(1, d), wp, wvd_t, bg]
    if rope:
        cos_t, sin_t = _rope_tables(seq)
        in_specs += [pl.BlockSpec((tm, LANES), lambda b, i: (i, 0))] * 2
        args += [cos_t, sin_t]

    def tok_out(width, dtype):
        return (jax.ShapeDtypeStruct((bsz, seq, width), dtype), pl.BlockSpec((1, tm, width), tok3))

    outs = [(jax.ShapeDtypeStruct((bsz, ATT_HEADS // 2, seq, LANES), BF16),
             pl.BlockSpec((1, ATT_HEADS // 2, tm, LANES), lambda b, i: (b, 0, i, 0))),
            tok_out(KV_W, BF16),
            (jax.ShapeDtypeStruct((bsz, KV_W, seq), BF16),
             pl.BlockSpec((1, KV_W, tm), lambda b, i: (b, 0, i)))]
    if with_nat:
        outs.append(tok_out(2 * KV_W, F32))
    outs += [tok_out(MLQK_W, BF16),
             (jax.ShapeDtypeStruct((bsz, MLQK_W, seq), BF16),
              pl.BlockSpec((1, MLQK_W, tm), lambda b, i: (b, 0, i))),
             tok_out(MLV_W, BF16), tok_out(LANES, F32),
             (jax.ShapeDtypeStruct((bsz, 2 * GATE_W, seq), F32),
              pl.BlockSpec((1, 2 * GATE_W, tm), lambda b, i: (b, 0, i))),
             tok_out(MLV_W, BF16), tok_out(D_MODEL, BF16), tok_out(D_MODEL, BF16)]
    res = pl.pallas_call(
        functools.partial(_proj_kernel, segs=segs, rope=rope, with_nat=with_nat),
        grid=(bsz, seq // tm),
        in_specs=in_specs,
        out_specs=[o[1] for o in outs],
        out_shape=[o[0] for o in outs],
        compiler_params=_cparams(("parallel", "parallel")),
        name="proj_rope" if rope else "proj_ctx",
    )(*args)
    names = ["q", "kd", "vd"] + (["nat"] if with_nat else []) + ["mq", "mk", "mv", "gates", "gates_t", "mo", "ga", "gm"]
    return dict(zip(names, res))


def _attn_kernel(*refs, window, n_blocks, n_ctx):
    it = iter(refs)
    sink_ref, q_ref = next(it), next(it)
    if window:
        kp_ref, kc_ref, kn_ref, vp_ref, vc_ref, vn_ref = (next(it) for _ in range(6))
    ck_ref, cv_ref, o_ref = next(it), next(it), next(it)
    j = pl.program_id(1)
    n_win = 3 * BLOCK if window else 0
    cols = ATT_GROUP * BLOCK

    col_id = lax.broadcasted_iota(jnp.int32, (1, cols), 1)
    if window:
        kl = lax.broadcasted_iota(jnp.int32, (BLOCK, cols), 0)
        ql = lax.broadcasted_iota(jnp.int32, (BLOCK, cols), 1) & (BLOCK - 1)
        prev_ok = kl >= ql + jnp.where(j == 0, BLOCK, 0)
        next_ok = kl <= ql - jnp.where(j == n_blocks - 1, BLOCK, 0)
    lane_row = lax.broadcasted_iota(jnp.int32, (1, LANES), 1) < HEAD_DIM
    half_lo = jnp.where(lane_row, 1.0, 0.0).astype(BF16)
    half_hi = jnp.where(lane_row, 0.0, 1.0).astype(BF16)

    if window:
        keys = jnp.concatenate([kp_ref[0], kc_ref[0], kn_ref[0], ck_ref[0]], axis=0)
        vals_t = jnp.concatenate([vp_ref[0], vc_ref[0], vn_ref[0], cv_ref[0]], axis=1)
    else:
        keys = ck_ref[0]
        vals_t = cv_ref[0]

    for kvh in range(ATT_KV_HEADS):
        qs = jnp.concatenate([q_ref[0, g] * (half_lo if kvh == 0 else half_hi) for g in range(ATT_GROUP)], axis=0)
        s = lax.dot_general(keys, qs, (((1,), (1,)), ((), ())), preferred_element_type=F32)
        if window:
            s = jnp.concatenate([jnp.where(prev_ok, s[:BLOCK], NEG), s[BLOCK:2 * BLOCK],
                                 jnp.where(next_ok, s[2 * BLOCK:n_win], NEG), s[n_win:]], axis=0)
        snk = jnp.full((1, cols), sink_ref[kvh * ATT_GROUP + ATT_GROUP - 1], F32)
        for g in range(ATT_GROUP - 2, -1, -1):
            snk = jnp.where(col_id < (g + 1) * BLOCK, sink_ref[kvh * ATT_GROUP + g], snk)
        m = jnp.maximum(jnp.max(s, axis=0, keepdims=True), snk)
        p = jnp.exp(s - m)
        den = jnp.sum(p, axis=0, keepdims=True) + jnp.exp(snk - m)
        o_t = jnp.dot(vals_t, p.astype(BF16), preferred_element_type=F32) / den
        o_t = o_t[kvh * HEAD_DIM:(kvh + 1) * HEAD_DIM]
        for pair in range(ATT_GROUP // 2):
            two = jnp.concatenate([o_t[:, (2 * pair) * BLOCK:(2 * pair + 1) * BLOCK],
                                   o_t[:, (2 * pair + 1) * BLOCK:(2 * pair + 2) * BLOCK]], axis=0)
            col = (kvh * (ATT_GROUP // 2) + pair) * LANES
            o_ref[0, :, col:col + LANES] = two.T.astype(BF16)


def _attention(q, kd, vd_t, ckd, cvd_t, sink, window):
    bsz, _, seq, _ = q.shape
    nb = seq // BLOCK
    n_ctx = ckd.shape[1]
    in_specs = [pl.BlockSpec(memory_space=pltpu.SMEM),
                pl.BlockSpec((1, ATT_HEADS // 2, BLOCK, LANES), lambda b, j: (b, 0, j, 0))]
    args = [sink.astype(F32), q]
    if window:
        in_specs += [pl.BlockSpec((1, BLOCK, KV_W), lambda b, j: (b, jnp.maximum(j - 1, 0), 0)),
                     pl.BlockSpec((1, BLOCK, KV_W), lambda b, j: (b, j, 0)),
                     pl.BlockSpec((1, BLOCK, KV_W), lambda b, j: (b, jnp.minimum(j + 1, nb - 1), 0)),
                     pl.BlockSpec((1, KV_W, BLOCK), lambda b, j: (b, 0, jnp.maximum(j - 1, 0))),
                     pl.BlockSpec((1, KV_W, BLOCK), lambda b, j: (b, 0, j)),
                     pl.BlockSpec((1, KV_W, BLOCK), lambda b, j: (b, 0, jnp.minimum(j + 1, nb - 1)))]
        args += [kd, kd, kd, vd_t, vd_t, vd_t]
    in_specs += [pl.BlockSpec((1, n_ctx, KV_W), lambda b, j: (b, 0, 0)),
                 pl.BlockSpec((1, KV_W, n_ctx), lambda b, j: (b, 0, 0))]
    args += [ckd, cvd_t]
    return pl.pallas_call(
        functools.partial(_attn_kernel, window=window, n_blocks=nb, n_ctx=n_ctx),
        grid=(bsz, nb),
        in_specs=in_specs,
        out_specs=pl.BlockSpec((1, BLOCK, Q_W), lambda b, j: (b, j, 0)),
        out_shape=jax.ShapeDtypeStruct((bsz, seq, Q_W), BF16),
        compiler_params=_cparams(("parallel", "parallel")),
        name="attn_window" if window else "attn_ctx",
    )(*args)


def _flat_heads(t, transpose):
    b, l = t.shape[:2]
    out = t.reshape(b, l, KV_W).astype(BF16)
    return jnp.swapaxes(out, 1, 2) if transpose else out


def _mlstm_kernel(*refs, chunk_len, n_chunks, has_init, emit_state):
    it = iter(refs)
    qkvg = [[next(it) for _ in range(5)] for _ in range(2)]
    if has_init:
        c0_ref, n0_ref, m0_ref = next(it), next(it), next(it)
    h_refs = [next(it), next(it)]
    if emit_state:
        co_ref, no_ref, mo_ref = next(it), next(it), next(it)
    st_s, m_s = next(it), next(it)
    t = pl.program_id(1)
    L = chunk_len

    @pl.when(t == 0)
    def _():
        for d in range(2):
            for h in range(ML_HEADS):
                if has_init:
                    st_s[d, h, :, :ML_DV] = c0_ref[0, d, h].T
                    st_s[d, h, :, ML_DV:] = jnp.broadcast_to(n0_ref[0, d, h:h + 1, :], (LANES, ML_DK)).T
                    m_s[d, h] = m0_ref[0, d, h:h + 1, :]
                else:
                    st_s[d, h] = jnp.zeros((ML_DK, ML_DV + LANES), F32)
                    m_s[d, h] = jnp.zeros((1, LANES), F32)

    ri = lax.broadcasted_iota(jnp.int32, (L, L), 0)
    ci = lax.broadcasted_iota(jnp.int32, (L, L), 1)
    ones_blk = jnp.ones((L, LANES), BF16)

    for d in range(2):
        q_ref, kt_ref, v_ref, cum_ref, rows_ref = qkvg[d]
        tri = (ci <= ri) if d == 0 else (ci >= ri)
        bc = cum_ref[0]
        g_t = rows_ref[0, :GATE_W, :]
        br = rows_ref[0, GATE_W:, :]
        last = L - 1 if d == 0 else 0

        for h in range(ML_HEADS):
            icol_i = 2 * ML_HEADS * d + h
            fcol = icol_i + ML_HEADS
            bcol = bc[:, fcol:fcol + 1]
            brow = br[fcol:fcol + 1, :]
            irow = g_t[icol_i:icol_i + 1, :]
            b_last = bc[last:last + 1, fcol:fcol + 1]
            m_prev = m_s[d, h][:, 0:1]
            st_prev = st_s[d, h]
            qh = q_ref[0, :, h * ML_DK:(h + 1) * ML_DK]
            kt = kt_ref[0, h * ML_DK:(h + 1) * ML_DK, :]
            v_aug = jnp.concatenate([v_ref[0, :, h * ML_DV:(h + 1) * ML_DV], ones_blk], axis=1)

            rel = jnp.where(tri, irow - brow, NEG)
            inter = bcol + m_prev
            m_t = jnp.maximum(inter, bcol + jnp.max(rel, axis=-1, keepdims=True))
            w = jnp.exp(rel + (bcol - m_t))
            w_inter = jnp.exp(inter - m_t)
            s = jnp.dot(qh, kt, preferred_element_type=F32) * w
            lhs = jnp.concatenate([s.astype(BF16), (qh.astype(F32) * w_inter).astype(BF16)], axis=1)
            rhs = jnp.concatenate([v_aug, st_prev.astype(BF16)], axis=0)
            mix = jnp.dot(lhs, rhs, preferred_element_type=F32)
            inv = 1.0 / jnp.maximum(jnp.abs(mix[:, ML_DV:]), jnp.exp(-m_t))
            for c in range(ML_DV // LANES):
                lo = h * ML_DV + c * LANES
                h_refs[d][0, :, lo:lo + LANES] = mix[:, c * LANES:(c + 1) * LANES] * inv

            a = b_last - brow + irow
            m_new = jnp.maximum(b_last + m_prev, jnp.max(a, axis=-1, keepdims=True))
            wk = jnp.exp(a - m_new)
            decay = jnp.exp(b_last + m_prev - m_new)
            kw_t = (kt.astype(F32) * wk).astype(BF16)
            st_new = decay * st_prev + jnp.dot(kw_t, v_aug, preferred_element_type=F32)
            st_s[d, h] = st_new
            m_s[d, h] = jnp.broadcast_to(m_new, (1, LANES))
            if emit_state:
                @pl.when(t == n_chunks - 1)
                def _():
                    co_ref[0, d, h] = st_new[:, :ML_DV].T
                    no_ref[0, d, h:h + 1, :] = st_new[:, ML_DV:].T[0:1, :]
                    mo_ref[0, d, h:h + 1, :] = jnp.broadcast_to(m_new, (1, LANES))


def _mlstm(mq, mk_t, mv, gate_cum, gate_rows, init, emit_state, b0=0):
    bsz, seq, _ = mq.shape
    chunk_len = min(ML_CHUNK, seq)
    nc = seq // chunk_len
    fwd = lambda b, t: (b, t, 0)
    bwd = lambda b, t: (b, nc - 1 - t, 0)
    fwd_t = lambda b, t: (b, 0, t)
    bwd_t = lambda b, t: (b, 0, nc - 1 - t)
    state5 = lambda b, t: (b, 0, 0, 0, 0)
    state4 = lambda b, t: (b, 0, 0, 0)
    init5 = lambda b, t: (b + b0, 0, 0, 0, 0)
    init4 = lambda b, t: (b + b0, 0, 0, 0)
    in_specs, args = [], []
    for chunk, chunk_t in ((fwd, fwd_t), (bwd, bwd_t)):
        in_specs += [pl.BlockSpec((1, chunk_len, MLQK_W), chunk), pl.BlockSpec((1, MLQK_W, chunk_len), chunk_t),
                     pl.BlockSpec((1, chunk_len, MLV_W), chunk), pl.BlockSpec((1, chunk_len, LANES), chunk),
                     pl.BlockSpec((1, 2 * GATE_W, chunk_len), chunk_t)]
        args += [mq, mk_t, mv, gate_cum, gate_rows]
    if init is not None:
        c0, n0, m0 = init
        in_specs += [pl.BlockSpec((1, 2, ML_HEADS, ML_DV, ML_DK), init5),
                     pl.BlockSpec((1, 2, ML_HEADS, ML_DK), init4),
                     pl.BlockSpec((1, 2, ML_HEADS, LANES), init4)]
        args += [c0.astype(F32), n0.astype(F32),
                 jnp.broadcast_to(m0.astype(F32)[..., None], m0.shape + (LANES,))]
    out_shape = [jax.ShapeDtypeStruct((bsz, seq, MLV_W), F32)] * 2
    out_specs = [pl.BlockSpec((1, chunk_len, MLV_W), fwd), pl.BlockSpec((1, chunk_len, MLV_W), bwd)]
    if emit_state:
        out_shape += [jax.ShapeDtypeStruct((bsz, 2, ML_HEADS, ML_DV, ML_DK), F32),
                      jax.ShapeDtypeStruct((bsz, 2, ML_HEADS, ML_DK), F32),
                      jax.ShapeDtypeStruct((bsz, 2, ML_HEADS, LANES), F32)]
        out_specs += [pl.BlockSpec((1, 2, ML_HEADS, ML_DV, ML_DK), state5),
                      pl.BlockSpec((1, 2, ML_HEADS, ML_DK), state4),
                      pl.BlockSpec((1, 2, ML_HEADS, LANES), state4)]
    return pl.pallas_call(
        functools.partial(_mlstm_kernel, chunk_len=chunk_len, n_chunks=nc, has_init=init is not None, emit_state=emit_state),
        grid=(bsz, nc),
        in_specs=in_specs,
        out_specs=out_specs,
        out_shape=out_shape,
        scratch_shapes=[pltpu.VMEM((2, ML_HEADS, ML_DK, ML_DV + LANES), F32),
                        pltpu.VMEM((2, ML_HEADS, 1, LANES), F32)],
        compiler_params=_cparams(("parallel", "arbitrary")),
        name="mlstm_state" if emit_state else "mlstm",
    )(*args)


def _sigmoid(x):
    return 0.5 * jnp.tanh(0.5 * x) + 0.5


def _rms(x, g):
    return x * lax.rsqrt(jnp.mean(x * x, axis=-1, keepdims=True) + EPS) * g


def _pack_bf16_pairs(x):
    n = x.shape[1] // 2
    lo = pltpu.bitcast(x[:, :n].astype(BF16).astype(F32), jnp.int32)
    hi = pltpu.bitcast(x[:, n:].astype(BF16).astype(F32), jnp.int32)
    return lax.shift_right_logical(lo, 16) | hi


def _unpack_bf16_pairs(p):
    lo = pltpu.bitcast(lax.shift_left(p, 16), F32)
    hi = pltpu.bitcast(p & jnp.int32(-65536), F32)
    return lo, hi


def _merge_kernel(att_ref, hf_ref, hb_ref, mo_ref, ga_ref, gm_ref, x_ref, mod_ref, nml_ref, gpm_ref, gpf_ref,
                  wua_ref, wum_ref, wo_ref, wr_ref, br_ref, x1_ref, h2_ref, idx_ref, gate_ref, cnt_ref):
    hsum = hf_ref[0] + hb_ref[0]
    parts = []
    for h in range(ML_HEADS):
        hh = hsum[:, h * ML_DV:(h + 1) * ML_DV]
        parts.append(hh * lax.rsqrt(jnp.mean(hh * hh, axis=-1, keepdims=True) + EPS))
    hn = jnp.concatenate(parts, axis=-1) * nml_ref[...]
    ml = (hn * mo_ref[0].astype(F32)).astype(BF16)
    a = jnp.dot(att_ref[0], wua_ref[...], preferred_element_type=F32)
    m = jnp.dot(ml, wum_ref[...], preferred_element_type=F32)
    z = (ga_ref[0].astype(F32) * a + gm_ref[0].astype(F32) * m).astype(BF16)
    mix = jnp.dot(z, wo_ref[...], preferred_element_type=F32)
    gate1 = mod_ref[0, 2:3, :]
    shift2 = mod_ref[0, 3:4, :]
    scale2 = mod_ref[0, 4:5, :]
    x1 = x_ref[0] + gate1 * _rms(mix, gpm_ref[...])
    x1_ref[0] = x1
    h2 = _rms(x1, gpf_ref[...]) * (1.0 + scale2) + shift2
    h2_ref[0] = _pack_bf16_pairs(h2)

    h2_hi = h2.astype(BF16)
    h2_lo = (h2 - h2_hi.astype(F32)).astype(BF16)
    wr = wr_ref[...]
    wr_hi = wr.astype(BF16)
    wr_lo = (wr - wr_hi.astype(F32)).astype(BF16)
    hi_terms = jnp.dot(h2_hi, jnp.concatenate([wr_hi, wr_lo], axis=1), preferred_element_type=F32)
    logits = (hi_terms[:, :LANES] + hi_terms[:, LANES:]
              + jnp.dot(h2_lo, wr_hi, preferred_element_type=F32)) + br_ref[...]
    lane = lax.broadcasted_iota(jnp.int32, logits.shape, 1).astype(F32)
    work = jnp.where(lane < N_EXPERTS, logits, -jnp.inf)
    idx_out = jnp.zeros(logits.shape, F32)
    val_out = jnp.zeros(logits.shape, F32)
    picked = jnp.zeros(logits.shape, F32)
    top0 = None
    esum = None
    for k in range(TOP_K):
        mx = jnp.max(work, axis=-1, keepdims=True)
        sel = jnp.min(jnp.where(work == mx, lane, float(LANES)), axis=-1, keepdims=True)
        if k == 0:
            top0 = mx
        e = jnp.exp(mx - top0)
        esum = e if k == 0 else esum + e
        idx_out = jnp.where(lane == k, sel, idx_out)
        val_out = jnp.where(lane == k, e, val_out)
        picked = jnp.where(lane == sel, 1.0, picked)
        work = jnp.where(lane == sel, -jnp.inf, work)
    idx_ref[0] = idx_out.astype(jnp.int32)
    gate_ref[0] = val_out / esum
    cnt_ref[0] = jnp.sum(picked, axis=0, keepdims=True)


def _merge(att, hf, hb, pj, x, mod, mod_row, p, tm, b0):
    bsz = att.shape[0]
    _, seq, d = x.shape
    tok3 = lambda b, i: (b, i, 0)
    wr = jnp.pad(p["w_router"].astype(F32), ((0, 0), (0, LANES - N_EXPERTS)))
    br = jnp.pad(p["b_router"].astype(F32), (0, LANES - N_EXPERTS)).reshape(1, LANES)
    row = lambda v: v.astype(F32).reshape(1, -1)
    in_specs = [pl.BlockSpec((1, tm, Q_W), tok3),
                pl.BlockSpec((1, tm, MLV_W), tok3), pl.BlockSpec((1, tm, MLV_W), tok3),
                pl.BlockSpec((1, tm, MLV_W), tok3), pl.BlockSpec((1, tm, d), tok3), pl.BlockSpec((1, tm, d), tok3),
                pl.BlockSpec((1, tm, d), lambda b, i: (b + b0, i, 0)),
                pl.BlockSpec((1, 6, d), lambda b, i: (mod_row(b), 0, 0)),
                _const_spec((1, MLV_W)), _const_spec((1, d)), _const_spec((1, d)),
                _const_spec((Q_W, d)), _const_spec((MLV_W, d)), _const_spec((d, d)),
                _const_spec((d, LANES)), _const_spec((1, LANES))]
    out_shape = [jax.ShapeDtypeStruct((bsz, seq, d), F32), jax.ShapeDtypeStruct((bsz, seq, d // 2), jnp.int32),
                 jax.ShapeDtypeStruct((bsz, seq, LANES), jnp.int32), jax.ShapeDtypeStruct((bsz, seq, LANES), F32),
                 jax.ShapeDtypeStruct((bsz * (seq // tm), 1, LANES), F32)]
    out_specs = [pl.BlockSpec((1, tm, d), tok3), pl.BlockSpec((1, tm, d // 2), tok3),
                 pl.BlockSpec((1, tm, LANES), tok3), pl.BlockSpec((1, tm, LANES), tok3),
                 pl.BlockSpec((1, 1, LANES), lambda b, i: (b * (seq // tm) + i, 0, 0))]
    return pl.pallas_call(
        _merge_kernel,
        grid=(bsz, seq // tm),
        in_specs=in_specs,
        out_specs=out_specs,
        out_shape=out_shape,
        compiler_params=_cparams(("parallel", "parallel")),
        name="merge_router",
    )(att, hf, hb, pj["mo"], pj["ga"], pj["gm"], x, mod, row(p["norm_ml"]), row(p["g_post_mix"]),
      row(p["g_pre_ffn"]), p["w_up_att"].astype(BF16), p["w_up_ml"].astype(BF16), p["w_out"].astype(BF16), wr, br)


def _sc_gather_rows(table, idx):
    n = idx.shape[0]
    width = table.shape[1]
    chunk = SC_GATHER_CHUNK
    n_workers = SC_CORES * SC_SUBCORES
    rows_per_worker = n // n_workers
    n_pairs = rows_per_worker // (2 * chunk)
    assert n_pairs * 2 * chunk * n_workers == n
    mesh = plsc.VectorSubcoreMesh(core_axis_name="c", subcore_axis_name="s")

    def body(table_hbm, idx_hbm, out_hbm, idx_v, rows_v, gsem, osem):
        wid = lax.axis_index("s") * SC_CORES + lax.axis_index("c")
        base0 = wid * rows_per_worker

        def gather_copy(b):
            return pltpu.make_async_copy(table_hbm.at[idx_v.at[b]], rows_v.at[b], gsem.at[b])

        def write_copy(ci, b):
            base = pl.multiple_of(base0 + ci * chunk, 8)
            return pltpu.make_async_copy(rows_v.at[b], out_hbm.at[pl.ds(base, chunk)], osem.at[b])

        def issue(ci, b):
            base = pl.multiple_of(base0 + ci * chunk, 8)
            pltpu.sync_copy(idx_hbm.at[pl.ds(base, chunk)], idx_v.at[b])
            gather_copy(b).start()

        def finish(ci, b):
            gather_copy(b).wait()
            write_copy(ci, b).start()

        issue(0, 0)

        @pl.loop(0, n_pairs)
        def _(j):
            @pl.when(j > 0)
            def _():
                write_copy(2 * j - 1, 1).wait()
            issue(2 * j + 1, 1)
            finish(2 * j, 0)

            @pl.when(j < n_pairs - 1)
            def _():
                write_copy(2 * j, 0).wait()
                issue(2 * j + 2, 0)
            finish(2 * j + 1, 1)

        write_copy(2 * n_pairs - 2, 0).wait()
        write_copy(2 * n_pairs - 1, 1).wait()

    return pl.kernel(
        body, mesh=mesh,
        out_type=jax.ShapeDtypeStruct((n, width), table.dtype),
        scratch_types=[pltpu.VMEM((2, chunk), jnp.int32), pltpu.VMEM((2, chunk, width), table.dtype),
                       pltpu.SemaphoreType.DMA((2,)), pltpu.SemaphoreType.DMA((2,))],
    )(table, idx)


def _ffn_kernel(be_ref, nu_ref, xs_ref, w1_ref, b1_ref, w2_ref, b2_ref, ys_ref, *bf16_refs, weights_are_bf16):
    i = pl.program_id(0)
    e = be_ref[i]
    e_prev = be_ref[jnp.maximum(i - 1, 0)]
    if weights_are_bf16:
        w1b_ref, w2b_ref = w1_ref, w2_ref
    else:
        w1b_ref, w2b_ref = bf16_refs

        @pl.when((i == 0) | (e != e_prev))
        def _():
            w1b_ref[0] = w1_ref[0].astype(BF16)
            w2b_ref[0] = w2_ref[0].astype(BF16)

    @pl.when(i < nu_ref[0])
    def _():
        x_lo, x_hi = _unpack_bf16_pairs(xs_ref[...])
        xb = jnp.concatenate([x_lo.astype(BF16), x_hi.astype(BF16)], axis=1)
        hmid = jnp.dot(xb, w1b_ref[0], preferred_element_type=F32) + b1_ref[0]
        glu = jnp.minimum(hmid[:, :D_FF], SWIGLU_LIMIT)
        lin = jnp.clip(hmid[:, D_FF:], -SWIGLU_LIMIT, SWIGLU_LIMIT)
        act = ((lin + 1.0) * glu * _sigmoid(SWIGLU_ALPHA * glu)).astype(BF16)
        y = jnp.dot(act, w2b_ref[0], preferred_element_type=F32) + b2_ref[0]
        ys_ref[...] = _pack_bf16_pairs(y)

    @pl.when(i >= nu_ref[0])
    def _():
        ys_ref[...] = jnp.zeros(ys_ref.shape, jnp.int32)


def _sc_scatter_rows(tables, positions, n_rows):
    width = tables[0].shape[1]
    chunk = SC_GATHER_CHUNK
    n_workers = SC_CORES * SC_SUBCORES
    n_streams = len(tables)
    mesh = plsc.VectorSubcoreMesh(core_axis_name="c", subcore_axis_name="s")

    def body(*refs):
        table_refs = refs[0:2 * n_streams:2]
        pos_refs = refs[1:2 * n_streams:2]
        out_hbm, idx_v, rows_v, lsem, ssem = refs[2 * n_streams:]
        wid = lax.axis_index("s") * SC_CORES + lax.axis_index("c")

        for table_hbm, pos_hbm in zip(table_refs, pos_refs):
            tok_per_worker = table_hbm.shape[0] // n_workers
            n_pairs = tok_per_worker // (2 * chunk)
            assert n_pairs * 2 * chunk * n_workers == table_hbm.shape[0]
            base0 = wid * tok_per_worker

            def load_copy(ci, b):
                base = pl.multiple_of(base0 + ci * chunk, 8)
                return pltpu.make_async_copy(table_hbm.at[pl.ds(base, chunk)], rows_v.at[b], lsem.at[b])

            def scatter_copy(b, k):
                return pltpu.make_async_copy(rows_v.at[b], out_hbm.at[idx_v.at[b, k]], ssem.at[b])

            def load(ci, b):
                base = pl.multiple_of(base0 + ci * chunk, 8)
                load_copy(ci, b).start()
                for k in range(TOP_K):
                    pltpu.sync_copy(pos_hbm.at[k, pl.ds(base, chunk)], idx_v.at[b, k])

            def scatter(ci, b):
                load_copy(ci, b).wait()
                for k in range(TOP_K):
                    scatter_copy(b, k).start()

            def drain(b):
                for k in range(TOP_K):
                    scatter_copy(b, k).wait()

            load(0, 0)

            @pl.loop(0, n_pairs)
            def _(j):
                @pl.when(j > 0)
                def _():
                    drain(1)
                load(2 * j + 1, 1)
                scatter(2 * j, 0)

                @pl.when(j < n_pairs - 1)
                def _():
                    drain(0)
                    load(2 * j + 2, 0)
                scatter(2 * j + 1, 1)

            drain(0)
            drain(1)

    args = [a for pair in zip(tables, positions) for a in pair]
    return pl.kernel(
        body, mesh=mesh,
        out_type=jax.ShapeDtypeStruct((n_rows, width), tables[0].dtype),
        scratch_types=[pltpu.VMEM((2, TOP_K, chunk), jnp.int32), pltpu.VMEM((2, chunk, width), tables[0].dtype),
                       pltpu.SemaphoreType.DMA((2,)), pltpu.SemaphoreType.DMA((2,))],
    )(*args)


def _pos_kernel(idx_ref, base_ref, pos_ref):
    idx = idx_ref[...]
    rows = idx.shape[0]
    lane = lax.broadcasted_iota(jnp.int32, idx.shape, 1)
    hots = [jnp.where(lane == idx[:, k:k + 1], 1.0, 0.0) for k in range(TOP_K)]
    cnt = hots[0] + hots[1] + hots[2] + hots[3]
    ri = lax.broadcasted_iota(jnp.int32, (rows, rows), 0)
    ci = lax.broadcasted_iota(jnp.int32, (rows, rows), 1)
    earlier = jnp.where(ci < ri, 1.0, 0.0).astype(BF16)
    ahead = jnp.dot(earlier, cnt.astype(BF16), preferred_element_type=F32) + base_ref[0]
    posmat = jnp.zeros(idx.shape, F32)
    for k in range(TOP_K):
        posmat = jnp.where(lane == k, jnp.sum(hots[k] * ahead, axis=-1, keepdims=True), posmat)
    pos_ref[...] = posmat.T[:TOP_K, :].astype(jnp.int32)


def _route(idx_list, count_list, every_expert):
    tiles = [c.shape[0] for c in count_list]
    tile_tok = [idx.size // LANES // n for idx, n in zip(idx_list, tiles)]
    n_tok = sum(n * sz for n, sz in zip(tiles, tile_tok))
    n_blocks = n_tok * TOP_K // MOE_ROWS + N_EXPERTS
    cnt = jnp.concatenate([c.reshape(-1, LANES) for c in count_list], axis=0).astype(jnp.int32)
    tile_off = jnp.cumsum(cnt, axis=0) - cnt
    total = jnp.sum(cnt, axis=0)
    padded = (total + MOE_ROWS - 1) // MOE_ROWS * MOE_ROWS
    if every_expert:
        padded = jnp.where(jnp.arange(LANES) < N_EXPERTS, jnp.maximum(padded, MOE_ROWS), 0)
    pend = jnp.cumsum(padded)
    base = ((pend - padded)[None, :] + tile_off).astype(F32)
    starts = jnp.arange(n_blocks, dtype=jnp.int32) * MOE_ROWS
    block_e = jnp.minimum(jnp.sum(pend[None, :N_EXPERTS] <= starts[:, None], axis=1), N_EXPERTS - 1).astype(jnp.int32)
    n_used = (pend[N_EXPERTS - 1] // MOE_ROWS).astype(jnp.int32).reshape(1)
    positions, first = [], 0
    for idx, n_tiles, tile in zip(idx_list, tiles, tile_tok):
        stream_tok = n_tiles * tile
        positions.append(pl.pallas_call(
            _pos_kernel,
            grid=(n_tiles,),
            in_specs=[pl.BlockSpec((tile, LANES), lambda i: (i, 0)),
                      pl.BlockSpec((1, 1, LANES), lambda i: (i, 0, 0))],
            out_specs=pl.BlockSpec((TOP_K, tile), lambda i: (0, i)),
            out_shape=jax.ShapeDtypeStruct((TOP_K, stream_tok), jnp.int32),
            compiler_params=_cparams(("parallel",)),
            name="route_pos",
        )(idx.reshape(stream_tok, LANES), base[first:first + n_tiles].reshape(n_tiles, 1, LANES)))
        first += n_tiles
    return block_e, n_used, positions


def _moe(h2p_list, idx_list, count_list, w1, b1, w2, b2):
    half = h2p_list[0].shape[1]
    d = 2 * half
    cast_here = w1.dtype != BF16
    block_e, n_used, positions = _route(idx_list, count_list, every_expert=cast_here)
    n_blocks = block_e.shape[0]
    xs = _sc_scatter_rows(h2p_list, positions, n_blocks * MOE_ROWS)
    grid_spec = pltpu.PrefetchScalarGridSpec(
        num_scalar_prefetch=2,
        grid=(n_blocks,),
        in_specs=[pl.BlockSpec((MOE_ROWS, half), lambda i, be, nu: (i, 0)),
                  pl.BlockSpec((1, d, 2 * D_FF), lambda i, be, nu: (be[i], 0, 0)),
                  pl.BlockSpec((1, 1, 2 * D_FF), lambda i, be, nu: (be[i], 0, 0)),
                  pl.BlockSpec((1, D_FF, d), lambda i, be, nu: (be[i], 0, 0)),
                  pl.BlockSpec((1, 1, d), lambda i, be, nu: (be[i], 0, 0))],
        out_specs=[pl.BlockSpec((MOE_ROWS, half), lambda i, be, nu: (i, 0))] + (
            [pl.BlockSpec((1, d, 2 * D_FF), lambda i, be, nu: (be[i], 0, 0)),
             pl.BlockSpec((1, D_FF, d), lambda i, be, nu: (be[i], 0, 0))] if cast_here else []),
    )
    out_shape = [jax.ShapeDtypeStruct((n_blocks * MOE_ROWS, half), jnp.int32)]
    if cast_here:
        out_shape += [jax.ShapeDtypeStruct(w1.shape, BF16), jax.ShapeDtypeStruct(w2.shape, BF16)]
    res = pl.pallas_call(
        functools.partial(_ffn_kernel, weights_are_bf16=not cast_here),
        grid_spec=grid_spec,
        out_shape=out_shape,
        compiler_params=_cparams(("arbitrary",)),
        name="moe_ffn_cast" if cast_here else "moe_ffn",
    )(block_e, n_used, xs, w1, b1.reshape(N_EXPERTS, 1, -1), w2, b2.reshape(N_EXPERTS, 1, -1))
    w1b, w2b = (res[1], res[2]) if cast_here else (w1, w2)
    return [_sc_gather_rows(res[0], pos.reshape(-1)) for pos in positions], w1b, w2b


def _final_kernel(x1_ref, y0_ref, y1_ref, y2_ref, y3_ref, gate_ref, mod_ref, g_ref, *rest):
    o_ref = rest[-1]
    gates = gate_ref[0]
    lo = hi = None
    for k, y_ref in enumerate((y0_ref, y1_ref, y2_ref, y3_ref)):
        y_lo, y_hi = _unpack_bf16_pairs(y_ref[0, 0])
        gk = gates[:, k:k + 1]
        lo = gk * y_lo if k == 0 else lo + gk * y_lo
        hi = gk * y_hi if k == 0 else hi + gk * y_hi
    y = jnp.concatenate([lo, hi], axis=-1)
    gate2 = mod_ref[0, 5:6, :]
    o_ref[0] = x1_ref[0] + gate2 * _rms(y, g_ref[...])


def _final(x1, yg, gate, mod, mod_row, g_post_ffn, tm, b0, n_batch, out_prev):
    bsz, seq, d = x1.shape
    tok3 = lambda b, i: (b, i, 0)
    yg = yg.reshape(TOP_K, bsz, seq, d // 2)
    slot_specs = [pl.BlockSpec((1, 1, tm, d // 2), functools.partial(lambda b, i, k: (k, b, i, 0), k=k))
                  for k in range(TOP_K)]
    in_specs = ([pl.BlockSpec((1, tm, d), tok3)] + slot_specs +
                [pl.BlockSpec((1, tm, LANES), tok3),
                 pl.BlockSpec((1, 6, d), lambda b, i: (mod_row(b), 0, 0)), _const_spec((1, d))])
    args = [x1, yg, yg, yg, yg, gate, mod, g_post_ffn.astype(F32).reshape(1, d)]
    aliases = {}
    if out_prev is not None:
        in_specs.append(pl.BlockSpec(memory_space=pl.ANY))
        args.append(out_prev)
        aliases = {len(args) - 1: 0}
    return pl.pallas_call(
        _final_kernel,
        grid=(bsz, seq // tm),
        in_specs=in_specs,
        out_specs=pl.BlockSpec((1, tm, d), lambda b, i: (b + b0, i, 0)),
        out_shape=jax.ShapeDtypeStruct((n_batch, seq, d), F32),
        input_output_aliases=aliases,
        compiler_params=_cparams(("parallel", "parallel")),
        name="final_residual",
    )(*args)


def _stream(x, mod, mod_row, p, ctx_kv, init_state, rope, tm, b0, bsz):
    _, seq, d = x.shape
    is_ctx = ctx_kv is None
    pj = _project(x, mod, p["g_pre_mix"], p["w_in"], p["b_gates"], mod_row, rope, is_ctx, tm, b0, bsz)
    if is_ctx:
        att = _attention(pj["q"], None, None, pj["kd"], pj["vd"], p["sink"], window=False)
    else:
        att = _attention(pj["q"], pj["kd"], pj["vd"], ctx_kv[0], ctx_kv[1], p["sink"], window=True)
    assert min(ML_CHUNK, seq) == tm
    ml = _mlstm(pj["mq"], pj["mk"], pj["mv"], pj["gates"], pj["gates_t"], init_state, emit_state=is_ctx, b0=b0)
    x1, h2, idx, gate, tile_counts = _merge(att, ml[0], ml[1], pj, x, mod, mod_row, p, tm, b0)
    moe_in = (h2.reshape(bsz * seq, d // 2), idx, tile_counts)
    return x1, gate, moe_in, pj, ml


def kernel(x_prompt, x_sample, c, cache_k, cache_v, state_C, state_n, state_m, c_ctx, w_ada, b_ada, g_pre_mix,
           w_in, b_gates, attn_sink, norm_mlstm, w_up_att, w_up_ml, w_out, g_post_mix, g_pre_ffn, w_router,
           b_router, w1, b1, w2, b2, g_post_ffn):
    depth = w_ada.shape[0]
    n_dec = c.shape[0]
    cond = jnp.concatenate([c_ctx[None, :], c], axis=0).astype(F32)
    cond = jnp.pad(cond, ((0, 16 - cond.shape[0]), (0, 0)))
    y_prompt, y_sample = x_prompt, x_sample
    ks_, vs_, cs_, ns_, ms_ = [], [], [], [], []
    for l in range(depth):
        p = dict(g_pre_mix=g_pre_mix[l], w_in=w_in[l], b_gates=b_gates[l], sink=attn_sink[l], norm_ml=norm_mlstm[l],
                 w_up_att=w_up_att[l], w_up_ml=w_up_ml[l], w_out=w_out[l], g_post_mix=g_post_mix[l],
                 g_pre_ffn=g_pre_ffn[l], w_router=w_router[l], b_router=b_router[l], w1=w1[l], b1=b1[l],
                 w2=w2[l], b2=b2[l], g_post_ffn=g_post_ffn[l])
        mod = _adaln(cond, w_ada[l], b_ada[l]).reshape(16, 6, D_MODEL)
        row_p = lambda b: 0
        bsz, seq = x_prompt.shape[:2]
        x1_p, gate_p, moe_p, pj, ml = _stream(y_prompt, mod, row_p, p, None, None, False, 256, 0, bsz)
        nat = pj["nat"]
        ks_.append(nat[..., :KV_W].reshape(bsz, seq, ATT_KV_HEADS, HEAD_DIM))
        vs_.append(nat[..., KV_W:].reshape(bsz, seq, ATT_KV_HEADS, HEAD_DIM))
        cs_.append(ml[2])
        ns_.append(ml[3])
        ms_.append(ml[4][..., 0])
        ctx_kv = (_flat_heads(cache_k[:, l], False), _flat_heads(cache_v[:, l], True))
        init = (state_C[:, l], state_n[:, l], state_m[:, l])
        half = n_dec // 2
        experts = (p["w1"], p["b1"].astype(F32), p["w2"], p["b2"].astype(F32))
        groups = []
        for b0 in (0, half):
            row_s = functools.partial(lambda b, off: b + off + 1, off=b0)
            sl = slice(b0, b0 + half)
            x1_s, gate_s, moe_s, _, _ = _stream(y_sample, mod, row_s, p, (ctx_kv[0][sl], ctx_kv[1][sl]),
                                                init, True, 512, b0, half)
            groups.append((x1_s, gate_s, moe_s, row_s))
        (x1_a, gate_a, moe_a, row_a), (x1_b, gate_b, moe_b, row_b) = groups
        (yg_p, yg_a), w1b, w2b = _moe([moe_p[0], moe_a[0]], [moe_p[1], moe_a[1]], [moe_p[2], moe_a[2]], *experts)
        (yg_b,), _, _ = _moe([moe_b[0]], [moe_b[1]], [moe_b[2]], w1b, experts[1], w2b, experts[3])
        y_prompt = _final(x1_p, yg_p, gate_p, mod, row_p, p["g_post_ffn"], 256, 0, bsz, None)
        y_sample = _final(x1_a, yg_a, gate_a, mod, row_a, p["g_post_ffn"], 512, 0, n_dec, None)
        y_sample = _final(x1_b, yg_b, gate_b, mod, row_b, p["g_post_ffn"], 512, half, n_dec, y_sample)
    return (y_prompt, y_sample, jnp.stack(ks_, axis=1), jnp.stack(vs_, axis=1), jnp.stack(cs_, axis=1),
            jnp.stack(ns_, axis=1), jnp.stack(ms_, axis=1))
```

```python
import functools

import numpy as np
import jax
import jax.numpy as jnp
from jax import lax
from jax.experimental import pallas as pl
from jax.experimental.pallas import tpu as pltpu
from jax.experimental.pallas import tpu_sc as plsc

F32 = jnp.float32
BF16 = jnp.bfloat16

D_MODEL = 1024
GRID_W = 64
ATT_HEADS = 8
ATT_KV_HEADS = 2
ATT_GROUP = ATT_HEADS // ATT_KV_HEADS
HEAD_DIM = 64
BLOCK = 128
ROPE_THETA = 10000.0
AXIS_FREQS = HEAD_DIM // 4
ML_HEADS = 4
ML_DK = 128
ML_DV = 256
ML_CHUNK = 512
N_EXPERTS = 32
TOP_K = 4
D_FF = 1024
SWIGLU_ALPHA = 1.702
SWIGLU_LIMIT = 7.0
EPS = 1e-6
NEG = -1e30

Q_W = ATT_HEADS * HEAD_DIM
KV_W = ATT_KV_HEADS * HEAD_DIM
MLQK_W = ML_HEADS * ML_DK
MLV_W = ML_HEADS * ML_DV
GATE_W = 4 * ML_HEADS

LANES = 128
VMEM_LIMIT = 56 * 1024 * 1024
MOE_ROWS = 512
SC_CORES = 2
SC_SUBCORES = 16
SC_GATHER_CHUNK = 32


def _cparams(sem):
    return pltpu.CompilerParams(dimension_semantics=sem, vmem_limit_bytes=VMEM_LIMIT)


def _const_spec(shape):
    nd = len(shape)
    return pl.BlockSpec(shape, lambda *_: (0,) * nd)


def _adaln_kernel(cond_ref, w_ref, b_ref, o_ref):
    cnd = cond_ref[...]
    act = cnd * jax.nn.sigmoid(cnd)
    o_ref[...] = jnp.dot(act.astype(BF16), w_ref[...].astype(BF16),
                         preferred_element_type=F32) + b_ref[...]


def _adaln(cond, w_ada, b_ada):
    rows, d = cond.shape
    n = w_ada.shape[1]
    tn = 1536
    return pl.pallas_call(
        _adaln_kernel,
        grid=(n // tn,),
        in_specs=[_const_spec((rows, d)),
                  pl.BlockSpec((d, tn), lambda j: (0, j)),
                  pl.BlockSpec((1, tn), lambda j: (0, j))],
        out_specs=pl.BlockSpec((rows, tn), lambda j: (0, j)),
        out_shape=jax.ShapeDtypeStruct((rows, n), F32),
        compiler_params=_cparams(("parallel",)),
        name="adaln",
    )(cond, w_ada, b_ada.reshape(1, n))


def _pack_w_in(w_in, b_gates, rope, with_nat):
    o_q, o_k, o_v = 0, Q_W, Q_W + KV_W
    o_mq = Q_W + 2 * KV_W
    o_mk = o_mq + MLQK_W
    o_mv = o_mk + MLQK_W
    o_g = o_mv + MLV_W
    o_mo = o_g + GATE_W
    o_ga = o_mo + MLV_W
    o_gm = o_ga + D_MODEL
    half = HEAD_DIM // 2

    def head_cols(base, h, rot):
        lo = base + h * HEAD_DIM
        if rot:
            return [w_in[:, lo + half:lo + HEAD_DIM], w_in[:, lo:lo + half]]
        return [w_in[:, lo:lo + HEAD_DIM]]

    def q_cols(rot):
        cols = []
        for g in range(ATT_GROUP):
            cols += head_cols(o_q, g, rot) + head_cols(o_q, g + ATT_GROUP, rot)
        return cols

    def kv_cols(base, rot):
        cols = []
        for h in range(ATT_KV_HEADS):
            cols += head_cols(base, h, rot)
        return cols

    segs, cols, off = {}, [], 0

    def add(name, cl):
        nonlocal off
        width = sum(c.shape[1] for c in cl)
        segs[name] = (off, off + width)
        cols.extend(cl)
        off += width

    add("q", q_cols(False))
    if rope:
        add("qrot", q_cols(True))
    add("kd", kv_cols(o_k, False))
    if rope:
        add("kdrot", kv_cols(o_k, True))
    if with_nat:
        add("nat", [w_in[:, o_k:o_k + 2 * KV_W]])
    add("mq", [w_in[:, o_mq:o_mq + MLQK_W]])
    add("mv", [w_in[:, o_mv:o_mv + MLV_W]])
    add("mo", [w_in[:, o_mo:o_mo + MLV_W]])
    add("ga", [w_in[:, o_ga:o_ga + D_MODEL]])
    add("gm", [w_in[:, o_gm:o_gm + D_MODEL]])
    add("gates", [w_in[:, o_g:o_g + GATE_W], jnp.zeros((D_MODEL, LANES - GATE_W), w_in.dtype)])
    wp = jnp.concatenate(cols, axis=1).astype(BF16)
    wvd_t = jnp.concatenate(kv_cols(o_v, False) + [w_in[:, o_mk:o_mk + MLQK_W]], axis=1).T.astype(BF16)
    bg = jnp.pad(b_gates.astype(F32), (0, LANES - GATE_W)).reshape(1, LANES)
    return wp, wvd_t, bg, segs


def _rope_tables(n_tok):
    rows = n_tok // GRID_W
    row = np.repeat(np.arange(rows), GRID_W).astype(np.float32)
    col = np.tile(np.arange(GRID_W), rows).astype(np.float32)
    inv = (np.float32(ROPE_THETA) ** (-np.arange(AXIS_FREQS, dtype=np.float32) / AXIS_FREQS)).astype(np.float32)
    ang = np.concatenate([row[:, None] * inv, col[:, None] * inv], axis=-1).astype(np.float32)
    cos, sin = np.cos(ang), np.sin(ang)
    c64 = np.concatenate([cos, cos], axis=-1)
    s64 = np.concatenate([-sin, sin], axis=-1)
    return (jnp.asarray(np.tile(c64, (1, 2)), F32), jnp.asarray(np.tile(s64, (1, 2)), F32))


def _log_sigmoid(x):
    return jnp.minimum(x, 0.0) - jnp.log1p(jnp.exp(-jnp.abs(x)))


def _prefix_sum_rows(x):
    row = lax.broadcasted_iota(jnp.int32, x.shape, 0)
    k = 1
    while k < x.shape[0]:
        x = x + jnp.where(row >= k, pltpu.roll(x, k, 0), 0.0)
        k *= 2
    return x


def _proj_kernel(*refs, segs, rope, with_nat):
    it = iter(refs)
    x_ref, mod_ref, g_ref, w_ref, wvt_ref, bg_ref = (next(it) for _ in range(6))
    cos_ref = sin_ref = None
    if rope:
        cos_ref, sin_ref = next(it), next(it)
    q_ref, kd_ref, vd_ref = next(it), next(it), next(it)
    nat_ref = next(it) if with_nat else None
    mq_ref, mk_ref, mv_ref, gates_ref, gates_t_ref, mo_ref, ga_ref, gm_ref = (next(it) for _ in range(8))

    x = x_ref[0]
    shift = mod_ref[0, 0:1, :]
    scale = mod_ref[0, 1:2, :]
    ms = jnp.mean(x * x, axis=-1, keepdims=True)
    xn = x * lax.rsqrt(ms + EPS) * g_ref[...]
    xb = (xn * (1.0 + scale) + shift).astype(BF16)

    def seg(name):
        lo, hi = segs[name]
        return jnp.dot(xb, w_ref[:, lo:hi], preferred_element_type=F32)

    uq = seg("q")
    ukd = seg("kd")
    if rope:
        uqr = seg("qrot")
        ukr = seg("kdrot")
        cs = cos_ref[...]
        sn = sin_ref[...]
        for h in range(ATT_HEADS // 2):
            sl = slice(h * LANES, (h + 1) * LANES)
            q_ref[0, h] = ((uq[:, sl] * cs + uqr[:, sl] * sn) * HEAD_DIM ** -0.5).astype(BF16)
        kd_ref[0] = (ukd * cs + ukr * sn).astype(BF16)
    else:
        for h in range(ATT_HEADS // 2):
            sl = slice(h * LANES, (h + 1) * LANES)
            q_ref[0, h] = (uq[:, sl] * HEAD_DIM ** -0.5).astype(BF16)
        kd_ref[0] = ukd.astype(BF16)
    ut = lax.dot_general(wvt_ref[...], xb, (((1,), (1,)), ((), ())), preferred_element_type=F32)
    vd_ref[0] = ut[:KV_W].astype(BF16)
    mk_ref[0] = ut[KV_W:].astype(BF16)
    if with_nat:
        nat_ref[0] = seg("nat")
    mq_ref[0] = (seg("mq") * ML_DK ** -0.5).astype(BF16)
    mv_ref[0] = seg("mv").astype(BF16)
    gates = seg("gates") + bg_ref[...]
    lf = _log_sigmoid(gates)
    csum = _prefix_sum_rows(lf)
    lane = lax.broadcasted_iota(jnp.int32, gates.shape, 1)
    cum = jnp.where(lane >= 2 * ML_HEADS, csum[-1:, :] - csum + lf, csum)
    gates_ref[0] = cum
    gates_t_ref[0, :GATE_W, :] = gates.T[:GATE_W, :]
    gates_t_ref[0, GATE_W:, :] = cum.T[:GATE_W, :]
    mo_ref[0] = _sigmoid(seg("mo")).astype(BF16)
    ga_ref[0] = _sigmoid(seg("ga")).astype(BF16)
    gm_ref[0] = _sigmoid(seg("gm")).astype(BF16)


def _project(x, mod, g_pre, w_in, b_gates, mod_row, rope, with_nat, tm, b0, bsz):
    _, seq, d = x.shape
    wp, wvd_t, bg, segs = _pack_w_in(w_in, b_gates, True, True)
    nw = wp.shape[1]
    tok3 = lambda b, i: (b, i, 0)
    in_specs = [pl.BlockSpec((1, tm, d), lambda b, i: (b + b0, i, 0)),
                pl.BlockSpec((1, 6, d), lambda b, i: (mod_row(b), 0, 0)),
                _const_spec((1, d)),
                pl.BlockSpec((d, nw), lambda b, i: (0, 0), pipeline_mode=pl.Buffered(1)),
                _const_spec((KV_W + MLQK_W, d)), _const_spec((1, LANES))]
    args = [x, mod, g_pre.reshape(1, d), wp, wvd_t, bg]
    if rope:
        cos_t, sin_t = _rope_tables(seq)
        in_specs += [pl.BlockSpec((tm, LANES), lambda b, i: (i, 0))] * 2
        args += [cos_t, sin_t]

    def tok_out(width, dtype):
        return (jax.ShapeDtypeStruct((bsz, seq, width), dtype), pl.BlockSpec((1, tm, width), tok3))

    outs = [(jax.ShapeDtypeStruct((bsz, ATT_HEADS // 2, seq, LANES), BF16),
             pl.BlockSpec((1, ATT_HEADS // 2, tm, LANES), lambda b, i: (b, 0, i, 0))),
            tok_out(KV_W, BF16),
            (jax.ShapeDtypeStruct((bsz, KV_W, seq), BF16),
             pl.BlockSpec((1, KV_W, tm), lambda b, i: (b, 0, i)))]
    if with_nat:
        outs.append(tok_out(2 * KV_W, F32))
    outs += [tok_out(MLQK_W, BF16),
             (jax.ShapeDtypeStruct((bsz, MLQK_W, seq), BF16),
              pl.BlockSpec((1, MLQK_W, tm), lambda b, i: (b, 0, i))),
             tok_out(MLV_W, BF16), tok_out(LANES, F32),
             (jax.ShapeDtypeStruct((bsz, 2 * GATE_W, seq), F32),
              pl.BlockSpec((1, 2 * GATE_W, tm), lambda b, i: (b, 0, i))),
             tok_out(MLV_W, BF16), tok_out(D_MODEL, BF16), tok_out(D_MODEL, BF16)]
    res = pl.pallas_call(
        functools.partial(_proj_kernel, segs=segs, rope=rope, with_nat=with_nat),
        grid=(bsz, seq // tm),
        in_specs=in_specs,
        out_specs=[o[1] for o in outs],
        out_shape=[o[0] for o in outs],
        compiler_params=_cparams(("parallel", "parallel")),
        name="proj_rope" if rope else "proj_ctx",
    )(*args)
    names = ["q", "kd", "vd"] + (["nat"] if with_nat else []) + ["mq", "mk", "mv", "gates", "gates_t", "mo", "ga", "gm"]
    return dict(zip(names, res))


def _attn_kernel(*refs, window, n_blocks, n_ctx):
    it = iter(refs)
    sink_ref, q_ref = next(it), next(it)
    if window:
        kp_ref, kc_ref, kn_ref, vp_ref, vc_ref, vn_ref = (next(it) for _ in range(6))
    ck_ref, cv_ref, o_ref = next(it), next(it), next(it)
    j = pl.program_id(1)
    n_win = 3 * BLOCK if window else 0
    cols = ATT_GROUP * BLOCK

    col_id = lax.broadcasted_iota(jnp.int32, (1, cols), 1)
    if window:
        kl = lax.broadcasted_iota(jnp.int32, (BLOCK, cols), 0)
        ql = lax.broadcasted_iota(jnp.int32, (BLOCK, cols), 1) & (BLOCK - 1)
        prev_ok = kl >= ql + jnp.where(j == 0, BLOCK, 0)
        next_ok = kl <= ql - jnp.where(j == n_blocks - 1, BLOCK, 0)
    lane_row = lax.broadcasted_iota(jnp.int32, (1, LANES), 1) < HEAD_DIM
    half_lo = jnp.where(lane_row, 1.0, 0.0).astype(BF16)
    half_hi = jnp.where(lane_row, 0.0, 1.0).astype(BF16)

    if window:
        keys = jnp.concatenate([kp_ref[0], kc_ref[0], kn_ref[0], ck_ref[0]], axis=0)
        vals_t = jnp.concatenate([vp_ref[0], vc_ref[0], vn_ref[0], cv_ref[0]], axis=1)
    else:
        keys = ck_ref[0]
        vals_t = cv_ref[0]

    for kvh in range(ATT_KV_HEADS):
        qs = jnp.concatenate([q_ref[0, g] * (half_lo if kvh == 0 else half_hi) for g in range(ATT_GROUP)], axis=0)
        s = lax.dot_general(keys, qs, (((1,), (1,)), ((), ())), preferred_element_type=F32)
        if window:
            s = jnp.concatenate([jnp.where(prev_ok, s[:BLOCK], NEG), s[BLOCK:2 * BLOCK],
                                 jnp.where(next_ok, s[2 * BLOCK:n_win], NEG), s[n_win:]], axis=0)
        snk = jnp.full((1, cols), sink_ref[kvh * ATT_GROUP + ATT_GROUP - 1], F32)
        for g in range(ATT_GROUP - 2, -1, -1):
            snk = jnp.where(col_id < (g + 1) * BLOCK, sink_ref[kvh * ATT_GROUP + g], snk)
        m = jnp.maximum(jnp.max(s, axis=0, keepdims=True), snk)
        p = jnp.exp(s - m)
        den = jnp.sum(p, axis=0, keepdims=True) + jnp.exp(snk - m)
        o_t = jnp.dot(vals_t, p.astype(BF16), preferred_element_type=F32) / den
        o_t = o_t[kvh * HEAD_DIM:(kvh + 1) * HEAD_DIM]
        for pair in range(ATT_GROUP // 2):
            two = jnp.concatenate([o_t[:, (2 * pair) * BLOCK:(2 * pair + 1) * BLOCK],
                                   o_t[:, (2 * pair + 1) * BLOCK:(2 * pair + 2) * BLOCK]], axis=0)
            col = (kvh * (ATT_GROUP // 2) + pair) * LANES
            o_ref[0, :, col:col + LANES] = two.T.astype(BF16)


def _attention(q, kd, vd_t, ckd, cvd_t, sink, window):
    bsz, _, seq, _ = q.shape
    nb = seq // BLOCK
    n_ctx = ckd.shape[1]
    in_specs = [pl.BlockSpec(memory_space=pltpu.SMEM),
                pl.BlockSpec((1, ATT_HEADS // 2, BLOCK, LANES), lambda b, j: (b, 0, j, 0))]
    args = [sink.astype(F32), q]
    if window:
        in_specs += [pl.BlockSpec((1, BLOCK, KV_W), lambda b, j: (b, jnp.maximum(j - 1, 0), 0)),
                     pl.BlockSpec((1, BLOCK, KV_W), lambda b, j: (b, j, 0)),
                     pl.BlockSpec((1, BLOCK, KV_W), lambda b, j: (b, jnp.minimum(j + 1, nb - 1), 0)),
                     pl.BlockSpec((1, KV_W, BLOCK), lambda b, j: (b, 0, jnp.maximum(j - 1, 0))),
                     pl.BlockSpec((1, KV_W, BLOCK), lambda b, j: (b, 0, j)),
                     pl.BlockSpec((1, KV_W, BLOCK), lambda b, j: (b, 0, jnp.minimum(j + 1, nb - 1)))]
        args += [kd, kd, kd, vd_t, vd_t, vd_t]
    in_specs += [pl.BlockSpec((1, n_ctx, KV_W), lambda b, j: (b, 0, 0)),
                 pl.BlockSpec((1, KV_W, n_ctx), lambda b, j: (b, 0, 0))]
    args += [ckd, cvd_t]
    return pl.pallas_call(
        functools.partial(_attn_kernel, window=window, n_blocks=nb, n_ctx=n_ctx),
        grid=(bsz, nb),
        in_specs=in_specs,
        out_specs=pl.BlockSpec((1, BLOCK, Q_W), lambda b, j: (b, j, 0)),
        out_shape=jax.ShapeDtypeStruct((bsz, seq, Q_W), BF16),
        compiler_params=_cparams(("parallel", "parallel")),
        name="attn_window" if window else "attn_ctx",
    )(*args)


def _flat_heads(t, transpose):
    b, l = t.shape[:2]
    out = t.reshape(b, l, KV_W).astype(BF16)
    return jnp.swapaxes(out, 1, 2) if transpose else out


def _mlstm_kernel(*refs, chunk_len, n_chunks, has_init, emit_state):
    it = iter(refs)
    qkvg = [[next(it) for _ in range(5)] for _ in range(2)]
    if has_init:
        c0_ref, n0_ref, m0_ref = next(it), next(it), next(it)
    h_refs = [next(it), next(it)]
    if emit_state:
        co_ref, no_ref, mo_ref = next(it), next(it), next(it)
    st_s, m_s = next(it), next(it)
    t = pl.program_id(1)
    L = chunk_len

    @pl.when(t == 0)
    def _():
        for d in range(2):
            for h in range(ML_HEADS):
                if has_init:
                    st_s[d, h, :, :ML_DV] = c0_ref[0, d, h].T
                    st_s[d, h, :, ML_DV:] = jnp.broadcast_to(n0_ref[0, d, h:h + 1, :], (LANES, ML_DK)).T
                    m_s[d, h] = m0_ref[0, d, h:h + 1, :]
                else:
                    st_s[d, h] = jnp.zeros((ML_DK, ML_DV + LANES), F32)
                    m_s[d, h] = jnp.zeros((1, LANES), F32)

    ri = lax.broadcasted_iota(jnp.int32, (L, L), 0)
    ci = lax.broadcasted_iota(jnp.int32, (L, L), 1)
    ones_blk = jnp.ones((L, LANES), BF16)

    for d in range(2):
        q_ref, kt_ref, v_ref, cum_ref, rows_ref = qkvg[d]
        tri = (ci <= ri) if d == 0 else (ci >= ri)
        bc = cum_ref[0]
        g_t = rows_ref[0, :GATE_W, :]
        br = rows_ref[0, GATE_W:, :]
        last = L - 1 if d == 0 else 0

        for h in range(ML_HEADS):
            icol_i = 2 * ML_HEADS * d + h
            fcol = icol_i + ML_HEADS
            bcol = bc[:, fcol:fcol + 1]
            brow = br[fcol:fcol + 1, :]
            irow = g_t[icol_i:icol_i + 1, :]
            b_last = bc[last:last + 1, fcol:fcol + 1]
            m_prev = m_s[d, h][:, 0:1]
            st_prev = st_s[d, h]
            qh = q_ref[0, :, h * ML_DK:(h + 1) * ML_DK]
            kt = kt_ref[0, h * ML_DK:(h + 1) * ML_DK, :]
            v_aug = jnp.concatenate([v_ref[0, :, h * ML_DV:(h + 1) * ML_DV], ones_blk], axis=1)

            rel = jnp.where(tri, irow - brow, NEG)
            inter = bcol + m_prev
            m_t = jnp.maximum(inter, bcol + jnp.max(rel, axis=-1, keepdims=True))
            w = jnp.exp(rel + (bcol - m_t))
            w_inter = jnp.exp(inter - m_t)
            s = jnp.dot(qh, kt, preferred_element_type=F32) * w
            lhs = jnp.concatenate([s.astype(BF16), (qh.astype(F32) * w_inter).astype(BF16)], axis=1)
            rhs = jnp.concatenate([v_aug, st_prev.astype(BF16)], axis=0)
            mix = jnp.dot(lhs, rhs, preferred_element_type=F32)
            inv = 1.0 / jnp.maximum(jnp.abs(mix[:, ML_DV:]), jnp.exp(-m_t))
            for c in range(ML_DV // LANES):
                lo = h * ML_DV + c * LANES
                h_refs[d][0, :, lo:lo + LANES] = mix[:, c * LANES:(c + 1) * LANES] * inv

            a = b_last - brow + irow
            m_new = jnp.maximum(b_last + m_prev, jnp.max(a, axis=-1, keepdims=True))
            wk = jnp.exp(a - m_new)
            decay = jnp.exp(b_last + m_prev - m_new)
            kw_t = (kt.astype(F32) * wk).astype(BF16)
            st_new = decay * st_prev + jnp.dot(kw_t, v_aug, preferred_element_type=F32)
            st_s[d, h] = st_new
            m_s[d, h] = jnp.broadcast_to(m_new, (1, LANES))
            if emit_state:
                @pl.when(t == n_chunks - 1)
                def _():
                    co_ref[0, d, h] = st_new[:, :ML_DV].T
                    no_ref[0, d, h:h + 1, :] = st_new[:, ML_DV:].T[0:1, :]
                    mo_ref[0, d, h:h + 1, :] = jnp.broadcast_to(m_new, (1, LANES))


def _mlstm(mq, mk_t, mv, gate_cum, gate_rows, init, emit_state, b0=0):
    bsz, seq, _ = mq.shape
    chunk_len = min(ML_CHUNK, seq)
    nc = seq // chunk_len
    fwd = lambda b, t: (b, t, 0)
    bwd = lambda b, t: (b, nc - 1 - t, 0)
    fwd_t = lambda b, t: (b, 0, t)
    bwd_t = lambda b, t: (b, 0, nc - 1 - t)
    state5 = lambda b, t: (b, 0, 0, 0, 0)
    state4 = lambda b, t: (b, 0, 0, 0)
    init5 = lambda b, t: (b + b0, 0, 0, 0, 0)
    init4 = lambda b, t: (b + b0, 0, 0, 0)
    in_specs, args = [], []
    for chunk, chunk_t in ((fwd, fwd_t), (bwd, bwd_t)):
        in_specs += [pl.BlockSpec((1, chunk_len, MLQK_W), chunk), pl.BlockSpec((1, MLQK_W, chunk_len), chunk_t),
                     pl.BlockSpec((1, chunk_len, MLV_W), chunk), pl.BlockSpec((1, chunk_len, LANES), chunk),
                     pl.BlockSpec((1, 2 * GATE_W, chunk_len), chunk_t)]
        args += [mq, mk_t, mv, gate_cum, gate_rows]
    if init is not None:
        c0, n0, m0 = init
        in_specs += [pl.BlockSpec((1, 2, ML_HEADS, ML_DV, ML_DK), init5),
                     pl.BlockSpec((1, 2, ML_HEADS, ML_DK), init4),
                     pl.BlockSpec((1, 2, ML_HEADS, LANES), init4)]
        args += [c0.astype(F32), n0.astype(F32),
                 jnp.broadcast_to(m0.astype(F32)[..., None], m0.shape + (LANES,))]
    out_shape = [jax.ShapeDtypeStruct((bsz, seq, MLV_W), F32)] * 2
    out_specs = [pl.BlockSpec((1, chunk_len, MLV_W), fwd), pl.BlockSpec((1, chunk_len, MLV_W), bwd)]
    if emit_state:
        out_shape += [jax.ShapeDtypeStruct((bsz, 2, ML_HEADS, ML_DV, ML_DK), F32),
                      jax.ShapeDtypeStruct((bsz, 2, ML_HEADS, ML_DK), F32),
                      jax.ShapeDtypeStruct((bsz, 2, ML_HEADS, LANES), F32)]
        out_specs += [pl.BlockSpec((1, 2, ML_HEADS, ML_DV, ML_DK), state5),
                      pl.BlockSpec((1, 2, ML_HEADS, ML_DK), state4),
                      pl.BlockSpec((1, 2, ML_HEADS, LANES), state4)]
    return pl.pallas_call(
        functools.partial(_mlstm_kernel, chunk_len=chunk_len, n_chunks=nc, has_init=init is not None, emit_state=emit_state),
        grid=(bsz, nc),
        in_specs=in_specs,
        out_specs=out_specs,
        out_shape=out_shape,
        scratch_shapes=[pltpu.VMEM((2, ML_HEADS, ML_DK, ML_DV + LANES), F32),
                        pltpu.VMEM((2, ML_HEADS, 1, LANES), F32)],
        compiler_params=_cparams(("parallel", "arbitrary")),
        name="mlstm_state" if emit_state else "mlstm",
    )(*args)


def _sigmoid(x):
    return 0.5 * jnp.tanh(0.5 * x) + 0.5


def _rms(x, g):
    return x * lax.rsqrt(jnp.mean(x * x, axis=-1, keepdims=True) + EPS) * g


def _pack_bf16_pairs(x):
    n = x.shape[1] // 2
    lo = pltpu.bitcast(x[:, :n].astype(BF16).astype(F32), jnp.int32)
    hi = pltpu.bitcast(x[:, n:].astype(BF16).astype(F32), jnp.int32)
    return lax.shift_right_logical(lo, 16) | hi


def _unpack_bf16_pairs(p):
    lo = pltpu.bitcast(lax.shift_left(p, 16), F32)
    hi = pltpu.bitcast(p & jnp.int32(-65536), F32)
    return lo, hi


def _merge_kernel(att_ref, hf_ref, hb_ref, mo_ref, ga_ref, gm_ref, x_ref, mod_ref, nml_ref, gpm_ref, gpf_ref,
                  wua_ref, wum_ref, wo_ref, wr_ref, br_ref, x1_ref, h2_ref, idx_ref, gate_ref, cnt_ref):
    hsum = hf_ref[0] + hb_ref[0]
    parts = []
    for h in range(ML_HEADS):
        hh = hsum[:, h * ML_DV:(h + 1) * ML_DV]
        parts.append(hh * lax.rsqrt(jnp.mean(hh * hh, axis=-1, keepdims=True) + EPS))
    hn = jnp.concatenate(parts, axis=-1) * nml_ref[...]
    ml = (hn * mo_ref[0].astype(F32)).astype(BF16)
    a = jnp.dot(att_ref[0], wua_ref[...], preferred_element_type=F32)
    m = jnp.dot(ml, wum_ref[...], preferred_element_type=F32)
    z = (ga_ref[0].astype(F32) * a + gm_ref[0].astype(F32) * m).astype(BF16)
    mix = jnp.dot(z, wo_ref[...], preferred_element_type=F32)
    gate1 = mod_ref[0, 2:3, :]
    shift2 = mod_ref[0, 3:4, :]
    scale2 = mod_ref[0, 4:5, :]
    x1 = x_ref[0] + gate1 * _rms(mix, gpm_ref[...])
    x1_ref[0] = x1
    h2 = _rms(x1, gpf_ref[...]) * (1.0 + scale2) + shift2
    h2_ref[0] = _pack_bf16_pairs(h2)

    h2_hi = h2.astype(BF16)
    h2_lo = (h2 - h2_hi.astype(F32)).astype(BF16)
    wr = wr_ref[...]
    wr_hi = wr.astype(BF16)
    wr_lo = (wr - wr_hi.astype(F32)).astype(BF16)
    hi_terms = jnp.dot(h2_hi, jnp.concatenate([wr_hi, wr_lo], axis=1), preferred_element_type=F32)
    logits = (hi_terms[:, :LANES] + hi_terms[:, LANES:]
              + jnp.dot(h2_lo, wr_hi, preferred_element_type=F32)) + br_ref[...]
    lane = lax.broadcasted_iota(jnp.int32, logits.shape, 1).astype(F32)
    work = jnp.where(lane < N_EXPERTS, logits, -jnp.inf)
    idx_out = jnp.zeros(logits.shape, F32)
    val_out = jnp.zeros(logits.shape, F32)
    picked = jnp.zeros(logits.shape, F32)
    top0 = None
    esum = None
    for k in range(TOP_K):
        mx = jnp.max(work, axis=-1, keepdims=True)
        sel = jnp.min(jnp.where(work == mx, lane, float(LANES)), axis=-1, keepdims=True)
        if k == 0:
            top0 = mx
        e = jnp.exp(mx - top0)
        esum = e if k == 0 else esum + e
        idx_out = jnp.where(lane == k, sel, idx_out)
        val_out = jnp.where(lane == k, e, val_out)
        picked = jnp.where(lane == sel, 1.0, picked)
        work = jnp.where(lane == sel, -jnp.inf, work)
    idx_ref[0] = idx_out.astype(jnp.int32)
    gate_ref[0] = val_out / esum
    cnt_ref[0] = jnp.sum(picked, axis=0, keepdims=True)


def _merge(att, hf, hb, pj, x, mod, mod_row, p, tm, b0):
    bsz = att.shape[0]
    _, seq, d = x.shape
    tok3 = lambda b, i: (b, i, 0)
    wr = jnp.pad(p["w_router"].astype(F32), ((0, 0), (0, LANES - N_EXPERTS)))
    br = jnp.pad(p["b_router"].astype(F32), (0, LANES - N_EXPERTS)).reshape(1, LANES)
    row = lambda v: v.astype(F32).reshape(1, -1)
    in_specs = [pl.BlockSpec((1, tm, Q_W), tok3),
                pl.BlockSpec((1, tm, MLV_W), tok3), pl.BlockSpec((1, tm, MLV_W), tok3),
                pl.BlockSpec((1, tm, MLV_W), tok3), pl.BlockSpec((1, tm, d), tok3), pl.BlockSpec((1, tm, d), tok3),
                pl.BlockSpec((1, tm, d), lambda b, i: (b + b0, i, 0)),
                pl.BlockSpec((1, 6, d), lambda b, i: (mod_row(b), 0, 0)),
                _const_spec((1, MLV_W)), _const_spec((1, d)), _const_spec((1, d)),
                _const_spec((Q_W, d)), _const_spec((MLV_W, d)), _const_spec((d, d)),
                _const_spec((d, LANES)), _const_spec((1, LANES))]
    out_shape = [jax.ShapeDtypeStruct((bsz, seq, d), F32), jax.ShapeDtypeStruct((bsz, seq, d // 2), jnp.int32),
                 jax.ShapeDtypeStruct((bsz, seq, LANES), jnp.int32), jax.ShapeDtypeStruct((bsz, seq, LANES), F32),
                 jax.ShapeDtypeStruct((bsz * (seq // tm), 1, LANES), F32)]
    out_specs = [pl.BlockSpec((1, tm, d), tok3), pl.BlockSpec((1, tm, d // 2), tok3),
                 pl.BlockSpec((1, tm, LANES), tok3), pl.BlockSpec((1, tm, LANES), tok3),
                 pl.BlockSpec((1, 1, LANES), lambda b, i: (b * (seq // tm) + i, 0, 0))]
    return pl.pallas_call(
        _merge_kernel,
        grid=(bsz, seq // tm),
        in_specs=in_specs,
        out_specs=out_specs,
        out_shape=out_shape,
        compiler_params=_cparams(("parallel", "parallel")),
        name="merge_router",
    )(att, hf, hb, pj["mo"], pj["ga"], pj["gm"], x, mod, row(p["norm_ml"]), row(p["g_post_mix"]),
      row(p["g_pre_ffn"]), p["w_up_att"].astype(BF16), p["w_up_ml"].astype(BF16), p["w_out"].astype(BF16), wr, br)


def _sc_gather_rows(table, idx):
    n = idx.shape[0]
    width = table.shape[1]
    chunk = SC_GATHER_CHUNK
    n_workers = SC_CORES * SC_SUBCORES
    rows_per_worker = n // n_workers
    n_pairs = rows_per_worker // (2 * chunk)
    assert n_pairs * 2 * chunk * n_workers == n
    mesh = plsc.VectorSubcoreMesh(core_axis_name="c", subcore_axis_name="s")

    def body(table_hbm, idx_hbm, out_hbm, idx_v, rows_v, gsem, osem):
        wid = lax.axis_index("s") * SC_CORES + lax.axis_index("c")
        base0 = wid * rows_per_worker

        def gather_copy(b):
            return pltpu.make_async_copy(table_hbm.at[idx_v.at[b]], rows_v.at[b], gsem.at[b])

        def write_copy(ci, b):
            base = pl.multiple_of(base0 + ci * chunk, 8)
            return pltpu.make_async_copy(rows_v.at[b], out_hbm.at[pl.ds(base, chunk)], osem.at[b])

        def issue(ci, b):
            base = pl.multiple_of(base0 + ci * chunk, 8)
            pltpu.sync_copy(idx_hbm.at[pl.ds(base, chunk)], idx_v.at[b])
            gather_copy(b).start()

        def finish(ci, b):
            gather_copy(b).wait()
            write_copy(ci, b).start()

        issue(0, 0)

        @pl.loop(0, n_pairs)
        def _(j):
            @pl.when(j > 0)
            def _():
                write_copy(2 * j - 1, 1).wait()
            issue(2 * j + 1, 1)
            finish(2 * j, 0)

            @pl.when(j < n_pairs - 1)
            def _():
                write_copy(2 * j, 0).wait()
                issue(2 * j + 2, 0)
            finish(2 * j + 1, 1)

        write_copy(2 * n_pairs - 2, 0).wait()
        write_copy(2 * n_pairs - 1, 1).wait()

    return pl.kernel(
        body, mesh=mesh,
        out_type=jax.ShapeDtypeStruct((n, width), table.dtype),
        scratch_types=[pltpu.VMEM((2, chunk), jnp.int32), pltpu.VMEM((2, chunk, width), table.dtype),
                       pltpu.SemaphoreType.DMA((2,)), pltpu.SemaphoreType.DMA((2,))],
    )(table, idx)


def _ffn_kernel(be_ref, nu_ref, xs_ref, w1_ref, b1_ref, w2_ref, b2_ref, ys_ref, *bf16_refs, weights_are_bf16):
    i = pl.program_id(0)
    e = be_ref[i]
    e_prev = be_ref[jnp.maximum(i - 1, 0)]
    if weights_are_bf16:
        w1b_ref, w2b_ref = w1_ref, w2_ref
    else:
        w1b_ref, w2b_ref = bf16_refs

        @pl.when((i == 0) | (e != e_prev))
        def _():
            w1b_ref[0] = w1_ref[0].astype(BF16)
            w2b_ref[0] = w2_ref[0].astype(BF16)

    @pl.when(i < nu_ref[0])
    def _():
        x_lo, x_hi = _unpack_bf16_pairs(xs_ref[...])
        xb = jnp.concatenate([x_lo.astype(BF16), x_hi.astype(BF16)], axis=1)
        hmid = jnp.dot(xb, w1b_ref[0], preferred_element_type=F32) + b1_ref[0]
        glu = jnp.minimum(hmid[:, :D_FF], SWIGLU_LIMIT)
        lin = jnp.clip(hmid[:, D_FF:], -SWIGLU_LIMIT, SWIGLU_LIMIT)
        act = ((lin + 1.0) * glu * _sigmoid(SWIGLU_ALPHA * glu)).astype(BF16)
        y = jnp.dot(act, w2b_ref[0], preferred_element_type=F32) + b2_ref[0]
        ys_ref[...] = _pack_bf16_pairs(y)

    @pl.when(i >= nu_ref[0])
    def _():
        ys_ref[...] = jnp.zeros(ys_ref.shape, jnp.int32)


def _sc_scatter_rows(tables, positions, n_rows):
    width = tables[0].shape[1]
    chunk = SC_GATHER_CHUNK
    n_workers = SC_CORES * SC_SUBCORES
    n_streams = len(tables)
    mesh = plsc.VectorSubcoreMesh(core_axis_name="c", subcore_axis_name="s")

    def body(*refs):
        table_refs = refs[0:2 * n_streams:2]
        pos_refs = refs[1:2 * n_streams:2]
        out_hbm, idx_v, rows_v, lsem, ssem = refs[2 * n_streams:]
        wid = lax.axis_index("s") * SC_CORES + lax.axis_index("c")

        for table_hbm, pos_hbm in zip(table_refs, pos_refs):
            tok_per_worker = table_hbm.shape[0] // n_workers
            n_pairs = tok_per_worker // (2 * chunk)
            assert n_pairs * 2 * chunk * n_workers == table_hbm.shape[0]
            base0 = wid * tok_per_worker

            def load_copy(ci, b):
                base = pl.multiple_of(base0 + ci * chunk, 8)
                return pltpu.make_async_copy(table_hbm.at[pl.ds(base, chunk)], rows_v.at[b], lsem.at[b])

            def scatter_copy(b, k):
                return pltpu.make_async_copy(rows_v.at[b], out_hbm.at[idx_v.at[b, k]], ssem.at[b])

            def load(ci, b):
                base = pl.multiple_of(base0 + ci * chunk, 8)
                load_copy(ci, b).start()
                for k in range(TOP_K):
                    pltpu.sync_copy(pos_hbm.at[k, pl.ds(base, chunk)], idx_v.at[b, k])

            def scatter(ci, b):
                load_copy(ci, b).wait()
                for k in range(TOP_K):
                    scatter_copy(b, k).start()

            def drain(b):
                for k in range(TOP_K):
                    scatter_copy(b, k).wait()

            load(0, 0)

            @pl.loop(0, n_pairs)
            def _(j):
                @pl.when(j > 0)
                def _():
                    drain(1)
                load(2 * j + 1, 1)
                scatter(2 * j, 0)

                @pl.when(j < n_pairs - 1)
                def _():
                    drain(0)
                    load(2 * j + 2, 0)
                scatter(2 * j + 1, 1)

            drain(0)
            drain(1)

    args = [a for pair in zip(tables, positions) for a in pair]
    return pl.kernel(
        body, mesh=mesh,
        out_type=jax.ShapeDtypeStruct((n_rows, width), tables[0].dtype),
        scratch_types=[pltpu.VMEM((2, TOP_K, chunk), jnp.int32), pltpu.VMEM((2, chunk, width), tables[0].dtype),
                       pltpu.SemaphoreType.DMA((2,)), pltpu.SemaphoreType.DMA((2,))],
    )(*args)


def _pos_kernel(idx_ref, base_ref, pos_ref):
    idx = idx_ref[...]
    rows = idx.shape[0]
    lane = lax.broadcasted_iota(jnp.int32, idx.shape, 1)
    hots = [jnp.where(lane == idx[:, k:k + 1], 1.0, 0.0) for k in range(TOP_K)]
    cnt = hots[0] + hots[1] + hots[2] + hots[3]
    ri = lax.broadcasted_iota(jnp.int32, (rows, rows), 0)
    ci = lax.broadcasted_iota(jnp.int32, (rows, rows), 1)
    earlier = jnp.where(ci < ri, 1.0, 0.0).astype(BF16)
    ahead = jnp.dot(earlier, cnt.astype(BF16), preferred_element_type=F32) + base_ref[0]
    posmat = jnp.zeros(idx.shape, F32)
    for k in range(TOP_K):
        posmat = jnp.where(lane == k, jnp.sum(hots[k] * ahead, axis=-1, keepdims=True), posmat)
    pos_ref[...] = posmat.T[:TOP_K, :].astype(jnp.int32)


def _route(idx_list, count_list, every_expert):
    tiles = [c.shape[0] for c in count_list]
    tile_tok = [idx.size // LANES // n for idx, n in zip(idx_list, tiles)]
    n_tok = sum(n * sz for n, sz in zip(tiles, tile_tok))
    n_blocks = n_tok * TOP_K // MOE_ROWS + N_EXPERTS
    cnt = jnp.concatenate([c.reshape(-1, LANES) for c in count_list], axis=0).astype(jnp.int32)
    tile_off = jnp.cumsum(cnt, axis=0) - cnt
    total = jnp.sum(cnt, axis=0)
    padded = (total + MOE_ROWS - 1) // MOE_ROWS * MOE_ROWS
    if every_expert:
        padded = jnp.where(jnp.arange(LANES) < N_EXPERTS, jnp.maximum(padded, MOE_ROWS), 0)
    pend = jnp.cumsum(padded)
    base = ((pend - padded)[None, :] + tile_off).astype(F32)
    starts = jnp.arange(n_blocks, dtype=jnp.int32) * MOE_ROWS
    block_e = jnp.minimum(jnp.sum(pend[None, :N_EXPERTS] <= starts[:, None], axis=1), N_EXPERTS - 1).astype(jnp.int32)
    n_used = (pend[N_EXPERTS - 1] // MOE_ROWS).astype(jnp.int32).reshape(1)
    positions, first = [], 0
    for idx, n_tiles, tile in zip(idx_list, tiles, tile_tok):
        stream_tok = n_tiles * tile
        positions.append(pl.pallas_call(
            _pos_kernel,
            grid=(n_tiles,),
            in_specs=[pl.BlockSpec((tile, LANES), lambda i: (i, 0)),
                      pl.BlockSpec((1, 1, LANES), lambda i: (i, 0, 0))],
            out_specs=pl.BlockSpec((TOP_K, tile), lambda i: (0, i)),
            out_shape=jax.ShapeDtypeStruct((TOP_K, stream_tok), jnp.int32),
            compiler_params=_cparams(("parallel",)),
            name="route_pos",
        )(idx.reshape(stream_tok, LANES), base[first:first + n_tiles].reshape(n_tiles, 1, LANES)))
        first += n_tiles
    return block_e, n_used, positions


def _moe(h2p_list, idx_list, count_list, w1, b1, w2, b2):
    half = h2p_list[0].shape[1]
    d = 2 * half
    cast_here = w1.dtype != BF16
    block_e, n_used, positions = _route(idx_list, count_list, every_expert=cast_here)
    n_blocks = block_e.shape[0]
    xs = _sc_scatter_rows(h2p_list, positions, n_blocks * MOE_ROWS)
    grid_spec = pltpu.PrefetchScalarGridSpec(
        num_scalar_prefetch=2,
        grid=(n_blocks,),
        in_specs=[pl.BlockSpec((MOE_ROWS, half), lambda i, be, nu: (i, 0)),
                  pl.BlockSpec((1, d, 2 * D_FF), lambda i, be, nu: (be[i], 0, 0)),
                  pl.BlockSpec((1, 1, 2 * D_FF), lambda i, be, nu: (be[i], 0, 0)),
                  pl.BlockSpec((1, D_FF, d), lambda i, be, nu: (be[i], 0, 0)),
                  pl.BlockSpec((1, 1, d), lambda i, be, nu: (be[i], 0, 0))],
        out_specs=[pl.BlockSpec((MOE_ROWS, half), lambda i, be, nu: (i, 0))] + (
            [pl.BlockSpec((1, d, 2 * D_FF), lambda i, be, nu: (be[i], 0, 0)),
             pl.BlockSpec((1, D_FF, d), lambda i, be, nu: (be[i], 0, 0))] if cast_here else []),
    )
    out_shape = [jax.ShapeDtypeStruct((n_blocks * MOE_ROWS, half), jnp.int32)]
    if cast_here:
        out_shape += [jax.ShapeDtypeStruct(w1.shape, BF16), jax.ShapeDtypeStruct(w2.shape, BF16)]
    res = pl.pallas_call(
        functools.partial(_ffn_kernel, weights_are_bf16=not cast_here),
        grid_spec=grid_spec,
        out_shape=out_shape,
        compiler_params=_cparams(("arbitrary",)),
        name="moe_ffn_cast" if cast_here else "moe_ffn",
    )(block_e, n_used, xs, w1, b1.reshape(N_EXPERTS, 1, -1), w2, b2.reshape(N_EXPERTS, 1, -1))
    w1b, w2b = (res[1], res[2]) if cast_here else (w1, w2)
    return [_sc_gather_rows(res[0], pos.reshape(-1)) for pos in positions], w1b, w2b


def _final_kernel(x1_ref, y0_ref, y1_ref, y2_ref, y3_ref, gate_ref, mod_ref, g_ref, *rest):
    o_ref = rest[-1]
    gates = gate_ref[0]
    lo = hi = None
    for k, y_ref in enumerate((y0_ref, y1_ref, y2_ref, y3_ref)):
        y_lo, y_hi = _unpack_bf16_pairs(y_ref[0, 0])
        gk = gates[:, k:k + 1]
        lo = gk * y_lo if k == 0 else lo + gk * y_lo
        hi = gk * y_hi if k == 0 else hi + gk * y_hi
    y = jnp.concatenate([lo, hi], axis=-1)
    gate2 = mod_ref[0, 5:6, :]
    o_ref[0] = x1_ref[0] + gate2 * _rms(y, g_ref[...])


def _final(x1, yg, gate, mod, mod_row, g_post_ffn, tm, b0, n_batch, out_prev):
    bsz, seq, d = x1.shape
    tok3 = lambda b, i: (b, i, 0)
    yg = yg.reshape(TOP_K, bsz, seq, d // 2)
    slot_specs = [pl.BlockSpec((1, 1, tm, d // 2), functools.partial(lambda b, i, k: (k, b, i, 0), k=k))
                  for k in range(TOP_K)]
    in_specs = ([pl.BlockSpec((1, tm, d), tok3)] + slot_specs +
                [pl.BlockSpec((1, tm, LANES), tok3),
                 pl.BlockSpec((1, 6, d), lambda b, i: (mod_row(b), 0, 0)), _const_spec((1, d))])
    args = [x1, yg, yg, yg, yg, gate, mod, g_post_ffn.astype(F32).reshape(1, d)]
    aliases = {}
    if out_prev is not None:
        in_specs.append(pl.BlockSpec(memory_space=pl.ANY))
        args.append(out_prev)
        aliases = {len(args) - 1: 0}
    return pl.pallas_call(
        _final_kernel,
        grid=(bsz, seq // tm),
        in_specs=in_specs,
        out_specs=pl.BlockSpec((1, tm, d), lambda b, i: (b + b0, i, 0)),
        out_shape=jax.ShapeDtypeStruct((n_batch, seq, d), F32),
        input_output_aliases=aliases,
        compiler_params=_cparams(("parallel", "parallel")),
        name="final_residual",
    )(*args)


def _stream(x, mod, mod_row, p, ctx_kv, init_state, rope, tm, b0, bsz):
    _, seq, d = x.shape
    is_ctx = ctx_kv is None
    pj = _project(x, mod, p["g_pre_mix"], p["w_in"], p["b_gates"], mod_row, rope, is_ctx, tm, b0, bsz)
    if is_ctx:
        att = _attention(pj["q"], None, None, pj["kd"], pj["vd"], p["sink"], window=False)
    else:
        att = _attention(pj["q"], pj["kd"], pj["vd"], ctx_kv[0], ctx_kv[1], p["sink"], window=True)
    assert min(ML_CHUNK, seq) == tm
    ml = _mlstm(pj["mq"], pj["mk"], pj["mv"], pj["gates"], pj["gates_t"], init_state, emit_state=is_ctx, b0=b0)
    x1, h2, idx, gate, tile_counts = _merge(att, ml[0], ml[1], pj, x, mod, mod_row, p, tm, b0)
    moe_in = (h2.reshape(bsz * seq, d // 2), idx, tile_counts)
    return x1, gate, moe_in, pj, ml


def kernel(x_prompt, x_sample, c, cache_k, cache_v, state_C, state_n, state_m, c_ctx, w_ada, b_ada, g_pre_mix,
           w_in, b_gates, attn_sink, norm_mlstm, w_up_att, w_up_ml, w_out, g_post_mix, g_pre_ffn, w_router,
           b_router, w1, b1, w2, b2, g_post_ffn):
    depth = w_ada.shape[0]
    n_dec = c.shape[0]
    cond = jnp.concatenate([c_ctx[None, :], c], axis=0).astype(F32)
    cond = jnp.pad(cond, ((0, 16 - cond.shape[0]), (0, 0)))
    y_prompt, y_sample = x_prompt, x_sample
    ks_, vs_, cs_, ns_, ms_ = [], [], [], [], []
    for l in range(depth):
        p = dict(g_pre_mix=g_pre_mix[l], w_in=w_in[l], b_gates=b_gates[l], sink=attn_sink[l], norm_ml=norm_mlstm[l],
                 w_up_att=w_up_att[l], w_up_ml=w_up_ml[l], w_out=w_out[l], g_post_mix=g_post_mix[l],
                 g_pre_ffn=g_pre_ffn[l], w_router=w_router[l], b_router=b_router[l], w1=w1[l], b1=b1[l],
                 w2=w2[l], b2=b2[l], g_post_ffn=g_post_ffn[l])
        mod = _adaln(cond, w_ada[l], b_ada[l]).reshape(16, 6, D_MODEL)
        row_p = lambda b: 0
        bsz, seq = x_prompt.shape[:2]
        x1_p, gate_p, moe_p, pj, ml = _stream(y_prompt, mod, row_p, p, None, None, False, 256, 0, bsz)
        nat = pj["nat"]
        ks_.append(nat[..., :KV_W].reshape(bsz, seq, ATT_KV_HEADS, HEAD_DIM))
        vs_.append(nat[..., KV_W:].reshape(bsz, seq, ATT_KV_HEADS, HEAD_DIM))
        cs_.append(ml[2])
        ns_.append(ml[3])
        ms_.append(ml[4][..., 0])
        ctx_kv = (_flat_heads(cache_k[:, l], False), _flat_heads(cache_v[:, l], True))
        init = (state_C[:, l], state_n[:, l], state_m[:, l])
        half = n_dec // 2
        experts = (p["w1"], p["b1"].astype(F32), p["w2"], p["b2"].astype(F32))
        groups = []
        for b0 in (0, half):
            row_s = functools.partial(lambda b, off: b + off + 1, off=b0)
            sl = slice(b0, b0 + half)
            x1_s, gate_s, moe_s, _, _ = _stream(y_sample, mod, row_s, p, (ctx_kv[0][sl], ctx_kv[1][sl]),
                                                init, True, 512, b0, half)
            groups.append((x1_s, gate_s, moe_s, row_s))
        (x1_a, gate_a, moe_a, row_a), (x1_b, gate_b, moe_b, row_b) = groups
        (yg_p, yg_a), w1b, w2b = _moe([moe_p[0], moe_a[0]], [moe_p[1], moe_a[1]], [moe_p[2], moe_a[2]], *experts)
        (yg_b,), _, _ = _moe([moe_b[0]], [moe_b[1]], [moe_b[2]], w1b, experts[1], w2b, experts[3])
        y_prompt = _final(x1_p, yg_p, gate_p, mod, row_p, p["g_post_ffn"], 256, 0, bsz, None)
        y_sample = _final(x1_a, yg_a, gate_a, mod, row_a, p["g_post_ffn"], 512, 0, n_dec, None)
        y_sample = _final(x1_b, yg_b, gate_b, mod, row_b, p["g_post_ffn"], 512, half, n_dec, y_sample)
    return (y_prompt, y_sample, jnp.stack(ks_, axis=1), jnp.stack(vs_, axis=1), jnp.stack(cs_, axis=1),
            jnp.stack(ns_, axis=1), jnp.stack(ms_, axis=1))
```

```python
import functools

import numpy as np
import jax
import jax.numpy as jnp
from jax import lax
from jax.experimental import pallas as pl
from jax.experimental.pallas import tpu as pltpu
from jax.experimental.pallas import tpu_sc as plsc

F32 = jnp.float32
BF16 = jnp.bfloat16

D_MODEL = 1024
GRID_W = 64
ATT_HEADS = 8
ATT_KV_HEADS = 2
ATT_GROUP = ATT_HEADS // ATT_KV_HEADS
HEAD_DIM = 64
BLOCK = 128
ROPE_THETA = 10000.0
AXIS_FREQS = HEAD_DIM // 4
ML_HEADS = 4
ML_DK = 128
ML_DV = 256
ML_CHUNK = 512
N_EXPERTS = 32
TOP_K = 4
D_FF = 1024
SWIGLU_ALPHA = 1.702
SWIGLU_LIMIT = 7.0
EPS = 1e-6
NEG = -1e30

Q_W = ATT_HEADS * HEAD_DIM
KV_W = ATT_KV_HEADS * HEAD_DIM
MLQK_W = ML_HEADS * ML_DK
MLV_W = ML_HEADS * ML_DV
GATE_W = 4 * ML_HEADS

LANES = 128
VMEM_LIMIT = 56 * 1024 * 1024
MOE_ROWS = 512
SC_CORES = 2
SC_SUBCORES = 16
SC_GATHER_CHUNK = 32


def _cparams(sem):
    return pltpu.CompilerParams(dimension_semantics=sem, vmem_limit_bytes=VMEM_LIMIT)


def _const_spec(shape):
    nd = len(shape)
    return pl.BlockSpec(shape, lambda *_: (0,) * nd)


def _adaln_kernel(cond_ref, w_ref, b_ref, o_ref):
    cnd = cond_ref[...]
    act = cnd * jax.nn.sigmoid(cnd)
    o_ref[...] = jnp.dot(act.astype(BF16), w_ref[...].astype(BF16),
                         preferred_element_type=F32) + b_ref[...]


def _adaln(cond, w_ada, b_ada):
    rows, d = cond.shape
    n = w_ada.shape[1]
    tn = 1536
    return pl.pallas_call(
        _adaln_kernel,
        grid=(n // tn,),
        in_specs=[_const_spec((rows, d)),
                  pl.BlockSpec((d, tn), lambda j: (0, j)),
                  pl.BlockSpec((1, tn), lambda j: (0, j))],
        out_specs=pl.BlockSpec((rows, tn), lambda j: (0, j)),
        out_shape=jax.ShapeDtypeStruct((rows, n), F32),
        compiler_params=_cparams(("parallel",)),
        name="adaln",
    )(cond, w_ada, b_ada.reshape(1, n))


def _pack_w_in(w_in, b_gates, rope, with_nat):
    o_q, o_k, o_v = 0, Q_W, Q_W + KV_W
    o_mq = Q_W + 2 * KV_W
    o_mk = o_mq + MLQK_W
    o_mv = o_mk + MLQK_W
    o_g = o_mv + MLV_W
    o_mo = o_g + GATE_W
    o_ga = o_mo + MLV_W
    o_gm = o_ga + D_MODEL
    half = HEAD_DIM // 2

    def swap_halves(block, n_heads):
        return jnp.flip(block.reshape(D_MODEL, n_heads, 2, half), axis=2).reshape(D_MODEL, n_heads * HEAD_DIM)

    def pair_heads(block):
        return block.reshape(D_MODEL, ATT_KV_HEADS, ATT_GROUP, HEAD_DIM).transpose(0, 2, 1, 3).reshape(D_MODEL, Q_W)

    wq = w_in[:, o_q:o_q + Q_W]
    wk = w_in[:, o_k:o_k + KV_W]

    segs, cols, off = {}, [], 0

    def add(name, cl):
        nonlocal off
        width = sum(c.shape[1] for c in cl)
        segs[name] = (off, off + width)
        cols.extend(cl)
        off += width

    add("q", [pair_heads(wq)])
    if rope:
        add("qrot", [pair_heads(swap_halves(wq, ATT_HEADS))])
    add("kd", [wk])
    if rope:
        add("kdrot", [swap_halves(wk, ATT_KV_HEADS)])
    if with_nat:
        add("nat", [w_in[:, o_k:o_k + 2 * KV_W]])
    add("mq", [w_in[:, o_mq:o_mq + MLQK_W]])
    add("mv", [w_in[:, o_mv:o_mv + MLV_W]])
    add("mo", [w_in[:, o_mo:o_mo + MLV_W]])
    add("ga", [w_in[:, o_ga:o_ga + D_MODEL]])
    add("gm", [w_in[:, o_gm:o_gm + D_MODEL]])
    add("gates", [w_in[:, o_g:o_g + GATE_W], jnp.zeros((D_MODEL, LANES - GATE_W), w_in.dtype)])
    wp = jnp.concatenate(cols, axis=1).astype(BF16)
    wvd_t = jnp.concatenate([w_in[:, o_v:o_v + KV_W], w_in[:, o_mk:o_mk + MLQK_W]], axis=1).T.astype(BF16)
    bg = jnp.pad(b_gates.astype(F32), (0, LANES - GATE_W)).reshape(1, LANES)
    return wp, wvd_t, bg, segs


def _rope_tables(n_tok):
    rows = n_tok // GRID_W
    row = np.repeat(np.arange(rows), GRID_W).astype(np.float32)
    col = np.tile(np.arange(GRID_W), rows).astype(np.float32)
    inv = (np.float32(ROPE_THETA) ** (-np.arange(AXIS_FREQS, dtype=np.float32) / AXIS_FREQS)).astype(np.float32)
    ang = np.concatenate([row[:, None] * inv, col[:, None] * inv], axis=-1).astype(np.float32)
    cos, sin = np.cos(ang), np.sin(ang)
    c64 = np.concatenate([cos, cos], axis=-1)
    s64 = np.concatenate([-sin, sin], axis=-1)
    return (jnp.asarray(np.tile(c64, (1, 2)), F32), jnp.asarray(np.tile(s64, (1, 2)), F32))


def _log_sigmoid(x):
    return jnp.minimum(x, 0.0) - jnp.log1p(jnp.exp(-jnp.abs(x)))


def _prefix_sum_rows(x):
    row = lax.broadcasted_iota(jnp.int32, x.shape, 0)
    k = 1
    while k < x.shape[0]:
        x = x + jnp.where(row >= k, pltpu.roll(x, k, 0), 0.0)
        k *= 2
    return x


def _proj_kernel(*refs, segs, rope, with_nat):
    it = iter(refs)
    x_ref, mod_ref, g_ref, w_ref, wvt_ref, bg_ref = (next(it) for _ in range(6))
    cos_ref = sin_ref = None
    if rope:
        cos_ref, sin_ref = next(it), next(it)
    q_ref, kd_ref, vd_ref = next(it), next(it), next(it)
    nat_ref = next(it) if with_nat else None
    mq_ref, mk_ref, mv_ref, gates_ref, gates_t_ref, mo_ref, ga_ref, gm_ref = (next(it) for _ in range(8))

    x = x_ref[0]
    shift = mod_ref[0, 0:1, :]
    scale = mod_ref[0, 1:2, :]
    ms = jnp.mean(x * x, axis=-1, keepdims=True)
    xn = x * lax.rsqrt(ms + EPS) * g_ref[...]
    xb = (xn * (1.0 + scale) + shift).astype(BF16)

    def seg(name):
        lo, hi = segs[name]
        return jnp.dot(xb, w_ref[:, lo:hi], preferred_element_type=F32)

    uq = seg("q")
    ukd = seg("kd")
    if rope:
        uqr = seg("qrot")
        ukr = seg("kdrot")
        cs = cos_ref[...]
        sn = sin_ref[...]
        for h in range(ATT_HEADS // 2):
            sl = slice(h * LANES, (h + 1) * LANES)
            q_ref[0, h] = ((uq[:, sl] * cs + uqr[:, sl] * sn) * HEAD_DIM ** -0.5).astype(BF16)
        kd_ref[0] = (ukd * cs + ukr * sn).astype(BF16)
    else:
        for h in range(ATT_HEADS // 2):
            sl = slice(h * LANES, (h + 1) * LANES)
            q_ref[0, h] = (uq[:, sl] * HEAD_DIM ** -0.5).astype(BF16)
        kd_ref[0] = ukd.astype(BF16)
    ut = lax.dot_general(wvt_ref[...], xb, (((1,), (1,)), ((), ())), preferred_element_type=F32)
    vd_ref[0] = ut[:KV_W].astype(BF16)
    mk_ref[0] = ut[KV_W:].astype(BF16)
    if with_nat:
        nat_ref[0] = seg("nat")
    mq_ref[0] = (seg("mq") * ML_DK ** -0.5).astype(BF16)
    mv_ref[0] = seg("mv").astype(BF16)
    gates = seg("gates") + bg_ref[...]
    lf = _log_sigmoid(gates)
    csum = _prefix_sum_rows(lf)
    lane = lax.broadcasted_iota(jnp.int32, gates.shape, 1)
    cum = jnp.where(lane >= 2 * ML_HEADS, csum[-1:, :] - csum + lf, csum)
    gates_ref[0] = cum
    gates_t_ref[0, :GATE_W, :] = gates.T[:GATE_W, :]
    gates_t_ref[0, GATE_W:, :] = cum.T[:GATE_W, :]
    mo_ref[0] = _sigmoid(seg("mo")).astype(BF16)
    ga_ref[0] = _sigmoid(seg("ga")).astype(BF16)
    gm_ref[0] = _sigmoid(seg("gm")).astype(BF16)


def _project(x, mod, g_pre, w_in, b_gates, mod_row, rope, with_nat, tm, b0, bsz):
    _, seq, d = x.shape
    wp, wvd_t, bg, segs = _pack_w_in(w_in, b_gates, True, True)
    nw = wp.shape[1]
    tok3 = lambda b, i: (b, i, 0)
    in_specs = [pl.BlockSpec((1, tm, d), lambda b, i: (b + b0, i, 0)),
                pl.BlockSpec((1, 6, d), lambda b, i: (mod_row(b), 0, 0)),
                _const_spec((1, d)),
                pl.BlockSpec((d, nw), lambda b, i: (0, 0), pipeline_mode=pl.Buffered(1)),
                _const_spec((KV_W + MLQK_W, d)), _const_spec((1, LANES))]
    args = [x, mod, g_pre.reshape(1, d), wp, wvd_t, bg]
    if rope:
        cos_t, sin_t = _rope_tables(seq)
        in_specs += [pl.BlockSpec((tm, LANES), lambda b, i: (i, 0))] * 2
        args += [cos_t, sin_t]

    def tok_out(width, dtype):
        return (jax.ShapeDtypeStruct((bsz, seq, width), dtype), pl.BlockSpec((1, tm, width), tok3))

    outs = [(jax.ShapeDtypeStruct((bsz, ATT_HEADS // 2, seq, LANES), BF16),
             pl.BlockSpec((1, ATT_HEADS // 2, tm, LANES), lambda b, i: (b, 0, i, 0))),
            tok_out(KV_W, BF16),
            (jax.ShapeDtypeStruct((bsz, KV_W, seq), BF16),
             pl.BlockSpec((1, KV_W, tm), lambda b, i: (b, 0, i)))]
    if with_nat:
        outs.append(tok_out(2 * KV_W, F32))
    outs += [tok_out(MLQK_W, BF16),
             (jax.ShapeDtypeStruct((bsz, MLQK_W, seq), BF16),
              pl.BlockSpec((1, MLQK_W, tm), lambda b, i: (b, 0, i))),
             tok_out(MLV_W, BF16), tok_out(LANES, F32),
             (jax.ShapeDtypeStruct((bsz, 2 * GATE_W, seq), F32),
              pl.BlockSpec((1, 2 * GATE_W, tm), lambda b, i: (b, 0, i))),
             tok_out(MLV_W, BF16), tok_out(D_MODEL, BF16), tok_out(D_MODEL, BF16)]
    res = pl.pallas_call(
        functools.partial(_proj_kernel, segs=segs, rope=rope, with_nat=with_nat),
        grid=(bsz, seq // tm),
        in_specs=in_specs,
        out_specs=[o[1] for o in outs],
        out_shape=[o[0] for o in outs],
        compiler_params=_cparams(("parallel", "parallel")),
        name="proj_rope" if rope else "proj_ctx",
    )(*args)
    names = ["q", "kd", "vd"] + (["nat"] if with_nat else []) + ["mq", "mk", "mv", "gates", "gates_t", "mo", "ga", "gm"]
    return dict(zip(names, res))


def _attn_kernel(*refs, window, n_blocks, n_ctx):
    it = iter(refs)
    sink_ref, q_ref = next(it), next(it)
    if window:
        kp_ref, kc_ref, kn_ref, vp_ref, vc_ref, vn_ref = (next(it) for _ in range(6))
    ck_ref, cv_ref, o_ref = next(it), next(it), next(it)
    j = pl.program_id(1)
    n_win = 3 * BLOCK if window else 0
    cols = ATT_GROUP * BLOCK

    col_id = lax.broadcasted_iota(jnp.int32, (1, cols), 1)
    if window:
        kl = lax.broadcasted_iota(jnp.int32, (BLOCK, cols), 0)
        ql = lax.broadcasted_iota(jnp.int32, (BLOCK, cols), 1) & (BLOCK - 1)
        prev_ok = kl >= ql + jnp.where(j == 0, BLOCK, 0)
        next_ok = kl <= ql - jnp.where(j == n_blocks - 1, BLOCK, 0)
    lane_row = lax.broadcasted_iota(jnp.int32, (1, LANES), 1) < HEAD_DIM
    half_lo = jnp.where(lane_row, 1.0, 0.0).astype(BF16)
    half_hi = jnp.where(lane_row, 0.0, 1.0).astype(BF16)

    if window:
        keys = jnp.concatenate([kp_ref[0], kc_ref[0], kn_ref[0], ck_ref[0]], axis=0)
        vals_t = jnp.concatenate([vp_ref[0], vc_ref[0], vn_ref[0], cv_ref[0]], axis=1)
    else:
        keys = ck_ref[0]
        vals_t = cv_ref[0]

    for kvh in range(ATT_KV_HEADS):
        qs = jnp.concatenate([q_ref[0, g] * (half_lo if kvh == 0 else half_hi) for g in range(ATT_GROUP)], axis=0)
        s = lax.dot_general(keys, qs, (((1,), (1,)), ((), ())), preferred_element_type=F32)
        if window:
            s = jnp.concatenate([jnp.where(prev_ok, s[:BLOCK], NEG), s[BLOCK:2 * BLOCK],
                                 jnp.where(next_ok, s[2 * BLOCK:n_win], NEG), s[n_win:]], axis=0)
        snk = jnp.full((1, cols), sink_ref[kvh * ATT_GROUP + ATT_GROUP - 1], F32)
        for g in range(ATT_GROUP - 2, -1, -1):
            snk = jnp.where(col_id < (g + 1) * BLOCK, sink_ref[kvh * ATT_GROUP + g], snk)
        m = jnp.maximum(jnp.max(s, axis=0, keepdims=True), snk)
        p = jnp.exp(s - m)
        den = jnp.sum(p, axis=0, keepdims=True) + jnp.exp(snk - m)
        o_t = jnp.dot(vals_t, p.astype(BF16), preferred_element_type=F32) / den
        o_t = o_t[kvh * HEAD_DIM:(kvh + 1) * HEAD_DIM]
        for pair in range(ATT_GROUP // 2):
            two = jnp.concatenate([o_t[:, (2 * pair) * BLOCK:(2 * pair + 1) * BLOCK],
                                   o_t[:, (2 * pair + 1) * BLOCK:(2 * pair + 2) * BLOCK]], axis=0)
            col = (kvh * (ATT_GROUP // 2) + pair) * LANES
            o_ref[0, :, col:col + LANES] = two.T.astype(BF16)


def _attention(q, kd, vd_t, ckd, cvd_t, sink, window):
    bsz, _, seq, _ = q.shape
    nb = seq // BLOCK
    n_ctx = ckd.shape[1]
    in_specs = [pl.BlockSpec(memory_space=pltpu.SMEM),
                pl.BlockSpec((1, ATT_HEADS // 2, BLOCK, LANES), lambda b, j: (b, 0, j, 0))]
    args = [sink.astype(F32), q]
    if window:
        in_specs += [pl.BlockSpec((1, BLOCK, KV_W), lambda b, j: (b, jnp.maximum(j - 1, 0), 0)),
                     pl.BlockSpec((1, BLOCK, KV_W), lambda b, j: (b, j, 0)),
                     pl.BlockSpec((1, BLOCK, KV_W), lambda b, j: (b, jnp.minimum(j + 1, nb - 1), 0)),
                     pl.BlockSpec((1, KV_W, BLOCK), lambda b, j: (b, 0, jnp.maximum(j - 1, 0))),
                     pl.BlockSpec((1, KV_W, BLOCK), lambda b, j: (b, 0, j)),
                     pl.BlockSpec((1, KV_W, BLOCK), lambda b, j: (b, 0, jnp.minimum(j + 1, nb - 1)))]
        args += [kd, kd, kd, vd_t, vd_t, vd_t]
    in_specs += [pl.BlockSpec((1, n_ctx, KV_W), lambda b, j: (b, 0, 0)),
                 pl.BlockSpec((1, KV_W, n_ctx), lambda b, j: (b, 0, 0))]
    args += [ckd, cvd_t]
    return pl.pallas_call(
        functools.partial(_attn_kernel, window=window, n_blocks=nb, n_ctx=n_ctx),
        grid=(bsz, nb),
        in_specs=in_specs,
        out_specs=pl.BlockSpec((1, BLOCK, Q_W), lambda b, j: (b, j, 0)),
        out_shape=jax.ShapeDtypeStruct((bsz, seq, Q_W), BF16),
        compiler_params=_cparams(("parallel", "parallel")),
        name="attn_window" if window else "attn_ctx",
    )(*args)


def _flat_heads(t, transpose):
    b, l = t.shape[:2]
    out = t.reshape(b, l, KV_W).astype(BF16)
    return jnp.swapaxes(out, 1, 2) if transpose else out


def _mlstm_kernel(*refs, chunk_len, n_chunks, has_init, emit_state):
    it = iter(refs)
    qkvg = [[next(it) for _ in range(5)] for _ in range(2)]
    if has_init:
        c0_ref, n0_ref, m0_ref = next(it), next(it), next(it)
    h_refs = [next(it), next(it)]
    if emit_state:
        co_ref, no_ref, mo_ref = next(it), next(it), next(it)
    st_s, m_s = next(it), next(it)
    t = pl.program_id(1)
    L = chunk_len

    @pl.when(t == 0)
    def _():
        for d in range(2):
            for h in range(ML_HEADS):
                if has_init:
                    st_s[d, h, :, :ML_DV] = c0_ref[0, d, h].T
                    st_s[d, h, :, ML_DV:] = jnp.broadcast_to(n0_ref[0, d, h:h + 1, :], (LANES, ML_DK)).T
                    m_s[d, h] = m0_ref[0, d, h:h + 1, :]
                else:
                    st_s[d, h] = jnp.zeros((ML_DK, ML_DV + LANES), F32)
                    m_s[d, h] = jnp.zeros((1, LANES), F32)

    ri = lax.broadcasted_iota(jnp.int32, (L, L), 0)
    ci = lax.broadcasted_iota(jnp.int32, (L, L), 1)
    ones_blk = jnp.ones((L, LANES), BF16)

    for d in range(2):
        q_ref, kt_ref, v_ref, cum_ref, rows_ref = qkvg[d]
        tri = (ci <= ri) if d == 0 else (ci >= ri)
        bc = cum_ref[0]
        g_t = rows_ref[0, :GATE_W, :]
        br = rows_ref[0, GATE_W:, :]
        last = L - 1 if d == 0 else 0

        for h in range(ML_HEADS):
            icol_i = 2 * ML_HEADS * d + h
            fcol = icol_i + ML_HEADS
            bcol = bc[:, fcol:fcol + 1]
            brow = br[fcol:fcol + 1, :]
            irow = g_t[icol_i:icol_i + 1, :]
            b_last = bc[last:last + 1, fcol:fcol + 1]
            m_prev = m_s[d, h][:, 0:1]
            st_prev = st_s[d, h]
            qh = q_ref[0, :, h * ML_DK:(h + 1) * ML_DK]
            kt = kt_ref[0, h * ML_DK:(h + 1) * ML_DK, :]
            v_aug = jnp.concatenate([v_ref[0, :, h * ML_DV:(h + 1) * ML_DV], ones_blk], axis=1)

            rel = jnp.where(tri, irow - brow, NEG)
            inter = bcol + m_prev
            m_t = jnp.maximum(inter, bcol + jnp.max(rel, axis=-1, keepdims=True))
            w = jnp.exp(rel + (bcol - m_t))
            w_inter = jnp.exp(inter - m_t)
            s = jnp.dot(qh, kt, preferred_element_type=F32) * w
            lhs = jnp.concatenate([s.astype(BF16), (qh.astype(F32) * w_inter).astype(BF16)], axis=1)
            rhs = jnp.concatenate([v_aug, st_prev.astype(BF16)], axis=0)
            mix = jnp.dot(lhs, rhs, preferred_element_type=F32)
            inv = 1.0 / jnp.maximum(jnp.abs(mix[:, ML_DV:]), jnp.exp(-m_t))
            for c in range(ML_DV // LANES):
                lo = h * ML_DV + c * LANES
                h_refs[d][0, :, lo:lo + LANES] = mix[:, c * LANES:(c + 1) * LANES] * inv

            a = b_last - brow + irow
            m_new = jnp.maximum(b_last + m_prev, jnp.max(a, axis=-1, keepdims=True))
            wk = jnp.exp(a - m_new)
            decay = jnp.exp(b_last + m_prev - m_new)
            kw_t = (kt.astype(F32) * wk).astype(BF16)
            st_new = decay * st_prev + jnp.dot(kw_t, v_aug, preferred_element_type=F32)
            st_s[d, h] = st_new
            m_s[d, h] = jnp.broadcast_to(m_new, (1, LANES))
            if emit_state:
                @pl.when(t == n_chunks - 1)
                def _():
                    co_ref[0, d, h] = st_new[:, :ML_DV].T
                    no_ref[0, d, h:h + 1, :] = st_new[:, ML_DV:].T[0:1, :]
                    mo_ref[0, d, h:h + 1, :] = jnp.broadcast_to(m_new, (1, LANES))


def _mlstm(mq, mk_t, mv, gate_cum, gate_rows, init, emit_state, b0=0):
    bsz, seq, _ = mq.shape
    chunk_len = min(ML_CHUNK, seq)
    nc = seq // chunk_len
    fwd = lambda b, t: (b, t, 0)
    bwd = lambda b, t: (b, nc - 1 - t, 0)
    fwd_t = lambda b, t: (b, 0, t)
    bwd_t = lambda b, t: (b, 0, nc - 1 - t)
    state5 = lambda b, t: (b, 0, 0, 0, 0)
    state4 = lambda b, t: (b, 0, 0, 0)
    init5 = lambda b, t: (b + b0, 0, 0, 0, 0)
    init4 = lambda b, t: (b + b0, 0, 0, 0)
    in_specs, args = [], []
    for chunk, chunk_t in ((fwd, fwd_t), (bwd, bwd_t)):
        in_specs += [pl.BlockSpec((1, chunk_len, MLQK_W), chunk), pl.BlockSpec((1, MLQK_W, chunk_len), chunk_t),
                     pl.BlockSpec((1, chunk_len, MLV_W), chunk), pl.BlockSpec((1, chunk_len, LANES), chunk),
                     pl.BlockSpec((1, 2 * GATE_W, chunk_len), chunk_t)]
        args += [mq, mk_t, mv, gate_cum, gate_rows]
    if init is not None:
        c0, n0, m0 = init
        in_specs += [pl.BlockSpec((1, 2, ML_HEADS, ML_DV, ML_DK), init5),
                     pl.BlockSpec((1, 2, ML_HEADS, ML_DK), init4),
                     pl.BlockSpec((1, 2, ML_HEADS, LANES), init4)]
        args += [c0.astype(F32), n0.astype(F32),
                 jnp.broadcast_to(m0.astype(F32)[..., None], m0.shape + (LANES,))]
    out_shape = [jax.ShapeDtypeStruct((bsz, seq, MLV_W), F32)] * 2
    out_specs = [pl.BlockSpec((1, chunk_len, MLV_W), fwd), pl.BlockSpec((1, chunk_len, MLV_W), bwd)]
    if emit_state:
        out_shape += [jax.ShapeDtypeStruct((bsz, 2, ML_HEADS, ML_DV, ML_DK), F32),
                      jax.ShapeDtypeStruct((bsz, 2, ML_HEADS, ML_DK), F32),
                      jax.ShapeDtypeStruct((bsz, 2, ML_HEADS, LANES), F32)]
        out_specs += [pl.BlockSpec((1, 2, ML_HEADS, ML_DV, ML_DK), state5),
                      pl.BlockSpec((1, 2, ML_HEADS, ML_DK), state4),
                      pl.BlockSpec((1, 2, ML_HEADS, LANES), state4)]
    return pl.pallas_call(
        functools.partial(_mlstm_kernel, chunk_len=chunk_len, n_chunks=nc, has_init=init is not None, emit_state=emit_state),
        grid=(bsz, nc),
        in_specs=in_specs,
        out_specs=out_specs,
        out_shape=out_shape,
        scratch_shapes=[pltpu.VMEM((2, ML_HEADS, ML_DK, ML_DV + LANES), F32),
                        pltpu.VMEM((2, ML_HEADS, 1, LANES), F32)],
        compiler_params=_cparams(("parallel", "arbitrary")),
        name="mlstm_state" if emit_state else "mlstm",
    )(*args)


def _sigmoid(x):
    return 0.5 * jnp.tanh(0.5 * x) + 0.5


def _rms(x, g):
    return x * lax.rsqrt(jnp.mean(x * x, axis=-1, keepdims=True) + EPS) * g


def _pack_bf16_pairs(x):
    n = x.shape[1] // 2
    lo = pltpu.bitcast(x[:, :n].astype(BF16).astype(F32), jnp.int32)
    hi = pltpu.bitcast(x[:, n:].astype(BF16).astype(F32), jnp.int32)
    return lax.shift_right_logical(lo, 16) | hi


def _unpack_bf16_pairs(p):
    lo = pltpu.bitcast(lax.shift_left(p, 16), F32)
    hi = pltpu.bitcast(p & jnp.int32(-65536), F32)
    return lo, hi


def _merge_kernel(att_ref, hf_ref, hb_ref, mo_ref, ga_ref, gm_ref, x_ref, mod_ref, nml_ref, gpm_ref, gpf_ref,
                  wua_ref, wum_ref, wo_ref, wr_ref, br_ref, x1_ref, h2_ref, idx_ref, gate_ref, cnt_ref):
    hsum = hf_ref[0] + hb_ref[0]
    parts = []
    for h in range(ML_HEADS):
        hh = hsum[:, h * ML_DV:(h + 1) * ML_DV]
        parts.append(hh * lax.rsqrt(jnp.mean(hh * hh, axis=-1, keepdims=True) + EPS))
    hn = jnp.concatenate(parts, axis=-1) * nml_ref[...]
    ml = (hn * mo_ref[0].astype(F32)).astype(BF16)
    a = jnp.dot(att_ref[0], wua_ref[...], preferred_element_type=F32)
    m = jnp.dot(ml, wum_ref[...], preferred_element_type=F32)
    z = (ga_ref[0].astype(F32) * a + gm_ref[0].astype(F32) * m).astype(BF16)
    mix = jnp.dot(z, wo_ref[...], preferred_element_type=F32)
    gate1 = mod_ref[0, 2:3, :]
    shift2 = mod_ref[0, 3:4, :]
    scale2 = mod_ref[0, 4:5, :]
    x1 = x_ref[0] + gate1 * _rms(mix, gpm_ref[...])
    x1_ref[0] = x1
    h2 = _rms(x1, gpf_ref[...]) * (1.0 + scale2) + shift2
    h2_ref[0] = _pack_bf16_pairs(h2)

    h2_hi = h2.astype(BF16)
    h2_lo = (h2 - h2_hi.astype(F32)).astype(BF16)
    wr = wr_ref[...]
    wr_hi = wr.astype(BF16)
    wr_lo = (wr - wr_hi.astype(F32)).astype(BF16)
    hi_terms = jnp.dot(h2_hi, jnp.concatenate([wr_hi, wr_lo], axis=1), preferred_element_type=F32)
    logits = (hi_terms[:, :LANES] + hi_terms[:, LANES:]
              + jnp.dot(h2_lo, wr_hi, preferred_element_type=F32)) + br_ref[...]
    lane = lax.broadcasted_iota(jnp.int32, logits.shape, 1).astype(F32)
    work = jnp.where(lane < N_EXPERTS, logits, -jnp.inf)
    idx_out = jnp.zeros(logits.shape, F32)
    val_out = jnp.zeros(logits.shape, F32)
    picked = jnp.zeros(logits.shape, F32)
    top0 = None
    esum = None
    for k in range(TOP_K):
        mx = jnp.max(work, axis=-1, keepdims=True)
        sel = jnp.min(jnp.where(work == mx, lane, float(LANES)), axis=-1, keepdims=True)
        if k == 0:
            top0 = mx
        e = jnp.exp(mx - top0)
        esum = e if k == 0 else esum + e
        idx_out = jnp.where(lane == k, sel, idx_out)
        val_out = jnp.where(lane == k, e, val_out)
        picked = jnp.where(lane == sel, 1.0, picked)
        work = jnp.where(lane == sel, -jnp.inf, work)
    idx_ref[0] = idx_out.astype(jnp.int32)
    gate_ref[0] = val_out / esum
    cnt_ref[0] = jnp.sum(picked, axis=0, keepdims=True)


def _merge(att, hf, hb, pj, x, mod, mod_row, p, tm, b0):
    bsz = att.shape[0]
    _, seq, d = x.shape
    tok3 = lambda b, i: (b, i, 0)
    wr = jnp.pad(p["w_router"].astype(F32), ((0, 0), (0, LANES - N_EXPERTS)))
    br = jnp.pad(p["b_router"].astype(F32), (0, LANES - N_EXPERTS)).reshape(1, LANES)
    row = lambda v: v.astype(F32).reshape(1, -1)
    in_specs = [pl.BlockSpec((1, tm, Q_W), tok3),
                pl.BlockSpec((1, tm, MLV_W), tok3), pl.BlockSpec((1, tm, MLV_W), tok3),
                pl.BlockSpec((1, tm, MLV_W), tok3), pl.BlockSpec((1, tm, d), tok3), pl.BlockSpec((1, tm, d), tok3),
                pl.BlockSpec((1, tm, d), lambda b, i: (b + b0, i, 0)),
                pl.BlockSpec((1, 6, d), lambda b, i: (mod_row(b), 0, 0)),
                _const_spec((1, MLV_W)), _const_spec((1, d)), _const_spec((1, d)),
                _const_spec((Q_W, d)), _const_spec((MLV_W, d)), _const_spec((d, d)),
                _const_spec((d, LANES)), _const_spec((1, LANES))]
    out_shape = [jax.ShapeDtypeStruct((bsz, seq, d), F32), jax.ShapeDtypeStruct((bsz, seq, d // 2), jnp.int32),
                 jax.ShapeDtypeStruct((bsz, seq, LANES), jnp.int32), jax.ShapeDtypeStruct((bsz, seq, LANES), F32),
                 jax.ShapeDtypeStruct((bsz * (seq // tm), 1, LANES), F32)]
    out_specs = [pl.BlockSpec((1, tm, d), tok3), pl.BlockSpec((1, tm, d // 2), tok3),
                 pl.BlockSpec((1, tm, LANES), tok3), pl.BlockSpec((1, tm, LANES), tok3),
                 pl.BlockSpec((1, 1, LANES), lambda b, i: (b * (seq // tm) + i, 0, 0))]
    return pl.pallas_call(
        _merge_kernel,
        grid=(bsz, seq // tm),
        in_specs=in_specs,
        out_specs=out_specs,
        out_shape=out_shape,
        compiler_params=_cparams(("parallel", "parallel")),
        name="merge_router",
    )(att, hf, hb, pj["mo"], pj["ga"], pj["gm"], x, mod, row(p["norm_ml"]), row(p["g_post_mix"]),
      row(p["g_pre_ffn"]), p["w_up_att"].astype(BF16), p["w_up_ml"].astype(BF16), p["w_out"].astype(BF16), wr, br)


def _sc_gather_rows(table, idx):
    n = idx.shape[0]
    width = table.shape[1]
    chunk = SC_GATHER_CHUNK
    n_workers = SC_CORES * SC_SUBCORES
    rows_per_worker = n // n_workers
    n_pairs = rows_per_worker // (2 * chunk)
    assert n_pairs * 2 * chunk * n_workers == n
    mesh = plsc.VectorSubcoreMesh(core_axis_name="c", subcore_axis_name="s")

    def body(table_hbm, idx_hbm, out_hbm, idx_v, rows_v, gsem, osem):
        wid = lax.axis_index("s") * SC_CORES + lax.axis_index("c")
        base0 = wid * rows_per_worker

        def gather_copy(b):
            return pltpu.make_async_copy(table_hbm.at[idx_v.at[b]], rows_v.at[b], gsem.at[b])

        def write_copy(ci, b):
            base = pl.multiple_of(base0 + ci * chunk, 8)
            return pltpu.make_async_copy(rows_v.at[b], out_hbm.at[pl.ds(base, chunk)], osem.at[b])

        def issue(ci, b):
            base = pl.multiple_of(base0 + ci * chunk, 8)
            pltpu.sync_copy(idx_hbm.at[pl.ds(base, chunk)], idx_v.at[b])
            gather_copy(b).start()

        def finish(ci, b):
            gather_copy(b).wait()
            write_copy(ci, b).start()

        issue(0, 0)

        @pl.loop(0, n_pairs)
        def _(j):
            @pl.when(j > 0)
            def _():
                write_copy(2 * j - 1, 1).wait()
            issue(2 * j + 1, 1)
            finish(2 * j, 0)

            @pl.when(j < n_pairs - 1)
            def _():
                write_copy(2 * j, 0).wait()
                issue(2 * j + 2, 0)
            finish(2 * j + 1, 1)

        write_copy(2 * n_pairs - 2, 0).wait()
        write_copy(2 * n_pairs - 1, 1).wait()

    return pl.kernel(
        body, mesh=mesh,
        out_type=jax.ShapeDtypeStruct((n, width), table.dtype),
        scratch_types=[pltpu.VMEM((2, chunk), jnp.int32), pltpu.VMEM((2, chunk, width), table.dtype),
                       pltpu.SemaphoreType.DMA((2,)), pltpu.SemaphoreType.DMA((2,))],
    )(table, idx)


def _ffn_kernel(be_ref, nu_ref, xs_ref, w1_ref, b1_ref, w2_ref, b2_ref, ys_ref, *bf16_refs, weights_are_bf16):
    i = pl.program_id(0)
    e = be_ref[i]
    e_prev = be_ref[jnp.maximum(i - 1, 0)]
    if weights_are_bf16:
        w1b_ref, w2b_ref = w1_ref, w2_ref
    else:
        w1b_ref, w2b_ref = bf16_refs

        @pl.when((i == 0) | (e != e_prev))
        def _():
            w1b_ref[0] = w1_ref[0].astype(BF16)
            w2b_ref[0] = w2_ref[0].astype(BF16)

    @pl.when(i < nu_ref[0])
    def _():
        x_lo, x_hi = _unpack_bf16_pairs(xs_ref[...])
        xb = jnp.concatenate([x_lo.astype(BF16), x_hi.astype(BF16)], axis=1)
        hmid = jnp.dot(xb, w1b_ref[0], preferred_element_type=F32) + b1_ref[0]
        glu = jnp.minimum(hmid[:, :D_FF], SWIGLU_LIMIT)
        lin = jnp.clip(hmid[:, D_FF:], -SWIGLU_LIMIT, SWIGLU_LIMIT)
        act = ((lin + 1.0) * glu * _sigmoid(SWIGLU_ALPHA * glu)).astype(BF16)
        y = jnp.dot(act, w2b_ref[0], preferred_element_type=F32) + b2_ref[0]
        ys_ref[...] = _pack_bf16_pairs(y)

    @pl.when(i >= nu_ref[0])
    def _():
        ys_ref[...] = jnp.zeros(ys_ref.shape, jnp.int32)


def _sc_scatter_rows(tables, positions, n_rows):
    width = tables[0].shape[1]
    chunk = SC_GATHER_CHUNK
    n_workers = SC_CORES * SC_SUBCORES
    n_streams = len(tables)
    mesh = plsc.VectorSubcoreMesh(core_axis_name="c", subcore_axis_name="s")

    def body(*refs):
        table_refs = refs[0:2 * n_streams:2]
        pos_refs = refs[1:2 * n_streams:2]
        out_hbm, idx_v, rows_v, lsem, ssem = refs[2 * n_streams:]
        wid = lax.axis_index("s") * SC_CORES + lax.axis_index("c")

        for table_hbm, pos_hbm in zip(table_refs, pos_refs):
            tok_per_worker = table_hbm.shape[0] // n_workers
            n_pairs = tok_per_worker // (2 * chunk)
            assert n_pairs * 2 * chunk * n_workers == table_hbm.shape[0]
            base0 = wid * tok_per_worker

            def load_copy(ci, b):
                base = pl.multiple_of(base0 + ci * chunk, 8)
                return pltpu.make_async_copy(table_hbm.at[pl.ds(base, chunk)], rows_v.at[b], lsem.at[b])

            def scatter_copy(b, k):
                return pltpu.make_async_copy(rows_v.at[b], out_hbm.at[idx_v.at[b, k]], ssem.at[b])

            def load(ci, b):
                base = pl.multiple_of(base0 + ci * chunk, 8)
                load_copy(ci, b).start()
                for k in range(TOP_K):
                    pltpu.sync_copy(pos_hbm.at[k, pl.ds(base, chunk)], idx_v.at[b, k])

            def scatter(ci, b):
                load_copy(ci, b).wait()
                for k in range(TOP_K):
                    scatter_copy(b, k).start()

            def drain(b):
                for k in range(TOP_K):
                    scatter_copy(b, k).wait()

            load(0, 0)

            @pl.loop(0, n_pairs)
            def _(j):
                @pl.when(j > 0)
                def _():
                    drain(1)
                load(2 * j + 1, 1)
                scatter(2 * j, 0)

                @pl.when(j < n_pairs - 1)
                def _():
                    drain(0)
                    load(2 * j + 2, 0)
                scatter(2 * j + 1, 1)

            drain(0)
            drain(1)

    args = [a for pair in zip(tables, positions) for a in pair]
    return pl.kernel(
        body, mesh=mesh,
        out_type=jax.ShapeDtypeStruct((n_rows, width), tables[0].dtype),
        scratch_types=[pltpu.VMEM((2, TOP_K, chunk), jnp.int32), pltpu.VMEM((2, chunk, width), tables[0].dtype),
                       pltpu.SemaphoreType.DMA((2,)), pltpu.SemaphoreType.DMA((2,))],
    )(*args)


def _pos_kernel(idx_ref, base_ref, pos_ref):
    idx = idx_ref[...]
    rows = idx.shape[0]
    lane = lax.broadcasted_iota(jnp.int32, idx.shape, 1)
    hots = [jnp.where(lane == idx[:, k:k + 1], 1.0, 0.0) for k in range(TOP_K)]
    cnt = hots[0] + hots[1] + hots[2] + hots[3]
    ri = lax.broadcasted_iota(jnp.int32, (rows, rows), 0)
    ci = lax.broadcasted_iota(jnp.int32, (rows, rows), 1)
    earlier = jnp.where(ci < ri, 1.0, 0.0).astype(BF16)
    ahead = jnp.dot(earlier, cnt.astype(BF16), preferred_element_type=F32) + base_ref[0]
    posmat = jnp.zeros(idx.shape, F32)
    for k in range(TOP_K):
        posmat = jnp.where(lane == k, jnp.sum(hots[k] * ahead, axis=-1, keepdims=True), posmat)
    pos_ref[...] = posmat.T[:TOP_K, :].astype(jnp.int32)


def _route(idx_list, count_list, every_expert):
    tiles = [c.shape[0] for c in count_list]
    tile_tok = [idx.size // LANES // n for idx, n in zip(idx_list, tiles)]
    n_tok = sum(n * sz for n, sz in zip(tiles, tile_tok))
    n_blocks = n_tok * TOP_K // MOE_ROWS + N_EXPERTS
    cnt = jnp.concatenate([c.reshape(-1, LANES) for c in count_list], axis=0).astype(jnp.int32)
    tile_off = jnp.cumsum(cnt, axis=0) - cnt
    total = jnp.sum(cnt, axis=0)
    padded = (total + MOE_ROWS - 1) // MOE_ROWS * MOE_ROWS
    if every_expert:
        padded = jnp.where(jnp.arange(LANES) < N_EXPERTS, jnp.maximum(padded, MOE_ROWS), 0)
    pend = jnp.cumsum(padded)
    base = ((pend - padded)[None, :] + tile_off).astype(F32)
    starts = jnp.arange(n_blocks, dtype=jnp.int32) * MOE_ROWS
    block_e = jnp.minimum(jnp.sum(pend[None, :N_EXPERTS] <= starts[:, None], axis=1), N_EXPERTS - 1).astype(jnp.int32)
    n_used = (pend[N_EXPERTS - 1] // MOE_ROWS).astype(jnp.int32).reshape(1)
    positions, first = [], 0
    for idx, n_tiles, tile in zip(idx_list, tiles, tile_tok):
        stream_tok = n_tiles * tile
        positions.append(pl.pallas_call(
            _pos_kernel,
            grid=(n_tiles,),
            in_specs=[pl.BlockSpec((tile, LANES), lambda i: (i, 0)),
                      pl.BlockSpec((1, 1, LANES), lambda i: (i, 0, 0))],
            out_specs=pl.BlockSpec((TOP_K, tile), lambda i: (0, i)),
            out_shape=jax.ShapeDtypeStruct((TOP_K, stream_tok), jnp.int32),
            compiler_params=_cparams(("parallel",)),
            name="route_pos",
        )(idx.reshape(stream_tok, LANES), base[first:first + n_tiles].reshape(n_tiles, 1, LANES)))
        first += n_tiles
    return block_e, n_used, positions


def _moe(h2p_list, idx_list, count_list, w1, b1, w2, b2):
    half = h2p_list[0].shape[1]
    d = 2 * half
    cast_here = w1.dtype != BF16
    block_e, n_used, positions = _route(idx_list, count_list, every_expert=cast_here)
    n_blocks = block_e.shape[0]
    xs = _sc_scatter_rows(h2p_list, positions, n_blocks * MOE_ROWS)
    grid_spec = pltpu.PrefetchScalarGridSpec(
        num_scalar_prefetch=2,
        grid=(n_blocks,),
        in_specs=[pl.BlockSpec((MOE_ROWS, half), lambda i, be, nu: (i, 0)),
                  pl.BlockSpec((1, d, 2 * D_FF), lambda i, be, nu: (be[i], 0, 0)),
                  pl.BlockSpec((1, 1, 2 * D_FF), lambda i, be, nu: (be[i], 0, 0)),
                  pl.BlockSpec((1, D_FF, d), lambda i, be, nu: (be[i], 0, 0)),
                  pl.BlockSpec((1, 1, d), lambda i, be, nu: (be[i], 0, 0))],
        out_specs=[pl.BlockSpec((MOE_ROWS, half), lambda i, be, nu: (i, 0))] + (
            [pl.BlockSpec((1, d, 2 * D_FF), lambda i, be, nu: (be[i], 0, 0)),
             pl.BlockSpec((1, D_FF, d), lambda i, be, nu: (be[i], 0, 0))] if cast_here else []),
    )
    out_shape = [jax.ShapeDtypeStruct((n_blocks * MOE_ROWS, half), jnp.int32)]
    if cast_here:
        out_shape += [jax.ShapeDtypeStruct(w1.shape, BF16), jax.ShapeDtypeStruct(w2.shape, BF16)]
    res = pl.pallas_call(
        functools.partial(_ffn_kernel, weights_are_bf16=not cast_here),
        grid_spec=grid_spec,
        out_shape=out_shape,
        compiler_params=_cparams(("arbitrary",)),
        name="moe_ffn_cast" if cast_here else "moe_ffn",
    )(block_e, n_used, xs, w1, b1.reshape(N_EXPERTS, 1, -1), w2, b2.reshape(N_EXPERTS, 1, -1))
    w1b, w2b = (res[1], res[2]) if cast_here else (w1, w2)
    return [_sc_gather_rows(res[0], pos.reshape(-1)) for pos in positions], w1b, w2b


def _final_kernel(x1_ref, y0_ref, y1_ref, y2_ref, y3_ref, gate_ref, mod_ref, g_ref, *rest):
    o_ref = rest[-1]
    gates = gate_ref[0]
    lo = hi = None
    for k, y_ref in enumerate((y0_ref, y1_ref, y2_ref, y3_ref)):
        y_lo, y_hi = _unpack_bf16_pairs(y_ref[0, 0])
        gk = gates[:, k:k + 1]
        lo = gk * y_lo if k == 0 else lo + gk * y_lo
        hi = gk * y_hi if k == 0 else hi + gk * y_hi
    y = jnp.concatenate([lo, hi], axis=-1)
    gate2 = mod_ref[0, 5:6, :]
    o_ref[0] = x1_ref[0] + gate2 * _rms(y, g_ref[...])


def _final(x1, yg, gate, mod, mod_row, g_post_ffn, tm, b0, n_batch, out_prev):
    bsz, seq, d = x1.shape
    tok3 = lambda b, i: (b, i, 0)
    yg = yg.reshape(TOP_K, bsz, seq, d // 2)
    slot_specs = [pl.BlockSpec((1, 1, tm, d // 2), functools.partial(lambda b, i, k: (k, b, i, 0), k=k))
                  for k in range(TOP_K)]
    in_specs = ([pl.BlockSpec((1, tm, d), tok3)] + slot_specs +
                [pl.BlockSpec((1, tm, LANES), tok3),
                 pl.BlockSpec((1, 6, d), lambda b, i: (mod_row(b), 0, 0)), _const_spec((1, d))])
    args = [x1, yg, yg, yg, yg, gate, mod, g_post_ffn.astype(F32).reshape(1, d)]
    aliases = {}
    if out_prev is not None:
        in_specs.append(pl.BlockSpec(memory_space=pl.ANY))
        args.append(out_prev)
        aliases = {len(args) - 1: 0}
    return pl.pallas_call(
        _final_kernel,
        grid=(bsz, seq // tm),
        in_specs=in_specs,
        out_specs=pl.BlockSpec((1, tm, d), lambda b, i: (b + b0, i, 0)),
        out_shape=jax.ShapeDtypeStruct((n_batch, seq, d), F32),
        input_output_aliases=aliases,
        compiler_params=_cparams(("parallel", "parallel")),
        name="final_residual",
    )(*args)


def _stream(x, mod, mod_row, p, ctx_kv, init_state, rope, tm, b0, bsz):
    _, seq, d = x.shape
    is_ctx = ctx_kv is None
    pj = _project(x, mod, p["g_pre_mix"], p["w_in"], p["b_gates"], mod_row, rope, is_ctx, tm, b0, bsz)
    if is_ctx:
        att = _attention(pj["q"], None, None, pj["kd"], pj["vd"], p["sink"], window=False)
    else:
        att = _attention(pj["q"], pj["kd"], pj["vd"], ctx_kv[0], ctx_kv[1], p["sink"], window=True)
    assert min(ML_CHUNK, seq) == tm
    ml = _mlstm(pj["mq"], pj["mk"], pj["mv"], pj["gates"], pj["gates_t"], init_state, emit_state=is_ctx, b0=b0)
    x1, h2, idx, gate, tile_counts = _merge(att, ml[0], ml[1], pj, x, mod, mod_row, p, tm, b0)
    moe_in = (h2.reshape(bsz * seq, d // 2), idx, tile_counts)
    return x1, gate, moe_in, pj, ml


def kernel(x_prompt, x_sample, c, cache_k, cache_v, state_C, state_n, state_m, c_ctx, w_ada, b_ada, g_pre_mix,
           w_in, b_gates, attn_sink, norm_mlstm, w_up_att, w_up_ml, w_out, g_post_mix, g_pre_ffn, w_router,
           b_router, w1, b1, w2, b2, g_post_ffn):
    depth = w_ada.shape[0]
    n_dec = c.shape[0]
    cond = jnp.concatenate([c_ctx[None, :], c], axis=0).astype(F32)
    cond = jnp.pad(cond, ((0, 16 - cond.shape[0]), (0, 0)))
    y_prompt, y_sample = x_prompt, x_sample
    ks_, vs_, cs_, ns_, ms_ = [], [], [], [], []
    for l in range(depth):
        p = dict(g_pre_mix=g_pre_mix[l], w_in=w_in[l], b_gates=b_gates[l], sink=attn_sink[l], norm_ml=norm_mlstm[l],
                 w_up_att=w_up_att[l], w_up_ml=w_up_ml[l], w_out=w_out[l], g_post_mix=g_post_mix[l],
                 g_pre_ffn=g_pre_ffn[l], w_router=w_router[l], b_router=b_router[l], w1=w1[l], b1=b1[l],
                 w2=w2[l], b2=b2[l], g_post_ffn=g_post_ffn[l])
        mod = _adaln(cond, w_ada[l], b_ada[l]).reshape(16, 6, D_MODEL)
        row_p = lambda b: 0
        bsz, seq = x_prompt.shape[:2]
        x1_p, gate_p, moe_p, pj, ml = _stream(y_prompt, mod, row_p, p, None, None, False, 256, 0, bsz)
        nat = pj["nat"]
        ks_.append(nat[..., :KV_W].reshape(bsz, seq, ATT_KV_HEADS, HEAD_DIM))
        vs_.append(nat[..., KV_W:].reshape(bsz, seq, ATT_KV_HEADS, HEAD_DIM))
        cs_.append(ml[2])
        ns_.append(ml[3])
        ms_.append(ml[4][..., 0])
        ctx_kv = (_flat_heads(cache_k[:, l], False), _flat_heads(cache_v[:, l], True))
        init = (state_C[:, l], state_n[:, l], state_m[:, l])
        half = n_dec // 2
        experts = (p["w1"], p["b1"].astype(F32), p["w2"], p["b2"].astype(F32))
        groups = []
        for b0 in (0, half):
            row_s = functools.partial(lambda b, off: b + off + 1, off=b0)
            sl = slice(b0, b0 + half)
            x1_s, gate_s, moe_s, _, _ = _stream(y_sample, mod, row_s, p, (ctx_kv[0][sl], ctx_kv[1][sl]),
                                                init, True, 512, b0, half)
            groups.append((x1_s, gate_s, moe_s, row_s))
        (x1_a, gate_a, moe_a, row_a), (x1_b, gate_b, moe_b, row_b) = groups
        (yg_p, yg_a), w1b, w2b = _moe([moe_p[0], moe_a[0]], [moe_p[1], moe_a[1]], [moe_p[2], moe_a[2]], *experts)
        (yg_b,), _, _ = _moe([moe_b[0]], [moe_b[1]], [moe_b[2]], w1b, experts[1], w2b, experts[3])
        y_prompt = _final(x1_p, yg_p, gate_p, mod, row_p, p["g_post_ffn"], 256, 0, bsz, None)
        y_sample = _final(x1_a, yg_a, gate_a, mod, row_a, p["g_post_ffn"], 512, 0, n_dec, None)
        y_sample = _final(x1_b, yg_b, gate_b, mod, row_b, p["g_post_ffn"], 512, half, n_dec, y_sample)
    return (y_prompt, y_sample, jnp.stack(ks_, axis=1), jnp.stack(vs_, axis=1), jnp.stack(cs_, axis=1),
            jnp.stack(ns_, axis=1), jnp.stack(ms_, axis=1))
```

```python
import functools

import numpy as np
import jax
import jax.numpy as jnp
from jax import lax
from jax.experimental import pallas as pl
from jax.experimental.pallas import tpu as pltpu
from jax.experimental.pallas import tpu_sc as plsc

F32 = jnp.float32
BF16 = jnp.bfloat16

D_MODEL = 1024
GRID_W = 64
ATT_HEADS = 8
ATT_KV_HEADS = 2
ATT_GROUP = ATT_HEADS // ATT_KV_HEADS
HEAD_DIM = 64
BLOCK = 128
ROPE_THETA = 10000.0
AXIS_FREQS = HEAD_DIM // 4
ML_HEADS = 4
ML_DK = 128
ML_DV = 256
ML_CHUNK = 512
N_EXPERTS = 32
TOP_K = 4
D_FF = 1024
SWIGLU_ALPHA = 1.702
SWIGLU_LIMIT = 7.0
EPS = 1e-6
NEG = -1e30

Q_W = ATT_HEADS * HEAD_DIM
KV_W = ATT_KV_HEADS * HEAD_DIM
MLQK_W = ML_HEADS * ML_DK
MLV_W = ML_HEADS * ML_DV
GATE_W = 4 * ML_HEADS

LANES = 128
VMEM_LIMIT = 56 * 1024 * 1024
MOE_ROWS = 512
SC_CORES = 2
SC_SUBCORES = 16
SC_GATHER_CHUNK = 32


def _cparams(sem):
    return pltpu.CompilerParams(dimension_semantics=sem, vmem_limit_bytes=VMEM_LIMIT)


def _const_spec(shape):
    nd = len(shape)
    return pl.BlockSpec(shape, lambda *_: (0,) * nd)


def _adaln_kernel(cond_ref, w_ref, b_ref, o_ref):
    cnd = cond_ref[...]
    act = cnd * jax.nn.sigmoid(cnd)
    o_ref[...] = jnp.dot(act.astype(BF16), w_ref[...].astype(BF16),
                         preferred_element_type=F32) + b_ref[...]


def _adaln(cond, w_ada, b_ada):
    rows, d = cond.shape
    n = w_ada.shape[1]
    tn = 1536
    return pl.pallas_call(
        _adaln_kernel,
        grid=(n // tn,),
        in_specs=[_const_spec((rows, d)),
                  pl.BlockSpec((d, tn), lambda j: (0, j)),
                  pl.BlockSpec((1, tn), lambda j: (0, j))],
        out_specs=pl.BlockSpec((rows, tn), lambda j: (0, j)),
        out_shape=jax.ShapeDtypeStruct((rows, n), F32),
        compiler_params=_cparams(("parallel",)),
        name="adaln",
    )(cond, w_ada, b_ada.reshape(1, n))


def _pack_w_in(w_in, b_gates, rope, with_nat):
    o_q, o_k, o_v = 0, Q_W, Q_W + KV_W
    o_mq = Q_W + 2 * KV_W
    o_mk = o_mq + MLQK_W
    o_mv = o_mk + MLQK_W
    o_g = o_mv + MLV_W
    o_mo = o_g + GATE_W
    o_ga = o_mo + MLV_W
    o_gm = o_ga + D_MODEL
    half = HEAD_DIM // 2

    def swap_halves(block, n_heads):
        return jnp.flip(block.reshape(D_MODEL, n_heads, 2, half), axis=2).reshape(D_MODEL, n_heads * HEAD_DIM)

    def pair_heads(block):
        return block.reshape(D_MODEL, ATT_KV_HEADS, ATT_GROUP, HEAD_DIM).transpose(0, 2, 1, 3).reshape(D_MODEL, Q_W)

    wq = w_in[:, o_q:o_q + Q_W]
    wk = w_in[:, o_k:o_k + KV_W]

    segs, cols, off = {}, [], 0

    def add(name, cl):
        nonlocal off
        width = sum(c.shape[1] for c in cl)
        segs[name] = (off, off + width)
        cols.extend(cl)
        off += width

    add("q", [pair_heads(wq)])
    if rope:
        add("qrot", [pair_heads(swap_halves(wq, ATT_HEADS))])
    add("kd", [wk])
    if rope:
        add("kdrot", [swap_halves(wk, ATT_KV_HEADS)])
    if with_nat:
        add("nat", [w_in[:, o_k:o_k + 2 * KV_W]])
    add("mq", [w_in[:, o_mq:o_mq + MLQK_W]])
    add("mv", [w_in[:, o_mv:o_mv + MLV_W]])
    add("mo", [w_in[:, o_mo:o_mo + MLV_W]])
    add("ga", [w_in[:, o_ga:o_ga + D_MODEL]])
    add("gm", [w_in[:, o_gm:o_gm + D_MODEL]])
    add("gates", [w_in[:, o_g:o_g + GATE_W], jnp.zeros((D_MODEL, LANES - GATE_W), w_in.dtype)])
    wp = jnp.concatenate(cols, axis=1).astype(BF16)
    wvd_t = jnp.concatenate([w_in[:, o_v:o_v + KV_W], w_in[:, o_mk:o_mk + MLQK_W]], axis=1).T.astype(BF16)
    bg = jnp.pad(b_gates.astype(F32), (0, LANES - GATE_W)).reshape(1, LANES)
    return wp, wvd_t, bg, segs


def _rope_tables(n_tok):
    rows = n_tok // GRID_W
    row = np.repeat(np.arange(rows), GRID_W).astype(np.float32)
    col = np.tile(np.arange(GRID_W), rows).astype(np.float32)
    inv = (np.float32(ROPE_THETA) ** (-np.arange(AXIS_FREQS, dtype=np.float32) / AXIS_FREQS)).astype(np.float32)
    ang = np.concatenate([row[:, None] * inv, col[:, None] * inv], axis=-1).astype(np.float32)
    cos, sin = np.cos(ang), np.sin(ang)
    c64 = np.concatenate([cos, cos], axis=-1)
    s64 = np.concatenate([-sin, sin], axis=-1)
    return (jnp.asarray(np.tile(c64, (1, 2)), F32), jnp.asarray(np.tile(s64, (1, 2)), F32))


def _log_sigmoid(x):
    return jnp.minimum(x, 0.0) - jnp.log1p(jnp.exp(-jnp.abs(x)))


def _prefix_sum_rows(x):
    row = lax.broadcasted_iota(jnp.int32, x.shape, 0)
    k = 1
    while k < x.shape[0]:
        x = x + jnp.where(row >= k, pltpu.roll(x, k, 0), 0.0)
        k *= 2
    return x


def _proj_kernel(*refs, segs, rope, with_nat):
    it = iter(refs)
    x_ref, mod_ref, g_ref, w_ref, wvt_ref, bg_ref = (next(it) for _ in range(6))
    cos_ref = sin_ref = None
    if rope:
        cos_ref, sin_ref = next(it), next(it)
    q_ref, kd_ref, vd_ref = next(it), next(it), next(it)
    nat_ref = next(it) if with_nat else None
    mq_ref, mk_ref, mv_ref, gates_ref, gates_t_ref, mo_ref, ga_ref, gm_ref = (next(it) for _ in range(8))

    x = x_ref[0]
    shift = mod_ref[0, 0:1, :]
    scale = mod_ref[0, 1:2, :]
    ms = jnp.mean(x * x, axis=-1, keepdims=True)
    xn = x * lax.rsqrt(ms + EPS) * g_ref[...]
    xb = (xn * (1.0 + scale) + shift).astype(BF16)

    def seg(name):
        lo, hi = segs[name]
        return jnp.dot(xb, w_ref[:, lo:hi], preferred_element_type=F32)

    uq = seg("q")
    ukd = seg("kd")
    if rope:
        uqr = seg("qrot")
        ukr = seg("kdrot")
        cs = cos_ref[...]
        sn = sin_ref[...]
        for h in range(ATT_HEADS // 2):
            sl = slice(h * LANES, (h + 1) * LANES)
            q_ref[0, h] = ((uq[:, sl] * cs + uqr[:, sl] * sn) * HEAD_DIM ** -0.5).astype(BF16)
        kd_ref[0] = (ukd * cs + ukr * sn).astype(BF16)
    else:
        for h in range(ATT_HEADS // 2):
            sl = slice(h * LANES, (h + 1) * LANES)
            q_ref[0, h] = (uq[:, sl] * HEAD_DIM ** -0.5).astype(BF16)
        kd_ref[0] = ukd.astype(BF16)
    ut = lax.dot_general(wvt_ref[...], xb, (((1,), (1,)), ((), ())), preferred_element_type=F32)
    vd_ref[0] = ut[:KV_W].astype(BF16)
    mk_ref[0] = ut[KV_W:].astype(BF16)
    if with_nat:
        nat_ref[0] = seg("nat")
    mq_ref[0] = (seg("mq") * ML_DK ** -0.5).astype(BF16)
    mv_ref[0] = seg("mv").astype(BF16)
    gates = seg("gates") + bg_ref[...]
    lf = _log_sigmoid(gates)
    csum = _prefix_sum_rows(lf)
    lane = lax.broadcasted_iota(jnp.int32, gates.shape, 1)
    cum = jnp.where(lane >= 2 * ML_HEADS, csum[-1:, :] - csum + lf, csum)
    gates_ref[0] = cum
    gates_t_ref[0, :GATE_W, :] = gates.T[:GATE_W, :]
    gates_t_ref[0, GATE_W:, :] = cum.T[:GATE_W, :]
    mo_ref[0] = _sigmoid(seg("mo")).astype(BF16)
    ga_ref[0] = _sigmoid(seg("ga")).astype(BF16)
    gm_ref[0] = _sigmoid(seg("gm")).astype(BF16)


def _project(x, mod, g_pre, w_in, b_gates, mod_row, rope, with_nat, tm, b0, bsz):
    _, seq, d = x.shape
    wp, wvd_t, bg, segs = _pack_w_in(w_in, b_gates, True, True)
    nw = wp.shape[1]
    tok3 = lambda b, i: (b, i, 0)
    in_specs = [pl.BlockSpec((1, tm, d), lambda b, i: (b + b0, i, 0)),
                pl.BlockSpec((1, 6, d), lambda b, i: (mod_row(b), 0, 0)),
                _const_spec((1, d)),
                pl.BlockSpec((d, nw), lambda b, i: (0, 0), pipeline_mode=pl.Buffered(1)),
                _const_spec((KV_W + MLQK_W, d)), _const_spec((1, LANES))]
    args = [x, mod, g_pre.reshape(1, d), wp, wvd_t, bg]
    if rope:
        cos_t, sin_t = _rope_tables(seq)
        in_specs += [pl.BlockSpec((tm, LANES), lambda b, i: (i, 0))] * 2
        args += [cos_t, sin_t]

    def tok_out(width, dtype):
        return (jax.ShapeDtypeStruct((bsz, seq, width), dtype), pl.BlockSpec((1, tm, width), tok3))

    outs = [(jax.ShapeDtypeStruct((bsz, ATT_HEADS // 2, seq, LANES), BF16),
             pl.BlockSpec((1, ATT_HEADS // 2, tm, LANES), lambda b, i: (b, 0, i, 0))),
            tok_out(KV_W, BF16),
            (jax.ShapeDtypeStruct((bsz, KV_W, seq), BF16),
             pl.BlockSpec((1, KV_W, tm), lambda b, i: (b, 0, i)))]
    if with_nat:
        outs.append(tok_out(2 * KV_W, F32))
    outs += [tok_out(MLQK_W, BF16),
             (jax.ShapeDtypeStruct((bsz, MLQK_W, seq), BF16),
              pl.BlockSpec((1, MLQK_W, tm), lambda b, i: (b, 0, i))),
             tok_out(MLV_W, BF16), tok_out(LANES, F32),
             (jax.ShapeDtypeStruct((bsz, 2 * GATE_W, seq), F32),
              pl.BlockSpec((1, 2 * GATE_W, tm), lambda b, i: (b, 0, i))),
             tok_out(MLV_W, BF16), tok_out(D_MODEL, BF16), tok_out(D_MODEL, BF16)]
    res = pl.pallas_call(
        functools.partial(_proj_kernel, segs=segs, rope=rope, with_nat=with_nat),
        grid=(bsz, seq // tm),
        in_specs=in_specs,
        out_specs=[o[1] for o in outs],
        out_shape=[o[0] for o in outs],
        compiler_params=_cparams(("parallel", "parallel")),
        name="proj_rope" if rope else "proj_ctx",
    )(*args)
    names = ["q", "kd", "vd"] + (["nat"] if with_nat else []) + ["mq", "mk", "mv", "gates", "gates_t", "mo", "ga", "gm"]
    return dict(zip(names, res))


def _attn_kernel(*refs, window, n_blocks, n_ctx):
    it = iter(refs)
    sink_ref, q_ref = next(it), next(it)
    if window:
        kp_ref, kc_ref, kn_ref, vp_ref, vc_ref, vn_ref = (next(it) for _ in range(6))
    ck_ref, cv_ref, o_ref = next(it), next(it), next(it)
    j = pl.program_id(1)
    n_win = 3 * BLOCK if window else 0
    cols = ATT_GROUP * BLOCK

    col_id = lax.broadcasted_iota(jnp.int32, (1, cols), 1)
    if window:
        kl = lax.broadcasted_iota(jnp.int32, (BLOCK, cols), 0)
        ql = lax.broadcasted_iota(jnp.int32, (BLOCK, cols), 1) & (BLOCK - 1)
        prev_ok = kl >= ql + jnp.where(j == 0, BLOCK, 0)
        next_ok = kl <= ql - jnp.where(j == n_blocks - 1, BLOCK, 0)
    lane_row = lax.broadcasted_iota(jnp.int32, (1, LANES), 1) < HEAD_DIM
    half_lo = jnp.where(lane_row, 1.0, 0.0).astype(BF16)
    half_hi = jnp.where(lane_row, 0.0, 1.0).astype(BF16)

    if window:
        keys = jnp.concatenate([kp_ref[0], kc_ref[0], kn_ref[0], ck_ref[0]], axis=0)
        vals_t = jnp.concatenate([vp_ref[0], vc_ref[0], vn_ref[0], cv_ref[0]], axis=1)
    else:
        keys = ck_ref[0]
        vals_t = cv_ref[0]

    for kvh in range(ATT_KV_HEADS):
        qs = jnp.concatenate([q_ref[0, g] * (half_lo if kvh == 0 else half_hi) for g in range(ATT_GROUP)], axis=0)
        s = lax.dot_general(keys, qs, (((1,), (1,)), ((), ())), preferred_element_type=F32)
        if window:
            s = jnp.concatenate([jnp.where(prev_ok, s[:BLOCK], NEG), s[BLOCK:2 * BLOCK],
                                 jnp.where(next_ok, s[2 * BLOCK:n_win], NEG), s[n_win:]], axis=0)
        snk = jnp.full((1, cols), sink_ref[kvh * ATT_GROUP + ATT_GROUP - 1], F32)
        for g in range(ATT_GROUP - 2, -1, -1):
            snk = jnp.where(col_id < (g + 1) * BLOCK, sink_ref[kvh * ATT_GROUP + g], snk)
        m = jnp.maximum(jnp.max(s, axis=0, keepdims=True), snk)
        p = jnp.exp(s - m)
        den = jnp.sum(p, axis=0, keepdims=True) + jnp.exp(snk - m)
        o_t = jnp.dot(vals_t, p.astype(BF16), preferred_element_type=F32) / den
        o_t = o_t[kvh * HEAD_DIM:(kvh + 1) * HEAD_DIM]
        for pair in range(ATT_GROUP // 2):
            two = jnp.concatenate([o_t[:, (2 * pair) * BLOCK:(2 * pair + 1) * BLOCK],
                                   o_t[:, (2 * pair + 1) * BLOCK:(2 * pair + 2) * BLOCK]], axis=0)
            col = (kvh * (ATT_GROUP // 2) + pair) * LANES
            o_ref[0, :, col:col + LANES] = two.T.astype(BF16)


def _attention(q, kd, vd_t, ckd, cvd_t, sink, window):
    bsz, _, seq, _ = q.shape
    nb = seq // BLOCK
    n_ctx = ckd.shape[1]
    in_specs = [pl.BlockSpec(memory_space=pltpu.SMEM),
                pl.BlockSpec((1, ATT_HEADS // 2, BLOCK, LANES), lambda b, j: (b, 0, j, 0))]
    args = [sink.astype(F32), q]
    if window:
        in_specs += [pl.BlockSpec((1, BLOCK, KV_W), lambda b, j: (b, jnp.maximum(j - 1, 0), 0)),
                     pl.BlockSpec((1, BLOCK, KV_W), lambda b, j: (b, j, 0)),
                     pl.BlockSpec((1, BLOCK, KV_W), lambda b, j: (b, jnp.minimum(j + 1, nb - 1), 0)),
                     pl.BlockSpec((1, KV_W, BLOCK), lambda b, j: (b, 0, jnp.maximum(j - 1, 0))),
                     pl.BlockSpec((1, KV_W, BLOCK), lambda b, j: (b, 0, j)),
                     pl.BlockSpec((1, KV_W, BLOCK), lambda b, j: (b, 0, jnp.minimum(j + 1, nb - 1)))]
        args += [kd, kd, kd, vd_t, vd_t, vd_t]
    in_specs += [pl.BlockSpec((1, n_ctx, KV_W), lambda b, j: (b, 0, 0)),
                 pl.BlockSpec((1, KV_W, n_ctx), lambda b, j: (b, 0, 0))]
    args += [ckd, cvd_t]
    return pl.pallas_call(
        functools.partial(_attn_kernel, window=window, n_blocks=nb, n_ctx=n_ctx),
        grid=(bsz, nb),
        in_specs=in_specs,
        out_specs=pl.BlockSpec((1, BLOCK, Q_W), lambda b, j: (b, j, 0)),
        out_shape=jax.ShapeDtypeStruct((bsz, seq, Q_W), BF16),
        compiler_params=_cparams(("parallel", "parallel")),
        name="attn_window" if window else "attn_ctx",
    )(*args)


def _flat_heads(t, transpose):
    b, l = t.shape[:2]
    out = t.reshape(b, l, KV_W).astype(BF16)
    return jnp.swapaxes(out, 1, 2) if transpose else out


def _mlstm_kernel(*refs, chunk_len, n_chunks, has_init, emit_state):
    it = iter(refs)
    qkvg = [[next(it) for _ in range(5)] for _ in range(2)]
    if has_init:
        c0_ref, n0_ref, m0_ref = next(it), next(it), next(it)
    h_refs = [next(it), next(it)]
    if emit_state:
        co_ref, no_ref, mo_ref = next(it), next(it), next(it)
    st_s, m_s = next(it), next(it)
    t = pl.program_id(1)
    L = chunk_len

    @pl.when(t == 0)
    def _():
        for d in range(2):
            for h in range(ML_HEADS):
                if has_init:
                    st_s[d, h, :, :ML_DV] = c0_ref[0, d, h].T
                    st_s[d, h, :, ML_DV:] = jnp.broadcast_to(n0_ref[0, d, h:h + 1, :], (LANES, ML_DK)).T
                    m_s[d, h] = m0_ref[0, d, h:h + 1, :]
                else:
                    st_s[d, h] = jnp.zeros((ML_DK, ML_DV + LANES), F32)
                    m_s[d, h] = jnp.zeros((1, LANES), F32)

    ri = lax.broadcasted_iota(jnp.int32, (L, L), 0)
    ci = lax.broadcasted_iota(jnp.int32, (L, L), 1)
    ones_blk = jnp.ones((L, LANES), BF16)

    for d in range(2):
        q_ref, kt_ref, v_ref, cum_ref, rows_ref = qkvg[d]
        tri = (ci <= ri) if d == 0 else (ci >= ri)
        bc = cum_ref[0]
        g_t = rows_ref[0, :GATE_W, :]
        br = rows_ref[0, GATE_W:, :]
        last = L - 1 if d == 0 else 0

        for h in range(ML_HEADS):
            icol_i = 2 * ML_HEADS * d + h
            fcol = icol_i + ML_HEADS
            bcol = bc[:, fcol:fcol + 1]
            brow = br[fcol:fcol + 1, :]
            irow = g_t[icol_i:icol_i + 1, :]
            b_last = bc[last:last + 1, fcol:fcol + 1]
            m_prev = m_s[d, h][:, 0:1]
            st_prev = st_s[d, h]
            qh = q_ref[0, :, h * ML_DK:(h + 1) * ML_DK]
            kt = kt_ref[0, h * ML_DK:(h + 1) * ML_DK, :]
            v_aug = jnp.concatenate([v_ref[0, :, h * ML_DV:(h + 1) * ML_DV], ones_blk], axis=1)

            rel = jnp.where(tri, irow - brow, NEG)
            inter = bcol + m_prev
            m_t = jnp.maximum(inter, bcol + jnp.max(rel, axis=-1, keepdims=True))
            w = jnp.exp(rel + (bcol - m_t))
            w_inter = jnp.exp(inter - m_t)
            s = jnp.dot(qh, kt, preferred_element_type=F32) * w
            lhs = jnp.concatenate([s.astype(BF16), (qh.astype(F32) * w_inter).astype(BF16)], axis=1)
            rhs = jnp.concatenate([v_aug, st_prev.astype(BF16)], axis=0)
            mix = jnp.dot(lhs, rhs, preferred_element_type=F32)
            inv = 1.0 / jnp.maximum(jnp.abs(mix[:, ML_DV:]), jnp.exp(-m_t))
            for c in range(ML_DV // LANES):
                lo = h * ML_DV + c * LANES
                h_refs[d][0, :, lo:lo + LANES] = mix[:, c * LANES:(c + 1) * LANES] * inv

            a = b_last - brow + irow
            m_new = jnp.maximum(b_last + m_prev, jnp.max(a, axis=-1, keepdims=True))
            wk = jnp.exp(a - m_new)
            decay = jnp.exp(b_last + m_prev - m_new)
            kw_t = (kt.astype(F32) * wk).astype(BF16)
            st_new = decay * st_prev + jnp.dot(kw_t, v_aug, preferred_element_type=F32)
            st_s[d, h] = st_new
            m_s[d, h] = jnp.broadcast_to(m_new, (1, LANES))
            if emit_state:
                @pl.when(t == n_chunks - 1)
                def _():
                    co_ref[0, d, h] = st_new[:, :ML_DV].T
                    no_ref[0, d, h:h + 1, :] = st_new[:, ML_DV:].T[0:1, :]
                    mo_ref[0, d, h:h + 1, :] = jnp.broadcast_to(m_new, (1, LANES))


def _mlstm(mq, mk_t, mv, gate_cum, gate_rows, init, emit_state, b0=0):
    bsz, seq, _ = mq.shape
    chunk_len = min(ML_CHUNK, seq)
    nc = seq // chunk_len
    fwd = lambda b, t: (b, t, 0)
    bwd = lambda b, t: (b, nc - 1 - t, 0)
    fwd_t = lambda b, t: (b, 0, t)
    bwd_t = lambda b, t: (b, 0, nc - 1 - t)
    state5 = lambda b, t: (b, 0, 0, 0, 0)
    state4 = lambda b, t: (b, 0, 0, 0)
    init5 = lambda b, t: (b + b0, 0, 0, 0, 0)
    init4 = lambda b, t: (b + b0, 0, 0, 0)
    in_specs, args = [], []
    for chunk, chunk_t in ((fwd, fwd_t), (bwd, bwd_t)):
        in_specs += [pl.BlockSpec((1, chunk_len, MLQK_W), chunk), pl.BlockSpec((1, MLQK_W, chunk_len), chunk_t),
                     pl.BlockSpec((1, chunk_len, MLV_W), chunk), pl.BlockSpec((1, chunk_len, LANES), chunk),
                     pl.BlockSpec((1, 2 * GATE_W, chunk_len), chunk_t)]
        args += [mq, mk_t, mv, gate_cum, gate_rows]
    if init is not None:
        c0, n0, m0 = init
        in_specs += [pl.BlockSpec((1, 2, ML_HEADS, ML_DV, ML_DK), init5),
                     pl.BlockSpec((1, 2, ML_HEADS, ML_DK), init4),
                     pl.BlockSpec((1, 2, ML_HEADS, LANES), init4)]
        args += [c0.astype(F32), n0.astype(F32),
                 jnp.broadcast_to(m0.astype(F32)[..., None], m0.shape + (LANES,))]
    out_shape = [jax.ShapeDtypeStruct((bsz, seq, MLV_W), F32)] * 2
    out_specs = [pl.BlockSpec((1, chunk_len, MLV_W), fwd), pl.BlockSpec((1, chunk_len, MLV_W), bwd)]
    if emit_state:
        out_shape += [jax.ShapeDtypeStruct((bsz, 2, ML_HEADS, ML_DV, ML_DK), F32),
                      jax.ShapeDtypeStruct((bsz, 2, ML_HEADS, ML_DK), F32),
                      jax.ShapeDtypeStruct((bsz, 2, ML_HEADS, LANES), F32)]
        out_specs += [pl.BlockSpec((1, 2, ML_HEADS, ML_DV, ML_DK), state5),
                      pl.BlockSpec((1, 2, ML_HEADS, ML_DK), state4),
                      pl.BlockSpec((1, 2, ML_HEADS, LANES), state4)]
    return pl.pallas_call(
        functools.partial(_mlstm_kernel, chunk_len=chunk_len, n_chunks=nc, has_init=init is not None, emit_state=emit_state),
        grid=(bsz, nc),
        in_specs=in_specs,
        out_specs=out_specs,
        out_shape=out_shape,
        scratch_shapes=[pltpu.VMEM((2, ML_HEADS, ML_DK, ML_DV + LANES), F32),
                        pltpu.VMEM((2, ML_HEADS, 1, LANES), F32)],
        compiler_params=_cparams(("parallel", "arbitrary")),
        name="mlstm_state" if emit_state else "mlstm",
    )(*args)


def _sigmoid(x):
    return 0.5 * jnp.tanh(0.5 * x) + 0.5


def _rms(x, g):
    return x * lax.rsqrt(jnp.mean(x * x, axis=-1, keepdims=True) + EPS) * g


def _pack_bf16_pairs(x):
    n = x.shape[1] // 2
    lo = pltpu.bitcast(x[:, :n].astype(BF16).astype(F32), jnp.int32)
    hi = pltpu.bitcast(x[:, n:].astype(BF16).astype(F32), jnp.int32)
    return lax.shift_right_logical(lo, 16) | hi


def _unpack_bf16_pairs(p):
    lo = pltpu.bitcast(lax.shift_left(p, 16), F32)
    hi = pltpu.bitcast(p & jnp.int32(-65536), F32)
    return lo, hi


def _merge_kernel(att_ref, hf_ref, hb_ref, mo_ref, ga_ref, gm_ref, x_ref, mod_ref, nml_ref, gpm_ref, gpf_ref,
                  wua_ref, wum_ref, wo_ref, wr_ref, br_ref, x1_ref, h2_ref, idx_ref, gate_ref, cnt_ref):
    hsum = hf_ref[0] + hb_ref[0]
    parts = []
    for h in range(ML_HEADS):
        hh = hsum[:, h * ML_DV:(h + 1) * ML_DV]
        parts.append(hh * lax.rsqrt(jnp.mean(hh * hh, axis=-1, keepdims=True) + EPS))
    hn = jnp.concatenate(parts, axis=-1) * nml_ref[...]
    ml = (hn * mo_ref[0].astype(F32)).astype(BF16)
    a = jnp.dot(att_ref[0], wua_ref[...], preferred_element_type=F32)
    m = jnp.dot(ml, wum_ref[...], preferred_element_type=F32)
    z = (ga_ref[0].astype(F32) * a + gm_ref[0].astype(F32) * m).astype(BF16)
    mix = jnp.dot(z, wo_ref[...], preferred_element_type=F32)
    gate1 = mod_ref[0, 2:3, :]
    shift2 = mod_ref[0, 3:4, :]
    scale2 = mod_ref[0, 4:5, :]
    x1 = x_ref[0] + gate1 * _rms(mix, gpm_ref[...])
    x1_ref[0] = x1
    h2 = _rms(x1, gpf_ref[...]) * (1.0 + scale2) + shift2
    h2_ref[0] = _pack_bf16_pairs(h2)

    h2_hi = h2.astype(BF16)
    h2_lo = (h2 - h2_hi.astype(F32)).astype(BF16)
    wr = wr_ref[...]
    wr_hi = wr.astype(BF16)
    wr_lo = (wr - wr_hi.astype(F32)).astype(BF16)
    hi_terms = jnp.dot(h2_hi, jnp.concatenate([wr_hi, wr_lo], axis=1), preferred_element_type=F32)
    logits = (hi_terms[:, :LANES] + hi_terms[:, LANES:]
              + jnp.dot(h2_lo, wr_hi, preferred_element_type=F32)) + br_ref[...]
    lane = lax.broadcasted_iota(jnp.int32, logits.shape, 1).astype(F32)
    work = jnp.where(lane < N_EXPERTS, logits, -jnp.inf)
    idx_out = jnp.zeros(logits.shape, F32)
    val_out = jnp.zeros(logits.shape, F32)
    picked = jnp.zeros(logits.shape, F32)
    top0 = None
    esum = None
    for k in range(TOP_K):
        mx = jnp.max(work, axis=-1, keepdims=True)
        sel = jnp.min(jnp.where(work == mx, lane, float(LANES)), axis=-1, keepdims=True)
        if k == 0:
            top0 = mx
        e = jnp.exp(mx - top0)
        esum = e if k == 0 else esum + e
        idx_out = jnp.where(lane == k, sel, idx_out)
        val_out = jnp.where(lane == k, e, val_out)
        picked = jnp.where(lane == sel, 1.0, picked)
        work = jnp.where(lane == sel, -jnp.inf, work)
    idx_ref[0] = idx_out.astype(jnp.int32)
    gate_ref[0] = val_out / esum
    cnt_ref[0] = jnp.sum(picked, axis=0, keepdims=True)


def _merge(att, hf, hb, pj, x, mod, mod_row, p, tm, b0):
    bsz = att.shape[0]
    _, seq, d = x.shape
    tok3 = lambda b, i: (b, i, 0)
    wr = jnp.pad(p["w_router"].astype(F32), ((0, 0), (0, LANES - N_EXPERTS)))
    br = jnp.pad(p["b_router"].astype(F32), (0, LANES - N_EXPERTS)).reshape(1, LANES)
    row = lambda v: v.astype(F32).reshape(1, -1)
    in_specs = [pl.BlockSpec((1, tm, Q_W), tok3),
                pl.BlockSpec((1, tm, MLV_W), tok3), pl.BlockSpec((1, tm, MLV_W), tok3),
                pl.BlockSpec((1, tm, MLV_W), tok3), pl.BlockSpec((1, tm, d), tok3), pl.BlockSpec((1, tm, d), tok3),
                pl.BlockSpec((1, tm, d), lambda b, i: (b + b0, i, 0)),
                pl.BlockSpec((1, 6, d), lambda b, i: (mod_row(b), 0, 0)),
                _const_spec((1, MLV_W)), _const_spec((1, d)), _const_spec((1, d)),
                _const_spec((Q_W, d)), _const_spec((MLV_W, d)), _const_spec((d, d)),
                _const_spec((d, LANES)), _const_spec((1, LANES))]
    out_shape = [jax.ShapeDtypeStruct((bsz, seq, d), F32), jax.ShapeDtypeStruct((bsz, seq, d // 2), jnp.int32),
                 jax.ShapeDtypeStruct((bsz, seq, LANES), jnp.int32), jax.ShapeDtypeStruct((bsz, seq, LANES), F32),
                 jax.ShapeDtypeStruct((bsz * (seq // tm), 1, LANES), F32)]
    out_specs = [pl.BlockSpec((1, tm, d), tok3), pl.BlockSpec((1, tm, d // 2), tok3),
                 pl.BlockSpec((1, tm, LANES), tok3), pl.BlockSpec((1, tm, LANES), tok3),
                 pl.BlockSpec((1, 1, LANES), lambda b, i: (b * (seq // tm) + i, 0, 0))]
    return pl.pallas_call(
        _merge_kernel,
        grid=(bsz, seq // tm),
        in_specs=in_specs,
        out_specs=out_specs,
        out_shape=out_shape,
        compiler_params=_cparams(("parallel", "parallel")),
        name="merge_router",
    )(att, hf, hb, pj["mo"], pj["ga"], pj["gm"], x, mod, row(p["norm_ml"]), row(p["g_post_mix"]),
      row(p["g_pre_ffn"]), p["w_up_att"].astype(BF16), p["w_up_ml"].astype(BF16), p["w_out"].astype(BF16), wr, br)


def _sc_gather_rows(table, idx):
    n = idx.shape[0]
    width = table.shape[1]
    chunk = SC_GATHER_CHUNK
    n_workers = SC_CORES * SC_SUBCORES
    rows_per_worker = n // n_workers
    n_pairs = rows_per_worker // (2 * chunk)
    assert n_pairs * 2 * chunk * n_workers == n
    mesh = plsc.VectorSubcoreMesh(core_axis_name="c", subcore_axis_name="s")

    def body(table_hbm, idx_hbm, out_hbm, idx_v, rows_v, gsem, osem):
        wid = lax.axis_index("s") * SC_CORES + lax.axis_index("c")
        base0 = wid * rows_per_worker

        def gather_copy(b):
            return pltpu.make_async_copy(table_hbm.at[idx_v.at[b]], rows_v.at[b], gsem.at[b])

        def write_copy(ci, b):
            base = pl.multiple_of(base0 + ci * chunk, 8)
            return pltpu.make_async_copy(rows_v.at[b], out_hbm.at[pl.ds(base, chunk)], osem.at[b])

        def issue(ci, b):
            base = pl.multiple_of(base0 + ci * chunk, 8)
            pltpu.sync_copy(idx_hbm.at[pl.ds(base, chunk)], idx_v.at[b])
            gather_copy(b).start()

        def finish(ci, b):
            gather_copy(b).wait()
            write_copy(ci, b).start()

        issue(0, 0)

        @pl.loop(0, n_pairs)
        def _(j):
            @pl.when(j > 0)
            def _():
                write_copy(2 * j - 1, 1).wait()
            issue(2 * j + 1, 1)
            finish(2 * j, 0)

            @pl.when(j < n_pairs - 1)
            def _():
                write_copy(2 * j, 0).wait()
                issue(2 * j + 2, 0)
            finish(2 * j + 1, 1)

        write_copy(2 * n_pairs - 2, 0).wait()
        write_copy(2 * n_pairs - 1, 1).wait()

    return pl.kernel(
        body, mesh=mesh,
        out_type=jax.ShapeDtypeStruct((n, width), table.dtype),
        scratch_types=[pltpu.VMEM((2, chunk), jnp.int32), pltpu.VMEM((2, chunk, width), table.dtype),
                       pltpu.SemaphoreType.DMA((2,)), pltpu.SemaphoreType.DMA((2,))],
    )(table, idx)


def _ffn_kernel(be_ref, nu_ref, xs_ref, w1_ref, b1_ref, w2_ref, b2_ref, ys_ref, *bf16_refs, weights_are_bf16):
    i = pl.program_id(0)
    e = be_ref[i]
    e_prev = be_ref[jnp.maximum(i - 1, 0)]
    if weights_are_bf16:
        w1b_ref, w2b_ref = w1_ref, w2_ref
    else:
        w1b_ref, w2b_ref, w1_stage, w2_stage, sems = bf16_refs

        def weight_copies(expert, slot):
            return (pltpu.make_async_copy(w1_ref.at[expert], w1_stage.at[slot], sems.at[0, slot]),
                    pltpu.make_async_copy(w2_ref.at[expert], w2_stage.at[slot], sems.at[1, slot]))

        @pl.when(i == 0)
        def _():
            for cp in weight_copies(e, e % 2):
                cp.start()

        @pl.when((i == 0) | (e != e_prev))
        def _():
            slot = e % 2
            for cp in weight_copies(e, slot):
                cp.wait()
            w1b_ref[0] = w1_stage[slot].astype(BF16)
            w2b_ref[0] = w2_stage[slot].astype(BF16)

            @pl.when(e + 1 < N_EXPERTS)
            def _():
                for cp in weight_copies(e + 1, 1 - slot):
                    cp.start()

    @pl.when(i < nu_ref[0])
    def _():
        x_lo, x_hi = _unpack_bf16_pairs(xs_ref[...])
        xb = jnp.concatenate([x_lo.astype(BF16), x_hi.astype(BF16)], axis=1)
        hmid = jnp.dot(xb, w1b_ref[0], preferred_element_type=F32) + b1_ref[0]
        glu = jnp.minimum(hmid[:, :D_FF], SWIGLU_LIMIT)
        lin = jnp.clip(hmid[:, D_FF:], -SWIGLU_LIMIT, SWIGLU_LIMIT)
        act = ((lin + 1.0) * glu * _sigmoid(SWIGLU_ALPHA * glu)).astype(BF16)
        y = jnp.dot(act, w2b_ref[0], preferred_element_type=F32) + b2_ref[0]
        ys_ref[...] = _pack_bf16_pairs(y)

    @pl.when(i >= nu_ref[0])
    def _():
        ys_ref[...] = jnp.zeros(ys_ref.shape, jnp.int32)


def _sc_scatter_rows(tables, positions, n_rows):
    width = tables[0].shape[1]
    chunk = SC_GATHER_CHUNK
    n_workers = SC_CORES * SC_SUBCORES
    n_streams = len(tables)
    mesh = plsc.VectorSubcoreMesh(core_axis_name="c", subcore_axis_name="s")

    def body(*refs):
        table_refs = refs[0:2 * n_streams:2]
        pos_refs = refs[1:2 * n_streams:2]
        out_hbm, idx_v, rows_v, lsem, ssem = refs[2 * n_streams:]
        wid = lax.axis_index("s") * SC_CORES + lax.axis_index("c")

        for table_hbm, pos_hbm in zip(table_refs, pos_refs):
            tok_per_worker = table_hbm.shape[0] // n_workers
            n_pairs = tok_per_worker // (2 * chunk)
            assert n_pairs * 2 * chunk * n_workers == table_hbm.shape[0]
            base0 = wid * tok_per_worker

            def load_copy(ci, b):
                base = pl.multiple_of(base0 + ci * chunk, 8)
                return pltpu.make_async_copy(table_hbm.at[pl.ds(base, chunk)], rows_v.at[b], lsem.at[b])

            def scatter_copy(b, k):
                return pltpu.make_async_copy(rows_v.at[b], out_hbm.at[idx_v.at[b, k]], ssem.at[b])

            def load(ci, b):
                base = pl.multiple_of(base0 + ci * chunk, 8)
                load_copy(ci, b).start()
                for k in range(TOP_K):
                    pltpu.sync_copy(pos_hbm.at[k, pl.ds(base, chunk)], idx_v.at[b, k])

            def scatter(ci, b):
                load_copy(ci, b).wait()
                for k in range(TOP_K):
                    scatter_copy(b, k).start()

            def drain(b):
                for k in range(TOP_K):
                    scatter_copy(b, k).wait()

            load(0, 0)

            @pl.loop(0, n_pairs)
            def _(j):
                @pl.when(j > 0)
                def _():
                    drain(1)
                load(2 * j + 1, 1)
                scatter(2 * j, 0)

                @pl.when(j < n_pairs - 1)
                def _():
                    drain(0)
                    load(2 * j + 2, 0)
                scatter(2 * j + 1, 1)

            drain(0)
            drain(1)

    args = [a for pair in zip(tables, positions) for a in pair]
    return pl.kernel(
        body, mesh=mesh,
        out_type=jax.ShapeDtypeStruct((n_rows, width), tables[0].dtype),
        scratch_types=[pltpu.VMEM((2, TOP_K, chunk), jnp.int32), pltpu.VMEM((2, chunk, width), tables[0].dtype),
                       pltpu.SemaphoreType.DMA((2,)), pltpu.SemaphoreType.DMA((2,))],
    )(*args)


def _pos_kernel(idx_ref, base_ref, pos_ref):
    idx = idx_ref[...]
    rows = idx.shape[0]
    lane = lax.broadcasted_iota(jnp.int32, idx.shape, 1)
    hots = [jnp.where(lane == idx[:, k:k + 1], 1.0, 0.0) for k in range(TOP_K)]
    cnt = hots[0] + hots[1] + hots[2] + hots[3]
    ri = lax.broadcasted_iota(jnp.int32, (rows, rows), 0)
    ci = lax.broadcasted_iota(jnp.int32, (rows, rows), 1)
    earlier = jnp.where(ci < ri, 1.0, 0.0).astype(BF16)
    ahead = jnp.dot(earlier, cnt.astype(BF16), preferred_element_type=F32) + base_ref[0]
    posmat = jnp.zeros(idx.shape, F32)
    for k in range(TOP_K):
        posmat = jnp.where(lane == k, jnp.sum(hots[k] * ahead, axis=-1, keepdims=True), posmat)
    pos_ref[...] = posmat.T[:TOP_K, :].astype(jnp.int32)


def _route(idx_list, count_list, every_expert):
    tiles = [c.shape[0] for c in count_list]
    tile_tok = [idx.size // LANES // n for idx, n in zip(idx_list, tiles)]
    n_tok = sum(n * sz for n, sz in zip(tiles, tile_tok))
    n_blocks = n_tok * TOP_K // MOE_ROWS + N_EXPERTS
    cnt = jnp.concatenate([c.reshape(-1, LANES) for c in count_list], axis=0).astype(jnp.int32)
    tile_off = jnp.cumsum(cnt, axis=0) - cnt
    total = jnp.sum(cnt, axis=0)
    padded = (total + MOE_ROWS - 1) // MOE_ROWS * MOE_ROWS
    if every_expert:
        padded = jnp.where(jnp.arange(LANES) < N_EXPERTS, jnp.maximum(padded, MOE_ROWS), 0)
    pend = jnp.cumsum(padded)
    base = ((pend - padded)[None, :] + tile_off).astype(F32)
    starts = jnp.arange(n_blocks, dtype=jnp.int32) * MOE_ROWS
    block_e = jnp.minimum(jnp.sum(pend[None, :N_EXPERTS] <= starts[:, None], axis=1), N_EXPERTS - 1).astype(jnp.int32)
    n_used = (pend[N_EXPERTS - 1] // MOE_ROWS).astype(jnp.int32).reshape(1)
    positions, first = [], 0
    for idx, n_tiles, tile in zip(idx_list, tiles, tile_tok):
        stream_tok = n_tiles * tile
        positions.append(pl.pallas_call(
            _pos_kernel,
            grid=(n_tiles,),
            in_specs=[pl.BlockSpec((tile, LANES), lambda i: (i, 0)),
                      pl.BlockSpec((1, 1, LANES), lambda i: (i, 0, 0))],
            out_specs=pl.BlockSpec((TOP_K, tile), lambda i: (0, i)),
            out_shape=jax.ShapeDtypeStruct((TOP_K, stream_tok), jnp.int32),
            compiler_params=_cparams(("parallel",)),
            name="route_pos",
        )(idx.reshape(stream_tok, LANES), base[first:first + n_tiles].reshape(n_tiles, 1, LANES)))
        first += n_tiles
    return block_e, n_used, positions


def _moe(h2p_list, idx_list, count_list, w1, b1, w2, b2):
    half = h2p_list[0].shape[1]
    d = 2 * half
    cast_here = w1.dtype != BF16
    block_e, n_used, positions = _route(idx_list, count_list, every_expert=cast_here)
    n_blocks = block_e.shape[0]
    xs = _sc_scatter_rows(h2p_list, positions, n_blocks * MOE_ROWS)
    w1_spec = pl.BlockSpec((1, d, 2 * D_FF), lambda i, be, nu: (be[i], 0, 0))
    w2_spec = pl.BlockSpec((1, D_FF, d), lambda i, be, nu: (be[i], 0, 0))
    hbm_spec = pl.BlockSpec(memory_space=pl.ANY)
    grid_spec = pltpu.PrefetchScalarGridSpec(
        num_scalar_prefetch=2,
        grid=(n_blocks,),
        in_specs=[pl.BlockSpec((MOE_ROWS, half), lambda i, be, nu: (i, 0)),
                  hbm_spec if cast_here else w1_spec,
                  pl.BlockSpec((1, 1, 2 * D_FF), lambda i, be, nu: (be[i], 0, 0)),
                  hbm_spec if cast_here else w2_spec,
                  pl.BlockSpec((1, 1, d), lambda i, be, nu: (be[i], 0, 0))],
        out_specs=[pl.BlockSpec((MOE_ROWS, half), lambda i, be, nu: (i, 0))] + ([w1_spec, w2_spec] if cast_here else []),
        scratch_shapes=[pltpu.VMEM((2, d, 2 * D_FF), F32), pltpu.VMEM((2, D_FF, d), F32),
                        pltpu.SemaphoreType.DMA((2, 2))] if cast_here else [],
    )
    out_shape = [jax.ShapeDtypeStruct((n_blocks * MOE_ROWS, half), jnp.int32)]
    if cast_here:
        out_shape += [jax.ShapeDtypeStruct(w1.shape, BF16), jax.ShapeDtypeStruct(w2.shape, BF16)]
    res = pl.pallas_call(
        functools.partial(_ffn_kernel, weights_are_bf16=not cast_here),
        grid_spec=grid_spec,
        out_shape=out_shape,
        compiler_params=_cparams(("arbitrary",)),
        name="moe_ffn_cast" if cast_here else "moe_ffn",
    )(block_e, n_used, xs, w1, b1.reshape(N_EXPERTS, 1, -1), w2, b2.reshape(N_EXPERTS, 1, -1))
    w1b, w2b = (res[1], res[2]) if cast_here else (w1, w2)
    return [_sc_gather_rows(res[0], pos.reshape(-1)) for pos in positions], w1b, w2b


def _final_kernel(x1_ref, y0_ref, y1_ref, y2_ref, y3_ref, gate_ref, mod_ref, g_ref, *rest):
    o_ref = rest[-1]
    gates = gate_ref[0]
    lo = hi = None
    for k, y_ref in enumerate((y0_ref, y1_ref, y2_ref, y3_ref)):
        y_lo, y_hi = _unpack_bf16_pairs(y_ref[0, 0])
        gk = gates[:, k:k + 1]
        lo = gk * y_lo if k == 0 else lo + gk * y_lo
        hi = gk * y_hi if k == 0 else hi + gk * y_hi
    y = jnp.concatenate([lo, hi], axis=-1)
    gate2 = mod_ref[0, 5:6, :]
    o_ref[0] = x1_ref[0] + gate2 * _rms(y, g_ref[...])


def _final(x1, yg, gate, mod, mod_row, g_post_ffn, tm, b0, n_batch, out_prev):
    bsz, seq, d = x1.shape
    tok3 = lambda b, i: (b, i, 0)
    yg = yg.reshape(TOP_K, bsz, seq, d // 2)
    slot_specs = [pl.BlockSpec((1, 1, tm, d // 2), functools.partial(lambda b, i, k: (k, b, i, 0), k=k))
                  for k in range(TOP_K)]
    in_specs = ([pl.BlockSpec((1, tm, d), tok3)] + slot_specs +
                [pl.BlockSpec((1, tm, LANES), tok3),
                 pl.BlockSpec((1, 6, d), lambda b, i: (mod_row(b), 0, 0)), _const_spec((1, d))])
    args = [x1, yg, yg, yg, yg, gate, mod, g_post_ffn.astype(F32).reshape(1, d)]
    aliases = {}
    if out_prev is not None:
        in_specs.append(pl.BlockSpec(memory_space=pl.ANY))
        args.append(out_prev)
        aliases = {len(args) - 1: 0}
    return pl.pallas_call(
        _final_kernel,
        grid=(bsz, seq // tm),
        in_specs=in_specs,
        out_specs=pl.BlockSpec((1, tm, d), lambda b, i: (b + b0, i, 0)),
        out_shape=jax.ShapeDtypeStruct((n_batch, seq, d), F32),
        input_output_aliases=aliases,
        compiler_params=_cparams(("parallel", "parallel")),
        name="final_residual",
    )(*args)


def _stream(x, mod, mod_row, p, ctx_kv, init_state, rope, tm, b0, bsz):
    _, seq, d = x.shape
    is_ctx = ctx_kv is None
    pj = _project(x, mod, p["g_pre_mix"], p["w_in"], p["b_gates"], mod_row, rope, is_ctx, tm, b0, bsz)
    if is_ctx:
        att = _attention(pj["q"], None, None, pj["kd"], pj["vd"], p["sink"], window=False)
    else:
        att = _attention(pj["q"], pj["kd"], pj["vd"], ctx_kv[0], ctx_kv[1], p["sink"], window=True)
    assert min(ML_CHUNK, seq) == tm
    ml = _mlstm(pj["mq"], pj["mk"], pj["mv"], pj["gates"], pj["gates_t"], init_state, emit_state=is_ctx, b0=b0)
    x1, h2, idx, gate, tile_counts = _merge(att, ml[0], ml[1], pj, x, mod, mod_row, p, tm, b0)
    moe_in = (h2.reshape(bsz * seq, d // 2), idx, tile_counts)
    return x1, gate, moe_in, pj, ml


def kernel(x_prompt, x_sample, c, cache_k, cache_v, state_C, state_n, state_m, c_ctx, w_ada, b_ada, g_pre_mix,
           w_in, b_gates, attn_sink, norm_mlstm, w_up_att, w_up_ml, w_out, g_post_mix, g_pre_ffn, w_router,
           b_router, w1, b1, w2, b2, g_post_ffn):
    depth = w_ada.shape[0]
    n_dec = c.shape[0]
    cond = jnp.concatenate([c_ctx[None, :], c], axis=0).astype(F32)
    cond = jnp.pad(cond, ((0, 16 - cond.shape[0]), (0, 0)))
    y_prompt, y_sample = x_prompt, x_sample
    ks_, vs_, cs_, ns_, ms_ = [], [], [], [], []
    for l in range(depth):
        p = dict(g_pre_mix=g_pre_mix[l], w_in=w_in[l], b_gates=b_gates[l], sink=attn_sink[l], norm_ml=norm_mlstm[l],
                 w_up_att=w_up_att[l], w_up_ml=w_up_ml[l], w_out=w_out[l], g_post_mix=g_post_mix[l],
                 g_pre_ffn=g_pre_ffn[l], w_router=w_router[l], b_router=b_router[l], w1=w1[l], b1=b1[l],
                 w2=w2[l], b2=b2[l], g_post_ffn=g_post_ffn[l])
        mod = _adaln(cond, w_ada[l], b_ada[l]).reshape(16, 6, D_MODEL)
        row_p = lambda b: 0
        bsz, seq = x_prompt.shape[:2]
        x1_p, gate_p, moe_p, pj, ml = _stream(y_prompt, mod, row_p, p, None, None, False, 256, 0, bsz)
        nat = pj["nat"]
        ks_.append(nat[..., :KV_W].reshape(bsz, seq, ATT_KV_HEADS, HEAD_DIM))
        vs_.append(nat[..., KV_W:].reshape(bsz, seq, ATT_KV_HEADS, HEAD_DIM))
        cs_.append(ml[2])
        ns_.append(ml[3])
        ms_.append(ml[4][..., 0])
        ctx_kv = (_flat_heads(cache_k[:, l], False), _flat_heads(cache_v[:, l], True))
        init = (state_C[:, l], state_n[:, l], state_m[:, l])
        half = n_dec // 2
        experts = (p["w1"], p["b1"].astype(F32), p["w2"], p["b2"].astype(F32))
        groups = []
        for b0 in (0, half):
            row_s = functools.partial(lambda b, off: b + off + 1, off=b0)
            sl = slice(b0, b0 + half)
            x1_s, gate_s, moe_s, _, _ = _stream(y_sample, mod, row_s, p, (ctx_kv[0][sl], ctx_kv[1][sl]),
                                                init, True, 512, b0, half)
            groups.append((x1_s, gate_s, moe_s, row_s))
        (x1_a, gate_a, moe_a, row_a), (x1_b, gate_b, moe_b, row_b) = groups
        (yg_p, yg_a), w1b, w2b = _moe([moe_p[0], moe_a[0]], [moe_p[1], moe_a[1]], [moe_p[2], moe_a[2]], *experts)
        (yg_b,), _, _ = _moe([moe_b[0]], [moe_b[1]], [moe_b[2]], w1b, experts[1], w2b, experts[3])
        y_prompt = _final(x1_p, yg_p, gate_p, mod, row_p, p["g_post_ffn"], 256, 0, bsz, None)
        y_sample = _final(x1_a, yg_a, gate_a, mod, row_a, p["g_post_ffn"], 512, 0, n_dec, None)
        y_sample = _final(x1_b, yg_b, gate_b, mod, row_b, p["g_post_ffn"], 512, half, n_dec, y_sample)
    return (y_prompt, y_sample, jnp.stack(ks_, axis=1), jnp.stack(vs_, axis=1), jnp.stack(cs_, axis=1),
            jnp.stack(ns_, axis=1), jnp.stack(ms_, axis=1))
```

```python
import functools

import numpy as np
import jax
import jax.numpy as jnp
from jax import lax
from jax.experimental import pallas as pl
from jax.experimental.pallas import tpu as pltpu
from jax.experimental.pallas import tpu_sc as plsc

F32 = jnp.float32
BF16 = jnp.bfloat16

D_MODEL = 1024
GRID_W = 64
ATT_HEADS = 8
ATT_KV_HEADS = 2
ATT_GROUP = ATT_HEADS // ATT_KV_HEADS
HEAD_DIM = 64
BLOCK = 128
ROPE_THETA = 10000.0
AXIS_FREQS = HEAD_DIM // 4
ML_HEADS = 4
ML_DK = 128
ML_DV = 256
ML_CHUNK = 512
N_EXPERTS = 32
TOP_K = 4
D_FF = 1024
SWIGLU_ALPHA = 1.702
SWIGLU_LIMIT = 7.0
EPS = 1e-6
NEG = -1e30

Q_W = ATT_HEADS * HEAD_DIM
KV_W = ATT_KV_HEADS * HEAD_DIM
MLQK_W = ML_HEADS * ML_DK
MLV_W = ML_HEADS * ML_DV
GATE_W = 4 * ML_HEADS

LANES = 128
VMEM_LIMIT = 56 * 1024 * 1024
MOE_ROWS = 512
SC_CORES = 2
SC_SUBCORES = 16
SC_GATHER_CHUNK = 32


def _cparams(sem):
    return pltpu.CompilerParams(dimension_semantics=sem, vmem_limit_bytes=VMEM_LIMIT)


def _const_spec(shape):
    nd = len(shape)
    return pl.BlockSpec(shape, lambda *_: (0,) * nd)


def _adaln_kernel(cond_ref, w_ref, b_ref, o_ref):
    cnd = cond_ref[...]
    act = cnd * jax.nn.sigmoid(cnd)
    o_ref[...] = jnp.dot(act.astype(BF16), w_ref[...].astype(BF16),
                         preferred_element_type=F32) + b_ref[...]


def _adaln(cond, w_ada, b_ada):
    rows, d = cond.shape
    n = w_ada.shape[1]
    tn = 1536
    return pl.pallas_call(
        _adaln_kernel,
        grid=(n // tn,),
        in_specs=[_const_spec((rows, d)),
                  pl.BlockSpec((d, tn), lambda j: (0, j)),
                  pl.BlockSpec((1, tn), lambda j: (0, j))],
        out_specs=pl.BlockSpec((rows, tn), lambda j: (0, j)),
        out_shape=jax.ShapeDtypeStruct((rows, n), F32),
        compiler_params=_cparams(("parallel",)),
        name="adaln",
    )(cond, w_ada, b_ada.reshape(1, n))


def _pack_w_in(w_in, b_gates, rope, with_nat):
    o_q, o_k, o_v = 0, Q_W, Q_W + KV_W
    o_mq = Q_W + 2 * KV_W
    o_mk = o_mq + MLQK_W
    o_mv = o_mk + MLQK_W
    o_g = o_mv + MLV_W
    o_mo = o_g + GATE_W
    o_ga = o_mo + MLV_W
    o_gm = o_ga + D_MODEL
    half = HEAD_DIM // 2

    def swap_halves(block, n_heads):
        return jnp.flip(block.reshape(D_MODEL, n_heads, 2, half), axis=2).reshape(D_MODEL, n_heads * HEAD_DIM)

    def pair_heads(block):
        return block.reshape(D_MODEL, ATT_KV_HEADS, ATT_GROUP, HEAD_DIM).transpose(0, 2, 1, 3).reshape(D_MODEL, Q_W)

    wq = w_in[:, o_q:o_q + Q_W]
    wk = w_in[:, o_k:o_k + KV_W]

    segs, cols, off = {}, [], 0

    def add(name, cl):
        nonlocal off
        width = sum(c.shape[1] for c in cl)
        segs[name] = (off, off + width)
        cols.extend(cl)
        off += width

    add("q", [pair_heads(wq)])
    if rope:
        add("qrot", [pair_heads(swap_halves(wq, ATT_HEADS))])
    add("kd", [wk])
    if rope:
        add("kdrot", [swap_halves(wk, ATT_KV_HEADS)])
    if with_nat:
        add("nat", [w_in[:, o_k:o_k + 2 * KV_W]])
    add("mq", [w_in[:, o_mq:o_mq + MLQK_W]])
    add("mv", [w_in[:, o_mv:o_mv + MLV_W]])
    add("mo", [w_in[:, o_mo:o_mo + MLV_W]])
    add("ga", [w_in[:, o_ga:o_ga + D_MODEL]])
    add("gm", [w_in[:, o_gm:o_gm + D_MODEL]])
    add("gates", [w_in[:, o_g:o_g + GATE_W], jnp.zeros((D_MODEL, LANES - GATE_W), w_in.dtype)])
    wp = jnp.concatenate(cols, axis=1).astype(BF16)
    wvd_t = jnp.concatenate([w_in[:, o_v:o_v + KV_W], w_in[:, o_mk:o_mk + MLQK_W]], axis=1).T.astype(BF16)
    bg = jnp.pad(b_gates.astype(F32), (0, LANES - GATE_W)).reshape(1, LANES)
    return wp, wvd_t, bg, segs


def _rope_tables(n_tok):
    rows = n_tok // GRID_W
    row = np.repeat(np.arange(rows), GRID_W).astype(np.float32)
    col = np.tile(np.arange(GRID_W), rows).astype(np.float32)
    inv = (np.float32(ROPE_THETA) ** (-np.arange(AXIS_FREQS, dtype=np.float32) / AXIS_FREQS)).astype(np.float32)
    ang = np.concatenate([row[:, None] * inv, col[:, None] * inv], axis=-1).astype(np.float32)
    cos, sin = np.cos(ang), np.sin(ang)
    c64 = np.concatenate([cos, cos], axis=-1)
    s64 = np.concatenate([-sin, sin], axis=-1)
    return (jnp.asarray(np.tile(c64, (1, 2)), F32), jnp.asarray(np.tile(s64, (1, 2)), F32))


def _log_sigmoid(x):
    return jnp.minimum(x, 0.0) - jnp.log1p(jnp.exp(-jnp.abs(x)))


def _prefix_sum_rows(x):
    row = lax.broadcasted_iota(jnp.int32, x.shape, 0)
    k = 1
    while k < x.shape[0]:
        x = x + jnp.where(row >= k, pltpu.roll(x, k, 0), 0.0)
        k *= 2
    return x


def _proj_kernel(*refs, segs, rope, with_nat):
    it = iter(refs)
    x_ref, mod_ref, g_ref, w_ref, wvt_ref, bg_ref = (next(it) for _ in range(6))
    cos_ref = sin_ref = None
    if rope:
        cos_ref, sin_ref = next(it), next(it)
    q_ref, kd_ref, vd_ref = next(it), next(it), next(it)
    nat_ref = next(it) if with_nat else None
    mq_ref, mk_ref, mv_ref, gates_ref, gates_t_ref, mo_ref, ga_ref, gm_ref = (next(it) for _ in range(8))

    x = x_ref[0]
    shift = mod_ref[0, 0:1, :]
    scale = mod_ref[0, 1:2, :]
    ms = jnp.mean(x * x, axis=-1, keepdims=True)
    xn = x * lax.rsqrt(ms + EPS) * g_ref[...]
    xb = (xn * (1.0 + scale) + shift).astype(BF16)

    def seg(name):
        lo, hi = segs[name]
        return jnp.dot(xb, w_ref[:, lo:hi], preferred_element_type=F32)

    uq = seg("q")
    ukd = seg("kd")
    if rope:
        uqr = seg("qrot")
        ukr = seg("kdrot")
        cs = cos_ref[...]
        sn = sin_ref[...]
        for h in range(ATT_HEADS // 2):
            sl = slice(h * LANES, (h + 1) * LANES)
            q_ref[0, h] = ((uq[:, sl] * cs + uqr[:, sl] * sn) * HEAD_DIM ** -0.5).astype(BF16)
        kd_ref[0] = (ukd * cs + ukr * sn).astype(BF16)
    else:
        for h in range(ATT_HEADS // 2):
            sl = slice(h * LANES, (h + 1) * LANES)
            q_ref[0, h] = (uq[:, sl] * HEAD_DIM ** -0.5).astype(BF16)
        kd_ref[0] = ukd.astype(BF16)
    ut = lax.dot_general(wvt_ref[...], xb, (((1,), (1,)), ((), ())), preferred_element_type=F32)
    vd_ref[0] = ut[:KV_W].astype(BF16)
    mk_ref[0] = ut[KV_W:].astype(BF16)
    if with_nat:
        nat_ref[0] = seg("nat")
    mq_ref[0] = (seg("mq") * ML_DK ** -0.5).astype(BF16)
    mv_ref[0] = seg("mv").astype(BF16)
    gates = seg("gates") + bg_ref[...]
    lf = _log_sigmoid(gates)
    csum = _prefix_sum_rows(lf)
    lane = lax.broadcasted_iota(jnp.int32, gates.shape, 1)
    cum = jnp.where(lane >= 2 * ML_HEADS, csum[-1:, :] - csum + lf, csum)
    gates_ref[0] = cum
    gates_t_ref[0, :GATE_W, :] = gates.T[:GATE_W, :]
    gates_t_ref[0, GATE_W:, :] = cum.T[:GATE_W, :]
    mo_ref[0] = _sigmoid(seg("mo")).astype(BF16)
    ga_ref[0] = _sigmoid(seg("ga")).astype(BF16)
    gm_ref[0] = _sigmoid(seg("gm")).astype(BF16)


def _project(x, mod, g_pre, w_in, b_gates, mod_row, rope, with_nat, tm, b0, bsz):
    _, seq, d = x.shape
    wp, wvd_t, bg, segs = _pack_w_in(w_in, b_gates, True, True)
    nw = wp.shape[1]
    tok3 = lambda b, i: (b, i, 0)
    in_specs = [pl.BlockSpec((1, tm, d), lambda b, i: (b + b0, i, 0)),
                pl.BlockSpec((1, 6, d), lambda b, i: (mod_row(b), 0, 0)),
                _const_spec((1, d)),
                pl.BlockSpec((d, nw), lambda b, i: (0, 0), pipeline_mode=pl.Buffered(1)),
                _const_spec((KV_W + MLQK_W, d)), _const_spec((1, LANES))]
    args = [x, mod, g_pre.reshape(1, d), wp, wvd_t, bg]
    if rope:
        cos_t, sin_t = _rope_tables(seq)
        in_specs += [pl.BlockSpec((tm, LANES), lambda b, i: (i, 0))] * 2
        args += [cos_t, sin_t]

    def tok_out(width, dtype):
        return (jax.ShapeDtypeStruct((bsz, seq, width), dtype), pl.BlockSpec((1, tm, width), tok3))

    outs = [(jax.ShapeDtypeStruct((bsz, ATT_HEADS // 2, seq, LANES), BF16),
             pl.BlockSpec((1, ATT_HEADS // 2, tm, LANES), lambda b, i: (b, 0, i, 0))),
            tok_out(KV_W, BF16),
            (jax.ShapeDtypeStruct((bsz, KV_W, seq), BF16),
             pl.BlockSpec((1, KV_W, tm), lambda b, i: (b, 0, i)))]
    if with_nat:
        outs.append(tok_out(2 * KV_W, F32))
    outs += [tok_out(MLQK_W, BF16),
             (jax.ShapeDtypeStruct((bsz, MLQK_W, seq), BF16),
              pl.BlockSpec((1, MLQK_W, tm), lambda b, i: (b, 0, i))),
             tok_out(MLV_W, BF16), tok_out(LANES, F32),
             (jax.ShapeDtypeStruct((bsz, 2 * GATE_W, seq), F32),
              pl.BlockSpec((1, 2 * GATE_W, tm), lambda b, i: (b, 0, i))),
             tok_out(MLV_W, BF16), tok_out(D_MODEL, BF16), tok_out(D_MODEL, BF16)]
    res = pl.pallas_call(
        functools.partial(_proj_kernel, segs=segs, rope=rope, with_nat=with_nat),
        grid=(bsz, seq // tm),
        in_specs=in_specs,
        out_specs=[o[1] for o in outs],
        out_shape=[o[0] for o in outs],
        compiler_params=_cparams(("parallel", "parallel")),
        name="proj_rope" if rope else "proj_ctx",
    )(*args)
    names = ["q", "kd", "vd"] + (["nat"] if with_nat else []) + ["mq", "mk", "mv", "gates", "gates_t", "mo", "ga", "gm"]
    return dict(zip(names, res))


def _attn_kernel(*refs, window, n_blocks, n_ctx):
    it = iter(refs)
    sink_ref, q_ref = next(it), next(it)
    if window:
        kp_ref, kc_ref, kn_ref, vp_ref, vc_ref, vn_ref = (next(it) for _ in range(6))
    ck_ref, cv_ref, o_ref = next(it), next(it), next(it)
    j = pl.program_id(1)
    n_win = 3 * BLOCK if window else 0
    cols = ATT_GROUP * BLOCK

    col_id = lax.broadcasted_iota(jnp.int32, (1, cols), 1)
    if window:
        kl = lax.broadcasted_iota(jnp.int32, (BLOCK, cols), 0)
        ql = lax.broadcasted_iota(jnp.int32, (BLOCK, cols), 1) & (BLOCK - 1)
        prev_ok = kl >= ql + jnp.where(j == 0, BLOCK, 0)
        next_ok = kl <= ql - jnp.where(j == n_blocks - 1, BLOCK, 0)
    lane_row = lax.broadcasted_iota(jnp.int32, (1, LANES), 1) < HEAD_DIM
    half_lo = jnp.where(lane_row, 1.0, 0.0).astype(BF16)
    half_hi = jnp.where(lane_row, 0.0, 1.0).astype(BF16)

    if window:
        keys = jnp.concatenate([kp_ref[0], kc_ref[0], kn_ref[0], ck_ref[0]], axis=0)
        vals_t = jnp.concatenate([vp_ref[0], vc_ref[0], vn_ref[0], cv_ref[0]], axis=1)
    else:
        keys = ck_ref[0]
        vals_t = cv_ref[0]

    for kvh in range(ATT_KV_HEADS):
        qs = jnp.concatenate([q_ref[0, g] * (half_lo if kvh == 0 else half_hi) for g in range(ATT_GROUP)], axis=0)
        s = lax.dot_general(keys, qs, (((1,), (1,)), ((), ())), preferred_element_type=F32)
        if window:
            s = jnp.concatenate([jnp.where(prev_ok, s[:BLOCK], NEG), s[BLOCK:2 * BLOCK],
                                 jnp.where(next_ok, s[2 * BLOCK:n_win], NEG), s[n_win:]], axis=0)
        snk = jnp.full((1, cols), sink_ref[kvh * ATT_GROUP + ATT_GROUP - 1], F32)
        for g in range(ATT_GROUP - 2, -1, -1):
            snk = jnp.where(col_id < (g + 1) * BLOCK, sink_ref[kvh * ATT_GROUP + g], snk)
        m = jnp.maximum(jnp.max(s, axis=0, keepdims=True), snk)
        p = jnp.exp(s - m)
        den = jnp.sum(p, axis=0, keepdims=True) + jnp.exp(snk - m)
        o_t = jnp.dot(vals_t, p.astype(BF16), preferred_element_type=F32) / den
        o_t = o_t[kvh * HEAD_DIM:(kvh + 1) * HEAD_DIM]
        for pair in range(ATT_GROUP // 2):
            two = jnp.concatenate([o_t[:, (2 * pair) * BLOCK:(2 * pair + 1) * BLOCK],
                                   o_t[:, (2 * pair + 1) * BLOCK:(2 * pair + 2) * BLOCK]], axis=0)
            col = (kvh * (ATT_GROUP // 2) + pair) * LANES
            o_ref[0, :, col:col + LANES] = two.T.astype(BF16)


def _attention(q, kd, vd_t, ckd, cvd_t, sink, window):
    bsz, _, seq, _ = q.shape
    nb = seq // BLOCK
    n_ctx = ckd.shape[1]
    in_specs = [pl.BlockSpec(memory_space=pltpu.SMEM),
                pl.BlockSpec((1, ATT_HEADS // 2, BLOCK, LANES), lambda b, j: (b, 0, j, 0))]
    args = [sink.astype(F32), q]
    if window:
        in_specs += [pl.BlockSpec((1, BLOCK, KV_W), lambda b, j: (b, jnp.maximum(j - 1, 0), 0)),
                     pl.BlockSpec((1, BLOCK, KV_W), lambda b, j: (b, j, 0)),
                     pl.BlockSpec((1, BLOCK, KV_W), lambda b, j: (b, jnp.minimum(j + 1, nb - 1), 0)),
                     pl.BlockSpec((1, KV_W, BLOCK), lambda b, j: (b, 0, jnp.maximum(j - 1, 0))),
                     pl.BlockSpec((1, KV_W, BLOCK), lambda b, j: (b, 0, j)),
                     pl.BlockSpec((1, KV_W, BLOCK), lambda b, j: (b, 0, jnp.minimum(j + 1, nb - 1)))]
        args += [kd, kd, kd, vd_t, vd_t, vd_t]
    in_specs += [pl.BlockSpec((1, n_ctx, KV_W), lambda b, j: (b, 0, 0)),
                 pl.BlockSpec((1, KV_W, n_ctx), lambda b, j: (b, 0, 0))]
    args += [ckd, cvd_t]
    return pl.pallas_call(
        functools.partial(_attn_kernel, window=window, n_blocks=nb, n_ctx=n_ctx),
        grid=(bsz, nb),
        in_specs=in_specs,
        out_specs=pl.BlockSpec((1, BLOCK, Q_W), lambda b, j: (b, j, 0)),
        out_shape=jax.ShapeDtypeStruct((bsz, seq, Q_W), BF16),
        compiler_params=_cparams(("parallel", "parallel")),
        name="attn_window" if window else "attn_ctx",
    )(*args)


def _flat_heads(t, transpose):
    b, l = t.shape[:2]
    out = t.reshape(b, l, KV_W).astype(BF16)
    return jnp.swapaxes(out, 1, 2) if transpose else out


def _mlstm_kernel(*refs, chunk_len, n_chunks, has_init, emit_state):
    it = iter(refs)
    qkvg = [[next(it) for _ in range(5)] for _ in range(2)]
    if has_init:
        c0_ref, n0_ref, m0_ref = next(it), next(it), next(it)
    h_refs = [next(it), next(it)]
    if emit_state:
        co_ref, no_ref, mo_ref = next(it), next(it), next(it)
    st_s, m_s = next(it), next(it)
    t = pl.program_id(1)
    L = chunk_len

    @pl.when(t == 0)
    def _():
        for d in range(2):
            for h in range(ML_HEADS):
                if has_init:
                    st_s[d, h, :, :ML_DV] = c0_ref[0, d, h].T
                    st_s[d, h, :, ML_DV:] = jnp.broadcast_to(n0_ref[0, d, h:h + 1, :], (LANES, ML_DK)).T
                    m_s[d, h] = m0_ref[0, d, h:h + 1, :]
                else:
                    st_s[d, h] = jnp.zeros((ML_DK, ML_DV + LANES), F32)
                    m_s[d, h] = jnp.zeros((1, LANES), F32)

    ri = lax.broadcasted_iota(jnp.int32, (L, L), 0)
    ci = lax.broadcasted_iota(jnp.int32, (L, L), 1)
    ones_blk = jnp.ones((L, LANES), BF16)

    for d in range(2):
        q_ref, kt_ref, v_ref, cum_ref, rows_ref = qkvg[d]
        tri = (ci <= ri) if d == 0 else (ci >= ri)
        bc = cum_ref[0]
        g_t = rows_ref[0, :GATE_W, :]
        br = rows_ref[0, GATE_W:, :]
        last = L - 1 if d == 0 else 0

        for h in range(ML_HEADS):
            icol_i = 2 * ML_HEADS * d + h
            fcol = icol_i + ML_HEADS
            bcol = bc[:, fcol:fcol + 1]
            brow = br[fcol:fcol + 1, :]
            irow = g_t[icol_i:icol_i + 1, :]
            b_last = bc[last:last + 1, fcol:fcol + 1]
            m_prev = m_s[d, h][:, 0:1]
            st_prev = st_s[d, h]
            qh = q_ref[0, :, h * ML_DK:(h + 1) * ML_DK]
            kt = kt_ref[0, h * ML_DK:(h + 1) * ML_DK, :]
            v_aug = jnp.concatenate([v_ref[0, :, h * ML_DV:(h + 1) * ML_DV], ones_blk], axis=1)

            rel = jnp.where(tri, irow - brow, NEG)
            inter = bcol + m_prev
            m_t = jnp.maximum(inter, bcol + jnp.max(rel, axis=-1, keepdims=True))
            w = jnp.exp(rel + (bcol - m_t))
            w_inter = jnp.exp(inter - m_t)
            s = jnp.dot(qh, kt, preferred_element_type=F32) * w
            lhs = jnp.concatenate([s.astype(BF16), (qh.astype(F32) * w_inter).astype(BF16)], axis=1)
            rhs = jnp.concatenate([v_aug, st_prev.astype(BF16)], axis=0)
            mix = jnp.dot(lhs, rhs, preferred_element_type=F32)
            inv = 1.0 / jnp.maximum(jnp.abs(mix[:, ML_DV:]), jnp.exp(-m_t))
            for c in range(ML_DV // LANES):
                lo = h * ML_DV + c * LANES
                h_refs[d][0, :, lo:lo + LANES] = mix[:, c * LANES:(c + 1) * LANES] * inv

            a = b_last - brow + irow
            m_new = jnp.maximum(b_last + m_prev, jnp.max(a, axis=-1, keepdims=True))
            wk = jnp.exp(a - m_new)
            decay = jnp.exp(b_last + m_prev - m_new)
            kw_t = (kt.astype(F32) * wk).astype(BF16)
            st_new = decay * st_prev + jnp.dot(kw_t, v_aug, preferred_element_type=F32)
            st_s[d, h] = st_new
            m_s[d, h] = jnp.broadcast_to(m_new, (1, LANES))
            if emit_state:
                @pl.when(t == n_chunks - 1)
                def _():
                    co_ref[0, d, h] = st_new[:, :ML_DV].T
                    no_ref[0, d, h:h + 1, :] = st_new[:, ML_DV:].T[0:1, :]
                    mo_ref[0, d, h:h + 1, :] = jnp.broadcast_to(m_new, (1, LANES))


def _mlstm(mq, mk_t, mv, gate_cum, gate_rows, init, emit_state, b0=0):
    bsz, seq, _ = mq.shape
    chunk_len = min(ML_CHUNK, seq)
    nc = seq // chunk_len
    fwd = lambda b, t: (b, t, 0)
    bwd = lambda b, t: (b, nc - 1 - t, 0)
    fwd_t = lambda b, t: (b, 0, t)
    bwd_t = lambda b, t: (b, 0, nc - 1 - t)
    state5 = lambda b, t: (b, 0, 0, 0, 0)
    state4 = lambda b, t: (b, 0, 0, 0)
    init5 = lambda b, t: (b + b0, 0, 0, 0, 0)
    init4 = lambda b, t: (b + b0, 0, 0, 0)
    in_specs, args = [], []
    for chunk, chunk_t in ((fwd, fwd_t), (bwd, bwd_t)):
        in_specs += [pl.BlockSpec((1, chunk_len, MLQK_W), chunk), pl.BlockSpec((1, MLQK_W, chunk_len), chunk_t),
                     pl.BlockSpec((1, chunk_len, MLV_W), chunk), pl.BlockSpec((1, chunk_len, LANES), chunk),
                     pl.BlockSpec((1, 2 * GATE_W, chunk_len), chunk_t)]
        args += [mq, mk_t, mv, gate_cum, gate_rows]
    if init is not None:
        c0, n0, m0 = init
        in_specs += [pl.BlockSpec((1, 2, ML_HEADS, ML_DV, ML_DK), init5),
                     pl.BlockSpec((1, 2, ML_HEADS, ML_DK), init4),
                     pl.BlockSpec((1, 2, ML_HEADS, LANES), init4)]
        args += [c0.astype(F32), n0.astype(F32),
                 jnp.broadcast_to(m0.astype(F32)[..., None], m0.shape + (LANES,))]
    out_shape = [jax.ShapeDtypeStruct((bsz, seq, MLV_W), F32)] * 2
    out_specs = [pl.BlockSpec((1, chunk_len, MLV_W), fwd), pl.BlockSpec((1, chunk_len, MLV_W), bwd)]
    if emit_state:
        out_shape += [jax.ShapeDtypeStruct((bsz, 2, ML_HEADS, ML_DV, ML_DK), F32),
                      jax.ShapeDtypeStruct((bsz, 2, ML_HEADS, ML_DK), F32),
                      jax.ShapeDtypeStruct((bsz, 2, ML_HEADS, LANES), F32)]
        out_specs += [pl.BlockSpec((1, 2, ML_HEADS, ML_DV, ML_DK), state5),
                      pl.BlockSpec((1, 2, ML_HEADS, ML_DK), state4),
                      pl.BlockSpec((1, 2, ML_HEADS, LANES), state4)]
    return pl.pallas_call(
        functools.partial(_mlstm_kernel, chunk_len=chunk_len, n_chunks=nc, has_init=init is not None, emit_state=emit_state),
        grid=(bsz, nc),
        in_specs=in_specs,
        out_specs=out_specs,
        out_shape=out_shape,
        scratch_shapes=[pltpu.VMEM((2, ML_HEADS, ML_DK, ML_DV + LANES), F32),
                        pltpu.VMEM((2, ML_HEADS, 1, LANES), F32)],
        compiler_params=_cparams(("parallel", "arbitrary")),
        name="mlstm_state" if emit_state else "mlstm",
    )(*args)


def _sigmoid(x):
    return 0.5 * jnp.tanh(0.5 * x) + 0.5


def _rms(x, g):
    return x * lax.rsqrt(jnp.mean(x * x, axis=-1, keepdims=True) + EPS) * g


def _pack_bf16_pairs(x):
    n = x.shape[1] // 2
    lo = pltpu.bitcast(x[:, :n].astype(BF16).astype(F32), jnp.int32)
    hi = pltpu.bitcast(x[:, n:].astype(BF16).astype(F32), jnp.int32)
    return lax.shift_right_logical(lo, 16) | hi


def _unpack_bf16_pairs(p):
    lo = pltpu.bitcast(lax.shift_left(p, 16), F32)
    hi = pltpu.bitcast(p & jnp.int32(-65536), F32)
    return lo, hi


def _merge_kernel(att_ref, hf_ref, hb_ref, mo_ref, ga_ref, gm_ref, x_ref, mod_ref, nml_ref, gpm_ref, gpf_ref,
                  wua_ref, wum_ref, wo_ref, wr_ref, br_ref, x1_ref, h2_ref, idx_ref, gate_ref, cnt_ref):
    hsum = hf_ref[0] + hb_ref[0]
    parts = []
    for h in range(ML_HEADS):
        hh = hsum[:, h * ML_DV:(h + 1) * ML_DV]
        parts.append(hh * lax.rsqrt(jnp.mean(hh * hh, axis=-1, keepdims=True) + EPS))
    hn = jnp.concatenate(parts, axis=-1) * nml_ref[...]
    ml = (hn * mo_ref[0].astype(F32)).astype(BF16)
    a = jnp.dot(att_ref[0], wua_ref[...], preferred_element_type=F32)
    m = jnp.dot(ml, wum_ref[...], preferred_element_type=F32)
    z = (ga_ref[0].astype(F32) * a + gm_ref[0].astype(F32) * m).astype(BF16)
    mix = jnp.dot(z, wo_ref[...], preferred_element_type=F32)
    gate1 = mod_ref[0, 2:3, :]
    shift2 = mod_ref[0, 3:4, :]
    scale2 = mod_ref[0, 4:5, :]
    x1 = x_ref[0] + gate1 * _rms(mix, gpm_ref[...])
    x1_ref[0] = x1
    h2 = _rms(x1, gpf_ref[...]) * (1.0 + scale2) + shift2
    h2_ref[0] = _pack_bf16_pairs(h2)

    h2_hi = h2.astype(BF16)
    h2_lo = (h2 - h2_hi.astype(F32)).astype(BF16)
    wr = wr_ref[...]
    wr_hi = wr.astype(BF16)
    wr_lo = (wr - wr_hi.astype(F32)).astype(BF16)
    hi_terms = jnp.dot(h2_hi, jnp.concatenate([wr_hi, wr_lo], axis=1), preferred_element_type=F32)
    logits = (hi_terms[:, :LANES] + hi_terms[:, LANES:]
              + jnp.dot(h2_lo, wr_hi, preferred_element_type=F32)) + br_ref[...]
    lane = lax.broadcasted_iota(jnp.int32, logits.shape, 1).astype(F32)
    work = jnp.where(lane < N_EXPERTS, logits, -jnp.inf)
    idx_out = jnp.zeros(logits.shape, F32)
    val_out = jnp.zeros(logits.shape, F32)
    picked = jnp.zeros(logits.shape, F32)
    top0 = None
    esum = None
    for k in range(TOP_K):
        mx = jnp.max(work, axis=-1, keepdims=True)
        sel = jnp.min(jnp.where(work == mx, lane, float(LANES)), axis=-1, keepdims=True)
        if k == 0:
            top0 = mx
        e = jnp.exp(mx - top0)
        esum = e if k == 0 else esum + e
        idx_out = jnp.where(lane == k, sel, idx_out)
        val_out = jnp.where(lane == k, e, val_out)
        picked = jnp.where(lane == sel, 1.0, picked)
        work = jnp.where(lane == sel, -jnp.inf, work)
    idx_ref[0] = idx_out.astype(jnp.int32)
    gate_ref[0] = val_out / esum
    cnt_ref[0] = jnp.sum(picked, axis=0, keepdims=True)


def _merge(att, hf, hb, pj, x, mod, mod_row, p, tm, b0):
    bsz = att.shape[0]
    _, seq, d = x.shape
    tok3 = lambda b, i: (b, i, 0)
    wr = jnp.pad(p["w_router"].astype(F32), ((0, 0), (0, LANES - N_EXPERTS)))
    br = jnp.pad(p["b_router"].astype(F32), (0, LANES - N_EXPERTS)).reshape(1, LANES)
    row = lambda v: v.astype(F32).reshape(1, -1)
    in_specs = [pl.BlockSpec((1, tm, Q_W), tok3),
                pl.BlockSpec((1, tm, MLV_W), tok3), pl.BlockSpec((1, tm, MLV_W), tok3),
                pl.BlockSpec((1, tm, MLV_W), tok3), pl.BlockSpec((1, tm, d), tok3), pl.BlockSpec((1, tm, d), tok3),
                pl.BlockSpec((1, tm, d), lambda b, i: (b + b0, i, 0)),
                pl.BlockSpec((1, 6, d), lambda b, i: (mod_row(b), 0, 0)),
                _const_spec((1, MLV_W)), _const_spec((1, d)), _const_spec((1, d)),
                _const_spec((Q_W, d)), _const_spec((MLV_W, d)), _const_spec((d, d)),
                _const_spec((d, LANES)), _const_spec((1, LANES))]
    out_shape = [jax.ShapeDtypeStruct((bsz, seq, d), F32), jax.ShapeDtypeStruct((bsz, seq, d // 2), jnp.int32),
                 jax.ShapeDtypeStruct((bsz, seq, LANES), jnp.int32), jax.ShapeDtypeStruct((bsz, seq, LANES), F32),
                 jax.ShapeDtypeStruct((bsz * (seq // tm), 1, LANES), F32)]
    out_specs = [pl.BlockSpec((1, tm, d), tok3), pl.BlockSpec((1, tm, d // 2), tok3),
                 pl.BlockSpec((1, tm, LANES), tok3), pl.BlockSpec((1, tm, LANES), tok3),
                 pl.BlockSpec((1, 1, LANES), lambda b, i: (b * (seq // tm) + i, 0, 0))]
    return pl.pallas_call(
        _merge_kernel,
        grid=(bsz, seq // tm),
        in_specs=in_specs,
        out_specs=out_specs,
        out_shape=out_shape,
        compiler_params=_cparams(("parallel", "parallel")),
        name="merge_router",
    )(att, hf, hb, pj["mo"], pj["ga"], pj["gm"], x, mod, row(p["norm_ml"]), row(p["g_post_mix"]),
      row(p["g_pre_ffn"]), p["w_up_att"].astype(BF16), p["w_up_ml"].astype(BF16), p["w_out"].astype(BF16), wr, br)


def _sc_gather_rows(table, idx):
    n = idx.shape[0]
    width = table.shape[1]
    chunk = SC_GATHER_CHUNK
    n_workers = SC_CORES * SC_SUBCORES
    rows_per_worker = n // n_workers
    n_pairs = rows_per_worker // (2 * chunk)
    assert n_pairs * 2 * chunk * n_workers == n
    mesh = plsc.VectorSubcoreMesh(core_axis_name="c", subcore_axis_name="s")

    def body(table_hbm, idx_hbm, out_hbm, idx_v, rows_v, gsem, osem):
        wid = lax.axis_index("s") * SC_CORES + lax.axis_index("c")
        base0 = wid * rows_per_worker

        def gather_copy(b):
            return pltpu.make_async_copy(table_hbm.at[idx_v.at[b]], rows_v.at[b], gsem.at[b])

        def write_copy(ci, b):
            base = pl.multiple_of(base0 + ci * chunk, 8)
            return pltpu.make_async_copy(rows_v.at[b], out_hbm.at[pl.ds(base, chunk)], osem.at[b])

        def issue(ci, b):
            base = pl.multiple_of(base0 + ci * chunk, 8)
            pltpu.sync_copy(idx_hbm.at[pl.ds(base, chunk)], idx_v.at[b])
            gather_copy(b).start()

        def finish(ci, b):
            gather_copy(b).wait()
            write_copy(ci, b).start()

        issue(0, 0)

        @pl.loop(0, n_pairs)
        def _(j):
            @pl.when(j > 0)
            def _():
                write_copy(2 * j - 1, 1).wait()
            issue(2 * j + 1, 1)
            finish(2 * j, 0)

            @pl.when(j < n_pairs - 1)
            def _():
                write_copy(2 * j, 0).wait()
                issue(2 * j + 2, 0)
            finish(2 * j + 1, 1)

        write_copy(2 * n_pairs - 2, 0).wait()
        write_copy(2 * n_pairs - 1, 1).wait()

    return pl.kernel(
        body, mesh=mesh,
        out_type=jax.ShapeDtypeStruct((n, width), table.dtype),
        scratch_types=[pltpu.VMEM((2, chunk), jnp.int32), pltpu.VMEM((2, chunk, width), table.dtype),
                       pltpu.SemaphoreType.DMA((2,)), pltpu.SemaphoreType.DMA((2,))],
    )(table, idx)


def _ffn_kernel(be_ref, nu_ref, xs_ref, w1_ref, b1_ref, w2_ref, b2_ref, ys_ref, *bf16_refs, weights_are_bf16):
    i = pl.program_id(0)
    e = be_ref[i]
    e_prev = be_ref[jnp.maximum(i - 1, 0)]
    w1_stage, w2_stage, sems = bf16_refs[-3:]
    slot = e % 2

    def weight_copies(expert, to_slot):
        return (pltpu.make_async_copy(w1_ref.at[expert], w1_stage.at[to_slot], sems.at[0, to_slot]),
                pltpu.make_async_copy(w2_ref.at[expert], w2_stage.at[to_slot], sems.at[1, to_slot]))

    @pl.when(i == 0)
    def _():
        for cp in weight_copies(e, slot):
            cp.start()

    @pl.when((i == 0) | (e != e_prev))
    def _():
        for cp in weight_copies(e, slot):
            cp.wait()
        if not weights_are_bf16:
            bf16_refs[0][0] = w1_stage[slot].astype(BF16)
            bf16_refs[1][0] = w2_stage[slot].astype(BF16)

        @pl.when(e + 1 < N_EXPERTS)
        def _():
            for cp in weight_copies(e + 1, 1 - slot):
                cp.start()

    if weights_are_bf16:
        read_w1, read_w2 = (lambda: w1_stage[slot]), (lambda: w2_stage[slot])
    else:
        read_w1, read_w2 = (lambda: bf16_refs[0][0]), (lambda: bf16_refs[1][0])

    @pl.when(i < nu_ref[0])
    def _():
        x_lo, x_hi = _unpack_bf16_pairs(xs_ref[...])
        xb = jnp.concatenate([x_lo.astype(BF16), x_hi.astype(BF16)], axis=1)
        hmid = jnp.dot(xb, read_w1(), preferred_element_type=F32) + b1_ref[0]
        glu = jnp.minimum(hmid[:, :D_FF], SWIGLU_LIMIT)
        lin = jnp.clip(hmid[:, D_FF:], -SWIGLU_LIMIT, SWIGLU_LIMIT)
        act = ((lin + 1.0) * glu * _sigmoid(SWIGLU_ALPHA * glu)).astype(BF16)
        y = jnp.dot(act, read_w2(), preferred_element_type=F32) + b2_ref[0]
        ys_ref[...] = _pack_bf16_pairs(y)

    @pl.when(i >= nu_ref[0])
    def _():
        ys_ref[...] = jnp.zeros(ys_ref.shape, jnp.int32)


def _sc_scatter_rows(tables, positions, n_rows):
    width = tables[0].shape[1]
    chunk = SC_GATHER_CHUNK
    n_workers = SC_CORES * SC_SUBCORES
    n_streams = len(tables)
    mesh = plsc.VectorSubcoreMesh(core_axis_name="c", subcore_axis_name="s")

    def body(*refs):
        table_refs = refs[0:2 * n_streams:2]
        pos_refs = refs[1:2 * n_streams:2]
        out_hbm, idx_v, rows_v, lsem, ssem = refs[2 * n_streams:]
        wid = lax.axis_index("s") * SC_CORES + lax.axis_index("c")

        for table_hbm, pos_hbm in zip(table_refs, pos_refs):
            tok_per_worker = table_hbm.shape[0] // n_workers
            n_pairs = tok_per_worker // (2 * chunk)
            assert n_pairs * 2 * chunk * n_workers == table_hbm.shape[0]
            base0 = wid * tok_per_worker

            def load_copy(ci, b):
                base = pl.multiple_of(base0 + ci * chunk, 8)
                return pltpu.make_async_copy(table_hbm.at[pl.ds(base, chunk)], rows_v.at[b], lsem.at[b])

            def scatter_copy(b, k):
                return pltpu.make_async_copy(rows_v.at[b], out_hbm.at[idx_v.at[b, k]], ssem.at[b])

            def load(ci, b):
                base = pl.multiple_of(base0 + ci * chunk, 8)
                load_copy(ci, b).start()
                for k in range(TOP_K):
                    pltpu.sync_copy(pos_hbm.at[k, pl.ds(base, chunk)], idx_v.at[b, k])

            def scatter(ci, b):
                load_copy(ci, b).wait()
                for k in range(TOP_K):
                    scatter_copy(b, k).start()

            def drain(b):
                for k in range(TOP_K):
                    scatter_copy(b, k).wait()

            load(0, 0)

            @pl.loop(0, n_pairs)
            def _(j):
                @pl.when(j > 0)
                def _():
                    drain(1)
                load(2 * j + 1, 1)
                scatter(2 * j, 0)

                @pl.when(j < n_pairs - 1)
                def _():
                    drain(0)
                    load(2 * j + 2, 0)
                scatter(2 * j + 1, 1)

            drain(0)
            drain(1)

    args = [a for pair in zip(tables, positions) for a in pair]
    return pl.kernel(
        body, mesh=mesh,
        out_type=jax.ShapeDtypeStruct((n_rows, width), tables[0].dtype),
        scratch_types=[pltpu.VMEM((2, TOP_K, chunk), jnp.int32), pltpu.VMEM((2, chunk, width), tables[0].dtype),
                       pltpu.SemaphoreType.DMA((2,)), pltpu.SemaphoreType.DMA((2,))],
    )(*args)


def _pos_kernel(idx_ref, base_ref, pos_ref):
    idx = idx_ref[...]
    rows = idx.shape[0]
    lane = lax.broadcasted_iota(jnp.int32, idx.shape, 1)
    hots = [jnp.where(lane == idx[:, k:k + 1], 1.0, 0.0) for k in range(TOP_K)]
    cnt = hots[0] + hots[1] + hots[2] + hots[3]
    ri = lax.broadcasted_iota(jnp.int32, (rows, rows), 0)
    ci = lax.broadcasted_iota(jnp.int32, (rows, rows), 1)
    earlier = jnp.where(ci < ri, 1.0, 0.0).astype(BF16)
    ahead = jnp.dot(earlier, cnt.astype(BF16), preferred_element_type=F32) + base_ref[0]
    posmat = jnp.zeros(idx.shape, F32)
    for k in range(TOP_K):
        posmat = jnp.where(lane == k, jnp.sum(hots[k] * ahead, axis=-1, keepdims=True), posmat)
    pos_ref[...] = posmat.T[:TOP_K, :].astype(jnp.int32)


def _route(idx_list, count_list, every_expert):
    tiles = [c.shape[0] for c in count_list]
    tile_tok = [idx.size // LANES // n for idx, n in zip(idx_list, tiles)]
    n_tok = sum(n * sz for n, sz in zip(tiles, tile_tok))
    n_blocks = n_tok * TOP_K // MOE_ROWS + N_EXPERTS
    cnt = jnp.concatenate([c.reshape(-1, LANES) for c in count_list], axis=0).astype(jnp.int32)
    tile_off = jnp.cumsum(cnt, axis=0) - cnt
    total = jnp.sum(cnt, axis=0)
    padded = (total + MOE_ROWS - 1) // MOE_ROWS * MOE_ROWS
    if every_expert:
        padded = jnp.where(jnp.arange(LANES) < N_EXPERTS, jnp.maximum(padded, MOE_ROWS), 0)
    pend = jnp.cumsum(padded)
    base = ((pend - padded)[None, :] + tile_off).astype(F32)
    starts = jnp.arange(n_blocks, dtype=jnp.int32) * MOE_ROWS
    block_e = jnp.minimum(jnp.sum(pend[None, :N_EXPERTS] <= starts[:, None], axis=1), N_EXPERTS - 1).astype(jnp.int32)
    n_used = (pend[N_EXPERTS - 1] // MOE_ROWS).astype(jnp.int32).reshape(1)
    positions, first = [], 0
    for idx, n_tiles, tile in zip(idx_list, tiles, tile_tok):
        stream_tok = n_tiles * tile
        positions.append(pl.pallas_call(
            _pos_kernel,
            grid=(n_tiles,),
            in_specs=[pl.BlockSpec((tile, LANES), lambda i: (i, 0)),
                      pl.BlockSpec((1, 1, LANES), lambda i: (i, 0, 0))],
            out_specs=pl.BlockSpec((TOP_K, tile), lambda i: (0, i)),
            out_shape=jax.ShapeDtypeStruct((TOP_K, stream_tok), jnp.int32),
            compiler_params=_cparams(("parallel",)),
            name="route_pos",
        )(idx.reshape(stream_tok, LANES), base[first:first + n_tiles].reshape(n_tiles, 1, LANES)))
        first += n_tiles
    return block_e, n_used, positions


def _moe(h2p_list, idx_list, count_list, w1, b1, w2, b2):
    half = h2p_list[0].shape[1]
    d = 2 * half
    cast_here = w1.dtype != BF16
    block_e, n_used, positions = _route(idx_list, count_list, every_expert=True)
    n_blocks = block_e.shape[0]
    xs = _sc_scatter_rows(h2p_list, positions, n_blocks * MOE_ROWS)
    w1_spec = pl.BlockSpec((1, d, 2 * D_FF), lambda i, be, nu: (be[i], 0, 0))
    w2_spec = pl.BlockSpec((1, D_FF, d), lambda i, be, nu: (be[i], 0, 0))
    hbm_spec = pl.BlockSpec(memory_space=pl.ANY)
    grid_spec = pltpu.PrefetchScalarGridSpec(
        num_scalar_prefetch=2,
        grid=(n_blocks,),
        in_specs=[pl.BlockSpec((MOE_ROWS, half), lambda i, be, nu: (i, 0)),
                  hbm_spec,
                  pl.BlockSpec((1, 1, 2 * D_FF), lambda i, be, nu: (be[i], 0, 0)),
                  hbm_spec,
                  pl.BlockSpec((1, 1, d), lambda i, be, nu: (be[i], 0, 0))],
        out_specs=[pl.BlockSpec((MOE_ROWS, half), lambda i, be, nu: (i, 0))] + ([w1_spec, w2_spec] if cast_here else []),
        scratch_shapes=[pltpu.VMEM((2, d, 2 * D_FF), w1.dtype), pltpu.VMEM((2, D_FF, d), w2.dtype),
                        pltpu.SemaphoreType.DMA((2, 2))],
    )
    out_shape = [jax.ShapeDtypeStruct((n_blocks * MOE_ROWS, half), jnp.int32)]
    if cast_here:
        out_shape += [jax.ShapeDtypeStruct(w1.shape, BF16), jax.ShapeDtypeStruct(w2.shape, BF16)]
    res = pl.pallas_call(
        functools.partial(_ffn_kernel, weights_are_bf16=not cast_here),
        grid_spec=grid_spec,
        out_shape=out_shape,
        compiler_params=_cparams(("arbitrary",)),
        name="moe_ffn_cast" if cast_here else "moe_ffn",
    )(block_e, n_used, xs, w1, b1.reshape(N_EXPERTS, 1, -1), w2, b2.reshape(N_EXPERTS, 1, -1))
    w1b, w2b = (res[1], res[2]) if cast_here else (w1, w2)
    return [_sc_gather_rows(res[0], pos.reshape(-1)) for pos in positions], w1b, w2b


def _final_kernel(x1_ref, y0_ref, y1_ref, y2_ref, y3_ref, gate_ref, mod_ref, g_ref, *rest):
    o_ref = rest[-1]
    gates = gate_ref[0]
    lo = hi = None
    for k, y_ref in enumerate((y0_ref, y1_ref, y2_ref, y3_ref)):
        y_lo, y_hi = _unpack_bf16_pairs(y_ref[0, 0])
        gk = gates[:, k:k + 1]
        lo = gk * y_lo if k == 0 else lo + gk * y_lo
        hi = gk * y_hi if k == 0 else hi + gk * y_hi
    y = jnp.concatenate([lo, hi], axis=-1)
    gate2 = mod_ref[0, 5:6, :]
    o_ref[0] = x1_ref[0] + gate2 * _rms(y, g_ref[...])


def _final(x1, yg, gate, mod, mod_row, g_post_ffn, tm, b0, n_batch, out_prev):
    bsz, seq, d = x1.shape
    tok3 = lambda b, i: (b, i, 0)
    yg = yg.reshape(TOP_K, bsz, seq, d // 2)
    slot_specs = [pl.BlockSpec((1, 1, tm, d // 2), functools.partial(lambda b, i, k: (k, b, i, 0), k=k))
                  for k in range(TOP_K)]
    in_specs = ([pl.BlockSpec((1, tm, d), tok3)] + slot_specs +
                [pl.BlockSpec((1, tm, LANES), tok3),
                 pl.BlockSpec((1, 6, d), lambda b, i: (mod_row(b), 0, 0)), _const_spec((1, d))])
    args = [x1, yg, yg, yg, yg, gate, mod, g_post_ffn.astype(F32).reshape(1, d)]
    aliases = {}
    if out_prev is not None:
        in_specs.append(pl.BlockSpec(memory_space=pl.ANY))
        args.append(out_prev)
        aliases = {len(args) - 1: 0}
    return pl.pallas_call(
        _final_kernel,
        grid=(bsz, seq // tm),
        in_specs=in_specs,
        out_specs=pl.BlockSpec((1, tm, d), lambda b, i: (b + b0, i, 0)),
        out_shape=jax.ShapeDtypeStruct((n_batch, seq, d), F32),
        input_output_aliases=aliases,
        compiler_params=_cparams(("parallel", "parallel")),
        name="final_residual",
    )(*args)


def _stream(x, mod, mod_row, p, ctx_kv, init_state, rope, tm, b0, bsz):
    _, seq, d = x.shape
    is_ctx = ctx_kv is None
    pj = _project(x, mod, p["g_pre_mix"], p["w_in"], p["b_gates"], mod_row, rope, is_ctx, tm, b0, bsz)
    if is_ctx:
        att = _attention(pj["q"], None, None, pj["kd"], pj["vd"], p["sink"], window=False)
    else:
        att = _attention(pj["q"], pj["kd"], pj["vd"], ctx_kv[0], ctx_kv[1], p["sink"], window=True)
    assert min(ML_CHUNK, seq) == tm
    ml = _mlstm(pj["mq"], pj["mk"], pj["mv"], pj["gates"], pj["gates_t"], init_state, emit_state=is_ctx, b0=b0)
    x1, h2, idx, gate, tile_counts = _merge(att, ml[0], ml[1], pj, x, mod, mod_row, p, tm, b0)
    moe_in = (h2.reshape(bsz * seq, d // 2), idx, tile_counts)
    return x1, gate, moe_in, pj, ml


def kernel(x_prompt, x_sample, c, cache_k, cache_v, state_C, state_n, state_m, c_ctx, w_ada, b_ada, g_pre_mix,
           w_in, b_gates, attn_sink, norm_mlstm, w_up_att, w_up_ml, w_out, g_post_mix, g_pre_ffn, w_router,
           b_router, w1, b1, w2, b2, g_post_ffn):
    depth = w_ada.shape[0]
    n_dec = c.shape[0]
    cond = jnp.concatenate([c_ctx[None, :], c], axis=0).astype(F32)
    cond = jnp.pad(cond, ((0, 16 - cond.shape[0]), (0, 0)))
    y_prompt, y_sample = x_prompt, x_sample
    ks_, vs_, cs_, ns_, ms_ = [], [], [], [], []
    for l in range(depth):
        p = dict(g_pre_mix=g_pre_mix[l], w_in=w_in[l], b_gates=b_gates[l], sink=attn_sink[l], norm_ml=norm_mlstm[l],
                 w_up_att=w_up_att[l], w_up_ml=w_up_ml[l], w_out=w_out[l], g_post_mix=g_post_mix[l],
                 g_pre_ffn=g_pre_ffn[l], w_router=w_router[l], b_router=b_router[l], w1=w1[l], b1=b1[l],
                 w2=w2[l], b2=b2[l], g_post_ffn=g_post_ffn[l])
        mod = _adaln(cond, w_ada[l], b_ada[l]).reshape(16, 6, D_MODEL)
        row_p = lambda b: 0
        bsz, seq = x_prompt.shape[:2]
        x1_p, gate_p, moe_p, pj, ml = _stream(y_prompt, mod, row_p, p, None, None, False, 256, 0, bsz)
        nat = pj["nat"]
        ks_.append(nat[..., :KV_W].reshape(bsz, seq, ATT_KV_HEADS, HEAD_DIM))
        vs_.append(nat[..., KV_W:].reshape(bsz, seq, ATT_KV_HEADS, HEAD_DIM))
        cs_.append(ml[2])
        ns_.append(ml[3])
        ms_.append(ml[4][..., 0])
        ctx_kv = (_flat_heads(cache_k[:, l], False), _flat_heads(cache_v[:, l], True))
        init = (state_C[:, l], state_n[:, l], state_m[:, l])
        half = n_dec // 2
        experts = (p["w1"], p["b1"].astype(F32), p["w2"], p["b2"].astype(F32))
        groups = []
        for b0 in (0, half):
            row_s = functools.partial(lambda b, off: b + off + 1, off=b0)
            sl = slice(b0, b0 + half)
            x1_s, gate_s, moe_s, _, _ = _stream(y_sample, mod, row_s, p, (ctx_kv[0][sl], ctx_kv[1][sl]),
                                                init, True, 512, b0, half)
            groups.append((x1_s, gate_s, moe_s, row_s))
        (x1_a, gate_a, moe_a, row_a), (x1_b, gate_b, moe_b, row_b) = groups
        (yg_p, yg_a), w1b, w2b = _moe([moe_p[0], moe_a[0]], [moe_p[1], moe_a[1]], [moe_p[2], moe_a[2]], *experts)
        (yg_b,), _, _ = _moe([moe_b[0]], [moe_b[1]], [moe_b[2]], w1b, experts[1], w2b, experts[3])
        y_prompt = _final(x1_p, yg_p, gate_p, mod, row_p, p["g_post_ffn"], 256, 0, bsz, None)
        y_sample = _final(x1_a, yg_a, gate_a, mod, row_a, p["g_post_ffn"], 512, 0, n_dec, None)
        y_sample = _final(x1_b, yg_b, gate_b, mod, row_b, p["g_post_ffn"], 512, half, n_dec, y_sample)
    return (y_prompt, y_sample, jnp.stack(ks_, axis=1), jnp.stack(vs_, axis=1), jnp.stack(cs_, axis=1),
            jnp.stack(ns_, axis=1), jnp.stack(ms_, axis=1))
```
